```python
import math
import jax, jax.numpy as jnp
from jax import lax
import numpy as np

D_MODEL = 1024
BATCH = 8
SEQ = 4096
DEPTH = 2

N_Q_HEADS = 16
N_KV_HEADS = 2
HEAD_DIM = 64
Q_PER_KV = N_Q_HEADS // N_KV_HEADS
WINDOW = 128
ATTN_BLOCK = 128
ROPE_THETA = 500000.0
ROT_DIM = HEAD_DIM // 4
SGU_WIDTH = 1024
SGU_GROUPS = 8
SGU_GROUP_DIM = SGU_WIDTH // SGU_GROUPS
SGU_CHUNK = 128
FFN_DIM = 2816
CONV_WIDTH = 3
NORM_EPS = 1e-6

Q_END = N_Q_HEADS * HEAD_DIM
K_END = Q_END + N_KV_HEADS * HEAD_DIM
V_END = K_END + N_KV_HEADS * HEAD_DIM
Z_END = V_END + 2 * SGU_WIDTH
IN_COLS = Z_END + 2 * D_MODEL

kernel_name = "hybrid_gmlp_swa_sink_convffn_adaln"


def rms_norm(x, w):
    xf = x.astype(jnp.float32)
    y = xf * lax.rsqrt(jnp.mean(xf * xf, axis=-1, keepdims=True) + NORM_EPS)
    return (y * w.astype(jnp.float32)).astype(x.dtype)


def layer_norm(x, w, b):
    xf = x.astype(jnp.float32)
    mu = jnp.mean(xf, axis=-1, keepdims=True)
    var = jnp.mean(jnp.square(xf - mu), axis=-1, keepdims=True)
    y = (xf - mu) * lax.rsqrt(var + NORM_EPS)
    return (y * w.astype(jnp.float32) + b.astype(jnp.float32)).astype(x.dtype)


def rope_tables(positions, dtype):
    inv_freq = ROPE_THETA ** (-jnp.arange(0, ROT_DIM, 2, dtype=jnp.float32) / ROT_DIM)
    ang = positions.astype(jnp.float32)[..., None] * inv_freq
    return jnp.cos(ang)[:, :, None, :].astype(dtype), jnp.sin(ang)[:, :, None, :].astype(dtype)


def apply_partial_rope(x, cos, sin):
    half = ROT_DIM // 2
    x1, x2, xp = x[..., :half], x[..., half:ROT_DIM], x[..., ROT_DIM:]
    return jnp.concatenate([x1 * cos - x2 * sin, x2 * cos + x1 * sin, xp], axis=-1)


def sliding_window_attention(q, k, v, sinks):
    B, S = q.shape[0], q.shape[1]
    nb = S // ATTN_BLOCK
    qb = q.reshape(B, nb, ATTN_BLOCK, N_KV_HEADS, Q_PER_KV, HEAD_DIM)
    kb = k.reshape(B, nb, ATTN_BLOCK, N_KV_HEADS, HEAD_DIM)
    vb = v.reshape(B, nb, ATTN_BLOCK, N_KV_HEADS, HEAD_DIM)
    pad = ((0, 0), (1, 0), (0, 0), (0, 0), (0, 0))
    k_band = jnp.concatenate([jnp.pad(kb[:, :-1], pad), kb], axis=2)
    v_band = jnp.concatenate([jnp.pad(vb[:, :-1], pad), vb], axis=2)
    scores = jnp.einsum('bnqhgd,bnkhd->bnhgqk', qb, k_band).astype(jnp.float32) * (HEAD_DIM ** -0.5)
    i = jnp.arange(ATTN_BLOCK)[:, None]
    j = jnp.arange(2 * ATTN_BLOCK)[None, :]
    band = (j > i + ATTN_BLOCK - WINDOW) & (j <= i + ATTN_BLOCK)
    exists = (jnp.arange(nb)[:, None, None] > 0) | (j >= ATTN_BLOCK)[None]
    mask = (band[None] & exists)[None, :, None, None]
    scores = jnp.where(mask, scores, -jnp.inf)
    sink = sinks.astype(jnp.float32).reshape(N_KV_HEADS, Q_PER_KV)[None, None, :, :, None, None]
    m = jnp.maximum(jnp.max(scores, axis=-1, keepdims=True), sink)
    p = jnp.exp(scores - m)
    denom = jnp.sum(p, axis=-1, keepdims=True) + jnp.exp(sink - m)
    out = jnp.einsum('bnhgqk,bnkhd->bnqhgd', (p / denom).astype(v.dtype), v_band)
    return out.reshape(B, S, N_Q_HEADS * HEAD_DIM)


def spatial_gating(z, ln_w, ln_b, w_s, b_s):
    B, S = z.shape[0], z.shape[1]
    u, v = jnp.split(z, 2, axis=-1)
    v = layer_norm(v, ln_w, ln_b)
    vb = v.reshape(B, S // SGU_CHUNK, SGU_CHUNK, SGU_GROUPS, SGU_GROUP_DIM)
    causal = jnp.tril(jnp.ones((SGU_CHUNK, SGU_CHUNK), dtype=bool))
    w = jnp.where(causal[None], w_s, jnp.zeros_like(w_s))
    f = jnp.einsum('gts,bnsgc->bntgc', w, vb) + b_s.T[None, None, :, :, None]
    return u * f.reshape(B, S, SGU_WIDTH)


def conv_ffn(h, w_gate, w_up, conv_w, conv_b, w_down):
    a = h @ w_gate
    S = a.shape[1]
    a_pad = jnp.pad(a, ((0, 0), (CONV_WIDTH - 1, 0), (0, 0)))
    a = conv_b + sum(conv_w[k] * a_pad[:, k:k + S] for k in range(CONV_WIDTH))
    return (jax.nn.silu(a) * (h @ w_up)) @ w_down


def _fwd_setup_inputs(seed: int = 0) -> dict:
    key = jax.random.key(seed)
    ks = jax.random.split(key, 24)
    L, D = DEPTH, D_MODEL
    nrm = lambda k, shape, s: jax.random.normal(k, shape, jnp.float32) * s
    offsets = jax.random.randint(ks[2], (BATCH, 1), 0, 2048, dtype=jnp.int32)
    return {
        "x": nrm(ks[0], (BATCH, SEQ, D), 1.0),
        "c": nrm(ks[1], (BATCH, D), 1.0),
        "positions": offsets + jnp.arange(SEQ, dtype=jnp.int32)[None, :],
        "ada_w": nrm(ks[3], (L, D, 6 * D), D ** -0.5),
        "ada_b": nrm(ks[4], (L, 6 * D), 0.02),
        "norm1_w": 1.0 + nrm(ks[5], (L, D), 0.05),
        "w_in": nrm(ks[6], (L, D, IN_COLS), D ** -0.5),
        "attn_sinks": nrm(ks[7], (L, N_Q_HEADS), 0.5),
        "sgu_ln_w": 1.0 + nrm(ks[8], (L, SGU_WIDTH), 0.05),
        "sgu_ln_b": nrm(ks[9], (L, SGU_WIDTH), 0.02),
        "sgu_w": nrm(ks[10], (L, SGU_GROUPS, SGU_CHUNK, SGU_CHUNK), SGU_CHUNK ** -0.5),
        "sgu_b": 1.0 + nrm(ks[11], (L, SGU_GROUPS, SGU_CHUNK), 0.05),
        "proj_a": nrm(ks[12], (L, SGU_WIDTH, D), SGU_WIDTH ** -0.5),
        "proj_b": nrm(ks[13], (L, N_Q_HEADS * HEAD_DIM, D), (N_Q_HEADS * HEAD_DIM) ** -0.5),
        "w_out": nrm(ks[14], (L, D, D), D ** -0.5),
        "norm2_w": 1.0 + nrm(ks[15], (L, D), 0.05),
        "ffn_w_gate": nrm(ks[16], (L, D, FFN_DIM), D ** -0.5),
        "ffn_w_up": nrm(ks[17], (L, D, FFN_DIM), D ** -0.5),
        "ffn_conv_w": nrm(ks[18], (L, CONV_WIDTH, FFN_DIM), CONV_WIDTH ** -0.5),
        "ffn_conv_b": nrm(ks[19], (L, FFN_DIM), 0.01),
        "ffn_w_down": nrm(ks[20], (L, FFN_DIM, D), FFN_DIM ** -0.5),
        "final_norm_w": 1.0 + nrm(ks[21], (D,), 0.05),
    }


def _fwd_reference(x, c, positions, ada_w, ada_b, norm1_w, w_in, attn_sinks, sgu_ln_w, sgu_ln_b, sgu_w, sgu_b,
              proj_a, proj_b, w_out, norm2_w, ffn_w_gate, ffn_w_up, ffn_conv_w, ffn_conv_b, ffn_w_down,
              final_norm_w):
    B, S = x.shape[0], x.shape[1]
    cos, sin = rope_tables(positions, x.dtype)
    c_act = jax.nn.silu(c)
    for l in range(DEPTH):
        mod = (c_act @ ada_w[l] + ada_b[l])[:, None, :]
        sh1, sc1, g1, sh2, sc2, g2 = jnp.split(mod, 6, axis=-1)
        h = rms_norm(x, norm1_w[l]) * (1 + sc1) + sh1
        proj = h @ w_in[l]
        q, k, v, z, gates = jnp.split(proj, [Q_END, K_END, V_END, Z_END], axis=-1)
        q = apply_partial_rope(q.reshape(B, S, N_Q_HEADS, HEAD_DIM), cos, sin)
        k = apply_partial_rope(k.reshape(B, S, N_KV_HEADS, HEAD_DIM), cos, sin)
        v = v.reshape(B, S, N_KV_HEADS, HEAD_DIM)
        y_attn = sliding_window_attention(q, k, v, attn_sinks[l])
        y_sgu = spatial_gating(jax.nn.gelu(z, approximate=False), sgu_ln_w[l], sgu_ln_b[l], sgu_w[l], sgu_b[l])
        gate_a, gate_b = jnp.split(jax.nn.sigmoid(gates), 2, axis=-1)
        merged = gate_a * (y_sgu @ proj_a[l]) + gate_b * (y_attn @ proj_b[l])
        x = x + g1 * (merged @ w_out[l])
        h2 = rms_norm(x, norm2_w[l]) * (1 + sc2) + sh2
        x = x + g2 * conv_ffn(h2, ffn_w_gate[l], ffn_w_up[l], ffn_conv_w[l], ffn_conv_b[l], ffn_w_down[l])
    return rms_norm(x, final_norm_w)


import jax as _jax
import jax.numpy as _jnp

TWIN_FORMAT = 'train_step'
FWD_PARAMS = ['x', 'c', 'positions', 'ada_w', 'ada_b', 'norm1_w', 'w_in', 'attn_sinks', 'sgu_ln_w', 'sgu_ln_b', 'sgu_w', 'sgu_b', 'proj_a', 'proj_b', 'w_out', 'norm2_w', 'ffn_w_gate', 'ffn_w_up', 'ffn_conv_w', 'ffn_conv_b', 'ffn_w_down', 'final_norm_w']
TWIN_WEIGHTS = ['ada_w', 'ada_b', 'norm1_w', 'w_in', 'attn_sinks', 'sgu_ln_w', 'sgu_ln_b', 'sgu_w', 'sgu_b', 'proj_a', 'proj_b', 'w_out', 'norm2_w', 'ffn_w_gate', 'ffn_w_up', 'ffn_conv_w', 'ffn_conv_b', 'ffn_w_down', 'final_norm_w']
TWIN_DIFF_INPUT = 'x'
TWIN_INPUTS = ['x', 'c', 'positions', 'ada_w', 'ada_b', 'norm1_w', 'w_in', 'attn_sinks', 'sgu_ln_w', 'sgu_ln_b', 'sgu_w', 'sgu_b', 'proj_a', 'proj_b', 'w_out', 'norm2_w', 'ffn_w_gate', 'ffn_w_up', 'ffn_conv_w', 'ffn_conv_b', 'ffn_w_down', 'final_norm_w', 'loss_target', 'm_ada_w', 'm_ada_b', 'm_norm1_w', 'm_w_in', 'm_attn_sinks', 'm_sgu_ln_w', 'm_sgu_ln_b', 'm_sgu_w', 'm_sgu_b', 'm_proj_a', 'm_proj_b', 'm_w_out', 'm_norm2_w', 'm_ffn_w_gate', 'm_ffn_w_up', 'm_ffn_conv_w', 'm_ffn_conv_b', 'm_ffn_w_down', 'm_final_norm_w', 'v_ada_w', 'v_ada_b', 'v_norm1_w', 'v_w_in', 'v_attn_sinks', 'v_sgu_ln_w', 'v_sgu_ln_b', 'v_sgu_w', 'v_sgu_b', 'v_proj_a', 'v_proj_b', 'v_w_out', 'v_norm2_w', 'v_ffn_w_gate', 'v_ffn_w_up', 'v_ffn_conv_w', 'v_ffn_conv_b', 'v_ffn_w_down', 'v_final_norm_w']
TWIN_OUTPUTS = ['loss', 'grad_x', 'grad_ada_w', 'grad_ada_b', 'grad_norm1_w', 'grad_w_in', 'grad_attn_sinks', 'grad_sgu_ln_w', 'grad_sgu_ln_b', 'grad_sgu_w', 'grad_sgu_b', 'grad_proj_a', 'grad_proj_b', 'grad_w_out', 'grad_norm2_w', 'grad_ffn_w_gate', 'grad_ffn_w_up', 'grad_ffn_conv_w', 'grad_ffn_conv_b', 'grad_ffn_w_down', 'grad_final_norm_w', 'delta_ada_w', 'delta_ada_b', 'delta_norm1_w', 'delta_w_in', 'delta_attn_sinks', 'delta_sgu_ln_w', 'delta_sgu_ln_b', 'delta_sgu_w', 'delta_sgu_b', 'delta_proj_a', 'delta_proj_b', 'delta_w_out', 'delta_norm2_w', 'delta_ffn_w_gate', 'delta_ffn_w_up', 'delta_ffn_conv_w', 'delta_ffn_conv_b', 'delta_ffn_w_down', 'delta_final_norm_w', 'new_m_ada_w', 'new_m_ada_b', 'new_m_norm1_w', 'new_m_w_in', 'new_m_attn_sinks', 'new_m_sgu_ln_w', 'new_m_sgu_ln_b', 'new_m_sgu_w', 'new_m_sgu_b', 'new_m_proj_a', 'new_m_proj_b', 'new_m_w_out', 'new_m_norm2_w', 'new_m_ffn_w_gate', 'new_m_ffn_w_up', 'new_m_ffn_conv_w', 'new_m_ffn_conv_b', 'new_m_ffn_w_down', 'new_m_final_norm_w', 'new_v_ada_w', 'new_v_ada_b', 'new_v_norm1_w', 'new_v_w_in', 'new_v_attn_sinks', 'new_v_sgu_ln_w', 'new_v_sgu_ln_b', 'new_v_sgu_w', 'new_v_sgu_b', 'new_v_proj_a', 'new_v_proj_b', 'new_v_w_out', 'new_v_norm2_w', 'new_v_ffn_w_gate', 'new_v_ffn_w_up', 'new_v_ffn_conv_w', 'new_v_ffn_conv_b', 'new_v_ffn_w_down', 'new_v_final_norm_w']
TWIN_LEAF_KINDS = {'loss': 'loss', 'grad_x': 'grad_x', 'grad_ada_w': 'grad_w', 'grad_ada_b': 'grad_w', 'grad_norm1_w': 'grad_w', 'grad_w_in': 'grad_w', 'grad_attn_sinks': 'grad_w', 'grad_sgu_ln_w': 'grad_w', 'grad_sgu_ln_b': 'grad_w', 'grad_sgu_w': 'grad_w', 'grad_sgu_b': 'grad_w', 'grad_proj_a': 'grad_w', 'grad_proj_b': 'grad_w', 'grad_w_out': 'grad_w', 'grad_norm2_w': 'grad_w', 'grad_ffn_w_gate': 'grad_w', 'grad_ffn_w_up': 'grad_w', 'grad_ffn_conv_w': 'grad_w', 'grad_ffn_conv_b': 'grad_w', 'grad_ffn_w_down': 'grad_w', 'grad_final_norm_w': 'grad_w', 'delta_ada_w': 'delta_w', 'delta_ada_b': 'delta_w', 'delta_norm1_w': 'delta_w', 'delta_w_in': 'delta_w', 'delta_attn_sinks': 'delta_w', 'delta_sgu_ln_w': 'delta_w', 'delta_sgu_ln_b': 'delta_w', 'delta_sgu_w': 'delta_w', 'delta_sgu_b': 'delta_w', 'delta_proj_a': 'delta_w', 'delta_proj_b': 'delta_w', 'delta_w_out': 'delta_w', 'delta_norm2_w': 'delta_w', 'delta_ffn_w_gate': 'delta_w', 'delta_ffn_w_up': 'delta_w', 'delta_ffn_conv_w': 'delta_w', 'delta_ffn_conv_b': 'delta_w', 'delta_ffn_w_down': 'delta_w', 'delta_final_norm_w': 'delta_w', 'new_m_ada_w': 'new_m', 'new_m_ada_b': 'new_m', 'new_m_norm1_w': 'new_m', 'new_m_w_in': 'new_m', 'new_m_attn_sinks': 'new_m', 'new_m_sgu_ln_w': 'new_m', 'new_m_sgu_ln_b': 'new_m', 'new_m_sgu_w': 'new_m', 'new_m_sgu_b': 'new_m', 'new_m_proj_a': 'new_m', 'new_m_proj_b': 'new_m', 'new_m_w_out': 'new_m', 'new_m_norm2_w': 'new_m', 'new_m_ffn_w_gate': 'new_m', 'new_m_ffn_w_up': 'new_m', 'new_m_ffn_conv_w': 'new_m', 'new_m_ffn_conv_b': 'new_m', 'new_m_ffn_w_down': 'new_m', 'new_m_final_norm_w': 'new_m', 'new_v_ada_w': 'new_v', 'new_v_ada_b': 'new_v', 'new_v_norm1_w': 'new_v', 'new_v_w_in': 'new_v', 'new_v_attn_sinks': 'new_v', 'new_v_sgu_ln_w': 'new_v', 'new_v_sgu_ln_b': 'new_v', 'new_v_sgu_w': 'new_v', 'new_v_sgu_b': 'new_v', 'new_v_proj_a': 'new_v', 'new_v_proj_b': 'new_v', 'new_v_w_out': 'new_v', 'new_v_norm2_w': 'new_v', 'new_v_ffn_w_gate': 'new_v', 'new_v_ffn_w_up': 'new_v', 'new_v_ffn_conv_w': 'new_v', 'new_v_ffn_conv_b': 'new_v', 'new_v_ffn_w_down': 'new_v', 'new_v_final_norm_w': 'new_v'}


def _forward(args):
    return _fwd_reference(*[args[k] for k in FWD_PARAMS])


def _output_shape():
    def fwd():
        inp = _fwd_setup_inputs(0)
        return _fwd_reference(*[inp[k] for k in FWD_PARAMS])
    out = _jax.eval_shape(fwd)
    return out.shape, out.dtype

N_MICROBATCH = 1
ADAM_LR = 0.001
ADAM_B1 = 0.9
ADAM_B2 = 0.999
ADAM_EPS = 1e-08
ADAM_WD = 0.01
ADAM_STEP = 10
PER_EXAMPLE_BATCH_AXIS = {'x': 0, 'c': 0, 'positions': 0, 'loss_target': 0}
SHARED_INPUTS = []
_WEIGHT_DTYPES = {'ada_w': _jnp.float32, 'ada_b': _jnp.float32, 'norm1_w': _jnp.float32, 'w_in': _jnp.float32, 'attn_sinks': _jnp.float32, 'sgu_ln_w': _jnp.float32, 'sgu_ln_b': _jnp.float32, 'sgu_w': _jnp.float32, 'sgu_b': _jnp.float32, 'proj_a': _jnp.float32, 'proj_b': _jnp.float32, 'w_out': _jnp.float32, 'norm2_w': _jnp.float32, 'ffn_w_gate': _jnp.float32, 'ffn_w_up': _jnp.float32, 'ffn_conv_w': _jnp.float32, 'ffn_conv_b': _jnp.float32, 'ffn_w_down': _jnp.float32, 'final_norm_w': _jnp.float32}
MOMENT_SCALE = {'ada_w': 1.097025e-01, 'ada_b': 2.151141e-01, 'norm1_w': 8.213391e-02, 'w_in': 4.783256e-02, 'attn_sinks': 1.592608e-02, 'sgu_ln_w': 3.524244e-02, 'sgu_ln_b': 3.512415e-02, 'sgu_w': 3.520654e-02, 'sgu_b': 4.852616e-02, 'proj_a': 7.351944e-02, 'proj_b': 6.280586e-02, 'w_out': 9.743609e-02, 'norm2_w': 1.171558e-01, 'ffn_w_gate': 6.001911e-02, 'ffn_w_up': 6.129904e-02, 'ffn_conv_w': 6.312563e-02, 'ffn_conv_b': 5.199918e-02, 'ffn_w_down': 1.022041e-01, 'final_norm_w': 3.258680e+01}


def _to_microbatches(a, axis):
    t = _jnp.moveaxis(a, axis, 0)
    t = t.reshape((N_MICROBATCH, t.shape[0] // N_MICROBATCH) + t.shape[1:])
    return _jnp.moveaxis(t, 1, axis + 1)


def setup_inputs(seed: int = 0) -> dict:
    inp = _fwd_setup_inputs(seed)
    key = _jax.random.fold_in(_jax.random.key(seed), 7919)
    shape, _ = _output_shape()
    out = dict(inp)
    out["loss_target"] = _jax.random.normal(_jax.random.fold_in(key, 0), shape, _jnp.float32)
    for i, name in enumerate(TWIN_WEIGHTS):
        w = inp[name].astype(_jnp.float32)
        if MOMENT_SCALE is None:
            s = _jnp.sqrt(_jnp.mean(_jnp.square(w)) + 1e-30)
        else:
            s = MOMENT_SCALE[name]
        km, kv = _jax.random.split(_jax.random.fold_in(key, i + 1))
        out[name] = w
        out["m_" + name] = s * _jax.random.normal(km, w.shape, _jnp.float32)
        out["v_" + name] = (s * s) * _jax.random.uniform(kv, w.shape, _jnp.float32, 0.5, 1.5)
    if N_MICROBATCH > 1:
        for name, axis in PER_EXAMPLE_BATCH_AXIS.items():
            out[name] = _to_microbatches(out[name], axis)
    return {'x': out['x'], 'c': out['c'], 'positions': out['positions'], 'ada_w': out['ada_w'], 'ada_b': out['ada_b'], 'norm1_w': out['norm1_w'], 'w_in': out['w_in'], 'attn_sinks': out['attn_sinks'], 'sgu_ln_w': out['sgu_ln_w'], 'sgu_ln_b': out['sgu_ln_b'], 'sgu_w': out['sgu_w'], 'sgu_b': out['sgu_b'], 'proj_a': out['proj_a'], 'proj_b': out['proj_b'], 'w_out': out['w_out'], 'norm2_w': out['norm2_w'], 'ffn_w_gate': out['ffn_w_gate'], 'ffn_w_up': out['ffn_w_up'], 'ffn_conv_w': out['ffn_conv_w'], 'ffn_conv_b': out['ffn_conv_b'], 'ffn_w_down': out['ffn_w_down'], 'final_norm_w': out['final_norm_w'], 'loss_target': out['loss_target'], 'm_ada_w': out['m_ada_w'], 'm_ada_b': out['m_ada_b'], 'm_norm1_w': out['m_norm1_w'], 'm_w_in': out['m_w_in'], 'm_attn_sinks': out['m_attn_sinks'], 'm_sgu_ln_w': out['m_sgu_ln_w'], 'm_sgu_ln_b': out['m_sgu_ln_b'], 'm_sgu_w': out['m_sgu_w'], 'm_sgu_b': out['m_sgu_b'], 'm_proj_a': out['m_proj_a'], 'm_proj_b': out['m_proj_b'], 'm_w_out': out['m_w_out'], 'm_norm2_w': out['m_norm2_w'], 'm_ffn_w_gate': out['m_ffn_w_gate'], 'm_ffn_w_up': out['m_ffn_w_up'], 'm_ffn_conv_w': out['m_ffn_conv_w'], 'm_ffn_conv_b': out['m_ffn_conv_b'], 'm_ffn_w_down': out['m_ffn_w_down'], 'm_final_norm_w': out['m_final_norm_w'], 'v_ada_w': out['v_ada_w'], 'v_ada_b': out['v_ada_b'], 'v_norm1_w': out['v_norm1_w'], 'v_w_in': out['v_w_in'], 'v_attn_sinks': out['v_attn_sinks'], 'v_sgu_ln_w': out['v_sgu_ln_w'], 'v_sgu_ln_b': out['v_sgu_ln_b'], 'v_sgu_w': out['v_sgu_w'], 'v_sgu_b': out['v_sgu_b'], 'v_proj_a': out['v_proj_a'], 'v_proj_b': out['v_proj_b'], 'v_w_out': out['v_w_out'], 'v_norm2_w': out['v_norm2_w'], 'v_ffn_w_gate': out['v_ffn_w_gate'], 'v_ffn_w_up': out['v_ffn_w_up'], 'v_ffn_conv_w': out['v_ffn_conv_w'], 'v_ffn_conv_b': out['v_ffn_conv_b'], 'v_ffn_w_down': out['v_ffn_w_down'], 'v_final_norm_w': out['v_final_norm_w']}


def _loss(weights, diff, rest, loss_target):
    with _jax.named_scope("forward"):
        args = {**rest, TWIN_DIFF_INPUT: diff, **{k: w.astype(_WEIGHT_DTYPES[k]) for k, w in weights.items()}}
        y = _forward(args)
    with _jax.named_scope("loss_head"):
        err = _jnp.square(y.astype(_jnp.float32) - loss_target)
        return 0.5 * _jnp.sum(_jnp.mean(err, axis=-1)) if err.ndim else 0.5 * err


def _adamw(w, g, m, v):
    m = ADAM_B1 * m + (1.0 - ADAM_B1) * g
    v = ADAM_B2 * v + (1.0 - ADAM_B2) * _jnp.square(g)
    m_hat = m / (1.0 - ADAM_B1 ** ADAM_STEP)
    v_hat = v / (1.0 - ADAM_B2 ** ADAM_STEP)
    delta = -ADAM_LR * (m_hat / (_jnp.sqrt(v_hat) + ADAM_EPS) + ADAM_WD * w)
    return delta, m, v


def reference(x, c, positions, ada_w, ada_b, norm1_w, w_in, attn_sinks, sgu_ln_w, sgu_ln_b, sgu_w, sgu_b, proj_a, proj_b, w_out, norm2_w, ffn_w_gate, ffn_w_up, ffn_conv_w, ffn_conv_b, ffn_w_down, final_norm_w, loss_target, m_ada_w, m_ada_b, m_norm1_w, m_w_in, m_attn_sinks, m_sgu_ln_w, m_sgu_ln_b, m_sgu_w, m_sgu_b, m_proj_a, m_proj_b, m_w_out, m_norm2_w, m_ffn_w_gate, m_ffn_w_up, m_ffn_conv_w, m_ffn_conv_b, m_ffn_w_down, m_final_norm_w, v_ada_w, v_ada_b, v_norm1_w, v_w_in, v_attn_sinks, v_sgu_ln_w, v_sgu_ln_b, v_sgu_w, v_sgu_b, v_proj_a, v_proj_b, v_w_out, v_norm2_w, v_ffn_w_gate, v_ffn_w_up, v_ffn_conv_w, v_ffn_conv_b, v_ffn_w_down, v_final_norm_w):
    given = dict(x=x, c=c, positions=positions, ada_w=ada_w, ada_b=ada_b, norm1_w=norm1_w, w_in=w_in, attn_sinks=attn_sinks, sgu_ln_w=sgu_ln_w, sgu_ln_b=sgu_ln_b, sgu_w=sgu_w, sgu_b=sgu_b, proj_a=proj_a, proj_b=proj_b, w_out=w_out, norm2_w=norm2_w, ffn_w_gate=ffn_w_gate, ffn_w_up=ffn_w_up, ffn_conv_w=ffn_conv_w, ffn_conv_b=ffn_conv_b, ffn_w_down=ffn_w_down, final_norm_w=final_norm_w, loss_target=loss_target, m_ada_w=m_ada_w, m_ada_b=m_ada_b, m_norm1_w=m_norm1_w, m_w_in=m_w_in, m_attn_sinks=m_attn_sinks, m_sgu_ln_w=m_sgu_ln_w, m_sgu_ln_b=m_sgu_ln_b, m_sgu_w=m_sgu_w, m_sgu_b=m_sgu_b, m_proj_a=m_proj_a, m_proj_b=m_proj_b, m_w_out=m_w_out, m_norm2_w=m_norm2_w, m_ffn_w_gate=m_ffn_w_gate, m_ffn_w_up=m_ffn_w_up, m_ffn_conv_w=m_ffn_conv_w, m_ffn_conv_b=m_ffn_conv_b, m_ffn_w_down=m_ffn_w_down, m_final_norm_w=m_final_norm_w, v_ada_w=v_ada_w, v_ada_b=v_ada_b, v_norm1_w=v_norm1_w, v_w_in=v_w_in, v_attn_sinks=v_attn_sinks, v_sgu_ln_w=v_sgu_ln_w, v_sgu_ln_b=v_sgu_ln_b, v_sgu_w=v_sgu_w, v_sgu_b=v_sgu_b, v_proj_a=v_proj_a, v_proj_b=v_proj_b, v_w_out=v_w_out, v_norm2_w=v_norm2_w, v_ffn_w_gate=v_ffn_w_gate, v_ffn_w_up=v_ffn_w_up, v_ffn_conv_w=v_ffn_conv_w, v_ffn_conv_b=v_ffn_conv_b, v_ffn_w_down=v_ffn_w_down, v_final_norm_w=v_final_norm_w)
    weights = {n: given[n] for n in TWIN_WEIGHTS}
    shared = {n: given[n] for n in SHARED_INPUTS}
    per_example = {n: given[n] for n in ['x', 'c', 'positions']}
    grad_fn = _jax.value_and_grad(_loss, argnums=(0, 1))

    def one_microbatch(ex, loss_target):
        ex = dict(ex)
        diff = ex.pop(TWIN_DIFF_INPUT)
        return grad_fn(weights, diff, {**shared, **ex}, loss_target)

    if N_MICROBATCH == 1:
        loss, (grad_w, grad_x) = one_microbatch(per_example, given["loss_target"])
    else:
        def body(carry, xs):
            loss_sum, grad_sum = carry
            l_k, (gw_k, gx_k) = one_microbatch(xs[0], xs[1])
            with _jax.named_scope("update"):
                return (loss_sum + l_k, _jax.tree.map(_jnp.add, grad_sum, gw_k)), gx_k

        init = (_jnp.zeros((), _jnp.float32), _jax.tree.map(_jnp.zeros_like, weights))
        (loss, grad_w), grad_x = _jax.lax.scan(body, init, (per_example, given["loss_target"]))
    with _jax.named_scope("update"):
        delta_w, new_m, new_v = {}, {}, {}
        for n in TWIN_WEIGHTS:
            delta_w[n], new_m[n], new_v[n] = _adamw(weights[n], grad_w[n], given["m_" + n], given["v_" + n])
    return (loss, grad_x, *[grad_w[n] for n in TWIN_WEIGHTS], *[delta_w[n] for n in TWIN_WEIGHTS],
            *[new_m[n] for n in TWIN_WEIGHTS], *[new_v[n] for n in TWIN_WEIGHTS])
```

```python
import functools

import jax
import jax.numpy as jnp
import numpy as np
from jax import lax
from jax.experimental import pallas as pl
from jax.experimental.pallas import tpu as pltpu

F32 = jnp.float32
BF = jnp.bfloat16

N_DEV = 8
D_MODEL = 1024
DEPTH = 2
N_Q_HEADS = 16
N_KV_HEADS = 2
HEAD_DIM = 64
Q_PER_KV = N_Q_HEADS // N_KV_HEADS
ATTN_BLOCK = 128
ROPE_THETA = 500000.0
ROT_DIM = HEAD_DIM // 4
SGU_WIDTH = 1024
SGU_GROUPS = 8
SGU_CHUNK = 128
FFN_DIM = 2816
NORM_EPS = 1e-6
Q_END = N_Q_HEADS * HEAD_DIM
K_END = Q_END + N_KV_HEADS * HEAD_DIM
V_END = K_END + N_KV_HEADS * HEAD_DIM
Z_END = V_END + 2 * SGU_WIDTH
IN_COLS = Z_END + 2 * D_MODEL
P_Z, P_G, P_Q, P_K, P_V = 0, 2048, 4096, 5120, 5248

ADAM_LR = 0.001
ADAM_B1 = 0.9
ADAM_B2 = 0.999
ADAM_EPS = 1e-08
ADAM_WD = 0.01
ADAM_STEP = 10

VMEM_LIMIT_BYTES = 56 * 1024 * 1024

BIG = ("w_in", "proj_a", "proj_b", "w_out", "ffn_w_gate", "ffn_w_up", "ffn_w_down")
COL_SHARDED = ("w_in", "ffn_w_gate", "ffn_w_up")
BIG_SHAPE = {"w_in": (D_MODEL, IN_COLS), "proj_a": (SGU_WIDTH, D_MODEL), "proj_b": (Q_END, D_MODEL),
             "w_out": (D_MODEL, D_MODEL), "ffn_w_gate": (D_MODEL, FFN_DIM), "ffn_w_up": (D_MODEL, FFN_DIM),
             "ffn_w_down": (FFN_DIM, D_MODEL)}
BIG_ROWS = {n: BIG_SHAPE[n][0] * BIG_SHAPE[n][1] // N_DEV // 1024 for n in BIG}
LAYER_ROWS = sum(BIG_ROWS.values())


def _pcall(body, **kw):
    return pl.pallas_call(body, **kw)


def _params(**kw):
    return pltpu.CompilerParams(vmem_limit_bytes=VMEM_LIMIT_BYTES, **kw)


def _tile(n, cap, unit=128):
    if n <= cap:
        return n
    best = 0
    t = unit
    while t <= cap:
        if n % t == 0:
            best = t
        t += unit
    assert best, (n, cap, unit)
    return best


def _mm(a, b, *, nt, out_dtype, name, res=None, gvec=None, tm=512, tn_cap=1024):
    M, K = a.shape
    N = b.shape[0] if nt else b.shape[1]
    tm = _tile(M, tm, 8)
    tn = _tile(N, tn_cap)
    dn = (((1,), (1,)), ((), ())) if nt else (((1,), (0,)), ((), ()))
    b_spec = pl.BlockSpec((tn, K), lambda i, j: (j, 0)) if nt else pl.BlockSpec((K, tn), lambda i, j: (0, j))
    o_spec = pl.BlockSpec((tm, tn), lambda i, j: (i, j))
    if res is None:
        def body(a_ref, b_ref, o_ref):
            acc = lax.dot_general(a_ref[...].astype(BF), b_ref[...].astype(BF), dn, preferred_element_type=F32)
            o_ref[...] = acc.astype(out_dtype)
        return _pcall(body, name=name, grid=(M // tm, N // tn),
                      in_specs=[pl.BlockSpec((tm, K), lambda i, j: (i, 0)), b_spec], out_specs=o_spec,
                      out_shape=jax.ShapeDtypeStruct((M, N), out_dtype), compiler_params=_params())(a, b)

    def body_res(a_ref, b_ref, r_ref, g_ref, o_ref, acc_ref):
        acc = lax.dot_general(a_ref[...].astype(BF), b_ref[...].astype(BF), dn, preferred_element_type=F32)
        acc_ref[...] = acc
        o_ref[...] = r_ref[...] + g_ref[...] * acc
    return _pcall(body_res, name=name, grid=(M // tm, N // tn),
                  in_specs=[pl.BlockSpec((tm, K), lambda i, j: (i, 0)), b_spec, o_spec,
                            pl.BlockSpec((1, tn), lambda i, j: (0, j))],
                  out_specs=[o_spec, o_spec],
                  out_shape=[jax.ShapeDtypeStruct((M, N), F32), jax.ShapeDtypeStruct((M, N), F32)],
                  compiler_params=_params())(a, b, res, gvec)


def _mm_tn(a, b, *, name, tk=512, tm_cap=1408, tn_cap=1024):
    S, M = a.shape
    N = b.shape[1]
    tk = _tile(S, tk, 8)
    tm = _tile(M, tm_cap)
    tn = _tile(N, tn_cap)

    def body(a_ref, b_ref, o_ref):
        @pl.when(pl.program_id(2) == 0)
        def _():
            o_ref[...] = jnp.zeros_like(o_ref)
        o_ref[...] += lax.dot_general(a_ref[...].astype(BF), b_ref[...].astype(BF), (((0,), (0,)), ((), ())),
                                      preferred_element_type=F32)
    return _pcall(body, name=name, grid=(M // tm, N // tn, S // tk),
                  in_specs=[pl.BlockSpec((tk, tm), lambda i, j, k: (k, i)),
                            pl.BlockSpec((tk, tn), lambda i, j, k: (k, j))],
                  out_specs=pl.BlockSpec((tm, tn), lambda i, j, k: (i, j)),
                  out_shape=jax.ShapeDtypeStruct((M, N), F32), compiler_params=_params())(a, b)


def _rms(x, w):
    return x * lax.rsqrt(jnp.mean(x * x, axis=-1, keepdims=True) + NORM_EPS) * w


def _normmod_fn(x, nw, sc, sh):
    return _rms(x, nw) * (1.0 + sc) + sh


def _gelu(x):
    return 0.5 * x * (1.0 + lax.erf(x * (2.0 ** -0.5)))


def _ln_gelu_fn(zv, w, b):
    v = _gelu(zv)
    mu = jnp.mean(v, axis=-1, keepdims=True)
    var = jnp.mean(jnp.square(v - mu), axis=-1, keepdims=True)
    return (v - mu) * lax.rsqrt(var + NORM_EPS) * w + b


def _sigmoid(x):
    return 1.0 / (1.0 + jnp.exp(-x))


def _row_spec(tm, n):
    return pl.BlockSpec((tm, n), lambda i: (i, 0))


def _vec_spec(n):
    return pl.BlockSpec((1, n), lambda i: (0, 0))


def _acc(ref, val):
    @pl.when(pl.program_id(0) == 0)
    def _():
        ref[...] = jnp.zeros_like(ref)
    ref[...] += val


def _normmod_fwd(x, nw, sc, sh, *, name, tm=512):
    S, Dm = x.shape
    tm = _tile(S, tm, 8)

    def body(x_ref, nw_ref, sc_ref, sh_ref, o_ref):
        o_ref[...] = _normmod_fn(x_ref[...], nw_ref[...], sc_ref[...], sh_ref[...]).astype(BF)
    return _pcall(body, name=name, grid=(S // tm,),
                  in_specs=[_row_spec(tm, Dm), _vec_spec(Dm), _vec_spec(Dm), _vec_spec(Dm)],
                  out_specs=_row_spec(tm, Dm), out_shape=jax.ShapeDtypeStruct((S, Dm), BF),
                  compiler_params=_params())(x, nw, sc, sh)


def _normmod_bwd(dh, x, nw, sc, sh, dres, *, name, tm=256):
    S, Dm = x.shape
    tm = _tile(S, tm, 8)

    def body(dh_ref, x_ref, nw_ref, sc_ref, sh_ref, dres_ref, dx_ref, dnw_ref, dsc_ref, dsh_ref):
        _, vjp = jax.vjp(_normmod_fn, x_ref[...], nw_ref[...], sc_ref[...], sh_ref[...])
        dx, dnw, dsc, dsh = vjp(dh_ref[...])
        dx_ref[...] = dres_ref[...] + dx
        _acc(dnw_ref, dnw)
        _acc(dsc_ref, dsc)
        _acc(dsh_ref, dsh)
    vec = jax.ShapeDtypeStruct((1, Dm), F32)
    return _pcall(body, name=name, grid=(S // tm,),
                  in_specs=[_row_spec(tm, Dm), _row_spec(tm, Dm), _vec_spec(Dm), _vec_spec(Dm), _vec_spec(Dm),
                            _row_spec(tm, Dm)],
                  out_specs=[_row_spec(tm, Dm), _vec_spec(Dm), _vec_spec(Dm), _vec_spec(Dm)],
                  out_shape=[jax.ShapeDtypeStruct((S, Dm), F32), vec, vec, vec],
                  compiler_params=_params())(dh, x, nw, sc, sh, dres)


def _scale_reduce(dx, o, g, *, name, tm=512):
    S, Dm = dx.shape
    tm = _tile(S, tm, 8)

    def body(dx_ref, o_ref, g_ref, do_ref, dg_ref):
        dxv = dx_ref[...]
        do_ref[...] = (dxv * g_ref[...]).astype(BF)
        _acc(dg_ref, jnp.sum(dxv * o_ref[...], axis=0, keepdims=True))
    return _pcall(body, name=name, grid=(S // tm,),
                  in_specs=[_row_spec(tm, Dm), _row_spec(tm, Dm), _vec_spec(Dm)],
                  out_specs=[_row_spec(tm, Dm), _vec_spec(Dm)],
                  out_shape=[jax.ShapeDtypeStruct((S, Dm), BF), jax.ShapeDtypeStruct((1, Dm), F32)],
                  compiler_params=_params())(dx, o, g)


def _head(x, fw, target, *, tm=256):
    S, Dm = x.shape
    tm = _tile(S, tm, 8)

    def body(x_ref, fw_ref, t_ref, dx_ref, dfw_ref, loss_ref):
        y, vjp = jax.vjp(_rms, x_ref[...], fw_ref[...])
        err = y - t_ref[...]
        dx, dfw = vjp(err * (1.0 / Dm))
        dx_ref[...] = dx
        _acc(dfw_ref, dfw)
        part = 0.5 * jnp.sum(jnp.mean(err * err, axis=-1, keepdims=True), axis=0, keepdims=True)
        _acc(loss_ref, jnp.broadcast_to(part, (8, 128)))
    return _pcall(body, name="head", grid=(S // tm,),
                  in_specs=[_row_spec(tm, Dm), _vec_spec(Dm), _row_spec(tm, Dm)],
                  out_specs=[_row_spec(tm, Dm), _vec_spec(Dm), pl.BlockSpec((8, 128), lambda i: (0, 0))],
                  out_shape=[jax.ShapeDtypeStruct((S, Dm), F32), jax.ShapeDtypeStruct((1, Dm), F32),
                             jax.ShapeDtypeStruct((8, 128), F32)],
                  compiler_params=_params())(x, fw, target)


def _tril_mask():
    r = lax.broadcasted_iota(jnp.int32, (SGU_CHUNK, SGU_CHUNK), 0)
    c = lax.broadcasted_iota(jnp.int32, (SGU_CHUNK, SGU_CHUNK), 1)
    return c <= r


def _sgu_fwd(proj, lnw, lnb, w, b_t, *, name, tm=256):
    S = proj.shape[0]
    tm = _tile(S, tm, SGU_CHUNK)

    def body(zu_ref, zv_ref, lnw_ref, lnb_ref, w_ref, bt_ref, o_ref):
        u = _gelu(zu_ref[...])
        vn = _ln_gelu_fn(zv_ref[...], lnw_ref[...], lnb_ref[...]).astype(BF)
        mask = _tril_mask()
        for g in range(SGU_GROUPS):
            wm = jnp.where(mask, w_ref[g], 0.0).astype(BF)
            cols = slice(g * 128, (g + 1) * 128)
            for ci in range(tm // SGU_CHUNK):
                rows = slice(ci * SGU_CHUNK, (ci + 1) * SGU_CHUNK)
                f = jnp.dot(wm, vn[rows, cols], preferred_element_type=F32) + bt_ref[:, g:g + 1]
                o_ref[rows, cols] = (u[rows, cols] * f).astype(BF)
    return _pcall(body, name=name, grid=(S // tm,),
                  in_specs=[pl.BlockSpec((tm, SGU_WIDTH), lambda i: (i, 0)), pl.BlockSpec((tm, SGU_WIDTH), lambda i: (i, 1)),
                            _vec_spec(SGU_WIDTH), _vec_spec(SGU_WIDTH),
                            pl.BlockSpec((SGU_GROUPS, 128, 128), lambda i: (0, 0, 0)),
                            pl.BlockSpec((128, SGU_GROUPS), lambda i: (0, 0))],
                  out_specs=_row_spec(tm, SGU_WIDTH), out_shape=jax.ShapeDtypeStruct((S, SGU_WIDTH), BF),
                  compiler_params=_params())(proj, proj, lnw, lnb, w, b_t)


def _sgu_bwd(dy, proj, lnw, lnb, w, b_t, *, name, tm=256):
    S = proj.shape[0]
    tm = _tile(S, tm, SGU_CHUNK)

    def body(dy_ref, zu_ref, zv_ref, lnw_ref, lnb_ref, w_ref, bt_ref, dz_ref, dlnw_ref, dlnb_ref, dw_ref, dbt_ref,
             f_s, dvn_s):
        first = pl.program_id(0) == 0

        @pl.when(first)
        def _():
            dw_ref[...] = jnp.zeros_like(dw_ref)
            dbt_ref[...] = jnp.zeros_like(dbt_ref)
        u, vjp_u = jax.vjp(_gelu, zu_ref[...])
        vn, vjp_v = jax.vjp(_ln_gelu_fn, zv_ref[...], lnw_ref[...], lnb_ref[...])
        vn = vn.astype(BF)
        dy_v = dy_ref[...]
        df = (dy_v * u).astype(BF)
        mask = _tril_mask()
        for g in range(SGU_GROUPS):
            wm = jnp.where(mask, w_ref[g], 0.0).astype(BF)
            cols = slice(g * 128, (g + 1) * 128)
            dwg = jnp.zeros((128, 128), F32)
            dbg = jnp.zeros((128, 1), F32)
            for ci in range(tm // SGU_CHUNK):
                rows = slice(ci * SGU_CHUNK, (ci + 1) * SGU_CHUNK)
                vn_c = vn[rows, cols]
                df_c = df[rows, cols]
                f_s[rows, cols] = jnp.dot(wm, vn_c, preferred_element_type=F32) + bt_ref[:, g:g + 1]
                dvn_s[rows, cols] = lax.dot_general(wm, df_c, (((0,), (0,)), ((), ())), preferred_element_type=F32)
                dwg = dwg + lax.dot_general(df_c, vn_c, (((1,), (1,)), ((), ())), preferred_element_type=F32)
                dbg = dbg + jnp.sum((dy_v[rows, cols] * u[rows, cols]), axis=1, keepdims=True)
            dw_ref[g] += jnp.where(mask, dwg, 0.0)
            dbt_ref[:, g:g + 1] += dbg
        (dzu,) = vjp_u(dy_v * f_s[...])
        dzv, dlnw, dlnb = vjp_v(dvn_s[...])
        dz_ref[:, :SGU_WIDTH] = dzu.astype(BF)
        dz_ref[:, SGU_WIDTH:] = dzv.astype(BF)
        _acc(dlnw_ref, dlnw)
        _acc(dlnb_ref, dlnb)
    vec = jax.ShapeDtypeStruct((1, SGU_WIDTH), F32)
    return _pcall(body, name=name, grid=(S // tm,),
                  in_specs=[_row_spec(tm, SGU_WIDTH),
                            pl.BlockSpec((tm, SGU_WIDTH), lambda i: (i, 0)), pl.BlockSpec((tm, SGU_WIDTH), lambda i: (i, 1)),
                            _vec_spec(SGU_WIDTH), _vec_spec(SGU_WIDTH),
                            pl.BlockSpec((SGU_GROUPS, 128, 128), lambda i: (0, 0, 0)),
                            pl.BlockSpec((128, SGU_GROUPS), lambda i: (0, 0))],
                  out_specs=[_row_spec(tm, 2 * SGU_WIDTH), _vec_spec(SGU_WIDTH), _vec_spec(SGU_WIDTH),
                             pl.BlockSpec((SGU_GROUPS, 128, 128), lambda i: (0, 0, 0)),
                             pl.BlockSpec((128, SGU_GROUPS), lambda i: (0, 0))],
                  out_shape=[jax.ShapeDtypeStruct((S, 2 * SGU_WIDTH), BF), vec, vec,
                             jax.ShapeDtypeStruct((SGU_GROUPS, 128, 128), F32),
                             jax.ShapeDtypeStruct((128, SGU_GROUPS), F32)],
                  scratch_shapes=[pltpu.VMEM((tm, SGU_WIDTH), F32), pltpu.VMEM((tm, SGU_WIDTH), F32)],
                  compiler_params=_params())(dy, proj, proj, lnw, lnb, w, b_t)


def _merge_fwd(a, b, proj, *, name, tm=512):
    S, Dm = a.shape
    tm = _tile(S, tm, 8)
    ga_blk, gb_blk = P_G // Dm, P_G // Dm + 1

    def body(a_ref, b_ref, ga_ref, gb_ref, o_ref):
        o_ref[...] = (_sigmoid(ga_ref[...]) * a_ref[...].astype(F32)
                      + _sigmoid(gb_ref[...]) * b_ref[...].astype(F32)).astype(BF)
    return _pcall(body, name=name, grid=(S // tm,),
                  in_specs=[_row_spec(tm, Dm), _row_spec(tm, Dm), pl.BlockSpec((tm, Dm), lambda i: (i, ga_blk)),
                            pl.BlockSpec((tm, Dm), lambda i: (i, gb_blk))],
                  out_specs=_row_spec(tm, Dm), out_shape=jax.ShapeDtypeStruct((S, Dm), BF),
                  compiler_params=_params())(a, b, proj, proj)


def _merge_bwd(dm, a, b, proj, *, name, tm=512):
    S, Dm = a.shape
    tm = _tile(S, tm, 8)
    ga_blk, gb_blk = P_G // Dm, P_G // Dm + 1

    def body(dm_ref, a_ref, b_ref, ga_ref, gb_ref, da_ref, db_ref, dg_ref):
        dmv = dm_ref[...]
        sa = _sigmoid(ga_ref[...])
        sb = _sigmoid(gb_ref[...])
        da_ref[...] = (dmv * sa).astype(BF)
        db_ref[...] = (dmv * sb).astype(BF)
        dg_ref[:, :Dm] = (dmv * a_ref[...].astype(F32) * sa * (1.0 - sa)).astype(BF)
        dg_ref[:, Dm:] = (dmv * b_ref[...].astype(F32) * sb * (1.0 - sb)).astype(BF)
    return _pcall(body, name=name, grid=(S // tm,),
                  in_specs=[_row_spec(tm, Dm), _row_spec(tm, Dm), _row_spec(tm, Dm),
                            pl.BlockSpec((tm, Dm), lambda i: (i, ga_blk)), pl.BlockSpec((tm, Dm), lambda i: (i, gb_blk))],
                  out_specs=[_row_spec(tm, Dm), _row_spec(tm, Dm), _row_spec(tm, 2 * Dm)],
                  out_shape=[jax.ShapeDtypeStruct((S, Dm), BF), jax.ShapeDtypeStruct((S, Dm), BF),
                             jax.ShapeDtypeStruct((S, 2 * Dm), BF)],
                  compiler_params=_params())(dm, a, b, proj, proj)


def _shift_rows(a, halo, k, up):
    n = a.shape[0]
    r8 = lax.broadcasted_iota(jnp.int32, (8, a.shape[1]), 0)
    if not up:
        rolled = pltpu.roll(a, k, 0)
        patch = jnp.where(r8 < k, pltpu.roll(halo, k, 0), rolled[:8])
        return jnp.concatenate([patch, rolled[8:]], axis=0)
    rolled = pltpu.roll(a, n - k, 0)
    patch = jnp.where(r8 >= 8 - k, pltpu.roll(halo, 8 - k, 0), rolled[n - 8:])
    return jnp.concatenate([rolled[:n - 8], patch], axis=0)


def _conv_taps(a, halo):
    return _shift_rows(a, halo, 2, False), _shift_rows(a, halo, 1, False), a


def _ffn_act_fwd(au, cw, cb, *, name, tm=256):
    S = au.shape[0]
    Fd = FFN_DIM
    tm = _tile(S, tm, 8)
    hb = tm // 8

    def body(a_ref, up_ref, halo_ref, cw_ref, cb_ref, o_ref):
        halo = jnp.where(pl.program_id(0) > 0, halo_ref[...], 0.0)
        t0, t1, t2 = _conv_taps(a_ref[...], halo)
        ac = cb_ref[...] + cw_ref[0:1, :] * t0 + cw_ref[1:2, :] * t1 + cw_ref[2:3, :] * t2
        o_ref[...] = (ac * _sigmoid(ac) * up_ref[...]).astype(BF)
    return _pcall(body, name=name, grid=(S // tm,),
                  in_specs=[pl.BlockSpec((tm, Fd), lambda i: (i, 0)), pl.BlockSpec((tm, Fd), lambda i: (i, 1)),
                            pl.BlockSpec((8, Fd), lambda i: (jnp.maximum(i * hb - 1, 0), 0)),
                            pl.BlockSpec((3, Fd), lambda i: (0, 0)), _vec_spec(Fd)],
                  out_specs=_row_spec(tm, Fd), out_shape=jax.ShapeDtypeStruct((S, Fd), BF),
                  compiler_params=_params())(au, au, au, cw, cb)


def _ffn_act_bwd_a(dhf, au, cw, cb, *, name, tm=256):
    S = au.shape[0]
    Fd = FFN_DIM
    tm = _tile(S, tm, 8)
    hb = tm // 8

    def body(dhf_ref, a_ref, up_ref, halo_ref, cw_ref, cb_ref, dac_ref, dup_ref, dcw_ref, dcb_ref):
        halo = jnp.where(pl.program_id(0) > 0, halo_ref[...], 0.0)
        t0, t1, t2 = _conv_taps(a_ref[...], halo)
        ac = cb_ref[...] + cw_ref[0:1, :] * t0 + cw_ref[1:2, :] * t1 + cw_ref[2:3, :] * t2
        s = _sigmoid(ac)
        dhf_v = dhf_ref[...]
        dup_ref[...] = (dhf_v * ac * s).astype(BF)
        dac = dhf_v * up_ref[...] * (s * (1.0 + ac * (1.0 - s)))
        dac_ref[...] = dac
        _acc(dcb_ref, jnp.sum(dac, axis=0, keepdims=True))
        _acc(dcw_ref, jnp.concatenate([jnp.sum(dac * t0, axis=0, keepdims=True),
                                       jnp.sum(dac * t1, axis=0, keepdims=True),
                                       jnp.sum(dac * t2, axis=0, keepdims=True)], axis=0))
    return _pcall(body, name=name, grid=(S // tm,),
                  in_specs=[_row_spec(tm, Fd), pl.BlockSpec((tm, Fd), lambda i: (i, 0)),
                            pl.BlockSpec((tm, Fd), lambda i: (i, 1)),
                            pl.BlockSpec((8, Fd), lambda i: (jnp.maximum(i * hb - 1, 0), 0)),
                            pl.BlockSpec((3, Fd), lambda i: (0, 0)), _vec_spec(Fd)],
                  out_specs=[_row_spec(tm, Fd), _row_spec(tm, Fd), pl.BlockSpec((3, Fd), lambda i: (0, 0)), _vec_spec(Fd)],
                  out_shape=[jax.ShapeDtypeStruct((S, Fd), F32), jax.ShapeDtypeStruct((S, Fd), BF),
                             jax.ShapeDtypeStruct((3, Fd), F32), jax.ShapeDtypeStruct((1, Fd), F32)],
                  compiler_params=_params())(dhf, au, au, au, cw, cb)


def _ffn_act_bwd_b(dac, cw, *, name, tm=256):
    S, Fd = dac.shape
    tm = _tile(S, tm, 8)
    hb = tm // 8
    last = S // tm - 1

    def body(d_ref, halo_ref, cw_ref, o_ref):
        halo = jnp.where(pl.program_id(0) < last, halo_ref[...], 0.0)
        d = d_ref[...]
        o_ref[...] = (cw_ref[2:3, :] * d + cw_ref[1:2, :] * _shift_rows(d, halo, 1, True)
                      + cw_ref[0:1, :] * _shift_rows(d, halo, 2, True)).astype(BF)
    return _pcall(body, name=name, grid=(S // tm,),
                  in_specs=[_row_spec(tm, Fd), pl.BlockSpec((8, Fd), lambda i: (jnp.minimum((i + 1) * hb, S // 8 - 1), 0)),
                            pl.BlockSpec((3, Fd), lambda i: (0, 0))],
                  out_specs=_row_spec(tm, Fd), out_shape=jax.ShapeDtypeStruct((S, Fd), BF),
                  compiler_params=_params())(dac, dac, cw)


def _rope_tables(pos_col, inv_row, m1_row, m2_row):
    S = pos_col.shape[0]
    tm = _tile(S, 512, 8)

    def body(p_ref, inv_ref, m1_ref, m2_ref, c_ref, s1_ref, s2_ref):
        ang = p_ref[...] * inv_ref[...]
        sn = jnp.sin(ang)
        c_ref[...] = jnp.cos(ang)
        s1_ref[...] = -sn * m1_ref[...]
        s2_ref[...] = sn * m2_ref[...]
    sh = jax.ShapeDtypeStruct((S, 128), F32)
    return _pcall(body, name="rope_tables", grid=(S // tm,),
                  in_specs=[pl.BlockSpec((tm, 1), lambda i: (i, 0)), _vec_spec(128), _vec_spec(128), _vec_spec(128)],
                  out_specs=[_row_spec(tm, 128)] * 3, out_shape=[sh, sh, sh], compiler_params=_params())(
                      pos_col, inv_row, m1_row, m2_row)


def _rope_apply(x, c, s1, s2):
    outs = []
    for j in range(x.shape[1] // 128):
        xj = x[:, j * 128:(j + 1) * 128]
        outs.append(xj * c + pltpu.roll(xj, 120, 1) * s1 + pltpu.roll(xj, 8, 1) * s2)
    return outs[0] if len(outs) == 1 else jnp.concatenate(outs, axis=1)


def _rope_apply_t(d, c, s1, s2):
    outs = []
    for j in range(d.shape[1] // 128):
        dj = d[:, j * 128:(j + 1) * 128]
        outs.append(dj * c + pltpu.roll(dj * s1, 8, 1) + pltpu.roll(dj * s2, 120, 1))
    return outs[0] if len(outs) == 1 else jnp.concatenate(outs, axis=1)


def _rope_fwd(proj, c, s1, s2, *, name, tm=512):
    S = proj.shape[0]
    tm = _tile(S, tm, 8)

    def body(q_ref, k_ref, v_ref, c_ref, s1_ref, s2_ref, qo_ref, kvo_ref):
        cv, s1v, s2v = c_ref[...], s1_ref[...], s2_ref[...]
        qo_ref[...] = _rope_apply(q_ref[...], cv, s1v, s2v).astype(BF)
        kvo_ref[:, :128] = _rope_apply(k_ref[...], cv, s1v, s2v).astype(BF)
        kvo_ref[:, 128:] = v_ref[...].astype(BF)
    return _pcall(body, name=name, grid=(S // tm,),
                  in_specs=[pl.BlockSpec((tm, Q_END), lambda i: (i, P_Q // Q_END)),
                            pl.BlockSpec((tm, 128), lambda i: (i, P_K // 128)),
                            pl.BlockSpec((tm, 128), lambda i: (i, P_V // 128)),
                            _row_spec(tm, 128), _row_spec(tm, 128), _row_spec(tm, 128)],
                  out_specs=[_row_spec(tm, Q_END), _row_spec(tm, 256)],
                  out_shape=[jax.ShapeDtypeStruct((S, Q_END), BF), jax.ShapeDtypeStruct((S, 256), BF)],
                  compiler_params=_params())(proj, proj, proj, c, s1, s2)


def _rope_bwd(dq, dkv, c, s1, s2, *, name, tm=512):
    S = dq.shape[0]
    tm = _tile(S, tm, 8)

    def body(dq_ref, dkv_ref, c_ref, s1_ref, s2_ref, o_ref):
        cv, s1v, s2v = c_ref[...], s1_ref[...], s2_ref[...]
        o_ref[:, :Q_END] = _rope_apply_t(dq_ref[...].astype(F32), cv, s1v, s2v).astype(BF)
        o_ref[:, Q_END:Q_END + 128] = _rope_apply_t(dkv_ref[:, :128].astype(F32), cv, s1v, s2v).astype(BF)
        o_ref[:, Q_END + 128:] = dkv_ref[:, 128:].astype(BF)
    return _pcall(body, name=name, grid=(S // tm,),
                  in_specs=[_row_spec(tm, Q_END), _row_spec(tm, 256), _row_spec(tm, 128), _row_spec(tm, 128),
                            _row_spec(tm, 128)],
                  out_specs=_row_spec(tm, V_END), out_shape=jax.ShapeDtypeStruct((S, V_END), BF),
                  compiler_params=_params())(dq, dkv, c, s1, s2)


def _attn_probs(q, kb, sink, n):
    R = Q_PER_KV * ATTN_BLOCK
    s = lax.dot_general(q, kb, (((1,), (1,)), ((), ())), preferred_element_type=F32) * (HEAD_DIM ** -0.5)
    i = lax.broadcasted_iota(jnp.int32, (R, 2 * ATTN_BLOCK), 0) & (ATTN_BLOCK - 1)
    j = lax.broadcasted_iota(jnp.int32, (R, 2 * ATTN_BLOCK), 1)
    ok = (j > i) & (j <= i + ATTN_BLOCK) & ((n > 0) | (j >= ATTN_BLOCK))
    s = jnp.where(ok, s, -jnp.inf)
    m = jnp.maximum(jnp.max(s, axis=-1, keepdims=True), sink)
    p = jnp.exp(s - m)
    es = jnp.exp(sink - m)
    inv = 1.0 / (jnp.sum(p, axis=-1, keepdims=True) + es)
    return p * inv, es * inv


def _attn_specs(S):
    nb = S // ATTN_BLOCK
    qs = pl.BlockSpec((Q_PER_KV, ATTN_BLOCK, HEAD_DIM), lambda g, n: (g, n, 0))
    cur = pl.BlockSpec((None, ATTN_BLOCK, HEAD_DIM), lambda g, n: (g, n, 0))
    prev = pl.BlockSpec((None, ATTN_BLOCK, HEAD_DIM), lambda g, n: (g, jnp.maximum(n - 1, 0), 0))
    sink = pl.BlockSpec((None, Q_PER_KV * ATTN_BLOCK, 1), lambda g, n: (g, 0, 0))
    return nb, qs, cur, prev, sink


def _attn_fwd(qh, kh, vh, sink_rows, *, name):
    S = qh.shape[1]
    nb, qs, cur, prev, sink = _attn_specs(S)
    R = Q_PER_KV * ATTN_BLOCK

    def body(q_ref, kp_ref, kc_ref, vp_ref, vc_ref, sk_ref, o_ref):
        n = pl.program_id(1)
        q = q_ref[...].reshape(R, HEAD_DIM)
        kb = jnp.concatenate([kp_ref[...], kc_ref[...]], axis=0)
        vb = jnp.concatenate([vp_ref[...], vc_ref[...]], axis=0)
        p, _ = _attn_probs(q, kb, sk_ref[...], n)
        o = jnp.dot(p.astype(BF), vb, preferred_element_type=F32)
        o_ref[...] = o.reshape(Q_PER_KV, ATTN_BLOCK, HEAD_DIM).astype(BF)
    return _pcall(body, name=name, grid=(N_KV_HEADS, nb), in_specs=[qs, prev, cur, prev, cur, sink], out_specs=qs,
                  out_shape=jax.ShapeDtypeStruct(qh.shape, BF), compiler_params=_params())(qh, kh, kh, vh, vh, sink_rows)


def _attn_bwd(do, qh, kh, vh, sink_rows, *, name):
    S = qh.shape[1]
    nb, qs, cur, prev, sink = _attn_specs(S)
    R = Q_PER_KV * ATTN_BLOCK
    full = pl.BlockSpec((None, S, HEAD_DIM), lambda g, n: (g, 0, 0))
    dsk_spec = pl.BlockSpec((None, Q_PER_KV, 128), lambda g, n: (g, 0, 0))

    def body(do_ref, q_ref, kp_ref, kc_ref, vp_ref, vc_ref, sk_ref, dq_ref, dk_ref, dv_ref, dsk_ref):
        n = pl.program_id(1)

        @pl.when(n == 0)
        def _():
            dk_ref[...] = jnp.zeros_like(dk_ref)
            dv_ref[...] = jnp.zeros_like(dv_ref)
            dsk_ref[...] = jnp.zeros_like(dsk_ref)
        q = q_ref[...].reshape(R, HEAD_DIM)
        dov = do_ref[...].reshape(R, HEAD_DIM)
        kb = jnp.concatenate([kp_ref[...], kc_ref[...]], axis=0)
        vb = jnp.concatenate([vp_ref[...], vc_ref[...]], axis=0)
        p, ps = _attn_probs(q, kb, sk_ref[...], n)
        dp = lax.dot_general(dov, vb, (((1,), (1,)), ((), ())), preferred_element_type=F32)
        dd = jnp.sum(p * dp, axis=-1, keepdims=True)
        ds = (p * (dp - dd) * (HEAD_DIM ** -0.5)).astype(BF)
        dq = jnp.dot(ds, kb, preferred_element_type=F32)
        dq_ref[...] = dq.reshape(Q_PER_KV, ATTN_BLOCK, HEAD_DIM).astype(BF)
        dkb = lax.dot_general(ds, q, (((0,), (0,)), ((), ())), preferred_element_type=F32)
        dvb = lax.dot_general(p.astype(BF), dov, (((0,), (0,)), ((), ())), preferred_element_type=F32)
        r0 = pl.multiple_of(n * ATTN_BLOCK, ATTN_BLOCK)
        dk_ref[pl.ds(r0, ATTN_BLOCK), :] += dkb[ATTN_BLOCK:]
        dv_ref[pl.ds(r0, ATTN_BLOCK), :] += dvb[ATTN_BLOCK:]

        @pl.when(n > 0)
        def _():
            rp = pl.multiple_of((n - 1) * ATTN_BLOCK, ATTN_BLOCK)
            dk_ref[pl.ds(rp, ATTN_BLOCK), :] += dkb[:ATTN_BLOCK]
            dv_ref[pl.ds(rp, ATTN_BLOCK), :] += dvb[:ATTN_BLOCK]
        dsr = -(ps * dd)
        sub = lax.broadcasted_iota(jnp.int32, (Q_PER_KV, 128), 0)
        upd = jnp.zeros((Q_PER_KV, 128), F32)
        for h in range(Q_PER_KV):
            upd = jnp.where(sub == h, jnp.sum(dsr[h * ATTN_BLOCK:(h + 1) * ATTN_BLOCK]), upd)
        dsk_ref[...] += upd
    return _pcall(body, name=name, grid=(N_KV_HEADS, nb), in_specs=[qs, qs, prev, cur, prev, cur, sink],
                  out_specs=[qs, full, full, dsk_spec],
                  out_shape=[jax.ShapeDtypeStruct(qh.shape, BF), jax.ShapeDtypeStruct(kh.shape, F32),
                             jax.ShapeDtypeStruct(kh.shape, F32), jax.ShapeDtypeStruct((N_KV_HEADS, Q_PER_KV, 128), F32)],
                  compiler_params=_params())(do, qh, kh, kh, vh, vh, sink_rows)


def _to_heads(x, nh):
    return x.reshape(x.shape[0], nh, HEAD_DIM).transpose(1, 0, 2)


def _from_heads(x):
    return x.transpose(1, 0, 2).reshape(x.shape[1], x.shape[0] * HEAD_DIM)


def _ada_fwd(c_all, ada_w):
    ncol = ada_w.shape[2]

    def body(c_ref, w_ref, o_ref):
        cv = c_ref[...]
        ca = (cv * _sigmoid(cv)).astype(BF)
        for l in range(DEPTH):
            o_ref[:, l * ncol:(l + 1) * ncol] = jnp.dot(ca, w_ref[l].astype(BF), preferred_element_type=F32)
    return _pcall(body, name="ada_fwd", out_shape=jax.ShapeDtypeStruct((N_DEV, DEPTH * ncol), F32),
                  compiler_params=_params())(c_all, ada_w)


def _ada_bwd(c_all, dm):
    ncol = dm.shape[2]

    def body(c_ref, dm_ref, o_ref):
        cv = c_ref[...]
        ca = (cv * _sigmoid(cv)).astype(BF)
        for l in range(DEPTH):
            o_ref[l] = lax.dot_general(ca, dm_ref[l].astype(BF), (((0,), (0,)), ((), ())), preferred_element_type=F32)
    return _pcall(body, name="ada_bwd", out_shape=jax.ShapeDtypeStruct((DEPTH, D_MODEL, ncol), F32),
                  compiler_params=_params())(c_all, dm)


def _adamw(w, g, m, v, *, name):
    R, C = w.shape
    tr = R
    for t in range(8, 513, 8):
        if R % t == 0:
            tr = t
    c1 = 1.0 - ADAM_B1 ** ADAM_STEP
    c2 = 1.0 - ADAM_B2 ** ADAM_STEP

    def body(w_ref, g_ref, m_ref, v_ref, d_ref, mo_ref, vo_ref):
        gv = g_ref[...]
        mn = ADAM_B1 * m_ref[...] + (1.0 - ADAM_B1) * gv
        vn = ADAM_B2 * v_ref[...] + (1.0 - ADAM_B2) * (gv * gv)
        mo_ref[...] = mn
        vo_ref[...] = vn
        d_ref[...] = -ADAM_LR * ((mn / c1) / (jnp.sqrt(vn / c2) + ADAM_EPS) + ADAM_WD * w_ref[...])
    spec = pl.BlockSpec((tr, C), lambda i: (i, 0))
    sh = jax.ShapeDtypeStruct((R, C), F32)
    return _pcall(body, name=name, grid=(R // tr,), in_specs=[spec] * 4, out_specs=[spec] * 3, out_shape=[sh, sh, sh],
                  compiler_params=_params())(w, g, m, v)


def _sum8(parts, *, name):
    _, R, C = parts.shape
    tr = R
    for t in range(16, 257, 16):
        if R % t == 0:
            tr = t

    def body(p_ref, o_ref):
        acc = p_ref[0].astype(F32)
        for k in range(1, N_DEV):
            acc = acc + p_ref[k].astype(F32)
        o_ref[...] = acc
    return _pcall(body, name=name, grid=(R // tr,), in_specs=[pl.BlockSpec((N_DEV, tr, C), lambda i: (0, i, 0))],
                  out_specs=pl.BlockSpec((tr, C), lambda i: (i, 0)), out_shape=jax.ShapeDtypeStruct((R, C), F32),
                  compiler_params=_params())(parts)


MESH_ID = pl.DeviceIdType.MESH
ANY = pl.BlockSpec(memory_space=pl.ANY)


def _all_gather(x, *, name):
    R, C = x.shape

    def body(x_ref, out_ref, send_sems, recv_sems, local_sem):
        mx, my, mc = lax.axis_index("x"), lax.axis_index("y"), lax.axis_index("c")
        me, sibling = (mx, my, mc), (mx, my, 1 - mc)
        chips = [(1 - mx, my), (mx, 1 - my), (1 - mx, 1 - my)]

        def blk(px, py, pc):
            return out_ref.at[4 * px + 2 * py + pc]

        def copy(k, block, to, src=None):
            return pltpu.make_async_remote_copy(
                src_ref=blk(*block) if src is None else src, dst_ref=blk(*block),
                send_sem=send_sems.at[k], recv_sem=recv_sems.at[k], device_id=to, device_id_type=MESH_ID)

        mine = pltpu.make_async_copy(x_ref, blk(*me), local_sem)
        mine.start()
        first = [copy(0, me, sibling, src=x_ref)]
        first += [copy(1 + j, me, (*chip, mc), src=x_ref) for j, chip in enumerate(chips)]
        for cp in first:
            cp.start()
        passed = [copy(4 + j, (*chip, mc), sibling) for j, chip in enumerate(chips)]
        for j, chip in enumerate(chips):
            copy(1 + j, (*chip, mc), me).wait_recv()
            passed[j].start()
        copy(0, sibling, me).wait_recv()
        for j, chip in enumerate(chips):
            copy(4 + j, (*chip, 1 - mc), me).wait_recv()
        for cp in first + passed:
            cp.wait_send()
        mine.wait()
    return _pcall(body, name=name, in_specs=[ANY], out_specs=ANY,
                  out_shape=jax.ShapeDtypeStruct((N_DEV, R, C), x.dtype),
                  scratch_shapes=[pltpu.SemaphoreType.DMA((7,)), pltpu.SemaphoreType.DMA((7,)), pltpu.SemaphoreType.DMA],
                  compiler_params=pltpu.CompilerParams(has_side_effects=True))(x)


def _exchange(g, *, name):
    _, R, C = g.shape

    def body(g_ref, out_ref, send_sems, recv_sems, local_sem):
        mx, my, mc = lax.axis_index("x"), lax.axis_index("y"), lax.axis_index("c")
        me = 4 * mx + 2 * my + mc
        mine = pltpu.make_async_copy(g_ref.at[me], out_ref.at[me], local_sem)
        mine.start()
        copies = []
        for k in range(1, N_DEV):
            px, py, pc = mx ^ (k >> 2), my ^ ((k >> 1) & 1), mc ^ (k & 1)
            copies.append(pltpu.make_async_remote_copy(
                src_ref=g_ref.at[4 * px + 2 * py + pc], dst_ref=out_ref.at[me],
                send_sem=send_sems.at[k - 1], recv_sem=recv_sems.at[k - 1],
                device_id=(px, py, pc), device_id_type=MESH_ID))
        for cp in copies:
            cp.start()
        for k in range(1, N_DEV):
            px, py, pc = mx ^ (k >> 2), my ^ ((k >> 1) & 1), mc ^ (k & 1)
            src = 4 * px + 2 * py + pc
            pltpu.make_async_remote_copy(
                src_ref=g_ref.at[src], dst_ref=out_ref.at[src], send_sem=send_sems.at[k - 1],
                recv_sem=recv_sems.at[k - 1], device_id=(px, py, pc), device_id_type=MESH_ID).wait_recv()
        for cp in copies:
            cp.wait_send()
        mine.wait()
    return _pcall(body, name=name, in_specs=[ANY], out_specs=ANY, out_shape=jax.ShapeDtypeStruct(g.shape, g.dtype),
                  scratch_shapes=[pltpu.SemaphoreType.DMA((7,)), pltpu.SemaphoreType.DMA((7,)), pltpu.SemaphoreType.DMA],
                  compiler_params=pltpu.CompilerParams(has_side_effects=True))(g)


def _pack_shards(shards):
    return jnp.concatenate([shards[n][l].astype(BF).reshape(BIG_ROWS[n], 1024) for l in range(DEPTH) for n in BIG], axis=0)


def _unpack_full(gathered):
    out = []
    r = 0
    for _ in range(DEPTH):
        ws = {}
        for n in BIG:
            rows = BIG_ROWS[n]
            blk = gathered[:, r:r + rows, :]
            K, N = BIG_SHAPE[n]
            if n in COL_SHARDED:
                ws[n] = blk.reshape(N_DEV, K, N // N_DEV).transpose(1, 0, 2).reshape(K, N)
            else:
                ws[n] = blk.reshape(K, N)
            r += rows
        out.append(ws)
    return out


def _pack_grads(grads):
    parts = []
    for l in range(DEPTH):
        for n in BIG:
            g = grads[l][n].astype(BF)
            K, N = BIG_SHAPE[n]
            if n in COL_SHARDED:
                g = g.reshape(K, N_DEV, N // N_DEV).transpose(1, 0, 2)
            parts.append(g.reshape(N_DEV, BIG_ROWS[n], 1024))
    return jnp.concatenate(parts, axis=1)


def _unpack_shard_grads(gsum):
    out = {n: [] for n in BIG}
    r = 0
    for _ in range(DEPTH):
        for n in BIG:
            rows = BIG_ROWS[n]
            K, N = BIG_SHAPE[n]
            shp = (K, N // N_DEV) if n in COL_SHARDED else (K // N_DEV, N)
            out[n].append(gsum[r:r + rows].reshape(shp))
            r += rows
    return {n: jnp.stack(v) for n, v in out.items()}


def _perm_w_in(w):
    return jnp.concatenate([w[:, V_END:], w[:, :V_END]], axis=1)


def _unperm_w_in(w):
    return jnp.concatenate([w[:, P_Q:], w[:, :P_Q]], axis=1)


def _local_step(x, mod, positions, weights, small, loss_target):
    S = x.shape[0]
    inv = ROPE_THETA ** (-jnp.arange(0, ROT_DIM, 2, dtype=F32) / ROT_DIM)
    lane = np.arange(128) % HEAD_DIM
    half = ROT_DIM // 2
    inv_row = jnp.where(lane < ROT_DIM, jnp.tile(inv, 128 // half), 0.0)[None, :].astype(F32)
    m1_row = jnp.asarray((lane < half).astype(np.float32))[None, :]
    m2_row = jnp.asarray(((lane >= half) & (lane < ROT_DIM)).astype(np.float32))[None, :]
    rc, rs1, rs2 = _rope_tables(positions.astype(F32).reshape(S, 1), inv_row, m1_row, m2_row)

    saved = []
    for l in range(DEPTH):
        W = weights[l]
        sh1, sc1, g1, sh2, sc2, g2 = [mod[l, i * D_MODEL:(i + 1) * D_MODEL][None, :] for i in range(6)]
        nw1, nw2 = small["norm1_w"][l][None, :], small["norm2_w"][l][None, :]
        h = _normmod_fwd(x, nw1, sc1, sh1, name=f"normmod1_fwd{l}")
        proj = _mm(h, W["w_in"], nt=False, out_dtype=F32, name=f"mm_in{l}", tn_cap=768)
        q_r, kv_r = _rope_fwd(proj, rc, rs1, rs2, name=f"rope_fwd{l}")
        qh, kh, vh = _to_heads(q_r, N_Q_HEADS), _to_heads(kv_r[:, :128], N_KV_HEADS), _to_heads(kv_r[:, 128:], N_KV_HEADS)
        sink_rows = jnp.repeat(small["attn_sinks"][l].reshape(N_KV_HEADS, Q_PER_KV), ATTN_BLOCK, axis=1)[..., None]
        y_attn = _from_heads(_attn_fwd(qh, kh, vh, sink_rows, name=f"attn_fwd{l}"))
        lnw, lnb = small["sgu_ln_w"][l][None, :], small["sgu_ln_b"][l][None, :]
        sgu_bt = small["sgu_b"][l].T
        y_sgu = _sgu_fwd(proj, lnw, lnb, small["sgu_w"][l], sgu_bt, name=f"sgu_fwd{l}")
        a_br = _mm(y_sgu, W["proj_a"], nt=False, out_dtype=BF, name=f"mm_pa{l}")
        b_br = _mm(y_attn, W["proj_b"], nt=False, out_dtype=BF, name=f"mm_pb{l}")
        merged = _merge_fwd(a_br, b_br, proj, name=f"merge_fwd{l}")
        x1, o1 = _mm(merged, W["w_out"], nt=False, out_dtype=F32, name=f"mm_out{l}", res=x, gvec=g1)
        h2 = _normmod_fwd(x1, nw2, sc2, sh2, name=f"normmod2_fwd{l}")
        au = _mm(h2, W["w_gu"], nt=False, out_dtype=F32, name=f"mm_gu{l}", tn_cap=1408)
        cw, cb = small["ffn_conv_w"][l], small["ffn_conv_b"][l][None, :]
        hf = _ffn_act_fwd(au, cw, cb, name=f"ffn_act_fwd{l}")
        x2, o2 = _mm(hf, W["ffn_w_down"], nt=False, out_dtype=F32, name=f"mm_down{l}", res=x1, gvec=g2)
        saved.append(dict(x=x, h=h, proj=proj, qh=qh, kh=kh, vh=vh, sink_rows=sink_rows, y_attn=y_attn, y_sgu=y_sgu,
                          a_br=a_br, b_br=b_br, merged=merged, x1=x1, o1=o1, h2=h2, au=au, hf=hf, o2=o2))
        x = x2

    dx, dfw, loss_tile = _head(x, small["final_norm_w"][None, :], loss_target)
    sg = {"final_norm_w": dfw[0]}
    per_layer = {n: [None] * DEPTH for n in ("norm1_w", "attn_sinks", "sgu_ln_w", "sgu_ln_b", "sgu_w", "sgu_b", "norm2_w",
                                            "ffn_conv_w", "ffn_conv_b")}
    wgrads = [None] * DEPTH
    dmod = [None] * DEPTH
    for l in reversed(range(DEPTH)):
        W, sv = weights[l], saved[l]
        sh1, sc1, g1, sh2, sc2, g2 = [mod[l, i * D_MODEL:(i + 1) * D_MODEL][None, :] for i in range(6)]
        nw1, nw2 = small["norm1_w"][l][None, :], small["norm2_w"][l][None, :]
        cw, cb = small["ffn_conv_w"][l], small["ffn_conv_b"][l][None, :]
        lnw, lnb = small["sgu_ln_w"][l][None, :], small["sgu_ln_b"][l][None, :]
        sgu_bt = small["sgu_b"][l].T
        wg = {}
        do2, dg2 = _scale_reduce(dx, sv["o2"], g2, name=f"scale2_{l}")
        dhf = _mm(do2, W["ffn_w_down"], nt=True, out_dtype=F32, name=f"mm_down_dx{l}", tn_cap=1408)
        wg["ffn_w_down"] = _mm_tn(sv["hf"], do2, name=f"mm_down_dw{l}")
        dac, dup, dcw, dcb = _ffn_act_bwd_a(dhf, sv["au"], cw, cb, name=f"ffn_act_bwd_a{l}")
        da = _ffn_act_bwd_b(dac, cw, name=f"ffn_act_bwd_b{l}")
        dau = jnp.concatenate([da, dup], axis=1)
        dh2 = _mm(dau, W["w_gu"], nt=True, out_dtype=F32, name=f"mm_gu_dx{l}")
        dwgu = _mm_tn(sv["h2"], dau, name=f"mm_gu_dw{l}")
        wg["ffn_w_gate"], wg["ffn_w_up"] = dwgu[:, :FFN_DIM], dwgu[:, FFN_DIM:]
        dx1, dnw2, dsc2, dsh2 = _normmod_bwd(dh2, sv["x1"], nw2, sc2, sh2, dx, name=f"normmod2_bwd{l}")
        do1, dg1 = _scale_reduce(dx1, sv["o1"], g1, name=f"scale1_{l}")
        dmerged = _mm(do1, W["w_out"], nt=True, out_dtype=F32, name=f"mm_out_dx{l}")
        wg["w_out"] = _mm_tn(sv["merged"], do1, name=f"mm_out_dw{l}")
        d_a, d_b, dgates = _merge_bwd(dmerged, sv["a_br"], sv["b_br"], sv["proj"], name=f"merge_bwd{l}")
        dysgu = _mm(d_a, W["proj_a"], nt=True, out_dtype=F32, name=f"mm_pa_dx{l}")
        dyattn = _mm(d_b, W["proj_b"], nt=True, out_dtype=BF, name=f"mm_pb_dx{l}")
        wg["proj_a"] = _mm_tn(sv["y_sgu"], d_a, name=f"mm_pa_dw{l}")
        wg["proj_b"] = _mm_tn(sv["y_attn"], d_b, name=f"mm_pb_dw{l}")
        dz, dlnw, dlnb, dsguw, dsgubt = _sgu_bwd(dysgu, sv["proj"], lnw, lnb, small["sgu_w"][l], sgu_bt, name=f"sgu_bwd{l}")
        dqh, dkh, dvh, dsk = _attn_bwd(_to_heads(dyattn, N_Q_HEADS), sv["qh"], sv["kh"], sv["vh"], sv["sink_rows"],
                                       name=f"attn_bwd{l}")
        dkv = jnp.concatenate([_from_heads(dkh), _from_heads(dvh)], axis=1)
        dqkv = _rope_bwd(_from_heads(dqh), dkv, rc, rs1, rs2, name=f"rope_bwd{l}")
        dproj = jnp.concatenate([dz, dgates, dqkv], axis=1)
        dh = _mm(dproj, W["w_in"], nt=True, out_dtype=F32, name=f"mm_in_dx{l}")
        wg["w_in"] = _mm_tn(sv["h"], dproj, name=f"mm_in_dw{l}", tn_cap=768)
        dx, dnw1, dsc1, dsh1 = _normmod_bwd(dh, sv["x"], nw1, sc1, sh1, dx1, name=f"normmod1_bwd{l}")
        wgrads[l] = wg
        dmod[l] = jnp.concatenate([dsh1, dsc1, dg1, dsh2, dsc2, dg2], axis=1)[0]
        per_layer["norm1_w"][l], per_layer["norm2_w"][l] = dnw1[0], dnw2[0]
        per_layer["attn_sinks"][l] = dsk[:, :, 0].reshape(N_Q_HEADS)
        per_layer["sgu_ln_w"][l], per_layer["sgu_ln_b"][l] = dlnw[0], dlnb[0]
        per_layer["sgu_w"][l], per_layer["sgu_b"][l] = dsguw, dsgubt.T
        per_layer["ffn_conv_w"][l], per_layer["ffn_conv_b"][l] = dcw, dcb[0]
    for n, v in per_layer.items():
        sg[n] = jnp.stack(v)
    return loss_tile, dx, wgrads, sg, jnp.stack(dmod)


SMALL = ("ada_b", "norm1_w", "attn_sinks", "sgu_ln_w", "sgu_ln_b", "sgu_w", "sgu_b", "norm2_w", "ffn_conv_b", "final_norm_w")
WEIGHT_ORDER = ("ada_w", "ada_b", "norm1_w", "w_in", "attn_sinks", "sgu_ln_w", "sgu_ln_b", "sgu_w", "sgu_b", "proj_a", "proj_b",
                "w_out", "norm2_w", "ffn_w_gate", "ffn_w_up", "ffn_conv_w", "ffn_conv_b", "ffn_w_down", "final_norm_w")


def _flat_pack(arrs, rows):
    flat = jnp.concatenate([a.reshape(-1) for a in arrs])
    return jnp.pad(flat, (0, rows * 1024 - flat.shape[0])).reshape(rows, 1024)


def _flat_unpack(buf, shapes):
    flat = buf.reshape(-1)
    out, o = [], 0
    for s in shapes:
        n = int(np.prod(s))
        out.append(flat[o:o + n].reshape(s))
        o += n
    return out


def _adam2d(w, g, m, v, *, name):
    shp = w.shape
    r2 = (int(np.prod(shp[:-1])), shp[-1]) if len(shp) > 1 else (1, shp[0])
    d, mn, vn = _adamw(w.reshape(r2), g.reshape(r2), m.reshape(r2), v.reshape(r2), name=name)
    return d.reshape(shp), mn.reshape(shp), vn.reshape(shp)


def kernel(x, c, positions, ada_w, ada_b, norm1_w, w_in, attn_sinks, sgu_ln_w, sgu_ln_b, sgu_w, sgu_b, proj_a, proj_b, w_out, norm2_w, ffn_w_gate, ffn_w_up, ffn_conv_w, ffn_conv_b, ffn_w_down, final_norm_w, loss_target, m_ada_w, m_ada_b, m_norm1_w, m_w_in, m_attn_sinks, m_sgu_ln_w, m_sgu_ln_b, m_sgu_w, m_sgu_b, m_proj_a, m_proj_b, m_w_out, m_norm2_w, m_ffn_w_gate, m_ffn_w_up, m_ffn_conv_w, m_ffn_conv_b, m_ffn_w_down, m_final_norm_w, v_ada_w, v_ada_b, v_norm1_w, v_w_in, v_attn_sinks, v_sgu_ln_w, v_sgu_ln_b, v_sgu_w, v_sgu_b, v_proj_a, v_proj_b, v_w_out, v_norm2_w, v_ffn_w_gate, v_ffn_w_up, v_ffn_conv_w, v_ffn_conv_b, v_ffn_w_down, v_final_norm_w):
    wts = dict(ada_w=ada_w, ada_b=ada_b, norm1_w=norm1_w, w_in=w_in, attn_sinks=attn_sinks, sgu_ln_w=sgu_ln_w,
               sgu_ln_b=sgu_ln_b, sgu_w=sgu_w, sgu_b=sgu_b, proj_a=proj_a, proj_b=proj_b, w_out=w_out, norm2_w=norm2_w,
               ffn_w_gate=ffn_w_gate, ffn_w_up=ffn_w_up, ffn_conv_w=ffn_conv_w, ffn_conv_b=ffn_conv_b,
               ffn_w_down=ffn_w_down, final_norm_w=final_norm_w)
    mom = dict(ada_w=m_ada_w, ada_b=m_ada_b, norm1_w=m_norm1_w, w_in=m_w_in, attn_sinks=m_attn_sinks, sgu_ln_w=m_sgu_ln_w,
               sgu_ln_b=m_sgu_ln_b, sgu_w=m_sgu_w, sgu_b=m_sgu_b, proj_a=m_proj_a, proj_b=m_proj_b, w_out=m_w_out,
               norm2_w=m_norm2_w, ffn_w_gate=m_ffn_w_gate, ffn_w_up=m_ffn_w_up, ffn_conv_w=m_ffn_conv_w,
               ffn_conv_b=m_ffn_conv_b, ffn_w_down=m_ffn_w_down, final_norm_w=m_final_norm_w)
    var = dict(ada_w=v_ada_w, ada_b=v_ada_b, norm1_w=v_norm1_w, w_in=v_w_in, attn_sinks=v_attn_sinks, sgu_ln_w=v_sgu_ln_w,
               sgu_ln_b=v_sgu_ln_b, sgu_w=v_sgu_w, sgu_b=v_sgu_b, proj_a=v_proj_a, proj_b=v_proj_b, w_out=v_w_out,
               norm2_w=v_norm2_w, ffn_w_gate=v_ffn_w_gate, ffn_w_up=v_ffn_w_up, ffn_conv_w=v_ffn_conv_w,
               ffn_conv_b=v_ffn_conv_b, ffn_w_down=v_ffn_w_down, final_norm_w=v_final_norm_w)
    me = 4 * lax.axis_index("x") + 2 * lax.axis_index("y") + lax.axis_index("c")
    ada_cols = ada_w.shape[2]

    c_all = _all_gather(jnp.broadcast_to(c, (8, D_MODEL)), name="ag_c")[:, 0, :]
    prod = _ada_fwd(c_all, ada_w)
    prod_all = _all_gather(prod, name="ag_mod")
    mine = lax.dynamic_index_in_dim(prod_all, me, axis=1, keepdims=False)
    mod = jnp.stack([mine[:, l * ada_cols:(l + 1) * ada_cols].reshape(-1) for l in range(DEPTH)]) + ada_b

    gathered = _all_gather(_pack_shards(wts), name="ag_weights")
    weights = _unpack_full(gathered)
    for W in weights:
        W["w_in"] = _perm_w_in(W["w_in"])
        W["w_gu"] = jnp.concatenate([W.pop("ffn_w_gate"), W.pop("ffn_w_up")], axis=1)
    conv_cols = ffn_conv_w.shape[2]
    conv_all = _all_gather(_flat_pack([ffn_conv_w], 8), name="ag_conv")
    conv_full = jnp.stack([a.reshape(DEPTH, 3, conv_cols) for a in
                           [conv_all[j].reshape(-1)[:DEPTH * 3 * conv_cols] for j in range(N_DEV)]], axis=2)
    conv_full = conv_full.reshape(DEPTH, 3, FFN_DIM)
    small = {n: wts[n] for n in SMALL}
    small["ffn_conv_w"] = conv_full

    loss_tile, grad_x, wgrads, sg, dmod = _local_step(x[0], mod, positions[0], weights, small, loss_target[0])
    loss = lax.psum(loss_tile[0, 0], ("x", "y", "c"))

    for wg in wgrads:
        wg["w_in"] = _unperm_w_in(wg["w_in"])
    parts = _exchange(_pack_grads(wgrads), name="rs_grads")
    grads = _unpack_shard_grads(_sum8(parts, name="sum_grads"))

    small_names = [n for n in SMALL if n != "ada_b"] + ["ffn_conv_w"]
    small_shapes = [(DEPTH, 6 * D_MODEL)] + [sg[n].shape for n in small_names]
    n_small = sum(int(np.prod(s)) for s in small_shapes)
    rows = -(-n_small // 1024 // 8) * 8
    sm_all = _all_gather(_flat_pack([dmod] + [sg[n] for n in small_names], rows), name="ag_small")
    sm_sum = _flat_unpack(_sum8(sm_all, name="sum_small"), small_shapes)
    grads["ada_b"] = sm_sum[0]
    for n, gsum in zip(small_names, sm_sum[1:]):
        grads[n] = gsum
    grads["ffn_conv_w"] = lax.dynamic_slice_in_dim(grads["ffn_conv_w"], me * conv_cols, conv_cols, axis=2)
    dmod_all = sm_all[:, :DEPTH * 6, :].reshape(N_DEV, DEPTH, 6 * D_MODEL)
    dm_mine = lax.dynamic_slice_in_dim(dmod_all, me * ada_cols, ada_cols, axis=2).transpose(1, 0, 2)
    dm_mine = jnp.pad(dm_mine, ((0, 0), (0, 8), (0, 0)))
    grads["ada_w"] = _ada_bwd(jnp.pad(c_all, ((0, 8), (0, 0))), dm_mine)

    packed_small = [n for n in SMALL]
    pshapes = [wts[n].shape for n in packed_small]
    prow = -(-sum(int(np.prod(s)) for s in pshapes) // 1024 // 8) * 8
    pk = lambda d: _flat_pack([d[n] for n in packed_small], prow)
    d_s, m_s, v_s = _adamw(pk(wts), pk(grads), pk(mom), pk(var), name="adamw_small")
    delta, new_m, new_v = {}, {}, {}
    for n, dd, mm, vv in zip(packed_small, _flat_unpack(d_s, pshapes), _flat_unpack(m_s, pshapes), _flat_unpack(v_s, pshapes)):
        delta[n], new_m[n], new_v[n] = dd, mm, vv
    for n in WEIGHT_ORDER:
        if n not in delta:
            delta[n], new_m[n], new_v[n] = _adam2d(wts[n], grads[n], mom[n], var[n], name=f"adamw_{n}")
    return (loss, grad_x[None], *[grads[n] for n in WEIGHT_ORDER], *[delta[n] for n in WEIGHT_ORDER],
            *[new_m[n] for n in WEIGHT_ORDER], *[new_v[n] for n in WEIGHT_ORDER])
```

```python
import functools

import jax
import jax.numpy as jnp
import numpy as np
from jax import lax
from jax.experimental import pallas as pl
from jax.experimental.pallas import tpu as pltpu

F32 = jnp.float32
BF = jnp.bfloat16

N_DEV = 8
D_MODEL = 1024
DEPTH = 2
N_Q_HEADS = 16
N_KV_HEADS = 2
HEAD_DIM = 64
Q_PER_KV = N_Q_HEADS // N_KV_HEADS
ATTN_BLOCK = 128
ROPE_THETA = 500000.0
ROT_DIM = HEAD_DIM // 4
SGU_WIDTH = 1024
SGU_GROUPS = 8
SGU_CHUNK = 128
FFN_DIM = 2816
NORM_EPS = 1e-6
Q_END = N_Q_HEADS * HEAD_DIM
K_END = Q_END + N_KV_HEADS * HEAD_DIM
V_END = K_END + N_KV_HEADS * HEAD_DIM
Z_END = V_END + 2 * SGU_WIDTH
IN_COLS = Z_END + 2 * D_MODEL
P_Z, P_G, P_Q, P_K, P_V = 0, 2048, 4096, 5120, 5248

ADAM_LR = 0.001
ADAM_B1 = 0.9
ADAM_B2 = 0.999
ADAM_EPS = 1e-08
ADAM_WD = 0.01
ADAM_STEP = 10

VMEM_LIMIT_BYTES = 56 * 1024 * 1024

BIG = ("w_in", "proj_a", "proj_b", "w_out", "ffn_w_gate", "ffn_w_up", "ffn_w_down")
COL_SHARDED = ("w_in", "ffn_w_gate", "ffn_w_up")
BIG_SHAPE = {"w_in": (D_MODEL, IN_COLS), "proj_a": (SGU_WIDTH, D_MODEL), "proj_b": (Q_END, D_MODEL),
             "w_out": (D_MODEL, D_MODEL), "ffn_w_gate": (D_MODEL, FFN_DIM), "ffn_w_up": (D_MODEL, FFN_DIM),
             "ffn_w_down": (FFN_DIM, D_MODEL)}
BIG_ROWS = {n: BIG_SHAPE[n][0] * BIG_SHAPE[n][1] // N_DEV // 1024 for n in BIG}
LAYER_ROWS = sum(BIG_ROWS.values())


def _pcall(body, **kw):
    return pl.pallas_call(body, **kw)


def _params(**kw):
    return pltpu.CompilerParams(vmem_limit_bytes=VMEM_LIMIT_BYTES, **kw)


def _tile(n, cap, unit=128):
    if n <= cap:
        return n
    best = 0
    t = unit
    while t <= cap:
        if n % t == 0:
            best = t
        t += unit
    assert best, (n, cap, unit)
    return best


def _mm(a, b, *, nt, out_dtype, name, res=None, gvec=None, tm=512, tn_cap=1024):
    M, K = a.shape
    N = b.shape[0] if nt else b.shape[1]
    tm = _tile(M, tm, 8)
    tn = _tile(N, tn_cap)
    dn = (((1,), (1,)), ((), ())) if nt else (((1,), (0,)), ((), ()))
    b_spec = pl.BlockSpec((tn, K), lambda i, j: (j, 0)) if nt else pl.BlockSpec((K, tn), lambda i, j: (0, j))
    o_spec = pl.BlockSpec((tm, tn), lambda i, j: (i, j))
    if res is None:
        def body(a_ref, b_ref, o_ref):
            acc = lax.dot_general(a_ref[...].astype(BF), b_ref[...].astype(BF), dn, preferred_element_type=F32)
            o_ref[...] = acc.astype(out_dtype)
        return _pcall(body, name=name, grid=(M // tm, N // tn),
                      in_specs=[pl.BlockSpec((tm, K), lambda i, j: (i, 0)), b_spec], out_specs=o_spec,
                      out_shape=jax.ShapeDtypeStruct((M, N), out_dtype), compiler_params=_params())(a, b)

    def body_res(a_ref, b_ref, r_ref, g_ref, o_ref, acc_ref):
        acc = lax.dot_general(a_ref[...].astype(BF), b_ref[...].astype(BF), dn, preferred_element_type=F32)
        acc_ref[...] = acc
        o_ref[...] = r_ref[...] + g_ref[...] * acc
    return _pcall(body_res, name=name, grid=(M // tm, N // tn),
                  in_specs=[pl.BlockSpec((tm, K), lambda i, j: (i, 0)), b_spec, o_spec,
                            pl.BlockSpec((1, tn), lambda i, j: (0, j))],
                  out_specs=[o_spec, o_spec],
                  out_shape=[jax.ShapeDtypeStruct((M, N), F32), jax.ShapeDtypeStruct((M, N), F32)],
                  compiler_params=_params())(a, b, res, gvec)


def _mm_tn(a, b, *, name, out_dtype=BF, tk=512, tm_cap=1408, tn_cap=1024):
    S, M = a.shape
    N = b.shape[1]
    tk = _tile(S, tk, 8)
    tm = _tile(M, tm_cap)
    tn = _tile(N, tn_cap)
    nk = S // tk

    def body(a_ref, b_ref, o_ref, acc_ref):
        k = pl.program_id(2)

        @pl.when(k == 0)
        def _():
            acc_ref[...] = jnp.zeros_like(acc_ref)
        acc_ref[...] += lax.dot_general(a_ref[...].astype(BF), b_ref[...].astype(BF), (((0,), (0,)), ((), ())),
                                        preferred_element_type=F32)

        @pl.when(k == nk - 1)
        def _():
            o_ref[...] = acc_ref[...].astype(out_dtype)
    return _pcall(body, name=name, grid=(M // tm, N // tn, nk),
                  in_specs=[pl.BlockSpec((tk, tm), lambda i, j, k: (k, i)),
                            pl.BlockSpec((tk, tn), lambda i, j, k: (k, j))],
                  out_specs=pl.BlockSpec((tm, tn), lambda i, j, k: (i, j)),
                  out_shape=jax.ShapeDtypeStruct((M, N), out_dtype), scratch_shapes=[pltpu.VMEM((tm, tn), F32)],
                  compiler_params=_params())(a, b)


def _rms(x, w):
    return x * lax.rsqrt(jnp.mean(x * x, axis=-1, keepdims=True) + NORM_EPS) * w


def _normmod_fn(x, nw, sc, sh):
    return _rms(x, nw) * (1.0 + sc) + sh


def _gelu(x):
    return 0.5 * x * (1.0 + lax.erf(x * (2.0 ** -0.5)))


def _ln_gelu_fn(zv, w, b):
    v = _gelu(zv)
    mu = jnp.mean(v, axis=-1, keepdims=True)
    var = jnp.mean(jnp.square(v - mu), axis=-1, keepdims=True)
    return (v - mu) * lax.rsqrt(var + NORM_EPS) * w + b


def _sigmoid(x):
    return 1.0 / (1.0 + jnp.exp(-x))


def _row_spec(tm, n):
    return pl.BlockSpec((tm, n), lambda i: (i, 0))


def _vec_spec(n):
    return pl.BlockSpec((1, n), lambda i: (0, 0))


def _acc(ref, val):
    @pl.when(pl.program_id(0) == 0)
    def _():
        ref[...] = jnp.zeros_like(ref)
    ref[...] += val


def _normmod_fwd(x, nw, sc, sh, *, name, tm=512):
    S, Dm = x.shape
    tm = _tile(S, tm, 8)

    def body(x_ref, nw_ref, sc_ref, sh_ref, o_ref):
        o_ref[...] = _normmod_fn(x_ref[...], nw_ref[...], sc_ref[...], sh_ref[...]).astype(BF)
    return _pcall(body, name=name, grid=(S // tm,),
                  in_specs=[_row_spec(tm, Dm), _vec_spec(Dm), _vec_spec(Dm), _vec_spec(Dm)],
                  out_specs=_row_spec(tm, Dm), out_shape=jax.ShapeDtypeStruct((S, Dm), BF),
                  compiler_params=_params())(x, nw, sc, sh)


def _normmod_bwd(dh, x, nw, sc, sh, dres, *, name, tm=256):
    S, Dm = x.shape
    tm = _tile(S, tm, 8)

    def body(dh_ref, x_ref, nw_ref, sc_ref, sh_ref, dres_ref, dx_ref, dnw_ref, dsc_ref, dsh_ref):
        _, vjp = jax.vjp(_normmod_fn, x_ref[...], nw_ref[...], sc_ref[...], sh_ref[...])
        dx, dnw, dsc, dsh = vjp(dh_ref[...])
        dx_ref[...] = dres_ref[...] + dx
        _acc(dnw_ref, dnw)
        _acc(dsc_ref, dsc)
        _acc(dsh_ref, dsh)
    vec = jax.ShapeDtypeStruct((1, Dm), F32)
    return _pcall(body, name=name, grid=(S // tm,),
                  in_specs=[_row_spec(tm, Dm), _row_spec(tm, Dm), _vec_spec(Dm), _vec_spec(Dm), _vec_spec(Dm),
                            _row_spec(tm, Dm)],
                  out_specs=[_row_spec(tm, Dm), _vec_spec(Dm), _vec_spec(Dm), _vec_spec(Dm)],
                  out_shape=[jax.ShapeDtypeStruct((S, Dm), F32), vec, vec, vec],
                  compiler_params=_params())(dh, x, nw, sc, sh, dres)


def _scale_reduce(dx, o, g, *, name, tm=512):
    S, Dm = dx.shape
    tm = _tile(S, tm, 8)

    def body(dx_ref, o_ref, g_ref, do_ref, dg_ref):
        dxv = dx_ref[...]
        do_ref[...] = (dxv * g_ref[...]).astype(BF)
        _acc(dg_ref, jnp.sum(dxv * o_ref[...], axis=0, keepdims=True))
    return _pcall(body, name=name, grid=(S // tm,),
                  in_specs=[_row_spec(tm, Dm), _row_spec(tm, Dm), _vec_spec(Dm)],
                  out_specs=[_row_spec(tm, Dm), _vec_spec(Dm)],
                  out_shape=[jax.ShapeDtypeStruct((S, Dm), BF), jax.ShapeDtypeStruct((1, Dm), F32)],
                  compiler_params=_params())(dx, o, g)


def _head(x, fw, target, *, tm=256):
    S, Dm = x.shape
    tm = _tile(S, tm, 8)

    def body(x_ref, fw_ref, t_ref, dx_ref, dfw_ref, loss_ref):
        y, vjp = jax.vjp(_rms, x_ref[...], fw_ref[...])
        err = y - t_ref[...]
        dx, dfw = vjp(err * (1.0 / Dm))
        dx_ref[...] = dx
        _acc(dfw_ref, dfw)
        part = 0.5 * jnp.sum(jnp.mean(err * err, axis=-1, keepdims=True), axis=0, keepdims=True)
        _acc(loss_ref, jnp.broadcast_to(part, (8, 128)))
    return _pcall(body, name="head", grid=(S // tm,),
                  in_specs=[_row_spec(tm, Dm), _vec_spec(Dm), _row_spec(tm, Dm)],
                  out_specs=[_row_spec(tm, Dm), _vec_spec(Dm), pl.BlockSpec((8, 128), lambda i: (0, 0))],
                  out_shape=[jax.ShapeDtypeStruct((S, Dm), F32), jax.ShapeDtypeStruct((1, Dm), F32),
                             jax.ShapeDtypeStruct((8, 128), F32)],
                  compiler_params=_params())(x, fw, target)


def _tril_mask():
    r = lax.broadcasted_iota(jnp.int32, (SGU_CHUNK, SGU_CHUNK), 0)
    c = lax.broadcasted_iota(jnp.int32, (SGU_CHUNK, SGU_CHUNK), 1)
    return c <= r


def _sgu_fwd(proj, lnw, lnb, w, b_t, *, name, tm=256):
    S = proj.shape[0]
    tm = _tile(S, tm, SGU_CHUNK)

    def body(zu_ref, zv_ref, lnw_ref, lnb_ref, w_ref, bt_ref, o_ref):
        u = _gelu(zu_ref[...])
        vn = _ln_gelu_fn(zv_ref[...], lnw_ref[...], lnb_ref[...]).astype(BF)
        mask = _tril_mask()
        for g in range(SGU_GROUPS):
            wm = jnp.where(mask, w_ref[g], 0.0).astype(BF)
            cols = slice(g * 128, (g + 1) * 128)
            for ci in range(tm // SGU_CHUNK):
                rows = slice(ci * SGU_CHUNK, (ci + 1) * SGU_CHUNK)
                f = jnp.dot(wm, vn[rows, cols], preferred_element_type=F32) + bt_ref[:, g:g + 1]
                o_ref[rows, cols] = (u[rows, cols] * f).astype(BF)
    return _pcall(body, name=name, grid=(S // tm,),
                  in_specs=[pl.BlockSpec((tm, SGU_WIDTH), lambda i: (i, 0)), pl.BlockSpec((tm, SGU_WIDTH), lambda i: (i, 1)),
                            _vec_spec(SGU_WIDTH), _vec_spec(SGU_WIDTH),
                            pl.BlockSpec((SGU_GROUPS, 128, 128), lambda i: (0, 0, 0)),
                            pl.BlockSpec((128, SGU_GROUPS), lambda i: (0, 0))],
                  out_specs=_row_spec(tm, SGU_WIDTH), out_shape=jax.ShapeDtypeStruct((S, SGU_WIDTH), BF),
                  compiler_params=_params())(proj, proj, lnw, lnb, w, b_t)


def _sgu_bwd(dy, proj, lnw, lnb, w, b_t, *, name, tm=256):
    S = proj.shape[0]
    tm = _tile(S, tm, SGU_CHUNK)

    def body(dy_ref, zu_ref, zv_ref, lnw_ref, lnb_ref, w_ref, bt_ref, dz_ref, dlnw_ref, dlnb_ref, dw_ref, dbt_ref,
             f_s, dvn_s):
        first = pl.program_id(0) == 0

        @pl.when(first)
        def _():
            dw_ref[...] = jnp.zeros_like(dw_ref)
            dbt_ref[...] = jnp.zeros_like(dbt_ref)
        u, vjp_u = jax.vjp(_gelu, zu_ref[...])
        vn, vjp_v = jax.vjp(_ln_gelu_fn, zv_ref[...], lnw_ref[...], lnb_ref[...])
        vn = vn.astype(BF)
        dy_v = dy_ref[...]
        df = (dy_v * u).astype(BF)
        mask = _tril_mask()
        for g in range(SGU_GROUPS):
            wm = jnp.where(mask, w_ref[g], 0.0).astype(BF)
            cols = slice(g * 128, (g + 1) * 128)
            dwg = jnp.zeros((128, 128), F32)
            dbg = jnp.zeros((128, 1), F32)
            for ci in range(tm // SGU_CHUNK):
                rows = slice(ci * SGU_CHUNK, (ci + 1) * SGU_CHUNK)
                vn_c = vn[rows, cols]
                df_c = df[rows, cols]
                f_s[rows, cols] = jnp.dot(wm, vn_c, preferred_element_type=F32) + bt_ref[:, g:g + 1]
                dvn_s[rows, cols] = lax.dot_general(wm, df_c, (((0,), (0,)), ((), ())), preferred_element_type=F32)
                dwg = dwg + lax.dot_general(df_c, vn_c, (((1,), (1,)), ((), ())), preferred_element_type=F32)
                dbg = dbg + jnp.sum((dy_v[rows, cols] * u[rows, cols]), axis=1, keepdims=True)
            dw_ref[g] += jnp.where(mask, dwg, 0.0)
            dbt_ref[:, g:g + 1] += dbg
        (dzu,) = vjp_u(dy_v * f_s[...])
        dzv, dlnw, dlnb = vjp_v(dvn_s[...])
        dz_ref[:, :SGU_WIDTH] = dzu.astype(BF)
        dz_ref[:, SGU_WIDTH:] = dzv.astype(BF)
        _acc(dlnw_ref, dlnw)
        _acc(dlnb_ref, dlnb)
    vec = jax.ShapeDtypeStruct((1, SGU_WIDTH), F32)
    return _pcall(body, name=name, grid=(S // tm,),
                  in_specs=[_row_spec(tm, SGU_WIDTH),
                            pl.BlockSpec((tm, SGU_WIDTH), lambda i: (i, 0)), pl.BlockSpec((tm, SGU_WIDTH), lambda i: (i, 1)),
                            _vec_spec(SGU_WIDTH), _vec_spec(SGU_WIDTH),
                            pl.BlockSpec((SGU_GROUPS, 128, 128), lambda i: (0, 0, 0)),
                            pl.BlockSpec((128, SGU_GROUPS), lambda i: (0, 0))],
                  out_specs=[_row_spec(tm, 2 * SGU_WIDTH), _vec_spec(SGU_WIDTH), _vec_spec(SGU_WIDTH),
                             pl.BlockSpec((SGU_GROUPS, 128, 128), lambda i: (0, 0, 0)),
                             pl.BlockSpec((128, SGU_GROUPS), lambda i: (0, 0))],
                  out_shape=[jax.ShapeDtypeStruct((S, 2 * SGU_WIDTH), BF), vec, vec,
                             jax.ShapeDtypeStruct((SGU_GROUPS, 128, 128), F32),
                             jax.ShapeDtypeStruct((128, SGU_GROUPS), F32)],
                  scratch_shapes=[pltpu.VMEM((tm, SGU_WIDTH), F32), pltpu.VMEM((tm, SGU_WIDTH), F32)],
                  compiler_params=_params())(dy, proj, proj, lnw, lnb, w, b_t)


def _merge_fwd(a, b, proj, *, name, tm=512):
    S, Dm = a.shape
    tm = _tile(S, tm, 8)
    ga_blk, gb_blk = P_G // Dm, P_G // Dm + 1

    def body(a_ref, b_ref, ga_ref, gb_ref, o_ref):
        o_ref[...] = (_sigmoid(ga_ref[...]) * a_ref[...].astype(F32)
                      + _sigmoid(gb_ref[...]) * b_ref[...].astype(F32)).astype(BF)
    return _pcall(body, name=name, grid=(S // tm,),
                  in_specs=[_row_spec(tm, Dm), _row_spec(tm, Dm), pl.BlockSpec((tm, Dm), lambda i: (i, ga_blk)),
                            pl.BlockSpec((tm, Dm), lambda i: (i, gb_blk))],
                  out_specs=_row_spec(tm, Dm), out_shape=jax.ShapeDtypeStruct((S, Dm), BF),
                  compiler_params=_params())(a, b, proj, proj)


def _merge_bwd(dm, a, b, proj, *, name, tm=512):
    S, Dm = a.shape
    tm = _tile(S, tm, 8)
    ga_blk, gb_blk = P_G // Dm, P_G // Dm + 1

    def body(dm_ref, a_ref, b_ref, ga_ref, gb_ref, da_ref, db_ref, dg_ref):
        dmv = dm_ref[...]
        sa = _sigmoid(ga_ref[...])
        sb = _sigmoid(gb_ref[...])
        da_ref[...] = (dmv * sa).astype(BF)
        db_ref[...] = (dmv * sb).astype(BF)
        dg_ref[:, :Dm] = (dmv * a_ref[...].astype(F32) * sa * (1.0 - sa)).astype(BF)
        dg_ref[:, Dm:] = (dmv * b_ref[...].astype(F32) * sb * (1.0 - sb)).astype(BF)
    return _pcall(body, name=name, grid=(S // tm,),
                  in_specs=[_row_spec(tm, Dm), _row_spec(tm, Dm), _row_spec(tm, Dm),
                            pl.BlockSpec((tm, Dm), lambda i: (i, ga_blk)), pl.BlockSpec((tm, Dm), lambda i: (i, gb_blk))],
                  out_specs=[_row_spec(tm, Dm), _row_spec(tm, Dm), _row_spec(tm, 2 * Dm)],
                  out_shape=[jax.ShapeDtypeStruct((S, Dm), BF), jax.ShapeDtypeStruct((S, Dm), BF),
                             jax.ShapeDtypeStruct((S, 2 * Dm), BF)],
                  compiler_params=_params())(dm, a, b, proj, proj)


def _shift_rows(a, halo, k, up):
    n = a.shape[0]
    r8 = lax.broadcasted_iota(jnp.int32, (8, a.shape[1]), 0)
    if not up:
        rolled = pltpu.roll(a, k, 0)
        patch = jnp.where(r8 < k, pltpu.roll(halo, k, 0), rolled[:8])
        return jnp.concatenate([patch, rolled[8:]], axis=0)
    rolled = pltpu.roll(a, n - k, 0)
    patch = jnp.where(r8 >= 8 - k, pltpu.roll(halo, 8 - k, 0), rolled[n - 8:])
    return jnp.concatenate([rolled[:n - 8], patch], axis=0)


def _conv_taps(a, halo):
    return _shift_rows(a, halo, 2, False), _shift_rows(a, halo, 1, False), a


def _ffn_act_fwd(au, cw, cb, *, name, tm=256):
    S = au.shape[0]
    Fd = FFN_DIM
    tm = _tile(S, tm, 8)
    hb = tm // 8

    def body(a_ref, up_ref, halo_ref, cw_ref, cb_ref, o_ref):
        halo = jnp.where(pl.program_id(0) > 0, halo_ref[...], 0.0)
        t0, t1, t2 = _conv_taps(a_ref[...], halo)
        ac = cb_ref[...] + cw_ref[0:1, :] * t0 + cw_ref[1:2, :] * t1 + cw_ref[2:3, :] * t2
        o_ref[...] = (ac * _sigmoid(ac) * up_ref[...]).astype(BF)
    return _pcall(body, name=name, grid=(S // tm,),
                  in_specs=[pl.BlockSpec((tm, Fd), lambda i: (i, 0)), pl.BlockSpec((tm, Fd), lambda i: (i, 1)),
                            pl.BlockSpec((8, Fd), lambda i: (jnp.maximum(i * hb - 1, 0), 0)),
                            pl.BlockSpec((3, Fd), lambda i: (0, 0)), _vec_spec(Fd)],
                  out_specs=_row_spec(tm, Fd), out_shape=jax.ShapeDtypeStruct((S, Fd), BF),
                  compiler_params=_params())(au, au, au, cw, cb)


def _ffn_act_bwd_a(dhf, au, cw, cb, *, name, tm=256):
    S = au.shape[0]
    Fd = FFN_DIM
    tm = _tile(S, tm, 8)
    hb = tm // 8

    def body(dhf_ref, a_ref, up_ref, halo_ref, cw_ref, cb_ref, dac_ref, dup_ref, dcw_ref, dcb_ref):
        halo = jnp.where(pl.program_id(0) > 0, halo_ref[...], 0.0)
        t0, t1, t2 = _conv_taps(a_ref[...], halo)
        ac = cb_ref[...] + cw_ref[0:1, :] * t0 + cw_ref[1:2, :] * t1 + cw_ref[2:3, :] * t2
        s = _sigmoid(ac)
        dhf_v = dhf_ref[...]
        dup_ref[...] = (dhf_v * ac * s).astype(BF)
        dac = dhf_v * up_ref[...] * (s * (1.0 + ac * (1.0 - s)))
        dac_ref[...] = dac
        _acc(dcb_ref, jnp.sum(dac, axis=0, keepdims=True))
        _acc(dcw_ref, jnp.concatenate([jnp.sum(dac * t0, axis=0, keepdims=True),
                                       jnp.sum(dac * t1, axis=0, keepdims=True),
                                       jnp.sum(dac * t2, axis=0, keepdims=True)], axis=0))
    return _pcall(body, name=name, grid=(S // tm,),
                  in_specs=[_row_spec(tm, Fd), pl.BlockSpec((tm, Fd), lambda i: (i, 0)),
                            pl.BlockSpec((tm, Fd), lambda i: (i, 1)),
                            pl.BlockSpec((8, Fd), lambda i: (jnp.maximum(i * hb - 1, 0), 0)),
                            pl.BlockSpec((3, Fd), lambda i: (0, 0)), _vec_spec(Fd)],
                  out_specs=[_row_spec(tm, Fd), _row_spec(tm, Fd), pl.BlockSpec((3, Fd), lambda i: (0, 0)), _vec_spec(Fd)],
                  out_shape=[jax.ShapeDtypeStruct((S, Fd), F32), jax.ShapeDtypeStruct((S, Fd), BF),
                             jax.ShapeDtypeStruct((3, Fd), F32), jax.ShapeDtypeStruct((1, Fd), F32)],
                  compiler_params=_params())(dhf, au, au, au, cw, cb)


def _ffn_act_bwd_b(dac, cw, *, name, tm=256):
    S, Fd = dac.shape
    tm = _tile(S, tm, 8)
    hb = tm // 8
    last = S // tm - 1

    def body(d_ref, halo_ref, cw_ref, o_ref):
        halo = jnp.where(pl.program_id(0) < last, halo_ref[...], 0.0)
        d = d_ref[...]
        o_ref[...] = (cw_ref[2:3, :] * d + cw_ref[1:2, :] * _shift_rows(d, halo, 1, True)
                      + cw_ref[0:1, :] * _shift_rows(d, halo, 2, True)).astype(BF)
    return _pcall(body, name=name, grid=(S // tm,),
                  in_specs=[_row_spec(tm, Fd), pl.BlockSpec((8, Fd), lambda i: (jnp.minimum((i + 1) * hb, S // 8 - 1), 0)),
                            pl.BlockSpec((3, Fd), lambda i: (0, 0))],
                  out_specs=_row_spec(tm, Fd), out_shape=jax.ShapeDtypeStruct((S, Fd), BF),
                  compiler_params=_params())(dac, dac, cw)


def _rope_tables(pos_col, inv_row, m1_row, m2_row):
    S = pos_col.shape[0]
    tm = _tile(S, 512, 8)

    def body(p_ref, inv_ref, m1_ref, m2_ref, c_ref, s1_ref, s2_ref):
        ang = p_ref[...] * inv_ref[...]
        sn = jnp.sin(ang)
        c_ref[...] = jnp.cos(ang)
        s1_ref[...] = -sn * m1_ref[...]
        s2_ref[...] = sn * m2_ref[...]
    sh = jax.ShapeDtypeStruct((S, 128), F32)
    return _pcall(body, name="rope_tables", grid=(S // tm,),
                  in_specs=[pl.BlockSpec((tm, 1), lambda i: (i, 0)), _vec_spec(128), _vec_spec(128), _vec_spec(128)],
                  out_specs=[_row_spec(tm, 128)] * 3, out_shape=[sh, sh, sh], compiler_params=_params())(
                      pos_col, inv_row, m1_row, m2_row)


def _rope_apply(x, c, s1, s2):
    outs = []
    for j in range(x.shape[1] // 128):
        xj = x[:, j * 128:(j + 1) * 128]
        outs.append(xj * c + pltpu.roll(xj, 120, 1) * s1 + pltpu.roll(xj, 8, 1) * s2)
    return outs[0] if len(outs) == 1 else jnp.concatenate(outs, axis=1)


def _rope_apply_t(d, c, s1, s2):
    outs = []
    for j in range(d.shape[1] // 128):
        dj = d[:, j * 128:(j + 1) * 128]
        outs.append(dj * c + pltpu.roll(dj * s1, 8, 1) + pltpu.roll(dj * s2, 120, 1))
    return outs[0] if len(outs) == 1 else jnp.concatenate(outs, axis=1)


def _rope_fwd(proj, c, s1, s2, *, name, tm=512):
    S = proj.shape[0]
    tm = _tile(S, tm, 8)

    def body(q_ref, k_ref, v_ref, c_ref, s1_ref, s2_ref, qo_ref, kvo_ref):
        cv, s1v, s2v = c_ref[...], s1_ref[...], s2_ref[...]
        qo_ref[...] = _rope_apply(q_ref[...], cv, s1v, s2v).astype(BF)
        kvo_ref[:, :128] = _rope_apply(k_ref[...], cv, s1v, s2v).astype(BF)
        kvo_ref[:, 128:] = v_ref[...].astype(BF)
    return _pcall(body, name=name, grid=(S // tm,),
                  in_specs=[pl.BlockSpec((tm, Q_END), lambda i: (i, P_Q // Q_END)),
                            pl.BlockSpec((tm, 128), lambda i: (i, P_K // 128)),
                            pl.BlockSpec((tm, 128), lambda i: (i, P_V // 128)),
                            _row_spec(tm, 128), _row_spec(tm, 128), _row_spec(tm, 128)],
                  out_specs=[_row_spec(tm, Q_END), _row_spec(tm, 256)],
                  out_shape=[jax.ShapeDtypeStruct((S, Q_END), BF), jax.ShapeDtypeStruct((S, 256), BF)],
                  compiler_params=_params())(proj, proj, proj, c, s1, s2)


def _rope_bwd(dq, dkv, c, s1, s2, *, name, tm=512):
    S = dq.shape[0]
    tm = _tile(S, tm, 8)

    def body(dq_ref, dkv_ref, c_ref, s1_ref, s2_ref, o_ref):
        cv, s1v, s2v = c_ref[...], s1_ref[...], s2_ref[...]
        o_ref[:, :Q_END] = _rope_apply_t(dq_ref[...].astype(F32), cv, s1v, s2v).astype(BF)
        o_ref[:, Q_END:Q_END + 128] = _rope_apply_t(dkv_ref[:, :128].astype(F32), cv, s1v, s2v).astype(BF)
        o_ref[:, Q_END + 128:] = dkv_ref[:, 128:].astype(BF)
    return _pcall(body, name=name, grid=(S // tm,),
                  in_specs=[_row_spec(tm, Q_END), _row_spec(tm, 256), _row_spec(tm, 128), _row_spec(tm, 128),
                            _row_spec(tm, 128)],
                  out_specs=_row_spec(tm, V_END), out_shape=jax.ShapeDtypeStruct((S, V_END), BF),
                  compiler_params=_params())(dq, dkv, c, s1, s2)


def _attn_probs(q, kb, sink, n):
    R = Q_PER_KV * ATTN_BLOCK
    s = lax.dot_general(q, kb, (((1,), (1,)), ((), ())), preferred_element_type=F32) * (HEAD_DIM ** -0.5)
    i = lax.broadcasted_iota(jnp.int32, (R, 2 * ATTN_BLOCK), 0) & (ATTN_BLOCK - 1)
    j = lax.broadcasted_iota(jnp.int32, (R, 2 * ATTN_BLOCK), 1)
    ok = (j > i) & (j <= i + ATTN_BLOCK) & ((n > 0) | (j >= ATTN_BLOCK))
    s = jnp.where(ok, s, -jnp.inf)
    m = jnp.maximum(jnp.max(s, axis=-1, keepdims=True), sink)
    p = jnp.exp(s - m)
    es = jnp.exp(sink - m)
    inv = 1.0 / (jnp.sum(p, axis=-1, keepdims=True) + es)
    return p * inv, es * inv


def _attn_specs(S):
    nb = S // ATTN_BLOCK
    qs = pl.BlockSpec((Q_PER_KV, ATTN_BLOCK, HEAD_DIM), lambda g, n: (g, n, 0))
    cur = pl.BlockSpec((None, ATTN_BLOCK, HEAD_DIM), lambda g, n: (g, n, 0))
    prev = pl.BlockSpec((None, ATTN_BLOCK, HEAD_DIM), lambda g, n: (g, jnp.maximum(n - 1, 0), 0))
    sink = pl.BlockSpec((None, Q_PER_KV * ATTN_BLOCK, 1), lambda g, n: (g, 0, 0))
    return nb, qs, cur, prev, sink


def _attn_fwd(qh, kh, vh, sink_rows, *, name):
    S = qh.shape[1]
    nb, qs, cur, prev, sink = _attn_specs(S)
    R = Q_PER_KV * ATTN_BLOCK

    def body(q_ref, kp_ref, kc_ref, vp_ref, vc_ref, sk_ref, o_ref):
        n = pl.program_id(1)
        q = q_ref[...].reshape(R, HEAD_DIM)
        kb = jnp.concatenate([kp_ref[...], kc_ref[...]], axis=0)
        vb = jnp.concatenate([vp_ref[...], vc_ref[...]], axis=0)
        p, _ = _attn_probs(q, kb, sk_ref[...], n)
        o = jnp.dot(p.astype(BF), vb, preferred_element_type=F32)
        o_ref[...] = o.reshape(Q_PER_KV, ATTN_BLOCK, HEAD_DIM).astype(BF)
    return _pcall(body, name=name, grid=(N_KV_HEADS, nb), in_specs=[qs, prev, cur, prev, cur, sink], out_specs=qs,
                  out_shape=jax.ShapeDtypeStruct(qh.shape, BF), compiler_params=_params())(qh, kh, kh, vh, vh, sink_rows)


def _attn_bwd(do, qh, kh, vh, sink_rows, *, name):
    S = qh.shape[1]
    nb, qs, cur, prev, sink = _attn_specs(S)
    R = Q_PER_KV * ATTN_BLOCK
    full = pl.BlockSpec((None, S, HEAD_DIM), lambda g, n: (g, 0, 0))
    dsk_spec = pl.BlockSpec((None, Q_PER_KV, 128), lambda g, n: (g, 0, 0))

    def body(do_ref, q_ref, kp_ref, kc_ref, vp_ref, vc_ref, sk_ref, dq_ref, dk_ref, dv_ref, dsk_ref):
        n = pl.program_id(1)

        @pl.when(n == 0)
        def _():
            dk_ref[...] = jnp.zeros_like(dk_ref)
            dv_ref[...] = jnp.zeros_like(dv_ref)
            dsk_ref[...] = jnp.zeros_like(dsk_ref)
        q = q_ref[...].reshape(R, HEAD_DIM)
        dov = do_ref[...].reshape(R, HEAD_DIM)
        kb = jnp.concatenate([kp_ref[...], kc_ref[...]], axis=0)
        vb = jnp.concatenate([vp_ref[...], vc_ref[...]], axis=0)
        p, ps = _attn_probs(q, kb, sk_ref[...], n)
        dp = lax.dot_general(dov, vb, (((1,), (1,)), ((), ())), preferred_element_type=F32)
        dd = jnp.sum(p * dp, axis=-1, keepdims=True)
        ds = (p * (dp - dd) * (HEAD_DIM ** -0.5)).astype(BF)
        dq = jnp.dot(ds, kb, preferred_element_type=F32)
        dq_ref[...] = dq.reshape(Q_PER_KV, ATTN_BLOCK, HEAD_DIM).astype(BF)
        dkb = lax.dot_general(ds, q, (((0,), (0,)), ((), ())), preferred_element_type=F32)
        dvb = lax.dot_general(p.astype(BF), dov, (((0,), (0,)), ((), ())), preferred_element_type=F32)
        r0 = pl.multiple_of(n * ATTN_BLOCK, ATTN_BLOCK)
        dk_ref[pl.ds(r0, ATTN_BLOCK), :] += dkb[ATTN_BLOCK:]
        dv_ref[pl.ds(r0, ATTN_BLOCK), :] += dvb[ATTN_BLOCK:]

        @pl.when(n > 0)
        def _():
            rp = pl.multiple_of((n - 1) * ATTN_BLOCK, ATTN_BLOCK)
            dk_ref[pl.ds(rp, ATTN_BLOCK), :] += dkb[:ATTN_BLOCK]
            dv_ref[pl.ds(rp, ATTN_BLOCK), :] += dvb[:ATTN_BLOCK]
        dsr = -(ps * dd)
        sub = lax.broadcasted_iota(jnp.int32, (Q_PER_KV, 128), 0)
        upd = jnp.zeros((Q_PER_KV, 128), F32)
        for h in range(Q_PER_KV):
            upd = jnp.where(sub == h, jnp.sum(dsr[h * ATTN_BLOCK:(h + 1) * ATTN_BLOCK]), upd)
        dsk_ref[...] += upd
    return _pcall(body, name=name, grid=(N_KV_HEADS, nb), in_specs=[qs, qs, prev, cur, prev, cur, sink],
                  out_specs=[qs, full, full, dsk_spec],
                  out_shape=[jax.ShapeDtypeStruct(qh.shape, BF), jax.ShapeDtypeStruct(kh.shape, F32),
                             jax.ShapeDtypeStruct(kh.shape, F32), jax.ShapeDtypeStruct((N_KV_HEADS, Q_PER_KV, 128), F32)],
                  compiler_params=_params())(do, qh, kh, kh, vh, vh, sink_rows)


def _to_heads(x, nh):
    return x.reshape(x.shape[0], nh, HEAD_DIM).transpose(1, 0, 2)


def _from_heads(x):
    return x.transpose(1, 0, 2).reshape(x.shape[1], x.shape[0] * HEAD_DIM)


def _ada_fwd(c_all, ada_w):
    ncol = ada_w.shape[2]

    def body(c_ref, w_ref, o_ref):
        cv = c_ref[...]
        ca = (cv * _sigmoid(cv)).astype(BF)
        for l in range(DEPTH):
            o_ref[:, l * ncol:(l + 1) * ncol] = jnp.dot(ca, w_ref[l].astype(BF), preferred_element_type=F32)
    return _pcall(body, name="ada_fwd", out_shape=jax.ShapeDtypeStruct((N_DEV, DEPTH * ncol), F32),
                  compiler_params=_params())(c_all, ada_w)


def _ada_bwd(c_all, dm):
    ncol = dm.shape[2]

    def body(c_ref, dm_ref, o_ref):
        cv = c_ref[...]
        ca = (cv * _sigmoid(cv)).astype(BF)
        for l in range(DEPTH):
            o_ref[l] = lax.dot_general(ca, dm_ref[l].astype(BF), (((0,), (0,)), ((), ())), preferred_element_type=F32)
    return _pcall(body, name="ada_bwd", out_shape=jax.ShapeDtypeStruct((DEPTH, D_MODEL, ncol), F32),
                  compiler_params=_params())(c_all, dm)


def _adamw(w, g, m, v, *, name):
    R, C = w.shape
    tr = R
    for t in range(8, 513, 8):
        if R % t == 0:
            tr = t
    c1 = 1.0 - ADAM_B1 ** ADAM_STEP
    c2 = 1.0 - ADAM_B2 ** ADAM_STEP

    def body(w_ref, g_ref, m_ref, v_ref, d_ref, mo_ref, vo_ref):
        gv = g_ref[...]
        mn = ADAM_B1 * m_ref[...] + (1.0 - ADAM_B1) * gv
        vn = ADAM_B2 * v_ref[...] + (1.0 - ADAM_B2) * (gv * gv)
        mo_ref[...] = mn
        vo_ref[...] = vn
        d_ref[...] = -ADAM_LR * ((mn / c1) / (jnp.sqrt(vn / c2) + ADAM_EPS) + ADAM_WD * w_ref[...])
    spec = pl.BlockSpec((tr, C), lambda i: (i, 0))
    sh = jax.ShapeDtypeStruct((R, C), F32)
    return _pcall(body, name=name, grid=(R // tr,), in_specs=[spec] * 4, out_specs=[spec] * 3, out_shape=[sh, sh, sh],
                  compiler_params=_params())(w, g, m, v)


def _sum8(parts, *, name):
    _, R, C = parts.shape
    tr = R
    for t in range(16, 257, 16):
        if R % t == 0:
            tr = t

    def body(p_ref, o_ref):
        acc = p_ref[0].astype(F32)
        for k in range(1, N_DEV):
            acc = acc + p_ref[k].astype(F32)
        o_ref[...] = acc
    return _pcall(body, name=name, grid=(R // tr,), in_specs=[pl.BlockSpec((N_DEV, tr, C), lambda i: (0, i, 0))],
                  out_specs=pl.BlockSpec((tr, C), lambda i: (i, 0)), out_shape=jax.ShapeDtypeStruct((R, C), F32),
                  compiler_params=_params())(parts)


MESH_ID = pl.DeviceIdType.MESH
ANY = pl.BlockSpec(memory_space=pl.ANY)


def _all_gather(x, *, name):
    R, C = x.shape

    def body(x_ref, out_ref, send_sems, recv_sems, local_sem):
        mx, my, mc = lax.axis_index("x"), lax.axis_index("y"), lax.axis_index("c")
        me, sibling = (mx, my, mc), (mx, my, 1 - mc)
        chips = [(1 - mx, my), (mx, 1 - my), (1 - mx, 1 - my)]

        def blk(px, py, pc):
            return out_ref.at[4 * px + 2 * py + pc]

        def copy(k, block, to, src=None):
            return pltpu.make_async_remote_copy(
                src_ref=blk(*block) if src is None else src, dst_ref=blk(*block),
                send_sem=send_sems.at[k], recv_sem=recv_sems.at[k], device_id=to, device_id_type=MESH_ID)

        mine = pltpu.make_async_copy(x_ref, blk(*me), local_sem)
        mine.start()
        first = [copy(0, me, sibling, src=x_ref)]
        first += [copy(1 + j, me, (*chip, mc), src=x_ref) for j, chip in enumerate(chips)]
        for cp in first:
            cp.start()
        passed = [copy(4 + j, (*chip, mc), sibling) for j, chip in enumerate(chips)]
        for j, chip in enumerate(chips):
            copy(1 + j, (*chip, mc), me).wait_recv()
            passed[j].start()
        copy(0, sibling, me).wait_recv()
        for j, chip in enumerate(chips):
            copy(4 + j, (*chip, 1 - mc), me).wait_recv()
        for cp in first + passed:
            cp.wait_send()
        mine.wait()
    return _pcall(body, name=name, in_specs=[ANY], out_specs=ANY,
                  out_shape=jax.ShapeDtypeStruct((N_DEV, R, C), x.dtype),
                  scratch_shapes=[pltpu.SemaphoreType.DMA((7,)), pltpu.SemaphoreType.DMA((7,)), pltpu.SemaphoreType.DMA],
                  compiler_params=pltpu.CompilerParams(has_side_effects=True))(x)


HBM_SPEC = pl.BlockSpec(memory_space=pltpu.HBM)
SEM_SPEC = pl.BlockSpec(memory_space=pltpu.SEMAPHORE)
DATAFLOW = pltpu.SideEffectType.DATAFLOW_SIDE_EFFECTING


def _coords():
    return lax.axis_index("x"), lax.axis_index("y"), lax.axis_index("c")


def _other_chips(mx, my):
    return [(1 - mx, my), (mx, 1 - my), (1 - mx, 1 - my)]


def _plan_gather_ici(refs, send, recv):
    src, land = refs
    mx, my, mc = _coords()
    return [pltpu.make_async_remote_copy(src_ref=src, dst_ref=land.at[mc, 2 * mx + my], send_sem=send[j], recv_sem=recv[j],
                                         device_id=(px, py, mc), device_id_type=MESH_ID)
            for j, (px, py) in enumerate(_other_chips(mx, my))]


def _plan_gather_d2d(refs, send, recv):
    (land,) = refs
    mx, my, mc = _coords()
    return [pltpu.make_async_remote_copy(src_ref=land.at[mc], dst_ref=land.at[mc], send_sem=send[0], recv_sem=recv[0],
                                         device_id=(mx, my, 1 - mc), device_id_type=MESH_ID)]


def _plan_reduce_d2d(refs, send, recv):
    g, land = refs
    mx, my, mc = _coords()
    return [pltpu.make_async_remote_copy(src_ref=g.at[1 - mc], dst_ref=land, send_sem=send[0], recv_sem=recv[0],
                                         device_id=(mx, my, 1 - mc), device_id_type=MESH_ID)]


def _plan_reduce_ici(refs, send, recv):
    h, land = refs
    mx, my, mc = _coords()
    return [pltpu.make_async_remote_copy(src_ref=h.at[2 * px + py], dst_ref=land.at[j], send_sem=send[j], recv_sem=recv[j],
                                         device_id=(px, py, mc), device_id_type=MESH_ID)
            for j, (px, py) in enumerate(_other_chips(mx, my))]


def _rdma_start(bufs, n, plan, *, name):
    nb = len(bufs)

    def body(*refs):
        ins, send, recv = refs[:nb], refs[nb:nb + n], refs[nb + n:nb + 2 * n]
        token = refs[-1]
        for cp in plan(ins, send, recv):
            cp.start()
        token[...] = jnp.zeros_like(token)
    out = _pcall(body, name=name,
                 out_shape=tuple([pltpu.SemaphoreType.DMA(())] * (2 * n) + [pltpu.HBM(b.shape, b.dtype) for b in bufs]
                                 + [jax.ShapeDtypeStruct((8, 128), F32)]),
                 in_specs=tuple([HBM_SPEC] * nb),
                 out_specs=tuple([SEM_SPEC] * (2 * n) + [HBM_SPEC] * nb + [pl.BlockSpec(memory_space=pltpu.VMEM)]),
                 input_output_aliases={i: 2 * n + i for i in range(nb)},
                 compiler_params=pltpu.CompilerParams(has_side_effects=DATAFLOW))(
                     *[pltpu.with_memory_space_constraint(b, pltpu.HBM) for b in bufs])
    return list(out[:2 * n]), list(out[2 * n:2 * n + nb]), out[-1]


def _rdma_wait(sems, bufs, n, plan, after, *, name):
    nb = len(bufs)

    def body(*refs):
        ins, send, recv = refs[:nb], refs[nb:nb + n], refs[nb + n:nb + 2 * n]
        for cp in plan(ins, send, recv):
            cp.wait_send()
            cp.wait_recv()
    out = _pcall(body, name=name, out_shape=tuple(pltpu.HBM(b.shape, b.dtype) for b in bufs),
                 in_specs=tuple([HBM_SPEC] * nb + [SEM_SPEC] * (2 * n) + [ANY]), out_specs=tuple([HBM_SPEC] * nb),
                 input_output_aliases={i: i for i in range(nb)},
                 compiler_params=pltpu.CompilerParams(has_side_effects=DATAFLOW))(*bufs, *sems, after)
    return list(out)


def _tie(x, token):
    return lax.optimization_barrier((x, token))[0]


def _sum_pair(g, land, cidx, *, name):
    _, nchip, R, C = g.shape
    tr = _tile(R, 512, 16)

    def body(c_ref, g_ref, l_ref, o_ref):
        o_ref[...] = (g_ref[...].astype(F32) + l_ref[...].astype(F32)).astype(BF)
    grid_spec = pltpu.PrefetchScalarGridSpec(
        num_scalar_prefetch=1, grid=(nchip, R // tr),
        in_specs=[pl.BlockSpec((None, None, tr, C), lambda p, i, c_ref: (c_ref[0], p, i, 0)),
                  pl.BlockSpec((None, tr, C), lambda p, i, c_ref: (p, i, 0))],
        out_specs=pl.BlockSpec((None, tr, C), lambda p, i, c_ref: (p, i, 0)))
    return _pcall(body, name=name, grid_spec=grid_spec, out_shape=jax.ShapeDtypeStruct((nchip, R, C), BF),
                  compiler_params=_params())(cidx, g, land)


def _sum_chips(h, land, chipidx, *, name):
    _, R, C = h.shape
    tr = _tile(R, 512, 16)

    def body(c_ref, h_ref, l_ref, o_ref):
        acc = h_ref[...].astype(F32)
        for j in range(3):
            acc = acc + l_ref[j].astype(F32)
        o_ref[...] = acc
    grid_spec = pltpu.PrefetchScalarGridSpec(
        num_scalar_prefetch=1, grid=(R // tr,),
        in_specs=[pl.BlockSpec((None, tr, C), lambda i, c_ref: (c_ref[0], i, 0)),
                  pl.BlockSpec((3, tr, C), lambda i, c_ref: (0, i, 0))],
        out_specs=pl.BlockSpec((tr, C), lambda i, c_ref: (i, 0)))
    return _pcall(body, name=name, grid_spec=grid_spec, out_shape=jax.ShapeDtypeStruct((R, C), F32),
                  compiler_params=_params())(chipidx, h, land)


ROW_OFF = {}
_r = 0
for _n in BIG:
    ROW_OFF[_n] = _r
    _r += BIG_ROWS[_n]


def _pack_shards(shards, l):
    return jnp.concatenate([(shards[n][l].T if n in COL_SHARDED else shards[n][l]).astype(BF) for n in BIG], axis=0)


def _unpack_weights(full8):
    def whole(n):
        return full8[:, ROW_OFF[n]:ROW_OFF[n] + BIG_ROWS[n], :].reshape(N_DEV * BIG_ROWS[n], 1024)
    wt_in = whole("w_in")
    return {"wt_in": jnp.concatenate([wt_in[V_END:], wt_in[:V_END]], axis=0),
            "proj_a": whole("proj_a"), "proj_b": whole("proj_b"), "w_out": whole("w_out"),
            "wt_gu": jnp.concatenate([whole("ffn_w_gate"), whole("ffn_w_up")], axis=0), "w_down": whole("ffn_w_down")}


def _pack_grads(wg):
    dwt_in = jnp.concatenate([wg["wt_in"][P_Q:], wg["wt_in"][:P_Q]], axis=0)
    parts = [dwt_in, wg["proj_a"], wg["proj_b"], wg["w_out"], wg["wt_gu"][:FFN_DIM], wg["wt_gu"][FFN_DIM:], wg["w_down"]]
    blocks = jnp.concatenate([p.reshape(N_DEV, BIG_ROWS[n], 1024) for n, p in zip(BIG, parts)], axis=1)
    return blocks.reshape(4, 2, LAYER_ROWS, 1024).transpose(1, 0, 2, 3)


def _unpack_shard_grads(gs):
    out = {}
    for n in BIG:
        blk = gs[ROW_OFF[n]:ROW_OFF[n] + BIG_ROWS[n]]
        out[n] = blk.T if n in COL_SHARDED else blk
    return out


def _rope_setup(positions):
    S = positions.shape[0]
    inv = ROPE_THETA ** (-jnp.arange(0, ROT_DIM, 2, dtype=F32) / ROT_DIM)
    lane = np.arange(128) % HEAD_DIM
    half = ROT_DIM // 2
    inv_row = jnp.where(lane < ROT_DIM, jnp.tile(inv, 128 // half), 0.0)[None, :].astype(F32)
    m1_row = jnp.asarray((lane < half).astype(np.float32))[None, :]
    m2_row = jnp.asarray(((lane >= half) & (lane < ROT_DIM)).astype(np.float32))[None, :]
    return _rope_tables(positions.astype(F32).reshape(S, 1), inv_row, m1_row, m2_row)


def _layer_fwd(l, x, mod_l, W, small, rope, tie=None, mid=None):
    rc, rs1, rs2 = rope
    sh1, sc1, g1, sh2, sc2, g2 = [mod_l[i * D_MODEL:(i + 1) * D_MODEL][None, :] for i in range(6)]
    nw1, nw2 = small["norm1_w"][l][None, :], small["norm2_w"][l][None, :]
    h = _normmod_fwd(x, nw1, sc1, sh1, name=f"normmod1_fwd{l}")
    if tie is not None:
        h = _tie(h, tie)
    proj = _mm(h, W["wt_in"], nt=True, out_dtype=F32, name=f"mm_in{l}", tn_cap=768)
    q_r, kv_r = _rope_fwd(proj, rc, rs1, rs2, name=f"rope_fwd{l}")
    qh, kh, vh = _to_heads(q_r, N_Q_HEADS), _to_heads(kv_r[:, :128], N_KV_HEADS), _to_heads(kv_r[:, 128:], N_KV_HEADS)
    sink_rows = jnp.repeat(small["attn_sinks"][l].reshape(N_KV_HEADS, Q_PER_KV), ATTN_BLOCK, axis=1)[..., None]
    y_attn = _from_heads(_attn_fwd(qh, kh, vh, sink_rows, name=f"attn_fwd{l}"))
    lnw, lnb = small["sgu_ln_w"][l][None, :], small["sgu_ln_b"][l][None, :]
    sgu_bt = small["sgu_b"][l].T
    y_sgu = _sgu_fwd(proj, lnw, lnb, small["sgu_w"][l], sgu_bt, name=f"sgu_fwd{l}")
    if mid is not None:
        y_sgu = _tie(y_sgu, mid(y_sgu))
    a_br = _mm(y_sgu, W["proj_a"], nt=False, out_dtype=BF, name=f"mm_pa{l}")
    b_br = _mm(y_attn, W["proj_b"], nt=False, out_dtype=BF, name=f"mm_pb{l}")
    merged = _merge_fwd(a_br, b_br, proj, name=f"merge_fwd{l}")
    x1, o1 = _mm(merged, W["w_out"], nt=False, out_dtype=F32, name=f"mm_out{l}", res=x, gvec=g1)
    h2 = _normmod_fwd(x1, nw2, sc2, sh2, name=f"normmod2_fwd{l}")
    au = _mm(h2, W["wt_gu"], nt=True, out_dtype=F32, name=f"mm_gu{l}", tn_cap=1408)
    cw, cb = small["ffn_conv_w"][l], small["ffn_conv_b"][l][None, :]
    hf = _ffn_act_fwd(au, cw, cb, name=f"ffn_act_fwd{l}")
    x2, o2 = _mm(hf, W["w_down"], nt=False, out_dtype=F32, name=f"mm_down{l}", res=x1, gvec=g2)
    saved = dict(x=x, h=h, proj=proj, qh=qh, kh=kh, vh=vh, sink_rows=sink_rows, y_attn=y_attn, y_sgu=y_sgu,
                 a_br=a_br, b_br=b_br, merged=merged, x1=x1, o1=o1, h2=h2, au=au, hf=hf, o2=o2)
    return x2, saved


def _layer_bwd(l, dx, mod_l, W, small, rope, sv, tie=None, mid=None):
    rc, rs1, rs2 = rope
    sh1, sc1, g1, sh2, sc2, g2 = [mod_l[i * D_MODEL:(i + 1) * D_MODEL][None, :] for i in range(6)]
    nw1, nw2 = small["norm1_w"][l][None, :], small["norm2_w"][l][None, :]
    cw, cb = small["ffn_conv_w"][l], small["ffn_conv_b"][l][None, :]
    lnw, lnb = small["sgu_ln_w"][l][None, :], small["sgu_ln_b"][l][None, :]
    sgu_bt = small["sgu_b"][l].T
    wg = {}
    if tie is not None:
        dx = _tie(dx, tie)
    do2, dg2 = _scale_reduce(dx, sv["o2"], g2, name=f"scale2_{l}")
    dhf = _mm(do2, W["w_down"], nt=True, out_dtype=F32, name=f"mm_down_dx{l}", tn_cap=1408)
    wg["w_down"] = _mm_tn(sv["hf"], do2, name=f"mm_down_dw{l}")
    dac, dup, dcw, dcb = _ffn_act_bwd_a(dhf, sv["au"], cw, cb, name=f"ffn_act_bwd_a{l}")
    da = _ffn_act_bwd_b(dac, cw, name=f"ffn_act_bwd_b{l}")
    dau = jnp.concatenate([da, dup], axis=1)
    if mid is not None:
        dau = _tie(dau, mid(dau))
    dh2 = _mm(dau, W["wt_gu"], nt=False, out_dtype=F32, name=f"mm_gu_dx{l}")
    wg["wt_gu"] = _mm_tn(dau, sv["h2"], name=f"mm_gu_dw{l}")
    dx1, dnw2, dsc2, dsh2 = _normmod_bwd(dh2, sv["x1"], nw2, sc2, sh2, dx, name=f"normmod2_bwd{l}")
    do1, dg1 = _scale_reduce(dx1, sv["o1"], g1, name=f"scale1_{l}")
    dmerged = _mm(do1, W["w_out"], nt=True, out_dtype=F32, name=f"mm_out_dx{l}")
    wg["w_out"] = _mm_tn(sv["merged"], do1, name=f"mm_out_dw{l}")
    d_a, d_b, dgates = _merge_bwd(dmerged, sv["a_br"], sv["b_br"], sv["proj"], name=f"merge_bwd{l}")
    dysgu = _mm(d_a, W["proj_a"], nt=True, out_dtype=F32, name=f"mm_pa_dx{l}")
    dyattn = _mm(d_b, W["proj_b"], nt=True, out_dtype=BF, name=f"mm_pb_dx{l}")
    wg["proj_a"] = _mm_tn(sv["y_sgu"], d_a, name=f"mm_pa_dw{l}")
    wg["proj_b"] = _mm_tn(sv["y_attn"], d_b, name=f"mm_pb_dw{l}")
    dz, dlnw, dlnb, dsguw, dsgubt = _sgu_bwd(dysgu, sv["proj"], lnw, lnb, small["sgu_w"][l], sgu_bt, name=f"sgu_bwd{l}")
    dqh, dkh, dvh, dsk = _attn_bwd(_to_heads(dyattn, N_Q_HEADS), sv["qh"], sv["kh"], sv["vh"], sv["sink_rows"],
                                   name=f"attn_bwd{l}")
    dkv = jnp.concatenate([_from_heads(dkh), _from_heads(dvh)], axis=1)
    dqkv = _rope_bwd(_from_heads(dqh), dkv, rc, rs1, rs2, name=f"rope_bwd{l}")
    dproj = jnp.concatenate([dz, dgates, dqkv], axis=1)
    dh = _mm(dproj, W["wt_in"], nt=False, out_dtype=F32, name=f"mm_in_dx{l}")
    wg["wt_in"] = _mm_tn(dproj, sv["h"], name=f"mm_in_dw{l}")
    dx0, dnw1, dsc1, dsh1 = _normmod_bwd(dh, sv["x"], nw1, sc1, sh1, dx1, name=f"normmod1_bwd{l}")
    dmod = jnp.concatenate([dsh1, dsc1, dg1, dsh2, dsc2, dg2], axis=1)[0]
    sg = {"norm1_w": dnw1[0], "norm2_w": dnw2[0], "attn_sinks": dsk[:, :, 0].reshape(N_Q_HEADS),
          "sgu_ln_w": dlnw[0], "sgu_ln_b": dlnb[0], "sgu_w": dsguw, "sgu_b": dsgubt.T,
          "ffn_conv_w": dcw, "ffn_conv_b": dcb[0]}
    return dx0, wg, sg, dmod


SMALL = ("ada_b", "norm1_w", "attn_sinks", "sgu_ln_w", "sgu_ln_b", "sgu_w", "sgu_b", "norm2_w", "ffn_conv_b", "final_norm_w")
WEIGHT_ORDER = ("ada_w", "ada_b", "norm1_w", "w_in", "attn_sinks", "sgu_ln_w", "sgu_ln_b", "sgu_w", "sgu_b", "proj_a", "proj_b",
                "w_out", "norm2_w", "ffn_w_gate", "ffn_w_up", "ffn_conv_w", "ffn_conv_b", "ffn_w_down", "final_norm_w")


def _flat_pack(arrs, rows):
    flat = jnp.concatenate([a.reshape(-1) for a in arrs])
    return jnp.pad(flat, (0, rows * 1024 - flat.shape[0])).reshape(rows, 1024)


def _flat_unpack(buf, shapes):
    flat = buf.reshape(-1)
    out, o = [], 0
    for s in shapes:
        n = int(np.prod(s))
        out.append(flat[o:o + n].reshape(s))
        o += n
    return out


def _adam2d(w, g, m, v, *, name):
    shp = w.shape
    r2 = (int(np.prod(shp[:-1])), shp[-1]) if len(shp) > 1 else (1, shp[0])
    d, mn, vn = _adamw(w.reshape(r2), g.reshape(r2), m.reshape(r2), v.reshape(r2), name=name)
    return d.reshape(shp), mn.reshape(shp), vn.reshape(shp)


def kernel(x, c, positions, ada_w, ada_b, norm1_w, w_in, attn_sinks, sgu_ln_w, sgu_ln_b, sgu_w, sgu_b, proj_a, proj_b, w_out, norm2_w, ffn_w_gate, ffn_w_up, ffn_conv_w, ffn_conv_b, ffn_w_down, final_norm_w, loss_target, m_ada_w, m_ada_b, m_norm1_w, m_w_in, m_attn_sinks, m_sgu_ln_w, m_sgu_ln_b, m_sgu_w, m_sgu_b, m_proj_a, m_proj_b, m_w_out, m_norm2_w, m_ffn_w_gate, m_ffn_w_up, m_ffn_conv_w, m_ffn_conv_b, m_ffn_w_down, m_final_norm_w, v_ada_w, v_ada_b, v_norm1_w, v_w_in, v_attn_sinks, v_sgu_ln_w, v_sgu_ln_b, v_sgu_w, v_sgu_b, v_proj_a, v_proj_b, v_w_out, v_norm2_w, v_ffn_w_gate, v_ffn_w_up, v_ffn_conv_w, v_ffn_conv_b, v_ffn_w_down, v_final_norm_w):
    wts = dict(ada_w=ada_w, ada_b=ada_b, norm1_w=norm1_w, w_in=w_in, attn_sinks=attn_sinks, sgu_ln_w=sgu_ln_w,
               sgu_ln_b=sgu_ln_b, sgu_w=sgu_w, sgu_b=sgu_b, proj_a=proj_a, proj_b=proj_b, w_out=w_out, norm2_w=norm2_w,
               ffn_w_gate=ffn_w_gate, ffn_w_up=ffn_w_up, ffn_conv_w=ffn_conv_w, ffn_conv_b=ffn_conv_b,
               ffn_w_down=ffn_w_down, final_norm_w=final_norm_w)
    mom = dict(ada_w=m_ada_w, ada_b=m_ada_b, norm1_w=m_norm1_w, w_in=m_w_in, attn_sinks=m_attn_sinks, sgu_ln_w=m_sgu_ln_w,
               sgu_ln_b=m_sgu_ln_b, sgu_w=m_sgu_w, sgu_b=m_sgu_b, proj_a=m_proj_a, proj_b=m_proj_b, w_out=m_w_out,
               norm2_w=m_norm2_w, ffn_w_gate=m_ffn_w_gate, ffn_w_up=m_ffn_w_up, ffn_conv_w=m_ffn_conv_w,
               ffn_conv_b=m_ffn_conv_b, ffn_w_down=m_ffn_w_down, final_norm_w=m_final_norm_w)
    var = dict(ada_w=v_ada_w, ada_b=v_ada_b, norm1_w=v_norm1_w, w_in=v_w_in, attn_sinks=v_attn_sinks, sgu_ln_w=v_sgu_ln_w,
               sgu_ln_b=v_sgu_ln_b, sgu_w=v_sgu_w, sgu_b=v_sgu_b, proj_a=v_proj_a, proj_b=v_proj_b, w_out=v_w_out,
               norm2_w=v_norm2_w, ffn_w_gate=v_ffn_w_gate, ffn_w_up=v_ffn_w_up, ffn_conv_w=v_ffn_conv_w,
               ffn_conv_b=v_ffn_conv_b, ffn_w_down=v_ffn_w_down, final_norm_w=v_final_norm_w)
    me = 4 * lax.axis_index("x") + 2 * lax.axis_index("y") + lax.axis_index("c")
    ada_cols = ada_w.shape[2]

    c_all = _all_gather(jnp.broadcast_to(c, (8, D_MODEL)), name="ag_c")[:, 0, :]
    prod = _ada_fwd(c_all, ada_w)
    prod_all = _all_gather(prod, name="ag_mod")
    mine = lax.dynamic_index_in_dim(prod_all, me, axis=1, keepdims=False)
    mod = jnp.stack([mine[:, l * ada_cols:(l + 1) * ada_cols].reshape(-1) for l in range(DEPTH)]) + ada_b

    conv_cols = ffn_conv_w.shape[2]
    conv_all = _all_gather(_flat_pack([ffn_conv_w], 8), name="ag_conv")
    conv_full = jnp.stack([a.reshape(DEPTH, 3, conv_cols) for a in
                           [conv_all[j].reshape(-1)[:DEPTH * 3 * conv_cols] for j in range(N_DEV)]], axis=2)
    conv_full = conv_full.reshape(DEPTH, 3, FFN_DIM)
    small = {n: wts[n] for n in SMALL}
    small["ffn_conv_w"] = conv_full

    mx, my, mc = _coords()
    cidx = jnp.reshape(mc, (1,)).astype(jnp.int32)
    chipidx = jnp.reshape(2 * mx + my, (1,)).astype(jnp.int32)
    packed = [_pack_shards(wts, l) for l in range(DEPTH)]
    W0 = _unpack_weights(_all_gather(packed[0], name="ag_w0"))
    land = lax.dynamic_update_slice(jnp.zeros((2, 4, LAYER_ROWS, 1024), BF), packed[1][None, None],
                                    (mc, 2 * mx + my, 0, 0))
    sems, (src_t, land), tok = _rdma_start([packed[1], land], 3, _plan_gather_ici, name="ag_w1_ici_start")
    gather = {}

    def fwd_mid(after):
        _, land_m = _rdma_wait(sems, [src_t, land], 3, _plan_gather_ici, after, name="ag_w1_ici_wait")
        gather["sems"], (gather["land"],), tok2 = _rdma_start([land_m], 1, _plan_gather_d2d, name="ag_w1_d2d_start")
        return tok2

    rope = _rope_setup(positions[0])
    x0 = x[0]
    x1, sv0 = _layer_fwd(0, x0, mod[0], W0, small, rope, tie=tok, mid=fwd_mid)
    (land,) = _rdma_wait(gather["sems"], [gather["land"]], 1, _plan_gather_d2d, x1, name="ag_w1_d2d_wait")
    W1 = _unpack_weights(land.transpose(1, 0, 2, 3).reshape(N_DEV, LAYER_ROWS, 1024))
    x2, sv1 = _layer_fwd(1, x1, mod[1], W1, small, rope)
    dx2, dfw, loss_tile = _head(x2, final_norm_w[None, :], loss_target[0])
    loss = lax.psum(loss_tile[0, 0], ("x", "y", "c"))

    dx1, wg1, sg1, dmod1 = _layer_bwd(1, dx2, mod[1], W1, small, rope, sv1)
    red = {}

    def reduce_start(wg, l):
        empty = jnp.zeros((4, LAYER_ROWS, 1024), BF)
        red["s1"], red["b1"], t = _rdma_start([_pack_grads(wg), empty], 1, _plan_reduce_d2d, name=f"rs{l}_d2d_start")
        return t

    def reduce_mid(after, l):
        g_t, land_a = _rdma_wait(red["s1"], red["b1"], 1, _plan_reduce_d2d, after, name=f"rs{l}_d2d_wait")
        h = _sum_pair(g_t, land_a, cidx, name=f"rs{l}_sum_pair")
        red["s2"], red["b2"], t = _rdma_start([h, jnp.zeros((3, LAYER_ROWS, 1024), BF)], 3, _plan_reduce_ici,
                                              name=f"rs{l}_ici_start")
        return t

    def reduce_end(after, l):
        h_t, land_b = _rdma_wait(red["s2"], red["b2"], 3, _plan_reduce_ici, after, name=f"rs{l}_ici_wait")
        return _unpack_shard_grads(_sum_chips(h_t, land_b, chipidx, name=f"rs{l}_sum_chips"))

    tok = reduce_start(wg1, 1)
    grad_x, wg0, sg0, dmod0 = _layer_bwd(0, dx1, mod[0], W0, small, rope, sv0, tie=tok, mid=lambda a: reduce_mid(a, 1))
    shard1 = reduce_end(grad_x, 1)
    tok = reduce_start(wg0, 0)
    tok = reduce_mid(tok, 0)
    shard0 = reduce_end(tok, 0)
    grads = {n: jnp.stack([shard0[n], shard1[n]]) for n in BIG}
    sg = {n: jnp.stack([sg0[n], sg1[n]]) for n in sg0}
    sg["final_norm_w"] = dfw[0]
    dmod = jnp.stack([dmod0, dmod1])

    small_names = [n for n in SMALL if n != "ada_b"] + ["ffn_conv_w"]
    small_shapes = [(DEPTH, 6 * D_MODEL)] + [sg[n].shape for n in small_names]
    n_small = sum(int(np.prod(s)) for s in small_shapes)
    rows = -(-n_small // 1024 // 8) * 8
    sm_all = _all_gather(_flat_pack([dmod] + [sg[n] for n in small_names], rows), name="ag_small")
    sm_sum = _flat_unpack(_sum8(sm_all, name="sum_small"), small_shapes)
    grads["ada_b"] = sm_sum[0]
    for n, gsum in zip(small_names, sm_sum[1:]):
        grads[n] = gsum
    grads["ffn_conv_w"] = lax.dynamic_slice_in_dim(grads["ffn_conv_w"], me * conv_cols, conv_cols, axis=2)
    dmod_all = sm_all[:, :DEPTH * 6, :].reshape(N_DEV, DEPTH, 6 * D_MODEL)
    dm_mine = lax.dynamic_slice_in_dim(dmod_all, me * ada_cols, ada_cols, axis=2).transpose(1, 0, 2)
    dm_mine = jnp.pad(dm_mine, ((0, 0), (0, 8), (0, 0)))
    grads["ada_w"] = _ada_bwd(jnp.pad(c_all, ((0, 8), (0, 0))), dm_mine)

    packed_small = [n for n in SMALL]
    pshapes = [wts[n].shape for n in packed_small]
    prow = -(-sum(int(np.prod(s)) for s in pshapes) // 1024 // 8) * 8
    pk = lambda d: _flat_pack([d[n] for n in packed_small], prow)
    d_s, m_s, v_s = _adamw(pk(wts), pk(grads), pk(mom), pk(var), name="adamw_small")
    delta, new_m, new_v = {}, {}, {}
    for n, dd, mm, vv in zip(packed_small, _flat_unpack(d_s, pshapes), _flat_unpack(m_s, pshapes), _flat_unpack(v_s, pshapes)):
        delta[n], new_m[n], new_v[n] = dd, mm, vv
    for n in WEIGHT_ORDER:
        if n not in delta:
            delta[n], new_m[n], new_v[n] = _adam2d(wts[n], grads[n], mom[n], var[n], name=f"adamw_{n}")
    return (loss, grad_x[None], *[grads[n] for n in WEIGHT_ORDER], *[delta[n] for n in WEIGHT_ORDER],
            *[new_m[n] for n in WEIGHT_ORDER], *[new_v[n] for n in WEIGHT_ORDER])
```

```python
import functools

import jax
import jax.numpy as jnp
import numpy as np
from jax import lax
from jax.experimental import pallas as pl
from jax.experimental.pallas import tpu as pltpu

F32 = jnp.float32
BF = jnp.bfloat16

N_DEV = 8
D_MODEL = 1024
DEPTH = 2
N_Q_HEADS = 16
N_KV_HEADS = 2
HEAD_DIM = 64
Q_PER_KV = N_Q_HEADS // N_KV_HEADS
ATTN_BLOCK = 128
ROPE_THETA = 500000.0
ROT_DIM = HEAD_DIM // 4
SGU_WIDTH = 1024
SGU_GROUPS = 8
SGU_CHUNK = 128
FFN_DIM = 2816
NORM_EPS = 1e-6
Q_END = N_Q_HEADS * HEAD_DIM
K_END = Q_END + N_KV_HEADS * HEAD_DIM
V_END = K_END + N_KV_HEADS * HEAD_DIM
Z_END = V_END + 2 * SGU_WIDTH
IN_COLS = Z_END + 2 * D_MODEL
P_Z, P_G, P_Q, P_K, P_V = 0, 2048, 4096, 5120, 5248

ADAM_LR = 0.001
ADAM_B1 = 0.9
ADAM_B2 = 0.999
ADAM_EPS = 1e-08
ADAM_WD = 0.01
ADAM_STEP = 10

VMEM_LIMIT_BYTES = 56 * 1024 * 1024

BIG = ("w_in", "proj_a", "proj_b", "w_out", "ffn_w_gate", "ffn_w_up", "ffn_w_down")
COL_SHARDED = ("w_in", "ffn_w_gate", "ffn_w_up")
BIG_SHAPE = {"w_in": (D_MODEL, IN_COLS), "proj_a": (SGU_WIDTH, D_MODEL), "proj_b": (Q_END, D_MODEL),
             "w_out": (D_MODEL, D_MODEL), "ffn_w_gate": (D_MODEL, FFN_DIM), "ffn_w_up": (D_MODEL, FFN_DIM),
             "ffn_w_down": (FFN_DIM, D_MODEL)}
BIG_ROWS = {n: BIG_SHAPE[n][0] * BIG_SHAPE[n][1] // N_DEV // 1024 for n in BIG}
LAYER_ROWS = sum(BIG_ROWS.values())


def _pcall(body, **kw):
    return pl.pallas_call(body, **kw)


def _params(**kw):
    return pltpu.CompilerParams(vmem_limit_bytes=VMEM_LIMIT_BYTES, **kw)


def _tile(n, cap, unit=128):
    if n <= cap:
        return n
    best = 0
    t = unit
    while t <= cap:
        if n % t == 0:
            best = t
        t += unit
    assert best, (n, cap, unit)
    return best


def _mm(a, b, *, nt, out_dtype, name, res=None, gvec=None, after=None, tm=512, tn_cap=1024):
    M, K = a.shape
    N = b.shape[0] if nt else b.shape[1]
    tm = _tile(M, tm, 8)
    tn = _tile(N, tn_cap)
    dn = (((1,), (1,)), ((), ())) if nt else (((1,), (0,)), ((), ()))
    b_spec = pl.BlockSpec((tn, K), lambda i, j: (j, 0)) if nt else pl.BlockSpec((K, tn), lambda i, j: (0, j))
    o_spec = pl.BlockSpec((tm, tn), lambda i, j: (i, j))
    if res is None:
        extra = [] if after is None else [after]

        def body(a_ref, b_ref, *rest):
            o_ref = rest[-1]
            acc = lax.dot_general(a_ref[...].astype(BF), b_ref[...].astype(BF), dn, preferred_element_type=F32)
            o_ref[...] = acc.astype(out_dtype)
        return _pcall(body, name=name, grid=(M // tm, N // tn),
                      in_specs=[pl.BlockSpec((tm, K), lambda i, j: (i, 0)), b_spec] + [ANY] * len(extra), out_specs=o_spec,
                      out_shape=jax.ShapeDtypeStruct((M, N), out_dtype), compiler_params=_params())(a, b, *extra)

    def body_res(a_ref, b_ref, r_ref, g_ref, o_ref, acc_ref):
        acc = lax.dot_general(a_ref[...].astype(BF), b_ref[...].astype(BF), dn, preferred_element_type=F32)
        acc_ref[...] = acc
        o_ref[...] = r_ref[...] + g_ref[...] * acc
    return _pcall(body_res, name=name, grid=(M // tm, N // tn),
                  in_specs=[pl.BlockSpec((tm, K), lambda i, j: (i, 0)), b_spec, o_spec,
                            pl.BlockSpec((1, tn), lambda i, j: (0, j))],
                  out_specs=[o_spec, o_spec],
                  out_shape=[jax.ShapeDtypeStruct((M, N), F32), jax.ShapeDtypeStruct((M, N), F32)],
                  compiler_params=_params())(a, b, res, gvec)


def _mm_tn(a, b, *, name, out_dtype=BF, tk=512, tm_cap=1408, tn_cap=1024):
    S, M = a.shape
    N = b.shape[1]
    tk = _tile(S, tk, 8)
    tm = _tile(M, tm_cap)
    tn = _tile(N, tn_cap)
    nk = S // tk

    def body(a_ref, b_ref, o_ref, acc_ref):
        k = pl.program_id(2)

        @pl.when(k == 0)
        def _():
            acc_ref[...] = jnp.zeros_like(acc_ref)
        acc_ref[...] += lax.dot_general(a_ref[...].astype(BF), b_ref[...].astype(BF), (((0,), (0,)), ((), ())),
                                        preferred_element_type=F32)

        @pl.when(k == nk - 1)
        def _():
            o_ref[...] = acc_ref[...].astype(out_dtype)
    return _pcall(body, name=name, grid=(M // tm, N // tn, nk),
                  in_specs=[pl.BlockSpec((tk, tm), lambda i, j, k: (k, i)),
                            pl.BlockSpec((tk, tn), lambda i, j, k: (k, j))],
                  out_specs=pl.BlockSpec((tm, tn), lambda i, j, k: (i, j)),
                  out_shape=jax.ShapeDtypeStruct((M, N), out_dtype), scratch_shapes=[pltpu.VMEM((tm, tn), F32)],
                  compiler_params=_params())(a, b)


def _rms(x, w):
    return x * lax.rsqrt(jnp.mean(x * x, axis=-1, keepdims=True) + NORM_EPS) * w


def _normmod_fn(x, nw, sc, sh):
    return _rms(x, nw) * (1.0 + sc) + sh


def _gelu(x):
    return 0.5 * x * (1.0 + lax.erf(x * (2.0 ** -0.5)))


def _ln_gelu_fn(zv, w, b):
    v = _gelu(zv)
    mu = jnp.mean(v, axis=-1, keepdims=True)
    var = jnp.mean(jnp.square(v - mu), axis=-1, keepdims=True)
    return (v - mu) * lax.rsqrt(var + NORM_EPS) * w + b


def _sigmoid(x):
    return 1.0 / (1.0 + jnp.exp(-x))


def _row_spec(tm, n):
    return pl.BlockSpec((tm, n), lambda i: (i, 0))


def _vec_spec(n):
    return pl.BlockSpec((1, n), lambda i: (0, 0))


def _acc(ref, val):
    @pl.when(pl.program_id(0) == 0)
    def _():
        ref[...] = jnp.zeros_like(ref)
    ref[...] += val


def _normmod_fwd(x, nw, sc, sh, *, name, tm=512):
    S, Dm = x.shape
    tm = _tile(S, tm, 8)

    def body(x_ref, nw_ref, sc_ref, sh_ref, o_ref):
        o_ref[...] = _normmod_fn(x_ref[...], nw_ref[...], sc_ref[...], sh_ref[...]).astype(BF)
    return _pcall(body, name=name, grid=(S // tm,),
                  in_specs=[_row_spec(tm, Dm), _vec_spec(Dm), _vec_spec(Dm), _vec_spec(Dm)],
                  out_specs=_row_spec(tm, Dm), out_shape=jax.ShapeDtypeStruct((S, Dm), BF),
                  compiler_params=_params())(x, nw, sc, sh)


def _normmod_bwd(dh, x, nw, sc, sh, dres, *, name, tm=256):
    S, Dm = x.shape
    tm = _tile(S, tm, 8)

    def body(dh_ref, x_ref, nw_ref, sc_ref, sh_ref, dres_ref, dx_ref, dnw_ref, dsc_ref, dsh_ref):
        _, vjp = jax.vjp(_normmod_fn, x_ref[...], nw_ref[...], sc_ref[...], sh_ref[...])
        dx, dnw, dsc, dsh = vjp(dh_ref[...])
        dx_ref[...] = dres_ref[...] + dx
        _acc(dnw_ref, dnw)
        _acc(dsc_ref, dsc)
        _acc(dsh_ref, dsh)
    vec = jax.ShapeDtypeStruct((1, Dm), F32)
    return _pcall(body, name=name, grid=(S // tm,),
                  in_specs=[_row_spec(tm, Dm), _row_spec(tm, Dm), _vec_spec(Dm), _vec_spec(Dm), _vec_spec(Dm),
                            _row_spec(tm, Dm)],
                  out_specs=[_row_spec(tm, Dm), _vec_spec(Dm), _vec_spec(Dm), _vec_spec(Dm)],
                  out_shape=[jax.ShapeDtypeStruct((S, Dm), F32), vec, vec, vec],
                  compiler_params=_params())(dh, x, nw, sc, sh, dres)


def _scale_reduce(dx, o, g, *, name, after=None, tm=512):
    S, Dm = dx.shape
    tm = _tile(S, tm, 8)
    extra = [] if after is None else [after]

    def body(dx_ref, o_ref, g_ref, *rest):
        do_ref, dg_ref = rest[-2:]
        dxv = dx_ref[...]
        do_ref[...] = (dxv * g_ref[...]).astype(BF)
        _acc(dg_ref, jnp.sum(dxv * o_ref[...], axis=0, keepdims=True))
    return _pcall(body, name=name, grid=(S // tm,),
                  in_specs=[_row_spec(tm, Dm), _row_spec(tm, Dm), _vec_spec(Dm)] + [ANY] * len(extra),
                  out_specs=[_row_spec(tm, Dm), _vec_spec(Dm)],
                  out_shape=[jax.ShapeDtypeStruct((S, Dm), BF), jax.ShapeDtypeStruct((1, Dm), F32)],
                  compiler_params=_params())(dx, o, g, *extra)


def _head(x, fw, target, *, tm=256):
    S, Dm = x.shape
    tm = _tile(S, tm, 8)

    def body(x_ref, fw_ref, t_ref, dx_ref, dfw_ref, loss_ref):
        y, vjp = jax.vjp(_rms, x_ref[...], fw_ref[...])
        err = y - t_ref[...]
        dx, dfw = vjp(err * (1.0 / Dm))
        dx_ref[...] = dx
        _acc(dfw_ref, dfw)
        part = 0.5 * jnp.sum(jnp.mean(err * err, axis=-1, keepdims=True), axis=0, keepdims=True)
        _acc(loss_ref, jnp.broadcast_to(part, (8, 128)))
    return _pcall(body, name="head", grid=(S // tm,),
                  in_specs=[_row_spec(tm, Dm), _vec_spec(Dm), _row_spec(tm, Dm)],
                  out_specs=[_row_spec(tm, Dm), _vec_spec(Dm), pl.BlockSpec((8, 128), lambda i: (0, 0))],
                  out_shape=[jax.ShapeDtypeStruct((S, Dm), F32), jax.ShapeDtypeStruct((1, Dm), F32),
                             jax.ShapeDtypeStruct((8, 128), F32)],
                  compiler_params=_params())(x, fw, target)


def _tril_mask():
    r = lax.broadcasted_iota(jnp.int32, (SGU_CHUNK, SGU_CHUNK), 0)
    c = lax.broadcasted_iota(jnp.int32, (SGU_CHUNK, SGU_CHUNK), 1)
    return c <= r


def _sgu_fwd(proj, lnw, lnb, w, b_t, *, name, tm=256):
    S = proj.shape[0]
    tm = _tile(S, tm, SGU_CHUNK)

    def body(zu_ref, zv_ref, lnw_ref, lnb_ref, w_ref, bt_ref, o_ref):
        u = _gelu(zu_ref[...])
        vn = _ln_gelu_fn(zv_ref[...], lnw_ref[...], lnb_ref[...]).astype(BF)
        mask = _tril_mask()
        for g in range(SGU_GROUPS):
            wm = jnp.where(mask, w_ref[g], 0.0).astype(BF)
            cols = slice(g * 128, (g + 1) * 128)
            for ci in range(tm // SGU_CHUNK):
                rows = slice(ci * SGU_CHUNK, (ci + 1) * SGU_CHUNK)
                f = jnp.dot(wm, vn[rows, cols], preferred_element_type=F32) + bt_ref[:, g:g + 1]
                o_ref[rows, cols] = (u[rows, cols] * f).astype(BF)
    return _pcall(body, name=name, grid=(S // tm,),
                  in_specs=[pl.BlockSpec((tm, SGU_WIDTH), lambda i: (i, 0)), pl.BlockSpec((tm, SGU_WIDTH), lambda i: (i, 1)),
                            _vec_spec(SGU_WIDTH), _vec_spec(SGU_WIDTH),
                            pl.BlockSpec((SGU_GROUPS, 128, 128), lambda i: (0, 0, 0)),
                            pl.BlockSpec((128, SGU_GROUPS), lambda i: (0, 0))],
                  out_specs=_row_spec(tm, SGU_WIDTH), out_shape=jax.ShapeDtypeStruct((S, SGU_WIDTH), BF),
                  compiler_params=_params())(proj, proj, lnw, lnb, w, b_t)


def _sgu_bwd(dy, proj, lnw, lnb, w, b_t, *, name, tm=256):
    S = proj.shape[0]
    tm = _tile(S, tm, SGU_CHUNK)

    def body(dy_ref, zu_ref, zv_ref, lnw_ref, lnb_ref, w_ref, bt_ref, dz_ref, dlnw_ref, dlnb_ref, dw_ref, dbt_ref,
             f_s, dvn_s):
        first = pl.program_id(0) == 0

        @pl.when(first)
        def _():
            dw_ref[...] = jnp.zeros_like(dw_ref)
            dbt_ref[...] = jnp.zeros_like(dbt_ref)
        u, vjp_u = jax.vjp(_gelu, zu_ref[...])
        vn, vjp_v = jax.vjp(_ln_gelu_fn, zv_ref[...], lnw_ref[...], lnb_ref[...])
        vn = vn.astype(BF)
        dy_v = dy_ref[...]
        df = (dy_v * u).astype(BF)
        mask = _tril_mask()
        for g in range(SGU_GROUPS):
            wm = jnp.where(mask, w_ref[g], 0.0).astype(BF)
            cols = slice(g * 128, (g + 1) * 128)
            dwg = jnp.zeros((128, 128), F32)
            dbg = jnp.zeros((128, 1), F32)
            for ci in range(tm // SGU_CHUNK):
                rows = slice(ci * SGU_CHUNK, (ci + 1) * SGU_CHUNK)
                vn_c = vn[rows, cols]
                df_c = df[rows, cols]
                f_s[rows, cols] = jnp.dot(wm, vn_c, preferred_element_type=F32) + bt_ref[:, g:g + 1]
                dvn_s[rows, cols] = lax.dot_general(wm, df_c, (((0,), (0,)), ((), ())), preferred_element_type=F32)
                dwg = dwg + lax.dot_general(df_c, vn_c, (((1,), (1,)), ((), ())), preferred_element_type=F32)
                dbg = dbg + jnp.sum((dy_v[rows, cols] * u[rows, cols]), axis=1, keepdims=True)
            dw_ref[g] += jnp.where(mask, dwg, 0.0)
            dbt_ref[:, g:g + 1] += dbg
        (dzu,) = vjp_u(dy_v * f_s[...])
        dzv, dlnw, dlnb = vjp_v(dvn_s[...])
        dz_ref[:, :SGU_WIDTH] = dzu.astype(BF)
        dz_ref[:, SGU_WIDTH:] = dzv.astype(BF)
        _acc(dlnw_ref, dlnw)
        _acc(dlnb_ref, dlnb)
    vec = jax.ShapeDtypeStruct((1, SGU_WIDTH), F32)
    return _pcall(body, name=name, grid=(S // tm,),
                  in_specs=[_row_spec(tm, SGU_WIDTH),
                            pl.BlockSpec((tm, SGU_WIDTH), lambda i: (i, 0)), pl.BlockSpec((tm, SGU_WIDTH), lambda i: (i, 1)),
                            _vec_spec(SGU_WIDTH), _vec_spec(SGU_WIDTH),
                            pl.BlockSpec((SGU_GROUPS, 128, 128), lambda i: (0, 0, 0)),
                            pl.BlockSpec((128, SGU_GROUPS), lambda i: (0, 0))],
                  out_specs=[_row_spec(tm, 2 * SGU_WIDTH), _vec_spec(SGU_WIDTH), _vec_spec(SGU_WIDTH),
                             pl.BlockSpec((SGU_GROUPS, 128, 128), lambda i: (0, 0, 0)),
                             pl.BlockSpec((128, SGU_GROUPS), lambda i: (0, 0))],
                  out_shape=[jax.ShapeDtypeStruct((S, 2 * SGU_WIDTH), BF), vec, vec,
                             jax.ShapeDtypeStruct((SGU_GROUPS, 128, 128), F32),
                             jax.ShapeDtypeStruct((128, SGU_GROUPS), F32)],
                  scratch_shapes=[pltpu.VMEM((tm, SGU_WIDTH), F32), pltpu.VMEM((tm, SGU_WIDTH), F32)],
                  compiler_params=_params())(dy, proj, proj, lnw, lnb, w, b_t)


def _merge_fwd(a, b, proj, *, name, tm=512):
    S, Dm = a.shape
    tm = _tile(S, tm, 8)
    ga_blk, gb_blk = P_G // Dm, P_G // Dm + 1

    def body(a_ref, b_ref, ga_ref, gb_ref, o_ref):
        o_ref[...] = (_sigmoid(ga_ref[...]) * a_ref[...].astype(F32)
                      + _sigmoid(gb_ref[...]) * b_ref[...].astype(F32)).astype(BF)
    return _pcall(body, name=name, grid=(S // tm,),
                  in_specs=[_row_spec(tm, Dm), _row_spec(tm, Dm), pl.BlockSpec((tm, Dm), lambda i: (i, ga_blk)),
                            pl.BlockSpec((tm, Dm), lambda i: (i, gb_blk))],
                  out_specs=_row_spec(tm, Dm), out_shape=jax.ShapeDtypeStruct((S, Dm), BF),
                  compiler_params=_params())(a, b, proj, proj)


def _merge_bwd(dm, a, b, proj, *, name, tm=512):
    S, Dm = a.shape
    tm = _tile(S, tm, 8)
    ga_blk, gb_blk = P_G // Dm, P_G // Dm + 1

    def body(dm_ref, a_ref, b_ref, ga_ref, gb_ref, da_ref, db_ref, dg_ref):
        dmv = dm_ref[...]
        sa = _sigmoid(ga_ref[...])
        sb = _sigmoid(gb_ref[...])
        da_ref[...] = (dmv * sa).astype(BF)
        db_ref[...] = (dmv * sb).astype(BF)
        dg_ref[:, :Dm] = (dmv * a_ref[...].astype(F32) * sa * (1.0 - sa)).astype(BF)
        dg_ref[:, Dm:] = (dmv * b_ref[...].astype(F32) * sb * (1.0 - sb)).astype(BF)
    return _pcall(body, name=name, grid=(S // tm,),
                  in_specs=[_row_spec(tm, Dm), _row_spec(tm, Dm), _row_spec(tm, Dm),
                            pl.BlockSpec((tm, Dm), lambda i: (i, ga_blk)), pl.BlockSpec((tm, Dm), lambda i: (i, gb_blk))],
                  out_specs=[_row_spec(tm, Dm), _row_spec(tm, Dm), _row_spec(tm, 2 * Dm)],
                  out_shape=[jax.ShapeDtypeStruct((S, Dm), BF), jax.ShapeDtypeStruct((S, Dm), BF),
                             jax.ShapeDtypeStruct((S, 2 * Dm), BF)],
                  compiler_params=_params())(dm, a, b, proj, proj)


def _shift_rows(a, halo, k, up):
    n = a.shape[0]
    r8 = lax.broadcasted_iota(jnp.int32, (8, a.shape[1]), 0)
    if not up:
        rolled = pltpu.roll(a, k, 0)
        patch = jnp.where(r8 < k, pltpu.roll(halo, k, 0), rolled[:8])
        return jnp.concatenate([patch, rolled[8:]], axis=0)
    rolled = pltpu.roll(a, n - k, 0)
    patch = jnp.where(r8 >= 8 - k, pltpu.roll(halo, 8 - k, 0), rolled[n - 8:])
    return jnp.concatenate([rolled[:n - 8], patch], axis=0)


def _conv_taps(a, halo):
    return _shift_rows(a, halo, 2, False), _shift_rows(a, halo, 1, False), a


def _ffn_act_fwd(au, cw, cb, *, name, tm=256):
    S = au.shape[0]
    Fd = FFN_DIM
    tm = _tile(S, tm, 8)
    hb = tm // 8

    def body(a_ref, up_ref, halo_ref, cw_ref, cb_ref, o_ref):
        halo = jnp.where(pl.program_id(0) > 0, halo_ref[...], 0.0)
        t0, t1, t2 = _conv_taps(a_ref[...], halo)
        ac = cb_ref[...] + cw_ref[0:1, :] * t0 + cw_ref[1:2, :] * t1 + cw_ref[2:3, :] * t2
        o_ref[...] = (ac * _sigmoid(ac) * up_ref[...]).astype(BF)
    return _pcall(body, name=name, grid=(S // tm,),
                  in_specs=[pl.BlockSpec((tm, Fd), lambda i: (i, 0)), pl.BlockSpec((tm, Fd), lambda i: (i, 1)),
                            pl.BlockSpec((8, Fd), lambda i: (jnp.maximum(i * hb - 1, 0), 0)),
                            pl.BlockSpec((3, Fd), lambda i: (0, 0)), _vec_spec(Fd)],
                  out_specs=_row_spec(tm, Fd), out_shape=jax.ShapeDtypeStruct((S, Fd), BF),
                  compiler_params=_params())(au, au, au, cw, cb)


def _ffn_act_bwd_a(dhf, au, cw, cb, *, name, tm=256):
    S = au.shape[0]
    Fd = FFN_DIM
    tm = _tile(S, tm, 8)
    hb = tm // 8

    def body(dhf_ref, a_ref, up_ref, halo_ref, cw_ref, cb_ref, dac_ref, dup_ref, dcw_ref, dcb_ref):
        halo = jnp.where(pl.program_id(0) > 0, halo_ref[...], 0.0)
        t0, t1, t2 = _conv_taps(a_ref[...], halo)
        ac = cb_ref[...] + cw_ref[0:1, :] * t0 + cw_ref[1:2, :] * t1 + cw_ref[2:3, :] * t2
        s = _sigmoid(ac)
        dhf_v = dhf_ref[...]
        dup_ref[...] = (dhf_v * ac * s).astype(BF)
        dac = dhf_v * up_ref[...] * (s * (1.0 + ac * (1.0 - s)))
        dac_ref[...] = dac
        _acc(dcb_ref, jnp.sum(dac, axis=0, keepdims=True))
        _acc(dcw_ref, jnp.concatenate([jnp.sum(dac * t0, axis=0, keepdims=True),
                                       jnp.sum(dac * t1, axis=0, keepdims=True),
                                       jnp.sum(dac * t2, axis=0, keepdims=True)], axis=0))
    return _pcall(body, name=name, grid=(S // tm,),
                  in_specs=[_row_spec(tm, Fd), pl.BlockSpec((tm, Fd), lambda i: (i, 0)),
                            pl.BlockSpec((tm, Fd), lambda i: (i, 1)),
                            pl.BlockSpec((8, Fd), lambda i: (jnp.maximum(i * hb - 1, 0), 0)),
                            pl.BlockSpec((3, Fd), lambda i: (0, 0)), _vec_spec(Fd)],
                  out_specs=[_row_spec(tm, Fd), _row_spec(tm, Fd), pl.BlockSpec((3, Fd), lambda i: (0, 0)), _vec_spec(Fd)],
                  out_shape=[jax.ShapeDtypeStruct((S, Fd), F32), jax.ShapeDtypeStruct((S, Fd), BF),
                             jax.ShapeDtypeStruct((3, Fd), F32), jax.ShapeDtypeStruct((1, Fd), F32)],
                  compiler_params=_params())(dhf, au, au, au, cw, cb)


def _ffn_act_bwd_b(dac, cw, *, name, tm=256):
    S, Fd = dac.shape
    tm = _tile(S, tm, 8)
    hb = tm // 8
    last = S // tm - 1

    def body(d_ref, halo_ref, cw_ref, o_ref):
        halo = jnp.where(pl.program_id(0) < last, halo_ref[...], 0.0)
        d = d_ref[...]
        o_ref[...] = (cw_ref[2:3, :] * d + cw_ref[1:2, :] * _shift_rows(d, halo, 1, True)
                      + cw_ref[0:1, :] * _shift_rows(d, halo, 2, True)).astype(BF)
    return _pcall(body, name=name, grid=(S // tm,),
                  in_specs=[_row_spec(tm, Fd), pl.BlockSpec((8, Fd), lambda i: (jnp.minimum((i + 1) * hb, S // 8 - 1), 0)),
                            pl.BlockSpec((3, Fd), lambda i: (0, 0))],
                  out_specs=_row_spec(tm, Fd), out_shape=jax.ShapeDtypeStruct((S, Fd), BF),
                  compiler_params=_params())(dac, dac, cw)


def _rope_tables(pos_col, inv_row, m1_row, m2_row):
    S = pos_col.shape[0]
    tm = _tile(S, 512, 8)

    def body(p_ref, inv_ref, m1_ref, m2_ref, c_ref, s1_ref, s2_ref):
        ang = p_ref[...] * inv_ref[...]
        sn = jnp.sin(ang)
        c_ref[...] = jnp.cos(ang)
        s1_ref[...] = -sn * m1_ref[...]
        s2_ref[...] = sn * m2_ref[...]
    sh = jax.ShapeDtypeStruct((S, 128), F32)
    return _pcall(body, name="rope_tables", grid=(S // tm,),
                  in_specs=[pl.BlockSpec((tm, 1), lambda i: (i, 0)), _vec_spec(128), _vec_spec(128), _vec_spec(128)],
                  out_specs=[_row_spec(tm, 128)] * 3, out_shape=[sh, sh, sh], compiler_params=_params())(
                      pos_col, inv_row, m1_row, m2_row)


def _rope_apply(x, c, s1, s2):
    outs = []
    for j in range(x.shape[1] // 128):
        xj = x[:, j * 128:(j + 1) * 128]
        outs.append(xj * c + pltpu.roll(xj, 120, 1) * s1 + pltpu.roll(xj, 8, 1) * s2)
    return outs[0] if len(outs) == 1 else jnp.concatenate(outs, axis=1)


def _rope_apply_t(d, c, s1, s2):
    outs = []
    for j in range(d.shape[1] // 128):
        dj = d[:, j * 128:(j + 1) * 128]
        outs.append(dj * c + pltpu.roll(dj * s1, 8, 1) + pltpu.roll(dj * s2, 120, 1))
    return outs[0] if len(outs) == 1 else jnp.concatenate(outs, axis=1)


def _rope_fwd(proj, c, s1, s2, *, name, tm=512):
    S = proj.shape[0]
    tm = _tile(S, tm, 8)

    def body(q_ref, k_ref, v_ref, c_ref, s1_ref, s2_ref, qo_ref, kvo_ref):
        cv, s1v, s2v = c_ref[...], s1_ref[...], s2_ref[...]
        qo_ref[...] = _rope_apply(q_ref[...], cv, s1v, s2v).astype(BF)
        kvo_ref[:, :128] = _rope_apply(k_ref[...], cv, s1v, s2v).astype(BF)
        kvo_ref[:, 128:] = v_ref[...].astype(BF)
    return _pcall(body, name=name, grid=(S // tm,),
                  in_specs=[pl.BlockSpec((tm, Q_END), lambda i: (i, P_Q // Q_END)),
                            pl.BlockSpec((tm, 128), lambda i: (i, P_K // 128)),
                            pl.BlockSpec((tm, 128), lambda i: (i, P_V // 128)),
                            _row_spec(tm, 128), _row_spec(tm, 128), _row_spec(tm, 128)],
                  out_specs=[_row_spec(tm, Q_END), _row_spec(tm, 256)],
                  out_shape=[jax.ShapeDtypeStruct((S, Q_END), BF), jax.ShapeDtypeStruct((S, 256), BF)],
                  compiler_params=_params())(proj, proj, proj, c, s1, s2)


def _rope_bwd(dq, dkv, c, s1, s2, *, name, tm=512):
    S = dq.shape[0]
    tm = _tile(S, tm, 8)

    def body(dq_ref, dkv_ref, c_ref, s1_ref, s2_ref, o_ref):
        cv, s1v, s2v = c_ref[...], s1_ref[...], s2_ref[...]
        o_ref[:, :Q_END] = _rope_apply_t(dq_ref[...].astype(F32), cv, s1v, s2v).astype(BF)
        o_ref[:, Q_END:Q_END + 128] = _rope_apply_t(dkv_ref[:, :128].astype(F32), cv, s1v, s2v).astype(BF)
        o_ref[:, Q_END + 128:] = dkv_ref[:, 128:].astype(BF)
    return _pcall(body, name=name, grid=(S // tm,),
                  in_specs=[_row_spec(tm, Q_END), _row_spec(tm, 256), _row_spec(tm, 128), _row_spec(tm, 128),
                            _row_spec(tm, 128)],
                  out_specs=_row_spec(tm, V_END), out_shape=jax.ShapeDtypeStruct((S, V_END), BF),
                  compiler_params=_params())(dq, dkv, c, s1, s2)


def _attn_probs(q, kb, sink, n):
    R = Q_PER_KV * ATTN_BLOCK
    s = lax.dot_general(q, kb, (((1,), (1,)), ((), ())), preferred_element_type=F32) * (HEAD_DIM ** -0.5)
    i = lax.broadcasted_iota(jnp.int32, (R, 2 * ATTN_BLOCK), 0) & (ATTN_BLOCK - 1)
    j = lax.broadcasted_iota(jnp.int32, (R, 2 * ATTN_BLOCK), 1)
    ok = (j > i) & (j <= i + ATTN_BLOCK) & ((n > 0) | (j >= ATTN_BLOCK))
    s = jnp.where(ok, s, -jnp.inf)
    m = jnp.maximum(jnp.max(s, axis=-1, keepdims=True), sink)
    p = jnp.exp(s - m)
    es = jnp.exp(sink - m)
    inv = 1.0 / (jnp.sum(p, axis=-1, keepdims=True) + es)
    return p * inv, es * inv


def _attn_specs(S):
    nb = S // ATTN_BLOCK
    qs = pl.BlockSpec((Q_PER_KV, ATTN_BLOCK, HEAD_DIM), lambda g, n: (g, n, 0))
    cur = pl.BlockSpec((None, ATTN_BLOCK, HEAD_DIM), lambda g, n: (g, n, 0))
    prev = pl.BlockSpec((None, ATTN_BLOCK, HEAD_DIM), lambda g, n: (g, jnp.maximum(n - 1, 0), 0))
    sink = pl.BlockSpec((None, Q_PER_KV * ATTN_BLOCK, 1), lambda g, n: (g, 0, 0))
    return nb, qs, cur, prev, sink


def _attn_fwd(qh, kh, vh, sink_rows, *, name):
    S = qh.shape[1]
    nb, qs, cur, prev, sink = _attn_specs(S)
    R = Q_PER_KV * ATTN_BLOCK

    def body(q_ref, kp_ref, kc_ref, vp_ref, vc_ref, sk_ref, o_ref):
        n = pl.program_id(1)
        q = q_ref[...].reshape(R, HEAD_DIM)
        kb = jnp.concatenate([kp_ref[...], kc_ref[...]], axis=0)
        vb = jnp.concatenate([vp_ref[...], vc_ref[...]], axis=0)
        p, _ = _attn_probs(q, kb, sk_ref[...], n)
        o = jnp.dot(p.astype(BF), vb, preferred_element_type=F32)
        o_ref[...] = o.reshape(Q_PER_KV, ATTN_BLOCK, HEAD_DIM).astype(BF)
    return _pcall(body, name=name, grid=(N_KV_HEADS, nb), in_specs=[qs, prev, cur, prev, cur, sink], out_specs=qs,
                  out_shape=jax.ShapeDtypeStruct(qh.shape, BF), compiler_params=_params())(qh, kh, kh, vh, vh, sink_rows)


def _attn_bwd(do, qh, kh, vh, sink_rows, *, name):
    S = qh.shape[1]
    nb, qs, cur, prev, sink = _attn_specs(S)
    R = Q_PER_KV * ATTN_BLOCK
    full = pl.BlockSpec((None, S, HEAD_DIM), lambda g, n: (g, 0, 0))
    dsk_spec = pl.BlockSpec((None, Q_PER_KV, 128), lambda g, n: (g, 0, 0))

    def body(do_ref, q_ref, kp_ref, kc_ref, vp_ref, vc_ref, sk_ref, dq_ref, dk_ref, dv_ref, dsk_ref):
        n = pl.program_id(1)

        @pl.when(n == 0)
        def _():
            dk_ref[...] = jnp.zeros_like(dk_ref)
            dv_ref[...] = jnp.zeros_like(dv_ref)
            dsk_ref[...] = jnp.zeros_like(dsk_ref)
        q = q_ref[...].reshape(R, HEAD_DIM)
        dov = do_ref[...].reshape(R, HEAD_DIM)
        kb = jnp.concatenate([kp_ref[...], kc_ref[...]], axis=0)
        vb = jnp.concatenate([vp_ref[...], vc_ref[...]], axis=0)
        p, ps = _attn_probs(q, kb, sk_ref[...], n)
        dp = lax.dot_general(dov, vb, (((1,), (1,)), ((), ())), preferred_element_type=F32)
        dd = jnp.sum(p * dp, axis=-1, keepdims=True)
        ds = (p * (dp - dd) * (HEAD_DIM ** -0.5)).astype(BF)
        dq = jnp.dot(ds, kb, preferred_element_type=F32)
        dq_ref[...] = dq.reshape(Q_PER_KV, ATTN_BLOCK, HEAD_DIM).astype(BF)
        dkb = lax.dot_general(ds, q, (((0,), (0,)), ((), ())), preferred_element_type=F32)
        dvb = lax.dot_general(p.astype(BF), dov, (((0,), (0,)), ((), ())), preferred_element_type=F32)
        r0 = pl.multiple_of(n * ATTN_BLOCK, ATTN_BLOCK)
        dk_ref[pl.ds(r0, ATTN_BLOCK), :] += dkb[ATTN_BLOCK:]
        dv_ref[pl.ds(r0, ATTN_BLOCK), :] += dvb[ATTN_BLOCK:]

        @pl.when(n > 0)
        def _():
            rp = pl.multiple_of((n - 1) * ATTN_BLOCK, ATTN_BLOCK)
            dk_ref[pl.ds(rp, ATTN_BLOCK), :] += dkb[:ATTN_BLOCK]
            dv_ref[pl.ds(rp, ATTN_BLOCK), :] += dvb[:ATTN_BLOCK]
        dsr = -(ps * dd)
        sub = lax.broadcasted_iota(jnp.int32, (Q_PER_KV, 128), 0)
        upd = jnp.zeros((Q_PER_KV, 128), F32)
        for h in range(Q_PER_KV):
            upd = jnp.where(sub == h, jnp.sum(dsr[h * ATTN_BLOCK:(h + 1) * ATTN_BLOCK]), upd)
        dsk_ref[...] += upd
    return _pcall(body, name=name, grid=(N_KV_HEADS, nb), in_specs=[qs, qs, prev, cur, prev, cur, sink],
                  out_specs=[qs, full, full, dsk_spec],
                  out_shape=[jax.ShapeDtypeStruct(qh.shape, BF), jax.ShapeDtypeStruct(kh.shape, F32),
                             jax.ShapeDtypeStruct(kh.shape, F32), jax.ShapeDtypeStruct((N_KV_HEADS, Q_PER_KV, 128), F32)],
                  compiler_params=_params())(do, qh, kh, kh, vh, vh, sink_rows)


def _to_heads(x, nh):
    return x.reshape(x.shape[0], nh, HEAD_DIM).transpose(1, 0, 2)


def _from_heads(x):
    return x.transpose(1, 0, 2).reshape(x.shape[1], x.shape[0] * HEAD_DIM)


def _ada_fwd(c_all, ada_w):
    ncol = ada_w.shape[2]

    def body(c_ref, w_ref, o_ref):
        cv = c_ref[...]
        ca = (cv * _sigmoid(cv)).astype(BF)
        for l in range(DEPTH):
            o_ref[:, l * ncol:(l + 1) * ncol] = jnp.dot(ca, w_ref[l].astype(BF), preferred_element_type=F32)
    return _pcall(body, name="ada_fwd", out_shape=jax.ShapeDtypeStruct((N_DEV, DEPTH * ncol), F32),
                  compiler_params=_params())(c_all, ada_w)


def _ada_bwd(c_all, dm):
    ncol = dm.shape[2]

    def body(c_ref, dm_ref, o_ref):
        cv = c_ref[...]
        ca = (cv * _sigmoid(cv)).astype(BF)
        for l in range(DEPTH):
            o_ref[l] = lax.dot_general(ca, dm_ref[l].astype(BF), (((0,), (0,)), ((), ())), preferred_element_type=F32)
    return _pcall(body, name="ada_bwd", out_shape=jax.ShapeDtypeStruct((DEPTH, D_MODEL, ncol), F32),
                  compiler_params=_params())(c_all, dm)


def _adamw(w, g, m, v, *, name):
    R, C = w.shape
    tr = R
    for t in range(8, 513, 8):
        if R % t == 0:
            tr = t
    c1 = 1.0 - ADAM_B1 ** ADAM_STEP
    c2 = 1.0 - ADAM_B2 ** ADAM_STEP

    def body(w_ref, g_ref, m_ref, v_ref, d_ref, mo_ref, vo_ref):
        gv = g_ref[...]
        mn = ADAM_B1 * m_ref[...] + (1.0 - ADAM_B1) * gv
        vn = ADAM_B2 * v_ref[...] + (1.0 - ADAM_B2) * (gv * gv)
        mo_ref[...] = mn
        vo_ref[...] = vn
        d_ref[...] = -ADAM_LR * ((mn / c1) / (jnp.sqrt(vn / c2) + ADAM_EPS) + ADAM_WD * w_ref[...])
    spec = pl.BlockSpec((tr, C), lambda i: (i, 0))
    sh = jax.ShapeDtypeStruct((R, C), F32)
    return _pcall(body, name=name, grid=(R // tr,), in_specs=[spec] * 4, out_specs=[spec] * 3, out_shape=[sh, sh, sh],
                  compiler_params=_params())(w, g, m, v)


def _sum8(parts, *, name):
    _, R, C = parts.shape
    tr = R
    for t in range(16, 257, 16):
        if R % t == 0:
            tr = t

    def body(p_ref, o_ref):
        acc = p_ref[0].astype(F32)
        for k in range(1, N_DEV):
            acc = acc + p_ref[k].astype(F32)
        o_ref[...] = acc
    return _pcall(body, name=name, grid=(R // tr,), in_specs=[pl.BlockSpec((N_DEV, tr, C), lambda i: (0, i, 0))],
                  out_specs=pl.BlockSpec((tr, C), lambda i: (i, 0)), out_shape=jax.ShapeDtypeStruct((R, C), F32),
                  compiler_params=_params())(parts)


MESH_ID = pl.DeviceIdType.MESH
ANY = pl.BlockSpec(memory_space=pl.ANY)


def _all_gather(x, *, name):
    R, C = x.shape

    def body(x_ref, out_ref, send_sems, recv_sems, local_sem):
        mx, my, mc = lax.axis_index("x"), lax.axis_index("y"), lax.axis_index("c")
        me, sibling = (mx, my, mc), (mx, my, 1 - mc)
        chips = [(1 - mx, my), (mx, 1 - my), (1 - mx, 1 - my)]

        def blk(px, py, pc):
            return out_ref.at[4 * px + 2 * py + pc]

        def copy(k, block, to, src=None):
            return pltpu.make_async_remote_copy(
                src_ref=blk(*block) if src is None else src, dst_ref=blk(*block),
                send_sem=send_sems.at[k], recv_sem=recv_sems.at[k], device_id=to, device_id_type=MESH_ID)

        mine = pltpu.make_async_copy(x_ref, blk(*me), local_sem)
        mine.start()
        first = [copy(0, me, sibling, src=x_ref)]
        first += [copy(1 + j, me, (*chip, mc), src=x_ref) for j, chip in enumerate(chips)]
        for cp in first:
            cp.start()
        passed = [copy(4 + j, (*chip, mc), sibling) for j, chip in enumerate(chips)]
        for j, chip in enumerate(chips):
            copy(1 + j, (*chip, mc), me).wait_recv()
            passed[j].start()
        copy(0, sibling, me).wait_recv()
        for j, chip in enumerate(chips):
            copy(4 + j, (*chip, 1 - mc), me).wait_recv()
        for cp in first + passed:
            cp.wait_send()
        mine.wait()
    return _pcall(body, name=name, in_specs=[ANY], out_specs=ANY,
                  out_shape=jax.ShapeDtypeStruct((N_DEV, R, C), x.dtype),
                  scratch_shapes=[pltpu.SemaphoreType.DMA((7,)), pltpu.SemaphoreType.DMA((7,)), pltpu.SemaphoreType.DMA],
                  compiler_params=pltpu.CompilerParams(has_side_effects=True))(x)


HBM_SPEC = pl.BlockSpec(memory_space=pltpu.HBM)
SEM_SPEC = pl.BlockSpec(memory_space=pltpu.SEMAPHORE)
DATAFLOW = pltpu.SideEffectType.DATAFLOW_SIDE_EFFECTING


def _coords():
    return lax.axis_index("x"), lax.axis_index("y"), lax.axis_index("c")


def _other_chips(mx, my):
    return [(1 - mx, my), (mx, 1 - my), (1 - mx, 1 - my)]


def _plan_gather_ici(refs, send, recv):
    src, land = refs
    mx, my, mc = _coords()
    return [pltpu.make_async_remote_copy(src_ref=src, dst_ref=land.at[mc, 2 * mx + my], send_sem=send[j], recv_sem=recv[j],
                                         device_id=(px, py, mc), device_id_type=MESH_ID)
            for j, (px, py) in enumerate(_other_chips(mx, my))]


def _plan_gather_d2d(refs, send, recv):
    (land,) = refs
    mx, my, mc = _coords()
    return [pltpu.make_async_remote_copy(src_ref=land.at[mc], dst_ref=land.at[mc], send_sem=send[0], recv_sem=recv[0],
                                         device_id=(mx, my, 1 - mc), device_id_type=MESH_ID)]


def _plan_reduce_d2d(refs, send, recv):
    g, land = refs
    mx, my, mc = _coords()
    return [pltpu.make_async_remote_copy(src_ref=g.at[1 - mc], dst_ref=land, send_sem=send[0], recv_sem=recv[0],
                                         device_id=(mx, my, 1 - mc), device_id_type=MESH_ID)]


def _plan_reduce_ici(refs, send, recv):
    h, land = refs
    mx, my, mc = _coords()
    return [pltpu.make_async_remote_copy(src_ref=h.at[2 * px + py], dst_ref=land.at[j], send_sem=send[j], recv_sem=recv[j],
                                         device_id=(px, py, mc), device_id_type=MESH_ID)
            for j, (px, py) in enumerate(_other_chips(mx, my))]


def _rdma_start(bufs, n, plan, *, name):
    nb = len(bufs)

    def body(*refs):
        ins, send, recv = refs[:nb], refs[nb:nb + n], refs[nb + n:nb + 2 * n]
        token = refs[-1]
        for cp in plan(ins, send, recv):
            cp.start()
        token[...] = jnp.zeros_like(token)
    out = _pcall(body, name=name,
                 out_shape=tuple([pltpu.SemaphoreType.DMA(())] * (2 * n) + [pltpu.HBM(b.shape, b.dtype) for b in bufs]
                                 + [jax.ShapeDtypeStruct((8, 128), F32)]),
                 in_specs=tuple([HBM_SPEC] * nb),
                 out_specs=tuple([SEM_SPEC] * (2 * n) + [HBM_SPEC] * nb + [pl.BlockSpec(memory_space=pltpu.VMEM)]),
                 input_output_aliases={i: 2 * n + i for i in range(nb)},
                 compiler_params=pltpu.CompilerParams(has_side_effects=DATAFLOW))(
                     *[pltpu.with_memory_space_constraint(b, pltpu.HBM) for b in bufs])
    return list(out[:2 * n]), list(out[2 * n:2 * n + nb]), out[-1]


def _rdma_wait(sems, bufs, n, plan, after, *, name):
    nb = len(bufs)

    def body(*refs):
        ins, send, recv = refs[:nb], refs[nb:nb + n], refs[nb + n:nb + 2 * n]
        for cp in plan(ins, send, recv):
            cp.wait_send()
            cp.wait_recv()
    out = _pcall(body, name=name, out_shape=tuple(pltpu.HBM(b.shape, b.dtype) for b in bufs),
                 in_specs=tuple([HBM_SPEC] * nb + [SEM_SPEC] * (2 * n) + [ANY]), out_specs=tuple([HBM_SPEC] * nb),
                 input_output_aliases={i: i for i in range(nb)},
                 compiler_params=pltpu.CompilerParams(has_side_effects=DATAFLOW))(*bufs, *sems, after)
    return list(out)


def _sum_pair(g, land, cidx, *, name):
    _, nchip, R, C = g.shape
    tr = _tile(R, 512, 16)

    def body(c_ref, g_ref, l_ref, o_ref):
        o_ref[...] = (g_ref[...].astype(F32) + l_ref[...].astype(F32)).astype(BF)
    grid_spec = pltpu.PrefetchScalarGridSpec(
        num_scalar_prefetch=1, grid=(nchip, R // tr),
        in_specs=[pl.BlockSpec((None, None, tr, C), lambda p, i, c_ref: (c_ref[0], p, i, 0)),
                  pl.BlockSpec((None, tr, C), lambda p, i, c_ref: (p, i, 0))],
        out_specs=pl.BlockSpec((None, tr, C), lambda p, i, c_ref: (p, i, 0)))
    return _pcall(body, name=name, grid_spec=grid_spec, out_shape=jax.ShapeDtypeStruct((nchip, R, C), BF),
                  compiler_params=_params())(cidx, g, land)


def _sum_chips(h, land, chipidx, *, name):
    _, R, C = h.shape
    tr = _tile(R, 512, 16)

    def body(c_ref, h_ref, l_ref, o_ref):
        acc = h_ref[...].astype(F32)
        for j in range(3):
            acc = acc + l_ref[j].astype(F32)
        o_ref[...] = acc
    grid_spec = pltpu.PrefetchScalarGridSpec(
        num_scalar_prefetch=1, grid=(R // tr,),
        in_specs=[pl.BlockSpec((None, tr, C), lambda i, c_ref: (c_ref[0], i, 0)),
                  pl.BlockSpec((3, tr, C), lambda i, c_ref: (0, i, 0))],
        out_specs=pl.BlockSpec((tr, C), lambda i, c_ref: (i, 0)))
    return _pcall(body, name=name, grid_spec=grid_spec, out_shape=jax.ShapeDtypeStruct((R, C), F32),
                  compiler_params=_params())(chipidx, h, land)


ROW_OFF = {}
_r = 0
for _n in BIG:
    ROW_OFF[_n] = _r
    _r += BIG_ROWS[_n]


def _pack_shards(shards, l):
    return jnp.concatenate([(shards[n][l].T if n in COL_SHARDED else shards[n][l]).astype(BF) for n in BIG], axis=0)


def _unpack_weights(full8):
    def whole(n):
        return full8[:, ROW_OFF[n]:ROW_OFF[n] + BIG_ROWS[n], :].reshape(N_DEV * BIG_ROWS[n], 1024)
    wt_in = whole("w_in")
    return {"wt_in": jnp.concatenate([wt_in[V_END:], wt_in[:V_END]], axis=0),
            "proj_a": whole("proj_a"), "proj_b": whole("proj_b"), "w_out": whole("w_out"),
            "wt_gu": jnp.concatenate([whole("ffn_w_gate"), whole("ffn_w_up")], axis=0), "w_down": whole("ffn_w_down")}


def _pack_grads(wg):
    dwt_in = jnp.concatenate([wg["wt_in"][P_Q:], wg["wt_in"][:P_Q]], axis=0)
    parts = [dwt_in, wg["proj_a"], wg["proj_b"], wg["w_out"], wg["wt_gu"][:FFN_DIM], wg["wt_gu"][FFN_DIM:], wg["w_down"]]
    blocks = jnp.concatenate([p.reshape(N_DEV, BIG_ROWS[n], 1024) for n, p in zip(BIG, parts)], axis=1)
    return blocks.reshape(4, 2, LAYER_ROWS, 1024).transpose(1, 0, 2, 3)


def _unpack_shard_grads(gs):
    out = {}
    for n in BIG:
        blk = gs[ROW_OFF[n]:ROW_OFF[n] + BIG_ROWS[n]]
        out[n] = blk.T if n in COL_SHARDED else blk
    return out


def _rope_setup(positions):
    S = positions.shape[0]
    inv = ROPE_THETA ** (-jnp.arange(0, ROT_DIM, 2, dtype=F32) / ROT_DIM)
    lane = np.arange(128) % HEAD_DIM
    half = ROT_DIM // 2
    inv_row = jnp.where(lane < ROT_DIM, jnp.tile(inv, 128 // half), 0.0)[None, :].astype(F32)
    m1_row = jnp.asarray((lane < half).astype(np.float32))[None, :]
    m2_row = jnp.asarray(((lane >= half) & (lane < ROT_DIM)).astype(np.float32))[None, :]
    return _rope_tables(positions.astype(F32).reshape(S, 1), inv_row, m1_row, m2_row)


def _layer_fwd(l, x, mod_l, W, small, rope, tie=None, mid=None):
    rc, rs1, rs2 = rope
    sh1, sc1, g1, sh2, sc2, g2 = [mod_l[i * D_MODEL:(i + 1) * D_MODEL][None, :] for i in range(6)]
    nw1, nw2 = small["norm1_w"][l][None, :], small["norm2_w"][l][None, :]
    h = _normmod_fwd(x, nw1, sc1, sh1, name=f"normmod1_fwd{l}")
    proj = _mm(h, W["wt_in"], nt=True, out_dtype=F32, name=f"mm_in{l}", after=tie, tn_cap=768)
    q_r, kv_r = _rope_fwd(proj, rc, rs1, rs2, name=f"rope_fwd{l}")
    qh, kh, vh = _to_heads(q_r, N_Q_HEADS), _to_heads(kv_r[:, :128], N_KV_HEADS), _to_heads(kv_r[:, 128:], N_KV_HEADS)
    sink_rows = jnp.repeat(small["attn_sinks"][l].reshape(N_KV_HEADS, Q_PER_KV), ATTN_BLOCK, axis=1)[..., None]
    y_attn = _from_heads(_attn_fwd(qh, kh, vh, sink_rows, name=f"attn_fwd{l}"))
    lnw, lnb = small["sgu_ln_w"][l][None, :], small["sgu_ln_b"][l][None, :]
    sgu_bt = small["sgu_b"][l].T
    y_sgu = _sgu_fwd(proj, lnw, lnb, small["sgu_w"][l], sgu_bt, name=f"sgu_fwd{l}")
    a_br = _mm(y_sgu, W["proj_a"], nt=False, out_dtype=BF, name=f"mm_pa{l}", after=None if mid is None else mid(y_sgu))
    b_br = _mm(y_attn, W["proj_b"], nt=False, out_dtype=BF, name=f"mm_pb{l}")
    merged = _merge_fwd(a_br, b_br, proj, name=f"merge_fwd{l}")
    x1, o1 = _mm(merged, W["w_out"], nt=False, out_dtype=F32, name=f"mm_out{l}", res=x, gvec=g1)
    h2 = _normmod_fwd(x1, nw2, sc2, sh2, name=f"normmod2_fwd{l}")
    au = _mm(h2, W["wt_gu"], nt=True, out_dtype=F32, name=f"mm_gu{l}", tn_cap=1408)
    cw, cb = small["ffn_conv_w"][l], small["ffn_conv_b"][l][None, :]
    hf = _ffn_act_fwd(au, cw, cb, name=f"ffn_act_fwd{l}")
    x2, o2 = _mm(hf, W["w_down"], nt=False, out_dtype=F32, name=f"mm_down{l}", res=x1, gvec=g2)
    saved = dict(x=x, h=h, proj=proj, qh=qh, kh=kh, vh=vh, sink_rows=sink_rows, y_attn=y_attn, y_sgu=y_sgu,
                 a_br=a_br, b_br=b_br, merged=merged, x1=x1, o1=o1, h2=h2, au=au, hf=hf, o2=o2)
    return x2, saved


def _layer_bwd(l, dx, mod_l, W, small, rope, sv, tie=None, mid=None):
    rc, rs1, rs2 = rope
    sh1, sc1, g1, sh2, sc2, g2 = [mod_l[i * D_MODEL:(i + 1) * D_MODEL][None, :] for i in range(6)]
    nw1, nw2 = small["norm1_w"][l][None, :], small["norm2_w"][l][None, :]
    cw, cb = small["ffn_conv_w"][l], small["ffn_conv_b"][l][None, :]
    lnw, lnb = small["sgu_ln_w"][l][None, :], small["sgu_ln_b"][l][None, :]
    sgu_bt = small["sgu_b"][l].T
    wg = {}
    do2, dg2 = _scale_reduce(dx, sv["o2"], g2, name=f"scale2_{l}", after=tie)
    dhf = _mm(do2, W["w_down"], nt=True, out_dtype=F32, name=f"mm_down_dx{l}", tn_cap=1408)
    wg["w_down"] = _mm_tn(sv["hf"], do2, name=f"mm_down_dw{l}")
    dac, dup, dcw, dcb = _ffn_act_bwd_a(dhf, sv["au"], cw, cb, name=f"ffn_act_bwd_a{l}")
    da = _ffn_act_bwd_b(dac, cw, name=f"ffn_act_bwd_b{l}")
    dau = jnp.concatenate([da, dup], axis=1)
    dh2 = _mm(dau, W["wt_gu"], nt=False, out_dtype=F32, name=f"mm_gu_dx{l}", after=None if mid is None else mid(dau))
    wg["wt_gu"] = _mm_tn(dau, sv["h2"], name=f"mm_gu_dw{l}")
    dx1, dnw2, dsc2, dsh2 = _normmod_bwd(dh2, sv["x1"], nw2, sc2, sh2, dx, name=f"normmod2_bwd{l}")
    do1, dg1 = _scale_reduce(dx1, sv["o1"], g1, name=f"scale1_{l}")
    dmerged = _mm(do1, W["w_out"], nt=True, out_dtype=F32, name=f"mm_out_dx{l}")
    wg["w_out"] = _mm_tn(sv["merged"], do1, name=f"mm_out_dw{l}")
    d_a, d_b, dgates = _merge_bwd(dmerged, sv["a_br"], sv["b_br"], sv["proj"], name=f"merge_bwd{l}")
    dysgu = _mm(d_a, W["proj_a"], nt=True, out_dtype=F32, name=f"mm_pa_dx{l}")
    dyattn = _mm(d_b, W["proj_b"], nt=True, out_dtype=BF, name=f"mm_pb_dx{l}")
    wg["proj_a"] = _mm_tn(sv["y_sgu"], d_a, name=f"mm_pa_dw{l}")
    wg["proj_b"] = _mm_tn(sv["y_attn"], d_b, name=f"mm_pb_dw{l}")
    dz, dlnw, dlnb, dsguw, dsgubt = _sgu_bwd(dysgu, sv["proj"], lnw, lnb, small["sgu_w"][l], sgu_bt, name=f"sgu_bwd{l}")
    dqh, dkh, dvh, dsk = _attn_bwd(_to_heads(dyattn, N_Q_HEADS), sv["qh"], sv["kh"], sv["vh"], sv["sink_rows"],
                                   name=f"attn_bwd{l}")
    dkv = jnp.concatenate([_from_heads(dkh), _from_heads(dvh)], axis=1)
    dqkv = _rope_bwd(_from_heads(dqh), dkv, rc, rs1, rs2, name=f"rope_bwd{l}")
    dproj = jnp.concatenate([dz, dgates, dqkv], axis=1)
    dh = _mm(dproj, W["wt_in"], nt=False, out_dtype=F32, name=f"mm_in_dx{l}")
    wg["wt_in"] = _mm_tn(dproj, sv["h"], name=f"mm_in_dw{l}")
    dx0, dnw1, dsc1, dsh1 = _normmod_bwd(dh, sv["x"], nw1, sc1, sh1, dx1, name=f"normmod1_bwd{l}")
    dmod = jnp.concatenate([dsh1, dsc1, dg1, dsh2, dsc2, dg2], axis=1)[0]
    sg = {"norm1_w": dnw1[0], "norm2_w": dnw2[0], "attn_sinks": dsk[:, :, 0].reshape(N_Q_HEADS),
          "sgu_ln_w": dlnw[0], "sgu_ln_b": dlnb[0], "sgu_w": dsguw, "sgu_b": dsgubt.T,
          "ffn_conv_w": dcw, "ffn_conv_b": dcb[0]}
    return dx0, wg, sg, dmod


SMALL = ("ada_b", "norm1_w", "attn_sinks", "sgu_ln_w", "sgu_ln_b", "sgu_w", "sgu_b", "norm2_w", "ffn_conv_b", "final_norm_w")
WEIGHT_ORDER = ("ada_w", "ada_b", "norm1_w", "w_in", "attn_sinks", "sgu_ln_w", "sgu_ln_b", "sgu_w", "sgu_b", "proj_a", "proj_b",
                "w_out", "norm2_w", "ffn_w_gate", "ffn_w_up", "ffn_conv_w", "ffn_conv_b", "ffn_w_down", "final_norm_w")


def _flat_pack(arrs, rows):
    flat = jnp.concatenate([a.reshape(-1) for a in arrs])
    return jnp.pad(flat, (0, rows * 1024 - flat.shape[0])).reshape(rows, 1024)


def _flat_unpack(buf, shapes):
    flat = buf.reshape(-1)
    out, o = [], 0
    for s in shapes:
        n = int(np.prod(s))
        out.append(flat[o:o + n].reshape(s))
        o += n
    return out


def _adam2d(w, g, m, v, *, name):
    shp = w.shape
    r2 = (int(np.prod(shp[:-1])), shp[-1]) if len(shp) > 1 else (1, shp[0])
    d, mn, vn = _adamw(w.reshape(r2), g.reshape(r2), m.reshape(r2), v.reshape(r2), name=name)
    return d.reshape(shp), mn.reshape(shp), vn.reshape(shp)


def kernel(x, c, positions, ada_w, ada_b, norm1_w, w_in, attn_sinks, sgu_ln_w, sgu_ln_b, sgu_w, sgu_b, proj_a, proj_b, w_out, norm2_w, ffn_w_gate, ffn_w_up, ffn_conv_w, ffn_conv_b, ffn_w_down, final_norm_w, loss_target, m_ada_w, m_ada_b, m_norm1_w, m_w_in, m_attn_sinks, m_sgu_ln_w, m_sgu_ln_b, m_sgu_w, m_sgu_b, m_proj_a, m_proj_b, m_w_out, m_norm2_w, m_ffn_w_gate, m_ffn_w_up, m_ffn_conv_w, m_ffn_conv_b, m_ffn_w_down, m_final_norm_w, v_ada_w, v_ada_b, v_norm1_w, v_w_in, v_attn_sinks, v_sgu_ln_w, v_sgu_ln_b, v_sgu_w, v_sgu_b, v_proj_a, v_proj_b, v_w_out, v_norm2_w, v_ffn_w_gate, v_ffn_w_up, v_ffn_conv_w, v_ffn_conv_b, v_ffn_w_down, v_final_norm_w):
    wts = dict(ada_w=ada_w, ada_b=ada_b, norm1_w=norm1_w, w_in=w_in, attn_sinks=attn_sinks, sgu_ln_w=sgu_ln_w,
               sgu_ln_b=sgu_ln_b, sgu_w=sgu_w, sgu_b=sgu_b, proj_a=proj_a, proj_b=proj_b, w_out=w_out, norm2_w=norm2_w,
               ffn_w_gate=ffn_w_gate, ffn_w_up=ffn_w_up, ffn_conv_w=ffn_conv_w, ffn_conv_b=ffn_conv_b,
               ffn_w_down=ffn_w_down, final_norm_w=final_norm_w)
    mom = dict(ada_w=m_ada_w, ada_b=m_ada_b, norm1_w=m_norm1_w, w_in=m_w_in, attn_sinks=m_attn_sinks, sgu_ln_w=m_sgu_ln_w,
               sgu_ln_b=m_sgu_ln_b, sgu_w=m_sgu_w, sgu_b=m_sgu_b, proj_a=m_proj_a, proj_b=m_proj_b, w_out=m_w_out,
               norm2_w=m_norm2_w, ffn_w_gate=m_ffn_w_gate, ffn_w_up=m_ffn_w_up, ffn_conv_w=m_ffn_conv_w,
               ffn_conv_b=m_ffn_conv_b, ffn_w_down=m_ffn_w_down, final_norm_w=m_final_norm_w)
    var = dict(ada_w=v_ada_w, ada_b=v_ada_b, norm1_w=v_norm1_w, w_in=v_w_in, attn_sinks=v_attn_sinks, sgu_ln_w=v_sgu_ln_w,
               sgu_ln_b=v_sgu_ln_b, sgu_w=v_sgu_w, sgu_b=v_sgu_b, proj_a=v_proj_a, proj_b=v_proj_b, w_out=v_w_out,
               norm2_w=v_norm2_w, ffn_w_gate=v_ffn_w_gate, ffn_w_up=v_ffn_w_up, ffn_conv_w=v_ffn_conv_w,
               ffn_conv_b=v_ffn_conv_b, ffn_w_down=v_ffn_w_down, final_norm_w=v_final_norm_w)
    me = 4 * lax.axis_index("x") + 2 * lax.axis_index("y") + lax.axis_index("c")
    ada_cols = ada_w.shape[2]

    c_all = _all_gather(jnp.broadcast_to(c, (8, D_MODEL)), name="ag_c")[:, 0, :]
    prod = _ada_fwd(c_all, ada_w)
    prod_all = _all_gather(prod, name="ag_mod")
    mine = lax.dynamic_index_in_dim(prod_all, me, axis=1, keepdims=False)
    mod = jnp.stack([mine[:, l * ada_cols:(l + 1) * ada_cols].reshape(-1) for l in range(DEPTH)]) + ada_b

    conv_cols = ffn_conv_w.shape[2]
    conv_all = _all_gather(_flat_pack([ffn_conv_w], 8), name="ag_conv")
    conv_full = jnp.stack([a.reshape(DEPTH, 3, conv_cols) for a in
                           [conv_all[j].reshape(-1)[:DEPTH * 3 * conv_cols] for j in range(N_DEV)]], axis=2)
    conv_full = conv_full.reshape(DEPTH, 3, FFN_DIM)
    small = {n: wts[n] for n in SMALL}
    small["ffn_conv_w"] = conv_full

    mx, my, mc = _coords()
    cidx = jnp.reshape(mc, (1,)).astype(jnp.int32)
    chipidx = jnp.reshape(2 * mx + my, (1,)).astype(jnp.int32)
    packed = [_pack_shards(wts, l) for l in range(DEPTH)]
    W0 = _unpack_weights(_all_gather(packed[0], name="ag_w0"))
    land = lax.dynamic_update_slice(jnp.zeros((2, 4, LAYER_ROWS, 1024), BF), packed[1][None, None],
                                    (mc, 2 * mx + my, 0, 0))
    sems, (src_t, land), tok = _rdma_start([packed[1], land], 3, _plan_gather_ici, name="ag_w1_ici_start")
    gather = {}

    def fwd_mid(after):
        _, land_m = _rdma_wait(sems, [src_t, land], 3, _plan_gather_ici, after, name="ag_w1_ici_wait")
        gather["sems"], (gather["land"],), tok2 = _rdma_start([land_m], 1, _plan_gather_d2d, name="ag_w1_d2d_start")
        return tok2

    rope = _rope_setup(positions[0])
    x0 = x[0]
    x1, sv0 = _layer_fwd(0, x0, mod[0], W0, small, rope, tie=tok, mid=fwd_mid)
    (land,) = _rdma_wait(gather["sems"], [gather["land"]], 1, _plan_gather_d2d, x1, name="ag_w1_d2d_wait")
    W1 = _unpack_weights(land.transpose(1, 0, 2, 3).reshape(N_DEV, LAYER_ROWS, 1024))
    x2, sv1 = _layer_fwd(1, x1, mod[1], W1, small, rope)
    dx2, dfw, loss_tile = _head(x2, final_norm_w[None, :], loss_target[0])
    loss = lax.psum(loss_tile[0, 0], ("x", "y", "c"))

    dx1, wg1, sg1, dmod1 = _layer_bwd(1, dx2, mod[1], W1, small, rope, sv1)
    red = {}

    def reduce_start(wg, l):
        empty = jnp.zeros((4, LAYER_ROWS, 1024), BF)
        red["s1"], red["b1"], t = _rdma_start([_pack_grads(wg), empty], 1, _plan_reduce_d2d, name=f"rs{l}_d2d_start")
        return t

    def reduce_mid(after, l):
        g_t, land_a = _rdma_wait(red["s1"], red["b1"], 1, _plan_reduce_d2d, after, name=f"rs{l}_d2d_wait")
        h = _sum_pair(g_t, land_a, cidx, name=f"rs{l}_sum_pair")
        red["s2"], red["b2"], t = _rdma_start([h, jnp.zeros((3, LAYER_ROWS, 1024), BF)], 3, _plan_reduce_ici,
                                              name=f"rs{l}_ici_start")
        return t

    def reduce_end(after, l):
        h_t, land_b = _rdma_wait(red["s2"], red["b2"], 3, _plan_reduce_ici, after, name=f"rs{l}_ici_wait")
        return _unpack_shard_grads(_sum_chips(h_t, land_b, chipidx, name=f"rs{l}_sum_chips"))

    tok = reduce_start(wg1, 1)
    grad_x, wg0, sg0, dmod0 = _layer_bwd(0, dx1, mod[0], W0, small, rope, sv0, tie=tok, mid=lambda a: reduce_mid(a, 1))
    shard1 = reduce_end(grad_x, 1)
    tok = reduce_start(wg0, 0)
    tok = reduce_mid(tok, 0)
    shard0 = reduce_end(tok, 0)
    grads = {n: jnp.stack([shard0[n], shard1[n]]) for n in BIG}
    sg = {n: jnp.stack([sg0[n], sg1[n]]) for n in sg0}
    sg["final_norm_w"] = dfw[0]
    dmod = jnp.stack([dmod0, dmod1])

    small_names = [n for n in SMALL if n != "ada_b"] + ["ffn_conv_w"]
    small_shapes = [(DEPTH, 6 * D_MODEL)] + [sg[n].shape for n in small_names]
    n_small = sum(int(np.prod(s)) for s in small_shapes)
    rows = -(-n_small // 1024 // 8) * 8
    sm_all = _all_gather(_flat_pack([dmod] + [sg[n] for n in small_names], rows), name="ag_small")
    sm_sum = _flat_unpack(_sum8(sm_all, name="sum_small"), small_shapes)
    grads["ada_b"] = sm_sum[0]
    for n, gsum in zip(small_names, sm_sum[1:]):
        grads[n] = gsum
    grads["ffn_conv_w"] = lax.dynamic_slice_in_dim(grads["ffn_conv_w"], me * conv_cols, conv_cols, axis=2)
    dmod_all = sm_all[:, :DEPTH * 6, :].reshape(N_DEV, DEPTH, 6 * D_MODEL)
    dm_mine = lax.dynamic_slice_in_dim(dmod_all, me * ada_cols, ada_cols, axis=2).transpose(1, 0, 2)
    dm_mine = jnp.pad(dm_mine, ((0, 0), (0, 8), (0, 0)))
    grads["ada_w"] = _ada_bwd(jnp.pad(c_all, ((0, 8), (0, 0))), dm_mine)

    packed_small = [n for n in SMALL]
    pshapes = [wts[n].shape for n in packed_small]
    prow = -(-sum(int(np.prod(s)) for s in pshapes) // 1024 // 8) * 8
    pk = lambda d: _flat_pack([d[n] for n in packed_small], prow)
    d_s, m_s, v_s = _adamw(pk(wts), pk(grads), pk(mom), pk(var), name="adamw_small")
    delta, new_m, new_v = {}, {}, {}
    for n, dd, mm, vv in zip(packed_small, _flat_unpack(d_s, pshapes), _flat_unpack(m_s, pshapes), _flat_unpack(v_s, pshapes)):
        delta[n], new_m[n], new_v[n] = dd, mm, vv
    for n in WEIGHT_ORDER:
        if n not in delta:
            delta[n], new_m[n], new_v[n] = _adam2d(wts[n], grads[n], mom[n], var[n], name=f"adamw_{n}")
    return (loss, grad_x[None], *[grads[n] for n in WEIGHT_ORDER], *[delta[n] for n in WEIGHT_ORDER],
            *[new_m[n] for n in WEIGHT_ORDER], *[new_v[n] for n in WEIGHT_ORDER])
```

```python
import functools

import jax
import jax.numpy as jnp
import numpy as np
from jax import lax
from jax.experimental import pallas as pl
from jax.experimental.pallas import tpu as pltpu

F32 = jnp.float32
BF = jnp.bfloat16

N_DEV = 8
D_MODEL = 1024
DEPTH = 2
N_Q_HEADS = 16
N_KV_HEADS = 2
HEAD_DIM = 64
Q_PER_KV = N_Q_HEADS // N_KV_HEADS
ATTN_BLOCK = 128
ROPE_THETA = 500000.0
ROT_DIM = HEAD_DIM // 4
SGU_WIDTH = 1024
SGU_GROUPS = 8
SGU_CHUNK = 128
FFN_DIM = 2816
NORM_EPS = 1e-6
Q_END = N_Q_HEADS * HEAD_DIM
K_END = Q_END + N_KV_HEADS * HEAD_DIM
V_END = K_END + N_KV_HEADS * HEAD_DIM
Z_END = V_END + 2 * SGU_WIDTH
IN_COLS = Z_END + 2 * D_MODEL
P_Z, P_G, P_Q, P_K, P_V = 0, 2048, 4096, 5120, 5248

ADAM_LR = 0.001
ADAM_B1 = 0.9
ADAM_B2 = 0.999
ADAM_EPS = 1e-08
ADAM_WD = 0.01
ADAM_STEP = 10

VMEM_LIMIT_BYTES = 56 * 1024 * 1024

BIG = ("w_in", "proj_a", "proj_b", "w_out", "ffn_w_gate", "ffn_w_up", "ffn_w_down")
COL_SHARDED = ("w_in", "ffn_w_gate", "ffn_w_up")
BIG_SHAPE = {"w_in": (D_MODEL, IN_COLS), "proj_a": (SGU_WIDTH, D_MODEL), "proj_b": (Q_END, D_MODEL),
             "w_out": (D_MODEL, D_MODEL), "ffn_w_gate": (D_MODEL, FFN_DIM), "ffn_w_up": (D_MODEL, FFN_DIM),
             "ffn_w_down": (FFN_DIM, D_MODEL)}
BIG_ROWS = {n: BIG_SHAPE[n][0] * BIG_SHAPE[n][1] // N_DEV // 1024 for n in BIG}
LAYER_ROWS = sum(BIG_ROWS.values())


def _pcall(body, **kw):
    return pl.pallas_call(body, **kw)


def _params(**kw):
    return pltpu.CompilerParams(vmem_limit_bytes=VMEM_LIMIT_BYTES, **kw)


def _tile(n, cap, unit=128):
    if n <= cap:
        return n
    best = 0
    t = unit
    while t <= cap:
        if n % t == 0:
            best = t
        t += unit
    assert best, (n, cap, unit)
    return best


def _mm(a, b, *, nt, out_dtype, name, res=None, gvec=None, after=None, tm=512, tn_cap=1024):
    M, K = a.shape
    N = b.shape[0] if nt else b.shape[1]
    tm = _tile(M, tm, 8)
    tn = _tile(N, tn_cap)
    dn = (((1,), (1,)), ((), ())) if nt else (((1,), (0,)), ((), ()))
    b_spec = pl.BlockSpec((tn, K), lambda i, j: (j, 0)) if nt else pl.BlockSpec((K, tn), lambda i, j: (0, j))
    o_spec = pl.BlockSpec((tm, tn), lambda i, j: (i, j))
    if res is None:
        extra = [] if after is None else [after]

        def body(a_ref, b_ref, *rest):
            o_ref = rest[-1]
            acc = lax.dot_general(a_ref[...].astype(BF), b_ref[...].astype(BF), dn, preferred_element_type=F32)
            o_ref[...] = acc.astype(out_dtype)
        return _pcall(body, name=name, grid=(M // tm, N // tn),
                      in_specs=[pl.BlockSpec((tm, K), lambda i, j: (i, 0)), b_spec] + [ANY] * len(extra), out_specs=o_spec,
                      out_shape=jax.ShapeDtypeStruct((M, N), out_dtype), compiler_params=_params())(a, b, *extra)

    def body_res(a_ref, b_ref, r_ref, g_ref, o_ref, acc_ref):
        acc = lax.dot_general(a_ref[...].astype(BF), b_ref[...].astype(BF), dn, preferred_element_type=F32)
        acc_ref[...] = acc
        o_ref[...] = r_ref[...] + g_ref[...] * acc
    return _pcall(body_res, name=name, grid=(M // tm, N // tn),
                  in_specs=[pl.BlockSpec((tm, K), lambda i, j: (i, 0)), b_spec, o_spec,
                            pl.BlockSpec((1, tn), lambda i, j: (0, j))],
                  out_specs=[o_spec, o_spec],
                  out_shape=[jax.ShapeDtypeStruct((M, N), F32), jax.ShapeDtypeStruct((M, N), F32)],
                  compiler_params=_params())(a, b, res, gvec)


def _mm_tn(a, b, *, name, out_dtype=BF, tk=512, tm_cap=1408, tn_cap=1024):
    S, M = a.shape
    N = b.shape[1]
    tk = _tile(S, tk, 8)
    tm = _tile(M, tm_cap)
    tn = _tile(N, tn_cap)
    nk = S // tk

    def body(a_ref, b_ref, o_ref, acc_ref):
        k = pl.program_id(2)

        @pl.when(k == 0)
        def _():
            acc_ref[...] = jnp.zeros_like(acc_ref)
        acc_ref[...] += lax.dot_general(a_ref[...].astype(BF), b_ref[...].astype(BF), (((0,), (0,)), ((), ())),
                                        preferred_element_type=F32)

        @pl.when(k == nk - 1)
        def _():
            o_ref[...] = acc_ref[...].astype(out_dtype)
    return _pcall(body, name=name, grid=(M // tm, N // tn, nk),
                  in_specs=[pl.BlockSpec((tk, tm), lambda i, j, k: (k, i)),
                            pl.BlockSpec((tk, tn), lambda i, j, k: (k, j))],
                  out_specs=pl.BlockSpec((tm, tn), lambda i, j, k: (i, j)),
                  out_shape=jax.ShapeDtypeStruct((M, N), out_dtype), scratch_shapes=[pltpu.VMEM((tm, tn), F32)],
                  compiler_params=_params())(a, b)


def _rms(x, w):
    return x * lax.rsqrt(jnp.mean(x * x, axis=-1, keepdims=True) + NORM_EPS) * w


def _normmod_fn(x, nw, sc, sh):
    return _rms(x, nw) * (1.0 + sc) + sh


def _gelu(x):
    return 0.5 * x * (1.0 + lax.erf(x * (2.0 ** -0.5)))


def _ln_gelu_fn(zv, w, b):
    v = _gelu(zv)
    mu = jnp.mean(v, axis=-1, keepdims=True)
    var = jnp.mean(jnp.square(v - mu), axis=-1, keepdims=True)
    return (v - mu) * lax.rsqrt(var + NORM_EPS) * w + b


def _sigmoid(x):
    return 1.0 / (1.0 + jnp.exp(-x))


def _row_spec(tm, n):
    return pl.BlockSpec((tm, n), lambda i: (i, 0))


def _vec_spec(n):
    return pl.BlockSpec((1, n), lambda i: (0, 0))


def _acc(ref, val):
    @pl.when(pl.program_id(0) == 0)
    def _():
        ref[...] = jnp.zeros_like(ref)
    ref[...] += val


def _normmod_fwd(x, nw, sc, sh, *, name, tm=512):
    S, Dm = x.shape
    tm = _tile(S, tm, 8)

    def body(x_ref, nw_ref, sc_ref, sh_ref, o_ref):
        o_ref[...] = _normmod_fn(x_ref[...], nw_ref[...], sc_ref[...], sh_ref[...]).astype(BF)
    return _pcall(body, name=name, grid=(S // tm,),
                  in_specs=[_row_spec(tm, Dm), _vec_spec(Dm), _vec_spec(Dm), _vec_spec(Dm)],
                  out_specs=_row_spec(tm, Dm), out_shape=jax.ShapeDtypeStruct((S, Dm), BF),
                  compiler_params=_params())(x, nw, sc, sh)


def _normmod_bwd(dh, x, nw, sc, sh, dres, *, name, tm=256):
    S, Dm = x.shape
    tm = _tile(S, tm, 8)

    def body(dh_ref, x_ref, nw_ref, sc_ref, sh_ref, dres_ref, dx_ref, dnw_ref, dsc_ref, dsh_ref):
        _, vjp = jax.vjp(_normmod_fn, x_ref[...], nw_ref[...], sc_ref[...], sh_ref[...])
        dx, dnw, dsc, dsh = vjp(dh_ref[...])
        dx_ref[...] = dres_ref[...] + dx
        _acc(dnw_ref, dnw)
        _acc(dsc_ref, dsc)
        _acc(dsh_ref, dsh)
    vec = jax.ShapeDtypeStruct((1, Dm), F32)
    return _pcall(body, name=name, grid=(S // tm,),
                  in_specs=[_row_spec(tm, Dm), _row_spec(tm, Dm), _vec_spec(Dm), _vec_spec(Dm), _vec_spec(Dm),
                            _row_spec(tm, Dm)],
                  out_specs=[_row_spec(tm, Dm), _vec_spec(Dm), _vec_spec(Dm), _vec_spec(Dm)],
                  out_shape=[jax.ShapeDtypeStruct((S, Dm), F32), vec, vec, vec],
                  compiler_params=_params())(dh, x, nw, sc, sh, dres)


def _scale_reduce(dx, o, g, *, name, after=None, tm=512):
    S, Dm = dx.shape
    tm = _tile(S, tm, 8)
    extra = [] if after is None else [after]

    def body(dx_ref, o_ref, g_ref, *rest):
        do_ref, dg_ref = rest[-2:]
        dxv = dx_ref[...]
        do_ref[...] = (dxv * g_ref[...]).astype(BF)
        _acc(dg_ref, jnp.sum(dxv * o_ref[...], axis=0, keepdims=True))
    return _pcall(body, name=name, grid=(S // tm,),
                  in_specs=[_row_spec(tm, Dm), _row_spec(tm, Dm), _vec_spec(Dm)] + [ANY] * len(extra),
                  out_specs=[_row_spec(tm, Dm), _vec_spec(Dm)],
                  out_shape=[jax.ShapeDtypeStruct((S, Dm), BF), jax.ShapeDtypeStruct((1, Dm), F32)],
                  compiler_params=_params())(dx, o, g, *extra)


def _head(x, fw, target, *, tm=256):
    S, Dm = x.shape
    tm = _tile(S, tm, 8)

    def body(x_ref, fw_ref, t_ref, dx_ref, dfw_ref, loss_ref):
        y, vjp = jax.vjp(_rms, x_ref[...], fw_ref[...])
        err = y - t_ref[...]
        dx, dfw = vjp(err * (1.0 / Dm))
        dx_ref[...] = dx
        _acc(dfw_ref, dfw)
        part = 0.5 * jnp.sum(jnp.mean(err * err, axis=-1, keepdims=True), axis=0, keepdims=True)
        _acc(loss_ref, jnp.broadcast_to(part, (8, 128)))
    return _pcall(body, name="head", grid=(S // tm,),
                  in_specs=[_row_spec(tm, Dm), _vec_spec(Dm), _row_spec(tm, Dm)],
                  out_specs=[_row_spec(tm, Dm), _vec_spec(Dm), pl.BlockSpec((8, 128), lambda i: (0, 0))],
                  out_shape=[jax.ShapeDtypeStruct((S, Dm), F32), jax.ShapeDtypeStruct((1, Dm), F32),
                             jax.ShapeDtypeStruct((8, 128), F32)],
                  compiler_params=_params())(x, fw, target)


def _tril_mask():
    r = lax.broadcasted_iota(jnp.int32, (SGU_CHUNK, SGU_CHUNK), 0)
    c = lax.broadcasted_iota(jnp.int32, (SGU_CHUNK, SGU_CHUNK), 1)
    return c <= r


def _sgu_fwd(proj, lnw, lnb, w, b_t, *, name, tm=256):
    S = proj.shape[0]
    tm = _tile(S, tm, SGU_CHUNK)

    def body(zu_ref, zv_ref, lnw_ref, lnb_ref, w_ref, bt_ref, o_ref):
        u = _gelu(zu_ref[...])
        vn = _ln_gelu_fn(zv_ref[...], lnw_ref[...], lnb_ref[...]).astype(BF)
        mask = _tril_mask()
        for g in range(SGU_GROUPS):
            wm = jnp.where(mask, w_ref[g], 0.0).astype(BF)
            cols = slice(g * 128, (g + 1) * 128)
            for ci in range(tm // SGU_CHUNK):
                rows = slice(ci * SGU_CHUNK, (ci + 1) * SGU_CHUNK)
                f = jnp.dot(wm, vn[rows, cols], preferred_element_type=F32) + bt_ref[:, g:g + 1]
                o_ref[rows, cols] = (u[rows, cols] * f).astype(BF)
    return _pcall(body, name=name, grid=(S // tm,),
                  in_specs=[pl.BlockSpec((tm, SGU_WIDTH), lambda i: (i, 0)), pl.BlockSpec((tm, SGU_WIDTH), lambda i: (i, 1)),
                            _vec_spec(SGU_WIDTH), _vec_spec(SGU_WIDTH),
                            pl.BlockSpec((SGU_GROUPS, 128, 128), lambda i: (0, 0, 0)),
                            pl.BlockSpec((128, SGU_GROUPS), lambda i: (0, 0))],
                  out_specs=_row_spec(tm, SGU_WIDTH), out_shape=jax.ShapeDtypeStruct((S, SGU_WIDTH), BF),
                  compiler_params=_params())(proj, proj, lnw, lnb, w, b_t)


def _sgu_bwd(dy, proj, lnw, lnb, w, b_t, *, name, tm=256):
    S = proj.shape[0]
    tm = _tile(S, tm, SGU_CHUNK)

    def body(dy_ref, zu_ref, zv_ref, lnw_ref, lnb_ref, w_ref, bt_ref, dz_ref, dlnw_ref, dlnb_ref, dw_ref, dbt_ref,
             f_s, dvn_s):
        first = pl.program_id(0) == 0

        @pl.when(first)
        def _():
            dw_ref[...] = jnp.zeros_like(dw_ref)
            dbt_ref[...] = jnp.zeros_like(dbt_ref)
        u, vjp_u = jax.vjp(_gelu, zu_ref[...])
        vn, vjp_v = jax.vjp(_ln_gelu_fn, zv_ref[...], lnw_ref[...], lnb_ref[...])
        vn = vn.astype(BF)
        dy_v = dy_ref[...]
        df = (dy_v * u).astype(BF)
        mask = _tril_mask()
        for g in range(SGU_GROUPS):
            wm = jnp.where(mask, w_ref[g], 0.0).astype(BF)
            cols = slice(g * 128, (g + 1) * 128)
            dwg = jnp.zeros((128, 128), F32)
            dbg = jnp.zeros((128, 1), F32)
            for ci in range(tm // SGU_CHUNK):
                rows = slice(ci * SGU_CHUNK, (ci + 1) * SGU_CHUNK)
                vn_c = vn[rows, cols]
                df_c = df[rows, cols]
                f_s[rows, cols] = jnp.dot(wm, vn_c, preferred_element_type=F32) + bt_ref[:, g:g + 1]
                dvn_s[rows, cols] = lax.dot_general(wm, df_c, (((0,), (0,)), ((), ())), preferred_element_type=F32)
                dwg = dwg + lax.dot_general(df_c, vn_c, (((1,), (1,)), ((), ())), preferred_element_type=F32)
                dbg = dbg + jnp.sum((dy_v[rows, cols] * u[rows, cols]), axis=1, keepdims=True)
            dw_ref[g] += jnp.where(mask, dwg, 0.0)
            dbt_ref[:, g:g + 1] += dbg
        (dzu,) = vjp_u(dy_v * f_s[...])
        dzv, dlnw, dlnb = vjp_v(dvn_s[...])
        dz_ref[:, :SGU_WIDTH] = dzu.astype(BF)
        dz_ref[:, SGU_WIDTH:] = dzv.astype(BF)
        _acc(dlnw_ref, dlnw)
        _acc(dlnb_ref, dlnb)
    vec = jax.ShapeDtypeStruct((1, SGU_WIDTH), F32)
    return _pcall(body, name=name, grid=(S // tm,),
                  in_specs=[_row_spec(tm, SGU_WIDTH),
                            pl.BlockSpec((tm, SGU_WIDTH), lambda i: (i, 0)), pl.BlockSpec((tm, SGU_WIDTH), lambda i: (i, 1)),
                            _vec_spec(SGU_WIDTH), _vec_spec(SGU_WIDTH),
                            pl.BlockSpec((SGU_GROUPS, 128, 128), lambda i: (0, 0, 0)),
                            pl.BlockSpec((128, SGU_GROUPS), lambda i: (0, 0))],
                  out_specs=[_row_spec(tm, 2 * SGU_WIDTH), _vec_spec(SGU_WIDTH), _vec_spec(SGU_WIDTH),
                             pl.BlockSpec((SGU_GROUPS, 128, 128), lambda i: (0, 0, 0)),
                             pl.BlockSpec((128, SGU_GROUPS), lambda i: (0, 0))],
                  out_shape=[jax.ShapeDtypeStruct((S, 2 * SGU_WIDTH), BF), vec, vec,
                             jax.ShapeDtypeStruct((SGU_GROUPS, 128, 128), F32),
                             jax.ShapeDtypeStruct((128, SGU_GROUPS), F32)],
                  scratch_shapes=[pltpu.VMEM((tm, SGU_WIDTH), F32), pltpu.VMEM((tm, SGU_WIDTH), F32)],
                  compiler_params=_params())(dy, proj, proj, lnw, lnb, w, b_t)


def _merge_fwd(a, b, proj, *, name, tm=512):
    S, Dm = a.shape
    tm = _tile(S, tm, 8)
    ga_blk, gb_blk = P_G // Dm, P_G // Dm + 1

    def body(a_ref, b_ref, ga_ref, gb_ref, o_ref):
        o_ref[...] = (_sigmoid(ga_ref[...]) * a_ref[...].astype(F32)
                      + _sigmoid(gb_ref[...]) * b_ref[...].astype(F32)).astype(BF)
    return _pcall(body, name=name, grid=(S // tm,),
                  in_specs=[_row_spec(tm, Dm), _row_spec(tm, Dm), pl.BlockSpec((tm, Dm), lambda i: (i, ga_blk)),
                            pl.BlockSpec((tm, Dm), lambda i: (i, gb_blk))],
                  out_specs=_row_spec(tm, Dm), out_shape=jax.ShapeDtypeStruct((S, Dm), BF),
                  compiler_params=_params())(a, b, proj, proj)


def _merge_bwd(dm, a, b, proj, *, name, tm=512):
    S, Dm = a.shape
    tm = _tile(S, tm, 8)
    ga_blk, gb_blk = P_G // Dm, P_G // Dm + 1

    def body(dm_ref, a_ref, b_ref, ga_ref, gb_ref, da_ref, db_ref, dg_ref):
        dmv = dm_ref[...]
        sa = _sigmoid(ga_ref[...])
        sb = _sigmoid(gb_ref[...])
        da_ref[...] = (dmv * sa).astype(BF)
        db_ref[...] = (dmv * sb).astype(BF)
        dg_ref[:, :Dm] = (dmv * a_ref[...].astype(F32) * sa * (1.0 - sa)).astype(BF)
        dg_ref[:, Dm:] = (dmv * b_ref[...].astype(F32) * sb * (1.0 - sb)).astype(BF)
    return _pcall(body, name=name, grid=(S // tm,),
                  in_specs=[_row_spec(tm, Dm), _row_spec(tm, Dm), _row_spec(tm, Dm),
                            pl.BlockSpec((tm, Dm), lambda i: (i, ga_blk)), pl.BlockSpec((tm, Dm), lambda i: (i, gb_blk))],
                  out_specs=[_row_spec(tm, Dm), _row_spec(tm, Dm), _row_spec(tm, 2 * Dm)],
                  out_shape=[jax.ShapeDtypeStruct((S, Dm), BF), jax.ShapeDtypeStruct((S, Dm), BF),
                             jax.ShapeDtypeStruct((S, 2 * Dm), BF)],
                  compiler_params=_params())(dm, a, b, proj, proj)


def _shift_rows(a, halo, k, up):
    n = a.shape[0]
    r8 = lax.broadcasted_iota(jnp.int32, (8, a.shape[1]), 0)
    if not up:
        rolled = pltpu.roll(a, k, 0)
        patch = jnp.where(r8 < k, pltpu.roll(halo, k, 0), rolled[:8])
        return jnp.concatenate([patch, rolled[8:]], axis=0)
    rolled = pltpu.roll(a, n - k, 0)
    patch = jnp.where(r8 >= 8 - k, pltpu.roll(halo, 8 - k, 0), rolled[n - 8:])
    return jnp.concatenate([rolled[:n - 8], patch], axis=0)


def _conv_taps(a, halo):
    return _shift_rows(a, halo, 2, False), _shift_rows(a, halo, 1, False), a


def _ffn_act_fwd(au, cw, cb, *, name, tm=256):
    S = au.shape[0]
    Fd = FFN_DIM
    tm = _tile(S, tm, 8)
    hb = tm // 8

    def body(a_ref, up_ref, halo_ref, cw_ref, cb_ref, o_ref):
        halo = jnp.where(pl.program_id(0) > 0, halo_ref[...], 0.0)
        t0, t1, t2 = _conv_taps(a_ref[...], halo)
        ac = cb_ref[...] + cw_ref[0:1, :] * t0 + cw_ref[1:2, :] * t1 + cw_ref[2:3, :] * t2
        o_ref[...] = (ac * _sigmoid(ac) * up_ref[...]).astype(BF)
    return _pcall(body, name=name, grid=(S // tm,),
                  in_specs=[pl.BlockSpec((tm, Fd), lambda i: (i, 0)), pl.BlockSpec((tm, Fd), lambda i: (i, 1)),
                            pl.BlockSpec((8, Fd), lambda i: (jnp.maximum(i * hb - 1, 0), 0)),
                            pl.BlockSpec((3, Fd), lambda i: (0, 0)), _vec_spec(Fd)],
                  out_specs=_row_spec(tm, Fd), out_shape=jax.ShapeDtypeStruct((S, Fd), BF),
                  compiler_params=_params())(au, au, au, cw, cb)


def _ffn_act_bwd_a(dhf, au, cw, cb, *, name, tm=256):
    S = au.shape[0]
    Fd = FFN_DIM
    tm = _tile(S, tm, 8)
    hb = tm // 8

    def body(dhf_ref, a_ref, up_ref, halo_ref, cw_ref, cb_ref, dac_ref, dup_ref, dcw_ref, dcb_ref):
        halo = jnp.where(pl.program_id(0) > 0, halo_ref[...], 0.0)
        t0, t1, t2 = _conv_taps(a_ref[...], halo)
        ac = cb_ref[...] + cw_ref[0:1, :] * t0 + cw_ref[1:2, :] * t1 + cw_ref[2:3, :] * t2
        s = _sigmoid(ac)
        dhf_v = dhf_ref[...]
        dup_ref[...] = (dhf_v * ac * s).astype(BF)
        dac = dhf_v * up_ref[...] * (s * (1.0 + ac * (1.0 - s)))
        dac_ref[...] = dac
        _acc(dcb_ref, jnp.sum(dac, axis=0, keepdims=True))
        _acc(dcw_ref, jnp.concatenate([jnp.sum(dac * t0, axis=0, keepdims=True),
                                       jnp.sum(dac * t1, axis=0, keepdims=True),
                                       jnp.sum(dac * t2, axis=0, keepdims=True)], axis=0))
    return _pcall(body, name=name, grid=(S // tm,),
                  in_specs=[_row_spec(tm, Fd), pl.BlockSpec((tm, Fd), lambda i: (i, 0)),
                            pl.BlockSpec((tm, Fd), lambda i: (i, 1)),
                            pl.BlockSpec((8, Fd), lambda i: (jnp.maximum(i * hb - 1, 0), 0)),
                            pl.BlockSpec((3, Fd), lambda i: (0, 0)), _vec_spec(Fd)],
                  out_specs=[_row_spec(tm, Fd), _row_spec(tm, Fd), pl.BlockSpec((3, Fd), lambda i: (0, 0)), _vec_spec(Fd)],
                  out_shape=[jax.ShapeDtypeStruct((S, Fd), F32), jax.ShapeDtypeStruct((S, Fd), BF),
                             jax.ShapeDtypeStruct((3, Fd), F32), jax.ShapeDtypeStruct((1, Fd), F32)],
                  compiler_params=_params())(dhf, au, au, au, cw, cb)


def _ffn_act_bwd_b(dac, cw, *, name, tm=256):
    S, Fd = dac.shape
    tm = _tile(S, tm, 8)
    hb = tm // 8
    last = S // tm - 1

    def body(d_ref, halo_ref, cw_ref, o_ref):
        halo = jnp.where(pl.program_id(0) < last, halo_ref[...], 0.0)
        d = d_ref[...]
        o_ref[...] = (cw_ref[2:3, :] * d + cw_ref[1:2, :] * _shift_rows(d, halo, 1, True)
                      + cw_ref[0:1, :] * _shift_rows(d, halo, 2, True)).astype(BF)
    return _pcall(body, name=name, grid=(S // tm,),
                  in_specs=[_row_spec(tm, Fd), pl.BlockSpec((8, Fd), lambda i: (jnp.minimum((i + 1) * hb, S // 8 - 1), 0)),
                            pl.BlockSpec((3, Fd), lambda i: (0, 0))],
                  out_specs=_row_spec(tm, Fd), out_shape=jax.ShapeDtypeStruct((S, Fd), BF),
                  compiler_params=_params())(dac, dac, cw)


def _rope_tables(pos_col, inv_row, m1_row, m2_row):
    S = pos_col.shape[0]
    tm = _tile(S, 512, 8)

    def body(p_ref, inv_ref, m1_ref, m2_ref, c_ref, s1_ref, s2_ref):
        ang = p_ref[...] * inv_ref[...]
        sn = jnp.sin(ang)
        c_ref[...] = jnp.cos(ang)
        s1_ref[...] = -sn * m1_ref[...]
        s2_ref[...] = sn * m2_ref[...]
    sh = jax.ShapeDtypeStruct((S, 128), F32)
    return _pcall(body, name="rope_tables", grid=(S // tm,),
                  in_specs=[pl.BlockSpec((tm, 1), lambda i: (i, 0)), _vec_spec(128), _vec_spec(128), _vec_spec(128)],
                  out_specs=[_row_spec(tm, 128)] * 3, out_shape=[sh, sh, sh], compiler_params=_params())(
                      pos_col, inv_row, m1_row, m2_row)


def _rope_apply(x, c, s1, s2):
    outs = []
    for j in range(x.shape[1] // 128):
        xj = x[:, j * 128:(j + 1) * 128]
        outs.append(xj * c + pltpu.roll(xj, 120, 1) * s1 + pltpu.roll(xj, 8, 1) * s2)
    return outs[0] if len(outs) == 1 else jnp.concatenate(outs, axis=1)


def _rope_apply_t(d, c, s1, s2):
    outs = []
    for j in range(d.shape[1] // 128):
        dj = d[:, j * 128:(j + 1) * 128]
        outs.append(dj * c + pltpu.roll(dj * s1, 8, 1) + pltpu.roll(dj * s2, 120, 1))
    return outs[0] if len(outs) == 1 else jnp.concatenate(outs, axis=1)


def _rope_fwd(proj, c, s1, s2, *, name, tm=512):
    S = proj.shape[0]
    tm = _tile(S, tm, 8)

    def body(q_ref, k_ref, v_ref, c_ref, s1_ref, s2_ref, qo_ref, ko_ref, vo_ref):
        cv, s1v, s2v = c_ref[...], s1_ref[...], s2_ref[...]
        qo_ref[...] = _rope_apply(q_ref[...], cv, s1v, s2v).astype(BF)
        ko_ref[...] = _rope_apply(k_ref[...], cv, s1v, s2v).astype(BF)
        vo_ref[...] = v_ref[...].astype(BF)
    return _pcall(body, name=name, grid=(S // tm,),
                  in_specs=[pl.BlockSpec((tm, Q_END), lambda i: (i, P_Q // Q_END)),
                            pl.BlockSpec((tm, 128), lambda i: (i, P_K // 128)),
                            pl.BlockSpec((tm, 128), lambda i: (i, P_V // 128)),
                            _row_spec(tm, 128), _row_spec(tm, 128), _row_spec(tm, 128)],
                  out_specs=[_row_spec(tm, Q_END), _row_spec(tm, 128), _row_spec(tm, 128)],
                  out_shape=[jax.ShapeDtypeStruct((S, Q_END), BF), jax.ShapeDtypeStruct((S, 128), BF),
                             jax.ShapeDtypeStruct((S, 128), BF)],
                  compiler_params=_params())(proj, proj, proj, c, s1, s2)


def _rope_bwd(dq, dk, dv, c, s1, s2, *, name, tm=512):
    S = dq.shape[0]
    tm = _tile(S, tm, 8)

    def body(dq_ref, dk_ref, dv_ref, c_ref, s1_ref, s2_ref, o_ref):
        cv, s1v, s2v = c_ref[...], s1_ref[...], s2_ref[...]
        o_ref[:, :Q_END] = _rope_apply_t(dq_ref[...].astype(F32), cv, s1v, s2v).astype(BF)
        o_ref[:, Q_END:Q_END + 128] = _rope_apply_t(dk_ref[...], cv, s1v, s2v).astype(BF)
        o_ref[:, Q_END + 128:] = dv_ref[...].astype(BF)
    return _pcall(body, name=name, grid=(S // tm,),
                  in_specs=[_row_spec(tm, Q_END), _row_spec(tm, 128), _row_spec(tm, 128), _row_spec(tm, 128),
                            _row_spec(tm, 128), _row_spec(tm, 128)],
                  out_specs=_row_spec(tm, V_END), out_shape=jax.ShapeDtypeStruct((S, V_END), BF),
                  compiler_params=_params())(dq, dk, dv, c, s1, s2)


def _lane_lo(shape):
    return lax.broadcasted_iota(jnp.int32, shape, 1) < HEAD_DIM


def _stack_heads(x, g):
    lo = _lane_lo((ATTN_BLOCK, 128))
    zero = jnp.zeros((ATTN_BLOCK, 128), x.dtype)
    parts = []
    for p in range(Q_PER_KV // 2):
        xp = x[:, (g * 4 + p) * 128:(g * 4 + p + 1) * 128]
        parts += [jnp.where(lo, xp, zero), jnp.where(lo, zero, xp)]
    return jnp.concatenate(parts, axis=0)


def _unstack_heads(o2):
    lo = _lane_lo((ATTN_BLOCK, 128))
    return [jnp.where(lo, o2[2 * p * ATTN_BLOCK:(2 * p + 1) * ATTN_BLOCK], o2[(2 * p + 1) * ATTN_BLOCK:(2 * p + 2) * ATTN_BLOCK])
            for p in range(Q_PER_KV // 2)]


def _dup_half(prev, cur, g):
    x = jnp.concatenate([prev, cur], axis=0).astype(F32)
    lo = _lane_lo(x.shape)
    r = pltpu.roll(x, HEAD_DIM, 1)
    return (jnp.where(lo, x, r) if g == 0 else jnp.where(lo, r, x)).astype(BF)


def _fold_halves(x):
    return x + pltpu.roll(x, HEAD_DIM, 1)


def _attn_bias():
    i = lax.broadcasted_iota(jnp.int32, (Q_PER_KV * ATTN_BLOCK, 2 * ATTN_BLOCK), 0) & (ATTN_BLOCK - 1)
    j = lax.broadcasted_iota(jnp.int32, (Q_PER_KV * ATTN_BLOCK, 2 * ATTN_BLOCK), 1)
    band = (j > i) & (j <= i + ATTN_BLOCK)
    return jnp.stack([jnp.where(band & (j >= ATTN_BLOCK), 0.0, -jnp.inf), jnp.where(band, 0.0, -jnp.inf)]).astype(F32)


def _attn_probs(qs, kb, sink, bias):
    s = lax.dot_general(qs, kb, (((1,), (1,)), ((), ())), preferred_element_type=F32) * (HEAD_DIM ** -0.5) + bias
    m = jnp.maximum(jnp.max(s, axis=-1, keepdims=True), sink)
    p = jnp.exp(s - m)
    es = jnp.exp(sink - m)
    inv = 1.0 / (jnp.sum(p, axis=-1, keepdims=True) + es)
    return p, inv, es * inv


def _attn_specs(S):
    nb = S // ATTN_BLOCK
    qs = pl.BlockSpec((ATTN_BLOCK, Q_END), lambda n: (n, 0))
    cur = pl.BlockSpec((ATTN_BLOCK, 128), lambda n: (n, 0))
    prev = pl.BlockSpec((ATTN_BLOCK, 128), lambda n: (jnp.maximum(n - 1, 0), 0))
    sink = pl.BlockSpec((N_KV_HEADS, Q_PER_KV * ATTN_BLOCK, 1), lambda n: (0, 0, 0))
    bias = pl.BlockSpec((None, Q_PER_KV * ATTN_BLOCK, 2 * ATTN_BLOCK), lambda n: (jnp.minimum(n, 1), 0, 0))
    return nb, qs, cur, prev, sink, bias


def _attn_fwd(q, k, v, sink_rows, bias, *, name):
    S = q.shape[0]
    nb, qs, cur, prev, sink, bs = _attn_specs(S)

    def body(q_ref, kp_ref, kc_ref, vp_ref, vc_ref, sk_ref, b_ref, o_ref):
        for g in range(N_KV_HEADS):
            kb = _dup_half(kp_ref[...], kc_ref[...], g)
            vb = _dup_half(vp_ref[...], vc_ref[...], g)
            p, inv, _ = _attn_probs(_stack_heads(q_ref[...], g), kb, sk_ref[g], b_ref[...])
            o2 = jnp.dot(p.astype(BF), vb, preferred_element_type=F32) * inv
            for t, tile in enumerate(_unstack_heads(o2)):
                o_ref[:, (g * 4 + t) * 128:(g * 4 + t + 1) * 128] = tile.astype(BF)
    return _pcall(body, name=name, grid=(nb,), in_specs=[qs, prev, cur, prev, cur, sink, bs], out_specs=qs,
                  out_shape=jax.ShapeDtypeStruct(q.shape, BF), compiler_params=_params())(q, k, k, v, v, sink_rows, bias)


def _attn_bwd(do, q, k, v, sink_rows, bias, *, name):
    S = q.shape[0]
    nb, qs, cur, prev, sink, bs = _attn_specs(S)
    full = pl.BlockSpec((S, 128), lambda n: (0, 0))
    dsk_spec = pl.BlockSpec((N_KV_HEADS, Q_PER_KV, 128), lambda n: (0, 0, 0))

    def body(do_ref, q_ref, kp_ref, kc_ref, vp_ref, vc_ref, sk_ref, b_ref, dq_ref, dk_ref, dv_ref, dsk_ref):
        n = pl.program_id(0)

        @pl.when(n == 0)
        def _():
            dk_ref[...] = jnp.zeros_like(dk_ref)
            dv_ref[...] = jnp.zeros_like(dv_ref)
            dsk_ref[...] = jnp.zeros_like(dsk_ref)
        sub = lax.broadcasted_iota(jnp.int32, (Q_PER_KV, 128), 0)
        dkf, dvf = [], []
        for g in range(N_KV_HEADS):
            qst = _stack_heads(q_ref[...], g)
            dos = _stack_heads(do_ref[...], g)
            kb = _dup_half(kp_ref[...], kc_ref[...], g)
            vb = _dup_half(vp_ref[...], vc_ref[...], g)
            pu, inv, ps = _attn_probs(qst, kb, sk_ref[g], b_ref[...])
            p = pu * inv
            dp = lax.dot_general(dos, vb, (((1,), (1,)), ((), ())), preferred_element_type=F32)
            dd = jnp.sum(p * dp, axis=-1, keepdims=True)
            ds = (p * (dp - dd) * (HEAD_DIM ** -0.5)).astype(BF)
            dq2 = jnp.dot(ds, kb, preferred_element_type=F32)
            for t, tile in enumerate(_unstack_heads(dq2)):
                dq_ref[:, (g * 4 + t) * 128:(g * 4 + t + 1) * 128] = tile.astype(BF)
            dkf.append(_fold_halves(lax.dot_general(ds, qst, (((0,), (0,)), ((), ())), preferred_element_type=F32)))
            dvf.append(_fold_halves(lax.dot_general(p.astype(BF), dos, (((0,), (0,)), ((), ())),
                                                    preferred_element_type=F32)))
            dsr = -(ps * dd)
            upd = jnp.zeros((Q_PER_KV, 128), F32)
            for h in range(Q_PER_KV):
                upd = jnp.where(sub == h, jnp.sum(dsr[h * ATTN_BLOCK:(h + 1) * ATTN_BLOCK]), upd)
            dsk_ref[g] += upd
        lo = _lane_lo((2 * ATTN_BLOCK, 128))
        dkb = jnp.where(lo, dkf[0], dkf[1])
        dvb = jnp.where(lo, dvf[0], dvf[1])
        r0 = pl.multiple_of(n * ATTN_BLOCK, ATTN_BLOCK)
        dk_ref[pl.ds(r0, ATTN_BLOCK), :] += dkb[ATTN_BLOCK:]
        dv_ref[pl.ds(r0, ATTN_BLOCK), :] += dvb[ATTN_BLOCK:]

        @pl.when(n > 0)
        def _():
            rp = pl.multiple_of((n - 1) * ATTN_BLOCK, ATTN_BLOCK)
            dk_ref[pl.ds(rp, ATTN_BLOCK), :] += dkb[:ATTN_BLOCK]
            dv_ref[pl.ds(rp, ATTN_BLOCK), :] += dvb[:ATTN_BLOCK]
    return _pcall(body, name=name, grid=(nb,), in_specs=[qs, qs, prev, cur, prev, cur, sink, bs],
                  out_specs=[qs, full, full, dsk_spec],
                  out_shape=[jax.ShapeDtypeStruct(q.shape, BF), jax.ShapeDtypeStruct((S, 128), F32),
                             jax.ShapeDtypeStruct((S, 128), F32), jax.ShapeDtypeStruct((N_KV_HEADS, Q_PER_KV, 128), F32)],
                  compiler_params=_params())(do, q, k, k, v, v, sink_rows, bias)


def _ada_fwd(c_all, ada_w):
    ncol = ada_w.shape[2]

    def body(c_ref, w_ref, o_ref):
        cv = c_ref[...]
        ca = (cv * _sigmoid(cv)).astype(BF)
        for l in range(DEPTH):
            o_ref[:, l * ncol:(l + 1) * ncol] = jnp.dot(ca, w_ref[l].astype(BF), preferred_element_type=F32)
    return _pcall(body, name="ada_fwd", out_shape=jax.ShapeDtypeStruct((N_DEV, DEPTH * ncol), F32),
                  compiler_params=_params())(c_all, ada_w)


def _ada_bwd(c_all, dm):
    ncol = dm.shape[2]

    def body(c_ref, dm_ref, o_ref):
        cv = c_ref[...]
        ca = (cv * _sigmoid(cv)).astype(BF)
        for l in range(DEPTH):
            o_ref[l] = lax.dot_general(ca, dm_ref[l].astype(BF), (((0,), (0,)), ((), ())), preferred_element_type=F32)
    return _pcall(body, name="ada_bwd", out_shape=jax.ShapeDtypeStruct((DEPTH, D_MODEL, ncol), F32),
                  compiler_params=_params())(c_all, dm)


def _adamw(w, g, m, v, *, name):
    R, C = w.shape
    tr = R
    for t in range(8, 513, 8):
        if R % t == 0:
            tr = t
    c1 = 1.0 - ADAM_B1 ** ADAM_STEP
    c2 = 1.0 - ADAM_B2 ** ADAM_STEP

    def body(w_ref, g_ref, m_ref, v_ref, d_ref, mo_ref, vo_ref):
        gv = g_ref[...]
        mn = ADAM_B1 * m_ref[...] + (1.0 - ADAM_B1) * gv
        vn = ADAM_B2 * v_ref[...] + (1.0 - ADAM_B2) * (gv * gv)
        mo_ref[...] = mn
        vo_ref[...] = vn
        d_ref[...] = -ADAM_LR * ((mn / c1) / (jnp.sqrt(vn / c2) + ADAM_EPS) + ADAM_WD * w_ref[...])
    spec = pl.BlockSpec((tr, C), lambda i: (i, 0))
    sh = jax.ShapeDtypeStruct((R, C), F32)
    return _pcall(body, name=name, grid=(R // tr,), in_specs=[spec] * 4, out_specs=[spec] * 3, out_shape=[sh, sh, sh],
                  compiler_params=_params())(w, g, m, v)


def _sum8(parts, *, name):
    _, R, C = parts.shape
    tr = R
    for t in range(16, 257, 16):
        if R % t == 0:
            tr = t

    def body(p_ref, o_ref):
        acc = p_ref[0].astype(F32)
        for k in range(1, N_DEV):
            acc = acc + p_ref[k].astype(F32)
        o_ref[...] = acc
    return _pcall(body, name=name, grid=(R // tr,), in_specs=[pl.BlockSpec((N_DEV, tr, C), lambda i: (0, i, 0))],
                  out_specs=pl.BlockSpec((tr, C), lambda i: (i, 0)), out_shape=jax.ShapeDtypeStruct((R, C), F32),
                  compiler_params=_params())(parts)


MESH_ID = pl.DeviceIdType.MESH
ANY = pl.BlockSpec(memory_space=pl.ANY)


def _all_gather(x, *, name, after=None):
    R, C = x.shape
    extra = [] if after is None else [after]

    def body(x_ref, *rest):
        out_ref, send_sems, recv_sems, local_sem = rest[-4:]
        mx, my, mc = lax.axis_index("x"), lax.axis_index("y"), lax.axis_index("c")
        me, sibling = (mx, my, mc), (mx, my, 1 - mc)
        chips = [(1 - mx, my), (mx, 1 - my), (1 - mx, 1 - my)]

        def blk(px, py, pc):
            return out_ref.at[4 * px + 2 * py + pc]

        def copy(k, block, to, src=None):
            return pltpu.make_async_remote_copy(
                src_ref=blk(*block) if src is None else src, dst_ref=blk(*block),
                send_sem=send_sems.at[k], recv_sem=recv_sems.at[k], device_id=to, device_id_type=MESH_ID)

        mine = pltpu.make_async_copy(x_ref, blk(*me), local_sem)
        mine.start()
        first = [copy(0, me, sibling, src=x_ref)]
        first += [copy(1 + j, me, (*chip, mc), src=x_ref) for j, chip in enumerate(chips)]
        for cp in first:
            cp.start()
        passed = [copy(4 + j, (*chip, mc), sibling) for j, chip in enumerate(chips)]
        for j, chip in enumerate(chips):
            copy(1 + j, (*chip, mc), me).wait_recv()
            passed[j].start()
        copy(0, sibling, me).wait_recv()
        for j, chip in enumerate(chips):
            copy(4 + j, (*chip, 1 - mc), me).wait_recv()
        for cp in first + passed:
            cp.wait_send()
        mine.wait()
    return _pcall(body, name=name, in_specs=[ANY] * (1 + len(extra)), out_specs=ANY,
                  out_shape=jax.ShapeDtypeStruct((N_DEV, R, C), x.dtype),
                  scratch_shapes=[pltpu.SemaphoreType.DMA((7,)), pltpu.SemaphoreType.DMA((7,)), pltpu.SemaphoreType.DMA],
                  compiler_params=pltpu.CompilerParams(has_side_effects=True))(x, *extra)


HBM_SPEC = pl.BlockSpec(memory_space=pltpu.HBM)
SEM_SPEC = pl.BlockSpec(memory_space=pltpu.SEMAPHORE)
DATAFLOW = pltpu.SideEffectType.DATAFLOW_SIDE_EFFECTING


def _coords():
    return lax.axis_index("x"), lax.axis_index("y"), lax.axis_index("c")


def _other_chips(mx, my):
    return [(1 - mx, my), (mx, 1 - my), (1 - mx, 1 - my)]


def _plan_gather_ici(refs, send, recv):
    src, land = refs
    mx, my, mc = _coords()
    return [pltpu.make_async_remote_copy(src_ref=src, dst_ref=land.at[mc, 2 * mx + my], send_sem=send[j], recv_sem=recv[j],
                                         device_id=(px, py, mc), device_id_type=MESH_ID)
            for j, (px, py) in enumerate(_other_chips(mx, my))]


def _plan_gather_d2d(refs, send, recv):
    (land,) = refs
    mx, my, mc = _coords()
    return [pltpu.make_async_remote_copy(src_ref=land.at[mc], dst_ref=land.at[mc], send_sem=send[0], recv_sem=recv[0],
                                         device_id=(mx, my, 1 - mc), device_id_type=MESH_ID)]


def _plan_reduce_d2d(refs, send, recv):
    g, land = refs
    mx, my, mc = _coords()
    return [pltpu.make_async_remote_copy(src_ref=g.at[1 - mc], dst_ref=land, send_sem=send[0], recv_sem=recv[0],
                                         device_id=(mx, my, 1 - mc), device_id_type=MESH_ID)]


def _plan_reduce_ici(refs, send, recv):
    h, land = refs
    mx, my, mc = _coords()
    return [pltpu.make_async_remote_copy(src_ref=h.at[2 * px + py], dst_ref=land.at[j], send_sem=send[j], recv_sem=recv[j],
                                         device_id=(px, py, mc), device_id_type=MESH_ID)
            for j, (px, py) in enumerate(_other_chips(mx, my))]


def _rdma_start(bufs, n, plan, *, name, after=None):
    nb = len(bufs)
    extra = [] if after is None else [after]
    ne = len(extra)

    def body(*refs):
        ins, send, recv = refs[:nb], refs[nb + ne:nb + ne + n], refs[nb + ne + n:nb + ne + 2 * n]
        token = refs[-1]
        for cp in plan(ins, send, recv):
            cp.start()
        token[...] = jnp.zeros_like(token)
    out = _pcall(body, name=name,
                 out_shape=tuple([pltpu.SemaphoreType.DMA(())] * (2 * n) + [pltpu.HBM(b.shape, b.dtype) for b in bufs]
                                 + [jax.ShapeDtypeStruct((8, 128), F32)]),
                 in_specs=tuple([HBM_SPEC] * nb + [ANY] * ne),
                 out_specs=tuple([SEM_SPEC] * (2 * n) + [HBM_SPEC] * nb + [pl.BlockSpec(memory_space=pltpu.VMEM)]),
                 input_output_aliases={i: 2 * n + i for i in range(nb)},
                 compiler_params=pltpu.CompilerParams(has_side_effects=DATAFLOW))(
                     *[pltpu.with_memory_space_constraint(b, pltpu.HBM) for b in bufs], *extra)
    return list(out[:2 * n]), list(out[2 * n:2 * n + nb]), out[-1]


def _rdma_wait(sems, bufs, n, plan, after, *, name):
    nb = len(bufs)

    def body(*refs):
        ins, send, recv = refs[:nb], refs[nb:nb + n], refs[nb + n:nb + 2 * n]
        for cp in plan(ins, send, recv):
            cp.wait_send()
            cp.wait_recv()
    out = _pcall(body, name=name, out_shape=tuple(pltpu.HBM(b.shape, b.dtype) for b in bufs),
                 in_specs=tuple([HBM_SPEC] * nb + [SEM_SPEC] * (2 * n) + [ANY]), out_specs=tuple([HBM_SPEC] * nb),
                 input_output_aliases={i: i for i in range(nb)},
                 compiler_params=pltpu.CompilerParams(has_side_effects=DATAFLOW))(*bufs, *sems, after)
    return list(out)


def _sum_pair(g, land, cidx, *, name):
    _, nchip, R, C = g.shape
    tr = _tile(R, 512, 16)

    def body(c_ref, g_ref, l_ref, o_ref):
        o_ref[...] = (g_ref[...].astype(F32) + l_ref[...].astype(F32)).astype(BF)
    grid_spec = pltpu.PrefetchScalarGridSpec(
        num_scalar_prefetch=1, grid=(nchip, R // tr),
        in_specs=[pl.BlockSpec((None, None, tr, C), lambda p, i, c_ref: (c_ref[0], p, i, 0)),
                  pl.BlockSpec((None, tr, C), lambda p, i, c_ref: (p, i, 0))],
        out_specs=pl.BlockSpec((None, tr, C), lambda p, i, c_ref: (p, i, 0)))
    return _pcall(body, name=name, grid_spec=grid_spec, out_shape=jax.ShapeDtypeStruct((nchip, R, C), BF),
                  compiler_params=_params())(cidx, g, land)


def _sum_chips(h, land, chipidx, *, name):
    _, R, C = h.shape
    tr = _tile(R, 512, 16)

    def body(c_ref, h_ref, l_ref, o_ref):
        acc = h_ref[...].astype(F32)
        for j in range(3):
            acc = acc + l_ref[j].astype(F32)
        o_ref[...] = acc
    grid_spec = pltpu.PrefetchScalarGridSpec(
        num_scalar_prefetch=1, grid=(R // tr,),
        in_specs=[pl.BlockSpec((None, tr, C), lambda i, c_ref: (c_ref[0], i, 0)),
                  pl.BlockSpec((3, tr, C), lambda i, c_ref: (0, i, 0))],
        out_specs=pl.BlockSpec((tr, C), lambda i, c_ref: (i, 0)))
    return _pcall(body, name=name, grid_spec=grid_spec, out_shape=jax.ShapeDtypeStruct((R, C), F32),
                  compiler_params=_params())(chipidx, h, land)


PART_IN = ("w_in",)
PART_MIX = ("proj_a", "proj_b", "w_out")
PART_FFN = ("ffn_w_gate", "ffn_w_up", "ffn_w_down")


def _part_rows(names):
    return sum(BIG_ROWS[n] for n in names)


def _part_offsets(names):
    off, r = {}, 0
    for n in names:
        off[n] = r
        r += BIG_ROWS[n]
    return off


def _pack_shards(shards, l, names):
    return jnp.concatenate([(shards[n][l].T if n in COL_SHARDED else shards[n][l]).astype(BF) for n in names], axis=0)


def _unpack_weights(full8, names):
    off = _part_offsets(names)

    def whole(n):
        return full8[:, off[n]:off[n] + BIG_ROWS[n], :].reshape(N_DEV * BIG_ROWS[n], 1024)
    out = {}
    if "w_in" in names:
        wt_in = whole("w_in")
        out["wt_in"] = jnp.concatenate([wt_in[V_END:], wt_in[:V_END]], axis=0)
    for n in ("proj_a", "proj_b", "w_out"):
        if n in names:
            out[n] = whole(n)
    if "ffn_w_gate" in names:
        out["wt_gu"] = jnp.concatenate([whole("ffn_w_gate"), whole("ffn_w_up")], axis=0)
        out["w_down"] = whole("ffn_w_down")
    return out


def _from_land(land):
    return land.transpose(1, 0, 2, 3).reshape(N_DEV, land.shape[2], 1024)


def _pack_grads(wg, names):
    full = {"proj_a": wg.get("proj_a"), "proj_b": wg.get("proj_b"), "w_out": wg.get("w_out"), "ffn_w_down": wg.get("w_down")}
    if "w_in" in names:
        full["w_in"] = jnp.concatenate([wg["wt_in"][P_Q:], wg["wt_in"][:P_Q]], axis=0)
    if "ffn_w_gate" in names:
        full["ffn_w_gate"], full["ffn_w_up"] = wg["wt_gu"][:FFN_DIM], wg["wt_gu"][FFN_DIM:]
    blocks = jnp.concatenate([full[n].reshape(N_DEV, BIG_ROWS[n], 1024) for n in names], axis=1)
    return blocks.reshape(4, 2, _part_rows(names), 1024).transpose(1, 0, 2, 3)


def _unpack_shard_grads(gs, names):
    off = _part_offsets(names)
    out = {}
    for n in names:
        blk = gs[off[n]:off[n] + BIG_ROWS[n]]
        out[n] = blk.T if n in COL_SHARDED else blk
    return out


def _rope_setup(positions):
    S = positions.shape[0]
    inv = ROPE_THETA ** (-jnp.arange(0, ROT_DIM, 2, dtype=F32) / ROT_DIM)
    lane = np.arange(128) % HEAD_DIM
    half = ROT_DIM // 2
    inv_row = jnp.where(lane < ROT_DIM, jnp.tile(inv, 128 // half), 0.0)[None, :].astype(F32)
    m1_row = jnp.asarray((lane < half).astype(np.float32))[None, :]
    m2_row = jnp.asarray(((lane >= half) & (lane < ROT_DIM)).astype(np.float32))[None, :]
    return (*_rope_tables(positions.astype(F32).reshape(S, 1), inv_row, m1_row, m2_row), _attn_bias())


def _hook(hooks, point, after):
    f = None if hooks is None else hooks.get(point)
    return None if f is None else f(after)


def _layer_fwd(l, x, mod_l, W, small, rope, hooks=None):
    rc, rs1, rs2, bias = rope
    sh1, sc1, g1, sh2, sc2, g2 = [mod_l[i * D_MODEL:(i + 1) * D_MODEL][None, :] for i in range(6)]
    nw1, nw2 = small["norm1_w"][l][None, :], small["norm2_w"][l][None, :]
    h = _normmod_fwd(x, nw1, sc1, sh1, name=f"normmod1_fwd{l}")
    tok = _hook(hooks, "mm_in", h)
    proj = _mm(h, W["wt_in"], nt=True, out_dtype=F32, name=f"mm_in{l}", after=tok, tn_cap=768)
    q_r, k_r, v_b = _rope_fwd(proj, rc, rs1, rs2, name=f"rope_fwd{l}")
    sink_rows = jnp.repeat(small["attn_sinks"][l].reshape(N_KV_HEADS, Q_PER_KV), ATTN_BLOCK, axis=1)[..., None]
    y_attn = _attn_fwd(q_r, k_r, v_b, sink_rows, bias, name=f"attn_fwd{l}")
    lnw, lnb = small["sgu_ln_w"][l][None, :], small["sgu_ln_b"][l][None, :]
    sgu_bt = small["sgu_b"][l].T
    y_sgu = _sgu_fwd(proj, lnw, lnb, small["sgu_w"][l], sgu_bt, name=f"sgu_fwd{l}")
    tok = _hook(hooks, "mm_pa", y_sgu)
    a_br = _mm(y_sgu, W["proj_a"], nt=False, out_dtype=BF, name=f"mm_pa{l}", after=tok)
    b_br = _mm(y_attn, W["proj_b"], nt=False, out_dtype=BF, name=f"mm_pb{l}")
    merged = _merge_fwd(a_br, b_br, proj, name=f"merge_fwd{l}")
    x1, o1 = _mm(merged, W["w_out"], nt=False, out_dtype=F32, name=f"mm_out{l}", res=x, gvec=g1)
    h2 = _normmod_fwd(x1, nw2, sc2, sh2, name=f"normmod2_fwd{l}")
    au = _mm(h2, W["wt_gu"], nt=True, out_dtype=F32, name=f"mm_gu{l}", tn_cap=1408)
    cw, cb = small["ffn_conv_w"][l], small["ffn_conv_b"][l][None, :]
    hf = _ffn_act_fwd(au, cw, cb, name=f"ffn_act_fwd{l}")
    x2, o2 = _mm(hf, W["w_down"], nt=False, out_dtype=F32, name=f"mm_down{l}", res=x1, gvec=g2)
    saved = dict(x=x, h=h, proj=proj, q_r=q_r, k_r=k_r, v_b=v_b, sink_rows=sink_rows, y_attn=y_attn, y_sgu=y_sgu,
                 a_br=a_br, b_br=b_br, merged=merged, x1=x1, o1=o1, h2=h2, au=au, hf=hf, o2=o2)
    return x2, saved


def _layer_bwd(l, dx, mod_l, W, small, rope, sv, hooks=None, wg=None):
    rc, rs1, rs2, bias = rope
    sh1, sc1, g1, sh2, sc2, g2 = [mod_l[i * D_MODEL:(i + 1) * D_MODEL][None, :] for i in range(6)]
    nw1, nw2 = small["norm1_w"][l][None, :], small["norm2_w"][l][None, :]
    cw, cb = small["ffn_conv_w"][l], small["ffn_conv_b"][l][None, :]
    lnw, lnb = small["sgu_ln_w"][l][None, :], small["sgu_ln_b"][l][None, :]
    sgu_bt = small["sgu_b"][l].T
    wg = {} if wg is None else wg
    do2, dg2 = _scale_reduce(dx, sv["o2"], g2, name=f"scale2_{l}", after=_hook(hooks, "scale2", dx))
    dhf = _mm(do2, W["w_down"], nt=True, out_dtype=F32, name=f"mm_down_dx{l}", tn_cap=1408)
    wg["w_down"] = _mm_tn(sv["hf"], do2, name=f"mm_down_dw{l}")
    dac, dup, dcw, dcb = _ffn_act_bwd_a(dhf, sv["au"], cw, cb, name=f"ffn_act_bwd_a{l}")
    da = _ffn_act_bwd_b(dac, cw, name=f"ffn_act_bwd_b{l}")
    dau = jnp.concatenate([da, dup], axis=1)
    dh2 = _mm(dau, W["wt_gu"], nt=False, out_dtype=F32, name=f"mm_gu_dx{l}", after=_hook(hooks, "mm_gu_dx", dau))
    wg["wt_gu"] = _mm_tn(dau, sv["h2"], name=f"mm_gu_dw{l}")
    dx1, dnw2, dsc2, dsh2 = _normmod_bwd(dh2, sv["x1"], nw2, sc2, sh2, dx, name=f"normmod2_bwd{l}")
    do1, dg1 = _scale_reduce(dx1, sv["o1"], g1, name=f"scale1_{l}", after=_hook(hooks, "scale1", dx1))
    dmerged = _mm(do1, W["w_out"], nt=True, out_dtype=F32, name=f"mm_out_dx{l}")
    wg["w_out"] = _mm_tn(sv["merged"], do1, name=f"mm_out_dw{l}")
    d_a, d_b, dgates = _merge_bwd(dmerged, sv["a_br"], sv["b_br"], sv["proj"], name=f"merge_bwd{l}")
    dysgu = _mm(d_a, W["proj_a"], nt=True, out_dtype=F32, name=f"mm_pa_dx{l}", after=_hook(hooks, "mm_pa_dx", d_a))
    dyattn = _mm(d_b, W["proj_b"], nt=True, out_dtype=BF, name=f"mm_pb_dx{l}")
    wg["proj_a"] = _mm_tn(sv["y_sgu"], d_a, name=f"mm_pa_dw{l}")
    wg["proj_b"] = _mm_tn(sv["y_attn"], d_b, name=f"mm_pb_dw{l}")
    dz, dlnw, dlnb, dsguw, dsgubt = _sgu_bwd(dysgu, sv["proj"], lnw, lnb, small["sgu_w"][l], sgu_bt, name=f"sgu_bwd{l}")
    dq_r, dk_r, dv_b, dsk = _attn_bwd(dyattn, sv["q_r"], sv["k_r"], sv["v_b"], sv["sink_rows"], bias, name=f"attn_bwd{l}")
    dqkv = _rope_bwd(dq_r, dk_r, dv_b, rc, rs1, rs2, name=f"rope_bwd{l}")
    dproj = jnp.concatenate([dz, dgates, dqkv], axis=1)
    dh = _mm(dproj, W["wt_in"], nt=False, out_dtype=F32, name=f"mm_in_dx{l}")
    wg["wt_in"] = _mm_tn(dproj, sv["h"], name=f"mm_in_dw{l}")
    dx0, dnw1, dsc1, dsh1 = _normmod_bwd(dh, sv["x"], nw1, sc1, sh1, dx1, name=f"normmod1_bwd{l}")
    dmod = jnp.concatenate([dsh1, dsc1, dg1, dsh2, dsc2, dg2], axis=1)[0]
    sg = {"norm1_w": dnw1[0], "norm2_w": dnw2[0], "attn_sinks": dsk[:, :, 0].reshape(N_Q_HEADS),
          "sgu_ln_w": dlnw[0], "sgu_ln_b": dlnb[0], "sgu_w": dsguw, "sgu_b": dsgubt.T,
          "ffn_conv_w": dcw, "ffn_conv_b": dcb[0]}
    return dx0, wg, sg, dmod


SMALL = ("ada_b", "norm1_w", "attn_sinks", "sgu_ln_w", "sgu_ln_b", "sgu_w", "sgu_b", "norm2_w", "ffn_conv_b", "final_norm_w")
WEIGHT_ORDER = ("ada_w", "ada_b", "norm1_w", "w_in", "attn_sinks", "sgu_ln_w", "sgu_ln_b", "sgu_w", "sgu_b", "proj_a", "proj_b",
                "w_out", "norm2_w", "ffn_w_gate", "ffn_w_up", "ffn_conv_w", "ffn_conv_b", "ffn_w_down", "final_norm_w")


def _flat_pack(arrs, rows):
    flat = jnp.concatenate([a.reshape(-1) for a in arrs])
    return jnp.pad(flat, (0, rows * 1024 - flat.shape[0])).reshape(rows, 1024)


def _flat_unpack(buf, shapes):
    flat = buf.reshape(-1)
    out, o = [], 0
    for s in shapes:
        n = int(np.prod(s))
        out.append(flat[o:o + n].reshape(s))
        o += n
    return out


def _adam2d(w, g, m, v, *, name):
    shp = w.shape
    r2 = (int(np.prod(shp[:-1])), shp[-1]) if len(shp) > 1 else (1, shp[0])
    d, mn, vn = _adamw(w.reshape(r2), g.reshape(r2), m.reshape(r2), v.reshape(r2), name=name)
    return d.reshape(shp), mn.reshape(shp), vn.reshape(shp)


def kernel(x, c, positions, ada_w, ada_b, norm1_w, w_in, attn_sinks, sgu_ln_w, sgu_ln_b, sgu_w, sgu_b, proj_a, proj_b, w_out, norm2_w, ffn_w_gate, ffn_w_up, ffn_conv_w, ffn_conv_b, ffn_w_down, final_norm_w, loss_target, m_ada_w, m_ada_b, m_norm1_w, m_w_in, m_attn_sinks, m_sgu_ln_w, m_sgu_ln_b, m_sgu_w, m_sgu_b, m_proj_a, m_proj_b, m_w_out, m_norm2_w, m_ffn_w_gate, m_ffn_w_up, m_ffn_conv_w, m_ffn_conv_b, m_ffn_w_down, m_final_norm_w, v_ada_w, v_ada_b, v_norm1_w, v_w_in, v_attn_sinks, v_sgu_ln_w, v_sgu_ln_b, v_sgu_w, v_sgu_b, v_proj_a, v_proj_b, v_w_out, v_norm2_w, v_ffn_w_gate, v_ffn_w_up, v_ffn_conv_w, v_ffn_conv_b, v_ffn_w_down, v_final_norm_w):
    wts = dict(ada_w=ada_w, ada_b=ada_b, norm1_w=norm1_w, w_in=w_in, attn_sinks=attn_sinks, sgu_ln_w=sgu_ln_w,
               sgu_ln_b=sgu_ln_b, sgu_w=sgu_w, sgu_b=sgu_b, proj_a=proj_a, proj_b=proj_b, w_out=w_out, norm2_w=norm2_w,
               ffn_w_gate=ffn_w_gate, ffn_w_up=ffn_w_up, ffn_conv_w=ffn_conv_w, ffn_conv_b=ffn_conv_b,
               ffn_w_down=ffn_w_down, final_norm_w=final_norm_w)
    mom = dict(ada_w=m_ada_w, ada_b=m_ada_b, norm1_w=m_norm1_w, w_in=m_w_in, attn_sinks=m_attn_sinks, sgu_ln_w=m_sgu_ln_w,
               sgu_ln_b=m_sgu_ln_b, sgu_w=m_sgu_w, sgu_b=m_sgu_b, proj_a=m_proj_a, proj_b=m_proj_b, w_out=m_w_out,
               norm2_w=m_norm2_w, ffn_w_gate=m_ffn_w_gate, ffn_w_up=m_ffn_w_up, ffn_conv_w=m_ffn_conv_w,
               ffn_conv_b=m_ffn_conv_b, ffn_w_down=m_ffn_w_down, final_norm_w=m_final_norm_w)
    var = dict(ada_w=v_ada_w, ada_b=v_ada_b, norm1_w=v_norm1_w, w_in=v_w_in, attn_sinks=v_attn_sinks, sgu_ln_w=v_sgu_ln_w,
               sgu_ln_b=v_sgu_ln_b, sgu_w=v_sgu_w, sgu_b=v_sgu_b, proj_a=v_proj_a, proj_b=v_proj_b, w_out=v_w_out,
               norm2_w=v_norm2_w, ffn_w_gate=v_ffn_w_gate, ffn_w_up=v_ffn_w_up, ffn_conv_w=v_ffn_conv_w,
               ffn_conv_b=v_ffn_conv_b, ffn_w_down=v_ffn_w_down, final_norm_w=v_final_norm_w)
    me = 4 * lax.axis_index("x") + 2 * lax.axis_index("y") + lax.axis_index("c")
    ada_cols = ada_w.shape[2]

    c_all = _all_gather(jnp.broadcast_to(c, (8, D_MODEL)), name="ag_c")[:, 0, :]
    prod = _ada_fwd(c_all, ada_w)
    prod_all = _all_gather(prod, name="ag_mod")
    mine = lax.dynamic_index_in_dim(prod_all, me, axis=1, keepdims=False)
    mod = jnp.stack([mine[:, l * ada_cols:(l + 1) * ada_cols].reshape(-1) for l in range(DEPTH)]) + ada_b

    conv_cols = ffn_conv_w.shape[2]
    conv_all = _all_gather(_flat_pack([ffn_conv_w], 8), name="ag_conv", after=mod)
    conv_full = jnp.stack([a.reshape(DEPTH, 3, conv_cols) for a in
                           [conv_all[j].reshape(-1)[:DEPTH * 3 * conv_cols] for j in range(N_DEV)]], axis=2)
    conv_full = conv_full.reshape(DEPTH, 3, FFN_DIM)
    small = {n: wts[n] for n in SMALL}
    small["ffn_conv_w"] = conv_full

    mx, my, mc = _coords()
    cidx = jnp.reshape(mc, (1,)).astype(jnp.int32)
    chipidx = jnp.reshape(2 * mx + my, (1,)).astype(jnp.int32)
    rope = _rope_setup(positions[0])

    class Gather:
        def __init__(self, names, l, tag):
            self.names, self.tag = names, tag
            self.src = _pack_shards(wts, l, names)
            self.land = lax.dynamic_update_slice(jnp.zeros((2, 4, _part_rows(names), 1024), BF), self.src[None, None],
                                                 (mc, 2 * mx + my, 0, 0))

        def ici_start(self, after):
            self.sems, (self.src, self.land), tok = _rdma_start([self.src, self.land], 3, _plan_gather_ici,
                                                                name=f"ag_{self.tag}_ici_start", after=after)
            return tok

        def ici_wait_d2d_start(self, after):
            _, land = _rdma_wait(self.sems, [self.src, self.land], 3, _plan_gather_ici, after, name=f"ag_{self.tag}_ici_wait")
            self.sems, (self.land,), tok = _rdma_start([land], 1, _plan_gather_d2d, name=f"ag_{self.tag}_d2d_start")
            return tok

        def d2d_wait(self, after):
            (land,) = _rdma_wait(self.sems, [self.land], 1, _plan_gather_d2d, after, name=f"ag_{self.tag}_d2d_wait")
            return _unpack_weights(_from_land(land), self.names)

    rest = PART_MIX + PART_FFN
    W0 = _unpack_weights(_all_gather(_pack_shards(wts, 0, PART_IN), name="ag_w0_in", after=conv_all), PART_IN)
    W1 = {}
    g_rest0 = Gather(rest, 0, "w0_rest")
    g_all1 = Gather(BIG, 1, "w1")

    def rest0_then_layer1(after):
        W0.update(g_rest0.d2d_wait(g_rest0.ici_wait_d2d_start(after)))
        return g_all1.ici_start(W0["proj_a"])

    x1, sv0 = _layer_fwd(0, x[0], mod[0], W0, small, rope,
                         {"mm_in": lambda after: g_rest0.ici_start(W0["wt_in"]), "mm_pa": rest0_then_layer1})
    g_all1.ici_wait_d2d_start(x1)
    x2, sv1 = _layer_fwd(1, x1, mod[1], W1, small, rope, {"mm_in": lambda after: W1.update(g_all1.d2d_wait(after))})
    dx2, dfw, loss_tile = _head(x2, final_norm_w[None, :], loss_target[0])
    loss = lax.psum(loss_tile[0, 0], ("x", "y", "c"))

    class Reduce:
        def __init__(self, names, tag):
            self.names, self.tag, self.rows = names, tag, _part_rows(names)

        def d2d_start(self, wg, after=None):
            self.sems, self.bufs, tok = _rdma_start([_pack_grads(wg, self.names), jnp.zeros((4, self.rows, 1024), BF)], 1,
                                                    _plan_reduce_d2d, name=f"rs_{self.tag}_d2d_start", after=after)
            return tok

        def d2d_wait_ici_start(self, after):
            g_t, land_a = _rdma_wait(self.sems, self.bufs, 1, _plan_reduce_d2d, after, name=f"rs_{self.tag}_d2d_wait")
            h = _sum_pair(g_t, land_a, cidx, name=f"rs_{self.tag}_sum_pair")
            self.sems, self.bufs, tok = _rdma_start([h, jnp.zeros((3, self.rows, 1024), BF)], 3, _plan_reduce_ici,
                                                    name=f"rs_{self.tag}_ici_start")
            return tok

        def ici_wait(self, after):
            h_t, land_b = _rdma_wait(self.sems, self.bufs, 3, _plan_reduce_ici, after, name=f"rs_{self.tag}_ici_wait")
            return _unpack_shard_grads(_sum_chips(h_t, land_b, chipidx, name=f"rs_{self.tag}_sum_chips"), self.names)

    dx1, wg1, sg1, dmod1 = _layer_bwd(1, dx2, mod[1], W1, small, rope, sv1)
    r_all1, r_ffn0, r_mix0 = Reduce(BIG, "g1"), Reduce(PART_FFN, "g0_ffn"), Reduce(PART_IN + PART_MIX, "g0_mix")
    tok1 = r_all1.d2d_start(wg1)
    wg0, shard1 = {}, {}

    def layer1_done_then_ffn0(after):
        shard1.update(r_all1.ici_wait(after))
        return r_ffn0.d2d_wait_ici_start(shard1["w_in"])

    grad_x, _, sg0, dmod0 = _layer_bwd(
        0, dx1, mod[0], W0, small, rope, sv0, wg=wg0,
        hooks={"scale2": lambda after: tok1, "mm_gu_dx": r_all1.d2d_wait_ici_start,
               "scale1": lambda after: r_ffn0.d2d_start(wg0, after), "mm_pa_dx": layer1_done_then_ffn0})
    tok = r_mix0.d2d_start(wg0, grad_x)
    shard0 = r_ffn0.ici_wait(tok)
    shard0.update(r_mix0.ici_wait(r_mix0.d2d_wait_ici_start(shard0["ffn_w_down"])))
    grads = {n: jnp.stack([shard0[n], shard1[n]]) for n in BIG}
    sg = {n: jnp.stack([sg0[n], sg1[n]]) for n in sg0}
    sg["final_norm_w"] = dfw[0]
    dmod = jnp.stack([dmod0, dmod1])

    small_names = [n for n in SMALL if n != "ada_b"] + ["ffn_conv_w"]
    small_shapes = [(DEPTH, 6 * D_MODEL)] + [sg[n].shape for n in small_names]
    n_small = sum(int(np.prod(s)) for s in small_shapes)
    rows = -(-n_small // 1024 // 8) * 8
    sm_all = _all_gather(_flat_pack([dmod] + [sg[n] for n in small_names], rows), name="ag_small", after=shard0["w_in"])
    sm_sum = _flat_unpack(_sum8(sm_all, name="sum_small"), small_shapes)
    grads["ada_b"] = sm_sum[0]
    for n, gsum in zip(small_names, sm_sum[1:]):
        grads[n] = gsum
    grads["ffn_conv_w"] = lax.dynamic_slice_in_dim(grads["ffn_conv_w"], me * conv_cols, conv_cols, axis=2)
    dmod_all = sm_all[:, :DEPTH * 6, :].reshape(N_DEV, DEPTH, 6 * D_MODEL)
    dm_mine = lax.dynamic_slice_in_dim(dmod_all, me * ada_cols, ada_cols, axis=2).transpose(1, 0, 2)
    dm_mine = jnp.pad(dm_mine, ((0, 0), (0, 8), (0, 0)))
    grads["ada_w"] = _ada_bwd(jnp.pad(c_all, ((0, 8), (0, 0))), dm_mine)

    packed_small = [n for n in SMALL]
    pshapes = [wts[n].shape for n in packed_small]
    prow = -(-sum(int(np.prod(s)) for s in pshapes) // 1024 // 8) * 8
    pk = lambda d: _flat_pack([d[n] for n in packed_small], prow)
    d_s, m_s, v_s = _adamw(pk(wts), pk(grads), pk(mom), pk(var), name="adamw_small")
    delta, new_m, new_v = {}, {}, {}
    for n, dd, mm, vv in zip(packed_small, _flat_unpack(d_s, pshapes), _flat_unpack(m_s, pshapes), _flat_unpack(v_s, pshapes)):
        delta[n], new_m[n], new_v[n] = dd, mm, vv
    for n in WEIGHT_ORDER:
        if n not in delta:
            delta[n], new_m[n], new_v[n] = _adam2d(wts[n], grads[n], mom[n], var[n], name=f"adamw_{n}")
    return (loss, grad_x[None], *[grads[n] for n in WEIGHT_ORDER], *[delta[n] for n in WEIGHT_ORDER],
            *[new_m[n] for n in WEIGHT_ORDER], *[new_v[n] for n in WEIGHT_ORDER])
```

```python
import functools

import jax
import jax.numpy as jnp
import numpy as np
from jax import lax
from jax.experimental import pallas as pl
from jax.experimental.pallas import tpu as pltpu

F32 = jnp.float32
BF = jnp.bfloat16

N_DEV = 8
D_MODEL = 1024
DEPTH = 2
N_Q_HEADS = 16
N_KV_HEADS = 2
HEAD_DIM = 64
Q_PER_KV = N_Q_HEADS // N_KV_HEADS
ATTN_BLOCK = 128
ROPE_THETA = 500000.0
ROT_DIM = HEAD_DIM // 4
SGU_WIDTH = 1024
SGU_GROUPS = 8
SGU_CHUNK = 128
FFN_DIM = 2816
NORM_EPS = 1e-6
Q_END = N_Q_HEADS * HEAD_DIM
K_END = Q_END + N_KV_HEADS * HEAD_DIM
V_END = K_END + N_KV_HEADS * HEAD_DIM
Z_END = V_END + 2 * SGU_WIDTH
IN_COLS = Z_END + 2 * D_MODEL
P_Z, P_G, P_Q, P_K, P_V = 0, 2048, 4096, 5120, 5248

ADAM_LR = 0.001
ADAM_B1 = 0.9
ADAM_B2 = 0.999
ADAM_EPS = 1e-08
ADAM_WD = 0.01
ADAM_STEP = 10

VMEM_LIMIT_BYTES = 56 * 1024 * 1024

BIG = ("w_in", "proj_a", "proj_b", "w_out", "ffn_w_gate", "ffn_w_up", "ffn_w_down")
COL_SHARDED = ("w_in", "ffn_w_gate", "ffn_w_up")
BIG_SHAPE = {"w_in": (D_MODEL, IN_COLS), "proj_a": (SGU_WIDTH, D_MODEL), "proj_b": (Q_END, D_MODEL),
             "w_out": (D_MODEL, D_MODEL), "ffn_w_gate": (D_MODEL, FFN_DIM), "ffn_w_up": (D_MODEL, FFN_DIM),
             "ffn_w_down": (FFN_DIM, D_MODEL)}
BIG_ROWS = {n: BIG_SHAPE[n][0] * BIG_SHAPE[n][1] // N_DEV // 1024 for n in BIG}
LAYER_ROWS = sum(BIG_ROWS.values())


def _pcall(body, **kw):
    return pl.pallas_call(body, **kw)


def _params(**kw):
    return pltpu.CompilerParams(vmem_limit_bytes=VMEM_LIMIT_BYTES, **kw)


def _tile(n, cap, unit=128):
    if n <= cap:
        return n
    best = 0
    t = unit
    while t <= cap:
        if n % t == 0:
            best = t
        t += unit
    assert best, (n, cap, unit)
    return best


def _mm(a, b, *, nt, out_dtype, name, res=None, gvec=None, after=None, tm=None, tn_cap=1024):
    a_list = list(a) if isinstance(a, (list, tuple)) else [a]
    b_list = list(b) if isinstance(b, (list, tuple)) else [b]
    a, b = a_list[0], b_list[0]
    M, K = a.shape
    N = b.shape[0] if nt else b.shape[1]
    k_total = sum(x.shape[1] for x in a_list)
    tm = _tile(M, tm or (1024 if k_total <= 1024 else 512), 8)
    tn = _tile(N, tn_cap)
    dn = (((1,), (1,)), ((), ())) if nt else (((1,), (0,)), ((), ()))

    def b_spec_of(x):
        k = x.shape[1] if nt else x.shape[0]
        return pl.BlockSpec((tn, k), lambda i, j: (j, 0)) if nt else pl.BlockSpec((k, tn), lambda i, j: (0, j))
    b_spec = b_spec_of(b)
    o_spec = pl.BlockSpec((tm, tn), lambda i, j: (i, j))
    if res is None:
        extra = [] if after is None else [after]
        n = len(a_list)

        def body(*refs):
            o_ref = refs[-1]
            acc = None
            for a_ref, b_ref in zip(refs[:n], refs[n:2 * n]):
                d = lax.dot_general(a_ref[...].astype(BF), b_ref[...].astype(BF), dn, preferred_element_type=F32)
                acc = d if acc is None else acc + d
            o_ref[...] = acc.astype(out_dtype)
        return _pcall(body, name=name, grid=(M // tm, N // tn),
                      in_specs=[pl.BlockSpec((tm, x.shape[1]), lambda i, j: (i, 0)) for x in a_list]
                      + [b_spec_of(x) for x in b_list] + [ANY] * len(extra), out_specs=o_spec,
                      out_shape=jax.ShapeDtypeStruct((M, N), out_dtype), compiler_params=_params())(
                          *a_list, *b_list, *extra)

    def body_res(a_ref, b_ref, r_ref, g_ref, o_ref, acc_ref):
        acc = lax.dot_general(a_ref[...].astype(BF), b_ref[...].astype(BF), dn, preferred_element_type=F32)
        acc_ref[...] = acc
        o_ref[...] = r_ref[...] + g_ref[...] * acc
    return _pcall(body_res, name=name, grid=(M // tm, N // tn),
                  in_specs=[pl.BlockSpec((tm, K), lambda i, j: (i, 0)), b_spec, o_spec,
                            pl.BlockSpec((1, tn), lambda i, j: (0, j))],
                  out_specs=[o_spec, o_spec],
                  out_shape=[jax.ShapeDtypeStruct((M, N), F32), jax.ShapeDtypeStruct((M, N), F32)],
                  compiler_params=_params())(a, b, res, gvec)


def _mm_tn(a, b, *, name, out_dtype=BF, tk=1024, tm_cap=1408, tn_cap=1024):
    S, M = a.shape
    N = b.shape[1]
    tk = _tile(S, tk, 8)
    tm = _tile(M, tm_cap)
    tn = _tile(N, tn_cap)
    nk = S // tk

    def body(a_ref, b_ref, o_ref, acc_ref):
        k = pl.program_id(2)

        @pl.when(k == 0)
        def _():
            acc_ref[...] = jnp.zeros_like(acc_ref)
        acc_ref[...] += lax.dot_general(a_ref[...].astype(BF), b_ref[...].astype(BF), (((0,), (0,)), ((), ())),
                                        preferred_element_type=F32)

        @pl.when(k == nk - 1)
        def _():
            o_ref[...] = acc_ref[...].astype(out_dtype)
    return _pcall(body, name=name, grid=(M // tm, N // tn, nk),
                  in_specs=[pl.BlockSpec((tk, tm), lambda i, j, k: (k, i)),
                            pl.BlockSpec((tk, tn), lambda i, j, k: (k, j))],
                  out_specs=pl.BlockSpec((tm, tn), lambda i, j, k: (i, j)),
                  out_shape=jax.ShapeDtypeStruct((M, N), out_dtype), scratch_shapes=[pltpu.VMEM((tm, tn), F32)],
                  compiler_params=_params())(a, b)


def _rms(x, w):
    return x * lax.rsqrt(jnp.mean(x * x, axis=-1, keepdims=True) + NORM_EPS) * w


def _normmod_fn(x, nw, sc, sh):
    return _rms(x, nw) * (1.0 + sc) + sh


def _gelu(x):
    return 0.5 * x * (1.0 + lax.erf(x * (2.0 ** -0.5)))


def _ln_gelu_fn(zv, w, b):
    v = _gelu(zv)
    mu = jnp.mean(v, axis=-1, keepdims=True)
    var = jnp.mean(jnp.square(v - mu), axis=-1, keepdims=True)
    return (v - mu) * lax.rsqrt(var + NORM_EPS) * w + b


def _sigmoid(x):
    return 1.0 / (1.0 + jnp.exp(-x))


def _row_spec(tm, n):
    return pl.BlockSpec((tm, n), lambda i: (i, 0))


def _vec_spec(n):
    return pl.BlockSpec((1, n), lambda i: (0, 0))


def _acc(ref, val):
    @pl.when(pl.program_id(0) == 0)
    def _():
        ref[...] = jnp.zeros_like(ref)
    ref[...] += val


def _normmod_fwd(x, nw, sc, sh, *, name, tm=512):
    S, Dm = x.shape
    tm = _tile(S, tm, 8)

    def body(x_ref, nw_ref, sc_ref, sh_ref, o_ref):
        o_ref[...] = _normmod_fn(x_ref[...], nw_ref[...], sc_ref[...], sh_ref[...]).astype(BF)
    return _pcall(body, name=name, grid=(S // tm,),
                  in_specs=[_row_spec(tm, Dm), _vec_spec(Dm), _vec_spec(Dm), _vec_spec(Dm)],
                  out_specs=_row_spec(tm, Dm), out_shape=jax.ShapeDtypeStruct((S, Dm), BF),
                  compiler_params=_params())(x, nw, sc, sh)


def _normmod_bwd(dh, x, nw, sc, sh, dres, *, name, tm=256):
    S, Dm = x.shape
    tm = _tile(S, tm, 8)

    def body(dh_ref, x_ref, nw_ref, sc_ref, sh_ref, dres_ref, dx_ref, dnw_ref, dsc_ref, dsh_ref):
        _, vjp = jax.vjp(_normmod_fn, x_ref[...], nw_ref[...], sc_ref[...], sh_ref[...])
        dx, dnw, dsc, dsh = vjp(dh_ref[...])
        dx_ref[...] = dres_ref[...] + dx
        _acc(dnw_ref, dnw)
        _acc(dsc_ref, dsc)
        _acc(dsh_ref, dsh)
    vec = jax.ShapeDtypeStruct((1, Dm), F32)
    return _pcall(body, name=name, grid=(S // tm,),
                  in_specs=[_row_spec(tm, Dm), _row_spec(tm, Dm), _vec_spec(Dm), _vec_spec(Dm), _vec_spec(Dm),
                            _row_spec(tm, Dm)],
                  out_specs=[_row_spec(tm, Dm), _vec_spec(Dm), _vec_spec(Dm), _vec_spec(Dm)],
                  out_shape=[jax.ShapeDtypeStruct((S, Dm), F32), vec, vec, vec],
                  compiler_params=_params())(dh, x, nw, sc, sh, dres)


def _scale_reduce(dx, o, g, *, name, after=None, tm=512):
    S, Dm = dx.shape
    tm = _tile(S, tm, 8)
    extra = [] if after is None else [after]

    def body(dx_ref, o_ref, g_ref, *rest):
        do_ref, dg_ref = rest[-2:]
        dxv = dx_ref[...]
        do_ref[...] = (dxv * g_ref[...]).astype(BF)
        _acc(dg_ref, jnp.sum(dxv * o_ref[...], axis=0, keepdims=True))
    return _pcall(body, name=name, grid=(S // tm,),
                  in_specs=[_row_spec(tm, Dm), _row_spec(tm, Dm), _vec_spec(Dm)] + [ANY] * len(extra),
                  out_specs=[_row_spec(tm, Dm), _vec_spec(Dm)],
                  out_shape=[jax.ShapeDtypeStruct((S, Dm), BF), jax.ShapeDtypeStruct((1, Dm), F32)],
                  compiler_params=_params())(dx, o, g, *extra)


def _head(x, fw, target, *, tm=256):
    S, Dm = x.shape
    tm = _tile(S, tm, 8)

    def body(x_ref, fw_ref, t_ref, dx_ref, dfw_ref, loss_ref):
        y, vjp = jax.vjp(_rms, x_ref[...], fw_ref[...])
        err = y - t_ref[...]
        dx, dfw = vjp(err * (1.0 / Dm))
        dx_ref[...] = dx
        _acc(dfw_ref, dfw)
        part = 0.5 * jnp.sum(jnp.mean(err * err, axis=-1, keepdims=True), axis=0, keepdims=True)
        _acc(loss_ref, jnp.broadcast_to(part, (8, 128)))
    return _pcall(body, name="head", grid=(S // tm,),
                  in_specs=[_row_spec(tm, Dm), _vec_spec(Dm), _row_spec(tm, Dm)],
                  out_specs=[_row_spec(tm, Dm), _vec_spec(Dm), pl.BlockSpec((8, 128), lambda i: (0, 0))],
                  out_shape=[jax.ShapeDtypeStruct((S, Dm), F32), jax.ShapeDtypeStruct((1, Dm), F32),
                             jax.ShapeDtypeStruct((8, 128), F32)],
                  compiler_params=_params())(x, fw, target)


def _tril_mask():
    r = lax.broadcasted_iota(jnp.int32, (SGU_CHUNK, SGU_CHUNK), 0)
    c = lax.broadcasted_iota(jnp.int32, (SGU_CHUNK, SGU_CHUNK), 1)
    return c <= r


def _sgu_fwd(proj, lnw, lnb, w, b_t, *, name, tm=256):
    S = proj.shape[0]
    tm = _tile(S, tm, SGU_CHUNK)

    def body(zu_ref, zv_ref, lnw_ref, lnb_ref, w_ref, bt_ref, o_ref):
        u = _gelu(zu_ref[...].astype(F32))
        vn = _ln_gelu_fn(zv_ref[...].astype(F32), lnw_ref[...], lnb_ref[...]).astype(BF)
        mask = _tril_mask()
        for g in range(SGU_GROUPS):
            wm = jnp.where(mask, w_ref[g], 0.0).astype(BF)
            cols = slice(g * 128, (g + 1) * 128)
            for ci in range(tm // SGU_CHUNK):
                rows = slice(ci * SGU_CHUNK, (ci + 1) * SGU_CHUNK)
                f = jnp.dot(wm, vn[rows, cols], preferred_element_type=F32) + bt_ref[:, g:g + 1]
                o_ref[rows, cols] = (u[rows, cols] * f).astype(BF)
    return _pcall(body, name=name, grid=(S // tm,),
                  in_specs=[pl.BlockSpec((tm, SGU_WIDTH), lambda i: (i, 0)), pl.BlockSpec((tm, SGU_WIDTH), lambda i: (i, 1)),
                            _vec_spec(SGU_WIDTH), _vec_spec(SGU_WIDTH),
                            pl.BlockSpec((SGU_GROUPS, 128, 128), lambda i: (0, 0, 0)),
                            pl.BlockSpec((128, SGU_GROUPS), lambda i: (0, 0))],
                  out_specs=_row_spec(tm, SGU_WIDTH), out_shape=jax.ShapeDtypeStruct((S, SGU_WIDTH), BF),
                  compiler_params=_params())(proj, proj, lnw, lnb, w, b_t)


def _sgu_bwd(dy, proj, lnw, lnb, w, b_t, dproj, *, name, tm=256):
    S = proj.shape[0]
    tm = _tile(S, tm, SGU_CHUNK)

    def body(dy_ref, zu_ref, zv_ref, lnw_ref, lnb_ref, w_ref, bt_ref, _, dz_ref, dlnw_ref, dlnb_ref, dw_ref, dbt_ref,
             f_s, dvn_s):
        first = pl.program_id(0) == 0

        @pl.when(first)
        def _():
            dw_ref[...] = jnp.zeros_like(dw_ref)
            dbt_ref[...] = jnp.zeros_like(dbt_ref)
        u, vjp_u = jax.vjp(_gelu, zu_ref[...].astype(F32))
        vn, vjp_v = jax.vjp(_ln_gelu_fn, zv_ref[...].astype(F32), lnw_ref[...], lnb_ref[...])
        vn = vn.astype(BF)
        dy_v = dy_ref[...]
        df = (dy_v * u).astype(BF)
        mask = _tril_mask()
        for g in range(SGU_GROUPS):
            wm = jnp.where(mask, w_ref[g], 0.0).astype(BF)
            cols = slice(g * 128, (g + 1) * 128)
            dwg = jnp.zeros((128, 128), F32)
            dbg = jnp.zeros((128, 1), F32)
            for ci in range(tm // SGU_CHUNK):
                rows = slice(ci * SGU_CHUNK, (ci + 1) * SGU_CHUNK)
                vn_c = vn[rows, cols]
                df_c = df[rows, cols]
                f_s[rows, cols] = jnp.dot(wm, vn_c, preferred_element_type=F32) + bt_ref[:, g:g + 1]
                dvn_s[rows, cols] = lax.dot_general(wm, df_c, (((0,), (0,)), ((), ())), preferred_element_type=F32)
                dwg = dwg + lax.dot_general(df_c, vn_c, (((1,), (1,)), ((), ())), preferred_element_type=F32)
                dbg = dbg + jnp.sum((dy_v[rows, cols] * u[rows, cols]), axis=1, keepdims=True)
            dw_ref[g] += jnp.where(mask, dwg, 0.0)
            dbt_ref[:, g:g + 1] += dbg
        (dzu,) = vjp_u(dy_v * f_s[...])
        dzv, dlnw, dlnb = vjp_v(dvn_s[...])
        dz_ref[:, :SGU_WIDTH] = dzu.astype(BF)
        dz_ref[:, SGU_WIDTH:] = dzv.astype(BF)
        _acc(dlnw_ref, dlnw)
        _acc(dlnb_ref, dlnb)
    vec = jax.ShapeDtypeStruct((1, SGU_WIDTH), F32)
    return _pcall(body, name=name, grid=(S // tm,),
                  in_specs=[_row_spec(tm, SGU_WIDTH),
                            pl.BlockSpec((tm, SGU_WIDTH), lambda i: (i, 0)), pl.BlockSpec((tm, SGU_WIDTH), lambda i: (i, 1)),
                            _vec_spec(SGU_WIDTH), _vec_spec(SGU_WIDTH),
                            pl.BlockSpec((SGU_GROUPS, 128, 128), lambda i: (0, 0, 0)),
                            pl.BlockSpec((128, SGU_GROUPS), lambda i: (0, 0)), ANY],
                  out_specs=[pl.BlockSpec((tm, 2 * SGU_WIDTH), lambda i: (i, P_Z // (2 * SGU_WIDTH))),
                             _vec_spec(SGU_WIDTH), _vec_spec(SGU_WIDTH),
                             pl.BlockSpec((SGU_GROUPS, 128, 128), lambda i: (0, 0, 0)),
                             pl.BlockSpec((128, SGU_GROUPS), lambda i: (0, 0))],
                  out_shape=[jax.ShapeDtypeStruct(dproj.shape, BF), vec, vec,
                             jax.ShapeDtypeStruct((SGU_GROUPS, 128, 128), F32),
                             jax.ShapeDtypeStruct((128, SGU_GROUPS), F32)],
                  scratch_shapes=[pltpu.VMEM((tm, SGU_WIDTH), F32), pltpu.VMEM((tm, SGU_WIDTH), F32)],
                  input_output_aliases={7: 0},
                  compiler_params=_params())(dy, proj, proj, lnw, lnb, w, b_t, dproj)


def _merge_fwd(a, b, proj, *, name, tm=512):
    S, Dm = a.shape
    tm = _tile(S, tm, 8)
    ga_blk, gb_blk = P_G // Dm, P_G // Dm + 1

    def body(a_ref, b_ref, ga_ref, gb_ref, o_ref):
        o_ref[...] = (_sigmoid(ga_ref[...].astype(F32)) * a_ref[...].astype(F32)
                      + _sigmoid(gb_ref[...].astype(F32)) * b_ref[...].astype(F32)).astype(BF)
    return _pcall(body, name=name, grid=(S // tm,),
                  in_specs=[_row_spec(tm, Dm), _row_spec(tm, Dm), pl.BlockSpec((tm, Dm), lambda i: (i, ga_blk)),
                            pl.BlockSpec((tm, Dm), lambda i: (i, gb_blk))],
                  out_specs=_row_spec(tm, Dm), out_shape=jax.ShapeDtypeStruct((S, Dm), BF),
                  compiler_params=_params())(a, b, proj, proj)


def _merge_bwd(dm, a, b, proj, *, name, tm=512):
    S, Dm = a.shape
    tm = _tile(S, tm, 8)
    ga_blk, gb_blk = P_G // Dm, P_G // Dm + 1

    def body(dm_ref, a_ref, b_ref, ga_ref, gb_ref, da_ref, db_ref, dg_ref):
        dmv = dm_ref[...]
        sa = _sigmoid(ga_ref[...].astype(F32))
        sb = _sigmoid(gb_ref[...].astype(F32))
        da_ref[...] = (dmv * sa).astype(BF)
        db_ref[...] = (dmv * sb).astype(BF)
        dg_ref[:, :Dm] = (dmv * a_ref[...].astype(F32) * sa * (1.0 - sa)).astype(BF)
        dg_ref[:, Dm:] = (dmv * b_ref[...].astype(F32) * sb * (1.0 - sb)).astype(BF)
    return _pcall(body, name=name, grid=(S // tm,),
                  in_specs=[_row_spec(tm, Dm), _row_spec(tm, Dm), _row_spec(tm, Dm),
                            pl.BlockSpec((tm, Dm), lambda i: (i, ga_blk)), pl.BlockSpec((tm, Dm), lambda i: (i, gb_blk))],
                  out_specs=[_row_spec(tm, Dm), _row_spec(tm, Dm), pl.BlockSpec((tm, 2 * Dm), lambda i: (i, P_G // (2 * Dm)))],
                  out_shape=[jax.ShapeDtypeStruct((S, Dm), BF), jax.ShapeDtypeStruct((S, Dm), BF),
                             jax.ShapeDtypeStruct((S, IN_COLS), BF)],
                  compiler_params=_params())(dm, a, b, proj, proj)


def _shift_rows(a, halo, k, up):
    n = a.shape[0]
    r8 = lax.broadcasted_iota(jnp.int32, (8, a.shape[1]), 0)
    if not up:
        rolled = pltpu.roll(a, k, 0)
        patch = jnp.where(r8 < k, pltpu.roll(halo, k, 0), rolled[:8])
        return jnp.concatenate([patch, rolled[8:]], axis=0)
    rolled = pltpu.roll(a, n - k, 0)
    patch = jnp.where(r8 >= 8 - k, pltpu.roll(halo, 8 - k, 0), rolled[n - 8:])
    return jnp.concatenate([rolled[:n - 8], patch], axis=0)


def _conv_taps(a, halo):
    return _shift_rows(a, halo, 2, False), _shift_rows(a, halo, 1, False), a


HALO = 16


def _prev_halo_spec(tm, Fd):
    return pl.BlockSpec((HALO, Fd), lambda i: (jnp.maximum(i * (tm // HALO) - 1, 0), 0))


def _conv_fwd(a_ref, halo_ref, cw_ref, cb_ref):
    halo = jnp.where(pl.program_id(0) > 0, halo_ref[...].astype(F32)[HALO - 8:], 0.0)
    t0, t1, t2 = _conv_taps(a_ref[...].astype(F32), halo)
    return t0, t1, t2, cb_ref[...] + cw_ref[0:1, :] * t0 + cw_ref[1:2, :] * t1 + cw_ref[2:3, :] * t2


def _ffn_act_fwd(a, up, cw, cb, *, name, tm=256):
    S, Fd = a.shape
    tm = _tile(S, tm, HALO)

    def body(a_ref, up_ref, halo_ref, cw_ref, cb_ref, o_ref):
        _, _, _, ac = _conv_fwd(a_ref, halo_ref, cw_ref, cb_ref)
        o_ref[...] = (ac * _sigmoid(ac) * up_ref[...].astype(F32)).astype(BF)
    return _pcall(body, name=name, grid=(S // tm,),
                  in_specs=[_row_spec(tm, Fd), _row_spec(tm, Fd), _prev_halo_spec(tm, Fd),
                            pl.BlockSpec((3, Fd), lambda i: (0, 0)), _vec_spec(Fd)],
                  out_specs=_row_spec(tm, Fd), out_shape=jax.ShapeDtypeStruct((S, Fd), BF),
                  compiler_params=_params())(a, up, a, cw, cb)


def _ffn_act_bwd_a(dhf, a, up, cw, cb, *, name, tm=256):
    S, Fd = a.shape
    tm = _tile(S, tm, HALO)

    def body(dhf_ref, a_ref, up_ref, halo_ref, cw_ref, cb_ref, dac_ref, dup_ref, dcw_ref, dcb_ref):
        t0, t1, t2, ac = _conv_fwd(a_ref, halo_ref, cw_ref, cb_ref)
        s = _sigmoid(ac)
        dhf_v = dhf_ref[...].astype(F32)
        dup_ref[...] = (dhf_v * ac * s).astype(BF)
        dac = dhf_v * up_ref[...].astype(F32) * (s * (1.0 + ac * (1.0 - s)))
        dac_ref[...] = dac.astype(BF)
        _acc(dcb_ref, jnp.sum(dac, axis=0, keepdims=True))
        _acc(dcw_ref, jnp.concatenate([jnp.sum(dac * t0, axis=0, keepdims=True),
                                       jnp.sum(dac * t1, axis=0, keepdims=True),
                                       jnp.sum(dac * t2, axis=0, keepdims=True)], axis=0))
    return _pcall(body, name=name, grid=(S // tm,),
                  in_specs=[_row_spec(tm, Fd), _row_spec(tm, Fd), _row_spec(tm, Fd), _prev_halo_spec(tm, Fd),
                            pl.BlockSpec((3, Fd), lambda i: (0, 0)), _vec_spec(Fd)],
                  out_specs=[_row_spec(tm, Fd), _row_spec(tm, Fd), pl.BlockSpec((3, Fd), lambda i: (0, 0)), _vec_spec(Fd)],
                  out_shape=[jax.ShapeDtypeStruct((S, Fd), BF), jax.ShapeDtypeStruct((S, Fd), BF),
                             jax.ShapeDtypeStruct((3, Fd), F32), jax.ShapeDtypeStruct((1, Fd), F32)],
                  compiler_params=_params())(dhf, a, up, a, cw, cb)


def _ffn_act_bwd_b(dac, cw, *, name, tm=256):
    S, Fd = dac.shape
    tm = _tile(S, tm, HALO)
    last = S // tm - 1

    def body(d_ref, halo_ref, cw_ref, o_ref):
        halo = jnp.where(pl.program_id(0) < last, halo_ref[...].astype(F32)[:8], 0.0)
        d = d_ref[...].astype(F32)
        o_ref[...] = (cw_ref[2:3, :] * d + cw_ref[1:2, :] * _shift_rows(d, halo, 1, True)
                      + cw_ref[0:1, :] * _shift_rows(d, halo, 2, True)).astype(BF)
    return _pcall(body, name=name, grid=(S // tm,),
                  in_specs=[_row_spec(tm, Fd),
                            pl.BlockSpec((HALO, Fd), lambda i: (jnp.minimum((i + 1) * (tm // HALO), S // HALO - 1), 0)),
                            pl.BlockSpec((3, Fd), lambda i: (0, 0))],
                  out_specs=_row_spec(tm, Fd), out_shape=jax.ShapeDtypeStruct((S, Fd), BF),
                  compiler_params=_params())(dac, dac, cw)


def _rope_tables(pos_col, inv_row, m1_row, m2_row):
    S = pos_col.shape[0]
    tm = _tile(S, 512, 8)

    def body(p_ref, inv_ref, m1_ref, m2_ref, c_ref, s1_ref, s2_ref):
        ang = p_ref[...] * inv_ref[...]
        sn = jnp.sin(ang)
        c_ref[...] = jnp.cos(ang)
        s1_ref[...] = -sn * m1_ref[...]
        s2_ref[...] = sn * m2_ref[...]
    sh = jax.ShapeDtypeStruct((S, 128), F32)
    return _pcall(body, name="rope_tables", grid=(S // tm,),
                  in_specs=[pl.BlockSpec((tm, 1), lambda i: (i, 0)), _vec_spec(128), _vec_spec(128), _vec_spec(128)],
                  out_specs=[_row_spec(tm, 128)] * 3, out_shape=[sh, sh, sh], compiler_params=_params())(
                      pos_col, inv_row, m1_row, m2_row)


def _rope_apply(x, c, s1, s2):
    outs = []
    for j in range(x.shape[1] // 128):
        xj = x[:, j * 128:(j + 1) * 128]
        outs.append(xj * c + pltpu.roll(xj, 120, 1) * s1 + pltpu.roll(xj, 8, 1) * s2)
    return outs[0] if len(outs) == 1 else jnp.concatenate(outs, axis=1)


def _rope_apply_t(d, c, s1, s2):
    outs = []
    for j in range(d.shape[1] // 128):
        dj = d[:, j * 128:(j + 1) * 128]
        outs.append(dj * c + pltpu.roll(dj * s1, 8, 1) + pltpu.roll(dj * s2, 120, 1))
    return outs[0] if len(outs) == 1 else jnp.concatenate(outs, axis=1)


def _rope_fwd(proj, c, s1, s2, *, name, tm=512):
    S = proj.shape[0]
    tm = _tile(S, tm, 8)

    def body(q_ref, k_ref, v_ref, c_ref, s1_ref, s2_ref, qo_ref, ko_ref, vo_ref):
        cv, s1v, s2v = c_ref[...], s1_ref[...], s2_ref[...]
        qo_ref[...] = _rope_apply(q_ref[...].astype(F32), cv, s1v, s2v).astype(BF)
        ko_ref[...] = _rope_apply(k_ref[...].astype(F32), cv, s1v, s2v).astype(BF)
        vo_ref[...] = v_ref[...].astype(BF)
    return _pcall(body, name=name, grid=(S // tm,),
                  in_specs=[pl.BlockSpec((tm, Q_END), lambda i: (i, P_Q // Q_END)),
                            pl.BlockSpec((tm, 128), lambda i: (i, P_K // 128)),
                            pl.BlockSpec((tm, 128), lambda i: (i, P_V // 128)),
                            _row_spec(tm, 128), _row_spec(tm, 128), _row_spec(tm, 128)],
                  out_specs=[_row_spec(tm, Q_END), _row_spec(tm, 128), _row_spec(tm, 128)],
                  out_shape=[jax.ShapeDtypeStruct((S, Q_END), BF), jax.ShapeDtypeStruct((S, 128), BF),
                             jax.ShapeDtypeStruct((S, 128), BF)],
                  compiler_params=_params())(proj, proj, proj, c, s1, s2)


def _rope_bwd(dq, dk, dv, c, s1, s2, dproj, *, name, tm=512):
    S = dq.shape[0]
    tm = _tile(S, tm, 8)
    qsteps = Q_END // 256

    def body(dq_ref, dk_ref, dv_ref, c_ref, s1_ref, s2_ref, _, o_ref):
        cv, s1v, s2v = c_ref[...], s1_ref[...], s2_ref[...]
        j = pl.program_id(1)

        @pl.when(j < qsteps)
        def _():
            o_ref[...] = _rope_apply_t(dq_ref[...].astype(F32), cv, s1v, s2v).astype(BF)

        @pl.when(j == qsteps)
        def _():
            o_ref[:, :128] = _rope_apply_t(dk_ref[...], cv, s1v, s2v).astype(BF)
            o_ref[:, 128:] = dv_ref[...].astype(BF)
    row = lambda i, j: (i, 0)
    return _pcall(body, name=name, grid=(S // tm, qsteps + 1),
                  in_specs=[pl.BlockSpec((tm, 256), lambda i, j: (i, jnp.minimum(j, qsteps - 1))),
                            pl.BlockSpec((tm, 128), row), pl.BlockSpec((tm, 128), row), pl.BlockSpec((tm, 128), row),
                            pl.BlockSpec((tm, 128), row), pl.BlockSpec((tm, 128), row), ANY],
                  out_specs=pl.BlockSpec((tm, 256), lambda i, j: (i, P_Q // 256 + j)),
                  out_shape=jax.ShapeDtypeStruct(dproj.shape, BF), input_output_aliases={6: 0},
                  compiler_params=_params())(dq, dk, dv, c, s1, s2, dproj)


def _lane_lo(shape):
    return lax.broadcasted_iota(jnp.int32, shape, 1) < HEAD_DIM


def _stack_heads(x, g):
    lo = _lane_lo((ATTN_BLOCK, 128))
    zero = jnp.zeros((ATTN_BLOCK, 128), x.dtype)
    parts = []
    for p in range(Q_PER_KV // 2):
        xp = x[:, (g * 4 + p) * 128:(g * 4 + p + 1) * 128]
        parts += [jnp.where(lo, xp, zero), jnp.where(lo, zero, xp)]
    return jnp.concatenate(parts, axis=0)


def _unstack_heads(o2):
    lo = _lane_lo((ATTN_BLOCK, 128))
    return [jnp.where(lo, o2[2 * p * ATTN_BLOCK:(2 * p + 1) * ATTN_BLOCK], o2[(2 * p + 1) * ATTN_BLOCK:(2 * p + 2) * ATTN_BLOCK])
            for p in range(Q_PER_KV // 2)]


def _dup_half(prev, cur, g):
    x = jnp.concatenate([prev, cur], axis=0).astype(F32)
    lo = _lane_lo(x.shape)
    r = pltpu.roll(x, HEAD_DIM, 1)
    return (jnp.where(lo, x, r) if g == 0 else jnp.where(lo, r, x)).astype(BF)


def _fold_halves(x):
    return x + pltpu.roll(x, HEAD_DIM, 1)


def _attn_bias():
    i = lax.broadcasted_iota(jnp.int32, (Q_PER_KV * ATTN_BLOCK, 2 * ATTN_BLOCK), 0) & (ATTN_BLOCK - 1)
    j = lax.broadcasted_iota(jnp.int32, (Q_PER_KV * ATTN_BLOCK, 2 * ATTN_BLOCK), 1)
    band = (j > i) & (j <= i + ATTN_BLOCK)
    return jnp.stack([jnp.where(band & (j >= ATTN_BLOCK), 0.0, -jnp.inf), jnp.where(band, 0.0, -jnp.inf)]).astype(F32)


def _attn_probs(qs, kb, sink, bias):
    s = lax.dot_general(qs, kb, (((1,), (1,)), ((), ())), preferred_element_type=F32) * (HEAD_DIM ** -0.5) + bias
    m = jnp.maximum(jnp.max(s, axis=-1, keepdims=True), sink)
    p = jnp.exp(s - m)
    es = jnp.exp(sink - m)
    inv = 1.0 / (jnp.sum(p, axis=-1, keepdims=True) + es)
    return p, inv, es * inv


def _attn_specs(S):
    nb = S // ATTN_BLOCK
    qs = pl.BlockSpec((ATTN_BLOCK, Q_END), lambda n: (n, 0))
    cur = pl.BlockSpec((ATTN_BLOCK, 128), lambda n: (n, 0))
    prev = pl.BlockSpec((ATTN_BLOCK, 128), lambda n: (jnp.maximum(n - 1, 0), 0))
    sink = pl.BlockSpec((N_KV_HEADS, Q_PER_KV * ATTN_BLOCK, 1), lambda n: (0, 0, 0))
    bias = pl.BlockSpec((None, Q_PER_KV * ATTN_BLOCK, 2 * ATTN_BLOCK), lambda n: (jnp.minimum(n, 1), 0, 0))
    return nb, qs, cur, prev, sink, bias


def _attn_fwd(q, k, v, sink_rows, bias, *, name):
    S = q.shape[0]
    nb, qs, cur, prev, sink, bs = _attn_specs(S)

    def body(q_ref, kp_ref, kc_ref, vp_ref, vc_ref, sk_ref, b_ref, o_ref):
        for g in range(N_KV_HEADS):
            kb = _dup_half(kp_ref[...], kc_ref[...], g)
            vb = _dup_half(vp_ref[...], vc_ref[...], g)
            p, inv, _ = _attn_probs(_stack_heads(q_ref[...], g), kb, sk_ref[g], b_ref[...])
            o2 = jnp.dot(p.astype(BF), vb, preferred_element_type=F32) * inv
            for t, tile in enumerate(_unstack_heads(o2)):
                o_ref[:, (g * 4 + t) * 128:(g * 4 + t + 1) * 128] = tile.astype(BF)
    return _pcall(body, name=name, grid=(nb,), in_specs=[qs, prev, cur, prev, cur, sink, bs], out_specs=qs,
                  out_shape=jax.ShapeDtypeStruct(q.shape, BF), compiler_params=_params())(q, k, k, v, v, sink_rows, bias)


def _attn_bwd(do, q, k, v, sink_rows, bias, *, name):
    S = q.shape[0]
    nb, qs, cur, prev, sink, bs = _attn_specs(S)
    full = pl.BlockSpec((S, 128), lambda n: (0, 0))
    dsk_spec = pl.BlockSpec((N_KV_HEADS, Q_PER_KV, 128), lambda n: (0, 0, 0))

    def body(do_ref, q_ref, kp_ref, kc_ref, vp_ref, vc_ref, sk_ref, b_ref, dq_ref, dk_ref, dv_ref, dsk_ref):
        n = pl.program_id(0)

        @pl.when(n == 0)
        def _():
            dk_ref[...] = jnp.zeros_like(dk_ref)
            dv_ref[...] = jnp.zeros_like(dv_ref)
            dsk_ref[...] = jnp.zeros_like(dsk_ref)
        sub = lax.broadcasted_iota(jnp.int32, (Q_PER_KV, 128), 0)
        dkf, dvf = [], []
        for g in range(N_KV_HEADS):
            qst = _stack_heads(q_ref[...], g)
            dos = _stack_heads(do_ref[...], g)
            kb = _dup_half(kp_ref[...], kc_ref[...], g)
            vb = _dup_half(vp_ref[...], vc_ref[...], g)
            pu, inv, ps = _attn_probs(qst, kb, sk_ref[g], b_ref[...])
            p = pu * inv
            dp = lax.dot_general(dos, vb, (((1,), (1,)), ((), ())), preferred_element_type=F32)
            dd = jnp.sum(p * dp, axis=-1, keepdims=True)
            ds = (p * (dp - dd) * (HEAD_DIM ** -0.5)).astype(BF)
            dq2 = jnp.dot(ds, kb, preferred_element_type=F32)
            for t, tile in enumerate(_unstack_heads(dq2)):
                dq_ref[:, (g * 4 + t) * 128:(g * 4 + t + 1) * 128] = tile.astype(BF)
            dkf.append(_fold_halves(lax.dot_general(ds, qst, (((0,), (0,)), ((), ())), preferred_element_type=F32)))
            dvf.append(_fold_halves(lax.dot_general(p.astype(BF), dos, (((0,), (0,)), ((), ())),
                                                    preferred_element_type=F32)))
            dsr = -(ps * dd)
            upd = jnp.zeros((Q_PER_KV, 128), F32)
            for h in range(Q_PER_KV):
                upd = jnp.where(sub == h, jnp.sum(dsr[h * ATTN_BLOCK:(h + 1) * ATTN_BLOCK]), upd)
            dsk_ref[g] += upd
        lo = _lane_lo((2 * ATTN_BLOCK, 128))
        dkb = jnp.where(lo, dkf[0], dkf[1])
        dvb = jnp.where(lo, dvf[0], dvf[1])
        r0 = pl.multiple_of(n * ATTN_BLOCK, ATTN_BLOCK)
        dk_ref[pl.ds(r0, ATTN_BLOCK), :] += dkb[ATTN_BLOCK:]
        dv_ref[pl.ds(r0, ATTN_BLOCK), :] += dvb[ATTN_BLOCK:]

        @pl.when(n > 0)
        def _():
            rp = pl.multiple_of((n - 1) * ATTN_BLOCK, ATTN_BLOCK)
            dk_ref[pl.ds(rp, ATTN_BLOCK), :] += dkb[:ATTN_BLOCK]
            dv_ref[pl.ds(rp, ATTN_BLOCK), :] += dvb[:ATTN_BLOCK]
    return _pcall(body, name=name, grid=(nb,), in_specs=[qs, qs, prev, cur, prev, cur, sink, bs],
                  out_specs=[qs, full, full, dsk_spec],
                  out_shape=[jax.ShapeDtypeStruct(q.shape, BF), jax.ShapeDtypeStruct((S, 128), F32),
                             jax.ShapeDtypeStruct((S, 128), F32), jax.ShapeDtypeStruct((N_KV_HEADS, Q_PER_KV, 128), F32)],
                  compiler_params=_params())(do, q, k, k, v, v, sink_rows, bias)


def _ada_fwd(c_all, ada_w):
    ncol = ada_w.shape[2]

    def body(c_ref, w_ref, o_ref):
        cv = c_ref[...]
        ca = (cv * _sigmoid(cv)).astype(BF)
        for l in range(DEPTH):
            o_ref[:, l * ncol:(l + 1) * ncol] = jnp.dot(ca, w_ref[l].astype(BF), preferred_element_type=F32)
    return _pcall(body, name="ada_fwd", out_shape=jax.ShapeDtypeStruct((N_DEV, DEPTH * ncol), F32),
                  compiler_params=_params())(c_all, ada_w)


def _ada_bwd(c_all, dm):
    ncol = dm.shape[2]

    def body(c_ref, dm_ref, o_ref):
        cv = c_ref[...]
        ca = (cv * _sigmoid(cv)).astype(BF)
        for l in range(DEPTH):
            o_ref[l] = lax.dot_general(ca, dm_ref[l].astype(BF), (((0,), (0,)), ((), ())), preferred_element_type=F32)
    return _pcall(body, name="ada_bwd", out_shape=jax.ShapeDtypeStruct((DEPTH, D_MODEL, ncol), F32),
                  compiler_params=_params())(c_all, dm)


def _adamw(w, g, m, v, *, name):
    R, C = w.shape
    tr = R
    for t in range(8, 513, 8):
        if R % t == 0:
            tr = t
    c1 = 1.0 - ADAM_B1 ** ADAM_STEP
    c2 = 1.0 - ADAM_B2 ** ADAM_STEP

    def body(w_ref, g_ref, m_ref, v_ref, d_ref, mo_ref, vo_ref):
        gv = g_ref[...]
        mn = ADAM_B1 * m_ref[...] + (1.0 - ADAM_B1) * gv
        vn = ADAM_B2 * v_ref[...] + (1.0 - ADAM_B2) * (gv * gv)
        mo_ref[...] = mn
        vo_ref[...] = vn
        d_ref[...] = -ADAM_LR * ((mn / c1) / (jnp.sqrt(vn / c2) + ADAM_EPS) + ADAM_WD * w_ref[...])
    spec = pl.BlockSpec((tr, C), lambda i: (i, 0))
    sh = jax.ShapeDtypeStruct((R, C), F32)
    return _pcall(body, name=name, grid=(R // tr,), in_specs=[spec] * 4, out_specs=[spec] * 3, out_shape=[sh, sh, sh],
                  compiler_params=_params())(w, g, m, v)


def _sum8(parts, *, name):
    _, R, C = parts.shape
    tr = R
    for t in range(16, 257, 16):
        if R % t == 0:
            tr = t

    def body(p_ref, o_ref):
        acc = p_ref[0].astype(F32)
        for k in range(1, N_DEV):
            acc = acc + p_ref[k].astype(F32)
        o_ref[...] = acc
    return _pcall(body, name=name, grid=(R // tr,), in_specs=[pl.BlockSpec((N_DEV, tr, C), lambda i: (0, i, 0))],
                  out_specs=pl.BlockSpec((tr, C), lambda i: (i, 0)), out_shape=jax.ShapeDtypeStruct((R, C), F32),
                  compiler_params=_params())(parts)


MESH_ID = pl.DeviceIdType.MESH
ANY = pl.BlockSpec(memory_space=pl.ANY)


def _all_gather(x, *, name, after=None):
    R, C = x.shape
    extra = [] if after is None else [after]

    def body(x_ref, *rest):
        out_ref, send_sems, recv_sems, local_sem = rest[-4:]
        mx, my, mc = lax.axis_index("x"), lax.axis_index("y"), lax.axis_index("c")
        me, sibling = (mx, my, mc), (mx, my, 1 - mc)
        chips = [(1 - mx, my), (mx, 1 - my), (1 - mx, 1 - my)]

        def blk(px, py, pc):
            return out_ref.at[4 * px + 2 * py + pc]

        def copy(k, block, to, src=None):
            return pltpu.make_async_remote_copy(
                src_ref=blk(*block) if src is None else src, dst_ref=blk(*block),
                send_sem=send_sems.at[k], recv_sem=recv_sems.at[k], device_id=to, device_id_type=MESH_ID)

        mine = pltpu.make_async_copy(x_ref, blk(*me), local_sem)
        mine.start()
        first = [copy(0, me, sibling, src=x_ref)]
        first += [copy(1 + j, me, (*chip, mc), src=x_ref) for j, chip in enumerate(chips)]
        for cp in first:
            cp.start()
        passed = [copy(4 + j, (*chip, mc), sibling) for j, chip in enumerate(chips)]
        for j, chip in enumerate(chips):
            copy(1 + j, (*chip, mc), me).wait_recv()
            passed[j].start()
        copy(0, sibling, me).wait_recv()
        for j, chip in enumerate(chips):
            copy(4 + j, (*chip, 1 - mc), me).wait_recv()
        for cp in first + passed:
            cp.wait_send()
        mine.wait()
    return _pcall(body, name=name, in_specs=[ANY] * (1 + len(extra)), out_specs=ANY,
                  out_shape=jax.ShapeDtypeStruct((N_DEV, R, C), x.dtype),
                  scratch_shapes=[pltpu.SemaphoreType.DMA((7,)), pltpu.SemaphoreType.DMA((7,)), pltpu.SemaphoreType.DMA],
                  compiler_params=pltpu.CompilerParams(has_side_effects=True))(x, *extra)


HBM_SPEC = pl.BlockSpec(memory_space=pltpu.HBM)
SEM_SPEC = pl.BlockSpec(memory_space=pltpu.SEMAPHORE)
DATAFLOW = pltpu.SideEffectType.DATAFLOW_SIDE_EFFECTING


def _coords():
    return lax.axis_index("x"), lax.axis_index("y"), lax.axis_index("c")


def _other_chips(mx, my):
    return [(1 - mx, my), (mx, 1 - my), (1 - mx, 1 - my)]


def _plan_gather_ici(refs, send, recv):
    src, land = refs
    mx, my, mc = _coords()
    return [pltpu.make_async_remote_copy(src_ref=src, dst_ref=land.at[mc, 2 * mx + my], send_sem=send[j], recv_sem=recv[j],
                                         device_id=(px, py, mc), device_id_type=MESH_ID)
            for j, (px, py) in enumerate(_other_chips(mx, my))]


def _plan_gather_d2d(refs, send, recv):
    (land,) = refs
    mx, my, mc = _coords()
    return [pltpu.make_async_remote_copy(src_ref=land.at[mc], dst_ref=land.at[mc], send_sem=send[0], recv_sem=recv[0],
                                         device_id=(mx, my, 1 - mc), device_id_type=MESH_ID)]


def _plan_reduce_d2d(refs, send, recv):
    g, land = refs
    mx, my, mc = _coords()
    return [pltpu.make_async_remote_copy(src_ref=g.at[1 - mc], dst_ref=land, send_sem=send[0], recv_sem=recv[0],
                                         device_id=(mx, my, 1 - mc), device_id_type=MESH_ID)]


def _plan_reduce_ici(refs, send, recv):
    h, land = refs
    mx, my, mc = _coords()
    return [pltpu.make_async_remote_copy(src_ref=h.at[2 * px + py], dst_ref=land.at[j], send_sem=send[j], recv_sem=recv[j],
                                         device_id=(px, py, mc), device_id_type=MESH_ID)
            for j, (px, py) in enumerate(_other_chips(mx, my))]


def _rdma_start(bufs, n, plan, *, name, after=None):
    nb = len(bufs)
    extra = [] if after is None else [after]
    ne = len(extra)

    def body(*refs):
        ins, send, recv = refs[:nb], refs[nb + ne:nb + ne + n], refs[nb + ne + n:nb + ne + 2 * n]
        token = refs[-1]
        for cp in plan(ins, send, recv):
            cp.start()
        token[...] = jnp.zeros_like(token)
    out = _pcall(body, name=name,
                 out_shape=tuple([pltpu.SemaphoreType.DMA(())] * (2 * n) + [pltpu.HBM(b.shape, b.dtype) for b in bufs]
                                 + [jax.ShapeDtypeStruct((8, 128), F32)]),
                 in_specs=tuple([HBM_SPEC] * nb + [ANY] * ne),
                 out_specs=tuple([SEM_SPEC] * (2 * n) + [HBM_SPEC] * nb + [pl.BlockSpec(memory_space=pltpu.VMEM)]),
                 input_output_aliases={i: 2 * n + i for i in range(nb)},
                 compiler_params=pltpu.CompilerParams(has_side_effects=DATAFLOW))(
                     *[pltpu.with_memory_space_constraint(b, pltpu.HBM) for b in bufs], *extra)
    return list(out[:2 * n]), list(out[2 * n:2 * n + nb]), out[-1]


def _rdma_wait(sems, bufs, n, plan, after, *, name):
    nb = len(bufs)

    def body(*refs):
        ins, send, recv = refs[:nb], refs[nb:nb + n], refs[nb + n:nb + 2 * n]
        for cp in plan(ins, send, recv):
            cp.wait_send()
            cp.wait_recv()
    out = _pcall(body, name=name, out_shape=tuple(pltpu.HBM(b.shape, b.dtype) for b in bufs),
                 in_specs=tuple([HBM_SPEC] * nb + [SEM_SPEC] * (2 * n) + [ANY]), out_specs=tuple([HBM_SPEC] * nb),
                 input_output_aliases={i: i for i in range(nb)},
                 compiler_params=pltpu.CompilerParams(has_side_effects=DATAFLOW))(*bufs, *sems, after)
    return list(out)


def _sum_pair(g, land, cidx, *, name):
    _, nchip, R, C = g.shape
    tr = _tile(R, 512, 16)

    def body(c_ref, g_ref, l_ref, o_ref):
        o_ref[...] = (g_ref[...].astype(F32) + l_ref[...].astype(F32)).astype(BF)
    grid_spec = pltpu.PrefetchScalarGridSpec(
        num_scalar_prefetch=1, grid=(nchip, R // tr),
        in_specs=[pl.BlockSpec((None, None, tr, C), lambda p, i, c_ref: (c_ref[0], p, i, 0)),
                  pl.BlockSpec((None, tr, C), lambda p, i, c_ref: (p, i, 0))],
        out_specs=pl.BlockSpec((None, tr, C), lambda p, i, c_ref: (p, i, 0)))
    return _pcall(body, name=name, grid_spec=grid_spec, out_shape=jax.ShapeDtypeStruct((nchip, R, C), BF),
                  compiler_params=_params())(cidx, g, land)


def _sum_chips(h, land, chipidx, *, name):
    _, R, C = h.shape
    tr = _tile(R, 512, 16)

    def body(c_ref, h_ref, l_ref, o_ref):
        acc = h_ref[...].astype(F32)
        for j in range(3):
            acc = acc + l_ref[j].astype(F32)
        o_ref[...] = acc
    grid_spec = pltpu.PrefetchScalarGridSpec(
        num_scalar_prefetch=1, grid=(R // tr,),
        in_specs=[pl.BlockSpec((None, tr, C), lambda i, c_ref: (c_ref[0], i, 0)),
                  pl.BlockSpec((3, tr, C), lambda i, c_ref: (0, i, 0))],
        out_specs=pl.BlockSpec((tr, C), lambda i, c_ref: (i, 0)))
    return _pcall(body, name=name, grid_spec=grid_spec, out_shape=jax.ShapeDtypeStruct((R, C), F32),
                  compiler_params=_params())(chipidx, h, land)


PART_IN = ("w_in",)
PART_MIX = ("proj_a", "proj_b", "w_out")
PART_FFN = ("ffn_w_gate", "ffn_w_up", "ffn_w_down")


def _part_rows(names):
    return sum(BIG_ROWS[n] for n in names)


def _part_offsets(names):
    off, r = {}, 0
    for n in names:
        off[n] = r
        r += BIG_ROWS[n]
    return off


def _pack_shards(shards, l, names):
    return jnp.concatenate([(shards[n][l].T if n in COL_SHARDED else shards[n][l]).astype(BF) for n in names], axis=0)


def _unpack_weights(full8, names):
    off = _part_offsets(names)

    def whole(n):
        return full8[:, off[n]:off[n] + BIG_ROWS[n], :].reshape(N_DEV * BIG_ROWS[n], 1024)
    out = {}
    if "w_in" in names:
        wt_in = whole("w_in")
        out["wt_in"] = jnp.concatenate([wt_in[V_END:], wt_in[:V_END]], axis=0)
    for n in ("proj_a", "proj_b", "w_out"):
        if n in names:
            out[n] = whole(n)
    if "ffn_w_gate" in names:
        out["wt_gate"], out["wt_up"], out["w_down"] = whole("ffn_w_gate"), whole("ffn_w_up"), whole("ffn_w_down")
    return out


def _from_land(land):
    return land.transpose(1, 0, 2, 3).reshape(N_DEV, land.shape[2], 1024)


def _pack_grads(wg, names):
    full = {"proj_a": wg.get("proj_a"), "proj_b": wg.get("proj_b"), "w_out": wg.get("w_out"), "ffn_w_down": wg.get("w_down"),
            "ffn_w_gate": wg.get("wt_gate"), "ffn_w_up": wg.get("wt_up")}
    if "w_in" in names:
        full["w_in"] = jnp.concatenate([wg["wt_in"][P_Q:], wg["wt_in"][:P_Q]], axis=0)
    blocks = jnp.concatenate([full[n].reshape(N_DEV, BIG_ROWS[n], 1024) for n in names], axis=1)
    return blocks.reshape(4, 2, _part_rows(names), 1024).transpose(1, 0, 2, 3)


def _unpack_shard_grads(gs, names):
    off = _part_offsets(names)
    out = {}
    for n in names:
        blk = gs[off[n]:off[n] + BIG_ROWS[n]]
        out[n] = blk.T if n in COL_SHARDED else blk
    return out


def _rope_setup(positions):
    S = positions.shape[0]
    inv = ROPE_THETA ** (-jnp.arange(0, ROT_DIM, 2, dtype=F32) / ROT_DIM)
    lane = np.arange(128) % HEAD_DIM
    half = ROT_DIM // 2
    inv_row = jnp.where(lane < ROT_DIM, jnp.tile(inv, 128 // half), 0.0)[None, :].astype(F32)
    m1_row = jnp.asarray((lane < half).astype(np.float32))[None, :]
    m2_row = jnp.asarray(((lane >= half) & (lane < ROT_DIM)).astype(np.float32))[None, :]
    return (*_rope_tables(positions.astype(F32).reshape(S, 1), inv_row, m1_row, m2_row), _attn_bias())


def _hook(hooks, point, after):
    f = None if hooks is None else hooks.get(point)
    return None if f is None else f(after)


def _layer_fwd(l, x, mod_l, W, small, rope, hooks=None):
    rc, rs1, rs2, bias = rope
    sh1, sc1, g1, sh2, sc2, g2 = [mod_l[i * D_MODEL:(i + 1) * D_MODEL][None, :] for i in range(6)]
    nw1, nw2 = small["norm1_w"][l][None, :], small["norm2_w"][l][None, :]
    h = _normmod_fwd(x, nw1, sc1, sh1, name=f"normmod1_fwd{l}")
    tok = _hook(hooks, "mm_in", h)
    proj = _mm(h, W["wt_in"], nt=True, out_dtype=BF, name=f"mm_in{l}", after=tok, tn_cap=768)
    q_r, k_r, v_b = _rope_fwd(proj, rc, rs1, rs2, name=f"rope_fwd{l}")
    sink_rows = jnp.repeat(small["attn_sinks"][l].reshape(N_KV_HEADS, Q_PER_KV), ATTN_BLOCK, axis=1)[..., None]
    y_attn = _attn_fwd(q_r, k_r, v_b, sink_rows, bias, name=f"attn_fwd{l}")
    lnw, lnb = small["sgu_ln_w"][l][None, :], small["sgu_ln_b"][l][None, :]
    sgu_bt = small["sgu_b"][l].T
    y_sgu = _sgu_fwd(proj, lnw, lnb, small["sgu_w"][l], sgu_bt, name=f"sgu_fwd{l}")
    tok = _hook(hooks, "mm_pa", y_sgu)
    a_br = _mm(y_sgu, W["proj_a"], nt=False, out_dtype=BF, name=f"mm_pa{l}", after=tok)
    b_br = _mm(y_attn, W["proj_b"], nt=False, out_dtype=BF, name=f"mm_pb{l}")
    merged = _merge_fwd(a_br, b_br, proj, name=f"merge_fwd{l}")
    x1, o1 = _mm(merged, W["w_out"], nt=False, out_dtype=F32, name=f"mm_out{l}", res=x, gvec=g1)
    h2 = _normmod_fwd(x1, nw2, sc2, sh2, name=f"normmod2_fwd{l}")
    a_g = _mm(h2, W["wt_gate"], nt=True, out_dtype=BF, name=f"mm_gate{l}", tn_cap=1408)
    a_u = _mm(h2, W["wt_up"], nt=True, out_dtype=BF, name=f"mm_up{l}", tn_cap=1408)
    cw, cb = small["ffn_conv_w"][l], small["ffn_conv_b"][l][None, :]
    hf = _ffn_act_fwd(a_g, a_u, cw, cb, name=f"ffn_act_fwd{l}")
    x2, o2 = _mm(hf, W["w_down"], nt=False, out_dtype=F32, name=f"mm_down{l}", res=x1, gvec=g2)
    saved = dict(x=x, h=h, proj=proj, q_r=q_r, k_r=k_r, v_b=v_b, sink_rows=sink_rows, y_attn=y_attn, y_sgu=y_sgu,
                 a_br=a_br, b_br=b_br, merged=merged, x1=x1, o1=o1, h2=h2, a_g=a_g, a_u=a_u, hf=hf, o2=o2)
    return x2, saved


def _layer_bwd(l, dx, mod_l, W, small, rope, sv, hooks=None, wg=None):
    rc, rs1, rs2, bias = rope
    sh1, sc1, g1, sh2, sc2, g2 = [mod_l[i * D_MODEL:(i + 1) * D_MODEL][None, :] for i in range(6)]
    nw1, nw2 = small["norm1_w"][l][None, :], small["norm2_w"][l][None, :]
    cw, cb = small["ffn_conv_w"][l], small["ffn_conv_b"][l][None, :]
    lnw, lnb = small["sgu_ln_w"][l][None, :], small["sgu_ln_b"][l][None, :]
    sgu_bt = small["sgu_b"][l].T
    wg = {} if wg is None else wg
    do2, dg2 = _scale_reduce(dx, sv["o2"], g2, name=f"scale2_{l}", after=_hook(hooks, "scale2", dx))
    dhf = _mm(do2, W["w_down"], nt=True, out_dtype=BF, name=f"mm_down_dx{l}", tn_cap=1408)
    wg["w_down"] = _mm_tn(sv["hf"], do2, name=f"mm_down_dw{l}")
    dac, dup, dcw, dcb = _ffn_act_bwd_a(dhf, sv["a_g"], sv["a_u"], cw, cb, name=f"ffn_act_bwd_a{l}")
    da = _ffn_act_bwd_b(dac, cw, name=f"ffn_act_bwd_b{l}")
    dh2 = _mm([da, dup], [W["wt_gate"], W["wt_up"]], nt=False, out_dtype=F32, name=f"mm_gu_dx{l}",
              after=_hook(hooks, "mm_gu_dx", da))
    wg["wt_gate"] = _mm_tn(da, sv["h2"], name=f"mm_gate_dw{l}")
    wg["wt_up"] = _mm_tn(dup, sv["h2"], name=f"mm_up_dw{l}")
    dx1, dnw2, dsc2, dsh2 = _normmod_bwd(dh2, sv["x1"], nw2, sc2, sh2, dx, name=f"normmod2_bwd{l}")
    do1, dg1 = _scale_reduce(dx1, sv["o1"], g1, name=f"scale1_{l}", after=_hook(hooks, "scale1", dx1))
    dmerged = _mm(do1, W["w_out"], nt=True, out_dtype=F32, name=f"mm_out_dx{l}")
    wg["w_out"] = _mm_tn(sv["merged"], do1, name=f"mm_out_dw{l}")
    d_a, d_b, dproj = _merge_bwd(dmerged, sv["a_br"], sv["b_br"], sv["proj"], name=f"merge_bwd{l}")
    dysgu = _mm(d_a, W["proj_a"], nt=True, out_dtype=F32, name=f"mm_pa_dx{l}", after=_hook(hooks, "mm_pa_dx", d_a))
    dyattn = _mm(d_b, W["proj_b"], nt=True, out_dtype=BF, name=f"mm_pb_dx{l}")
    wg["proj_a"] = _mm_tn(sv["y_sgu"], d_a, name=f"mm_pa_dw{l}")
    wg["proj_b"] = _mm_tn(sv["y_attn"], d_b, name=f"mm_pb_dw{l}")
    dproj, dlnw, dlnb, dsguw, dsgubt = _sgu_bwd(dysgu, sv["proj"], lnw, lnb, small["sgu_w"][l], sgu_bt, dproj,
                                                name=f"sgu_bwd{l}")
    dq_r, dk_r, dv_b, dsk = _attn_bwd(dyattn, sv["q_r"], sv["k_r"], sv["v_b"], sv["sink_rows"], bias, name=f"attn_bwd{l}")
    dproj = _rope_bwd(dq_r, dk_r, dv_b, rc, rs1, rs2, dproj, name=f"rope_bwd{l}")
    dh = _mm(dproj, W["wt_in"], nt=False, out_dtype=F32, name=f"mm_in_dx{l}")
    wg["wt_in"] = _mm_tn(dproj, sv["h"], name=f"mm_in_dw{l}")
    dx0, dnw1, dsc1, dsh1 = _normmod_bwd(dh, sv["x"], nw1, sc1, sh1, dx1, name=f"normmod1_bwd{l}")
    dmod = jnp.concatenate([dsh1, dsc1, dg1, dsh2, dsc2, dg2], axis=1)[0]
    sg = {"norm1_w": dnw1[0], "norm2_w": dnw2[0], "attn_sinks": dsk[:, :, 0].reshape(N_Q_HEADS),
          "sgu_ln_w": dlnw[0], "sgu_ln_b": dlnb[0], "sgu_w": dsguw, "sgu_b": dsgubt.T,
          "ffn_conv_w": dcw, "ffn_conv_b": dcb[0]}
    return dx0, wg, sg, dmod


SMALL = ("ada_b", "norm1_w", "attn_sinks", "sgu_ln_w", "sgu_ln_b", "sgu_w", "sgu_b", "norm2_w", "ffn_conv_b", "final_norm_w")
WEIGHT_ORDER = ("ada_w", "ada_b", "norm1_w", "w_in", "attn_sinks", "sgu_ln_w", "sgu_ln_b", "sgu_w", "sgu_b", "proj_a", "proj_b",
                "w_out", "norm2_w", "ffn_w_gate", "ffn_w_up", "ffn_conv_w", "ffn_conv_b", "ffn_w_down", "final_norm_w")


def _flat_pack(arrs, rows):
    flat = jnp.concatenate([a.reshape(-1) for a in arrs])
    return jnp.pad(flat, (0, rows * 1024 - flat.shape[0])).reshape(rows, 1024)


def _flat_unpack(buf, shapes):
    flat = buf.reshape(-1)
    out, o = [], 0
    for s in shapes:
        n = int(np.prod(s))
        out.append(flat[o:o + n].reshape(s))
        o += n
    return out


def _adam2d(w, g, m, v, *, name):
    shp = w.shape
    r2 = (int(np.prod(shp[:-1])), shp[-1]) if len(shp) > 1 else (1, shp[0])
    d, mn, vn = _adamw(w.reshape(r2), g.reshape(r2), m.reshape(r2), v.reshape(r2), name=name)
    return d.reshape(shp), mn.reshape(shp), vn.reshape(shp)


def kernel(x, c, positions, ada_w, ada_b, norm1_w, w_in, attn_sinks, sgu_ln_w, sgu_ln_b, sgu_w, sgu_b, proj_a, proj_b, w_out, norm2_w, ffn_w_gate, ffn_w_up, ffn_conv_w, ffn_conv_b, ffn_w_down, final_norm_w, loss_target, m_ada_w, m_ada_b, m_norm1_w, m_w_in, m_attn_sinks, m_sgu_ln_w, m_sgu_ln_b, m_sgu_w, m_sgu_b, m_proj_a, m_proj_b, m_w_out, m_norm2_w, m_ffn_w_gate, m_ffn_w_up, m_ffn_conv_w, m_ffn_conv_b, m_ffn_w_down, m_final_norm_w, v_ada_w, v_ada_b, v_norm1_w, v_w_in, v_attn_sinks, v_sgu_ln_w, v_sgu_ln_b, v_sgu_w, v_sgu_b, v_proj_a, v_proj_b, v_w_out, v_norm2_w, v_ffn_w_gate, v_ffn_w_up, v_ffn_conv_w, v_ffn_conv_b, v_ffn_w_down, v_final_norm_w):
    wts = dict(ada_w=ada_w, ada_b=ada_b, norm1_w=norm1_w, w_in=w_in, attn_sinks=attn_sinks, sgu_ln_w=sgu_ln_w,
               sgu_ln_b=sgu_ln_b, sgu_w=sgu_w, sgu_b=sgu_b, proj_a=proj_a, proj_b=proj_b, w_out=w_out, norm2_w=norm2_w,
               ffn_w_gate=ffn_w_gate, ffn_w_up=ffn_w_up, ffn_conv_w=ffn_conv_w, ffn_conv_b=ffn_conv_b,
               ffn_w_down=ffn_w_down, final_norm_w=final_norm_w)
    mom = dict(ada_w=m_ada_w, ada_b=m_ada_b, norm1_w=m_norm1_w, w_in=m_w_in, attn_sinks=m_attn_sinks, sgu_ln_w=m_sgu_ln_w,
               sgu_ln_b=m_sgu_ln_b, sgu_w=m_sgu_w, sgu_b=m_sgu_b, proj_a=m_proj_a, proj_b=m_proj_b, w_out=m_w_out,
               norm2_w=m_norm2_w, ffn_w_gate=m_ffn_w_gate, ffn_w_up=m_ffn_w_up, ffn_conv_w=m_ffn_conv_w,
               ffn_conv_b=m_ffn_conv_b, ffn_w_down=m_ffn_w_down, final_norm_w=m_final_norm_w)
    var = dict(ada_w=v_ada_w, ada_b=v_ada_b, norm1_w=v_norm1_w, w_in=v_w_in, attn_sinks=v_attn_sinks, sgu_ln_w=v_sgu_ln_w,
               sgu_ln_b=v_sgu_ln_b, sgu_w=v_sgu_w, sgu_b=v_sgu_b, proj_a=v_proj_a, proj_b=v_proj_b, w_out=v_w_out,
               norm2_w=v_norm2_w, ffn_w_gate=v_ffn_w_gate, ffn_w_up=v_ffn_w_up, ffn_conv_w=v_ffn_conv_w,
               ffn_conv_b=v_ffn_conv_b, ffn_w_down=v_ffn_w_down, final_norm_w=v_final_norm_w)
    me = 4 * lax.axis_index("x") + 2 * lax.axis_index("y") + lax.axis_index("c")
    ada_cols = ada_w.shape[2]

    c_all = _all_gather(jnp.broadcast_to(c, (8, D_MODEL)), name="ag_c")[:, 0, :]
    prod = _ada_fwd(c_all, ada_w)
    prod_all = _all_gather(prod, name="ag_mod")
    mine = lax.dynamic_index_in_dim(prod_all, me, axis=1, keepdims=False)
    mod = jnp.stack([mine[:, l * ada_cols:(l + 1) * ada_cols].reshape(-1) for l in range(DEPTH)]) + ada_b

    conv_cols = ffn_conv_w.shape[2]
    conv_all = _all_gather(_flat_pack([ffn_conv_w], 8), name="ag_conv", after=mod)
    conv_full = jnp.stack([a.reshape(DEPTH, 3, conv_cols) for a in
                           [conv_all[j].reshape(-1)[:DEPTH * 3 * conv_cols] for j in range(N_DEV)]], axis=2)
    conv_full = conv_full.reshape(DEPTH, 3, FFN_DIM)
    small = {n: wts[n] for n in SMALL}
    small["ffn_conv_w"] = conv_full

    mx, my, mc = _coords()
    cidx = jnp.reshape(mc, (1,)).astype(jnp.int32)
    chipidx = jnp.reshape(2 * mx + my, (1,)).astype(jnp.int32)
    rope = _rope_setup(positions[0])

    class Gather:
        def __init__(self, names, l, tag):
            self.names, self.tag = names, tag
            self.src = _pack_shards(wts, l, names)
            self.land = lax.dynamic_update_slice(jnp.zeros((2, 4, _part_rows(names), 1024), BF), self.src[None, None],
                                                 (mc, 2 * mx + my, 0, 0))

        def ici_start(self, after):
            self.sems, (self.src, self.land), tok = _rdma_start([self.src, self.land], 3, _plan_gather_ici,
                                                                name=f"ag_{self.tag}_ici_start", after=after)
            return tok

        def ici_wait_d2d_start(self, after):
            _, land = _rdma_wait(self.sems, [self.src, self.land], 3, _plan_gather_ici, after, name=f"ag_{self.tag}_ici_wait")
            self.sems, (self.land,), tok = _rdma_start([land], 1, _plan_gather_d2d, name=f"ag_{self.tag}_d2d_start")
            return tok

        def d2d_wait(self, after):
            (land,) = _rdma_wait(self.sems, [self.land], 1, _plan_gather_d2d, after, name=f"ag_{self.tag}_d2d_wait")
            return _unpack_weights(_from_land(land), self.names)

    rest = PART_MIX + PART_FFN
    W0 = _unpack_weights(_all_gather(_pack_shards(wts, 0, PART_IN), name="ag_w0_in", after=conv_all), PART_IN)
    W1 = {}
    g_rest0 = Gather(rest, 0, "w0_rest")
    g_all1 = Gather(BIG, 1, "w1")

    def rest0_then_layer1(after):
        W0.update(g_rest0.d2d_wait(g_rest0.ici_wait_d2d_start(after)))
        return g_all1.ici_start(W0["proj_a"])

    x1, sv0 = _layer_fwd(0, x[0], mod[0], W0, small, rope,
                         {"mm_in": lambda after: g_rest0.ici_start(W0["wt_in"]), "mm_pa": rest0_then_layer1})
    g_all1.ici_wait_d2d_start(x1)
    x2, sv1 = _layer_fwd(1, x1, mod[1], W1, small, rope, {"mm_in": lambda after: W1.update(g_all1.d2d_wait(after))})
    dx2, dfw, loss_tile = _head(x2, final_norm_w[None, :], loss_target[0])
    loss = lax.psum(loss_tile[0, 0], ("x", "y", "c"))

    class Reduce:
        def __init__(self, names, tag):
            self.names, self.tag, self.rows = names, tag, _part_rows(names)

        def d2d_start(self, wg, after=None):
            self.sems, self.bufs, tok = _rdma_start([_pack_grads(wg, self.names), jnp.zeros((4, self.rows, 1024), BF)], 1,
                                                    _plan_reduce_d2d, name=f"rs_{self.tag}_d2d_start", after=after)
            return tok

        def d2d_wait_ici_start(self, after):
            g_t, land_a = _rdma_wait(self.sems, self.bufs, 1, _plan_reduce_d2d, after, name=f"rs_{self.tag}_d2d_wait")
            h = _sum_pair(g_t, land_a, cidx, name=f"rs_{self.tag}_sum_pair")
            self.sems, self.bufs, tok = _rdma_start([h, jnp.zeros((3, self.rows, 1024), BF)], 3, _plan_reduce_ici,
                                                    name=f"rs_{self.tag}_ici_start")
            return tok

        def ici_wait(self, after):
            h_t, land_b = _rdma_wait(self.sems, self.bufs, 3, _plan_reduce_ici, after, name=f"rs_{self.tag}_ici_wait")
            return _unpack_shard_grads(_sum_chips(h_t, land_b, chipidx, name=f"rs_{self.tag}_sum_chips"), self.names)

    dx1, wg1, sg1, dmod1 = _layer_bwd(1, dx2, mod[1], W1, small, rope, sv1)
    r_all1, r_ffn0, r_mix0 = Reduce(BIG, "g1"), Reduce(PART_FFN, "g0_ffn"), Reduce(PART_IN + PART_MIX, "g0_mix")
    tok1 = r_all1.d2d_start(wg1)
    wg0, shard1 = {}, {}

    def layer1_done_then_ffn0(after):
        shard1.update(r_all1.ici_wait(after))
        return r_ffn0.d2d_wait_ici_start(shard1["w_in"])

    grad_x, _, sg0, dmod0 = _layer_bwd(
        0, dx1, mod[0], W0, small, rope, sv0, wg=wg0,
        hooks={"scale2": lambda after: tok1, "mm_gu_dx": r_all1.d2d_wait_ici_start,
               "scale1": lambda after: r_ffn0.d2d_start(wg0, after), "mm_pa_dx": layer1_done_then_ffn0})
    tok = r_mix0.d2d_start(wg0, grad_x)
    shard0 = r_ffn0.ici_wait(tok)
    shard0.update(r_mix0.ici_wait(r_mix0.d2d_wait_ici_start(shard0["ffn_w_down"])))
    grads = {n: jnp.stack([shard0[n], shard1[n]]) for n in BIG}
    sg = {n: jnp.stack([sg0[n], sg1[n]]) for n in sg0}
    sg["final_norm_w"] = dfw[0]
    dmod = jnp.stack([dmod0, dmod1])

    small_names = [n for n in SMALL if n != "ada_b"] + ["ffn_conv_w"]
    small_shapes = [(DEPTH, 6 * D_MODEL)] + [sg[n].shape for n in small_names]
    n_small = sum(int(np.prod(s)) for s in small_shapes)
    rows = -(-n_small // 1024 // 8) * 8
    sm_all = _all_gather(_flat_pack([dmod] + [sg[n] for n in small_names], rows), name="ag_small", after=shard0["w_in"])
    sm_sum = _flat_unpack(_sum8(sm_all, name="sum_small"), small_shapes)
    grads["ada_b"] = sm_sum[0]
    for n, gsum in zip(small_names, sm_sum[1:]):
        grads[n] = gsum
    grads["ffn_conv_w"] = lax.dynamic_slice_in_dim(grads["ffn_conv_w"], me * conv_cols, conv_cols, axis=2)
    dmod_all = sm_all[:, :DEPTH * 6, :].reshape(N_DEV, DEPTH, 6 * D_MODEL)
    dm_mine = lax.dynamic_slice_in_dim(dmod_all, me * ada_cols, ada_cols, axis=2).transpose(1, 0, 2)
    dm_mine = jnp.pad(dm_mine, ((0, 0), (0, 8), (0, 0)))
    grads["ada_w"] = _ada_bwd(jnp.pad(c_all, ((0, 8), (0, 0))), dm_mine)

    packed_small = [n for n in SMALL]
    pshapes = [wts[n].shape for n in packed_small]
    prow = -(-sum(int(np.prod(s)) for s in pshapes) // 1024 // 8) * 8
    pk = lambda d: _flat_pack([d[n] for n in packed_small], prow)
    d_s, m_s, v_s = _adamw(pk(wts), pk(grads), pk(mom), pk(var), name="adamw_small")
    delta, new_m, new_v = {}, {}, {}
    for n, dd, mm, vv in zip(packed_small, _flat_unpack(d_s, pshapes), _flat_unpack(m_s, pshapes), _flat_unpack(v_s, pshapes)):
        delta[n], new_m[n], new_v[n] = dd, mm, vv
    for n in WEIGHT_ORDER:
        if n not in delta:
            delta[n], new_m[n], new_v[n] = _adam2d(wts[n], grads[n], mom[n], var[n], name=f"adamw_{n}")
    return (loss, grad_x[None], *[grads[n] for n in WEIGHT_ORDER], *[delta[n] for n in WEIGHT_ORDER],
            *[new_m[n] for n in WEIGHT_ORDER], *[new_v[n] for n in WEIGHT_ORDER])
```

```python
import functools

import jax
import jax.numpy as jnp
import numpy as np
from jax import lax
from jax.experimental import pallas as pl
from jax.experimental.pallas import tpu as pltpu

F32 = jnp.float32
BF = jnp.bfloat16

N_DEV = 8
D_MODEL = 1024
DEPTH = 2
N_Q_HEADS = 16
N_KV_HEADS = 2
HEAD_DIM = 64
Q_PER_KV = N_Q_HEADS // N_KV_HEADS
ATTN_BLOCK = 128
ROPE_THETA = 500000.0
ROT_DIM = HEAD_DIM // 4
SGU_WIDTH = 1024
SGU_GROUPS = 8
SGU_CHUNK = 128
FFN_DIM = 2816
NORM_EPS = 1e-6
Q_END = N_Q_HEADS * HEAD_DIM
K_END = Q_END + N_KV_HEADS * HEAD_DIM
V_END = K_END + N_KV_HEADS * HEAD_DIM
Z_END = V_END + 2 * SGU_WIDTH
IN_COLS = Z_END + 2 * D_MODEL
P_Z, P_G, P_Q, P_K, P_V = 0, 2048, 4096, 5120, 5248

ADAM_LR = 0.001
ADAM_B1 = 0.9
ADAM_B2 = 0.999
ADAM_EPS = 1e-08
ADAM_WD = 0.01
ADAM_STEP = 10

VMEM_LIMIT_BYTES = 56 * 1024 * 1024

BIG = ("w_in", "proj_a", "proj_b", "w_out", "ffn_w_gate", "ffn_w_up", "ffn_w_down")
COL_SHARDED = ("w_in", "ffn_w_gate", "ffn_w_up")
BIG_SHAPE = {"w_in": (D_MODEL, IN_COLS), "proj_a": (SGU_WIDTH, D_MODEL), "proj_b": (Q_END, D_MODEL),
             "w_out": (D_MODEL, D_MODEL), "ffn_w_gate": (D_MODEL, FFN_DIM), "ffn_w_up": (D_MODEL, FFN_DIM),
             "ffn_w_down": (FFN_DIM, D_MODEL)}
BIG_ROWS = {n: BIG_SHAPE[n][0] * BIG_SHAPE[n][1] // N_DEV // 1024 for n in BIG}
LAYER_ROWS = sum(BIG_ROWS.values())


def _pcall(body, **kw):
    return pl.pallas_call(body, **kw)


def _params(**kw):
    return pltpu.CompilerParams(vmem_limit_bytes=VMEM_LIMIT_BYTES, **kw)


def _tile(n, cap, unit=128):
    if n <= cap:
        return n
    best = 0
    t = unit
    while t <= cap:
        if n % t == 0:
            best = t
        t += unit
    assert best, (n, cap, unit)
    return best


def _mm(a, b, *, nt, out_dtype, name, res=None, gvec=None, after=None, tm=None, tn_cap=1024):
    a_list = list(a) if isinstance(a, (list, tuple)) else [a]
    b_list = list(b) if isinstance(b, (list, tuple)) else [b]
    a, b = a_list[0], b_list[0]
    M, K = a.shape
    N = b.shape[0] if nt else b.shape[1]
    k_total = sum(x.shape[1] for x in a_list)
    tm = _tile(M, tm or (1024 if k_total <= 1024 else 512), 8)
    tn = _tile(N, tn_cap)
    dn = (((1,), (1,)), ((), ())) if nt else (((1,), (0,)), ((), ()))

    def b_spec_of(x):
        k = x.shape[1] if nt else x.shape[0]
        return pl.BlockSpec((tn, k), lambda i, j: (j, 0)) if nt else pl.BlockSpec((k, tn), lambda i, j: (0, j))
    b_spec = b_spec_of(b)
    o_spec = pl.BlockSpec((tm, tn), lambda i, j: (i, j))
    if res is None:
        extra = [] if after is None else [after]
        n = len(a_list)

        def body(*refs):
            o_ref = refs[-1]
            acc = None
            for a_ref, b_ref in zip(refs[:n], refs[n:2 * n]):
                d = lax.dot_general(a_ref[...].astype(BF), b_ref[...].astype(BF), dn, preferred_element_type=F32)
                acc = d if acc is None else acc + d
            o_ref[...] = acc.astype(out_dtype)
        return _pcall(body, name=name, grid=(M // tm, N // tn),
                      in_specs=[pl.BlockSpec((tm, x.shape[1]), lambda i, j: (i, 0)) for x in a_list]
                      + [b_spec_of(x) for x in b_list] + [ANY] * len(extra), out_specs=o_spec,
                      out_shape=jax.ShapeDtypeStruct((M, N), out_dtype), compiler_params=_params())(
                          *a_list, *b_list, *extra)

    def body_res(a_ref, b_ref, r_ref, g_ref, o_ref, acc_ref):
        acc = lax.dot_general(a_ref[...].astype(BF), b_ref[...].astype(BF), dn, preferred_element_type=F32)
        acc_ref[...] = acc
        o_ref[...] = r_ref[...] + g_ref[...] * acc
    return _pcall(body_res, name=name, grid=(M // tm, N // tn),
                  in_specs=[pl.BlockSpec((tm, K), lambda i, j: (i, 0)), b_spec, o_spec,
                            pl.BlockSpec((1, tn), lambda i, j: (0, j))],
                  out_specs=[o_spec, o_spec],
                  out_shape=[jax.ShapeDtypeStruct((M, N), F32), jax.ShapeDtypeStruct((M, N), F32)],
                  compiler_params=_params())(a, b, res, gvec)


def _mm_tn(a, b, *, name, out_dtype=BF, tk=1024, tm_cap=1408, tn_cap=1024):
    S, M = a.shape
    N = b.shape[1]
    tk = _tile(S, tk, 8)
    tm = _tile(M, tm_cap)
    tn = _tile(N, tn_cap)
    nk = S // tk

    def body(a_ref, b_ref, o_ref, acc_ref):
        k = pl.program_id(2)

        @pl.when(k == 0)
        def _():
            acc_ref[...] = jnp.zeros_like(acc_ref)
        acc_ref[...] += lax.dot_general(a_ref[...].astype(BF), b_ref[...].astype(BF), (((0,), (0,)), ((), ())),
                                        preferred_element_type=F32)

        @pl.when(k == nk - 1)
        def _():
            o_ref[...] = acc_ref[...].astype(out_dtype)
    return _pcall(body, name=name, grid=(M // tm, N // tn, nk),
                  in_specs=[pl.BlockSpec((tk, tm), lambda i, j, k: (k, i)),
                            pl.BlockSpec((tk, tn), lambda i, j, k: (k, j))],
                  out_specs=pl.BlockSpec((tm, tn), lambda i, j, k: (i, j)),
                  out_shape=jax.ShapeDtypeStruct((M, N), out_dtype), scratch_shapes=[pltpu.VMEM((tm, tn), F32)],
                  compiler_params=_params())(a, b)


def _rms(x, w):
    return x * lax.rsqrt(jnp.mean(x * x, axis=-1, keepdims=True) + NORM_EPS) * w


def _normmod_fn(x, nw, sc, sh):
    return _rms(x, nw) * (1.0 + sc) + sh


def _gelu(x):
    return 0.5 * x * (1.0 + lax.erf(x * (2.0 ** -0.5)))


def _ln_gelu_fn(zv, w, b):
    v = _gelu(zv)
    mu = jnp.mean(v, axis=-1, keepdims=True)
    var = jnp.mean(jnp.square(v - mu), axis=-1, keepdims=True)
    return (v - mu) * lax.rsqrt(var + NORM_EPS) * w + b


def _sigmoid(x):
    return 1.0 / (1.0 + jnp.exp(-x))


def _row_spec(tm, n):
    return pl.BlockSpec((tm, n), lambda i: (i, 0))


def _vec_spec(n):
    return pl.BlockSpec((1, n), lambda i: (0, 0))


def _acc(ref, val):
    @pl.when(pl.program_id(0) == 0)
    def _():
        ref[...] = jnp.zeros_like(ref)
    ref[...] += val


def _normmod_fwd(x, nw, sc, sh, *, name, tm=512):
    S, Dm = x.shape
    tm = _tile(S, tm, 8)

    def body(x_ref, nw_ref, sc_ref, sh_ref, o_ref):
        o_ref[...] = _normmod_fn(x_ref[...], nw_ref[...], sc_ref[...], sh_ref[...]).astype(BF)
    return _pcall(body, name=name, grid=(S // tm,),
                  in_specs=[_row_spec(tm, Dm), _vec_spec(Dm), _vec_spec(Dm), _vec_spec(Dm)],
                  out_specs=_row_spec(tm, Dm), out_shape=jax.ShapeDtypeStruct((S, Dm), BF),
                  compiler_params=_params())(x, nw, sc, sh)


def _normmod_bwd(dh, x, nw, sc, sh, dres, *, name, tm=256):
    S, Dm = x.shape
    tm = _tile(S, tm, 8)

    def body(dh_ref, x_ref, nw_ref, sc_ref, sh_ref, dres_ref, dx_ref, dnw_ref, dsc_ref, dsh_ref):
        _, vjp = jax.vjp(_normmod_fn, x_ref[...], nw_ref[...], sc_ref[...], sh_ref[...])
        dx, dnw, dsc, dsh = vjp(dh_ref[...])
        dx_ref[...] = dres_ref[...] + dx
        _acc(dnw_ref, dnw)
        _acc(dsc_ref, dsc)
        _acc(dsh_ref, dsh)
    vec = jax.ShapeDtypeStruct((1, Dm), F32)
    return _pcall(body, name=name, grid=(S // tm,),
                  in_specs=[_row_spec(tm, Dm), _row_spec(tm, Dm), _vec_spec(Dm), _vec_spec(Dm), _vec_spec(Dm),
                            _row_spec(tm, Dm)],
                  out_specs=[_row_spec(tm, Dm), _vec_spec(Dm), _vec_spec(Dm), _vec_spec(Dm)],
                  out_shape=[jax.ShapeDtypeStruct((S, Dm), F32), vec, vec, vec],
                  compiler_params=_params())(dh, x, nw, sc, sh, dres)


def _scale_reduce(dx, o, g, *, name, after=None, tm=512):
    S, Dm = dx.shape
    tm = _tile(S, tm, 8)
    extra = [] if after is None else [after]

    def body(dx_ref, o_ref, g_ref, *rest):
        do_ref, dg_ref = rest[-2:]
        dxv = dx_ref[...]
        do_ref[...] = (dxv * g_ref[...]).astype(BF)
        _acc(dg_ref, jnp.sum(dxv * o_ref[...], axis=0, keepdims=True))
    return _pcall(body, name=name, grid=(S // tm,),
                  in_specs=[_row_spec(tm, Dm), _row_spec(tm, Dm), _vec_spec(Dm)] + [ANY] * len(extra),
                  out_specs=[_row_spec(tm, Dm), _vec_spec(Dm)],
                  out_shape=[jax.ShapeDtypeStruct((S, Dm), BF), jax.ShapeDtypeStruct((1, Dm), F32)],
                  compiler_params=_params())(dx, o, g, *extra)


def _head(x, fw, target, *, tm=256):
    S, Dm = x.shape
    tm = _tile(S, tm, 8)

    def body(x_ref, fw_ref, t_ref, dx_ref, dfw_ref, loss_ref):
        y, vjp = jax.vjp(_rms, x_ref[...], fw_ref[...])
        err = y - t_ref[...]
        dx, dfw = vjp(err * (1.0 / Dm))
        dx_ref[...] = dx
        _acc(dfw_ref, dfw)
        part = 0.5 * jnp.sum(jnp.mean(err * err, axis=-1, keepdims=True), axis=0, keepdims=True)
        _acc(loss_ref, jnp.broadcast_to(part, (8, 128)))
    return _pcall(body, name="head", grid=(S // tm,),
                  in_specs=[_row_spec(tm, Dm), _vec_spec(Dm), _row_spec(tm, Dm)],
                  out_specs=[_row_spec(tm, Dm), _vec_spec(Dm), pl.BlockSpec((8, 128), lambda i: (0, 0))],
                  out_shape=[jax.ShapeDtypeStruct((S, Dm), F32), jax.ShapeDtypeStruct((1, Dm), F32),
                             jax.ShapeDtypeStruct((8, 128), F32)],
                  compiler_params=_params())(x, fw, target)


def _tril_mask():
    r = lax.broadcasted_iota(jnp.int32, (SGU_CHUNK, SGU_CHUNK), 0)
    c = lax.broadcasted_iota(jnp.int32, (SGU_CHUNK, SGU_CHUNK), 1)
    return c <= r


def _sgu_fwd(proj, lnw, lnb, w, b_t, *, name, after=None, tm=256):
    S = proj.shape[0]
    tm = _tile(S, tm, SGU_CHUNK)
    extra = [] if after is None else [after]

    def body(zu_ref, zv_ref, lnw_ref, lnb_ref, w_ref, bt_ref, *rest):
        o_ref = rest[-1]
        u = _gelu(zu_ref[...].astype(F32))
        vn = _ln_gelu_fn(zv_ref[...].astype(F32), lnw_ref[...], lnb_ref[...]).astype(BF)
        mask = _tril_mask()
        for g in range(SGU_GROUPS):
            wm = jnp.where(mask, w_ref[g], 0.0).astype(BF)
            cols = slice(g * 128, (g + 1) * 128)
            for ci in range(tm // SGU_CHUNK):
                rows = slice(ci * SGU_CHUNK, (ci + 1) * SGU_CHUNK)
                f = jnp.dot(wm, vn[rows, cols], preferred_element_type=F32) + bt_ref[:, g:g + 1]
                o_ref[rows, cols] = (u[rows, cols] * f).astype(BF)
    return _pcall(body, name=name, grid=(S // tm,),
                  in_specs=[pl.BlockSpec((tm, SGU_WIDTH), lambda i: (i, 0)), pl.BlockSpec((tm, SGU_WIDTH), lambda i: (i, 1)),
                            _vec_spec(SGU_WIDTH), _vec_spec(SGU_WIDTH),
                            pl.BlockSpec((SGU_GROUPS, 128, 128), lambda i: (0, 0, 0)),
                            pl.BlockSpec((128, SGU_GROUPS), lambda i: (0, 0))] + [ANY] * len(extra),
                  out_specs=_row_spec(tm, SGU_WIDTH), out_shape=jax.ShapeDtypeStruct((S, SGU_WIDTH), BF),
                  compiler_params=_params())(proj, proj, lnw, lnb, w, b_t, *extra)


def _sgu_bwd(dy, proj, lnw, lnb, w, b_t, dproj, *, name, tm=256):
    S = proj.shape[0]
    tm = _tile(S, tm, SGU_CHUNK)

    def body(dy_ref, zu_ref, zv_ref, lnw_ref, lnb_ref, w_ref, bt_ref, _, dz_ref, dlnw_ref, dlnb_ref, dw_ref, dbt_ref,
             f_s, dvn_s):
        first = pl.program_id(0) == 0

        @pl.when(first)
        def _():
            dw_ref[...] = jnp.zeros_like(dw_ref)
            dbt_ref[...] = jnp.zeros_like(dbt_ref)
        u, vjp_u = jax.vjp(_gelu, zu_ref[...].astype(F32))
        vn, vjp_v = jax.vjp(_ln_gelu_fn, zv_ref[...].astype(F32), lnw_ref[...], lnb_ref[...])
        vn = vn.astype(BF)
        dy_v = dy_ref[...]
        df = (dy_v * u).astype(BF)
        mask = _tril_mask()
        for g in range(SGU_GROUPS):
            wm = jnp.where(mask, w_ref[g], 0.0).astype(BF)
            cols = slice(g * 128, (g + 1) * 128)
            dwg = jnp.zeros((128, 128), F32)
            dbg = jnp.zeros((128, 1), F32)
            for ci in range(tm // SGU_CHUNK):
                rows = slice(ci * SGU_CHUNK, (ci + 1) * SGU_CHUNK)
                vn_c = vn[rows, cols]
                df_c = df[rows, cols]
                f_s[rows, cols] = jnp.dot(wm, vn_c, preferred_element_type=F32) + bt_ref[:, g:g + 1]
                dvn_s[rows, cols] = lax.dot_general(wm, df_c, (((0,), (0,)), ((), ())), preferred_element_type=F32)
                dwg = dwg + lax.dot_general(df_c, vn_c, (((1,), (1,)), ((), ())), preferred_element_type=F32)
                dbg = dbg + jnp.sum((dy_v[rows, cols] * u[rows, cols]), axis=1, keepdims=True)
            dw_ref[g] += jnp.where(mask, dwg, 0.0)
            dbt_ref[:, g:g + 1] += dbg
        (dzu,) = vjp_u(dy_v * f_s[...])
        dzv, dlnw, dlnb = vjp_v(dvn_s[...])
        dz_ref[:, :SGU_WIDTH] = dzu.astype(BF)
        dz_ref[:, SGU_WIDTH:] = dzv.astype(BF)
        _acc(dlnw_ref, dlnw)
        _acc(dlnb_ref, dlnb)
    vec = jax.ShapeDtypeStruct((1, SGU_WIDTH), F32)
    return _pcall(body, name=name, grid=(S // tm,),
                  in_specs=[_row_spec(tm, SGU_WIDTH),
                            pl.BlockSpec((tm, SGU_WIDTH), lambda i: (i, 0)), pl.BlockSpec((tm, SGU_WIDTH), lambda i: (i, 1)),
                            _vec_spec(SGU_WIDTH), _vec_spec(SGU_WIDTH),
                            pl.BlockSpec((SGU_GROUPS, 128, 128), lambda i: (0, 0, 0)),
                            pl.BlockSpec((128, SGU_GROUPS), lambda i: (0, 0)), ANY],
                  out_specs=[pl.BlockSpec((tm, 2 * SGU_WIDTH), lambda i: (i, P_Z // (2 * SGU_WIDTH))),
                             _vec_spec(SGU_WIDTH), _vec_spec(SGU_WIDTH),
                             pl.BlockSpec((SGU_GROUPS, 128, 128), lambda i: (0, 0, 0)),
                             pl.BlockSpec((128, SGU_GROUPS), lambda i: (0, 0))],
                  out_shape=[jax.ShapeDtypeStruct(dproj.shape, BF), vec, vec,
                             jax.ShapeDtypeStruct((SGU_GROUPS, 128, 128), F32),
                             jax.ShapeDtypeStruct((128, SGU_GROUPS), F32)],
                  scratch_shapes=[pltpu.VMEM((tm, SGU_WIDTH), F32), pltpu.VMEM((tm, SGU_WIDTH), F32)],
                  input_output_aliases={7: 0},
                  compiler_params=_params())(dy, proj, proj, lnw, lnb, w, b_t, dproj)


def _merge_fwd(a, b, proj, *, name, tm=512):
    S, Dm = a.shape
    tm = _tile(S, tm, 8)
    ga_blk, gb_blk = P_G // Dm, P_G // Dm + 1

    def body(a_ref, b_ref, ga_ref, gb_ref, o_ref):
        o_ref[...] = (_sigmoid(ga_ref[...].astype(F32)) * a_ref[...].astype(F32)
                      + _sigmoid(gb_ref[...].astype(F32)) * b_ref[...].astype(F32)).astype(BF)
    return _pcall(body, name=name, grid=(S // tm,),
                  in_specs=[_row_spec(tm, Dm), _row_spec(tm, Dm), pl.BlockSpec((tm, Dm), lambda i: (i, ga_blk)),
                            pl.BlockSpec((tm, Dm), lambda i: (i, gb_blk))],
                  out_specs=_row_spec(tm, Dm), out_shape=jax.ShapeDtypeStruct((S, Dm), BF),
                  compiler_params=_params())(a, b, proj, proj)


def _merge_bwd(dm, a, b, proj, *, name, tm=512):
    S, Dm = a.shape
    tm = _tile(S, tm, 8)
    ga_blk, gb_blk = P_G // Dm, P_G // Dm + 1

    def body(dm_ref, a_ref, b_ref, ga_ref, gb_ref, da_ref, db_ref, dg_ref):
        dmv = dm_ref[...]
        sa = _sigmoid(ga_ref[...].astype(F32))
        sb = _sigmoid(gb_ref[...].astype(F32))
        da_ref[...] = (dmv * sa).astype(BF)
        db_ref[...] = (dmv * sb).astype(BF)
        dg_ref[:, :Dm] = (dmv * a_ref[...].astype(F32) * sa * (1.0 - sa)).astype(BF)
        dg_ref[:, Dm:] = (dmv * b_ref[...].astype(F32) * sb * (1.0 - sb)).astype(BF)
    return _pcall(body, name=name, grid=(S // tm,),
                  in_specs=[_row_spec(tm, Dm), _row_spec(tm, Dm), _row_spec(tm, Dm),
                            pl.BlockSpec((tm, Dm), lambda i: (i, ga_blk)), pl.BlockSpec((tm, Dm), lambda i: (i, gb_blk))],
                  out_specs=[_row_spec(tm, Dm), _row_spec(tm, Dm), pl.BlockSpec((tm, 2 * Dm), lambda i: (i, P_G // (2 * Dm)))],
                  out_shape=[jax.ShapeDtypeStruct((S, Dm), BF), jax.ShapeDtypeStruct((S, Dm), BF),
                             jax.ShapeDtypeStruct((S, IN_COLS), BF)],
                  compiler_params=_params())(dm, a, b, proj, proj)


def _shift_rows(a, halo, k, up):
    n = a.shape[0]
    r8 = lax.broadcasted_iota(jnp.int32, (8, a.shape[1]), 0)
    if not up:
        rolled = pltpu.roll(a, k, 0)
        patch = jnp.where(r8 < k, pltpu.roll(halo, k, 0), rolled[:8])
        return jnp.concatenate([patch, rolled[8:]], axis=0)
    rolled = pltpu.roll(a, n - k, 0)
    patch = jnp.where(r8 >= 8 - k, pltpu.roll(halo, 8 - k, 0), rolled[n - 8:])
    return jnp.concatenate([rolled[:n - 8], patch], axis=0)


def _conv_taps(a, halo):
    return _shift_rows(a, halo, 2, False), _shift_rows(a, halo, 1, False), a


HALO = 16


def _prev_halo_spec(tm, Fd):
    return pl.BlockSpec((HALO, Fd), lambda i: (jnp.maximum(i * (tm // HALO) - 1, 0), 0))


def _conv_fwd(a_ref, halo_ref, cw_ref, cb_ref):
    halo = jnp.where(pl.program_id(0) > 0, halo_ref[...].astype(F32)[HALO - 8:], 0.0)
    t0, t1, t2 = _conv_taps(a_ref[...].astype(F32), halo)
    return t0, t1, t2, cb_ref[...] + cw_ref[0:1, :] * t0 + cw_ref[1:2, :] * t1 + cw_ref[2:3, :] * t2


def _ffn_act_fwd(a, up, cw, cb, *, name, tm=256):
    S, Fd = a.shape
    tm = _tile(S, tm, HALO)

    def body(a_ref, up_ref, halo_ref, cw_ref, cb_ref, o_ref):
        _, _, _, ac = _conv_fwd(a_ref, halo_ref, cw_ref, cb_ref)
        o_ref[...] = (ac * _sigmoid(ac) * up_ref[...].astype(F32)).astype(BF)
    return _pcall(body, name=name, grid=(S // tm,),
                  in_specs=[_row_spec(tm, Fd), _row_spec(tm, Fd), _prev_halo_spec(tm, Fd),
                            pl.BlockSpec((3, Fd), lambda i: (0, 0)), _vec_spec(Fd)],
                  out_specs=_row_spec(tm, Fd), out_shape=jax.ShapeDtypeStruct((S, Fd), BF),
                  compiler_params=_params())(a, up, a, cw, cb)


def _ffn_act_bwd_a(dhf, a, up, cw, cb, *, name, tm=256):
    S, Fd = a.shape
    tm = _tile(S, tm, HALO)

    def body(dhf_ref, a_ref, up_ref, halo_ref, cw_ref, cb_ref, dac_ref, dup_ref, dcw_ref, dcb_ref):
        t0, t1, t2, ac = _conv_fwd(a_ref, halo_ref, cw_ref, cb_ref)
        s = _sigmoid(ac)
        dhf_v = dhf_ref[...].astype(F32)
        dup_ref[...] = (dhf_v * ac * s).astype(BF)
        dac = dhf_v * up_ref[...].astype(F32) * (s * (1.0 + ac * (1.0 - s)))
        dac_ref[...] = dac.astype(BF)
        _acc(dcb_ref, jnp.sum(dac, axis=0, keepdims=True))
        _acc(dcw_ref, jnp.concatenate([jnp.sum(dac * t0, axis=0, keepdims=True),
                                       jnp.sum(dac * t1, axis=0, keepdims=True),
                                       jnp.sum(dac * t2, axis=0, keepdims=True)], axis=0))
    return _pcall(body, name=name, grid=(S // tm,),
                  in_specs=[_row_spec(tm, Fd), _row_spec(tm, Fd), _row_spec(tm, Fd), _prev_halo_spec(tm, Fd),
                            pl.BlockSpec((3, Fd), lambda i: (0, 0)), _vec_spec(Fd)],
                  out_specs=[_row_spec(tm, Fd), _row_spec(tm, Fd), pl.BlockSpec((3, Fd), lambda i: (0, 0)), _vec_spec(Fd)],
                  out_shape=[jax.ShapeDtypeStruct((S, Fd), BF), jax.ShapeDtypeStruct((S, Fd), BF),
                             jax.ShapeDtypeStruct((3, Fd), F32), jax.ShapeDtypeStruct((1, Fd), F32)],
                  compiler_params=_params())(dhf, a, up, a, cw, cb)


def _ffn_act_bwd_b(dac, cw, *, name, tm=256):
    S, Fd = dac.shape
    tm = _tile(S, tm, HALO)
    last = S // tm - 1

    def body(d_ref, halo_ref, cw_ref, o_ref):
        halo = jnp.where(pl.program_id(0) < last, halo_ref[...].astype(F32)[:8], 0.0)
        d = d_ref[...].astype(F32)
        o_ref[...] = (cw_ref[2:3, :] * d + cw_ref[1:2, :] * _shift_rows(d, halo, 1, True)
                      + cw_ref[0:1, :] * _shift_rows(d, halo, 2, True)).astype(BF)
    return _pcall(body, name=name, grid=(S // tm,),
                  in_specs=[_row_spec(tm, Fd),
                            pl.BlockSpec((HALO, Fd), lambda i: (jnp.minimum((i + 1) * (tm // HALO), S // HALO - 1), 0)),
                            pl.BlockSpec((3, Fd), lambda i: (0, 0))],
                  out_specs=_row_spec(tm, Fd), out_shape=jax.ShapeDtypeStruct((S, Fd), BF),
                  compiler_params=_params())(dac, dac, cw)


def _rope_tables(pos_col, inv_row, m1_row, m2_row):
    S = pos_col.shape[0]
    tm = _tile(S, 512, 8)

    def body(p_ref, inv_ref, m1_ref, m2_ref, c_ref, s1_ref, s2_ref):
        ang = p_ref[...] * inv_ref[...]
        sn = jnp.sin(ang)
        c_ref[...] = jnp.cos(ang)
        s1_ref[...] = -sn * m1_ref[...]
        s2_ref[...] = sn * m2_ref[...]
    sh = jax.ShapeDtypeStruct((S, 128), F32)
    return _pcall(body, name="rope_tables", grid=(S // tm,),
                  in_specs=[pl.BlockSpec((tm, 1), lambda i: (i, 0)), _vec_spec(128), _vec_spec(128), _vec_spec(128)],
                  out_specs=[_row_spec(tm, 128)] * 3, out_shape=[sh, sh, sh], compiler_params=_params())(
                      pos_col, inv_row, m1_row, m2_row)


def _rope_apply(x, c, s1, s2):
    outs = []
    for j in range(x.shape[1] // 128):
        xj = x[:, j * 128:(j + 1) * 128]
        outs.append(xj * c + pltpu.roll(xj, 120, 1) * s1 + pltpu.roll(xj, 8, 1) * s2)
    return outs[0] if len(outs) == 1 else jnp.concatenate(outs, axis=1)


def _rope_apply_t(d, c, s1, s2):
    outs = []
    for j in range(d.shape[1] // 128):
        dj = d[:, j * 128:(j + 1) * 128]
        outs.append(dj * c + pltpu.roll(dj * s1, 8, 1) + pltpu.roll(dj * s2, 120, 1))
    return outs[0] if len(outs) == 1 else jnp.concatenate(outs, axis=1)


def _rope_fwd(proj, c, s1, s2, *, name, tm=512):
    S = proj.shape[0]
    tm = _tile(S, tm, 8)

    def body(q_ref, k_ref, v_ref, c_ref, s1_ref, s2_ref, qo_ref, ko_ref, vo_ref):
        cv, s1v, s2v = c_ref[...], s1_ref[...], s2_ref[...]
        qo_ref[...] = (_rope_apply(q_ref[...].astype(F32), cv, s1v, s2v) * (HEAD_DIM ** -0.5)).astype(BF)
        ko_ref[...] = _rope_apply(k_ref[...].astype(F32), cv, s1v, s2v).astype(BF)
        vo_ref[...] = v_ref[...].astype(BF)
    return _pcall(body, name=name, grid=(S // tm,),
                  in_specs=[pl.BlockSpec((tm, Q_END), lambda i: (i, P_Q // Q_END)),
                            pl.BlockSpec((tm, 128), lambda i: (i, P_K // 128)),
                            pl.BlockSpec((tm, 128), lambda i: (i, P_V // 128)),
                            _row_spec(tm, 128), _row_spec(tm, 128), _row_spec(tm, 128)],
                  out_specs=[_row_spec(tm, Q_END), _row_spec(tm, 128), _row_spec(tm, 128)],
                  out_shape=[jax.ShapeDtypeStruct((S, Q_END), BF), jax.ShapeDtypeStruct((S, 128), BF),
                             jax.ShapeDtypeStruct((S, 128), BF)],
                  compiler_params=_params())(proj, proj, proj, c, s1, s2)


def _rope_bwd(dq, dk, dv, c, s1, s2, dproj, *, name, tm=512):
    S = dq.shape[0]
    tm = _tile(S, tm, 8)
    tabs = [_row_spec(tm, 128)] * 3
    shape = jax.ShapeDtypeStruct(dproj.shape, BF)

    def body_q(dq_ref, c_ref, s1_ref, s2_ref, _, o_ref):
        o_ref[...] = _rope_apply_t(dq_ref[...].astype(F32), c_ref[...], s1_ref[...], s2_ref[...]).astype(BF)
    dproj = _pcall(body_q, name=name + "_q", grid=(S // tm,), in_specs=[_row_spec(tm, Q_END)] + tabs + [ANY],
                   out_specs=pl.BlockSpec((tm, Q_END), lambda i: (i, P_Q // Q_END)), out_shape=shape,
                   input_output_aliases={4: 0}, compiler_params=_params())(dq, c, s1, s2, dproj)

    def body_kv(dk_ref, dv_ref, c_ref, s1_ref, s2_ref, _, o_ref):
        o_ref[:, :128] = _rope_apply_t(dk_ref[...], c_ref[...], s1_ref[...], s2_ref[...]).astype(BF)
        o_ref[:, 128:] = dv_ref[...].astype(BF)
    return _pcall(body_kv, name=name + "_kv", grid=(S // tm,),
                  in_specs=[_row_spec(tm, 128), _row_spec(tm, 128)] + tabs + [ANY],
                  out_specs=pl.BlockSpec((tm, 256), lambda i: (i, P_K // 256)), out_shape=shape,
                  input_output_aliases={5: 0}, compiler_params=_params())(dk, dv, c, s1, s2, dproj)


def _lane_lo(shape):
    return lax.broadcasted_iota(jnp.int32, shape, 1) < HEAD_DIM


def _stack_heads(x, g):
    lo = _lane_lo((ATTN_BLOCK, 128))
    zero = jnp.zeros((ATTN_BLOCK, 128), x.dtype)
    parts = []
    for p in range(Q_PER_KV // 2):
        xp = x[:, (g * 4 + p) * 128:(g * 4 + p + 1) * 128]
        parts += [jnp.where(lo, xp, zero), jnp.where(lo, zero, xp)]
    return jnp.concatenate(parts, axis=0)


def _unstack_heads(o2):
    lo = _lane_lo((ATTN_BLOCK, 128))
    return [jnp.where(lo, o2[2 * p * ATTN_BLOCK:(2 * p + 1) * ATTN_BLOCK], o2[(2 * p + 1) * ATTN_BLOCK:(2 * p + 2) * ATTN_BLOCK])
            for p in range(Q_PER_KV // 2)]


def _dup_half(prev, cur, g):
    x = jnp.concatenate([prev, cur], axis=0).astype(F32)
    lo = _lane_lo(x.shape)
    r = pltpu.roll(x, HEAD_DIM, 1)
    return (jnp.where(lo, x, r) if g == 0 else jnp.where(lo, r, x)).astype(BF)


def _fold_halves(x):
    return x + pltpu.roll(x, HEAD_DIM, 1)


def _attn_bias():
    i = lax.broadcasted_iota(jnp.int32, (Q_PER_KV * ATTN_BLOCK, 2 * ATTN_BLOCK), 0) & (ATTN_BLOCK - 1)
    j = lax.broadcasted_iota(jnp.int32, (Q_PER_KV * ATTN_BLOCK, 2 * ATTN_BLOCK), 1)
    band = (j > i) & (j <= i + ATTN_BLOCK)
    return jnp.stack([jnp.where(band & (j >= ATTN_BLOCK), 0.0, -jnp.inf), jnp.where(band, 0.0, -jnp.inf)]).astype(F32)


def _attn_probs(qs, kb, sink, bias):
    s = lax.dot_general(qs, kb, (((1,), (1,)), ((), ())), preferred_element_type=F32) + bias
    m = jnp.maximum(jnp.max(s, axis=-1, keepdims=True), sink)
    p = jnp.exp(s - m)
    es = jnp.exp(sink - m)
    inv = 1.0 / (jnp.sum(p, axis=-1, keepdims=True) + es)
    return p, inv, es * inv


def _attn_specs(S):
    nb = S // ATTN_BLOCK
    qs = pl.BlockSpec((ATTN_BLOCK, Q_END), lambda n: (n, 0))
    cur = pl.BlockSpec((ATTN_BLOCK, 128), lambda n: (n, 0))
    prev = pl.BlockSpec((ATTN_BLOCK, 128), lambda n: (jnp.maximum(n - 1, 0), 0))
    sink = pl.BlockSpec((N_KV_HEADS, Q_PER_KV * ATTN_BLOCK, 1), lambda n: (0, 0, 0))
    bias = pl.BlockSpec((None, Q_PER_KV * ATTN_BLOCK, 2 * ATTN_BLOCK), lambda n: (jnp.minimum(n, 1), 0, 0))
    return nb, qs, cur, prev, sink, bias


def _attn_fwd(q, k, v, sink_rows, bias, *, name):
    S = q.shape[0]
    nb, qs, cur, prev, sink, bs = _attn_specs(S)

    def body(q_ref, kp_ref, kc_ref, vp_ref, vc_ref, sk_ref, b_ref, o_ref):
        for g in range(N_KV_HEADS):
            kb = _dup_half(kp_ref[...], kc_ref[...], g)
            vb = _dup_half(vp_ref[...], vc_ref[...], g)
            p, inv, _ = _attn_probs(_stack_heads(q_ref[...], g), kb, sk_ref[g], b_ref[...])
            o2 = jnp.dot(p.astype(BF), vb, preferred_element_type=F32) * inv
            for t, tile in enumerate(_unstack_heads(o2)):
                o_ref[:, (g * 4 + t) * 128:(g * 4 + t + 1) * 128] = tile.astype(BF)
    return _pcall(body, name=name, grid=(nb,), in_specs=[qs, prev, cur, prev, cur, sink, bs], out_specs=qs,
                  out_shape=jax.ShapeDtypeStruct(q.shape, BF), compiler_params=_params())(q, k, k, v, v, sink_rows, bias)


def _attn_bwd(do, q, k, v, sink_rows, bias, *, name):
    S = q.shape[0]
    nb, qs, cur, prev, sink, bs = _attn_specs(S)
    full = pl.BlockSpec((S, 128), lambda n: (0, 0))
    dsk_spec = pl.BlockSpec((N_KV_HEADS, Q_PER_KV, 128), lambda n: (0, 0, 0))

    def body(do_ref, q_ref, kp_ref, kc_ref, vp_ref, vc_ref, sk_ref, b_ref, dq_ref, dk_ref, dv_ref, dsk_ref):
        n = pl.program_id(0)

        @pl.when(n == 0)
        def _():
            dk_ref[...] = jnp.zeros_like(dk_ref)
            dv_ref[...] = jnp.zeros_like(dv_ref)
            dsk_ref[...] = jnp.zeros_like(dsk_ref)
        sub = lax.broadcasted_iota(jnp.int32, (Q_PER_KV, 128), 0)
        dkf, dvf = [], []
        for g in range(N_KV_HEADS):
            qst = _stack_heads(q_ref[...], g)
            dos = _stack_heads(do_ref[...], g)
            kb = _dup_half(kp_ref[...], kc_ref[...], g)
            vb = _dup_half(vp_ref[...], vc_ref[...], g)
            pu, inv, ps = _attn_probs(qst, kb, sk_ref[g], b_ref[...])
            p = pu * inv
            dp = lax.dot_general(dos, vb, (((1,), (1,)), ((), ())), preferred_element_type=F32)
            dd = jnp.sum(p * dp, axis=-1, keepdims=True)
            ds = (p * (dp - dd)).astype(BF)
            dq2 = jnp.dot(ds, kb, preferred_element_type=F32) * (HEAD_DIM ** -0.5)
            for t, tile in enumerate(_unstack_heads(dq2)):
                dq_ref[:, (g * 4 + t) * 128:(g * 4 + t + 1) * 128] = tile.astype(BF)
            dkf.append(_fold_halves(lax.dot_general(ds, qst, (((0,), (0,)), ((), ())), preferred_element_type=F32)))
            dvf.append(_fold_halves(lax.dot_general(p.astype(BF), dos, (((0,), (0,)), ((), ())),
                                                    preferred_element_type=F32)))
            dsr = -(ps * dd)
            upd = jnp.zeros((Q_PER_KV, 128), F32)
            for h in range(Q_PER_KV):
                upd = jnp.where(sub == h, jnp.sum(dsr[h * ATTN_BLOCK:(h + 1) * ATTN_BLOCK]), upd)
            dsk_ref[g] += upd
        lo = _lane_lo((2 * ATTN_BLOCK, 128))
        dkb = jnp.where(lo, dkf[0], dkf[1])
        dvb = jnp.where(lo, dvf[0], dvf[1])
        r0 = pl.multiple_of(n * ATTN_BLOCK, ATTN_BLOCK)
        dk_ref[pl.ds(r0, ATTN_BLOCK), :] += dkb[ATTN_BLOCK:]
        dv_ref[pl.ds(r0, ATTN_BLOCK), :] += dvb[ATTN_BLOCK:]

        @pl.when(n > 0)
        def _():
            rp = pl.multiple_of((n - 1) * ATTN_BLOCK, ATTN_BLOCK)
            dk_ref[pl.ds(rp, ATTN_BLOCK), :] += dkb[:ATTN_BLOCK]
            dv_ref[pl.ds(rp, ATTN_BLOCK), :] += dvb[:ATTN_BLOCK]
    return _pcall(body, name=name, grid=(nb,), in_specs=[qs, qs, prev, cur, prev, cur, sink, bs],
                  out_specs=[qs, full, full, dsk_spec],
                  out_shape=[jax.ShapeDtypeStruct(q.shape, BF), jax.ShapeDtypeStruct((S, 128), F32),
                             jax.ShapeDtypeStruct((S, 128), F32), jax.ShapeDtypeStruct((N_KV_HEADS, Q_PER_KV, 128), F32)],
                  compiler_params=_params())(do, q, k, k, v, v, sink_rows, bias)


def _ada_fwd(c_all, ada_w):
    ncol = ada_w.shape[2]

    def body(c_ref, w_ref, o_ref):
        cv = c_ref[...]
        ca = (cv * _sigmoid(cv)).astype(BF)
        for l in range(DEPTH):
            o_ref[:, l * ncol:(l + 1) * ncol] = jnp.dot(ca, w_ref[l].astype(BF), preferred_element_type=F32)
    return _pcall(body, name="ada_fwd", out_shape=jax.ShapeDtypeStruct((N_DEV, DEPTH * ncol), F32),
                  compiler_params=_params())(c_all, ada_w)


def _ada_bwd(c_all, dm):
    ncol = dm.shape[2]

    def body(c_ref, dm_ref, o_ref):
        cv = c_ref[...]
        ca = (cv * _sigmoid(cv)).astype(BF)
        for l in range(DEPTH):
            o_ref[l] = lax.dot_general(ca, dm_ref[l].astype(BF), (((0,), (0,)), ((), ())), preferred_element_type=F32)
    return _pcall(body, name="ada_bwd", out_shape=jax.ShapeDtypeStruct((DEPTH, D_MODEL, ncol), F32),
                  compiler_params=_params())(c_all, dm)


def _adamw(w, g, m, v, *, name):
    R, C = w.shape
    tr = R
    for t in range(8, 513, 8):
        if R % t == 0:
            tr = t
    c1 = 1.0 - ADAM_B1 ** ADAM_STEP
    c2 = 1.0 - ADAM_B2 ** ADAM_STEP

    def body(w_ref, g_ref, m_ref, v_ref, d_ref, mo_ref, vo_ref):
        gv = g_ref[...]
        mn = ADAM_B1 * m_ref[...] + (1.0 - ADAM_B1) * gv
        vn = ADAM_B2 * v_ref[...] + (1.0 - ADAM_B2) * (gv * gv)
        mo_ref[...] = mn
        vo_ref[...] = vn
        d_ref[...] = -ADAM_LR * ((mn / c1) / (jnp.sqrt(vn / c2) + ADAM_EPS) + ADAM_WD * w_ref[...])
    spec = pl.BlockSpec((tr, C), lambda i: (i, 0))
    sh = jax.ShapeDtypeStruct((R, C), F32)
    return _pcall(body, name=name, grid=(R // tr,), in_specs=[spec] * 4, out_specs=[spec] * 3, out_shape=[sh, sh, sh],
                  compiler_params=_params())(w, g, m, v)


def _sum8(parts, *, name):
    _, R, C = parts.shape
    tr = _tile(R, 512, 16)

    def body(p_ref, o_ref):
        acc = p_ref[0].astype(F32)
        for k in range(1, N_DEV):
            acc = acc + p_ref[k].astype(F32)
        o_ref[...] = acc
    return _pcall(body, name=name, grid=(R // tr,), in_specs=[pl.BlockSpec((N_DEV, tr, C), lambda i: (0, i, 0))],
                  out_specs=pl.BlockSpec((tr, C), lambda i: (i, 0)), out_shape=jax.ShapeDtypeStruct((R, C), F32),
                  compiler_params=_params())(parts)


MESH_ID = pl.DeviceIdType.MESH
ANY = pl.BlockSpec(memory_space=pl.ANY)


def _all_gather(x, *, name, after=None):
    R, C = x.shape
    extra = [] if after is None else [after]

    def body(x_ref, *rest):
        out_ref, send_sems, recv_sems, local_sem = rest[-4:]
        mx, my, mc = lax.axis_index("x"), lax.axis_index("y"), lax.axis_index("c")
        me, sibling = (mx, my, mc), (mx, my, 1 - mc)
        chips = [(1 - mx, my), (mx, 1 - my), (1 - mx, 1 - my)]

        def blk(px, py, pc):
            return out_ref.at[4 * px + 2 * py + pc]

        def copy(k, block, to, src=None):
            return pltpu.make_async_remote_copy(
                src_ref=blk(*block) if src is None else src, dst_ref=blk(*block),
                send_sem=send_sems.at[k], recv_sem=recv_sems.at[k], device_id=to, device_id_type=MESH_ID)

        mine = pltpu.make_async_copy(x_ref, blk(*me), local_sem)
        mine.start()
        first = [copy(0, me, sibling, src=x_ref)]
        first += [copy(1 + j, me, (*chip, mc), src=x_ref) for j, chip in enumerate(chips)]
        for cp in first:
            cp.start()
        passed = [copy(4 + j, (*chip, mc), sibling) for j, chip in enumerate(chips)]
        for j, chip in enumerate(chips):
            copy(1 + j, (*chip, mc), me).wait_recv()
            passed[j].start()
        copy(0, sibling, me).wait_recv()
        for j, chip in enumerate(chips):
            copy(4 + j, (*chip, 1 - mc), me).wait_recv()
        for cp in first + passed:
            cp.wait_send()
        mine.wait()
    return _pcall(body, name=name, in_specs=[ANY] * (1 + len(extra)), out_specs=ANY,
                  out_shape=jax.ShapeDtypeStruct((N_DEV, R, C), x.dtype),
                  scratch_shapes=[pltpu.SemaphoreType.DMA((7,)), pltpu.SemaphoreType.DMA((7,)), pltpu.SemaphoreType.DMA],
                  compiler_params=pltpu.CompilerParams(has_side_effects=True))(x, *extra)


HBM_SPEC = pl.BlockSpec(memory_space=pltpu.HBM)
SEM_SPEC = pl.BlockSpec(memory_space=pltpu.SEMAPHORE)
DATAFLOW = pltpu.SideEffectType.DATAFLOW_SIDE_EFFECTING


def _coords():
    return lax.axis_index("x"), lax.axis_index("y"), lax.axis_index("c")


def _other_chips(mx, my):
    return [(1 - mx, my), (mx, 1 - my), (1 - mx, 1 - my)]


def _plan_gather_ici(refs, send, recv):
    src, land = refs
    mx, my, mc = _coords()
    return [pltpu.make_async_remote_copy(src_ref=src, dst_ref=land.at[mc, 2 * mx + my], send_sem=send[j], recv_sem=recv[j],
                                         device_id=(px, py, mc), device_id_type=MESH_ID)
            for j, (px, py) in enumerate(_other_chips(mx, my))]


def _plan_gather_d2d(refs, send, recv):
    (land,) = refs
    mx, my, mc = _coords()
    return [pltpu.make_async_remote_copy(src_ref=land.at[mc], dst_ref=land.at[mc], send_sem=send[0], recv_sem=recv[0],
                                         device_id=(mx, my, 1 - mc), device_id_type=MESH_ID)]


def _plan_reduce_d2d(refs, send, recv):
    g, land = refs
    mx, my, mc = _coords()
    return [pltpu.make_async_remote_copy(src_ref=g.at[1 - mc], dst_ref=land, send_sem=send[0], recv_sem=recv[0],
                                         device_id=(mx, my, 1 - mc), device_id_type=MESH_ID)]


def _plan_reduce_ici(refs, send, recv):
    h, land = refs
    mx, my, mc = _coords()
    return [pltpu.make_async_remote_copy(src_ref=h.at[2 * px + py], dst_ref=land.at[j], send_sem=send[j], recv_sem=recv[j],
                                         device_id=(px, py, mc), device_id_type=MESH_ID)
            for j, (px, py) in enumerate(_other_chips(mx, my))]


def _rdma_start(bufs, n, plan, *, name, after=None):
    nb = len(bufs)
    extra = [] if after is None else [after]
    ne = len(extra)

    def body(*refs):
        ins, send, recv = refs[:nb], refs[nb + ne:nb + ne + n], refs[nb + ne + n:nb + ne + 2 * n]
        token = refs[-1]
        for cp in plan(ins, send, recv):
            cp.start()
        token[...] = jnp.zeros_like(token)
    out = _pcall(body, name=name,
                 out_shape=tuple([pltpu.SemaphoreType.DMA(())] * (2 * n) + [pltpu.HBM(b.shape, b.dtype) for b in bufs]
                                 + [jax.ShapeDtypeStruct((8, 128), F32)]),
                 in_specs=tuple([HBM_SPEC] * nb + [ANY] * ne),
                 out_specs=tuple([SEM_SPEC] * (2 * n) + [HBM_SPEC] * nb + [pl.BlockSpec(memory_space=pltpu.VMEM)]),
                 input_output_aliases={i: 2 * n + i for i in range(nb)},
                 compiler_params=pltpu.CompilerParams(has_side_effects=DATAFLOW))(
                     *[pltpu.with_memory_space_constraint(b, pltpu.HBM) for b in bufs], *extra)
    return list(out[:2 * n]), list(out[2 * n:2 * n + nb]), out[-1]


def _rdma_wait(sems, bufs, n, plan, after, *, name):
    nb = len(bufs)

    def body(*refs):
        ins, send, recv = refs[:nb], refs[nb:nb + n], refs[nb + n:nb + 2 * n]
        for cp in plan(ins, send, recv):
            cp.wait_send()
            cp.wait_recv()
    out = _pcall(body, name=name, out_shape=tuple(pltpu.HBM(b.shape, b.dtype) for b in bufs),
                 in_specs=tuple([HBM_SPEC] * nb + [SEM_SPEC] * (2 * n) + [ANY]), out_specs=tuple([HBM_SPEC] * nb),
                 input_output_aliases={i: i for i in range(nb)},
                 compiler_params=pltpu.CompilerParams(has_side_effects=DATAFLOW))(*bufs, *sems, after)
    return list(out)


def _sum_pair(g, land, cidx, *, name):
    _, nchip, R, C = g.shape
    tr = _tile(R, 1056, 16)

    def body(c_ref, g_ref, l_ref, o_ref):
        o_ref[...] = (g_ref[...].astype(F32) + l_ref[...].astype(F32)).astype(BF)
    grid_spec = pltpu.PrefetchScalarGridSpec(
        num_scalar_prefetch=1, grid=(nchip, R // tr),
        in_specs=[pl.BlockSpec((None, None, tr, C), lambda p, i, c_ref: (c_ref[0], p, i, 0)),
                  pl.BlockSpec((None, tr, C), lambda p, i, c_ref: (p, i, 0))],
        out_specs=pl.BlockSpec((None, tr, C), lambda p, i, c_ref: (p, i, 0)))
    return _pcall(body, name=name, grid_spec=grid_spec, out_shape=jax.ShapeDtypeStruct((nchip, R, C), BF),
                  compiler_params=_params())(cidx, g, land)


def _sum_chips(h, land, chipidx, *, name):
    _, R, C = h.shape
    tr = _tile(R, 1056, 16)

    def body(c_ref, h_ref, l_ref, o_ref):
        acc = h_ref[...].astype(F32)
        for j in range(3):
            acc = acc + l_ref[j].astype(F32)
        o_ref[...] = acc
    grid_spec = pltpu.PrefetchScalarGridSpec(
        num_scalar_prefetch=1, grid=(R // tr,),
        in_specs=[pl.BlockSpec((None, tr, C), lambda i, c_ref: (c_ref[0], i, 0)),
                  pl.BlockSpec((3, tr, C), lambda i, c_ref: (0, i, 0))],
        out_specs=pl.BlockSpec((tr, C), lambda i, c_ref: (i, 0)))
    return _pcall(body, name=name, grid_spec=grid_spec, out_shape=jax.ShapeDtypeStruct((R, C), F32),
                  compiler_params=_params())(chipidx, h, land)


PART_IN = ("w_in",)
PART_MIX = ("proj_a", "proj_b", "w_out")
PART_FFN = ("ffn_w_gate", "ffn_w_up", "ffn_w_down")


def _part_rows(names):
    return sum(BIG_ROWS[n] for n in names)


def _part_offsets(names):
    off, r = {}, 0
    for n in names:
        off[n] = r
        r += BIG_ROWS[n]
    return off


def _pack_shards(shards, l, names):
    return jnp.concatenate([(shards[n][l].T if n in COL_SHARDED else shards[n][l]).astype(BF) for n in names], axis=0)


def _unpack_weights(full8, names):
    off = _part_offsets(names)

    def whole(n):
        return full8[:, off[n]:off[n] + BIG_ROWS[n], :].reshape(N_DEV * BIG_ROWS[n], 1024)
    out = {}
    if "w_in" in names:
        wt_in = whole("w_in")
        out["wt_in"] = jnp.concatenate([wt_in[V_END:], wt_in[:V_END]], axis=0)
    for n in ("proj_a", "proj_b", "w_out"):
        if n in names:
            out[n] = whole(n)
    if "ffn_w_gate" in names:
        out["wt_gate"], out["wt_up"], out["w_down"] = whole("ffn_w_gate"), whole("ffn_w_up"), whole("ffn_w_down")
    return out


def _from_land(land):
    return land.transpose(1, 0, 2, 3).reshape(N_DEV, land.shape[2], 1024)


def _pack_grads(wg, names):
    full = {"proj_a": wg.get("proj_a"), "proj_b": wg.get("proj_b"), "w_out": wg.get("w_out"), "ffn_w_down": wg.get("w_down"),
            "ffn_w_gate": wg.get("wt_gate"), "ffn_w_up": wg.get("wt_up")}
    if "w_in" in names:
        full["w_in"] = jnp.concatenate([wg["wt_in"][P_Q:], wg["wt_in"][:P_Q]], axis=0)
    blocks = jnp.concatenate([full[n].reshape(N_DEV, BIG_ROWS[n], 1024) for n in names], axis=1)
    return blocks.reshape(4, 2, _part_rows(names), 1024).transpose(1, 0, 2, 3)


def _unpack_shard_grads(gs, names):
    off = _part_offsets(names)
    out = {}
    for n in names:
        blk = gs[off[n]:off[n] + BIG_ROWS[n]]
        out[n] = blk.T if n in COL_SHARDED else blk
    return out


def _rope_setup(positions):
    S = positions.shape[0]
    inv = ROPE_THETA ** (-jnp.arange(0, ROT_DIM, 2, dtype=F32) / ROT_DIM)
    lane = np.arange(128) % HEAD_DIM
    half = ROT_DIM // 2
    inv_row = jnp.where(lane < ROT_DIM, jnp.tile(inv, 128 // half), 0.0)[None, :].astype(F32)
    m1_row = jnp.asarray((lane < half).astype(np.float32))[None, :]
    m2_row = jnp.asarray(((lane >= half) & (lane < ROT_DIM)).astype(np.float32))[None, :]
    return (*_rope_tables(positions.astype(F32).reshape(S, 1), inv_row, m1_row, m2_row), _attn_bias())


def _hook(hooks, point, after):
    f = None if hooks is None else hooks.get(point)
    return None if f is None else f(after)


def _layer_fwd(l, x, mod_l, W, small, rope, hooks=None):
    rc, rs1, rs2, bias = rope
    sh1, sc1, g1, sh2, sc2, g2 = [mod_l[i * D_MODEL:(i + 1) * D_MODEL][None, :] for i in range(6)]
    nw1, nw2 = small["norm1_w"][l][None, :], small["norm2_w"][l][None, :]
    h = _normmod_fwd(x, nw1, sc1, sh1, name=f"normmod1_fwd{l}")
    tok = _hook(hooks, "mm_in", h)
    proj = _mm(h, W["wt_in"], nt=True, out_dtype=BF, name=f"mm_in{l}", after=tok, tn_cap=768)
    q_r, k_r, v_b = _rope_fwd(proj, rc, rs1, rs2, name=f"rope_fwd{l}")
    sink_rows = jnp.repeat(small["attn_sinks"][l].reshape(N_KV_HEADS, Q_PER_KV), ATTN_BLOCK, axis=1)[..., None]
    y_attn = _attn_fwd(q_r, k_r, v_b, sink_rows, bias, name=f"attn_fwd{l}")
    lnw, lnb = small["sgu_ln_w"][l][None, :], small["sgu_ln_b"][l][None, :]
    sgu_bt = small["sgu_b"][l].T
    y_sgu = _sgu_fwd(proj, lnw, lnb, small["sgu_w"][l], sgu_bt, name=f"sgu_fwd{l}", after=_hook(hooks, "sgu", y_attn))
    tok = _hook(hooks, "mm_pa", y_sgu)
    a_br = _mm(y_sgu, W["proj_a"], nt=False, out_dtype=BF, name=f"mm_pa{l}", after=tok)
    b_br = _mm(y_attn, W["proj_b"], nt=False, out_dtype=BF, name=f"mm_pb{l}")
    merged = _merge_fwd(a_br, b_br, proj, name=f"merge_fwd{l}")
    x1, o1 = _mm(merged, W["w_out"], nt=False, out_dtype=F32, name=f"mm_out{l}", res=x, gvec=g1)
    h2 = _normmod_fwd(x1, nw2, sc2, sh2, name=f"normmod2_fwd{l}")
    a_g = _mm(h2, W["wt_gate"], nt=True, out_dtype=BF, name=f"mm_gate{l}", tn_cap=1408)
    a_u = _mm(h2, W["wt_up"], nt=True, out_dtype=BF, name=f"mm_up{l}", tn_cap=1408)
    cw, cb = small["ffn_conv_w"][l], small["ffn_conv_b"][l][None, :]
    hf = _ffn_act_fwd(a_g, a_u, cw, cb, name=f"ffn_act_fwd{l}")
    x2, o2 = _mm(hf, W["w_down"], nt=False, out_dtype=F32, name=f"mm_down{l}", res=x1, gvec=g2)
    saved = dict(x=x, h=h, proj=proj, q_r=q_r, k_r=k_r, v_b=v_b, sink_rows=sink_rows, y_attn=y_attn, y_sgu=y_sgu,
                 a_br=a_br, b_br=b_br, merged=merged, x1=x1, o1=o1, h2=h2, a_g=a_g, a_u=a_u, hf=hf, o2=o2)
    return x2, saved


def _layer_bwd(l, dx, mod_l, W, small, rope, sv, hooks=None, wg=None):
    rc, rs1, rs2, bias = rope
    sh1, sc1, g1, sh2, sc2, g2 = [mod_l[i * D_MODEL:(i + 1) * D_MODEL][None, :] for i in range(6)]
    nw1, nw2 = small["norm1_w"][l][None, :], small["norm2_w"][l][None, :]
    cw, cb = small["ffn_conv_w"][l], small["ffn_conv_b"][l][None, :]
    lnw, lnb = small["sgu_ln_w"][l][None, :], small["sgu_ln_b"][l][None, :]
    sgu_bt = small["sgu_b"][l].T
    wg = {} if wg is None else wg
    do2, dg2 = _scale_reduce(dx, sv["o2"], g2, name=f"scale2_{l}", after=_hook(hooks, "scale2", dx))
    dhf = _mm(do2, W["w_down"], nt=True, out_dtype=BF, name=f"mm_down_dx{l}", tn_cap=1408)
    wg["w_down"] = _mm_tn(sv["hf"], do2, name=f"mm_down_dw{l}")
    dac, dup, dcw, dcb = _ffn_act_bwd_a(dhf, sv["a_g"], sv["a_u"], cw, cb, name=f"ffn_act_bwd_a{l}")
    da = _ffn_act_bwd_b(dac, cw, name=f"ffn_act_bwd_b{l}")
    dh2 = _mm([da, dup], [W["wt_gate"], W["wt_up"]], nt=False, out_dtype=F32, name=f"mm_gu_dx{l}",
              after=_hook(hooks, "mm_gu_dx", da))
    wg["wt_gate"] = _mm_tn(da, sv["h2"], name=f"mm_gate_dw{l}")
    wg["wt_up"] = _mm_tn(dup, sv["h2"], name=f"mm_up_dw{l}")
    dx1, dnw2, dsc2, dsh2 = _normmod_bwd(dh2, sv["x1"], nw2, sc2, sh2, dx, name=f"normmod2_bwd{l}")
    do1, dg1 = _scale_reduce(dx1, sv["o1"], g1, name=f"scale1_{l}", after=_hook(hooks, "scale1", dx1))
    dmerged = _mm(do1, W["w_out"], nt=True, out_dtype=F32, name=f"mm_out_dx{l}")
    wg["w_out"] = _mm_tn(sv["merged"], do1, name=f"mm_out_dw{l}")
    d_a, d_b, dproj = _merge_bwd(dmerged, sv["a_br"], sv["b_br"], sv["proj"], name=f"merge_bwd{l}")
    dysgu = _mm(d_a, W["proj_a"], nt=True, out_dtype=F32, name=f"mm_pa_dx{l}", after=_hook(hooks, "mm_pa_dx", d_a))
    dyattn = _mm(d_b, W["proj_b"], nt=True, out_dtype=BF, name=f"mm_pb_dx{l}")
    wg["proj_a"] = _mm_tn(sv["y_sgu"], d_a, name=f"mm_pa_dw{l}")
    wg["proj_b"] = _mm_tn(sv["y_attn"], d_b, name=f"mm_pb_dw{l}")
    dproj, dlnw, dlnb, dsguw, dsgubt = _sgu_bwd(dysgu, sv["proj"], lnw, lnb, small["sgu_w"][l], sgu_bt, dproj,
                                                name=f"sgu_bwd{l}")
    dq_r, dk_r, dv_b, dsk = _attn_bwd(dyattn, sv["q_r"], sv["k_r"], sv["v_b"], sv["sink_rows"], bias, name=f"attn_bwd{l}")
    dproj = _rope_bwd(dq_r, dk_r, dv_b, rc, rs1, rs2, dproj, name=f"rope_bwd{l}")
    dh = _mm(dproj, W["wt_in"], nt=False, out_dtype=F32, name=f"mm_in_dx{l}", after=_hook(hooks, "mm_in_dx", dproj))
    wg["wt_in"] = _mm_tn(dproj, sv["h"], name=f"mm_in_dw{l}")
    dx0, dnw1, dsc1, dsh1 = _normmod_bwd(dh, sv["x"], nw1, sc1, sh1, dx1, name=f"normmod1_bwd{l}")
    dmod = jnp.concatenate([dsh1, dsc1, dg1, dsh2, dsc2, dg2], axis=1)[0]
    sg = {"norm1_w": dnw1[0], "norm2_w": dnw2[0], "attn_sinks": dsk[:, :, 0].reshape(N_Q_HEADS),
          "sgu_ln_w": dlnw[0], "sgu_ln_b": dlnb[0], "sgu_w": dsguw, "sgu_b": dsgubt.T,
          "ffn_conv_w": dcw, "ffn_conv_b": dcb[0]}
    return dx0, wg, sg, dmod


SMALL = ("ada_b", "norm1_w", "attn_sinks", "sgu_ln_w", "sgu_ln_b", "sgu_w", "sgu_b", "norm2_w", "ffn_conv_b", "final_norm_w")
WEIGHT_ORDER = ("ada_w", "ada_b", "norm1_w", "w_in", "attn_sinks", "sgu_ln_w", "sgu_ln_b", "sgu_w", "sgu_b", "proj_a", "proj_b",
                "w_out", "norm2_w", "ffn_w_gate", "ffn_w_up", "ffn_conv_w", "ffn_conv_b", "ffn_w_down", "final_norm_w")


def _flat_pack(arrs, rows):
    flat = jnp.concatenate([a.reshape(-1) for a in arrs])
    return jnp.pad(flat, (0, rows * 1024 - flat.shape[0])).reshape(rows, 1024)


def _flat_unpack(buf, shapes):
    flat = buf.reshape(-1)
    out, o = [], 0
    for s in shapes:
        n = int(np.prod(s))
        out.append(flat[o:o + n].reshape(s))
        o += n
    return out


def _adam2d(w, g, m, v, *, name):
    shp = w.shape
    r2 = (int(np.prod(shp[:-1])), shp[-1]) if len(shp) > 1 else (1, shp[0])
    d, mn, vn = _adamw(w.reshape(r2), g.reshape(r2), m.reshape(r2), v.reshape(r2), name=name)
    return d.reshape(shp), mn.reshape(shp), vn.reshape(shp)


def kernel(x, c, positions, ada_w, ada_b, norm1_w, w_in, attn_sinks, sgu_ln_w, sgu_ln_b, sgu_w, sgu_b, proj_a, proj_b, w_out, norm2_w, ffn_w_gate, ffn_w_up, ffn_conv_w, ffn_conv_b, ffn_w_down, final_norm_w, loss_target, m_ada_w, m_ada_b, m_norm1_w, m_w_in, m_attn_sinks, m_sgu_ln_w, m_sgu_ln_b, m_sgu_w, m_sgu_b, m_proj_a, m_proj_b, m_w_out, m_norm2_w, m_ffn_w_gate, m_ffn_w_up, m_ffn_conv_w, m_ffn_conv_b, m_ffn_w_down, m_final_norm_w, v_ada_w, v_ada_b, v_norm1_w, v_w_in, v_attn_sinks, v_sgu_ln_w, v_sgu_ln_b, v_sgu_w, v_sgu_b, v_proj_a, v_proj_b, v_w_out, v_norm2_w, v_ffn_w_gate, v_ffn_w_up, v_ffn_conv_w, v_ffn_conv_b, v_ffn_w_down, v_final_norm_w):
    wts = dict(ada_w=ada_w, ada_b=ada_b, norm1_w=norm1_w, w_in=w_in, attn_sinks=attn_sinks, sgu_ln_w=sgu_ln_w,
               sgu_ln_b=sgu_ln_b, sgu_w=sgu_w, sgu_b=sgu_b, proj_a=proj_a, proj_b=proj_b, w_out=w_out, norm2_w=norm2_w,
               ffn_w_gate=ffn_w_gate, ffn_w_up=ffn_w_up, ffn_conv_w=ffn_conv_w, ffn_conv_b=ffn_conv_b,
               ffn_w_down=ffn_w_down, final_norm_w=final_norm_w)
    mom = dict(ada_w=m_ada_w, ada_b=m_ada_b, norm1_w=m_norm1_w, w_in=m_w_in, attn_sinks=m_attn_sinks, sgu_ln_w=m_sgu_ln_w,
               sgu_ln_b=m_sgu_ln_b, sgu_w=m_sgu_w, sgu_b=m_sgu_b, proj_a=m_proj_a, proj_b=m_proj_b, w_out=m_w_out,
               norm2_w=m_norm2_w, ffn_w_gate=m_ffn_w_gate, ffn_w_up=m_ffn_w_up, ffn_conv_w=m_ffn_conv_w,
               ffn_conv_b=m_ffn_conv_b, ffn_w_down=m_ffn_w_down, final_norm_w=m_final_norm_w)
    var = dict(ada_w=v_ada_w, ada_b=v_ada_b, norm1_w=v_norm1_w, w_in=v_w_in, attn_sinks=v_attn_sinks, sgu_ln_w=v_sgu_ln_w,
               sgu_ln_b=v_sgu_ln_b, sgu_w=v_sgu_w, sgu_b=v_sgu_b, proj_a=v_proj_a, proj_b=v_proj_b, w_out=v_w_out,
               norm2_w=v_norm2_w, ffn_w_gate=v_ffn_w_gate, ffn_w_up=v_ffn_w_up, ffn_conv_w=v_ffn_conv_w,
               ffn_conv_b=v_ffn_conv_b, ffn_w_down=v_ffn_w_down, final_norm_w=v_final_norm_w)
    me = 4 * lax.axis_index("x") + 2 * lax.axis_index("y") + lax.axis_index("c")
    ada_cols = ada_w.shape[2]

    c_all = _all_gather(jnp.broadcast_to(c, (8, D_MODEL)), name="ag_c")[:, 0, :]
    prod = _ada_fwd(c_all, ada_w)
    prod_all = _all_gather(prod, name="ag_mod")
    mine = lax.dynamic_index_in_dim(prod_all, me, axis=1, keepdims=False)
    mod = jnp.stack([mine[:, l * ada_cols:(l + 1) * ada_cols].reshape(-1) for l in range(DEPTH)]) + ada_b

    conv_cols = ffn_conv_w.shape[2]
    conv_all = _all_gather(_flat_pack([ffn_conv_w], 8), name="ag_conv", after=mod)
    conv_full = jnp.stack([a.reshape(DEPTH, 3, conv_cols) for a in
                           [conv_all[j].reshape(-1)[:DEPTH * 3 * conv_cols] for j in range(N_DEV)]], axis=2)
    conv_full = conv_full.reshape(DEPTH, 3, FFN_DIM)
    small = {n: wts[n] for n in SMALL}
    small["ffn_conv_w"] = conv_full

    mx, my, mc = _coords()
    cidx = jnp.reshape(mc, (1,)).astype(jnp.int32)
    chipidx = jnp.reshape(2 * mx + my, (1,)).astype(jnp.int32)
    rope = _rope_setup(positions[0])

    class Gather:
        def __init__(self, names, l, tag):
            self.names, self.tag = names, tag
            self.src = _pack_shards(wts, l, names)
            self.land = lax.dynamic_update_slice(lax.empty((2, 4, _part_rows(names), 1024), BF), self.src[None, None],
                                                 (mc, 2 * mx + my, 0, 0))

        def ici_start(self, after):
            self.sems, (self.src, self.land), tok = _rdma_start([self.src, self.land], 3, _plan_gather_ici,
                                                                name=f"ag_{self.tag}_ici_start", after=after)
            return tok

        def ici_wait_d2d_start(self, after):
            _, land = _rdma_wait(self.sems, [self.src, self.land], 3, _plan_gather_ici, after, name=f"ag_{self.tag}_ici_wait")
            self.sems, (self.land,), tok = _rdma_start([land], 1, _plan_gather_d2d, name=f"ag_{self.tag}_d2d_start")
            return tok

        def d2d_wait(self, after):
            (land,) = _rdma_wait(self.sems, [self.land], 1, _plan_gather_d2d, after, name=f"ag_{self.tag}_d2d_wait")
            return _unpack_weights(_from_land(land), self.names)

    W0 = _unpack_weights(_all_gather(_pack_shards(wts, 0, PART_IN), name="ag_w0_in", after=conv_all), PART_IN)
    W1 = {}
    g_mix0, g_ffn0 = Gather(PART_MIX, 0, "w0_mix"), Gather(PART_FFN, 0, "w0_ffn")
    g_all1 = Gather(BIG, 1, "w1")
    toks = {}

    def rest0_to_sibling(after):
        toks["mix"] = g_mix0.ici_wait_d2d_start(after)
        toks["ffn"] = g_ffn0.ici_wait_d2d_start(toks["mix"])
        return toks["ffn"]

    def rest0_then_layer1(after):
        W0.update(g_mix0.d2d_wait(after))
        W0.update(g_ffn0.d2d_wait(W0["proj_a"]))
        return g_all1.ici_start(W0["w_down"])

    x1, sv0 = _layer_fwd(0, x[0], mod[0], W0, small, rope,
                         {"mm_in": lambda after: g_ffn0.ici_start(g_mix0.ici_start(W0["wt_in"])),
                          "sgu": rest0_to_sibling, "mm_pa": rest0_then_layer1})
    g_all1.ici_wait_d2d_start(x1)
    x2, sv1 = _layer_fwd(1, x1, mod[1], W1, small, rope, {"mm_in": lambda after: W1.update(g_all1.d2d_wait(after))})
    dx2, dfw, loss_tile = _head(x2, final_norm_w[None, :], loss_target[0])
    loss = lax.psum(loss_tile[0, 0], ("x", "y", "c"))

    class Reduce:
        def __init__(self, names, tag):
            self.names, self.tag, self.rows = names, tag, _part_rows(names)

        def d2d_start(self, wg, after=None):
            self.sems, self.bufs, tok = _rdma_start([_pack_grads(wg, self.names), lax.empty((4, self.rows, 1024), BF)], 1,
                                                    _plan_reduce_d2d, name=f"rs_{self.tag}_d2d_start", after=after)
            return tok

        def d2d_wait_ici_start(self, after):
            g_t, land_a = _rdma_wait(self.sems, self.bufs, 1, _plan_reduce_d2d, after, name=f"rs_{self.tag}_d2d_wait")
            h = _sum_pair(g_t, land_a, cidx, name=f"rs_{self.tag}_sum_pair")
            self.sems, self.bufs, tok = _rdma_start([h, lax.empty((3, self.rows, 1024), BF)], 3, _plan_reduce_ici,
                                                    name=f"rs_{self.tag}_ici_start")
            return tok

        def ici_wait(self, after):
            h_t, land_b = _rdma_wait(self.sems, self.bufs, 3, _plan_reduce_ici, after, name=f"rs_{self.tag}_ici_wait")
            return _unpack_shard_grads(_sum_chips(h_t, land_b, chipidx, name=f"rs_{self.tag}_sum_chips"), self.names)

    dx1, wg1, sg1, dmod1 = _layer_bwd(1, dx2, mod[1], W1, small, rope, sv1)
    r_all1, r_ffn0, r_mix0, r_in0 = Reduce(BIG, "g1"), Reduce(PART_FFN, "g0_ffn"), Reduce(PART_MIX, "g0_mix"), Reduce(PART_IN, "g0_in")
    tok1 = r_all1.d2d_start(wg1)
    wg0, shard1 = {}, {}

    def layer1_done_then_ffn0(after):
        shard1.update(r_all1.ici_wait(after))
        return r_ffn0.d2d_wait_ici_start(shard1["w_in"])

    grad_x, _, sg0, dmod0 = _layer_bwd(
        0, dx1, mod[0], W0, small, rope, sv0, wg=wg0,
        hooks={"scale2": lambda after: tok1, "mm_gu_dx": r_all1.d2d_wait_ici_start,
               "scale1": lambda after: r_ffn0.d2d_start(wg0, after), "mm_pa_dx": layer1_done_then_ffn0,
               "mm_in_dx": lambda after: r_mix0.d2d_wait_ici_start(r_mix0.d2d_start(wg0, after))})
    tok = r_in0.d2d_start(wg0, grad_x)
    shard0 = r_ffn0.ici_wait(tok)
    shard0.update(r_mix0.ici_wait(shard0["ffn_w_down"]))
    shard0.update(r_in0.ici_wait(r_in0.d2d_wait_ici_start(shard0["w_out"])))
    grads = {n: jnp.stack([shard0[n], shard1[n]]) for n in BIG}
    sg = {n: jnp.stack([sg0[n], sg1[n]]) for n in sg0}
    sg["final_norm_w"] = dfw[0]
    dmod = jnp.stack([dmod0, dmod1])

    small_names = [n for n in SMALL if n != "ada_b"] + ["ffn_conv_w"]
    small_shapes = [(DEPTH, 6 * D_MODEL)] + [sg[n].shape for n in small_names]
    n_small = sum(int(np.prod(s)) for s in small_shapes)
    rows = -(-n_small // 1024 // 16) * 16
    sm_all = _all_gather(_flat_pack([dmod] + [sg[n] for n in small_names], rows).astype(BF), name="ag_small",
                         after=shard0["w_in"])
    sm_sum = _flat_unpack(_sum8(sm_all, name="sum_small"), small_shapes)
    sm_all = sm_all.astype(F32)
    grads["ada_b"] = sm_sum[0]
    for n, gsum in zip(small_names, sm_sum[1:]):
        grads[n] = gsum
    grads["ffn_conv_w"] = lax.dynamic_slice_in_dim(grads["ffn_conv_w"], me * conv_cols, conv_cols, axis=2)
    dmod_all = sm_all[:, :DEPTH * 6, :].reshape(N_DEV, DEPTH, 6 * D_MODEL)
    dm_mine = lax.dynamic_slice_in_dim(dmod_all, me * ada_cols, ada_cols, axis=2).transpose(1, 0, 2)
    dm_mine = jnp.pad(dm_mine, ((0, 0), (0, 8), (0, 0)))
    grads["ada_w"] = _ada_bwd(jnp.pad(c_all, ((0, 8), (0, 0))), dm_mine)

    packed_small = [n for n in SMALL]
    pshapes = [wts[n].shape for n in packed_small]
    prow = -(-sum(int(np.prod(s)) for s in pshapes) // 1024 // 8) * 8
    pk = lambda d: _flat_pack([d[n] for n in packed_small], prow)
    d_s, m_s, v_s = _adamw(pk(wts), pk(grads), pk(mom), pk(var), name="adamw_small")
    delta, new_m, new_v = {}, {}, {}
    for n, dd, mm, vv in zip(packed_small, _flat_unpack(d_s, pshapes), _flat_unpack(m_s, pshapes), _flat_unpack(v_s, pshapes)):
        delta[n], new_m[n], new_v[n] = dd, mm, vv
    for n in WEIGHT_ORDER:
        if n not in delta:
            delta[n], new_m[n], new_v[n] = _adam2d(wts[n], grads[n], mom[n], var[n], name=f"adamw_{n}")
    return (loss, grad_x[None], *[grads[n] for n in WEIGHT_ORDER], *[delta[n] for n in WEIGHT_ORDER],
            *[new_m[n] for n in WEIGHT_ORDER], *[new_v[n] for n in WEIGHT_ORDER])
```

```python
import functools

import jax
import jax.numpy as jnp
import numpy as np
from jax import lax
from jax.experimental import pallas as pl
from jax.experimental.pallas import tpu as pltpu

F32 = jnp.float32
BF = jnp.bfloat16

N_DEV = 8
D_MODEL = 1024
DEPTH = 2
N_Q_HEADS = 16
N_KV_HEADS = 2
HEAD_DIM = 64
Q_PER_KV = N_Q_HEADS // N_KV_HEADS
ATTN_BLOCK = 128
ROPE_THETA = 500000.0
ROT_DIM = HEAD_DIM // 4
SGU_WIDTH = 1024
SGU_GROUPS = 8
SGU_CHUNK = 128
FFN_DIM = 2816
NORM_EPS = 1e-6
Q_END = N_Q_HEADS * HEAD_DIM
K_END = Q_END + N_KV_HEADS * HEAD_DIM
V_END = K_END + N_KV_HEADS * HEAD_DIM
Z_END = V_END + 2 * SGU_WIDTH
IN_COLS = Z_END + 2 * D_MODEL
P_Z, P_G, P_Q, P_K, P_V = 0, 2048, 4096, 5120, 5248

ADAM_LR = 0.001
ADAM_B1 = 0.9
ADAM_B2 = 0.999
ADAM_EPS = 1e-08
ADAM_WD = 0.01
ADAM_STEP = 10

VMEM_LIMIT_BYTES = 56 * 1024 * 1024

BIG = ("w_in", "proj_a", "proj_b", "w_out", "ffn_w_gate", "ffn_w_up", "ffn_w_down")
COL_SHARDED = ("w_in", "ffn_w_gate", "ffn_w_up")
BIG_SHAPE = {"w_in": (D_MODEL, IN_COLS), "proj_a": (SGU_WIDTH, D_MODEL), "proj_b": (Q_END, D_MODEL),
             "w_out": (D_MODEL, D_MODEL), "ffn_w_gate": (D_MODEL, FFN_DIM), "ffn_w_up": (D_MODEL, FFN_DIM),
             "ffn_w_down": (FFN_DIM, D_MODEL)}
BIG_ROWS = {n: BIG_SHAPE[n][0] * BIG_SHAPE[n][1] // N_DEV // 1024 for n in BIG}
LAYER_ROWS = sum(BIG_ROWS.values())


def _pcall(body, **kw):
    return pl.pallas_call(body, **kw)


def _params(**kw):
    return pltpu.CompilerParams(vmem_limit_bytes=VMEM_LIMIT_BYTES, **kw)


def _tile(n, cap, unit=128):
    if n <= cap:
        return n
    best = 0
    t = unit
    while t <= cap:
        if n % t == 0:
            best = t
        t += unit
    assert best, (n, cap, unit)
    return best


def _mm(a, b, *, nt, out_dtype, name, res=None, gvec=None, after=None, tm=None, tn_cap=1024):
    a_list = list(a) if isinstance(a, (list, tuple)) else [a]
    b_list = list(b) if isinstance(b, (list, tuple)) else [b]
    a, b = a_list[0], b_list[0]
    M, K = a.shape
    N = b.shape[0] if nt else b.shape[1]
    k_total = sum(x.shape[1] for x in a_list)
    tm = _tile(M, tm or (1024 if k_total <= 1024 else 512), 8)
    tn = _tile(N, tn_cap)
    dn = (((1,), (1,)), ((), ())) if nt else (((1,), (0,)), ((), ()))

    def b_spec_of(x):
        k = x.shape[1] if nt else x.shape[0]
        return pl.BlockSpec((tn, k), lambda i, j: (j, 0)) if nt else pl.BlockSpec((k, tn), lambda i, j: (0, j))
    b_spec = b_spec_of(b)
    o_spec = pl.BlockSpec((tm, tn), lambda i, j: (i, j))
    if res is None:
        extra = [] if after is None else [after]
        n = len(a_list)

        def body(*refs):
            o_ref = refs[-1]
            acc = None
            for a_ref, b_ref in zip(refs[:n], refs[n:2 * n]):
                d = lax.dot_general(a_ref[...].astype(BF), b_ref[...].astype(BF), dn, preferred_element_type=F32)
                acc = d if acc is None else acc + d
            o_ref[...] = acc.astype(out_dtype)
        return _pcall(body, name=name, grid=(M // tm, N // tn),
                      in_specs=[pl.BlockSpec((tm, x.shape[1]), lambda i, j: (i, 0)) for x in a_list]
                      + [b_spec_of(x) for x in b_list] + [ANY] * len(extra), out_specs=o_spec,
                      out_shape=jax.ShapeDtypeStruct((M, N), out_dtype), compiler_params=_params())(
                          *a_list, *b_list, *extra)

    def body_res(a_ref, b_ref, r_ref, g_ref, o_ref, acc_ref):
        acc = lax.dot_general(a_ref[...].astype(BF), b_ref[...].astype(BF), dn, preferred_element_type=F32)
        acc_ref[...] = acc
        o_ref[...] = r_ref[...] + g_ref[...] * acc
    return _pcall(body_res, name=name, grid=(M // tm, N // tn),
                  in_specs=[pl.BlockSpec((tm, K), lambda i, j: (i, 0)), b_spec, o_spec,
                            pl.BlockSpec((1, tn), lambda i, j: (0, j))],
                  out_specs=[o_spec, o_spec],
                  out_shape=[jax.ShapeDtypeStruct((M, N), F32), jax.ShapeDtypeStruct((M, N), F32)],
                  compiler_params=_params())(a, b, res, gvec)


def _mm_tn(a, b, *, name, out_dtype=BF, tk=1024, tm_cap=1408, tn_cap=1024):
    S, M = a.shape
    N = b.shape[1]
    tk = _tile(S, tk, 8)
    tm = _tile(M, tm_cap)
    tn = _tile(N, tn_cap)
    nk = S // tk

    def body(a_ref, b_ref, o_ref, acc_ref):
        k = pl.program_id(2)

        @pl.when(k == 0)
        def _():
            acc_ref[...] = jnp.zeros_like(acc_ref)
        acc_ref[...] += lax.dot_general(a_ref[...].astype(BF), b_ref[...].astype(BF), (((0,), (0,)), ((), ())),
                                        preferred_element_type=F32)

        @pl.when(k == nk - 1)
        def _():
            o_ref[...] = acc_ref[...].astype(out_dtype)
    return _pcall(body, name=name, grid=(M // tm, N // tn, nk),
                  in_specs=[pl.BlockSpec((tk, tm), lambda i, j, k: (k, i)),
                            pl.BlockSpec((tk, tn), lambda i, j, k: (k, j))],
                  out_specs=pl.BlockSpec((tm, tn), lambda i, j, k: (i, j)),
                  out_shape=jax.ShapeDtypeStruct((M, N), out_dtype), scratch_shapes=[pltpu.VMEM((tm, tn), F32)],
                  compiler_params=_params())(a, b)


def _rms(x, w):
    return x * lax.rsqrt(jnp.mean(x * x, axis=-1, keepdims=True) + NORM_EPS) * w


def _normmod_fn(x, nw, sc, sh):
    return _rms(x, nw) * (1.0 + sc) + sh


def _gelu(x):
    return 0.5 * x * (1.0 + lax.erf(x * (2.0 ** -0.5)))


def _ln_gelu_fn(zv, w, b):
    v = _gelu(zv)
    mu = jnp.mean(v, axis=-1, keepdims=True)
    var = jnp.mean(jnp.square(v - mu), axis=-1, keepdims=True)
    return (v - mu) * lax.rsqrt(var + NORM_EPS) * w + b


def _sigmoid(x):
    return 1.0 / (1.0 + jnp.exp(-x))


def _row_spec(tm, n):
    return pl.BlockSpec((tm, n), lambda i: (i, 0))


def _vec_spec(n):
    return pl.BlockSpec((1, n), lambda i: (0, 0))


def _acc(ref, val):
    @pl.when(pl.program_id(0) == 0)
    def _():
        ref[...] = jnp.zeros_like(ref)
    ref[...] += val


def _normmod_fwd(x, nw, sc, sh, *, name, tm=512):
    S, Dm = x.shape
    tm = _tile(S, tm, 8)

    def body(x_ref, nw_ref, sc_ref, sh_ref, o_ref):
        o_ref[...] = _normmod_fn(x_ref[...], nw_ref[...], sc_ref[...], sh_ref[...]).astype(BF)
    return _pcall(body, name=name, grid=(S // tm,),
                  in_specs=[_row_spec(tm, Dm), _vec_spec(Dm), _vec_spec(Dm), _vec_spec(Dm)],
                  out_specs=_row_spec(tm, Dm), out_shape=jax.ShapeDtypeStruct((S, Dm), BF),
                  compiler_params=_params())(x, nw, sc, sh)


def _gate_bwd(dxv, o_ref, g_ref, do_ref, dg_ref):
    do_ref[...] = (dxv * g_ref[...]).astype(BF)
    _acc(dg_ref, jnp.sum(dxv * o_ref[...], axis=0, keepdims=True))


def _normmod_bwd(dh, x, nw, sc, sh, dres, gate, *, name, tm=256):
    S, Dm = x.shape
    tm = _tile(S, tm, 8)
    ng = 0 if gate is None else 2

    def body(dh_ref, x_ref, nw_ref, sc_ref, sh_ref, dres_ref, *rest):
        dx_ref, dnw_ref, dsc_ref, dsh_ref = rest[ng:ng + 4]
        _, vjp = jax.vjp(_normmod_fn, x_ref[...], nw_ref[...], sc_ref[...], sh_ref[...])
        dx, dnw, dsc, dsh = vjp(dh_ref[...])
        dxv = dres_ref[...] + dx
        dx_ref[...] = dxv
        _acc(dnw_ref, dnw)
        _acc(dsc_ref, dsc)
        _acc(dsh_ref, dsh)
        if gate is not None:
            _gate_bwd(dxv, rest[0], rest[1], rest[ng + 4], rest[ng + 5])
    vec = jax.ShapeDtypeStruct((1, Dm), F32)
    gate_in = [] if gate is None else [_row_spec(tm, Dm), _vec_spec(Dm)]
    gate_out = [] if gate is None else [_row_spec(tm, Dm), _vec_spec(Dm)]
    gate_shape = [] if gate is None else [jax.ShapeDtypeStruct((S, Dm), BF), vec]
    return _pcall(body, name=name, grid=(S // tm,),
                  in_specs=[_row_spec(tm, Dm), _row_spec(tm, Dm), _vec_spec(Dm), _vec_spec(Dm), _vec_spec(Dm),
                            _row_spec(tm, Dm)] + gate_in,
                  out_specs=[_row_spec(tm, Dm), _vec_spec(Dm), _vec_spec(Dm), _vec_spec(Dm)] + gate_out,
                  out_shape=[jax.ShapeDtypeStruct((S, Dm), F32), vec, vec, vec] + gate_shape,
                  compiler_params=_params())(dh, x, nw, sc, sh, dres, *([] if gate is None else gate))


def _head(x, fw, target, gate, *, tm=256):
    S, Dm = x.shape
    tm = _tile(S, tm, 8)

    def body(x_ref, fw_ref, t_ref, o_ref, g_ref, dx_ref, dfw_ref, loss_ref, do_ref, dg_ref):
        y, vjp = jax.vjp(_rms, x_ref[...], fw_ref[...])
        err = y - t_ref[...]
        dx, dfw = vjp(err * (1.0 / Dm))
        dx_ref[...] = dx
        _acc(dfw_ref, dfw)
        part = 0.5 * jnp.sum(jnp.mean(err * err, axis=-1, keepdims=True), axis=0, keepdims=True)
        _acc(loss_ref, jnp.broadcast_to(part, (8, 128)))
        _gate_bwd(dx, o_ref, g_ref, do_ref, dg_ref)
    vec = jax.ShapeDtypeStruct((1, Dm), F32)
    return _pcall(body, name="head", grid=(S // tm,),
                  in_specs=[_row_spec(tm, Dm), _vec_spec(Dm), _row_spec(tm, Dm), _row_spec(tm, Dm), _vec_spec(Dm)],
                  out_specs=[_row_spec(tm, Dm), _vec_spec(Dm), pl.BlockSpec((8, 128), lambda i: (0, 0)),
                             _row_spec(tm, Dm), _vec_spec(Dm)],
                  out_shape=[jax.ShapeDtypeStruct((S, Dm), F32), vec, jax.ShapeDtypeStruct((8, 128), F32),
                             jax.ShapeDtypeStruct((S, Dm), BF), vec],
                  compiler_params=_params())(x, fw, target, *gate)


def _tril_mask():
    r = lax.broadcasted_iota(jnp.int32, (SGU_CHUNK, SGU_CHUNK), 0)
    c = lax.broadcasted_iota(jnp.int32, (SGU_CHUNK, SGU_CHUNK), 1)
    return c <= r


def _sgu_fwd(proj, lnw, lnb, w, b_t, *, name, after=None, tm=256):
    S = proj.shape[0]
    tm = _tile(S, tm, SGU_CHUNK)
    extra = [] if after is None else [after]

    def body(zu_ref, zv_ref, lnw_ref, lnb_ref, w_ref, bt_ref, *rest):
        o_ref = rest[-1]
        u = _gelu(zu_ref[...].astype(F32))
        vn = _ln_gelu_fn(zv_ref[...].astype(F32), lnw_ref[...], lnb_ref[...]).astype(BF)
        mask = _tril_mask()
        for g in range(SGU_GROUPS):
            wm = jnp.where(mask, w_ref[g], 0.0).astype(BF)
            cols = slice(g * 128, (g + 1) * 128)
            for ci in range(tm // SGU_CHUNK):
                rows = slice(ci * SGU_CHUNK, (ci + 1) * SGU_CHUNK)
                f = jnp.dot(wm, vn[rows, cols], preferred_element_type=F32) + bt_ref[:, g:g + 1]
                o_ref[rows, cols] = (u[rows, cols] * f).astype(BF)
    return _pcall(body, name=name, grid=(S // tm,),
                  in_specs=[pl.BlockSpec((tm, SGU_WIDTH), lambda i: (i, 0)), pl.BlockSpec((tm, SGU_WIDTH), lambda i: (i, 1)),
                            _vec_spec(SGU_WIDTH), _vec_spec(SGU_WIDTH),
                            pl.BlockSpec((SGU_GROUPS, 128, 128), lambda i: (0, 0, 0)),
                            pl.BlockSpec((128, SGU_GROUPS), lambda i: (0, 0))] + [ANY] * len(extra),
                  out_specs=_row_spec(tm, SGU_WIDTH), out_shape=jax.ShapeDtypeStruct((S, SGU_WIDTH), BF),
                  compiler_params=_params())(proj, proj, lnw, lnb, w, b_t, *extra)


def _sgu_bwd(dy, proj, lnw, lnb, w, b_t, dproj, *, name, tm=256):
    S = proj.shape[0]
    tm = _tile(S, tm, SGU_CHUNK)

    def body(dy_ref, zu_ref, zv_ref, lnw_ref, lnb_ref, w_ref, bt_ref, _, dz_ref, dlnw_ref, dlnb_ref, dw_ref, dbt_ref,
             f_s, dvn_s):
        first = pl.program_id(0) == 0

        @pl.when(first)
        def _():
            dw_ref[...] = jnp.zeros_like(dw_ref)
            dbt_ref[...] = jnp.zeros_like(dbt_ref)
        u, vjp_u = jax.vjp(_gelu, zu_ref[...].astype(F32))
        vn, vjp_v = jax.vjp(_ln_gelu_fn, zv_ref[...].astype(F32), lnw_ref[...], lnb_ref[...])
        vn = vn.astype(BF)
        dy_v = dy_ref[...]
        df = (dy_v * u).astype(BF)
        mask = _tril_mask()
        for g in range(SGU_GROUPS):
            wm = jnp.where(mask, w_ref[g], 0.0).astype(BF)
            cols = slice(g * 128, (g + 1) * 128)
            dwg = jnp.zeros((128, 128), F32)
            dbg = jnp.zeros((128, 1), F32)
            for ci in range(tm // SGU_CHUNK):
                rows = slice(ci * SGU_CHUNK, (ci + 1) * SGU_CHUNK)
                vn_c = vn[rows, cols]
                df_c = df[rows, cols]
                f_s[rows, cols] = jnp.dot(wm, vn_c, preferred_element_type=F32) + bt_ref[:, g:g + 1]
                dvn_s[rows, cols] = lax.dot_general(wm, df_c, (((0,), (0,)), ((), ())), preferred_element_type=F32)
                dwg = dwg + lax.dot_general(df_c, vn_c, (((1,), (1,)), ((), ())), preferred_element_type=F32)
                dbg = dbg + jnp.sum((dy_v[rows, cols] * u[rows, cols]), axis=1, keepdims=True)
            dw_ref[g] += jnp.where(mask, dwg, 0.0)
            dbt_ref[:, g:g + 1] += dbg
        (dzu,) = vjp_u(dy_v * f_s[...])
        dzv, dlnw, dlnb = vjp_v(dvn_s[...])
        dz_ref[:, :SGU_WIDTH] = dzu.astype(BF)
        dz_ref[:, SGU_WIDTH:] = dzv.astype(BF)
        _acc(dlnw_ref, dlnw)
        _acc(dlnb_ref, dlnb)
    vec = jax.ShapeDtypeStruct((1, SGU_WIDTH), F32)
    return _pcall(body, name=name, grid=(S // tm,),
                  in_specs=[_row_spec(tm, SGU_WIDTH),
                            pl.BlockSpec((tm, SGU_WIDTH), lambda i: (i, 0)), pl.BlockSpec((tm, SGU_WIDTH), lambda i: (i, 1)),
                            _vec_spec(SGU_WIDTH), _vec_spec(SGU_WIDTH),
                            pl.BlockSpec((SGU_GROUPS, 128, 128), lambda i: (0, 0, 0)),
                            pl.BlockSpec((128, SGU_GROUPS), lambda i: (0, 0)), ANY],
                  out_specs=[pl.BlockSpec((tm, 2 * SGU_WIDTH), lambda i: (i, P_Z // (2 * SGU_WIDTH))),
                             _vec_spec(SGU_WIDTH), _vec_spec(SGU_WIDTH),
                             pl.BlockSpec((SGU_GROUPS, 128, 128), lambda i: (0, 0, 0)),
                             pl.BlockSpec((128, SGU_GROUPS), lambda i: (0, 0))],
                  out_shape=[jax.ShapeDtypeStruct(dproj.shape, BF), vec, vec,
                             jax.ShapeDtypeStruct((SGU_GROUPS, 128, 128), F32),
                             jax.ShapeDtypeStruct((128, SGU_GROUPS), F32)],
                  scratch_shapes=[pltpu.VMEM((tm, SGU_WIDTH), F32), pltpu.VMEM((tm, SGU_WIDTH), F32)],
                  input_output_aliases={7: 0},
                  compiler_params=_params())(dy, proj, proj, lnw, lnb, w, b_t, dproj)


def _merge_fwd(y_sgu, y_attn, pa, pb, proj, *, name, after=None, tm=1024, tn=512):
    S, Dm = y_sgu.shape
    tm = _tile(S, tm, 8)
    nj = Dm // tn
    extra = [] if after is None else [after]

    def body(ys_ref, ya_ref, pa_ref, pb_ref, ga_ref, gb_ref, *rest):
        a_ref, b_ref, m_ref = rest[-3:]
        a = jnp.dot(ys_ref[...], pa_ref[...], preferred_element_type=F32)
        b = jnp.dot(ya_ref[...], pb_ref[...], preferred_element_type=F32)
        a_ref[...] = a.astype(BF)
        b_ref[...] = b.astype(BF)
        m_ref[...] = (_sigmoid(ga_ref[...].astype(F32)) * a + _sigmoid(gb_ref[...].astype(F32)) * b).astype(BF)
    row = pl.BlockSpec((tm, Dm), lambda i, j: (i, 0))
    col = pl.BlockSpec((Dm, tn), lambda i, j: (0, j))
    out = pl.BlockSpec((tm, tn), lambda i, j: (i, j))
    sh = jax.ShapeDtypeStruct((S, Dm), BF)
    return _pcall(body, name=name, grid=(S // tm, nj),
                  in_specs=[row, row, col, col, pl.BlockSpec((tm, tn), lambda i, j: (i, P_G // tn + j)),
                            pl.BlockSpec((tm, tn), lambda i, j: (i, (P_G + Dm) // tn + j))] + [ANY] * len(extra),
                  out_specs=[out, out, out], out_shape=[sh, sh, sh],
                  compiler_params=_params())(y_sgu, y_attn, pa, pb, proj, proj, *extra)


def _merge_bwd(do, w_out, a, b, proj, *, name, after=None, tm=512):
    S, Dm = a.shape
    tm = _tile(S, tm, 8)
    ga_blk, gb_blk = P_G // Dm, P_G // Dm + 1
    extra = [] if after is None else [after]

    def body(do_ref, w_ref, a_ref, b_ref, ga_ref, gb_ref, *rest):
        da_ref, db_ref, dg_ref = rest[-3:]
        dmv = lax.dot_general(do_ref[...], w_ref[...], (((1,), (1,)), ((), ())), preferred_element_type=F32)
        sa = _sigmoid(ga_ref[...].astype(F32))
        sb = _sigmoid(gb_ref[...].astype(F32))
        da_ref[...] = (dmv * sa).astype(BF)
        db_ref[...] = (dmv * sb).astype(BF)
        dg_ref[:, :Dm] = (dmv * a_ref[...].astype(F32) * sa * (1.0 - sa)).astype(BF)
        dg_ref[:, Dm:] = (dmv * b_ref[...].astype(F32) * sb * (1.0 - sb)).astype(BF)
    return _pcall(body, name=name, grid=(S // tm,),
                  in_specs=[_row_spec(tm, Dm), pl.BlockSpec((Dm, Dm), lambda i: (0, 0)), _row_spec(tm, Dm), _row_spec(tm, Dm),
                            pl.BlockSpec((tm, Dm), lambda i: (i, ga_blk)), pl.BlockSpec((tm, Dm), lambda i: (i, gb_blk))]
                  + [ANY] * len(extra),
                  out_specs=[_row_spec(tm, Dm), _row_spec(tm, Dm), pl.BlockSpec((tm, 2 * Dm), lambda i: (i, P_G // (2 * Dm)))],
                  out_shape=[jax.ShapeDtypeStruct((S, Dm), BF), jax.ShapeDtypeStruct((S, Dm), BF),
                             jax.ShapeDtypeStruct((S, IN_COLS), BF)],
                  compiler_params=_params())(do, w_out, a, b, proj, proj, *extra)


def _shift_rows(a, halo, k, up):
    n = a.shape[0]
    r8 = lax.broadcasted_iota(jnp.int32, (8, a.shape[1]), 0)
    if not up:
        rolled = pltpu.roll(a, k, 0)
        patch = jnp.where(r8 < k, pltpu.roll(halo, k, 0), rolled[:8])
        return jnp.concatenate([patch, rolled[8:]], axis=0)
    rolled = pltpu.roll(a, n - k, 0)
    patch = jnp.where(r8 >= 8 - k, pltpu.roll(halo, 8 - k, 0), rolled[n - 8:])
    return jnp.concatenate([rolled[:n - 8], patch], axis=0)


def _conv_taps(a, halo):
    return _shift_rows(a, halo, 2, False), _shift_rows(a, halo, 1, False), a


HALO = 16


def _prev_halo_spec(tm, Fd):
    return pl.BlockSpec((HALO, Fd), lambda i: (jnp.maximum(i * (tm // HALO) - 1, 0), 0))


def _conv_fwd(a_ref, halo_ref, cw_ref, cb_ref):
    halo = jnp.where(pl.program_id(0) > 0, halo_ref[...].astype(F32)[HALO - 8:], 0.0)
    t0, t1, t2 = _conv_taps(a_ref[...].astype(F32), halo)
    return t0, t1, t2, cb_ref[...] + cw_ref[0:1, :] * t0 + cw_ref[1:2, :] * t1 + cw_ref[2:3, :] * t2


def _ffn_act_fwd(a, up, cw, cb, *, name, tm=256):
    S, Fd = a.shape
    tm = _tile(S, tm, HALO)

    def body(a_ref, up_ref, halo_ref, cw_ref, cb_ref, o_ref):
        _, _, _, ac = _conv_fwd(a_ref, halo_ref, cw_ref, cb_ref)
        o_ref[...] = (ac * _sigmoid(ac) * up_ref[...].astype(F32)).astype(BF)
    return _pcall(body, name=name, grid=(S // tm,),
                  in_specs=[_row_spec(tm, Fd), _row_spec(tm, Fd), _prev_halo_spec(tm, Fd),
                            pl.BlockSpec((3, Fd), lambda i: (0, 0)), _vec_spec(Fd)],
                  out_specs=_row_spec(tm, Fd), out_shape=jax.ShapeDtypeStruct((S, Fd), BF),
                  compiler_params=_params())(a, up, a, cw, cb)


def _ffn_act_bwd_a(dhf, a, up, cw, cb, *, name, tm=256):
    S, Fd = a.shape
    tm = _tile(S, tm, HALO)

    def body(dhf_ref, a_ref, up_ref, halo_ref, cw_ref, cb_ref, dac_ref, dup_ref, dcw_ref, dcb_ref):
        t0, t1, t2, ac = _conv_fwd(a_ref, halo_ref, cw_ref, cb_ref)
        s = _sigmoid(ac)
        dhf_v = dhf_ref[...].astype(F32)
        dup_ref[...] = (dhf_v * ac * s).astype(BF)
        dac = dhf_v * up_ref[...].astype(F32) * (s * (1.0 + ac * (1.0 - s)))
        dac_ref[...] = dac.astype(BF)
        _acc(dcb_ref, jnp.sum(dac, axis=0, keepdims=True))
        _acc(dcw_ref, jnp.concatenate([jnp.sum(dac * t0, axis=0, keepdims=True),
                                       jnp.sum(dac * t1, axis=0, keepdims=True),
                                       jnp.sum(dac * t2, axis=0, keepdims=True)], axis=0))
    return _pcall(body, name=name, grid=(S // tm,),
                  in_specs=[_row_spec(tm, Fd), _row_spec(tm, Fd), _row_spec(tm, Fd), _prev_halo_spec(tm, Fd),
                            pl.BlockSpec((3, Fd), lambda i: (0, 0)), _vec_spec(Fd)],
                  out_specs=[_row_spec(tm, Fd), _row_spec(tm, Fd), pl.BlockSpec((3, Fd), lambda i: (0, 0)), _vec_spec(Fd)],
                  out_shape=[jax.ShapeDtypeStruct((S, Fd), BF), jax.ShapeDtypeStruct((S, Fd), BF),
                             jax.ShapeDtypeStruct((3, Fd), F32), jax.ShapeDtypeStruct((1, Fd), F32)],
                  compiler_params=_params())(dhf, a, up, a, cw, cb)


def _ffn_act_bwd_b(dac, cw, *, name, tm=256):
    S, Fd = dac.shape
    tm = _tile(S, tm, HALO)
    last = S // tm - 1

    def body(d_ref, halo_ref, cw_ref, o_ref):
        halo = jnp.where(pl.program_id(0) < last, halo_ref[...].astype(F32)[:8], 0.0)
        d = d_ref[...].astype(F32)
        o_ref[...] = (cw_ref[2:3, :] * d + cw_ref[1:2, :] * _shift_rows(d, halo, 1, True)
                      + cw_ref[0:1, :] * _shift_rows(d, halo, 2, True)).astype(BF)
    return _pcall(body, name=name, grid=(S // tm,),
                  in_specs=[_row_spec(tm, Fd),
                            pl.BlockSpec((HALO, Fd), lambda i: (jnp.minimum((i + 1) * (tm // HALO), S // HALO - 1), 0)),
                            pl.BlockSpec((3, Fd), lambda i: (0, 0))],
                  out_specs=_row_spec(tm, Fd), out_shape=jax.ShapeDtypeStruct((S, Fd), BF),
                  compiler_params=_params())(dac, dac, cw)


def _rope_tables(pos_col, inv_row, m1_row, m2_row):
    S = pos_col.shape[0]
    tm = _tile(S, 512, 8)

    def body(p_ref, inv_ref, m1_ref, m2_ref, c_ref, s1_ref, s2_ref):
        ang = p_ref[...] * inv_ref[...]
        sn = jnp.sin(ang)
        c_ref[...] = jnp.cos(ang)
        s1_ref[...] = -sn * m1_ref[...]
        s2_ref[...] = sn * m2_ref[...]
    sh = jax.ShapeDtypeStruct((S, 128), F32)
    return _pcall(body, name="rope_tables", grid=(S // tm,),
                  in_specs=[pl.BlockSpec((tm, 1), lambda i: (i, 0)), _vec_spec(128), _vec_spec(128), _vec_spec(128)],
                  out_specs=[_row_spec(tm, 128)] * 3, out_shape=[sh, sh, sh], compiler_params=_params())(
                      pos_col, inv_row, m1_row, m2_row)


def _rope_apply(x, c, s1, s2):
    outs = []
    for j in range(x.shape[1] // 128):
        xj = x[:, j * 128:(j + 1) * 128]
        outs.append(xj * c + pltpu.roll(xj, 120, 1) * s1 + pltpu.roll(xj, 8, 1) * s2)
    return outs[0] if len(outs) == 1 else jnp.concatenate(outs, axis=1)


def _rope_apply_t(d, c, s1, s2):
    outs = []
    for j in range(d.shape[1] // 128):
        dj = d[:, j * 128:(j + 1) * 128]
        outs.append(dj * c + pltpu.roll(dj * s1, 8, 1) + pltpu.roll(dj * s2, 120, 1))
    return outs[0] if len(outs) == 1 else jnp.concatenate(outs, axis=1)


def _rope_fwd(proj, c, s1, s2, *, name, tm=512):
    S = proj.shape[0]
    tm = _tile(S, tm, 8)

    def body(q_ref, k_ref, v_ref, c_ref, s1_ref, s2_ref, qo_ref, ko_ref, vo_ref):
        cv, s1v, s2v = c_ref[...], s1_ref[...], s2_ref[...]
        qo_ref[...] = (_rope_apply(q_ref[...].astype(F32), cv, s1v, s2v) * (HEAD_DIM ** -0.5)).astype(BF)
        ko_ref[...] = _rope_apply(k_ref[...].astype(F32), cv, s1v, s2v).astype(BF)
        vo_ref[...] = v_ref[...].astype(BF)
    return _pcall(body, name=name, grid=(S // tm,),
                  in_specs=[pl.BlockSpec((tm, Q_END), lambda i: (i, P_Q // Q_END)),
                            pl.BlockSpec((tm, 128), lambda i: (i, P_K // 128)),
                            pl.BlockSpec((tm, 128), lambda i: (i, P_V // 128)),
                            _row_spec(tm, 128), _row_spec(tm, 128), _row_spec(tm, 128)],
                  out_specs=[_row_spec(tm, Q_END), _row_spec(tm, 128), _row_spec(tm, 128)],
                  out_shape=[jax.ShapeDtypeStruct((S, Q_END), BF), jax.ShapeDtypeStruct((S, 128), BF),
                             jax.ShapeDtypeStruct((S, 128), BF)],
                  compiler_params=_params())(proj, proj, proj, c, s1, s2)


def _rope_bwd(dq, dk, dv, c, s1, s2, dproj, *, name, tm=512):
    S = dq.shape[0]
    tm = _tile(S, tm, 8)
    tabs = [_row_spec(tm, 128)] * 3
    shape = jax.ShapeDtypeStruct(dproj.shape, BF)

    def body_q(dq_ref, c_ref, s1_ref, s2_ref, _, o_ref):
        o_ref[...] = _rope_apply_t(dq_ref[...].astype(F32), c_ref[...], s1_ref[...], s2_ref[...]).astype(BF)
    dproj = _pcall(body_q, name=name + "_q", grid=(S // tm,), in_specs=[_row_spec(tm, Q_END)] + tabs + [ANY],
                   out_specs=pl.BlockSpec((tm, Q_END), lambda i: (i, P_Q // Q_END)), out_shape=shape,
                   input_output_aliases={4: 0}, compiler_params=_params())(dq, c, s1, s2, dproj)

    def body_kv(dk_ref, dv_ref, c_ref, s1_ref, s2_ref, _, o_ref):
        o_ref[:, :128] = _rope_apply_t(dk_ref[...], c_ref[...], s1_ref[...], s2_ref[...]).astype(BF)
        o_ref[:, 128:] = dv_ref[...].astype(BF)
    return _pcall(body_kv, name=name + "_kv", grid=(S // tm,),
                  in_specs=[_row_spec(tm, 128), _row_spec(tm, 128)] + tabs + [ANY],
                  out_specs=pl.BlockSpec((tm, 256), lambda i: (i, P_K // 256)), out_shape=shape,
                  input_output_aliases={5: 0}, compiler_params=_params())(dk, dv, c, s1, s2, dproj)


def _lane_lo(shape):
    return lax.broadcasted_iota(jnp.int32, shape, 1) < HEAD_DIM


def _stack_heads(x, g):
    lo = _lane_lo((ATTN_BLOCK, 128))
    zero = jnp.zeros((ATTN_BLOCK, 128), x.dtype)
    parts = []
    for p in range(Q_PER_KV // 2):
        xp = x[:, (g * 4 + p) * 128:(g * 4 + p + 1) * 128]
        parts += [jnp.where(lo, xp, zero), jnp.where(lo, zero, xp)]
    return jnp.concatenate(parts, axis=0)


def _unstack_heads(o2):
    lo = _lane_lo((ATTN_BLOCK, 128))
    return [jnp.where(lo, o2[2 * p * ATTN_BLOCK:(2 * p + 1) * ATTN_BLOCK], o2[(2 * p + 1) * ATTN_BLOCK:(2 * p + 2) * ATTN_BLOCK])
            for p in range(Q_PER_KV // 2)]


def _dup_half(prev, cur, g):
    x = jnp.concatenate([prev, cur], axis=0).astype(F32)
    lo = _lane_lo(x.shape)
    r = pltpu.roll(x, HEAD_DIM, 1)
    return (jnp.where(lo, x, r) if g == 0 else jnp.where(lo, r, x)).astype(BF)


def _fold_halves(x):
    return x + pltpu.roll(x, HEAD_DIM, 1)


def _attn_bias():
    i = lax.broadcasted_iota(jnp.int32, (Q_PER_KV * ATTN_BLOCK, 2 * ATTN_BLOCK), 0) & (ATTN_BLOCK - 1)
    j = lax.broadcasted_iota(jnp.int32, (Q_PER_KV * ATTN_BLOCK, 2 * ATTN_BLOCK), 1)
    band = (j > i) & (j <= i + ATTN_BLOCK)
    return jnp.stack([jnp.where(band & (j >= ATTN_BLOCK), 0.0, -jnp.inf), jnp.where(band, 0.0, -jnp.inf)]).astype(F32)


def _attn_probs(qs, kb, sink, bias):
    s = lax.dot_general(qs, kb, (((1,), (1,)), ((), ())), preferred_element_type=F32) + bias
    m = jnp.maximum(jnp.max(s, axis=-1, keepdims=True), sink)
    p = jnp.exp(s - m)
    es = jnp.exp(sink - m)
    inv = 1.0 / (jnp.sum(p, axis=-1, keepdims=True) + es)
    return p, inv, es * inv


def _attn_specs(S):
    nb = S // ATTN_BLOCK
    qs = pl.BlockSpec((ATTN_BLOCK, Q_END), lambda n: (n, 0))
    cur = pl.BlockSpec((ATTN_BLOCK, 128), lambda n: (n, 0))
    prev = pl.BlockSpec((ATTN_BLOCK, 128), lambda n: (jnp.maximum(n - 1, 0), 0))
    sink = pl.BlockSpec((N_KV_HEADS, Q_PER_KV * ATTN_BLOCK, 1), lambda n: (0, 0, 0))
    bias = pl.BlockSpec((None, Q_PER_KV * ATTN_BLOCK, 2 * ATTN_BLOCK), lambda n: (jnp.minimum(n, 1), 0, 0))
    return nb, qs, cur, prev, sink, bias


def _attn_fwd(q, k, v, sink_rows, bias, *, name):
    S = q.shape[0]
    nb, qs, cur, prev, sink, bs = _attn_specs(S)

    def body(q_ref, kp_ref, kc_ref, vp_ref, vc_ref, sk_ref, b_ref, o_ref):
        for g in range(N_KV_HEADS):
            kb = _dup_half(kp_ref[...], kc_ref[...], g)
            vb = _dup_half(vp_ref[...], vc_ref[...], g)
            p, inv, _ = _attn_probs(_stack_heads(q_ref[...], g), kb, sk_ref[g], b_ref[...])
            o2 = jnp.dot(p.astype(BF), vb, preferred_element_type=F32) * inv
            for t, tile in enumerate(_unstack_heads(o2)):
                o_ref[:, (g * 4 + t) * 128:(g * 4 + t + 1) * 128] = tile.astype(BF)
    return _pcall(body, name=name, grid=(nb,), in_specs=[qs, prev, cur, prev, cur, sink, bs], out_specs=qs,
                  out_shape=jax.ShapeDtypeStruct(q.shape, BF), compiler_params=_params())(q, k, k, v, v, sink_rows, bias)


def _attn_bwd(do, q, k, v, sink_rows, bias, *, name):
    S = q.shape[0]
    nb, qs, cur, prev, sink, bs = _attn_specs(S)
    full = pl.BlockSpec((S, 128), lambda n: (0, 0))
    dsk_spec = pl.BlockSpec((N_KV_HEADS, Q_PER_KV, 128), lambda n: (0, 0, 0))

    def body(do_ref, q_ref, kp_ref, kc_ref, vp_ref, vc_ref, sk_ref, b_ref, dq_ref, dk_ref, dv_ref, dsk_ref):
        n = pl.program_id(0)

        @pl.when(n == 0)
        def _():
            dk_ref[...] = jnp.zeros_like(dk_ref)
            dv_ref[...] = jnp.zeros_like(dv_ref)
            dsk_ref[...] = jnp.zeros_like(dsk_ref)
        sub = lax.broadcasted_iota(jnp.int32, (Q_PER_KV, 128), 0)
        dkf, dvf = [], []
        for g in range(N_KV_HEADS):
            qst = _stack_heads(q_ref[...], g)
            dos = _stack_heads(do_ref[...], g)
            kb = _dup_half(kp_ref[...], kc_ref[...], g)
            vb = _dup_half(vp_ref[...], vc_ref[...], g)
            pu, inv, ps = _attn_probs(qst, kb, sk_ref[g], b_ref[...])
            p = pu * inv
            dp = lax.dot_general(dos, vb, (((1,), (1,)), ((), ())), preferred_element_type=F32)
            dd = jnp.sum(p * dp, axis=-1, keepdims=True)
            ds = (p * (dp - dd)).astype(BF)
            dq2 = jnp.dot(ds, kb, preferred_element_type=F32) * (HEAD_DIM ** -0.5)
            for t, tile in enumerate(_unstack_heads(dq2)):
                dq_ref[:, (g * 4 + t) * 128:(g * 4 + t + 1) * 128] = tile.astype(BF)
            dkf.append(_fold_halves(lax.dot_general(ds, qst, (((0,), (0,)), ((), ())), preferred_element_type=F32)))
            dvf.append(_fold_halves(lax.dot_general(p.astype(BF), dos, (((0,), (0,)), ((), ())),
                                                    preferred_element_type=F32)))
            dsr = -(ps * dd)
            upd = jnp.zeros((Q_PER_KV, 128), F32)
            for h in range(Q_PER_KV):
                upd = jnp.where(sub == h, jnp.sum(dsr[h * ATTN_BLOCK:(h + 1) * ATTN_BLOCK]), upd)
            dsk_ref[g] += upd
        lo = _lane_lo((2 * ATTN_BLOCK, 128))
        dkb = jnp.where(lo, dkf[0], dkf[1])
        dvb = jnp.where(lo, dvf[0], dvf[1])
        r0 = pl.multiple_of(n * ATTN_BLOCK, ATTN_BLOCK)
        dk_ref[pl.ds(r0, ATTN_BLOCK), :] += dkb[ATTN_BLOCK:]
        dv_ref[pl.ds(r0, ATTN_BLOCK), :] += dvb[ATTN_BLOCK:]

        @pl.when(n > 0)
        def _():
            rp = pl.multiple_of((n - 1) * ATTN_BLOCK, ATTN_BLOCK)
            dk_ref[pl.ds(rp, ATTN_BLOCK), :] += dkb[:ATTN_BLOCK]
            dv_ref[pl.ds(rp, ATTN_BLOCK), :] += dvb[:ATTN_BLOCK]
    return _pcall(body, name=name, grid=(nb,), in_specs=[qs, qs, prev, cur, prev, cur, sink, bs],
                  out_specs=[qs, full, full, dsk_spec],
                  out_shape=[jax.ShapeDtypeStruct(q.shape, BF), jax.ShapeDtypeStruct((S, 128), F32),
                             jax.ShapeDtypeStruct((S, 128), F32), jax.ShapeDtypeStruct((N_KV_HEADS, Q_PER_KV, 128), F32)],
                  compiler_params=_params())(do, q, k, k, v, v, sink_rows, bias)


def _ada_fwd(c_all, ada_w):
    ncol = ada_w.shape[2]

    def body(c_ref, w_ref, o_ref):
        cv = c_ref[...]
        ca = (cv * _sigmoid(cv)).astype(BF)
        for l in range(DEPTH):
            o_ref[:, l * ncol:(l + 1) * ncol] = jnp.dot(ca, w_ref[l].astype(BF), preferred_element_type=F32)
    return _pcall(body, name="ada_fwd", out_shape=jax.ShapeDtypeStruct((N_DEV, DEPTH * ncol), F32),
                  compiler_params=_params())(c_all, ada_w)


def _ada_bwd(c_all, dm):
    ncol = dm.shape[2]

    def body(c_ref, dm_ref, o_ref):
        cv = c_ref[...]
        ca = (cv * _sigmoid(cv)).astype(BF)
        for l in range(DEPTH):
            o_ref[l] = lax.dot_general(ca, dm_ref[l].astype(BF), (((0,), (0,)), ((), ())), preferred_element_type=F32)
    return _pcall(body, name="ada_bwd", out_shape=jax.ShapeDtypeStruct((DEPTH, D_MODEL, ncol), F32),
                  compiler_params=_params())(c_all, dm)


def _adamw(w, g, m, v, *, name):
    R, C = w.shape
    tr = R
    for t in range(8, 513, 8):
        if R % t == 0:
            tr = t
    c1 = 1.0 - ADAM_B1 ** ADAM_STEP
    c2 = 1.0 - ADAM_B2 ** ADAM_STEP

    def body(w_ref, g_ref, m_ref, v_ref, d_ref, mo_ref, vo_ref):
        gv = g_ref[...]
        mn = ADAM_B1 * m_ref[...] + (1.0 - ADAM_B1) * gv
        vn = ADAM_B2 * v_ref[...] + (1.0 - ADAM_B2) * (gv * gv)
        mo_ref[...] = mn
        vo_ref[...] = vn
        d_ref[...] = -ADAM_LR * ((mn / c1) / (jnp.sqrt(vn / c2) + ADAM_EPS) + ADAM_WD * w_ref[...])
    spec = pl.BlockSpec((tr, C), lambda i: (i, 0))
    sh = jax.ShapeDtypeStruct((R, C), F32)
    return _pcall(body, name=name, grid=(R // tr,), in_specs=[spec] * 4, out_specs=[spec] * 3, out_shape=[sh, sh, sh],
                  compiler_params=_params())(w, g, m, v)


def _sum8(parts, *, name):
    _, R, C = parts.shape
    tr = _tile(R, 512, 16)

    def body(p_ref, o_ref):
        acc = p_ref[0].astype(F32)
        for k in range(1, N_DEV):
            acc = acc + p_ref[k].astype(F32)
        o_ref[...] = acc
    return _pcall(body, name=name, grid=(R // tr,), in_specs=[pl.BlockSpec((N_DEV, tr, C), lambda i: (0, i, 0))],
                  out_specs=pl.BlockSpec((tr, C), lambda i: (i, 0)), out_shape=jax.ShapeDtypeStruct((R, C), F32),
                  compiler_params=_params())(parts)


MESH_ID = pl.DeviceIdType.MESH
ANY = pl.BlockSpec(memory_space=pl.ANY)


def _all_gather(x, *, name, after=None):
    R, C = x.shape
    extra = [] if after is None else [after]

    def body(x_ref, *rest):
        out_ref, send_sems, recv_sems, local_sem = rest[-4:]
        mx, my, mc = lax.axis_index("x"), lax.axis_index("y"), lax.axis_index("c")
        me, sibling = (mx, my, mc), (mx, my, 1 - mc)
        chips = [(1 - mx, my), (mx, 1 - my), (1 - mx, 1 - my)]

        def blk(px, py, pc):
            return out_ref.at[4 * px + 2 * py + pc]

        def copy(k, block, to, src=None):
            return pltpu.make_async_remote_copy(
                src_ref=blk(*block) if src is None else src, dst_ref=blk(*block),
                send_sem=send_sems.at[k], recv_sem=recv_sems.at[k], device_id=to, device_id_type=MESH_ID)

        mine = pltpu.make_async_copy(x_ref, blk(*me), local_sem)
        mine.start()
        first = [copy(0, me, sibling, src=x_ref)]
        first += [copy(1 + j, me, (*chip, mc), src=x_ref) for j, chip in enumerate(chips)]
        for cp in first:
            cp.start()
        passed = [copy(4 + j, (*chip, mc), sibling) for j, chip in enumerate(chips)]
        for j, chip in enumerate(chips):
            copy(1 + j, (*chip, mc), me).wait_recv()
            passed[j].start()
        copy(0, sibling, me).wait_recv()
        for j, chip in enumerate(chips):
            copy(4 + j, (*chip, 1 - mc), me).wait_recv()
        for cp in first + passed:
            cp.wait_send()
        mine.wait()
    return _pcall(body, name=name, in_specs=[ANY] * (1 + len(extra)), out_specs=ANY,
                  out_shape=jax.ShapeDtypeStruct((N_DEV, R, C), x.dtype),
                  scratch_shapes=[pltpu.SemaphoreType.DMA((7,)), pltpu.SemaphoreType.DMA((7,)), pltpu.SemaphoreType.DMA],
                  compiler_params=pltpu.CompilerParams(has_side_effects=True))(x, *extra)


HBM_SPEC = pl.BlockSpec(memory_space=pltpu.HBM)
SEM_SPEC = pl.BlockSpec(memory_space=pltpu.SEMAPHORE)
DATAFLOW = pltpu.SideEffectType.DATAFLOW_SIDE_EFFECTING


def _coords():
    return lax.axis_index("x"), lax.axis_index("y"), lax.axis_index("c")


def _other_chips(mx, my):
    return [(1 - mx, my), (mx, 1 - my), (1 - mx, 1 - my)]


def _plan_gather_ici(refs, send, recv):
    src, land = refs
    mx, my, mc = _coords()
    return [pltpu.make_async_remote_copy(src_ref=src, dst_ref=land.at[mc, 2 * mx + my], send_sem=send[j], recv_sem=recv[j],
                                         device_id=(px, py, mc), device_id_type=MESH_ID)
            for j, (px, py) in enumerate(_other_chips(mx, my))]


def _plan_gather_d2d(refs, send, recv):
    (land,) = refs
    mx, my, mc = _coords()
    return [pltpu.make_async_remote_copy(src_ref=land.at[mc], dst_ref=land.at[mc], send_sem=send[0], recv_sem=recv[0],
                                         device_id=(mx, my, 1 - mc), device_id_type=MESH_ID)]


def _plan_reduce_d2d(refs, send, recv):
    g, land = refs
    mx, my, mc = _coords()
    return [pltpu.make_async_remote_copy(src_ref=g.at[1 - mc], dst_ref=land, send_sem=send[0], recv_sem=recv[0],
                                         device_id=(mx, my, 1 - mc), device_id_type=MESH_ID)]


def _plan_reduce_ici(refs, send, recv):
    h, land = refs
    mx, my, mc = _coords()
    return [pltpu.make_async_remote_copy(src_ref=h.at[2 * px + py], dst_ref=land.at[j], send_sem=send[j], recv_sem=recv[j],
                                         device_id=(px, py, mc), device_id_type=MESH_ID)
            for j, (px, py) in enumerate(_other_chips(mx, my))]


def _rdma_start(bufs, n, plan, *, name, after=None):
    nb = len(bufs)
    extra = [] if after is None else [after]
    ne = len(extra)

    def body(*refs):
        ins, send, recv = refs[:nb], refs[nb + ne:nb + ne + n], refs[nb + ne + n:nb + ne + 2 * n]
        token = refs[-1]
        for cp in plan(ins, send, recv):
            cp.start()
        token[...] = jnp.zeros_like(token)
    out = _pcall(body, name=name,
                 out_shape=tuple([pltpu.SemaphoreType.DMA(())] * (2 * n) + [pltpu.HBM(b.shape, b.dtype) for b in bufs]
                                 + [jax.ShapeDtypeStruct((8, 128), F32)]),
                 in_specs=tuple([HBM_SPEC] * nb + [ANY] * ne),
                 out_specs=tuple([SEM_SPEC] * (2 * n) + [HBM_SPEC] * nb + [pl.BlockSpec(memory_space=pltpu.VMEM)]),
                 input_output_aliases={i: 2 * n + i for i in range(nb)},
                 compiler_params=pltpu.CompilerParams(has_side_effects=DATAFLOW))(
                     *[pltpu.with_memory_space_constraint(b, pltpu.HBM) for b in bufs], *extra)
    return list(out[:2 * n]), list(out[2 * n:2 * n + nb]), out[-1]


def _rdma_wait(sems, bufs, n, plan, after, *, name):
    nb = len(bufs)

    def body(*refs):
        ins, send, recv = refs[:nb], refs[nb:nb + n], refs[nb + n:nb + 2 * n]
        for cp in plan(ins, send, recv):
            cp.wait_send()
            cp.wait_recv()
    out = _pcall(body, name=name, out_shape=tuple(pltpu.HBM(b.shape, b.dtype) for b in bufs),
                 in_specs=tuple([HBM_SPEC] * nb + [SEM_SPEC] * (2 * n) + [ANY]), out_specs=tuple([HBM_SPEC] * nb),
                 input_output_aliases={i: i for i in range(nb)},
                 compiler_params=pltpu.CompilerParams(has_side_effects=DATAFLOW))(*bufs, *sems, after)
    return list(out)


def _sum_pair(g, land, cidx, *, name):
    _, nchip, R, C = g.shape
    tr = _tile(R, 1056, 16)

    def body(c_ref, g_ref, l_ref, o_ref):
        o_ref[...] = (g_ref[...].astype(F32) + l_ref[...].astype(F32)).astype(BF)
    grid_spec = pltpu.PrefetchScalarGridSpec(
        num_scalar_prefetch=1, grid=(nchip, R // tr),
        in_specs=[pl.BlockSpec((None, None, tr, C), lambda p, i, c_ref: (c_ref[0], p, i, 0)),
                  pl.BlockSpec((None, tr, C), lambda p, i, c_ref: (p, i, 0))],
        out_specs=pl.BlockSpec((None, tr, C), lambda p, i, c_ref: (p, i, 0)))
    return _pcall(body, name=name, grid_spec=grid_spec, out_shape=jax.ShapeDtypeStruct((nchip, R, C), BF),
                  compiler_params=_params())(cidx, g, land)


def _sum_chips(h, land, chipidx, *, name):
    _, R, C = h.shape
    tr = _tile(R, 1056, 16)

    def body(c_ref, h_ref, l_ref, o_ref):
        acc = h_ref[...].astype(F32)
        for j in range(3):
            acc = acc + l_ref[j].astype(F32)
        o_ref[...] = acc
    grid_spec = pltpu.PrefetchScalarGridSpec(
        num_scalar_prefetch=1, grid=(R // tr,),
        in_specs=[pl.BlockSpec((None, tr, C), lambda i, c_ref: (c_ref[0], i, 0)),
                  pl.BlockSpec((3, tr, C), lambda i, c_ref: (0, i, 0))],
        out_specs=pl.BlockSpec((tr, C), lambda i, c_ref: (i, 0)))
    return _pcall(body, name=name, grid_spec=grid_spec, out_shape=jax.ShapeDtypeStruct((R, C), F32),
                  compiler_params=_params())(chipidx, h, land)


PART_IN = ("w_in",)
PART_MIX = ("proj_a", "proj_b", "w_out")
PART_FFN = ("ffn_w_gate", "ffn_w_up", "ffn_w_down")


def _part_rows(names):
    return sum(BIG_ROWS[n] for n in names)


def _part_offsets(names):
    off, r = {}, 0
    for n in names:
        off[n] = r
        r += BIG_ROWS[n]
    return off


def _pack_shards(shards, l, names):
    return jnp.concatenate([(shards[n][l].T if n in COL_SHARDED else shards[n][l]).astype(BF) for n in names], axis=0)


def _unpack_weights(full8, names):
    off = _part_offsets(names)

    def whole(n):
        return full8[:, off[n]:off[n] + BIG_ROWS[n], :].reshape(N_DEV * BIG_ROWS[n], 1024)
    out = {}
    if "w_in" in names:
        wt_in = whole("w_in")
        out["wt_in"] = jnp.concatenate([wt_in[V_END:], wt_in[:V_END]], axis=0)
    for n in ("proj_a", "proj_b", "w_out"):
        if n in names:
            out[n] = whole(n)
    if "ffn_w_gate" in names:
        out["wt_gate"], out["wt_up"], out["w_down"] = whole("ffn_w_gate"), whole("ffn_w_up"), whole("ffn_w_down")
    return out


def _from_land(land):
    return land.transpose(1, 0, 2, 3).reshape(N_DEV, land.shape[2], 1024)


def _pack_grads(wg, names):
    full = {"proj_a": wg.get("proj_a"), "proj_b": wg.get("proj_b"), "w_out": wg.get("w_out"), "ffn_w_down": wg.get("w_down"),
            "ffn_w_gate": wg.get("wt_gate"), "ffn_w_up": wg.get("wt_up")}
    if "w_in" in names:
        full["w_in"] = jnp.concatenate([wg["wt_in"][P_Q:], wg["wt_in"][:P_Q]], axis=0)
    blocks = jnp.concatenate([full[n].reshape(N_DEV, BIG_ROWS[n], 1024) for n in names], axis=1)
    return blocks.reshape(4, 2, _part_rows(names), 1024).transpose(1, 0, 2, 3)


def _unpack_shard_grads(gs, names):
    off = _part_offsets(names)
    out = {}
    for n in names:
        blk = gs[off[n]:off[n] + BIG_ROWS[n]]
        out[n] = blk.T if n in COL_SHARDED else blk
    return out


def _rope_setup(positions):
    S = positions.shape[0]
    inv = ROPE_THETA ** (-jnp.arange(0, ROT_DIM, 2, dtype=F32) / ROT_DIM)
    lane = np.arange(128) % HEAD_DIM
    half = ROT_DIM // 2
    inv_row = jnp.where(lane < ROT_DIM, jnp.tile(inv, 128 // half), 0.0)[None, :].astype(F32)
    m1_row = jnp.asarray((lane < half).astype(np.float32))[None, :]
    m2_row = jnp.asarray(((lane >= half) & (lane < ROT_DIM)).astype(np.float32))[None, :]
    return (*_rope_tables(positions.astype(F32).reshape(S, 1), inv_row, m1_row, m2_row), _attn_bias())


def _hook(hooks, point, after):
    f = None if hooks is None else hooks.get(point)
    return None if f is None else f(after)


def _layer_fwd(l, x, mod_l, W, small, rope, hooks=None):
    rc, rs1, rs2, bias = rope
    sh1, sc1, g1, sh2, sc2, g2 = [mod_l[i * D_MODEL:(i + 1) * D_MODEL][None, :] for i in range(6)]
    nw1, nw2 = small["norm1_w"][l][None, :], small["norm2_w"][l][None, :]
    h = _normmod_fwd(x, nw1, sc1, sh1, name=f"normmod1_fwd{l}")
    tok = _hook(hooks, "mm_in", h)
    proj = _mm(h, W["wt_in"], nt=True, out_dtype=BF, name=f"mm_in{l}", after=tok, tn_cap=768)
    q_r, k_r, v_b = _rope_fwd(proj, rc, rs1, rs2, name=f"rope_fwd{l}")
    sink_rows = jnp.repeat(small["attn_sinks"][l].reshape(N_KV_HEADS, Q_PER_KV), ATTN_BLOCK, axis=1)[..., None]
    y_attn = _attn_fwd(q_r, k_r, v_b, sink_rows, bias, name=f"attn_fwd{l}")
    lnw, lnb = small["sgu_ln_w"][l][None, :], small["sgu_ln_b"][l][None, :]
    sgu_bt = small["sgu_b"][l].T
    y_sgu = _sgu_fwd(proj, lnw, lnb, small["sgu_w"][l], sgu_bt, name=f"sgu_fwd{l}", after=_hook(hooks, "sgu", y_attn))
    tok = _hook(hooks, "mm_pa", y_sgu)
    a_br, b_br, merged = _merge_fwd(y_sgu, y_attn, W["proj_a"], W["proj_b"], proj, name=f"merge_fwd{l}", after=tok)
    x1, o1 = _mm(merged, W["w_out"], nt=False, out_dtype=F32, name=f"mm_out{l}", res=x, gvec=g1)
    h2 = _normmod_fwd(x1, nw2, sc2, sh2, name=f"normmod2_fwd{l}")
    a_g = _mm(h2, W["wt_gate"], nt=True, out_dtype=BF, name=f"mm_gate{l}", tn_cap=1408)
    a_u = _mm(h2, W["wt_up"], nt=True, out_dtype=BF, name=f"mm_up{l}", tn_cap=1408)
    cw, cb = small["ffn_conv_w"][l], small["ffn_conv_b"][l][None, :]
    hf = _ffn_act_fwd(a_g, a_u, cw, cb, name=f"ffn_act_fwd{l}")
    x2, o2 = _mm(hf, W["w_down"], nt=False, out_dtype=F32, name=f"mm_down{l}", res=x1, gvec=g2)
    saved = dict(x=x, h=h, proj=proj, q_r=q_r, k_r=k_r, v_b=v_b, sink_rows=sink_rows, y_attn=y_attn, y_sgu=y_sgu,
                 a_br=a_br, b_br=b_br, merged=merged, x1=x1, o1=o1, h2=h2, a_g=a_g, a_u=a_u, hf=hf, o2=o2)
    return x2, saved


def _layer_bwd(l, dx, do2, dg2, mod_l, W, small, rope, sv, below=None, hooks=None, wg=None):
    rc, rs1, rs2, bias = rope
    sh1, sc1, g1, sh2, sc2, g2 = [mod_l[i * D_MODEL:(i + 1) * D_MODEL][None, :] for i in range(6)]
    nw1, nw2 = small["norm1_w"][l][None, :], small["norm2_w"][l][None, :]
    cw, cb = small["ffn_conv_w"][l], small["ffn_conv_b"][l][None, :]
    lnw, lnb = small["sgu_ln_w"][l][None, :], small["sgu_ln_b"][l][None, :]
    sgu_bt = small["sgu_b"][l].T
    wg = {} if wg is None else wg
    dhf = _mm(do2, W["w_down"], nt=True, out_dtype=BF, name=f"mm_down_dx{l}", after=_hook(hooks, "mm_down_dx", do2),
              tn_cap=1408)
    wg["w_down"] = _mm_tn(sv["hf"], do2, name=f"mm_down_dw{l}")
    dac, dup, dcw, dcb = _ffn_act_bwd_a(dhf, sv["a_g"], sv["a_u"], cw, cb, name=f"ffn_act_bwd_a{l}")
    da = _ffn_act_bwd_b(dac, cw, name=f"ffn_act_bwd_b{l}")
    dh2 = _mm([da, dup], [W["wt_gate"], W["wt_up"]], nt=False, out_dtype=F32, name=f"mm_gu_dx{l}",
              after=_hook(hooks, "mm_gu_dx", da))
    wg["wt_gate"] = _mm_tn(da, sv["h2"], name=f"mm_gate_dw{l}")
    wg["wt_up"] = _mm_tn(dup, sv["h2"], name=f"mm_up_dw{l}")
    dx1, dnw2, dsc2, dsh2, do1, dg1 = _normmod_bwd(dh2, sv["x1"], nw2, sc2, sh2, dx, (sv["o1"], g1), name=f"normmod2_bwd{l}")
    d_a, d_b, dproj = _merge_bwd(do1, W["w_out"], sv["a_br"], sv["b_br"], sv["proj"], name=f"merge_bwd{l}",
                                 after=_hook(hooks, "merge_bwd", do1))
    wg["w_out"] = _mm_tn(sv["merged"], do1, name=f"mm_out_dw{l}")
    dysgu = _mm(d_a, W["proj_a"], nt=True, out_dtype=F32, name=f"mm_pa_dx{l}", after=_hook(hooks, "mm_pa_dx", d_a))
    dyattn = _mm(d_b, W["proj_b"], nt=True, out_dtype=BF, name=f"mm_pb_dx{l}")
    wg["proj_a"] = _mm_tn(sv["y_sgu"], d_a, name=f"mm_pa_dw{l}")
    wg["proj_b"] = _mm_tn(sv["y_attn"], d_b, name=f"mm_pb_dw{l}")
    dproj, dlnw, dlnb, dsguw, dsgubt = _sgu_bwd(dysgu, sv["proj"], lnw, lnb, small["sgu_w"][l], sgu_bt, dproj,
                                                name=f"sgu_bwd{l}")
    dq_r, dk_r, dv_b, dsk = _attn_bwd(dyattn, sv["q_r"], sv["k_r"], sv["v_b"], sv["sink_rows"], bias, name=f"attn_bwd{l}")
    dproj = _rope_bwd(dq_r, dk_r, dv_b, rc, rs1, rs2, dproj, name=f"rope_bwd{l}")
    wg["wt_in"] = _mm_tn(dproj, sv["h"], name=f"mm_in_dw{l}")
    dh = _mm(dproj, W["wt_in"], nt=False, out_dtype=F32, name=f"mm_in_dx{l}", after=_hook(hooks, "mm_in_dx", wg["wt_in"]))
    dx0, dnw1, dsc1, dsh1, *gate_below = _normmod_bwd(dh, sv["x"], nw1, sc1, sh1, dx1, below, name=f"normmod1_bwd{l}")
    dmod = jnp.concatenate([dsh1, dsc1, dg1, dsh2, dsc2, dg2], axis=1)[0]
    sg = {"norm1_w": dnw1[0], "norm2_w": dnw2[0], "attn_sinks": dsk[:, :, 0].reshape(N_Q_HEADS),
          "sgu_ln_w": dlnw[0], "sgu_ln_b": dlnb[0], "sgu_w": dsguw, "sgu_b": dsgubt.T,
          "ffn_conv_w": dcw, "ffn_conv_b": dcb[0]}
    return (dx0, *gate_below), wg, sg, dmod


SMALL = ("ada_b", "norm1_w", "attn_sinks", "sgu_ln_w", "sgu_ln_b", "sgu_w", "sgu_b", "norm2_w", "ffn_conv_b", "final_norm_w")
WEIGHT_ORDER = ("ada_w", "ada_b", "norm1_w", "w_in", "attn_sinks", "sgu_ln_w", "sgu_ln_b", "sgu_w", "sgu_b", "proj_a", "proj_b",
                "w_out", "norm2_w", "ffn_w_gate", "ffn_w_up", "ffn_conv_w", "ffn_conv_b", "ffn_w_down", "final_norm_w")


def _flat_pack(arrs, rows):
    flat = jnp.concatenate([a.reshape(-1) for a in arrs])
    return jnp.pad(flat, (0, rows * 1024 - flat.shape[0])).reshape(rows, 1024)


def _flat_unpack(buf, shapes):
    flat = buf.reshape(-1)
    out, o = [], 0
    for s in shapes:
        n = int(np.prod(s))
        out.append(flat[o:o + n].reshape(s))
        o += n
    return out


def _adam2d(w, g, m, v, *, name):
    shp = w.shape
    r2 = (int(np.prod(shp[:-1])), shp[-1]) if len(shp) > 1 else (1, shp[0])
    d, mn, vn = _adamw(w.reshape(r2), g.reshape(r2), m.reshape(r2), v.reshape(r2), name=name)
    return d.reshape(shp), mn.reshape(shp), vn.reshape(shp)


def kernel(x, c, positions, ada_w, ada_b, norm1_w, w_in, attn_sinks, sgu_ln_w, sgu_ln_b, sgu_w, sgu_b, proj_a, proj_b, w_out, norm2_w, ffn_w_gate, ffn_w_up, ffn_conv_w, ffn_conv_b, ffn_w_down, final_norm_w, loss_target, m_ada_w, m_ada_b, m_norm1_w, m_w_in, m_attn_sinks, m_sgu_ln_w, m_sgu_ln_b, m_sgu_w, m_sgu_b, m_proj_a, m_proj_b, m_w_out, m_norm2_w, m_ffn_w_gate, m_ffn_w_up, m_ffn_conv_w, m_ffn_conv_b, m_ffn_w_down, m_final_norm_w, v_ada_w, v_ada_b, v_norm1_w, v_w_in, v_attn_sinks, v_sgu_ln_w, v_sgu_ln_b, v_sgu_w, v_sgu_b, v_proj_a, v_proj_b, v_w_out, v_norm2_w, v_ffn_w_gate, v_ffn_w_up, v_ffn_conv_w, v_ffn_conv_b, v_ffn_w_down, v_final_norm_w):
    wts = dict(ada_w=ada_w, ada_b=ada_b, norm1_w=norm1_w, w_in=w_in, attn_sinks=attn_sinks, sgu_ln_w=sgu_ln_w,
               sgu_ln_b=sgu_ln_b, sgu_w=sgu_w, sgu_b=sgu_b, proj_a=proj_a, proj_b=proj_b, w_out=w_out, norm2_w=norm2_w,
               ffn_w_gate=ffn_w_gate, ffn_w_up=ffn_w_up, ffn_conv_w=ffn_conv_w, ffn_conv_b=ffn_conv_b,
               ffn_w_down=ffn_w_down, final_norm_w=final_norm_w)
    mom = dict(ada_w=m_ada_w, ada_b=m_ada_b, norm1_w=m_norm1_w, w_in=m_w_in, attn_sinks=m_attn_sinks, sgu_ln_w=m_sgu_ln_w,
               sgu_ln_b=m_sgu_ln_b, sgu_w=m_sgu_w, sgu_b=m_sgu_b, proj_a=m_proj_a, proj_b=m_proj_b, w_out=m_w_out,
               norm2_w=m_norm2_w, ffn_w_gate=m_ffn_w_gate, ffn_w_up=m_ffn_w_up, ffn_conv_w=m_ffn_conv_w,
               ffn_conv_b=m_ffn_conv_b, ffn_w_down=m_ffn_w_down, final_norm_w=m_final_norm_w)
    var = dict(ada_w=v_ada_w, ada_b=v_ada_b, norm1_w=v_norm1_w, w_in=v_w_in, attn_sinks=v_attn_sinks, sgu_ln_w=v_sgu_ln_w,
               sgu_ln_b=v_sgu_ln_b, sgu_w=v_sgu_w, sgu_b=v_sgu_b, proj_a=v_proj_a, proj_b=v_proj_b, w_out=v_w_out,
               norm2_w=v_norm2_w, ffn_w_gate=v_ffn_w_gate, ffn_w_up=v_ffn_w_up, ffn_conv_w=v_ffn_conv_w,
               ffn_conv_b=v_ffn_conv_b, ffn_w_down=v_ffn_w_down, final_norm_w=v_final_norm_w)
    me = 4 * lax.axis_index("x") + 2 * lax.axis_index("y") + lax.axis_index("c")
    ada_cols = ada_w.shape[2]

    c_all = _all_gather(jnp.broadcast_to(c, (8, D_MODEL)), name="ag_c")[:, 0, :]
    prod = _ada_fwd(c_all, ada_w)
    prod_all = _all_gather(prod, name="ag_mod")
    mine = lax.dynamic_index_in_dim(prod_all, me, axis=1, keepdims=False)
    mod = jnp.stack([mine[:, l * ada_cols:(l + 1) * ada_cols].reshape(-1) for l in range(DEPTH)]) + ada_b

    conv_cols = ffn_conv_w.shape[2]
    conv_all = _all_gather(_flat_pack([ffn_conv_w], 8), name="ag_conv", after=mod)
    conv_full = jnp.stack([a.reshape(DEPTH, 3, conv_cols) for a in
                           [conv_all[j].reshape(-1)[:DEPTH * 3 * conv_cols] for j in range(N_DEV)]], axis=2)
    conv_full = conv_full.reshape(DEPTH, 3, FFN_DIM)
    small = {n: wts[n] for n in SMALL}
    small["ffn_conv_w"] = conv_full

    mx, my, mc = _coords()
    cidx = jnp.reshape(mc, (1,)).astype(jnp.int32)
    chipidx = jnp.reshape(2 * mx + my, (1,)).astype(jnp.int32)
    rope = _rope_setup(positions[0])

    class Gather:
        def __init__(self, names, l, tag):
            self.names, self.tag = names, tag
            self.src = _pack_shards(wts, l, names)
            self.land = lax.dynamic_update_slice(lax.empty((2, 4, _part_rows(names), 1024), BF), self.src[None, None],
                                                 (mc, 2 * mx + my, 0, 0))

        def ici_start(self, after):
            self.sems, (self.src, self.land), tok = _rdma_start([self.src, self.land], 3, _plan_gather_ici,
                                                                name=f"ag_{self.tag}_ici_start", after=after)
            return tok

        def ici_wait_d2d_start(self, after):
            _, land = _rdma_wait(self.sems, [self.src, self.land], 3, _plan_gather_ici, after, name=f"ag_{self.tag}_ici_wait")
            self.sems, (self.land,), tok = _rdma_start([land], 1, _plan_gather_d2d, name=f"ag_{self.tag}_d2d_start")
            return tok

        def d2d_wait(self, after):
            (land,) = _rdma_wait(self.sems, [self.land], 1, _plan_gather_d2d, after, name=f"ag_{self.tag}_d2d_wait")
            return _unpack_weights(_from_land(land), self.names)

    W0 = _unpack_weights(_all_gather(_pack_shards(wts, 0, PART_IN), name="ag_w0_in", after=conv_all), PART_IN)
    W1 = {}
    g_mix0, g_ffn0 = Gather(PART_MIX, 0, "w0_mix"), Gather(PART_FFN, 0, "w0_ffn")
    g_all1 = Gather(BIG, 1, "w1")
    toks = {}

    def rest0_to_sibling(after):
        toks["mix"] = g_mix0.ici_wait_d2d_start(after)
        toks["ffn"] = g_ffn0.ici_wait_d2d_start(toks["mix"])
        return toks["ffn"]

    def rest0_then_layer1(after):
        W0.update(g_mix0.d2d_wait(after))
        W0.update(g_ffn0.d2d_wait(W0["proj_a"]))
        return g_all1.ici_start(W0["w_down"])

    x1, sv0 = _layer_fwd(0, x[0], mod[0], W0, small, rope,
                         {"mm_in": lambda after: g_ffn0.ici_start(g_mix0.ici_start(W0["wt_in"])),
                          "sgu": rest0_to_sibling, "mm_pa": rest0_then_layer1})
    g_all1.ici_wait_d2d_start(x1)
    x2, sv1 = _layer_fwd(1, x1, mod[1], W1, small, rope, {"mm_in": lambda after: W1.update(g_all1.d2d_wait(after))})
    gate2 = [mod[l][5 * D_MODEL:][None, :] for l in range(DEPTH)]
    dx2, dfw, loss_tile, do2, dg2 = _head(x2, final_norm_w[None, :], loss_target[0], (sv1["o2"], gate2[1]))
    loss = lax.psum(loss_tile[0, 0], ("x", "y", "c"))

    class Reduce:
        def __init__(self, names, tag):
            self.names, self.tag, self.rows = names, tag, _part_rows(names)

        def d2d_start(self, wg, after=None):
            self.sems, self.bufs, tok = _rdma_start([_pack_grads(wg, self.names), lax.empty((4, self.rows, 1024), BF)], 1,
                                                    _plan_reduce_d2d, name=f"rs_{self.tag}_d2d_start", after=after)
            return tok

        def d2d_wait_ici_start(self, after):
            g_t, land_a = _rdma_wait(self.sems, self.bufs, 1, _plan_reduce_d2d, after, name=f"rs_{self.tag}_d2d_wait")
            h = _sum_pair(g_t, land_a, cidx, name=f"rs_{self.tag}_sum_pair")
            self.sems, self.bufs, tok = _rdma_start([h, lax.empty((3, self.rows, 1024), BF)], 3, _plan_reduce_ici,
                                                    name=f"rs_{self.tag}_ici_start")
            return tok

        def ici_wait(self, after):
            h_t, land_b = _rdma_wait(self.sems, self.bufs, 3, _plan_reduce_ici, after, name=f"rs_{self.tag}_ici_wait")
            return _unpack_shard_grads(_sum_chips(h_t, land_b, chipidx, name=f"rs_{self.tag}_sum_chips"), self.names)

    (dx1, do2, dg2), wg1, sg1, dmod1 = _layer_bwd(1, dx2, do2, dg2, mod[1], W1, small, rope, sv1, below=(sv0["o2"], gate2[0]))
    r_all1, r_ffn0, r_mix0, r_in0 = Reduce(BIG, "g1"), Reduce(PART_FFN, "g0_ffn"), Reduce(PART_MIX, "g0_mix"), Reduce(PART_IN, "g0_in")
    tok1 = r_all1.d2d_start(wg1)
    wg0, shard1 = {}, {}

    def layer1_done_then_ffn0(after):
        shard1.update(r_all1.ici_wait(after))
        return r_ffn0.d2d_wait_ici_start(shard1["w_in"])

    def mix0_and_in0(after):
        tok = r_in0.d2d_start(wg0, r_mix0.d2d_start(wg0, after))
        return r_in0.d2d_wait_ici_start(r_mix0.d2d_wait_ici_start(tok))

    (grad_x,), _, sg0, dmod0 = _layer_bwd(
        0, dx1, do2, dg2, mod[0], W0, small, rope, sv0, wg=wg0,
        hooks={"mm_down_dx": lambda after: tok1, "mm_gu_dx": r_all1.d2d_wait_ici_start,
               "merge_bwd": lambda after: r_ffn0.d2d_start(wg0, after), "mm_pa_dx": layer1_done_then_ffn0,
               "mm_in_dx": mix0_and_in0})
    shard0 = r_ffn0.ici_wait(grad_x)
    shard0.update(r_mix0.ici_wait(shard0["ffn_w_down"]))
    shard0.update(r_in0.ici_wait(shard0["w_out"]))
    grads = {n: jnp.stack([shard0[n], shard1[n]]) for n in BIG}
    sg = {n: jnp.stack([sg0[n], sg1[n]]) for n in sg0}
    sg["final_norm_w"] = dfw[0]
    dmod = jnp.stack([dmod0, dmod1])

    small_names = [n for n in SMALL if n != "ada_b"] + ["ffn_conv_w"]
    small_shapes = [(DEPTH, 6 * D_MODEL)] + [sg[n].shape for n in small_names]
    n_small = sum(int(np.prod(s)) for s in small_shapes)
    rows = -(-n_small // 1024 // 16) * 16
    sm_all = _all_gather(_flat_pack([dmod] + [sg[n] for n in small_names], rows).astype(BF), name="ag_small",
                         after=shard0["w_in"])
    sm_sum = _flat_unpack(_sum8(sm_all, name="sum_small"), small_shapes)
    sm_all = sm_all.astype(F32)
    grads["ada_b"] = sm_sum[0]
    for n, gsum in zip(small_names, sm_sum[1:]):
        grads[n] = gsum
    grads["ffn_conv_w"] = lax.dynamic_slice_in_dim(grads["ffn_conv_w"], me * conv_cols, conv_cols, axis=2)
    dmod_all = sm_all[:, :DEPTH * 6, :].reshape(N_DEV, DEPTH, 6 * D_MODEL)
    dm_mine = lax.dynamic_slice_in_dim(dmod_all, me * ada_cols, ada_cols, axis=2).transpose(1, 0, 2)
    dm_mine = jnp.pad(dm_mine, ((0, 0), (0, 8), (0, 0)))
    grads["ada_w"] = _ada_bwd(jnp.pad(c_all, ((0, 8), (0, 0))), dm_mine)

    packed_small = [n for n in SMALL]
    pshapes = [wts[n].shape for n in packed_small]
    prow = -(-sum(int(np.prod(s)) for s in pshapes) // 1024 // 8) * 8
    pk = lambda d: _flat_pack([d[n] for n in packed_small], prow)
    d_s, m_s, v_s = _adamw(pk(wts), pk(grads), pk(mom), pk(var), name="adamw_small")
    delta, new_m, new_v = {}, {}, {}
    for n, dd, mm, vv in zip(packed_small, _flat_unpack(d_s, pshapes), _flat_unpack(m_s, pshapes), _flat_unpack(v_s, pshapes)):
        delta[n], new_m[n], new_v[n] = dd, mm, vv
    for n in WEIGHT_ORDER:
        if n not in delta:
            delta[n], new_m[n], new_v[n] = _adam2d(wts[n], grads[n], mom[n], var[n], name=f"adamw_{n}")
    return (loss, grad_x[None], *[grads[n] for n in WEIGHT_ORDER], *[delta[n] for n in WEIGHT_ORDER],
            *[new_m[n] for n in WEIGHT_ORDER], *[new_v[n] for n in WEIGHT_ORDER])
```

```python
import functools

import jax
import jax.numpy as jnp
import numpy as np
from jax import lax
from jax.experimental import pallas as pl
from jax.experimental.pallas import tpu as pltpu

F32 = jnp.float32
BF = jnp.bfloat16

N_DEV = 8
D_MODEL = 1024
DEPTH = 2
N_Q_HEADS = 16
N_KV_HEADS = 2
HEAD_DIM = 64
Q_PER_KV = N_Q_HEADS // N_KV_HEADS
ATTN_BLOCK = 128
ROPE_THETA = 500000.0
ROT_DIM = HEAD_DIM // 4
SGU_WIDTH = 1024
SGU_GROUPS = 8
SGU_CHUNK = 128
FFN_DIM = 2816
NORM_EPS = 1e-6
Q_END = N_Q_HEADS * HEAD_DIM
K_END = Q_END + N_KV_HEADS * HEAD_DIM
V_END = K_END + N_KV_HEADS * HEAD_DIM
Z_END = V_END + 2 * SGU_WIDTH
IN_COLS = Z_END + 2 * D_MODEL
P_Z, P_G, P_Q, P_K, P_V = 0, 2048, 4096, 5120, 5248

ADAM_LR = 0.001
ADAM_B1 = 0.9
ADAM_B2 = 0.999
ADAM_EPS = 1e-08
ADAM_WD = 0.01
ADAM_STEP = 10

VMEM_LIMIT_BYTES = 56 * 1024 * 1024

BIG = ("w_in", "proj_a", "proj_b", "w_out", "ffn_w_gate", "ffn_w_up", "ffn_w_down")
COL_SHARDED = ("w_in", "ffn_w_gate", "ffn_w_up")
BIG_SHAPE = {"w_in": (D_MODEL, IN_COLS), "proj_a": (SGU_WIDTH, D_MODEL), "proj_b": (Q_END, D_MODEL),
             "w_out": (D_MODEL, D_MODEL), "ffn_w_gate": (D_MODEL, FFN_DIM), "ffn_w_up": (D_MODEL, FFN_DIM),
             "ffn_w_down": (FFN_DIM, D_MODEL)}
BIG_ROWS = {n: BIG_SHAPE[n][0] * BIG_SHAPE[n][1] // N_DEV // 1024 for n in BIG}
LAYER_ROWS = sum(BIG_ROWS.values())


def _pcall(body, **kw):
    return pl.pallas_call(body, **kw)


def _params(**kw):
    return pltpu.CompilerParams(vmem_limit_bytes=VMEM_LIMIT_BYTES, **kw)


def _tile(n, cap, unit=128):
    if n <= cap:
        return n
    best = 0
    t = unit
    while t <= cap:
        if n % t == 0:
            best = t
        t += unit
    assert best, (n, cap, unit)
    return best


def _mm(a, b, *, nt, out_dtype, name, res=None, gvec=None, after=None, tm=None, tn_cap=1024):
    a_list = list(a) if isinstance(a, (list, tuple)) else [a]
    b_list = list(b) if isinstance(b, (list, tuple)) else [b]
    a, b = a_list[0], b_list[0]
    M, K = a.shape
    N = b.shape[0] if nt else b.shape[1]
    k_total = sum(x.shape[1] for x in a_list)
    tm = _tile(M, tm or (1024 if k_total <= 1024 else 512), 8)
    tn = _tile(N, tn_cap)
    dn = (((1,), (1,)), ((), ())) if nt else (((1,), (0,)), ((), ()))

    def b_spec_of(x):
        k = x.shape[1] if nt else x.shape[0]
        return pl.BlockSpec((tn, k), lambda i, j: (j, 0)) if nt else pl.BlockSpec((k, tn), lambda i, j: (0, j))
    b_spec = b_spec_of(b)
    o_spec = pl.BlockSpec((tm, tn), lambda i, j: (i, j))
    if res is None:
        extra = [] if after is None else [after]
        n = len(a_list)

        def body(*refs):
            o_ref = refs[-1]
            acc = None
            for a_ref, b_ref in zip(refs[:n], refs[n:2 * n]):
                d = lax.dot_general(a_ref[...].astype(BF), b_ref[...].astype(BF), dn, preferred_element_type=F32)
                acc = d if acc is None else acc + d
            o_ref[...] = acc.astype(out_dtype)
        return _pcall(body, name=name, grid=(M // tm, N // tn),
                      in_specs=[pl.BlockSpec((tm, x.shape[1]), lambda i, j: (i, 0)) for x in a_list]
                      + [b_spec_of(x) for x in b_list] + [ANY] * len(extra), out_specs=o_spec,
                      out_shape=jax.ShapeDtypeStruct((M, N), out_dtype), compiler_params=_params())(
                          *a_list, *b_list, *extra)

    def body_res(a_ref, b_ref, r_ref, g_ref, o_ref, acc_ref):
        acc = lax.dot_general(a_ref[...].astype(BF), b_ref[...].astype(BF), dn, preferred_element_type=F32)
        acc_ref[...] = acc
        o_ref[...] = r_ref[...] + g_ref[...] * acc
    return _pcall(body_res, name=name, grid=(M // tm, N // tn),
                  in_specs=[pl.BlockSpec((tm, K), lambda i, j: (i, 0)), b_spec, o_spec,
                            pl.BlockSpec((1, tn), lambda i, j: (0, j))],
                  out_specs=[o_spec, o_spec],
                  out_shape=[jax.ShapeDtypeStruct((M, N), F32), jax.ShapeDtypeStruct((M, N), F32)],
                  compiler_params=_params())(a, b, res, gvec)


def _mm_tn(a, b, *, name, out_dtype=BF, tk=1024, tm_cap=1408, tn_cap=1024):
    S, M = a.shape
    N = b.shape[1]
    tk = _tile(S, tk, 8)
    tm = _tile(M, tm_cap)
    tn = _tile(N, tn_cap)
    nk = S // tk

    def body(a_ref, b_ref, o_ref, acc_ref):
        k = pl.program_id(2)

        @pl.when(k == 0)
        def _():
            acc_ref[...] = jnp.zeros_like(acc_ref)
        acc_ref[...] += lax.dot_general(a_ref[...].astype(BF), b_ref[...].astype(BF), (((0,), (0,)), ((), ())),
                                        preferred_element_type=F32)

        @pl.when(k == nk - 1)
        def _():
            o_ref[...] = acc_ref[...].astype(out_dtype)
    return _pcall(body, name=name, grid=(M // tm, N // tn, nk),
                  in_specs=[pl.BlockSpec((tk, tm), lambda i, j, k: (k, i)),
                            pl.BlockSpec((tk, tn), lambda i, j, k: (k, j))],
                  out_specs=pl.BlockSpec((tm, tn), lambda i, j, k: (i, j)),
                  out_shape=jax.ShapeDtypeStruct((M, N), out_dtype), scratch_shapes=[pltpu.VMEM((tm, tn), F32)],
                  compiler_params=_params())(a, b)


def _rms(x, w):
    return x * lax.rsqrt(jnp.mean(x * x, axis=-1, keepdims=True) + NORM_EPS) * w


def _normmod_fn(x, nw, sc, sh):
    return _rms(x, nw) * (1.0 + sc) + sh


def _gelu(x):
    return 0.5 * x * (1.0 + lax.erf(x * (2.0 ** -0.5)))


def _ln_gelu_fn(zv, w, b):
    v = _gelu(zv)
    mu = jnp.mean(v, axis=-1, keepdims=True)
    var = jnp.mean(jnp.square(v - mu), axis=-1, keepdims=True)
    return (v - mu) * lax.rsqrt(var + NORM_EPS) * w + b


def _sigmoid(x):
    return 1.0 / (1.0 + jnp.exp(-x))


def _row_spec(tm, n):
    return pl.BlockSpec((tm, n), lambda i: (i, 0))


def _vec_spec(n):
    return pl.BlockSpec((1, n), lambda i: (0, 0))


def _acc(ref, val):
    @pl.when(pl.program_id(0) == 0)
    def _():
        ref[...] = jnp.zeros_like(ref)
    ref[...] += val


def _normmod_fwd(x, nw, sc, sh, *, name, tm=512):
    S, Dm = x.shape
    tm = _tile(S, tm, 8)

    def body(x_ref, nw_ref, sc_ref, sh_ref, o_ref):
        o_ref[...] = _normmod_fn(x_ref[...], nw_ref[...], sc_ref[...], sh_ref[...]).astype(BF)
    return _pcall(body, name=name, grid=(S // tm,),
                  in_specs=[_row_spec(tm, Dm), _vec_spec(Dm), _vec_spec(Dm), _vec_spec(Dm)],
                  out_specs=_row_spec(tm, Dm), out_shape=jax.ShapeDtypeStruct((S, Dm), BF),
                  compiler_params=_params())(x, nw, sc, sh)


def _gate_bwd(dxv, o_ref, g_ref, do_ref, dg_ref):
    do_ref[...] = (dxv * g_ref[...]).astype(BF)
    _acc(dg_ref, jnp.sum(dxv * o_ref[...], axis=0, keepdims=True))


def _normmod_bwd(dh, x, nw, sc, sh, dres, gate, *, name, tm=256):
    S, Dm = x.shape
    tm = _tile(S, tm, 8)
    ng = 0 if gate is None else 2

    def body(dh_ref, x_ref, nw_ref, sc_ref, sh_ref, dres_ref, *rest):
        dx_ref, dnw_ref, dsc_ref, dsh_ref = rest[ng:ng + 4]
        _, vjp = jax.vjp(_normmod_fn, x_ref[...], nw_ref[...], sc_ref[...], sh_ref[...])
        dx, dnw, dsc, dsh = vjp(dh_ref[...])
        dxv = dres_ref[...] + dx
        dx_ref[...] = dxv
        _acc(dnw_ref, dnw)
        _acc(dsc_ref, dsc)
        _acc(dsh_ref, dsh)
        if gate is not None:
            _gate_bwd(dxv, rest[0], rest[1], rest[ng + 4], rest[ng + 5])
    vec = jax.ShapeDtypeStruct((1, Dm), F32)
    gate_in = [] if gate is None else [_row_spec(tm, Dm), _vec_spec(Dm)]
    gate_out = [] if gate is None else [_row_spec(tm, Dm), _vec_spec(Dm)]
    gate_shape = [] if gate is None else [jax.ShapeDtypeStruct((S, Dm), BF), vec]
    return _pcall(body, name=name, grid=(S // tm,),
                  in_specs=[_row_spec(tm, Dm), _row_spec(tm, Dm), _vec_spec(Dm), _vec_spec(Dm), _vec_spec(Dm),
                            _row_spec(tm, Dm)] + gate_in,
                  out_specs=[_row_spec(tm, Dm), _vec_spec(Dm), _vec_spec(Dm), _vec_spec(Dm)] + gate_out,
                  out_shape=[jax.ShapeDtypeStruct((S, Dm), F32), vec, vec, vec] + gate_shape,
                  compiler_params=_params())(dh, x, nw, sc, sh, dres, *([] if gate is None else gate))


def _head(x, fw, target, gate, *, tm=256):
    S, Dm = x.shape
    tm = _tile(S, tm, 8)

    def body(x_ref, fw_ref, t_ref, o_ref, g_ref, dx_ref, dfw_ref, loss_ref, do_ref, dg_ref):
        y, vjp = jax.vjp(_rms, x_ref[...], fw_ref[...])
        err = y - t_ref[...]
        dx, dfw = vjp(err * (1.0 / Dm))
        dx_ref[...] = dx
        _acc(dfw_ref, dfw)
        part = 0.5 * jnp.sum(jnp.mean(err * err, axis=-1, keepdims=True), axis=0, keepdims=True)
        _acc(loss_ref, jnp.broadcast_to(part, (8, 128)))
        _gate_bwd(dx, o_ref, g_ref, do_ref, dg_ref)
    vec = jax.ShapeDtypeStruct((1, Dm), F32)
    return _pcall(body, name="head", grid=(S // tm,),
                  in_specs=[_row_spec(tm, Dm), _vec_spec(Dm), _row_spec(tm, Dm), _row_spec(tm, Dm), _vec_spec(Dm)],
                  out_specs=[_row_spec(tm, Dm), _vec_spec(Dm), pl.BlockSpec((8, 128), lambda i: (0, 0)),
                             _row_spec(tm, Dm), _vec_spec(Dm)],
                  out_shape=[jax.ShapeDtypeStruct((S, Dm), F32), vec, jax.ShapeDtypeStruct((8, 128), F32),
                             jax.ShapeDtypeStruct((S, Dm), BF), vec],
                  compiler_params=_params())(x, fw, target, *gate)


def _tril_mask():
    r = lax.broadcasted_iota(jnp.int32, (SGU_CHUNK, SGU_CHUNK), 0)
    c = lax.broadcasted_iota(jnp.int32, (SGU_CHUNK, SGU_CHUNK), 1)
    return c <= r


def _sgu_fwd(proj, lnw, lnb, w, b_t, *, name, after=None, tm=256):
    S = proj.shape[0]
    tm = _tile(S, tm, SGU_CHUNK)
    extra = [] if after is None else [after]

    def body(zu_ref, zv_ref, lnw_ref, lnb_ref, w_ref, bt_ref, *rest):
        o_ref = rest[-1]
        u = _gelu(zu_ref[...].astype(F32))
        vn = _ln_gelu_fn(zv_ref[...].astype(F32), lnw_ref[...], lnb_ref[...]).astype(BF)
        mask = _tril_mask()
        for g in range(SGU_GROUPS):
            wm = jnp.where(mask, w_ref[g], 0.0).astype(BF)
            cols = slice(g * 128, (g + 1) * 128)
            for ci in range(tm // SGU_CHUNK):
                rows = slice(ci * SGU_CHUNK, (ci + 1) * SGU_CHUNK)
                f = jnp.dot(wm, vn[rows, cols], preferred_element_type=F32) + bt_ref[:, g:g + 1]
                o_ref[rows, cols] = (u[rows, cols] * f).astype(BF)
    return _pcall(body, name=name, grid=(S // tm,),
                  in_specs=[pl.BlockSpec((tm, SGU_WIDTH), lambda i: (i, 0)), pl.BlockSpec((tm, SGU_WIDTH), lambda i: (i, 1)),
                            _vec_spec(SGU_WIDTH), _vec_spec(SGU_WIDTH),
                            pl.BlockSpec((SGU_GROUPS, 128, 128), lambda i: (0, 0, 0)),
                            pl.BlockSpec((128, SGU_GROUPS), lambda i: (0, 0))] + [ANY] * len(extra),
                  out_specs=_row_spec(tm, SGU_WIDTH), out_shape=jax.ShapeDtypeStruct((S, SGU_WIDTH), BF),
                  compiler_params=_params())(proj, proj, lnw, lnb, w, b_t, *extra)


def _sgu_bwd(dy, proj, lnw, lnb, w, b_t, dproj, *, name, tm=256):
    S = proj.shape[0]
    tm = _tile(S, tm, SGU_CHUNK)

    def body(dy_ref, zu_ref, zv_ref, lnw_ref, lnb_ref, w_ref, bt_ref, _, dz_ref, dlnw_ref, dlnb_ref, dw_ref, dbt_ref,
             f_s, dvn_s):
        first = pl.program_id(0) == 0

        @pl.when(first)
        def _():
            dw_ref[...] = jnp.zeros_like(dw_ref)
            dbt_ref[...] = jnp.zeros_like(dbt_ref)
        u, vjp_u = jax.vjp(_gelu, zu_ref[...].astype(F32))
        vn, vjp_v = jax.vjp(_ln_gelu_fn, zv_ref[...].astype(F32), lnw_ref[...], lnb_ref[...])
        vn = vn.astype(BF)
        dy_v = dy_ref[...]
        df = (dy_v * u).astype(BF)
        mask = _tril_mask()
        for g in range(SGU_GROUPS):
            wm = jnp.where(mask, w_ref[g], 0.0).astype(BF)
            cols = slice(g * 128, (g + 1) * 128)
            dwg = jnp.zeros((128, 128), F32)
            dbg = jnp.zeros((128, 1), F32)
            for ci in range(tm // SGU_CHUNK):
                rows = slice(ci * SGU_CHUNK, (ci + 1) * SGU_CHUNK)
                vn_c = vn[rows, cols]
                df_c = df[rows, cols]
                f_s[rows, cols] = jnp.dot(wm, vn_c, preferred_element_type=F32) + bt_ref[:, g:g + 1]
                dvn_s[rows, cols] = lax.dot_general(wm, df_c, (((0,), (0,)), ((), ())), preferred_element_type=F32)
                dwg = dwg + lax.dot_general(df_c, vn_c, (((1,), (1,)), ((), ())), preferred_element_type=F32)
                dbg = dbg + jnp.sum((dy_v[rows, cols] * u[rows, cols]), axis=1, keepdims=True)
            dw_ref[g] += jnp.where(mask, dwg, 0.0)
            dbt_ref[:, g:g + 1] += dbg
        (dzu,) = vjp_u(dy_v * f_s[...])
        dzv, dlnw, dlnb = vjp_v(dvn_s[...])
        dz_ref[:, :SGU_WIDTH] = dzu.astype(BF)
        dz_ref[:, SGU_WIDTH:] = dzv.astype(BF)
        _acc(dlnw_ref, dlnw)
        _acc(dlnb_ref, dlnb)
    vec = jax.ShapeDtypeStruct((1, SGU_WIDTH), F32)
    return _pcall(body, name=name, grid=(S // tm,),
                  in_specs=[_row_spec(tm, SGU_WIDTH),
                            pl.BlockSpec((tm, SGU_WIDTH), lambda i: (i, 0)), pl.BlockSpec((tm, SGU_WIDTH), lambda i: (i, 1)),
                            _vec_spec(SGU_WIDTH), _vec_spec(SGU_WIDTH),
                            pl.BlockSpec((SGU_GROUPS, 128, 128), lambda i: (0, 0, 0)),
                            pl.BlockSpec((128, SGU_GROUPS), lambda i: (0, 0)), ANY],
                  out_specs=[pl.BlockSpec((tm, 2 * SGU_WIDTH), lambda i: (i, P_Z // (2 * SGU_WIDTH))),
                             _vec_spec(SGU_WIDTH), _vec_spec(SGU_WIDTH),
                             pl.BlockSpec((SGU_GROUPS, 128, 128), lambda i: (0, 0, 0)),
                             pl.BlockSpec((128, SGU_GROUPS), lambda i: (0, 0))],
                  out_shape=[jax.ShapeDtypeStruct(dproj.shape, BF), vec, vec,
                             jax.ShapeDtypeStruct((SGU_GROUPS, 128, 128), F32),
                             jax.ShapeDtypeStruct((128, SGU_GROUPS), F32)],
                  scratch_shapes=[pltpu.VMEM((tm, SGU_WIDTH), F32), pltpu.VMEM((tm, SGU_WIDTH), F32)],
                  input_output_aliases={7: 0},
                  compiler_params=_params())(dy, proj, proj, lnw, lnb, w, b_t, dproj)


def _merge_fwd(y_sgu, y_attn, pa, pb, proj, *, name, after=None, tm=1024, tn=512):
    S, Dm = y_sgu.shape
    tm = _tile(S, tm, 8)
    nj = Dm // tn
    extra = [] if after is None else [after]

    def body(ys_ref, ya_ref, pa_ref, pb_ref, ga_ref, gb_ref, *rest):
        a_ref, b_ref, m_ref = rest[-3:]
        a = jnp.dot(ys_ref[...], pa_ref[...], preferred_element_type=F32)
        b = jnp.dot(ya_ref[...], pb_ref[...], preferred_element_type=F32)
        a_ref[...] = a.astype(BF)
        b_ref[...] = b.astype(BF)
        m_ref[...] = (_sigmoid(ga_ref[...].astype(F32)) * a + _sigmoid(gb_ref[...].astype(F32)) * b).astype(BF)
    row = pl.BlockSpec((tm, Dm), lambda i, j: (i, 0))
    col = pl.BlockSpec((Dm, tn), lambda i, j: (0, j))
    out = pl.BlockSpec((tm, tn), lambda i, j: (i, j))
    sh = jax.ShapeDtypeStruct((S, Dm), BF)
    return _pcall(body, name=name, grid=(S // tm, nj),
                  in_specs=[row, row, col, col, pl.BlockSpec((tm, tn), lambda i, j: (i, P_G // tn + j)),
                            pl.BlockSpec((tm, tn), lambda i, j: (i, (P_G + Dm) // tn + j))] + [ANY] * len(extra),
                  out_specs=[out, out, out], out_shape=[sh, sh, sh],
                  compiler_params=_params())(y_sgu, y_attn, pa, pb, proj, proj, *extra)


def _merge_bwd(do, w_out, a, b, proj, *, name, after=None, tm=512):
    S, Dm = a.shape
    tm = _tile(S, tm, 8)
    ga_blk, gb_blk = P_G // Dm, P_G // Dm + 1
    extra = [] if after is None else [after]

    def body(do_ref, w_ref, a_ref, b_ref, ga_ref, gb_ref, *rest):
        da_ref, db_ref, dg_ref = rest[-3:]
        dmv = lax.dot_general(do_ref[...], w_ref[...], (((1,), (1,)), ((), ())), preferred_element_type=F32)
        sa = _sigmoid(ga_ref[...].astype(F32))
        sb = _sigmoid(gb_ref[...].astype(F32))
        da_ref[...] = (dmv * sa).astype(BF)
        db_ref[...] = (dmv * sb).astype(BF)
        dg_ref[:, :Dm] = (dmv * a_ref[...].astype(F32) * sa * (1.0 - sa)).astype(BF)
        dg_ref[:, Dm:] = (dmv * b_ref[...].astype(F32) * sb * (1.0 - sb)).astype(BF)
    return _pcall(body, name=name, grid=(S // tm,),
                  in_specs=[_row_spec(tm, Dm), pl.BlockSpec((Dm, Dm), lambda i: (0, 0)), _row_spec(tm, Dm), _row_spec(tm, Dm),
                            pl.BlockSpec((tm, Dm), lambda i: (i, ga_blk)), pl.BlockSpec((tm, Dm), lambda i: (i, gb_blk))]
                  + [ANY] * len(extra),
                  out_specs=[_row_spec(tm, Dm), _row_spec(tm, Dm), pl.BlockSpec((tm, 2 * Dm), lambda i: (i, P_G // (2 * Dm)))],
                  out_shape=[jax.ShapeDtypeStruct((S, Dm), BF), jax.ShapeDtypeStruct((S, Dm), BF),
                             jax.ShapeDtypeStruct((S, IN_COLS), BF)],
                  compiler_params=_params())(do, w_out, a, b, proj, proj, *extra)


def _shift_rows(a, halo, k, up):
    n = a.shape[0]
    r8 = lax.broadcasted_iota(jnp.int32, (8, a.shape[1]), 0)
    if not up:
        rolled = pltpu.roll(a, k, 0)
        patch = jnp.where(r8 < k, pltpu.roll(halo, k, 0), rolled[:8])
        return jnp.concatenate([patch, rolled[8:]], axis=0)
    rolled = pltpu.roll(a, n - k, 0)
    patch = jnp.where(r8 >= 8 - k, pltpu.roll(halo, 8 - k, 0), rolled[n - 8:])
    return jnp.concatenate([rolled[:n - 8], patch], axis=0)


def _conv_taps(a, halo):
    return _shift_rows(a, halo, 2, False), _shift_rows(a, halo, 1, False), a


HALO = 16


def _prev_halo_spec(tm, Fd):
    return pl.BlockSpec((HALO, Fd), lambda i: (jnp.maximum(i * (tm // HALO) - 1, 0), 0))


def _conv_fwd(a_ref, halo_ref, cw_ref, cb_ref):
    halo = jnp.where(pl.program_id(0) > 0, halo_ref[...].astype(F32)[HALO - 8:], 0.0)
    t0, t1, t2 = _conv_taps(a_ref[...].astype(F32), halo)
    return t0, t1, t2, cb_ref[...] + cw_ref[0:1, :] * t0 + cw_ref[1:2, :] * t1 + cw_ref[2:3, :] * t2


def _ffn_act_fwd(a, up, cw, cb, *, name, tm=256):
    S, Fd = a.shape
    tm = _tile(S, tm, HALO)

    def body(a_ref, up_ref, halo_ref, cw_ref, cb_ref, o_ref):
        _, _, _, ac = _conv_fwd(a_ref, halo_ref, cw_ref, cb_ref)
        o_ref[...] = (ac * _sigmoid(ac) * up_ref[...].astype(F32)).astype(BF)
    return _pcall(body, name=name, grid=(S // tm,),
                  in_specs=[_row_spec(tm, Fd), _row_spec(tm, Fd), _prev_halo_spec(tm, Fd),
                            pl.BlockSpec((3, Fd), lambda i: (0, 0)), _vec_spec(Fd)],
                  out_specs=_row_spec(tm, Fd), out_shape=jax.ShapeDtypeStruct((S, Fd), BF),
                  compiler_params=_params())(a, up, a, cw, cb)


def _ffn_act_bwd_a(dhf, a, up, cw, cb, *, name, tm=256):
    S, Fd = a.shape
    tm = _tile(S, tm, HALO)

    def body(dhf_ref, a_ref, up_ref, halo_ref, cw_ref, cb_ref, dac_ref, dup_ref, dcw_ref, dcb_ref):
        t0, t1, t2, ac = _conv_fwd(a_ref, halo_ref, cw_ref, cb_ref)
        s = _sigmoid(ac)
        dhf_v = dhf_ref[...].astype(F32)
        dup_ref[...] = (dhf_v * ac * s).astype(BF)
        dac = dhf_v * up_ref[...].astype(F32) * (s * (1.0 + ac * (1.0 - s)))
        dac_ref[...] = dac.astype(BF)
        _acc(dcb_ref, jnp.sum(dac, axis=0, keepdims=True))
        _acc(dcw_ref, jnp.concatenate([jnp.sum(dac * t0, axis=0, keepdims=True),
                                       jnp.sum(dac * t1, axis=0, keepdims=True),
                                       jnp.sum(dac * t2, axis=0, keepdims=True)], axis=0))
    return _pcall(body, name=name, grid=(S // tm,),
                  in_specs=[_row_spec(tm, Fd), _row_spec(tm, Fd), _row_spec(tm, Fd), _prev_halo_spec(tm, Fd),
                            pl.BlockSpec((3, Fd), lambda i: (0, 0)), _vec_spec(Fd)],
                  out_specs=[_row_spec(tm, Fd), _row_spec(tm, Fd), pl.BlockSpec((3, Fd), lambda i: (0, 0)), _vec_spec(Fd)],
                  out_shape=[jax.ShapeDtypeStruct((S, Fd), BF), jax.ShapeDtypeStruct((S, Fd), BF),
                             jax.ShapeDtypeStruct((3, Fd), F32), jax.ShapeDtypeStruct((1, Fd), F32)],
                  compiler_params=_params())(dhf, a, up, a, cw, cb)


def _ffn_act_bwd_b(dac, cw, *, name, tm=256):
    S, Fd = dac.shape
    tm = _tile(S, tm, HALO)
    last = S // tm - 1

    def body(d_ref, halo_ref, cw_ref, o_ref):
        halo = jnp.where(pl.program_id(0) < last, halo_ref[...].astype(F32)[:8], 0.0)
        d = d_ref[...].astype(F32)
        o_ref[...] = (cw_ref[2:3, :] * d + cw_ref[1:2, :] * _shift_rows(d, halo, 1, True)
                      + cw_ref[0:1, :] * _shift_rows(d, halo, 2, True)).astype(BF)
    return _pcall(body, name=name, grid=(S // tm,),
                  in_specs=[_row_spec(tm, Fd),
                            pl.BlockSpec((HALO, Fd), lambda i: (jnp.minimum((i + 1) * (tm // HALO), S // HALO - 1), 0)),
                            pl.BlockSpec((3, Fd), lambda i: (0, 0))],
                  out_specs=_row_spec(tm, Fd), out_shape=jax.ShapeDtypeStruct((S, Fd), BF),
                  compiler_params=_params())(dac, dac, cw)


def _rope_tables(pos_col, inv_row, m1_row, m2_row):
    S = pos_col.shape[0]
    tm = _tile(S, 512, 8)

    def body(p_ref, inv_ref, m1_ref, m2_ref, c_ref, s1_ref, s2_ref):
        ang = p_ref[...] * inv_ref[...]
        sn = jnp.sin(ang)
        c_ref[...] = jnp.cos(ang)
        s1_ref[...] = -sn * m1_ref[...]
        s2_ref[...] = sn * m2_ref[...]
    sh = jax.ShapeDtypeStruct((S, 128), F32)
    return _pcall(body, name="rope_tables", grid=(S // tm,),
                  in_specs=[pl.BlockSpec((tm, 1), lambda i: (i, 0)), _vec_spec(128), _vec_spec(128), _vec_spec(128)],
                  out_specs=[_row_spec(tm, 128)] * 3, out_shape=[sh, sh, sh], compiler_params=_params())(
                      pos_col, inv_row, m1_row, m2_row)


def _rope_apply(x, c, s1, s2):
    outs = []
    for j in range(x.shape[1] // 128):
        xj = x[:, j * 128:(j + 1) * 128]
        outs.append(xj * c + pltpu.roll(xj, 120, 1) * s1 + pltpu.roll(xj, 8, 1) * s2)
    return outs[0] if len(outs) == 1 else jnp.concatenate(outs, axis=1)


def _rope_apply_t(d, c, s1, s2):
    outs = []
    for j in range(d.shape[1] // 128):
        dj = d[:, j * 128:(j + 1) * 128]
        outs.append(dj * c + pltpu.roll(dj * s1, 8, 1) + pltpu.roll(dj * s2, 120, 1))
    return outs[0] if len(outs) == 1 else jnp.concatenate(outs, axis=1)


def _rope_fwd(proj, c, s1, s2, *, name, tm=512):
    S = proj.shape[0]
    tm = _tile(S, tm, 8)

    def body(q_ref, k_ref, v_ref, c_ref, s1_ref, s2_ref, qo_ref, ko_ref, vo_ref):
        cv, s1v, s2v = c_ref[...], s1_ref[...], s2_ref[...]
        qo_ref[...] = (_rope_apply(q_ref[...].astype(F32), cv, s1v, s2v) * (HEAD_DIM ** -0.5)).astype(BF)
        ko_ref[...] = _rope_apply(k_ref[...].astype(F32), cv, s1v, s2v).astype(BF)
        vo_ref[...] = v_ref[...].astype(BF)
    return _pcall(body, name=name, grid=(S // tm,),
                  in_specs=[pl.BlockSpec((tm, Q_END), lambda i: (i, P_Q // Q_END)),
                            pl.BlockSpec((tm, 128), lambda i: (i, P_K // 128)),
                            pl.BlockSpec((tm, 128), lambda i: (i, P_V // 128)),
                            _row_spec(tm, 128), _row_spec(tm, 128), _row_spec(tm, 128)],
                  out_specs=[_row_spec(tm, Q_END), _row_spec(tm, 128), _row_spec(tm, 128)],
                  out_shape=[jax.ShapeDtypeStruct((S, Q_END), BF), jax.ShapeDtypeStruct((S, 128), BF),
                             jax.ShapeDtypeStruct((S, 128), BF)],
                  compiler_params=_params())(proj, proj, proj, c, s1, s2)


def _rope_bwd(dq, dk, dv, c, s1, s2, dproj, *, name, tm=512):
    S = dq.shape[0]
    tm = _tile(S, tm, 8)
    tabs = [_row_spec(tm, 128)] * 3
    shape = jax.ShapeDtypeStruct(dproj.shape, BF)

    def body_q(dq_ref, c_ref, s1_ref, s2_ref, _, o_ref):
        o_ref[...] = _rope_apply_t(dq_ref[...].astype(F32), c_ref[...], s1_ref[...], s2_ref[...]).astype(BF)
    dproj = _pcall(body_q, name=name + "_q", grid=(S // tm,), in_specs=[_row_spec(tm, Q_END)] + tabs + [ANY],
                   out_specs=pl.BlockSpec((tm, Q_END), lambda i: (i, P_Q // Q_END)), out_shape=shape,
                   input_output_aliases={4: 0}, compiler_params=_params())(dq, c, s1, s2, dproj)

    def body_kv(dk_ref, dv_ref, c_ref, s1_ref, s2_ref, _, o_ref):
        o_ref[:, :128] = _rope_apply_t(dk_ref[...], c_ref[...], s1_ref[...], s2_ref[...]).astype(BF)
        o_ref[:, 128:] = dv_ref[...].astype(BF)
    return _pcall(body_kv, name=name + "_kv", grid=(S // tm,),
                  in_specs=[_row_spec(tm, 128), _row_spec(tm, 128)] + tabs + [ANY],
                  out_specs=pl.BlockSpec((tm, 256), lambda i: (i, P_K // 256)), out_shape=shape,
                  input_output_aliases={5: 0}, compiler_params=_params())(dk, dv, c, s1, s2, dproj)


def _lane_lo(shape):
    return lax.broadcasted_iota(jnp.int32, shape, 1) < HEAD_DIM


def _stack_heads(x, g):
    lo = _lane_lo((ATTN_BLOCK, 128))
    zero = jnp.zeros((ATTN_BLOCK, 128), x.dtype)
    parts = []
    for p in range(Q_PER_KV // 2):
        xp = x[:, (g * 4 + p) * 128:(g * 4 + p + 1) * 128]
        parts += [jnp.where(lo, xp, zero), jnp.where(lo, zero, xp)]
    return jnp.concatenate(parts, axis=0)


def _unstack_heads(o2):
    lo = _lane_lo((ATTN_BLOCK, 128))
    return [jnp.where(lo, o2[2 * p * ATTN_BLOCK:(2 * p + 1) * ATTN_BLOCK], o2[(2 * p + 1) * ATTN_BLOCK:(2 * p + 2) * ATTN_BLOCK])
            for p in range(Q_PER_KV // 2)]


def _dup_half(prev, cur, g):
    x = jnp.concatenate([prev, cur], axis=0).astype(F32)
    lo = _lane_lo(x.shape)
    r = pltpu.roll(x, HEAD_DIM, 1)
    return (jnp.where(lo, x, r) if g == 0 else jnp.where(lo, r, x)).astype(BF)


def _fold_halves(x):
    return x + pltpu.roll(x, HEAD_DIM, 1)


def _attn_bias():
    i = lax.broadcasted_iota(jnp.int32, (Q_PER_KV * ATTN_BLOCK, 2 * ATTN_BLOCK), 0) & (ATTN_BLOCK - 1)
    j = lax.broadcasted_iota(jnp.int32, (Q_PER_KV * ATTN_BLOCK, 2 * ATTN_BLOCK), 1)
    band = (j > i) & (j <= i + ATTN_BLOCK)
    return jnp.stack([jnp.where(band & (j >= ATTN_BLOCK), 0.0, -jnp.inf), jnp.where(band, 0.0, -jnp.inf)]).astype(F32)


def _attn_probs(qs, kb, sink, bias):
    s = lax.dot_general(qs, kb, (((1,), (1,)), ((), ())), preferred_element_type=F32) + bias
    m = jnp.maximum(jnp.max(s, axis=-1, keepdims=True), sink)
    p = jnp.exp(s - m)
    es = jnp.exp(sink - m)
    inv = 1.0 / (jnp.sum(p, axis=-1, keepdims=True) + es)
    return p, inv, es * inv


def _attn_specs(S):
    nb = S // ATTN_BLOCK
    qs = pl.BlockSpec((ATTN_BLOCK, Q_END), lambda n: (n, 0))
    cur = pl.BlockSpec((ATTN_BLOCK, 128), lambda n: (n, 0))
    prev = pl.BlockSpec((ATTN_BLOCK, 128), lambda n: (jnp.maximum(n - 1, 0), 0))
    sink = pl.BlockSpec((N_KV_HEADS, Q_PER_KV * ATTN_BLOCK, 1), lambda n: (0, 0, 0))
    bias = pl.BlockSpec((None, Q_PER_KV * ATTN_BLOCK, 2 * ATTN_BLOCK), lambda n: (jnp.minimum(n, 1), 0, 0))
    return nb, qs, cur, prev, sink, bias


def _attn_fwd(q, k, v, sink_rows, bias, *, name):
    S = q.shape[0]
    nb, qs, cur, prev, sink, bs = _attn_specs(S)

    def body(q_ref, kp_ref, kc_ref, vp_ref, vc_ref, sk_ref, b_ref, o_ref):
        for g in range(N_KV_HEADS):
            kb = _dup_half(kp_ref[...], kc_ref[...], g)
            vb = _dup_half(vp_ref[...], vc_ref[...], g)
            p, inv, _ = _attn_probs(_stack_heads(q_ref[...], g), kb, sk_ref[g], b_ref[...])
            o2 = jnp.dot(p.astype(BF), vb, preferred_element_type=F32) * inv
            for t, tile in enumerate(_unstack_heads(o2)):
                o_ref[:, (g * 4 + t) * 128:(g * 4 + t + 1) * 128] = tile.astype(BF)
    return _pcall(body, name=name, grid=(nb,), in_specs=[qs, prev, cur, prev, cur, sink, bs], out_specs=qs,
                  out_shape=jax.ShapeDtypeStruct(q.shape, BF), compiler_params=_params())(q, k, k, v, v, sink_rows, bias)


def _attn_bwd(do, q, k, v, sink_rows, bias, *, name):
    S = q.shape[0]
    nb, qs, cur, prev, sink, bs = _attn_specs(S)
    full = pl.BlockSpec((S, 128), lambda n: (0, 0))
    dsk_spec = pl.BlockSpec((N_KV_HEADS, Q_PER_KV, 128), lambda n: (0, 0, 0))

    def body(do_ref, q_ref, kp_ref, kc_ref, vp_ref, vc_ref, sk_ref, b_ref, dq_ref, dk_ref, dv_ref, dsk_ref):
        n = pl.program_id(0)

        @pl.when(n == 0)
        def _():
            dk_ref[...] = jnp.zeros_like(dk_ref)
            dv_ref[...] = jnp.zeros_like(dv_ref)
            dsk_ref[...] = jnp.zeros_like(dsk_ref)
        sub = lax.broadcasted_iota(jnp.int32, (Q_PER_KV, 128), 0)
        dkf, dvf = [], []
        for g in range(N_KV_HEADS):
            qst = _stack_heads(q_ref[...], g)
            dos = _stack_heads(do_ref[...], g)
            kb = _dup_half(kp_ref[...], kc_ref[...], g)
            vb = _dup_half(vp_ref[...], vc_ref[...], g)
            pu, inv, ps = _attn_probs(qst, kb, sk_ref[g], b_ref[...])
            p = pu * inv
            dp = lax.dot_general(dos, vb, (((1,), (1,)), ((), ())), preferred_element_type=F32)
            dd = jnp.sum(p * dp, axis=-1, keepdims=True)
            ds = (p * (dp - dd)).astype(BF)
            dq2 = jnp.dot(ds, kb, preferred_element_type=F32) * (HEAD_DIM ** -0.5)
            for t, tile in enumerate(_unstack_heads(dq2)):
                dq_ref[:, (g * 4 + t) * 128:(g * 4 + t + 1) * 128] = tile.astype(BF)
            dkf.append(_fold_halves(lax.dot_general(ds, qst, (((0,), (0,)), ((), ())), preferred_element_type=F32)))
            dvf.append(_fold_halves(lax.dot_general(p.astype(BF), dos, (((0,), (0,)), ((), ())),
                                                    preferred_element_type=F32)))
            dsr = -(ps * dd)
            upd = jnp.zeros((Q_PER_KV, 128), F32)
            for h in range(Q_PER_KV):
                upd = jnp.where(sub == h, jnp.sum(dsr[h * ATTN_BLOCK:(h + 1) * ATTN_BLOCK]), upd)
            dsk_ref[g] += upd
        lo = _lane_lo((2 * ATTN_BLOCK, 128))
        dkb = jnp.where(lo, dkf[0], dkf[1])
        dvb = jnp.where(lo, dvf[0], dvf[1])
        r0 = pl.multiple_of(n * ATTN_BLOCK, ATTN_BLOCK)
        dk_ref[pl.ds(r0, ATTN_BLOCK), :] += dkb[ATTN_BLOCK:]
        dv_ref[pl.ds(r0, ATTN_BLOCK), :] += dvb[ATTN_BLOCK:]

        @pl.when(n > 0)
        def _():
            rp = pl.multiple_of((n - 1) * ATTN_BLOCK, ATTN_BLOCK)
            dk_ref[pl.ds(rp, ATTN_BLOCK), :] += dkb[:ATTN_BLOCK]
            dv_ref[pl.ds(rp, ATTN_BLOCK), :] += dvb[:ATTN_BLOCK]
    return _pcall(body, name=name, grid=(nb,), in_specs=[qs, qs, prev, cur, prev, cur, sink, bs],
                  out_specs=[qs, full, full, dsk_spec],
                  out_shape=[jax.ShapeDtypeStruct(q.shape, BF), jax.ShapeDtypeStruct((S, 128), F32),
                             jax.ShapeDtypeStruct((S, 128), F32), jax.ShapeDtypeStruct((N_KV_HEADS, Q_PER_KV, 128), F32)],
                  compiler_params=_params())(do, q, k, k, v, v, sink_rows, bias)


def _ada_fwd(c_all, ada_w):
    ncol = ada_w.shape[2]

    def body(c_ref, w_ref, o_ref):
        cv = c_ref[...]
        ca = (cv * _sigmoid(cv)).astype(BF)
        for l in range(DEPTH):
            o_ref[:, l * ncol:(l + 1) * ncol] = jnp.dot(ca, w_ref[l].astype(BF), preferred_element_type=F32)
    return _pcall(body, name="ada_fwd", out_shape=jax.ShapeDtypeStruct((N_DEV, DEPTH * ncol), F32),
                  compiler_params=_params())(c_all, ada_w)


def _ada_bwd(c_all, dm):
    ncol = dm.shape[2]

    def body(c_ref, dm_ref, o_ref):
        cv = c_ref[...]
        ca = (cv * _sigmoid(cv)).astype(BF)
        for l in range(DEPTH):
            o_ref[l] = lax.dot_general(ca, dm_ref[l].astype(BF), (((0,), (0,)), ((), ())), preferred_element_type=F32)
    return _pcall(body, name="ada_bwd", out_shape=jax.ShapeDtypeStruct((DEPTH, D_MODEL, ncol), F32),
                  compiler_params=_params())(c_all, dm)


def _adamw(w, g, m, v, *, name):
    R, C = w.shape
    tr = R
    for t in range(8, 513, 8):
        if R % t == 0:
            tr = t
    c1 = 1.0 - ADAM_B1 ** ADAM_STEP
    c2 = 1.0 - ADAM_B2 ** ADAM_STEP

    def body(w_ref, g_ref, m_ref, v_ref, d_ref, mo_ref, vo_ref):
        gv = g_ref[...]
        mn = ADAM_B1 * m_ref[...] + (1.0 - ADAM_B1) * gv
        vn = ADAM_B2 * v_ref[...] + (1.0 - ADAM_B2) * (gv * gv)
        mo_ref[...] = mn
        vo_ref[...] = vn
        d_ref[...] = -ADAM_LR * ((mn / c1) / (jnp.sqrt(vn / c2) + ADAM_EPS) + ADAM_WD * w_ref[...])
    spec = pl.BlockSpec((tr, C), lambda i: (i, 0))
    sh = jax.ShapeDtypeStruct((R, C), F32)
    return _pcall(body, name=name, grid=(R // tr,), in_specs=[spec] * 4, out_specs=[spec] * 3, out_shape=[sh, sh, sh],
                  compiler_params=_params())(w, g, m, v)


def _adamw_layers(w, g_layers, m, v, *, name):
    L, R, C = w.shape
    assert L == 2 and len(g_layers) == 2
    tr = R
    for t in range(8, 513, 8):
        if R % t == 0:
            tr = t
    c1 = 1.0 - ADAM_B1 ** ADAM_STEP
    c2 = 1.0 - ADAM_B2 ** ADAM_STEP

    def body(w_ref, g0_ref, g1_ref, m_ref, v_ref, go_ref, d_ref, mo_ref, vo_ref):
        gv = jnp.where(pl.program_id(0) == 0, g0_ref[...], g1_ref[...])
        go_ref[...] = gv
        mn = ADAM_B1 * m_ref[...] + (1.0 - ADAM_B1) * gv
        vn = ADAM_B2 * v_ref[...] + (1.0 - ADAM_B2) * (gv * gv)
        mo_ref[...] = mn
        vo_ref[...] = vn
        d_ref[...] = -ADAM_LR * ((mn / c1) / (jnp.sqrt(vn / c2) + ADAM_EPS) + ADAM_WD * w_ref[...])
    spec = pl.BlockSpec((None, tr, C), lambda l, i: (l, i, 0))
    sh = jax.ShapeDtypeStruct((L, R, C), F32)
    g_specs = [pl.BlockSpec((tr, C), lambda l, i, k=k: (jnp.where(l == k, i, 0), 0)) for k in range(L)]
    return _pcall(body, name=name, grid=(L, R // tr), in_specs=[spec] + g_specs + [spec, spec], out_specs=[spec] * 4,
                  out_shape=[sh] * 4, compiler_params=_params())(w, *g_layers, m, v)


def _sum8(parts, *, name):
    _, R, C = parts.shape
    tr = _tile(R, 512, 16)

    def body(p_ref, o_ref):
        acc = p_ref[0].astype(F32)
        for k in range(1, N_DEV):
            acc = acc + p_ref[k].astype(F32)
        o_ref[...] = acc
    return _pcall(body, name=name, grid=(R // tr,), in_specs=[pl.BlockSpec((N_DEV, tr, C), lambda i: (0, i, 0))],
                  out_specs=pl.BlockSpec((tr, C), lambda i: (i, 0)), out_shape=jax.ShapeDtypeStruct((R, C), F32),
                  compiler_params=_params())(parts)


MESH_ID = pl.DeviceIdType.MESH
ANY = pl.BlockSpec(memory_space=pl.ANY)


def _all_gather(x, *, name, after=None):
    R, C = x.shape
    extra = [] if after is None else [after]

    def body(x_ref, *rest):
        out_ref, send_sems, recv_sems, local_sem = rest[-4:]
        mx, my, mc = lax.axis_index("x"), lax.axis_index("y"), lax.axis_index("c")
        me, sibling = (mx, my, mc), (mx, my, 1 - mc)
        chips = [(1 - mx, my), (mx, 1 - my), (1 - mx, 1 - my)]

        def blk(px, py, pc):
            return out_ref.at[4 * px + 2 * py + pc]

        def copy(k, block, to, src=None):
            return pltpu.make_async_remote_copy(
                src_ref=blk(*block) if src is None else src, dst_ref=blk(*block),
                send_sem=send_sems.at[k], recv_sem=recv_sems.at[k], device_id=to, device_id_type=MESH_ID)

        mine = pltpu.make_async_copy(x_ref, blk(*me), local_sem)
        mine.start()
        first = [copy(0, me, sibling, src=x_ref)]
        first += [copy(1 + j, me, (*chip, mc), src=x_ref) for j, chip in enumerate(chips)]
        for cp in first:
            cp.start()
        passed = [copy(4 + j, (*chip, mc), sibling) for j, chip in enumerate(chips)]
        for j, chip in enumerate(chips):
            copy(1 + j, (*chip, mc), me).wait_recv()
            passed[j].start()
        copy(0, sibling, me).wait_recv()
        for j, chip in enumerate(chips):
            copy(4 + j, (*chip, 1 - mc), me).wait_recv()
        for cp in first + passed:
            cp.wait_send()
        mine.wait()
    return _pcall(body, name=name, in_specs=[ANY] * (1 + len(extra)), out_specs=ANY,
                  out_shape=jax.ShapeDtypeStruct((N_DEV, R, C), x.dtype),
                  scratch_shapes=[pltpu.SemaphoreType.DMA((7,)), pltpu.SemaphoreType.DMA((7,)), pltpu.SemaphoreType.DMA],
                  compiler_params=pltpu.CompilerParams(has_side_effects=True))(x, *extra)


HBM_SPEC = pl.BlockSpec(memory_space=pltpu.HBM)
SEM_SPEC = pl.BlockSpec(memory_space=pltpu.SEMAPHORE)
DATAFLOW = pltpu.SideEffectType.DATAFLOW_SIDE_EFFECTING


def _coords():
    return lax.axis_index("x"), lax.axis_index("y"), lax.axis_index("c")


def _other_chips(mx, my):
    return [(1 - mx, my), (mx, 1 - my), (1 - mx, 1 - my)]


def _plan_gather_ici(refs, send, recv):
    src, land = refs
    mx, my, mc = _coords()
    return [pltpu.make_async_remote_copy(src_ref=src, dst_ref=land.at[mc, 2 * mx + my], send_sem=send[j], recv_sem=recv[j],
                                         device_id=(px, py, mc), device_id_type=MESH_ID)
            for j, (px, py) in enumerate(_other_chips(mx, my))]


def _plan_gather_d2d(refs, send, recv):
    (land,) = refs
    mx, my, mc = _coords()
    return [pltpu.make_async_remote_copy(src_ref=land.at[mc], dst_ref=land.at[mc], send_sem=send[0], recv_sem=recv[0],
                                         device_id=(mx, my, 1 - mc), device_id_type=MESH_ID)]


def _plan_reduce_d2d(refs, send, recv):
    g, land = refs
    mx, my, mc = _coords()
    return [pltpu.make_async_remote_copy(src_ref=g.at[1 - mc], dst_ref=land, send_sem=send[0], recv_sem=recv[0],
                                         device_id=(mx, my, 1 - mc), device_id_type=MESH_ID)]


def _plan_reduce_ici(refs, send, recv):
    h, land = refs
    mx, my, mc = _coords()
    return [pltpu.make_async_remote_copy(src_ref=h.at[2 * px + py], dst_ref=land.at[j], send_sem=send[j], recv_sem=recv[j],
                                         device_id=(px, py, mc), device_id_type=MESH_ID)
            for j, (px, py) in enumerate(_other_chips(mx, my))]


def _rdma_start(bufs, n, plan, *, name, after=None):
    nb = len(bufs)
    extra = [] if after is None else [after]
    ne = len(extra)

    def body(*refs):
        ins, send, recv = refs[:nb], refs[nb + ne:nb + ne + n], refs[nb + ne + n:nb + ne + 2 * n]
        token = refs[-1]
        for cp in plan(ins, send, recv):
            cp.start()
        token[...] = jnp.zeros_like(token)
    out = _pcall(body, name=name,
                 out_shape=tuple([pltpu.SemaphoreType.DMA(())] * (2 * n) + [pltpu.HBM(b.shape, b.dtype) for b in bufs]
                                 + [jax.ShapeDtypeStruct((8, 128), F32)]),
                 in_specs=tuple([HBM_SPEC] * nb + [ANY] * ne),
                 out_specs=tuple([SEM_SPEC] * (2 * n) + [HBM_SPEC] * nb + [pl.BlockSpec(memory_space=pltpu.VMEM)]),
                 input_output_aliases={i: 2 * n + i for i in range(nb)},
                 compiler_params=pltpu.CompilerParams(has_side_effects=DATAFLOW))(
                     *[pltpu.with_memory_space_constraint(b, pltpu.HBM) for b in bufs], *extra)
    return list(out[:2 * n]), list(out[2 * n:2 * n + nb]), out[-1]


def _rdma_wait(sems, bufs, n, plan, after, *, name):
    nb = len(bufs)

    def body(*refs):
        ins, send, recv = refs[:nb], refs[nb:nb + n], refs[nb + n:nb + 2 * n]
        for cp in plan(ins, send, recv):
            cp.wait_send()
            cp.wait_recv()
    out = _pcall(body, name=name, out_shape=tuple(pltpu.HBM(b.shape, b.dtype) for b in bufs),
                 in_specs=tuple([HBM_SPEC] * nb + [SEM_SPEC] * (2 * n) + [ANY]), out_specs=tuple([HBM_SPEC] * nb),
                 input_output_aliases={i: i for i in range(nb)},
                 compiler_params=pltpu.CompilerParams(has_side_effects=DATAFLOW))(*bufs, *sems, after)
    return list(out)


def _sum_pair(g, land, cidx, *, name):
    _, nchip, R, C = g.shape
    tr = _tile(R, 1056, 16)

    def body(c_ref, g_ref, l_ref, o_ref):
        o_ref[...] = (g_ref[...].astype(F32) + l_ref[...].astype(F32)).astype(BF)
    grid_spec = pltpu.PrefetchScalarGridSpec(
        num_scalar_prefetch=1, grid=(nchip, R // tr),
        in_specs=[pl.BlockSpec((None, None, tr, C), lambda p, i, c_ref: (c_ref[0], p, i, 0)),
                  pl.BlockSpec((None, tr, C), lambda p, i, c_ref: (p, i, 0))],
        out_specs=pl.BlockSpec((None, tr, C), lambda p, i, c_ref: (p, i, 0)))
    return _pcall(body, name=name, grid_spec=grid_spec, out_shape=jax.ShapeDtypeStruct((nchip, R, C), BF),
                  compiler_params=_params())(cidx, g, land)


def _sum_chips(h, land, chipidx, *, name):
    _, R, C = h.shape
    tr = _tile(R, 1056, 16)

    def body(c_ref, h_ref, l_ref, o_ref):
        acc = h_ref[...].astype(F32)
        for j in range(3):
            acc = acc + l_ref[j].astype(F32)
        o_ref[...] = acc
    grid_spec = pltpu.PrefetchScalarGridSpec(
        num_scalar_prefetch=1, grid=(R // tr,),
        in_specs=[pl.BlockSpec((None, tr, C), lambda i, c_ref: (c_ref[0], i, 0)),
                  pl.BlockSpec((3, tr, C), lambda i, c_ref: (0, i, 0))],
        out_specs=pl.BlockSpec((tr, C), lambda i, c_ref: (i, 0)))
    return _pcall(body, name=name, grid_spec=grid_spec, out_shape=jax.ShapeDtypeStruct((R, C), F32),
                  compiler_params=_params())(chipidx, h, land)


PART_IN = ("w_in",)
PART_MIX = ("proj_a", "proj_b", "w_out")
PART_FFN = ("ffn_w_gate", "ffn_w_up", "ffn_w_down")


def _part_rows(names):
    return sum(BIG_ROWS[n] for n in names)


def _part_offsets(names):
    off, r = {}, 0
    for n in names:
        off[n] = r
        r += BIG_ROWS[n]
    return off


def _pack_shards(shards, l, names):
    return jnp.concatenate([(shards[n][l].T if n in COL_SHARDED else shards[n][l]).astype(BF) for n in names], axis=0)


def _unpack_weights(full8, names):
    off = _part_offsets(names)

    def whole(n):
        return full8[:, off[n]:off[n] + BIG_ROWS[n], :].reshape(N_DEV * BIG_ROWS[n], 1024)
    out = {}
    if "w_in" in names:
        wt_in = whole("w_in")
        out["wt_in"] = jnp.concatenate([wt_in[V_END:], wt_in[:V_END]], axis=0)
    for n in ("proj_a", "proj_b", "w_out"):
        if n in names:
            out[n] = whole(n)
    if "ffn_w_gate" in names:
        out["wt_gate"], out["wt_up"], out["w_down"] = whole("ffn_w_gate"), whole("ffn_w_up"), whole("ffn_w_down")
    return out


def _from_land(land):
    return land.transpose(1, 0, 2, 3).reshape(N_DEV, land.shape[2], 1024)


def _pack_grads(wg, names):
    full = {"proj_a": wg.get("proj_a"), "proj_b": wg.get("proj_b"), "w_out": wg.get("w_out"), "ffn_w_down": wg.get("w_down"),
            "ffn_w_gate": wg.get("wt_gate"), "ffn_w_up": wg.get("wt_up")}
    if "w_in" in names:
        full["w_in"] = jnp.concatenate([wg["wt_in"][P_Q:], wg["wt_in"][:P_Q]], axis=0)
    blocks = jnp.concatenate([full[n].reshape(N_DEV, BIG_ROWS[n], 1024) for n in names], axis=1)
    return blocks.reshape(4, 2, _part_rows(names), 1024).transpose(1, 0, 2, 3)


def _unpack_shard_grads(gs, names):
    off = _part_offsets(names)
    out = {}
    for n in names:
        blk = gs[off[n]:off[n] + BIG_ROWS[n]]
        out[n] = blk.T if n in COL_SHARDED else blk
    return out


def _rope_setup(positions):
    S = positions.shape[0]
    inv = ROPE_THETA ** (-jnp.arange(0, ROT_DIM, 2, dtype=F32) / ROT_DIM)
    lane = np.arange(128) % HEAD_DIM
    half = ROT_DIM // 2
    inv_row = jnp.where(lane < ROT_DIM, jnp.tile(inv, 128 // half), 0.0)[None, :].astype(F32)
    m1_row = jnp.asarray((lane < half).astype(np.float32))[None, :]
    m2_row = jnp.asarray(((lane >= half) & (lane < ROT_DIM)).astype(np.float32))[None, :]
    return (*_rope_tables(positions.astype(F32).reshape(S, 1), inv_row, m1_row, m2_row), _attn_bias())


def _hook(hooks, point, after):
    f = None if hooks is None else hooks.get(point)
    return None if f is None else f(after)


def _layer_fwd(l, x, mod_l, W, small, rope, hooks=None):
    rc, rs1, rs2, bias = rope
    sh1, sc1, g1, sh2, sc2, g2 = [mod_l[i * D_MODEL:(i + 1) * D_MODEL][None, :] for i in range(6)]
    nw1, nw2 = small["norm1_w"][l][None, :], small["norm2_w"][l][None, :]
    h = _normmod_fwd(x, nw1, sc1, sh1, name=f"normmod1_fwd{l}")
    tok = _hook(hooks, "mm_in", h)
    proj = _mm(h, W["wt_in"], nt=True, out_dtype=BF, name=f"mm_in{l}", after=tok, tn_cap=768)
    q_r, k_r, v_b = _rope_fwd(proj, rc, rs1, rs2, name=f"rope_fwd{l}")
    sink_rows = jnp.repeat(small["attn_sinks"][l].reshape(N_KV_HEADS, Q_PER_KV), ATTN_BLOCK, axis=1)[..., None]
    y_attn = _attn_fwd(q_r, k_r, v_b, sink_rows, bias, name=f"attn_fwd{l}")
    lnw, lnb = small["sgu_ln_w"][l][None, :], small["sgu_ln_b"][l][None, :]
    sgu_bt = small["sgu_b"][l].T
    y_sgu = _sgu_fwd(proj, lnw, lnb, small["sgu_w"][l], sgu_bt, name=f"sgu_fwd{l}", after=_hook(hooks, "sgu", y_attn))
    tok = _hook(hooks, "mm_pa", y_sgu)
    a_br, b_br, merged = _merge_fwd(y_sgu, y_attn, W["proj_a"], W["proj_b"], proj, name=f"merge_fwd{l}", after=tok)
    x1, o1 = _mm(merged, W["w_out"], nt=False, out_dtype=F32, name=f"mm_out{l}", res=x, gvec=g1)
    h2 = _normmod_fwd(x1, nw2, sc2, sh2, name=f"normmod2_fwd{l}")
    a_g = _mm(h2, W["wt_gate"], nt=True, out_dtype=BF, name=f"mm_gate{l}", tn_cap=1408)
    a_u = _mm(h2, W["wt_up"], nt=True, out_dtype=BF, name=f"mm_up{l}", tn_cap=1408)
    cw, cb = small["ffn_conv_w"][l], small["ffn_conv_b"][l][None, :]
    hf = _ffn_act_fwd(a_g, a_u, cw, cb, name=f"ffn_act_fwd{l}")
    x2, o2 = _mm(hf, W["w_down"], nt=False, out_dtype=F32, name=f"mm_down{l}", res=x1, gvec=g2)
    saved = dict(x=x, h=h, proj=proj, q_r=q_r, k_r=k_r, v_b=v_b, sink_rows=sink_rows, y_attn=y_attn, y_sgu=y_sgu,
                 a_br=a_br, b_br=b_br, merged=merged, x1=x1, o1=o1, h2=h2, a_g=a_g, a_u=a_u, hf=hf, o2=o2)
    return x2, saved


def _layer_bwd(l, dx, do2, dg2, mod_l, W, small, rope, sv, below=None, hooks=None, wg=None):
    rc, rs1, rs2, bias = rope
    sh1, sc1, g1, sh2, sc2, g2 = [mod_l[i * D_MODEL:(i + 1) * D_MODEL][None, :] for i in range(6)]
    nw1, nw2 = small["norm1_w"][l][None, :], small["norm2_w"][l][None, :]
    cw, cb = small["ffn_conv_w"][l], small["ffn_conv_b"][l][None, :]
    lnw, lnb = small["sgu_ln_w"][l][None, :], small["sgu_ln_b"][l][None, :]
    sgu_bt = small["sgu_b"][l].T
    wg = {} if wg is None else wg
    dhf = _mm(do2, W["w_down"], nt=True, out_dtype=BF, name=f"mm_down_dx{l}", after=_hook(hooks, "mm_down_dx", do2),
              tn_cap=1408)
    wg["w_down"] = _mm_tn(sv["hf"], do2, name=f"mm_down_dw{l}")
    dac, dup, dcw, dcb = _ffn_act_bwd_a(dhf, sv["a_g"], sv["a_u"], cw, cb, name=f"ffn_act_bwd_a{l}")
    da = _ffn_act_bwd_b(dac, cw, name=f"ffn_act_bwd_b{l}")
    dh2 = _mm([da, dup], [W["wt_gate"], W["wt_up"]], nt=False, out_dtype=F32, name=f"mm_gu_dx{l}",
              after=_hook(hooks, "mm_gu_dx", da))
    wg["wt_gate"] = _mm_tn(da, sv["h2"], name=f"mm_gate_dw{l}")
    wg["wt_up"] = _mm_tn(dup, sv["h2"], name=f"mm_up_dw{l}")
    dx1, dnw2, dsc2, dsh2, do1, dg1 = _normmod_bwd(dh2, sv["x1"], nw2, sc2, sh2, dx, (sv["o1"], g1), name=f"normmod2_bwd{l}")
    d_a, d_b, dproj = _merge_bwd(do1, W["w_out"], sv["a_br"], sv["b_br"], sv["proj"], name=f"merge_bwd{l}",
                                 after=_hook(hooks, "merge_bwd", do1))
    wg["w_out"] = _mm_tn(sv["merged"], do1, name=f"mm_out_dw{l}")
    dysgu = _mm(d_a, W["proj_a"], nt=True, out_dtype=F32, name=f"mm_pa_dx{l}", after=_hook(hooks, "mm_pa_dx", d_a))
    dyattn = _mm(d_b, W["proj_b"], nt=True, out_dtype=BF, name=f"mm_pb_dx{l}")
    wg["proj_a"] = _mm_tn(sv["y_sgu"], d_a, name=f"mm_pa_dw{l}")
    wg["proj_b"] = _mm_tn(sv["y_attn"], d_b, name=f"mm_pb_dw{l}")
    dproj, dlnw, dlnb, dsguw, dsgubt = _sgu_bwd(dysgu, sv["proj"], lnw, lnb, small["sgu_w"][l], sgu_bt, dproj,
                                                name=f"sgu_bwd{l}")
    dq_r, dk_r, dv_b, dsk = _attn_bwd(dyattn, sv["q_r"], sv["k_r"], sv["v_b"], sv["sink_rows"], bias, name=f"attn_bwd{l}")
    dproj = _rope_bwd(dq_r, dk_r, dv_b, rc, rs1, rs2, dproj, name=f"rope_bwd{l}")
    wg["wt_in"] = _mm_tn(dproj, sv["h"], name=f"mm_in_dw{l}")
    dh = _mm(dproj, W["wt_in"], nt=False, out_dtype=F32, name=f"mm_in_dx{l}", after=_hook(hooks, "mm_in_dx", wg["wt_in"]))
    dx0, dnw1, dsc1, dsh1, *gate_below = _normmod_bwd(dh, sv["x"], nw1, sc1, sh1, dx1, below, name=f"normmod1_bwd{l}")
    dmod = jnp.concatenate([dsh1, dsc1, dg1, dsh2, dsc2, dg2], axis=1)[0]
    sg = {"norm1_w": dnw1[0], "norm2_w": dnw2[0], "attn_sinks": dsk[:, :, 0].reshape(N_Q_HEADS),
          "sgu_ln_w": dlnw[0], "sgu_ln_b": dlnb[0], "sgu_w": dsguw, "sgu_b": dsgubt.T,
          "ffn_conv_w": dcw, "ffn_conv_b": dcb[0]}
    return (dx0, *gate_below), wg, sg, dmod


SMALL = ("ada_b", "norm1_w", "attn_sinks", "sgu_ln_w", "sgu_ln_b", "sgu_w", "sgu_b", "norm2_w", "ffn_conv_b", "final_norm_w")
WEIGHT_ORDER = ("ada_w", "ada_b", "norm1_w", "w_in", "attn_sinks", "sgu_ln_w", "sgu_ln_b", "sgu_w", "sgu_b", "proj_a", "proj_b",
                "w_out", "norm2_w", "ffn_w_gate", "ffn_w_up", "ffn_conv_w", "ffn_conv_b", "ffn_w_down", "final_norm_w")


def _flat_pack(arrs, rows):
    flat = jnp.concatenate([a.reshape(-1) for a in arrs])
    return jnp.pad(flat, (0, rows * 1024 - flat.shape[0])).reshape(rows, 1024)


def _flat_unpack(buf, shapes):
    flat = buf.reshape(-1)
    out, o = [], 0
    for s in shapes:
        n = int(np.prod(s))
        out.append(flat[o:o + n].reshape(s))
        o += n
    return out


def _adam2d(w, g, m, v, *, name):
    shp = w.shape
    r2 = (int(np.prod(shp[:-1])), shp[-1]) if len(shp) > 1 else (1, shp[0])
    d, mn, vn = _adamw(w.reshape(r2), g.reshape(r2), m.reshape(r2), v.reshape(r2), name=name)
    return d.reshape(shp), mn.reshape(shp), vn.reshape(shp)


def kernel(x, c, positions, ada_w, ada_b, norm1_w, w_in, attn_sinks, sgu_ln_w, sgu_ln_b, sgu_w, sgu_b, proj_a, proj_b, w_out, norm2_w, ffn_w_gate, ffn_w_up, ffn_conv_w, ffn_conv_b, ffn_w_down, final_norm_w, loss_target, m_ada_w, m_ada_b, m_norm1_w, m_w_in, m_attn_sinks, m_sgu_ln_w, m_sgu_ln_b, m_sgu_w, m_sgu_b, m_proj_a, m_proj_b, m_w_out, m_norm2_w, m_ffn_w_gate, m_ffn_w_up, m_ffn_conv_w, m_ffn_conv_b, m_ffn_w_down, m_final_norm_w, v_ada_w, v_ada_b, v_norm1_w, v_w_in, v_attn_sinks, v_sgu_ln_w, v_sgu_ln_b, v_sgu_w, v_sgu_b, v_proj_a, v_proj_b, v_w_out, v_norm2_w, v_ffn_w_gate, v_ffn_w_up, v_ffn_conv_w, v_ffn_conv_b, v_ffn_w_down, v_final_norm_w):
    wts = dict(ada_w=ada_w, ada_b=ada_b, norm1_w=norm1_w, w_in=w_in, attn_sinks=attn_sinks, sgu_ln_w=sgu_ln_w,
               sgu_ln_b=sgu_ln_b, sgu_w=sgu_w, sgu_b=sgu_b, proj_a=proj_a, proj_b=proj_b, w_out=w_out, norm2_w=norm2_w,
               ffn_w_gate=ffn_w_gate, ffn_w_up=ffn_w_up, ffn_conv_w=ffn_conv_w, ffn_conv_b=ffn_conv_b,
               ffn_w_down=ffn_w_down, final_norm_w=final_norm_w)
    mom = dict(ada_w=m_ada_w, ada_b=m_ada_b, norm1_w=m_norm1_w, w_in=m_w_in, attn_sinks=m_attn_sinks, sgu_ln_w=m_sgu_ln_w,
               sgu_ln_b=m_sgu_ln_b, sgu_w=m_sgu_w, sgu_b=m_sgu_b, proj_a=m_proj_a, proj_b=m_proj_b, w_out=m_w_out,
               norm2_w=m_norm2_w, ffn_w_gate=m_ffn_w_gate, ffn_w_up=m_ffn_w_up, ffn_conv_w=m_ffn_conv_w,
               ffn_conv_b=m_ffn_conv_b, ffn_w_down=m_ffn_w_down, final_norm_w=m_final_norm_w)
    var = dict(ada_w=v_ada_w, ada_b=v_ada_b, norm1_w=v_norm1_w, w_in=v_w_in, attn_sinks=v_attn_sinks, sgu_ln_w=v_sgu_ln_w,
               sgu_ln_b=v_sgu_ln_b, sgu_w=v_sgu_w, sgu_b=v_sgu_b, proj_a=v_proj_a, proj_b=v_proj_b, w_out=v_w_out,
               norm2_w=v_norm2_w, ffn_w_gate=v_ffn_w_gate, ffn_w_up=v_ffn_w_up, ffn_conv_w=v_ffn_conv_w,
               ffn_conv_b=v_ffn_conv_b, ffn_w_down=v_ffn_w_down, final_norm_w=v_final_norm_w)
    me = 4 * lax.axis_index("x") + 2 * lax.axis_index("y") + lax.axis_index("c")
    ada_cols = ada_w.shape[2]

    c_all = _all_gather(jnp.broadcast_to(c, (8, D_MODEL)), name="ag_c")[:, 0, :]
    prod = _ada_fwd(c_all, ada_w)
    prod_all = _all_gather(prod, name="ag_mod")
    mine = lax.dynamic_index_in_dim(prod_all, me, axis=1, keepdims=False)
    mod = jnp.stack([mine[:, l * ada_cols:(l + 1) * ada_cols].reshape(-1) for l in range(DEPTH)]) + ada_b

    conv_cols = ffn_conv_w.shape[2]
    conv_all = _all_gather(_flat_pack([ffn_conv_w], 8), name="ag_conv", after=mod)
    conv_full = jnp.stack([a.reshape(DEPTH, 3, conv_cols) for a in
                           [conv_all[j].reshape(-1)[:DEPTH * 3 * conv_cols] for j in range(N_DEV)]], axis=2)
    conv_full = conv_full.reshape(DEPTH, 3, FFN_DIM)
    small = {n: wts[n] for n in SMALL}
    small["ffn_conv_w"] = conv_full

    mx, my, mc = _coords()
    cidx = jnp.reshape(mc, (1,)).astype(jnp.int32)
    chipidx = jnp.reshape(2 * mx + my, (1,)).astype(jnp.int32)
    rope = _rope_setup(positions[0])

    class Gather:
        def __init__(self, src, tag):
            self.tag, self.src = tag, src
            self.land = lax.dynamic_update_slice(lax.empty((2, 4) + src.shape, src.dtype), src[None, None],
                                                 (mc, 2 * mx + my, 0, 0))

        def ici_start(self, after):
            self.sems, (self.src, self.land), tok = _rdma_start([self.src, self.land], 3, _plan_gather_ici,
                                                                name=f"ag_{self.tag}_ici_start", after=after)
            return tok

        def ici_wait_d2d_start(self, after):
            _, land = _rdma_wait(self.sems, [self.src, self.land], 3, _plan_gather_ici, after, name=f"ag_{self.tag}_ici_wait")
            self.sems, (self.land,), tok = _rdma_start([land], 1, _plan_gather_d2d, name=f"ag_{self.tag}_d2d_start")
            return tok

        def d2d_wait(self, after):
            (land,) = _rdma_wait(self.sems, [self.land], 1, _plan_gather_d2d, after, name=f"ag_{self.tag}_d2d_wait")
            return _from_land(land)

    def weights_job(names, l, tag):
        job = Gather(_pack_shards(wts, l, names), tag)
        job.weights = lambda after: _unpack_weights(job.d2d_wait(after), names)
        return job

    W0 = _unpack_weights(_all_gather(_pack_shards(wts, 0, PART_IN), name="ag_w0_in", after=conv_all), PART_IN)
    W1 = {}
    g_mix0, g_ffn0 = weights_job(PART_MIX, 0, "w0_mix"), weights_job(PART_FFN, 0, "w0_ffn")
    g_all1 = weights_job(BIG, 1, "w1")
    toks = {}

    def rest0_to_sibling(after):
        toks["mix"] = g_mix0.ici_wait_d2d_start(after)
        toks["ffn"] = g_ffn0.ici_wait_d2d_start(toks["mix"])
        return toks["ffn"]

    def rest0_then_layer1(after):
        W0.update(g_mix0.weights(after))
        W0.update(g_ffn0.weights(W0["proj_a"]))
        return g_all1.ici_start(W0["w_down"])

    x1, sv0 = _layer_fwd(0, x[0], mod[0], W0, small, rope,
                         {"mm_in": lambda after: g_ffn0.ici_start(g_mix0.ici_start(W0["wt_in"])),
                          "sgu": rest0_to_sibling, "mm_pa": rest0_then_layer1})
    g_all1.ici_wait_d2d_start(x1)
    x2, sv1 = _layer_fwd(1, x1, mod[1], W1, small, rope, {"mm_in": lambda after: W1.update(g_all1.weights(after))})
    gate2 = [mod[l][5 * D_MODEL:][None, :] for l in range(DEPTH)]
    dx2, dfw, loss_tile, do2, dg2 = _head(x2, final_norm_w[None, :], loss_target[0], (sv1["o2"], gate2[1]))
    loss = lax.psum(loss_tile[0, 0], ("x", "y", "c"))

    class Reduce:
        def __init__(self, names, tag):
            self.names, self.tag, self.rows = names, tag, _part_rows(names)

        def d2d_start(self, wg, after=None):
            self.sems, self.bufs, tok = _rdma_start([_pack_grads(wg, self.names), lax.empty((4, self.rows, 1024), BF)], 1,
                                                    _plan_reduce_d2d, name=f"rs_{self.tag}_d2d_start", after=after)
            return tok

        def d2d_wait_ici_start(self, after):
            g_t, land_a = _rdma_wait(self.sems, self.bufs, 1, _plan_reduce_d2d, after, name=f"rs_{self.tag}_d2d_wait")
            h = _sum_pair(g_t, land_a, cidx, name=f"rs_{self.tag}_sum_pair")
            self.sems, self.bufs, tok = _rdma_start([h, lax.empty((3, self.rows, 1024), BF)], 3, _plan_reduce_ici,
                                                    name=f"rs_{self.tag}_ici_start")
            return tok

        def ici_wait(self, after):
            h_t, land_b = _rdma_wait(self.sems, self.bufs, 3, _plan_reduce_ici, after, name=f"rs_{self.tag}_ici_wait")
            return _unpack_shard_grads(_sum_chips(h_t, land_b, chipidx, name=f"rs_{self.tag}_sum_chips"), self.names)

    (dx1, do2, dg2), wg1, sg1, dmod1 = _layer_bwd(1, dx2, do2, dg2, mod[1], W1, small, rope, sv1, below=(sv0["o2"], gate2[0]))
    r_all1, r_ffn0, r_mix0, r_in0 = Reduce(BIG, "g1"), Reduce(PART_FFN, "g0_ffn"), Reduce(PART_MIX, "g0_mix"), Reduce(PART_IN, "g0_in")
    tok1 = r_all1.d2d_start(wg1)
    wg0, shard1 = {}, {}

    def layer1_done_then_ffn0(after):
        shard1.update(r_all1.ici_wait(after))
        return r_ffn0.d2d_wait_ici_start(shard1["w_in"])

    def mix0_and_in0(after):
        tok = r_in0.d2d_start(wg0, r_mix0.d2d_start(wg0, after))
        return r_in0.d2d_wait_ici_start(r_mix0.d2d_wait_ici_start(tok))

    (grad_x,), _, sg0, dmod0 = _layer_bwd(
        0, dx1, do2, dg2, mod[0], W0, small, rope, sv0, wg=wg0,
        hooks={"mm_down_dx": lambda after: tok1, "mm_gu_dx": r_all1.d2d_wait_ici_start,
               "merge_bwd": lambda after: r_ffn0.d2d_start(wg0, after), "mm_pa_dx": layer1_done_then_ffn0,
               "mm_in_dx": mix0_and_in0})
    sg = {n: jnp.stack([sg0[n], sg1[n]]) for n in sg0}
    sg["final_norm_w"] = dfw[0]
    dmod = jnp.stack([dmod0, dmod1])
    vec_names = [n for n in SMALL if n not in ("ada_b", "sgu_w")] + ["ffn_conv_w"]
    vec_shapes = [(DEPTH, 6 * D_MODEL)] + [sg[n].shape for n in vec_names]
    vec_rows = -(-sum(int(np.prod(s)) for s in vec_shapes) // 1024 // 16) * 16
    sgu_rows = sgu_w.size // 1024
    g_small = Gather(jnp.concatenate([_flat_pack([dmod] + [sg[n] for n in vec_names], vec_rows),
                                      sg["sgu_w"].reshape(sgu_rows, 1024)], axis=0).astype(BF), "small")
    tok = g_small.ici_start(grad_x)

    shard0 = r_ffn0.ici_wait(tok)
    shard0.update(r_mix0.ici_wait(shard0["ffn_w_down"]))
    shard0.update(r_in0.ici_wait(shard0["w_out"]))
    grads, delta, new_m, new_v = {}, {}, {}, {}
    for n in BIG:
        two = lambda a: a.reshape(DEPTH, -1, a.shape[-1])
        out = _adamw_layers(two(wts[n]), [shard0[n], shard1[n]], two(mom[n]), two(var[n]), name=f"adamw_{n}")
        grads[n], delta[n], new_m[n], new_v[n] = [o.reshape(wts[n].shape) for o in out]

    sm_all = g_small.d2d_wait(g_small.ici_wait_d2d_start(delta["ffn_w_gate"]))
    sm_sum = _sum8(sm_all, name="sum_small")
    vec_sum = _flat_unpack(sm_sum[:vec_rows], vec_shapes)
    grads["ada_b"] = vec_sum[0]
    for n, gsum in zip(vec_names, vec_sum[1:]):
        grads[n] = gsum
    grads["sgu_w"] = sm_sum[vec_rows:].reshape(sgu_w.shape)
    grads["ffn_conv_w"] = lax.dynamic_slice_in_dim(grads["ffn_conv_w"], me * conv_cols, conv_cols, axis=2)
    dmod_all = sm_all[:, :DEPTH * 6, :].astype(F32).reshape(N_DEV, DEPTH, 6 * D_MODEL)
    dm_mine = lax.dynamic_slice_in_dim(dmod_all, me * ada_cols, ada_cols, axis=2).transpose(1, 0, 2)
    dm_mine = jnp.pad(dm_mine, ((0, 0), (0, 8), (0, 0)))
    grads["ada_w"] = _ada_bwd(jnp.pad(c_all, ((0, 8), (0, 0))), dm_mine)

    packed_small = [n for n in SMALL if n != "sgu_w"]
    pshapes = [wts[n].shape for n in packed_small]
    prow = -(-sum(int(np.prod(s)) for s in pshapes) // 1024 // 8) * 8
    pk = lambda d: _flat_pack([d[n] for n in packed_small], prow)
    d_s, m_s, v_s = _adamw(pk(wts), pk(grads), pk(mom), pk(var), name="adamw_small")
    for n, dd, mm, vv in zip(packed_small, _flat_unpack(d_s, pshapes), _flat_unpack(m_s, pshapes), _flat_unpack(v_s, pshapes)):
        delta[n], new_m[n], new_v[n] = dd, mm, vv
    for n in WEIGHT_ORDER:
        if n not in delta:
            delta[n], new_m[n], new_v[n] = _adam2d(wts[n], grads[n], mom[n], var[n], name=f"adamw_{n}")
    return (loss, grad_x[None], *[grads[n] for n in WEIGHT_ORDER], *[delta[n] for n in WEIGHT_ORDER],
            *[new_m[n] for n in WEIGHT_ORDER], *[new_v[n] for n in WEIGHT_ORDER])
```

```python
import functools

import jax
import jax.numpy as jnp
import numpy as np
from jax import lax
from jax.experimental import pallas as pl
from jax.experimental.pallas import tpu as pltpu

F32 = jnp.float32
BF = jnp.bfloat16

N_DEV = 8
D_MODEL = 1024
DEPTH = 2
N_Q_HEADS = 16
N_KV_HEADS = 2
HEAD_DIM = 64
Q_PER_KV = N_Q_HEADS // N_KV_HEADS
ATTN_BLOCK = 128
ROPE_THETA = 500000.0
ROT_DIM = HEAD_DIM // 4
SGU_WIDTH = 1024
SGU_GROUPS = 8
SGU_CHUNK = 128
FFN_DIM = 2816
NORM_EPS = 1e-6
Q_END = N_Q_HEADS * HEAD_DIM
K_END = Q_END + N_KV_HEADS * HEAD_DIM
V_END = K_END + N_KV_HEADS * HEAD_DIM
Z_END = V_END + 2 * SGU_WIDTH
IN_COLS = Z_END + 2 * D_MODEL
P_Z, P_G, P_Q, P_K, P_V = 0, 2048, 4096, 5120, 5248

ADAM_LR = 0.001
ADAM_B1 = 0.9
ADAM_B2 = 0.999
ADAM_EPS = 1e-08
ADAM_WD = 0.01
ADAM_STEP = 10

VMEM_LIMIT_BYTES = 56 * 1024 * 1024

BIG = ("w_in", "proj_a", "proj_b", "w_out", "ffn_w_gate", "ffn_w_up", "ffn_w_down")
COL_SHARDED = ("w_in", "ffn_w_gate", "ffn_w_up")
BIG_SHAPE = {"w_in": (D_MODEL, IN_COLS), "proj_a": (SGU_WIDTH, D_MODEL), "proj_b": (Q_END, D_MODEL),
             "w_out": (D_MODEL, D_MODEL), "ffn_w_gate": (D_MODEL, FFN_DIM), "ffn_w_up": (D_MODEL, FFN_DIM),
             "ffn_w_down": (FFN_DIM, D_MODEL)}
BIG_ROWS = {n: BIG_SHAPE[n][0] * BIG_SHAPE[n][1] // N_DEV // 1024 for n in BIG}
LAYER_ROWS = sum(BIG_ROWS.values())


def _pcall(body, **kw):
    return pl.pallas_call(body, **kw)


def _params(**kw):
    return pltpu.CompilerParams(vmem_limit_bytes=VMEM_LIMIT_BYTES, **kw)


def _tile(n, cap, unit=128):
    if n <= cap:
        return n
    best = 0
    t = unit
    while t <= cap:
        if n % t == 0:
            best = t
        t += unit
    assert best, (n, cap, unit)
    return best


def _mm(a, b, *, nt, out_dtype, name, res=None, gvec=None, after=None, tm=None, tn_cap=1024):
    a_list = list(a) if isinstance(a, (list, tuple)) else [a]
    b_list = list(b) if isinstance(b, (list, tuple)) else [b]
    a, b = a_list[0], b_list[0]
    M, K = a.shape
    N = b.shape[0] if nt else b.shape[1]
    k_total = sum(x.shape[1] for x in a_list)
    tm = _tile(M, tm or (1024 if k_total <= 1024 else 512), 8)
    tn = _tile(N, tn_cap)
    dn = (((1,), (1,)), ((), ())) if nt else (((1,), (0,)), ((), ()))

    def b_spec_of(x):
        k = x.shape[1] if nt else x.shape[0]
        return pl.BlockSpec((tn, k), lambda i, j: (j, 0)) if nt else pl.BlockSpec((k, tn), lambda i, j: (0, j))
    b_spec = b_spec_of(b)
    o_spec = pl.BlockSpec((tm, tn), lambda i, j: (i, j))
    if res is None:
        extra = [] if after is None else [after]
        n = len(a_list)

        def body(*refs):
            o_ref = refs[-1]
            acc = None
            for a_ref, b_ref in zip(refs[:n], refs[n:2 * n]):
                d = lax.dot_general(a_ref[...].astype(BF), b_ref[...].astype(BF), dn, preferred_element_type=F32)
                acc = d if acc is None else acc + d
            o_ref[...] = acc.astype(out_dtype)
        return _pcall(body, name=name, grid=(M // tm, N // tn),
                      in_specs=[pl.BlockSpec((tm, x.shape[1]), lambda i, j: (i, 0)) for x in a_list]
                      + [b_spec_of(x) for x in b_list] + [ANY] * len(extra), out_specs=o_spec,
                      out_shape=jax.ShapeDtypeStruct((M, N), out_dtype), compiler_params=_params())(
                          *a_list, *b_list, *extra)

    def body_res(a_ref, b_ref, r_ref, g_ref, o_ref, acc_ref):
        acc = lax.dot_general(a_ref[...].astype(BF), b_ref[...].astype(BF), dn, preferred_element_type=F32)
        acc_ref[...] = acc
        o_ref[...] = r_ref[...] + g_ref[...] * acc
    return _pcall(body_res, name=name, grid=(M // tm, N // tn),
                  in_specs=[pl.BlockSpec((tm, K), lambda i, j: (i, 0)), b_spec, o_spec,
                            pl.BlockSpec((1, tn), lambda i, j: (0, j))],
                  out_specs=[o_spec, o_spec],
                  out_shape=[jax.ShapeDtypeStruct((M, N), F32), jax.ShapeDtypeStruct((M, N), F32)],
                  compiler_params=_params())(a, b, res, gvec)


def _mm_tn(a, b, *, name, out_dtype=BF, tk=1024, tm_cap=1408, tn_cap=1024):
    S, M = a.shape
    N = b.shape[1]
    tk = _tile(S, tk, 8)
    tm = _tile(M, tm_cap)
    tn = _tile(N, tn_cap)
    nk = S // tk

    def body(a_ref, b_ref, o_ref, acc_ref):
        k = pl.program_id(2)

        @pl.when(k == 0)
        def _():
            acc_ref[...] = jnp.zeros_like(acc_ref)
        acc_ref[...] += lax.dot_general(a_ref[...].astype(BF), b_ref[...].astype(BF), (((0,), (0,)), ((), ())),
                                        preferred_element_type=F32)

        @pl.when(k == nk - 1)
        def _():
            o_ref[...] = acc_ref[...].astype(out_dtype)
    return _pcall(body, name=name, grid=(M // tm, N // tn, nk),
                  in_specs=[pl.BlockSpec((tk, tm), lambda i, j, k: (k, i)),
                            pl.BlockSpec((tk, tn), lambda i, j, k: (k, j))],
                  out_specs=pl.BlockSpec((tm, tn), lambda i, j, k: (i, j)),
                  out_shape=jax.ShapeDtypeStruct((M, N), out_dtype), scratch_shapes=[pltpu.VMEM((tm, tn), F32)],
                  compiler_params=_params())(a, b)


def _rms(x, w):
    return x * lax.rsqrt(jnp.mean(x * x, axis=-1, keepdims=True) + NORM_EPS) * w


def _normmod_fn(x, nw, sc, sh):
    return _rms(x, nw) * (1.0 + sc) + sh


def _gelu(x):
    return 0.5 * x * (1.0 + lax.erf(x * (2.0 ** -0.5)))


def _ln_gelu_fn(zv, w, b):
    v = _gelu(zv)
    mu = jnp.mean(v, axis=-1, keepdims=True)
    var = jnp.mean(jnp.square(v - mu), axis=-1, keepdims=True)
    return (v - mu) * lax.rsqrt(var + NORM_EPS) * w + b


def _sigmoid(x):
    return 1.0 / (1.0 + jnp.exp(-x))


def _row_spec(tm, n):
    return pl.BlockSpec((tm, n), lambda i: (i, 0))


def _vec_spec(n):
    return pl.BlockSpec((1, n), lambda i: (0, 0))


def _acc(ref, val):
    @pl.when(pl.program_id(0) == 0)
    def _():
        ref[...] = jnp.zeros_like(ref)
    ref[...] += val


def _normmod_fwd(x, nw, sc, sh, *, name, tm=512):
    S, Dm = x.shape
    tm = _tile(S, tm, 8)

    def body(x_ref, nw_ref, sc_ref, sh_ref, o_ref):
        o_ref[...] = _normmod_fn(x_ref[...], nw_ref[...], sc_ref[...], sh_ref[...]).astype(BF)
    return _pcall(body, name=name, grid=(S // tm,),
                  in_specs=[_row_spec(tm, Dm), _vec_spec(Dm), _vec_spec(Dm), _vec_spec(Dm)],
                  out_specs=_row_spec(tm, Dm), out_shape=jax.ShapeDtypeStruct((S, Dm), BF),
                  compiler_params=_params())(x, nw, sc, sh)


def _gate_bwd(dxv, o_ref, g_ref, do_ref, dg_ref):
    do_ref[...] = (dxv * g_ref[...]).astype(BF)
    _acc(dg_ref, jnp.sum(dxv * o_ref[...], axis=0, keepdims=True))


def _normmod_bwd(dh, x, nw, sc, sh, dres, gate, *, name, tm=256):
    S, Dm = x.shape
    tm = _tile(S, tm, 8)
    ng = 0 if gate is None else 2

    def body(dh_ref, x_ref, nw_ref, sc_ref, sh_ref, dres_ref, *rest):
        dx_ref, dnw_ref, dsc_ref, dsh_ref = rest[ng:ng + 4]
        _, vjp = jax.vjp(_normmod_fn, x_ref[...], nw_ref[...], sc_ref[...], sh_ref[...])
        dx, dnw, dsc, dsh = vjp(dh_ref[...])
        dxv = dres_ref[...] + dx
        dx_ref[...] = dxv
        _acc(dnw_ref, dnw)
        _acc(dsc_ref, dsc)
        _acc(dsh_ref, dsh)
        if gate is not None:
            _gate_bwd(dxv, rest[0], rest[1], rest[ng + 4], rest[ng + 5])
    vec = jax.ShapeDtypeStruct((1, Dm), F32)
    gate_in = [] if gate is None else [_row_spec(tm, Dm), _vec_spec(Dm)]
    gate_out = [] if gate is None else [_row_spec(tm, Dm), _vec_spec(Dm)]
    gate_shape = [] if gate is None else [jax.ShapeDtypeStruct((S, Dm), BF), vec]
    return _pcall(body, name=name, grid=(S // tm,),
                  in_specs=[_row_spec(tm, Dm), _row_spec(tm, Dm), _vec_spec(Dm), _vec_spec(Dm), _vec_spec(Dm),
                            _row_spec(tm, Dm)] + gate_in,
                  out_specs=[_row_spec(tm, Dm), _vec_spec(Dm), _vec_spec(Dm), _vec_spec(Dm)] + gate_out,
                  out_shape=[jax.ShapeDtypeStruct((S, Dm), F32), vec, vec, vec] + gate_shape,
                  compiler_params=_params())(dh, x, nw, sc, sh, dres, *([] if gate is None else gate))


def _head(x, fw, target, gate, *, tm=256):
    S, Dm = x.shape
    tm = _tile(S, tm, 8)

    def body(x_ref, fw_ref, t_ref, o_ref, g_ref, dx_ref, dfw_ref, loss_ref, do_ref, dg_ref):
        y, vjp = jax.vjp(_rms, x_ref[...], fw_ref[...])
        err = y - t_ref[...]
        dx, dfw = vjp(err * (1.0 / Dm))
        dx_ref[...] = dx
        _acc(dfw_ref, dfw)
        part = 0.5 * jnp.sum(jnp.mean(err * err, axis=-1, keepdims=True), axis=0, keepdims=True)
        _acc(loss_ref, jnp.broadcast_to(part, (8, 128)))
        _gate_bwd(dx, o_ref, g_ref, do_ref, dg_ref)
    vec = jax.ShapeDtypeStruct((1, Dm), F32)
    return _pcall(body, name="head", grid=(S // tm,),
                  in_specs=[_row_spec(tm, Dm), _vec_spec(Dm), _row_spec(tm, Dm), _row_spec(tm, Dm), _vec_spec(Dm)],
                  out_specs=[_row_spec(tm, Dm), _vec_spec(Dm), pl.BlockSpec((8, 128), lambda i: (0, 0)),
                             _row_spec(tm, Dm), _vec_spec(Dm)],
                  out_shape=[jax.ShapeDtypeStruct((S, Dm), F32), vec, jax.ShapeDtypeStruct((8, 128), F32),
                             jax.ShapeDtypeStruct((S, Dm), BF), vec],
                  compiler_params=_params())(x, fw, target, *gate)


def _tril_mask():
    r = lax.broadcasted_iota(jnp.int32, (SGU_CHUNK, SGU_CHUNK), 0)
    c = lax.broadcasted_iota(jnp.int32, (SGU_CHUNK, SGU_CHUNK), 1)
    return c <= r


def _sgu_fwd(proj, lnw, lnb, w, b_t, *, name, after=None, tm=256):
    S = proj.shape[0]
    tm = _tile(S, tm, SGU_CHUNK)
    extra = [] if after is None else [after]

    def body(zu_ref, zv_ref, lnw_ref, lnb_ref, w_ref, bt_ref, *rest):
        o_ref = rest[-1]
        u = _gelu(zu_ref[...].astype(F32))
        vn = _ln_gelu_fn(zv_ref[...].astype(F32), lnw_ref[...], lnb_ref[...]).astype(BF)
        mask = _tril_mask()
        for g in range(SGU_GROUPS):
            wm = jnp.where(mask, w_ref[g], 0.0).astype(BF)
            cols = slice(g * 128, (g + 1) * 128)
            for ci in range(tm // SGU_CHUNK):
                rows = slice(ci * SGU_CHUNK, (ci + 1) * SGU_CHUNK)
                f = jnp.dot(wm, vn[rows, cols], preferred_element_type=F32) + bt_ref[:, g:g + 1]
                o_ref[rows, cols] = (u[rows, cols] * f).astype(BF)
    return _pcall(body, name=name, grid=(S // tm,),
                  in_specs=[pl.BlockSpec((tm, SGU_WIDTH), lambda i: (i, 0)), pl.BlockSpec((tm, SGU_WIDTH), lambda i: (i, 1)),
                            _vec_spec(SGU_WIDTH), _vec_spec(SGU_WIDTH),
                            pl.BlockSpec((SGU_GROUPS, 128, 128), lambda i: (0, 0, 0)),
                            pl.BlockSpec((128, SGU_GROUPS), lambda i: (0, 0))] + [ANY] * len(extra),
                  out_specs=_row_spec(tm, SGU_WIDTH), out_shape=jax.ShapeDtypeStruct((S, SGU_WIDTH), BF),
                  compiler_params=_params())(proj, proj, lnw, lnb, w, b_t, *extra)


def _sgu_bwd(dy, proj, lnw, lnb, w, b_t, dproj, *, name, tm=256):
    S = proj.shape[0]
    tm = _tile(S, tm, SGU_CHUNK)

    def body(dy_ref, zu_ref, zv_ref, lnw_ref, lnb_ref, w_ref, bt_ref, _, dz_ref, dlnw_ref, dlnb_ref, dw_ref, dbt_ref,
             f_s, dvn_s):
        first = pl.program_id(0) == 0

        @pl.when(first)
        def _():
            dw_ref[...] = jnp.zeros_like(dw_ref)
            dbt_ref[...] = jnp.zeros_like(dbt_ref)
        u, vjp_u = jax.vjp(_gelu, zu_ref[...].astype(F32))
        vn, vjp_v = jax.vjp(_ln_gelu_fn, zv_ref[...].astype(F32), lnw_ref[...], lnb_ref[...])
        vn = vn.astype(BF)
        dy_v = dy_ref[...]
        df = (dy_v * u).astype(BF)
        mask = _tril_mask()
        for g in range(SGU_GROUPS):
            wm = jnp.where(mask, w_ref[g], 0.0).astype(BF)
            cols = slice(g * 128, (g + 1) * 128)
            dwg = jnp.zeros((128, 128), F32)
            dbg = jnp.zeros((128, 1), F32)
            for ci in range(tm // SGU_CHUNK):
                rows = slice(ci * SGU_CHUNK, (ci + 1) * SGU_CHUNK)
                vn_c = vn[rows, cols]
                df_c = df[rows, cols]
                f_s[rows, cols] = jnp.dot(wm, vn_c, preferred_element_type=F32) + bt_ref[:, g:g + 1]
                dvn_s[rows, cols] = lax.dot_general(wm, df_c, (((0,), (0,)), ((), ())), preferred_element_type=F32)
                dwg = dwg + lax.dot_general(df_c, vn_c, (((1,), (1,)), ((), ())), preferred_element_type=F32)
                dbg = dbg + jnp.sum((dy_v[rows, cols] * u[rows, cols]), axis=1, keepdims=True)
            dw_ref[g] += jnp.where(mask, dwg, 0.0)
            dbt_ref[:, g:g + 1] += dbg
        (dzu,) = vjp_u(dy_v * f_s[...])
        dzv, dlnw, dlnb = vjp_v(dvn_s[...])
        dz_ref[:, :SGU_WIDTH] = dzu.astype(BF)
        dz_ref[:, SGU_WIDTH:] = dzv.astype(BF)
        _acc(dlnw_ref, dlnw)
        _acc(dlnb_ref, dlnb)
    vec = jax.ShapeDtypeStruct((1, SGU_WIDTH), F32)
    return _pcall(body, name=name, grid=(S // tm,),
                  in_specs=[_row_spec(tm, SGU_WIDTH),
                            pl.BlockSpec((tm, SGU_WIDTH), lambda i: (i, 0)), pl.BlockSpec((tm, SGU_WIDTH), lambda i: (i, 1)),
                            _vec_spec(SGU_WIDTH), _vec_spec(SGU_WIDTH),
                            pl.BlockSpec((SGU_GROUPS, 128, 128), lambda i: (0, 0, 0)),
                            pl.BlockSpec((128, SGU_GROUPS), lambda i: (0, 0)), ANY],
                  out_specs=[pl.BlockSpec((tm, 2 * SGU_WIDTH), lambda i: (i, P_Z // (2 * SGU_WIDTH))),
                             _vec_spec(SGU_WIDTH), _vec_spec(SGU_WIDTH),
                             pl.BlockSpec((SGU_GROUPS, 128, 128), lambda i: (0, 0, 0)),
                             pl.BlockSpec((128, SGU_GROUPS), lambda i: (0, 0))],
                  out_shape=[jax.ShapeDtypeStruct(dproj.shape, BF), vec, vec,
                             jax.ShapeDtypeStruct((SGU_GROUPS, 128, 128), F32),
                             jax.ShapeDtypeStruct((128, SGU_GROUPS), F32)],
                  scratch_shapes=[pltpu.VMEM((tm, SGU_WIDTH), F32), pltpu.VMEM((tm, SGU_WIDTH), F32)],
                  input_output_aliases={7: 0},
                  compiler_params=_params())(dy, proj, proj, lnw, lnb, w, b_t, dproj)


def _merge_fwd(y_sgu, y_attn, pa, pb, proj, *, name, after=None, tm=1024, tn=512):
    S, Dm = y_sgu.shape
    tm = _tile(S, tm, 8)
    nj = Dm // tn
    extra = [] if after is None else [after]

    def body(ys_ref, ya_ref, pa_ref, pb_ref, ga_ref, gb_ref, *rest):
        a_ref, b_ref, m_ref = rest[-3:]
        a = jnp.dot(ys_ref[...], pa_ref[...], preferred_element_type=F32)
        b = jnp.dot(ya_ref[...], pb_ref[...], preferred_element_type=F32)
        a_ref[...] = a.astype(BF)
        b_ref[...] = b.astype(BF)
        m_ref[...] = (_sigmoid(ga_ref[...].astype(F32)) * a + _sigmoid(gb_ref[...].astype(F32)) * b).astype(BF)
    row = pl.BlockSpec((tm, Dm), lambda i, j: (i, 0))
    col = pl.BlockSpec((Dm, tn), lambda i, j: (0, j))
    out = pl.BlockSpec((tm, tn), lambda i, j: (i, j))
    sh = jax.ShapeDtypeStruct((S, Dm), BF)
    return _pcall(body, name=name, grid=(S // tm, nj),
                  in_specs=[row, row, col, col, pl.BlockSpec((tm, tn), lambda i, j: (i, P_G // tn + j)),
                            pl.BlockSpec((tm, tn), lambda i, j: (i, (P_G + Dm) // tn + j))] + [ANY] * len(extra),
                  out_specs=[out, out, out], out_shape=[sh, sh, sh],
                  compiler_params=_params())(y_sgu, y_attn, pa, pb, proj, proj, *extra)


def _merge_bwd(do, w_out, a, b, proj, *, name, after=None, tm=512):
    S, Dm = a.shape
    tm = _tile(S, tm, 8)
    ga_blk, gb_blk = P_G // Dm, P_G // Dm + 1
    extra = [] if after is None else [after]

    def body(do_ref, w_ref, a_ref, b_ref, ga_ref, gb_ref, *rest):
        da_ref, db_ref, dg_ref = rest[-3:]
        dmv = lax.dot_general(do_ref[...], w_ref[...], (((1,), (1,)), ((), ())), preferred_element_type=F32)
        sa = _sigmoid(ga_ref[...].astype(F32))
        sb = _sigmoid(gb_ref[...].astype(F32))
        da_ref[...] = (dmv * sa).astype(BF)
        db_ref[...] = (dmv * sb).astype(BF)
        dg_ref[:, :Dm] = (dmv * a_ref[...].astype(F32) * sa * (1.0 - sa)).astype(BF)
        dg_ref[:, Dm:] = (dmv * b_ref[...].astype(F32) * sb * (1.0 - sb)).astype(BF)
    return _pcall(body, name=name, grid=(S // tm,),
                  in_specs=[_row_spec(tm, Dm), pl.BlockSpec((Dm, Dm), lambda i: (0, 0)), _row_spec(tm, Dm), _row_spec(tm, Dm),
                            pl.BlockSpec((tm, Dm), lambda i: (i, ga_blk)), pl.BlockSpec((tm, Dm), lambda i: (i, gb_blk))]
                  + [ANY] * len(extra),
                  out_specs=[_row_spec(tm, Dm), _row_spec(tm, Dm), pl.BlockSpec((tm, 2 * Dm), lambda i: (i, P_G // (2 * Dm)))],
                  out_shape=[jax.ShapeDtypeStruct((S, Dm), BF), jax.ShapeDtypeStruct((S, Dm), BF),
                             jax.ShapeDtypeStruct((S, IN_COLS), BF)],
                  compiler_params=_params())(do, w_out, a, b, proj, proj, *extra)


def _shift_rows(a, halo, k, up):
    n = a.shape[0]
    r8 = lax.broadcasted_iota(jnp.int32, (8, a.shape[1]), 0)
    if not up:
        rolled = pltpu.roll(a, k, 0)
        patch = jnp.where(r8 < k, pltpu.roll(halo, k, 0), rolled[:8])
        return jnp.concatenate([patch, rolled[8:]], axis=0)
    rolled = pltpu.roll(a, n - k, 0)
    patch = jnp.where(r8 >= 8 - k, pltpu.roll(halo, 8 - k, 0), rolled[n - 8:])
    return jnp.concatenate([rolled[:n - 8], patch], axis=0)


def _conv_taps(a, halo):
    return _shift_rows(a, halo, 2, False), _shift_rows(a, halo, 1, False), a


HALO = 16


def _prev_halo_spec(tm, Fd):
    return pl.BlockSpec((HALO, Fd), lambda i: (jnp.maximum(i * (tm // HALO) - 1, 0), 0))


def _conv_fwd(a_ref, halo_ref, cw_ref, cb_ref):
    halo = jnp.where(pl.program_id(0) > 0, halo_ref[...].astype(F32)[HALO - 8:], 0.0)
    t0, t1, t2 = _conv_taps(a_ref[...].astype(F32), halo)
    return t0, t1, t2, cb_ref[...] + cw_ref[0:1, :] * t0 + cw_ref[1:2, :] * t1 + cw_ref[2:3, :] * t2


def _ffn_act_fwd(a, up, cw, cb, *, name, tm=256):
    S, Fd = a.shape
    tm = _tile(S, tm, HALO)

    def body(a_ref, up_ref, halo_ref, cw_ref, cb_ref, o_ref):
        _, _, _, ac = _conv_fwd(a_ref, halo_ref, cw_ref, cb_ref)
        o_ref[...] = (ac * _sigmoid(ac) * up_ref[...].astype(F32)).astype(BF)
    return _pcall(body, name=name, grid=(S // tm,),
                  in_specs=[_row_spec(tm, Fd), _row_spec(tm, Fd), _prev_halo_spec(tm, Fd),
                            pl.BlockSpec((3, Fd), lambda i: (0, 0)), _vec_spec(Fd)],
                  out_specs=_row_spec(tm, Fd), out_shape=jax.ShapeDtypeStruct((S, Fd), BF),
                  compiler_params=_params())(a, up, a, cw, cb)


def _ffn_act_bwd_a(dhf, a, up, cw, cb, *, name, tm=256):
    S, Fd = a.shape
    tm = _tile(S, tm, HALO)

    def body(dhf_ref, a_ref, up_ref, halo_ref, cw_ref, cb_ref, dac_ref, dup_ref, dcw_ref, dcb_ref):
        t0, t1, t2, ac = _conv_fwd(a_ref, halo_ref, cw_ref, cb_ref)
        s = _sigmoid(ac)
        dhf_v = dhf_ref[...].astype(F32)
        dup_ref[...] = (dhf_v * ac * s).astype(BF)
        dac = dhf_v * up_ref[...].astype(F32) * (s * (1.0 + ac * (1.0 - s)))
        dac_ref[...] = dac.astype(BF)
        _acc(dcb_ref, jnp.sum(dac, axis=0, keepdims=True))
        _acc(dcw_ref, jnp.concatenate([jnp.sum(dac * t0, axis=0, keepdims=True),
                                       jnp.sum(dac * t1, axis=0, keepdims=True),
                                       jnp.sum(dac * t2, axis=0, keepdims=True)], axis=0))
    return _pcall(body, name=name, grid=(S // tm,),
                  in_specs=[_row_spec(tm, Fd), _row_spec(tm, Fd), _row_spec(tm, Fd), _prev_halo_spec(tm, Fd),
                            pl.BlockSpec((3, Fd), lambda i: (0, 0)), _vec_spec(Fd)],
                  out_specs=[_row_spec(tm, Fd), _row_spec(tm, Fd), pl.BlockSpec((3, Fd), lambda i: (0, 0)), _vec_spec(Fd)],
                  out_shape=[jax.ShapeDtypeStruct((S, Fd), BF), jax.ShapeDtypeStruct((S, Fd), BF),
                             jax.ShapeDtypeStruct((3, Fd), F32), jax.ShapeDtypeStruct((1, Fd), F32)],
                  compiler_params=_params())(dhf, a, up, a, cw, cb)


def _ffn_act_bwd_b(dac, cw, *, name, tm=256):
    S, Fd = dac.shape
    tm = _tile(S, tm, HALO)
    last = S // tm - 1

    def body(d_ref, halo_ref, cw_ref, o_ref):
        halo = jnp.where(pl.program_id(0) < last, halo_ref[...].astype(F32)[:8], 0.0)
        d = d_ref[...].astype(F32)
        o_ref[...] = (cw_ref[2:3, :] * d + cw_ref[1:2, :] * _shift_rows(d, halo, 1, True)
                      + cw_ref[0:1, :] * _shift_rows(d, halo, 2, True)).astype(BF)
    return _pcall(body, name=name, grid=(S // tm,),
                  in_specs=[_row_spec(tm, Fd),
                            pl.BlockSpec((HALO, Fd), lambda i: (jnp.minimum((i + 1) * (tm // HALO), S // HALO - 1), 0)),
                            pl.BlockSpec((3, Fd), lambda i: (0, 0))],
                  out_specs=_row_spec(tm, Fd), out_shape=jax.ShapeDtypeStruct((S, Fd), BF),
                  compiler_params=_params())(dac, dac, cw)


def _rope_tables(pos_col, inv_row, m1_row, m2_row):
    S = pos_col.shape[0]
    tm = _tile(S, 512, 8)

    def body(p_ref, inv_ref, m1_ref, m2_ref, c_ref, s1_ref, s2_ref):
        ang = p_ref[...] * inv_ref[...]
        sn = jnp.sin(ang)
        c_ref[...] = jnp.cos(ang)
        s1_ref[...] = -sn * m1_ref[...]
        s2_ref[...] = sn * m2_ref[...]
    sh = jax.ShapeDtypeStruct((S, 128), F32)
    return _pcall(body, name="rope_tables", grid=(S // tm,),
                  in_specs=[pl.BlockSpec((tm, 1), lambda i: (i, 0)), _vec_spec(128), _vec_spec(128), _vec_spec(128)],
                  out_specs=[_row_spec(tm, 128)] * 3, out_shape=[sh, sh, sh], compiler_params=_params())(
                      pos_col, inv_row, m1_row, m2_row)


def _rope_apply(x, c, s1, s2):
    outs = []
    for j in range(x.shape[1] // 128):
        xj = x[:, j * 128:(j + 1) * 128]
        outs.append(xj * c + pltpu.roll(xj, 120, 1) * s1 + pltpu.roll(xj, 8, 1) * s2)
    return outs[0] if len(outs) == 1 else jnp.concatenate(outs, axis=1)


def _rope_apply_t(d, c, s1, s2):
    outs = []
    for j in range(d.shape[1] // 128):
        dj = d[:, j * 128:(j + 1) * 128]
        outs.append(dj * c + pltpu.roll(dj * s1, 8, 1) + pltpu.roll(dj * s2, 120, 1))
    return outs[0] if len(outs) == 1 else jnp.concatenate(outs, axis=1)


def _rope_fwd(proj, c, s1, s2, *, name, tm=512):
    S = proj.shape[0]
    tm = _tile(S, tm, 8)

    def body(q_ref, k_ref, v_ref, c_ref, s1_ref, s2_ref, qo_ref, ko_ref, vo_ref):
        cv, s1v, s2v = c_ref[...], s1_ref[...], s2_ref[...]
        qo_ref[...] = (_rope_apply(q_ref[...].astype(F32), cv, s1v, s2v) * (HEAD_DIM ** -0.5)).astype(BF)
        ko_ref[...] = _rope_apply(k_ref[...].astype(F32), cv, s1v, s2v).astype(BF)
        vo_ref[...] = v_ref[...].astype(BF)
    return _pcall(body, name=name, grid=(S // tm,),
                  in_specs=[pl.BlockSpec((tm, Q_END), lambda i: (i, P_Q // Q_END)),
                            pl.BlockSpec((tm, 128), lambda i: (i, P_K // 128)),
                            pl.BlockSpec((tm, 128), lambda i: (i, P_V // 128)),
                            _row_spec(tm, 128), _row_spec(tm, 128), _row_spec(tm, 128)],
                  out_specs=[_row_spec(tm, Q_END), _row_spec(tm, 128), _row_spec(tm, 128)],
                  out_shape=[jax.ShapeDtypeStruct((S, Q_END), BF), jax.ShapeDtypeStruct((S, 128), BF),
                             jax.ShapeDtypeStruct((S, 128), BF)],
                  compiler_params=_params())(proj, proj, proj, c, s1, s2)


def _rope_bwd(dq, dk, dv, c, s1, s2, dproj, *, name, tm=512):
    S = dq.shape[0]
    tm = _tile(S, tm, 8)
    tabs = [_row_spec(tm, 128)] * 3
    shape = jax.ShapeDtypeStruct(dproj.shape, BF)

    def body_q(dq_ref, c_ref, s1_ref, s2_ref, _, o_ref):
        o_ref[...] = _rope_apply_t(dq_ref[...].astype(F32), c_ref[...], s1_ref[...], s2_ref[...]).astype(BF)
    dproj = _pcall(body_q, name=name + "_q", grid=(S // tm,), in_specs=[_row_spec(tm, Q_END)] + tabs + [ANY],
                   out_specs=pl.BlockSpec((tm, Q_END), lambda i: (i, P_Q // Q_END)), out_shape=shape,
                   input_output_aliases={4: 0}, compiler_params=_params())(dq, c, s1, s2, dproj)

    def body_kv(dk_ref, dv_ref, c_ref, s1_ref, s2_ref, _, o_ref):
        o_ref[:, :128] = _rope_apply_t(dk_ref[...], c_ref[...], s1_ref[...], s2_ref[...]).astype(BF)
        o_ref[:, 128:] = dv_ref[...].astype(BF)
    return _pcall(body_kv, name=name + "_kv", grid=(S // tm,),
                  in_specs=[_row_spec(tm, 128), _row_spec(tm, 128)] + tabs + [ANY],
                  out_specs=pl.BlockSpec((tm, 256), lambda i: (i, P_K // 256)), out_shape=shape,
                  input_output_aliases={5: 0}, compiler_params=_params())(dk, dv, c, s1, s2, dproj)


def _lane_lo(shape):
    return lax.broadcasted_iota(jnp.int32, shape, 1) < HEAD_DIM


def _stack_heads(x, g):
    lo = _lane_lo((ATTN_BLOCK, 128))
    zero = jnp.zeros((ATTN_BLOCK, 128), x.dtype)
    parts = []
    for p in range(Q_PER_KV // 2):
        xp = x[:, (g * 4 + p) * 128:(g * 4 + p + 1) * 128]
        parts += [jnp.where(lo, xp, zero), jnp.where(lo, zero, xp)]
    return jnp.concatenate(parts, axis=0)


def _unstack_heads(o2):
    lo = _lane_lo((ATTN_BLOCK, 128))
    return [jnp.where(lo, o2[2 * p * ATTN_BLOCK:(2 * p + 1) * ATTN_BLOCK], o2[(2 * p + 1) * ATTN_BLOCK:(2 * p + 2) * ATTN_BLOCK])
            for p in range(Q_PER_KV // 2)]


def _dup_half(prev, cur, g):
    x = jnp.concatenate([prev, cur], axis=0).astype(F32)
    lo = _lane_lo(x.shape)
    r = pltpu.roll(x, HEAD_DIM, 1)
    return (jnp.where(lo, x, r) if g == 0 else jnp.where(lo, r, x)).astype(BF)


def _fold_halves(x):
    return x + pltpu.roll(x, HEAD_DIM, 1)


def _attn_bias():
    i = lax.broadcasted_iota(jnp.int32, (Q_PER_KV * ATTN_BLOCK, 2 * ATTN_BLOCK), 0) & (ATTN_BLOCK - 1)
    j = lax.broadcasted_iota(jnp.int32, (Q_PER_KV * ATTN_BLOCK, 2 * ATTN_BLOCK), 1)
    band = (j > i) & (j <= i + ATTN_BLOCK)
    return jnp.stack([jnp.where(band & (j >= ATTN_BLOCK), 0.0, -jnp.inf), jnp.where(band, 0.0, -jnp.inf)]).astype(F32)


def _both(x):
    return jnp.concatenate([x, x], axis=1)


def _row_sums(x_bf):
    return jnp.dot(x_bf, jnp.ones((x_bf.shape[1], 128), BF), preferred_element_type=F32)


def _attn_probs(qs, kb, sink, bias):
    s = lax.dot_general(qs, kb, (((1,), (1,)), ((), ())), preferred_element_type=F32) + bias
    m = jnp.maximum(jnp.broadcast_to(jnp.max(s, axis=-1, keepdims=True), sink.shape), sink)
    return jnp.exp(s - _both(m)), jnp.exp(sink - m)


def _attn_specs(S):
    nb = S // ATTN_BLOCK
    qs = pl.BlockSpec((ATTN_BLOCK, Q_END), lambda n: (n, 0))
    cur = pl.BlockSpec((ATTN_BLOCK, 128), lambda n: (n, 0))
    prev = pl.BlockSpec((ATTN_BLOCK, 128), lambda n: (jnp.maximum(n - 1, 0), 0))
    sink = pl.BlockSpec((N_KV_HEADS, Q_PER_KV * ATTN_BLOCK, 128), lambda n: (0, 0, 0))
    bias = pl.BlockSpec((None, Q_PER_KV * ATTN_BLOCK, 2 * ATTN_BLOCK), lambda n: (jnp.minimum(n, 1), 0, 0))
    return nb, qs, cur, prev, sink, bias


def _attn_fwd(q, k, v, sink_rows, bias, *, name):
    S = q.shape[0]
    nb, qs, cur, prev, sink, bs = _attn_specs(S)

    def body(q_ref, kp_ref, kc_ref, vp_ref, vc_ref, sk_ref, b_ref, o_ref):
        for g in range(N_KV_HEADS):
            kb = _dup_half(kp_ref[...], kc_ref[...], g)
            vb = _dup_half(vp_ref[...], vc_ref[...], g)
            p, es = _attn_probs(_stack_heads(q_ref[...], g), kb, sk_ref[g], b_ref[...])
            ones = jnp.ones((2 * ATTN_BLOCK, 128), BF)
            o3 = jnp.dot(p.astype(BF), jnp.concatenate([vb, ones], axis=1), preferred_element_type=F32)
            o2 = o3[:, :128] / (o3[:, 128:] + es)
            for t, tile in enumerate(_unstack_heads(o2)):
                o_ref[:, (g * 4 + t) * 128:(g * 4 + t + 1) * 128] = tile.astype(BF)
    return _pcall(body, name=name, grid=(nb,), in_specs=[qs, prev, cur, prev, cur, sink, bs], out_specs=qs,
                  out_shape=jax.ShapeDtypeStruct(q.shape, BF), compiler_params=_params())(q, k, k, v, v, sink_rows, bias)


def _attn_bwd(do, q, k, v, sink_rows, bias, *, name):
    S = q.shape[0]
    nb, qs, cur, prev, sink, bs = _attn_specs(S)
    full = pl.BlockSpec((S, 128), lambda n: (0, 0))
    dsk_spec = pl.BlockSpec((N_KV_HEADS, Q_PER_KV, 128), lambda n: (0, 0, 0))

    def body(do_ref, q_ref, kp_ref, kc_ref, vp_ref, vc_ref, sk_ref, b_ref, dq_ref, dk_ref, dv_ref, dsk_ref):
        n = pl.program_id(0)

        @pl.when(n == 0)
        def _():
            dk_ref[...] = jnp.zeros_like(dk_ref)
            dv_ref[...] = jnp.zeros_like(dv_ref)
            dsk_ref[...] = jnp.zeros_like(dsk_ref)
        sub = lax.broadcasted_iota(jnp.int32, (Q_PER_KV, 128), 0)
        dkf, dvf = [], []
        for g in range(N_KV_HEADS):
            qst = _stack_heads(q_ref[...], g)
            dos = _stack_heads(do_ref[...], g)
            kb = _dup_half(kp_ref[...], kc_ref[...], g)
            vb = _dup_half(vp_ref[...], vc_ref[...], g)
            pu, es = _attn_probs(qst, kb, sk_ref[g], b_ref[...])
            inv = 1.0 / (_row_sums(pu.astype(BF)) + es)
            p = pu * _both(inv)
            dp = lax.dot_general(dos, vb, (((1,), (1,)), ((), ())), preferred_element_type=F32)
            dd = _row_sums((p * dp).astype(BF))
            ds = (p * (dp - _both(dd))).astype(BF)
            dq2 = jnp.dot(ds, kb, preferred_element_type=F32) * (HEAD_DIM ** -0.5)
            for t, tile in enumerate(_unstack_heads(dq2)):
                dq_ref[:, (g * 4 + t) * 128:(g * 4 + t + 1) * 128] = tile.astype(BF)
            dkf.append(_fold_halves(lax.dot_general(ds, qst, (((0,), (0,)), ((), ())), preferred_element_type=F32)))
            dvf.append(_fold_halves(lax.dot_general(p.astype(BF), dos, (((0,), (0,)), ((), ())),
                                                    preferred_element_type=F32)))
            dsr = -(es * inv * dd)
            upd = jnp.zeros((Q_PER_KV, 128), F32)
            for h in range(Q_PER_KV):
                upd = jnp.where(sub == h, jnp.sum(dsr[h * ATTN_BLOCK:(h + 1) * ATTN_BLOCK], axis=0, keepdims=True), upd)
            dsk_ref[g] += upd
        lo = _lane_lo((2 * ATTN_BLOCK, 128))
        dkb = jnp.where(lo, dkf[0], dkf[1])
        dvb = jnp.where(lo, dvf[0], dvf[1])
        r0 = pl.multiple_of(n * ATTN_BLOCK, ATTN_BLOCK)
        dk_ref[pl.ds(r0, ATTN_BLOCK), :] += dkb[ATTN_BLOCK:]
        dv_ref[pl.ds(r0, ATTN_BLOCK), :] += dvb[ATTN_BLOCK:]

        @pl.when(n > 0)
        def _():
            rp = pl.multiple_of((n - 1) * ATTN_BLOCK, ATTN_BLOCK)
            dk_ref[pl.ds(rp, ATTN_BLOCK), :] += dkb[:ATTN_BLOCK]
            dv_ref[pl.ds(rp, ATTN_BLOCK), :] += dvb[:ATTN_BLOCK]
    return _pcall(body, name=name, grid=(nb,), in_specs=[qs, qs, prev, cur, prev, cur, sink, bs],
                  out_specs=[qs, full, full, dsk_spec],
                  out_shape=[jax.ShapeDtypeStruct(q.shape, BF), jax.ShapeDtypeStruct((S, 128), F32),
                             jax.ShapeDtypeStruct((S, 128), F32), jax.ShapeDtypeStruct((N_KV_HEADS, Q_PER_KV, 128), F32)],
                  compiler_params=_params())(do, q, k, k, v, v, sink_rows, bias)


def _ada_fwd(c_all, ada_w):
    ncol = ada_w.shape[2]

    def body(c_ref, w_ref, o_ref):
        cv = c_ref[...]
        ca = (cv * _sigmoid(cv)).astype(BF)
        for l in range(DEPTH):
            o_ref[:, l * ncol:(l + 1) * ncol] = jnp.dot(ca, w_ref[l].astype(BF), preferred_element_type=F32)
    return _pcall(body, name="ada_fwd", out_shape=jax.ShapeDtypeStruct((N_DEV, DEPTH * ncol), F32),
                  compiler_params=_params())(c_all, ada_w)


def _ada_bwd(c_all, dm):
    ncol = dm.shape[2]

    def body(c_ref, dm_ref, o_ref):
        cv = c_ref[...]
        ca = (cv * _sigmoid(cv)).astype(BF)
        for l in range(DEPTH):
            o_ref[l] = lax.dot_general(ca, dm_ref[l].astype(BF), (((0,), (0,)), ((), ())), preferred_element_type=F32)
    return _pcall(body, name="ada_bwd", out_shape=jax.ShapeDtypeStruct((DEPTH, D_MODEL, ncol), F32),
                  compiler_params=_params())(c_all, dm)


def _adamw(w, g, m, v, *, name):
    R, C = w.shape
    tr = R
    for t in range(8, 513, 8):
        if R % t == 0:
            tr = t
    c1 = 1.0 - ADAM_B1 ** ADAM_STEP
    c2 = 1.0 - ADAM_B2 ** ADAM_STEP

    def body(w_ref, g_ref, m_ref, v_ref, d_ref, mo_ref, vo_ref):
        gv = g_ref[...]
        mn = ADAM_B1 * m_ref[...] + (1.0 - ADAM_B1) * gv
        vn = ADAM_B2 * v_ref[...] + (1.0 - ADAM_B2) * (gv * gv)
        mo_ref[...] = mn
        vo_ref[...] = vn
        d_ref[...] = -ADAM_LR * ((mn / c1) / (jnp.sqrt(vn / c2) + ADAM_EPS) + ADAM_WD * w_ref[...])
    spec = pl.BlockSpec((tr, C), lambda i: (i, 0))
    sh = jax.ShapeDtypeStruct((R, C), F32)
    return _pcall(body, name=name, grid=(R // tr,), in_specs=[spec] * 4, out_specs=[spec] * 3, out_shape=[sh, sh, sh],
                  compiler_params=_params())(w, g, m, v)


def _adamw_layers(w, g_layers, m, v, *, name):
    L, R, C = w.shape
    assert L == 2 and len(g_layers) == 2
    tr = R
    for t in range(8, 513, 8):
        if R % t == 0:
            tr = t
    c1 = 1.0 - ADAM_B1 ** ADAM_STEP
    c2 = 1.0 - ADAM_B2 ** ADAM_STEP

    def body(w_ref, g0_ref, g1_ref, m_ref, v_ref, go_ref, d_ref, mo_ref, vo_ref):
        gv = jnp.where(pl.program_id(0) == 0, g0_ref[...], g1_ref[...])
        go_ref[...] = gv
        mn = ADAM_B1 * m_ref[...] + (1.0 - ADAM_B1) * gv
        vn = ADAM_B2 * v_ref[...] + (1.0 - ADAM_B2) * (gv * gv)
        mo_ref[...] = mn
        vo_ref[...] = vn
        d_ref[...] = -ADAM_LR * ((mn / c1) / (jnp.sqrt(vn / c2) + ADAM_EPS) + ADAM_WD * w_ref[...])
    spec = pl.BlockSpec((None, tr, C), lambda l, i: (l, i, 0))
    sh = jax.ShapeDtypeStruct((L, R, C), F32)
    g_specs = [pl.BlockSpec((tr, C), lambda l, i, k=k: (jnp.where(l == k, i, 0), 0)) for k in range(L)]
    return _pcall(body, name=name, grid=(L, R // tr), in_specs=[spec] + g_specs + [spec, spec], out_specs=[spec] * 4,
                  out_shape=[sh] * 4, compiler_params=_params())(w, *g_layers, m, v)


def _sum8(parts, *, name):
    _, R, C = parts.shape
    tr = _tile(R, 512, 16)

    def body(p_ref, o_ref):
        acc = p_ref[0].astype(F32)
        for k in range(1, N_DEV):
            acc = acc + p_ref[k].astype(F32)
        o_ref[...] = acc
    return _pcall(body, name=name, grid=(R // tr,), in_specs=[pl.BlockSpec((N_DEV, tr, C), lambda i: (0, i, 0))],
                  out_specs=pl.BlockSpec((tr, C), lambda i: (i, 0)), out_shape=jax.ShapeDtypeStruct((R, C), F32),
                  compiler_params=_params())(parts)


MESH_ID = pl.DeviceIdType.MESH
ANY = pl.BlockSpec(memory_space=pl.ANY)


def _all_gather(x, *, name, after=None):
    R, C = x.shape
    extra = [] if after is None else [after]

    def body(x_ref, *rest):
        out_ref, send_sems, recv_sems, local_sem = rest[-4:]
        mx, my, mc = lax.axis_index("x"), lax.axis_index("y"), lax.axis_index("c")
        me, sibling = (mx, my, mc), (mx, my, 1 - mc)
        chips = [(1 - mx, my), (mx, 1 - my), (1 - mx, 1 - my)]

        def blk(px, py, pc):
            return out_ref.at[4 * px + 2 * py + pc]

        def copy(k, block, to, src=None):
            return pltpu.make_async_remote_copy(
                src_ref=blk(*block) if src is None else src, dst_ref=blk(*block),
                send_sem=send_sems.at[k], recv_sem=recv_sems.at[k], device_id=to, device_id_type=MESH_ID)

        mine = pltpu.make_async_copy(x_ref, blk(*me), local_sem)
        mine.start()
        first = [copy(0, me, sibling, src=x_ref)]
        first += [copy(1 + j, me, (*chip, mc), src=x_ref) for j, chip in enumerate(chips)]
        for cp in first:
            cp.start()
        passed = [copy(4 + j, (*chip, mc), sibling) for j, chip in enumerate(chips)]
        for j, chip in enumerate(chips):
            copy(1 + j, (*chip, mc), me).wait_recv()
            passed[j].start()
        copy(0, sibling, me).wait_recv()
        for j, chip in enumerate(chips):
            copy(4 + j, (*chip, 1 - mc), me).wait_recv()
        for cp in first + passed:
            cp.wait_send()
        mine.wait()
    return _pcall(body, name=name, in_specs=[ANY] * (1 + len(extra)), out_specs=ANY,
                  out_shape=jax.ShapeDtypeStruct((N_DEV, R, C), x.dtype),
                  scratch_shapes=[pltpu.SemaphoreType.DMA((7,)), pltpu.SemaphoreType.DMA((7,)), pltpu.SemaphoreType.DMA],
                  compiler_params=pltpu.CompilerParams(has_side_effects=True))(x, *extra)


HBM_SPEC = pl.BlockSpec(memory_space=pltpu.HBM)
SEM_SPEC = pl.BlockSpec(memory_space=pltpu.SEMAPHORE)
DATAFLOW = pltpu.SideEffectType.DATAFLOW_SIDE_EFFECTING


def _coords():
    return lax.axis_index("x"), lax.axis_index("y"), lax.axis_index("c")


def _other_chips(mx, my):
    return [(1 - mx, my), (mx, 1 - my), (1 - mx, 1 - my)]


def _plan_gather_ici(refs, send, recv):
    src, land = refs
    mx, my, mc = _coords()
    return [pltpu.make_async_remote_copy(src_ref=src, dst_ref=land.at[mc, 2 * mx + my], send_sem=send[j], recv_sem=recv[j],
                                         device_id=(px, py, mc), device_id_type=MESH_ID)
            for j, (px, py) in enumerate(_other_chips(mx, my))]


def _plan_gather_d2d(refs, send, recv):
    (land,) = refs
    mx, my, mc = _coords()
    return [pltpu.make_async_remote_copy(src_ref=land.at[mc], dst_ref=land.at[mc], send_sem=send[0], recv_sem=recv[0],
                                         device_id=(mx, my, 1 - mc), device_id_type=MESH_ID)]


def _plan_reduce_d2d(refs, send, recv):
    g, land = refs
    mx, my, mc = _coords()
    return [pltpu.make_async_remote_copy(src_ref=g.at[1 - mc], dst_ref=land, send_sem=send[0], recv_sem=recv[0],
                                         device_id=(mx, my, 1 - mc), device_id_type=MESH_ID)]


def _plan_reduce_ici(refs, send, recv):
    h, land = refs
    mx, my, mc = _coords()
    return [pltpu.make_async_remote_copy(src_ref=h.at[2 * px + py], dst_ref=land.at[j], send_sem=send[j], recv_sem=recv[j],
                                         device_id=(px, py, mc), device_id_type=MESH_ID)
            for j, (px, py) in enumerate(_other_chips(mx, my))]


def _rdma_start(bufs, n, plan, *, name, after=None):
    nb = len(bufs)
    extra = [] if after is None else [after]
    ne = len(extra)

    def body(*refs):
        ins, send, recv = refs[:nb], refs[nb + ne:nb + ne + n], refs[nb + ne + n:nb + ne + 2 * n]
        token = refs[-1]
        for cp in plan(ins, send, recv):
            cp.start()
        token[...] = jnp.zeros_like(token)
    out = _pcall(body, name=name,
                 out_shape=tuple([pltpu.SemaphoreType.DMA(())] * (2 * n) + [pltpu.HBM(b.shape, b.dtype) for b in bufs]
                                 + [jax.ShapeDtypeStruct((8, 128), F32)]),
                 in_specs=tuple([HBM_SPEC] * nb + [ANY] * ne),
                 out_specs=tuple([SEM_SPEC] * (2 * n) + [HBM_SPEC] * nb + [pl.BlockSpec(memory_space=pltpu.VMEM)]),
                 input_output_aliases={i: 2 * n + i for i in range(nb)},
                 compiler_params=pltpu.CompilerParams(has_side_effects=DATAFLOW))(
                     *[pltpu.with_memory_space_constraint(b, pltpu.HBM) for b in bufs], *extra)
    return list(out[:2 * n]), list(out[2 * n:2 * n + nb]), out[-1]


def _rdma_wait(sems, bufs, n, plan, after, *, name):
    nb = len(bufs)

    def body(*refs):
        ins, send, recv = refs[:nb], refs[nb:nb + n], refs[nb + n:nb + 2 * n]
        for cp in plan(ins, send, recv):
            cp.wait_send()
            cp.wait_recv()
    out = _pcall(body, name=name, out_shape=tuple(pltpu.HBM(b.shape, b.dtype) for b in bufs),
                 in_specs=tuple([HBM_SPEC] * nb + [SEM_SPEC] * (2 * n) + [ANY]), out_specs=tuple([HBM_SPEC] * nb),
                 input_output_aliases={i: i for i in range(nb)},
                 compiler_params=pltpu.CompilerParams(has_side_effects=DATAFLOW))(*bufs, *sems, after)
    return list(out)


def _sum_pair(g, land, cidx, *, name):
    _, nchip, R, C = g.shape
    tr = _tile(R, 1056, 16)

    def body(c_ref, g_ref, l_ref, o_ref):
        o_ref[...] = (g_ref[...].astype(F32) + l_ref[...].astype(F32)).astype(BF)
    grid_spec = pltpu.PrefetchScalarGridSpec(
        num_scalar_prefetch=1, grid=(nchip, R // tr),
        in_specs=[pl.BlockSpec((None, None, tr, C), lambda p, i, c_ref: (c_ref[0], p, i, 0)),
                  pl.BlockSpec((None, tr, C), lambda p, i, c_ref: (p, i, 0))],
        out_specs=pl.BlockSpec((None, tr, C), lambda p, i, c_ref: (p, i, 0)))
    return _pcall(body, name=name, grid_spec=grid_spec, out_shape=jax.ShapeDtypeStruct((nchip, R, C), BF),
                  compiler_params=_params())(cidx, g, land)


def _sum_chips(h, land, chipidx, *, name):
    _, R, C = h.shape
    tr = _tile(R, 1056, 16)

    def body(c_ref, h_ref, l_ref, o_ref):
        acc = h_ref[...].astype(F32)
        for j in range(3):
            acc = acc + l_ref[j].astype(F32)
        o_ref[...] = acc
    grid_spec = pltpu.PrefetchScalarGridSpec(
        num_scalar_prefetch=1, grid=(R // tr,),
        in_specs=[pl.BlockSpec((None, tr, C), lambda i, c_ref: (c_ref[0], i, 0)),
                  pl.BlockSpec((3, tr, C), lambda i, c_ref: (0, i, 0))],
        out_specs=pl.BlockSpec((tr, C), lambda i, c_ref: (i, 0)))
    return _pcall(body, name=name, grid_spec=grid_spec, out_shape=jax.ShapeDtypeStruct((R, C), F32),
                  compiler_params=_params())(chipidx, h, land)


PART_IN = ("w_in",)
PART_MIX = ("proj_a", "proj_b", "w_out")
PART_FFN = ("ffn_w_gate", "ffn_w_up", "ffn_w_down")


def _part_rows(names):
    return sum(BIG_ROWS[n] for n in names)


def _part_offsets(names):
    off, r = {}, 0
    for n in names:
        off[n] = r
        r += BIG_ROWS[n]
    return off


def _pack_shards(shards, l, names):
    return jnp.concatenate([(shards[n][l].T if n in COL_SHARDED else shards[n][l]).astype(BF) for n in names], axis=0)


def _unpack_weights(full8, names):
    off = _part_offsets(names)

    def whole(n):
        return full8[:, off[n]:off[n] + BIG_ROWS[n], :].reshape(N_DEV * BIG_ROWS[n], 1024)
    out = {}
    if "w_in" in names:
        wt_in = whole("w_in")
        out["wt_in"] = jnp.concatenate([wt_in[V_END:], wt_in[:V_END]], axis=0)
    for n in ("proj_a", "proj_b", "w_out"):
        if n in names:
            out[n] = whole(n)
    if "ffn_w_gate" in names:
        out["wt_gate"], out["wt_up"], out["w_down"] = whole("ffn_w_gate"), whole("ffn_w_up"), whole("ffn_w_down")
    return out


def _from_land(land):
    return land.transpose(1, 0, 2, 3).reshape(N_DEV, land.shape[2], 1024)


def _pack_grads(wg, names):
    full = {"proj_a": wg.get("proj_a"), "proj_b": wg.get("proj_b"), "w_out": wg.get("w_out"), "ffn_w_down": wg.get("w_down"),
            "ffn_w_gate": wg.get("wt_gate"), "ffn_w_up": wg.get("wt_up")}
    if "w_in" in names:
        full["w_in"] = jnp.concatenate([wg["wt_in"][P_Q:], wg["wt_in"][:P_Q]], axis=0)
    blocks = jnp.concatenate([full[n].reshape(N_DEV, BIG_ROWS[n], 1024) for n in names], axis=1)
    return blocks.reshape(4, 2, _part_rows(names), 1024).transpose(1, 0, 2, 3)


def _unpack_shard_grads(gs, names):
    off = _part_offsets(names)
    out = {}
    for n in names:
        blk = gs[off[n]:off[n] + BIG_ROWS[n]]
        out[n] = blk.T if n in COL_SHARDED else blk
    return out


def _rope_setup(positions):
    S = positions.shape[0]
    inv = ROPE_THETA ** (-jnp.arange(0, ROT_DIM, 2, dtype=F32) / ROT_DIM)
    lane = np.arange(128) % HEAD_DIM
    half = ROT_DIM // 2
    inv_row = jnp.where(lane < ROT_DIM, jnp.tile(inv, 128 // half), 0.0)[None, :].astype(F32)
    m1_row = jnp.asarray((lane < half).astype(np.float32))[None, :]
    m2_row = jnp.asarray(((lane >= half) & (lane < ROT_DIM)).astype(np.float32))[None, :]
    return (*_rope_tables(positions.astype(F32).reshape(S, 1), inv_row, m1_row, m2_row), _attn_bias())


def _hook(hooks, point, after):
    f = None if hooks is None else hooks.get(point)
    return None if f is None else f(after)


def _layer_fwd(l, x, mod_l, W, small, rope, hooks=None):
    rc, rs1, rs2, bias = rope
    sh1, sc1, g1, sh2, sc2, g2 = [mod_l[i * D_MODEL:(i + 1) * D_MODEL][None, :] for i in range(6)]
    nw1, nw2 = small["norm1_w"][l][None, :], small["norm2_w"][l][None, :]
    h = _normmod_fwd(x, nw1, sc1, sh1, name=f"normmod1_fwd{l}")
    tok = _hook(hooks, "mm_in", h)
    proj = _mm(h, W["wt_in"], nt=True, out_dtype=BF, name=f"mm_in{l}", after=tok, tn_cap=768)
    q_r, k_r, v_b = _rope_fwd(proj, rc, rs1, rs2, name=f"rope_fwd{l}")
    sink_rows = jnp.repeat(small["attn_sinks"][l].reshape(N_KV_HEADS, Q_PER_KV), ATTN_BLOCK, axis=1)
    sink_rows = jnp.broadcast_to(sink_rows[..., None], sink_rows.shape + (128,))
    y_attn = _attn_fwd(q_r, k_r, v_b, sink_rows, bias, name=f"attn_fwd{l}")
    lnw, lnb = small["sgu_ln_w"][l][None, :], small["sgu_ln_b"][l][None, :]
    sgu_bt = small["sgu_b"][l].T
    y_sgu = _sgu_fwd(proj, lnw, lnb, small["sgu_w"][l], sgu_bt, name=f"sgu_fwd{l}", after=_hook(hooks, "sgu", y_attn))
    tok = _hook(hooks, "mm_pa", y_sgu)
    a_br, b_br, merged = _merge_fwd(y_sgu, y_attn, W["proj_a"], W["proj_b"], proj, name=f"merge_fwd{l}", after=tok)
    x1, o1 = _mm(merged, W["w_out"], nt=False, out_dtype=F32, name=f"mm_out{l}", res=x, gvec=g1)
    h2 = _normmod_fwd(x1, nw2, sc2, sh2, name=f"normmod2_fwd{l}")
    a_g = _mm(h2, W["wt_gate"], nt=True, out_dtype=BF, name=f"mm_gate{l}", tn_cap=1408)
    a_u = _mm(h2, W["wt_up"], nt=True, out_dtype=BF, name=f"mm_up{l}", tn_cap=1408)
    cw, cb = small["ffn_conv_w"][l], small["ffn_conv_b"][l][None, :]
    hf = _ffn_act_fwd(a_g, a_u, cw, cb, name=f"ffn_act_fwd{l}")
    x2, o2 = _mm(hf, W["w_down"], nt=False, out_dtype=F32, name=f"mm_down{l}", res=x1, gvec=g2)
    saved = dict(x=x, h=h, proj=proj, q_r=q_r, k_r=k_r, v_b=v_b, sink_rows=sink_rows, y_attn=y_attn, y_sgu=y_sgu,
                 a_br=a_br, b_br=b_br, merged=merged, x1=x1, o1=o1, h2=h2, a_g=a_g, a_u=a_u, hf=hf, o2=o2)
    return x2, saved


def _layer_bwd(l, dx, do2, dg2, mod_l, W, small, rope, sv, below=None, hooks=None, wg=None):
    rc, rs1, rs2, bias = rope
    sh1, sc1, g1, sh2, sc2, g2 = [mod_l[i * D_MODEL:(i + 1) * D_MODEL][None, :] for i in range(6)]
    nw1, nw2 = small["norm1_w"][l][None, :], small["norm2_w"][l][None, :]
    cw, cb = small["ffn_conv_w"][l], small["ffn_conv_b"][l][None, :]
    lnw, lnb = small["sgu_ln_w"][l][None, :], small["sgu_ln_b"][l][None, :]
    sgu_bt = small["sgu_b"][l].T
    wg = {} if wg is None else wg
    dhf = _mm(do2, W["w_down"], nt=True, out_dtype=BF, name=f"mm_down_dx{l}", after=_hook(hooks, "mm_down_dx", do2),
              tn_cap=1408)
    wg["w_down"] = _mm_tn(sv["hf"], do2, name=f"mm_down_dw{l}")
    dac, dup, dcw, dcb = _ffn_act_bwd_a(dhf, sv["a_g"], sv["a_u"], cw, cb, name=f"ffn_act_bwd_a{l}")
    da = _ffn_act_bwd_b(dac, cw, name=f"ffn_act_bwd_b{l}")
    dh2 = _mm([da, dup], [W["wt_gate"], W["wt_up"]], nt=False, out_dtype=F32, name=f"mm_gu_dx{l}",
              after=_hook(hooks, "mm_gu_dx", da))
    wg["wt_gate"] = _mm_tn(da, sv["h2"], name=f"mm_gate_dw{l}")
    wg["wt_up"] = _mm_tn(dup, sv["h2"], name=f"mm_up_dw{l}")
    dx1, dnw2, dsc2, dsh2, do1, dg1 = _normmod_bwd(dh2, sv["x1"], nw2, sc2, sh2, dx, (sv["o1"], g1), name=f"normmod2_bwd{l}")
    d_a, d_b, dproj = _merge_bwd(do1, W["w_out"], sv["a_br"], sv["b_br"], sv["proj"], name=f"merge_bwd{l}",
                                 after=_hook(hooks, "merge_bwd", do1))
    wg["w_out"] = _mm_tn(sv["merged"], do1, name=f"mm_out_dw{l}")
    dysgu = _mm(d_a, W["proj_a"], nt=True, out_dtype=F32, name=f"mm_pa_dx{l}", after=_hook(hooks, "mm_pa_dx", d_a))
    dyattn = _mm(d_b, W["proj_b"], nt=True, out_dtype=BF, name=f"mm_pb_dx{l}")
    wg["proj_a"] = _mm_tn(sv["y_sgu"], d_a, name=f"mm_pa_dw{l}")
    wg["proj_b"] = _mm_tn(sv["y_attn"], d_b, name=f"mm_pb_dw{l}")
    dproj, dlnw, dlnb, dsguw, dsgubt = _sgu_bwd(dysgu, sv["proj"], lnw, lnb, small["sgu_w"][l], sgu_bt, dproj,
                                                name=f"sgu_bwd{l}")
    dq_r, dk_r, dv_b, dsk = _attn_bwd(dyattn, sv["q_r"], sv["k_r"], sv["v_b"], sv["sink_rows"], bias, name=f"attn_bwd{l}")
    dproj = _rope_bwd(dq_r, dk_r, dv_b, rc, rs1, rs2, dproj, name=f"rope_bwd{l}")
    wg["wt_in"] = _mm_tn(dproj, sv["h"], name=f"mm_in_dw{l}")
    dh = _mm(dproj, W["wt_in"], nt=False, out_dtype=F32, name=f"mm_in_dx{l}", after=_hook(hooks, "mm_in_dx", wg["wt_in"]))
    dx0, dnw1, dsc1, dsh1, *gate_below = _normmod_bwd(dh, sv["x"], nw1, sc1, sh1, dx1, below, name=f"normmod1_bwd{l}")
    dmod = jnp.concatenate([dsh1, dsc1, dg1, dsh2, dsc2, dg2], axis=1)[0]
    sg = {"norm1_w": dnw1[0], "norm2_w": dnw2[0], "attn_sinks": dsk[:, :, 0].reshape(N_Q_HEADS),
          "sgu_ln_w": dlnw[0], "sgu_ln_b": dlnb[0], "sgu_w": dsguw, "sgu_b": dsgubt.T,
          "ffn_conv_w": dcw, "ffn_conv_b": dcb[0]}
    return (dx0, *gate_below), wg, sg, dmod


SMALL = ("ada_b", "norm1_w", "attn_sinks", "sgu_ln_w", "sgu_ln_b", "sgu_w", "sgu_b", "norm2_w", "ffn_conv_b", "final_norm_w")
WEIGHT_ORDER = ("ada_w", "ada_b", "norm1_w", "w_in", "attn_sinks", "sgu_ln_w", "sgu_ln_b", "sgu_w", "sgu_b", "proj_a", "proj_b",
                "w_out", "norm2_w", "ffn_w_gate", "ffn_w_up", "ffn_conv_w", "ffn_conv_b", "ffn_w_down", "final_norm_w")


def _flat_pack(arrs, rows):
    flat = jnp.concatenate([a.reshape(-1) for a in arrs])
    return jnp.pad(flat, (0, rows * 1024 - flat.shape[0])).reshape(rows, 1024)


def _flat_unpack(buf, shapes):
    flat = buf.reshape(-1)
    out, o = [], 0
    for s in shapes:
        n = int(np.prod(s))
        out.append(flat[o:o + n].reshape(s))
        o += n
    return out


def _adam2d(w, g, m, v, *, name):
    shp = w.shape
    r2 = (int(np.prod(shp[:-1])), shp[-1]) if len(shp) > 1 else (1, shp[0])
    d, mn, vn = _adamw(w.reshape(r2), g.reshape(r2), m.reshape(r2), v.reshape(r2), name=name)
    return d.reshape(shp), mn.reshape(shp), vn.reshape(shp)


def kernel(x, c, positions, ada_w, ada_b, norm1_w, w_in, attn_sinks, sgu_ln_w, sgu_ln_b, sgu_w, sgu_b, proj_a, proj_b, w_out, norm2_w, ffn_w_gate, ffn_w_up, ffn_conv_w, ffn_conv_b, ffn_w_down, final_norm_w, loss_target, m_ada_w, m_ada_b, m_norm1_w, m_w_in, m_attn_sinks, m_sgu_ln_w, m_sgu_ln_b, m_sgu_w, m_sgu_b, m_proj_a, m_proj_b, m_w_out, m_norm2_w, m_ffn_w_gate, m_ffn_w_up, m_ffn_conv_w, m_ffn_conv_b, m_ffn_w_down, m_final_norm_w, v_ada_w, v_ada_b, v_norm1_w, v_w_in, v_attn_sinks, v_sgu_ln_w, v_sgu_ln_b, v_sgu_w, v_sgu_b, v_proj_a, v_proj_b, v_w_out, v_norm2_w, v_ffn_w_gate, v_ffn_w_up, v_ffn_conv_w, v_ffn_conv_b, v_ffn_w_down, v_final_norm_w):
    wts = dict(ada_w=ada_w, ada_b=ada_b, norm1_w=norm1_w, w_in=w_in, attn_sinks=attn_sinks, sgu_ln_w=sgu_ln_w,
               sgu_ln_b=sgu_ln_b, sgu_w=sgu_w, sgu_b=sgu_b, proj_a=proj_a, proj_b=proj_b, w_out=w_out, norm2_w=norm2_w,
               ffn_w_gate=ffn_w_gate, ffn_w_up=ffn_w_up, ffn_conv_w=ffn_conv_w, ffn_conv_b=ffn_conv_b,
               ffn_w_down=ffn_w_down, final_norm_w=final_norm_w)
    mom = dict(ada_w=m_ada_w, ada_b=m_ada_b, norm1_w=m_norm1_w, w_in=m_w_in, attn_sinks=m_attn_sinks, sgu_ln_w=m_sgu_ln_w,
               sgu_ln_b=m_sgu_ln_b, sgu_w=m_sgu_w, sgu_b=m_sgu_b, proj_a=m_proj_a, proj_b=m_proj_b, w_out=m_w_out,
               norm2_w=m_norm2_w, ffn_w_gate=m_ffn_w_gate, ffn_w_up=m_ffn_w_up, ffn_conv_w=m_ffn_conv_w,
               ffn_conv_b=m_ffn_conv_b, ffn_w_down=m_ffn_w_down, final_norm_w=m_final_norm_w)
    var = dict(ada_w=v_ada_w, ada_b=v_ada_b, norm1_w=v_norm1_w, w_in=v_w_in, attn_sinks=v_attn_sinks, sgu_ln_w=v_sgu_ln_w,
               sgu_ln_b=v_sgu_ln_b, sgu_w=v_sgu_w, sgu_b=v_sgu_b, proj_a=v_proj_a, proj_b=v_proj_b, w_out=v_w_out,
               norm2_w=v_norm2_w, ffn_w_gate=v_ffn_w_gate, ffn_w_up=v_ffn_w_up, ffn_conv_w=v_ffn_conv_w,
               ffn_conv_b=v_ffn_conv_b, ffn_w_down=v_ffn_w_down, final_norm_w=v_final_norm_w)
    me = 4 * lax.axis_index("x") + 2 * lax.axis_index("y") + lax.axis_index("c")
    ada_cols = ada_w.shape[2]

    c_all = _all_gather(jnp.broadcast_to(c, (8, D_MODEL)), name="ag_c")[:, 0, :]
    prod = _ada_fwd(c_all, ada_w)
    prod_all = _all_gather(prod, name="ag_mod")
    mine = lax.dynamic_index_in_dim(prod_all, me, axis=1, keepdims=False)
    mod = jnp.stack([mine[:, l * ada_cols:(l + 1) * ada_cols].reshape(-1) for l in range(DEPTH)]) + ada_b

    conv_cols = ffn_conv_w.shape[2]
    conv_all = _all_gather(_flat_pack([ffn_conv_w], 8), name="ag_conv", after=mod)
    conv_full = jnp.stack([a.reshape(DEPTH, 3, conv_cols) for a in
                           [conv_all[j].reshape(-1)[:DEPTH * 3 * conv_cols] for j in range(N_DEV)]], axis=2)
    conv_full = conv_full.reshape(DEPTH, 3, FFN_DIM)
    small = {n: wts[n] for n in SMALL}
    small["ffn_conv_w"] = conv_full

    mx, my, mc = _coords()
    cidx = jnp.reshape(mc, (1,)).astype(jnp.int32)
    chipidx = jnp.reshape(2 * mx + my, (1,)).astype(jnp.int32)
    rope = _rope_setup(positions[0])

    class Gather:
        def __init__(self, src, tag):
            self.tag, self.src = tag, src
            self.land = lax.dynamic_update_slice(lax.empty((2, 4) + src.shape, src.dtype), src[None, None],
                                                 (mc, 2 * mx + my, 0, 0))

        def ici_start(self, after):
            self.sems, (self.src, self.land), tok = _rdma_start([self.src, self.land], 3, _plan_gather_ici,
                                                                name=f"ag_{self.tag}_ici_start", after=after)
            return tok

        def ici_wait_d2d_start(self, after):
            _, land = _rdma_wait(self.sems, [self.src, self.land], 3, _plan_gather_ici, after, name=f"ag_{self.tag}_ici_wait")
            self.sems, (self.land,), tok = _rdma_start([land], 1, _plan_gather_d2d, name=f"ag_{self.tag}_d2d_start")
            return tok

        def d2d_wait(self, after):
            (land,) = _rdma_wait(self.sems, [self.land], 1, _plan_gather_d2d, after, name=f"ag_{self.tag}_d2d_wait")
            return _from_land(land)

    def weights_job(names, l, tag):
        job = Gather(_pack_shards(wts, l, names), tag)
        job.weights = lambda after: _unpack_weights(job.d2d_wait(after), names)
        return job

    W0 = _unpack_weights(_all_gather(_pack_shards(wts, 0, PART_IN), name="ag_w0_in", after=conv_all), PART_IN)
    W1 = {}
    g_mix0, g_ffn0 = weights_job(PART_MIX, 0, "w0_mix"), weights_job(PART_FFN, 0, "w0_ffn")
    g_all1 = weights_job(BIG, 1, "w1")
    toks = {}

    def rest0_to_sibling(after):
        toks["mix"] = g_mix0.ici_wait_d2d_start(after)
        toks["ffn"] = g_ffn0.ici_wait_d2d_start(toks["mix"])
        return toks["ffn"]

    def rest0_then_layer1(after):
        W0.update(g_mix0.weights(after))
        W0.update(g_ffn0.weights(W0["proj_a"]))
        return g_all1.ici_start(W0["w_down"])

    x1, sv0 = _layer_fwd(0, x[0], mod[0], W0, small, rope,
                         {"mm_in": lambda after: g_ffn0.ici_start(g_mix0.ici_start(W0["wt_in"])),
                          "sgu": rest0_to_sibling, "mm_pa": rest0_then_layer1})
    g_all1.ici_wait_d2d_start(x1)
    x2, sv1 = _layer_fwd(1, x1, mod[1], W1, small, rope, {"mm_in": lambda after: W1.update(g_all1.weights(after))})
    gate2 = [mod[l][5 * D_MODEL:][None, :] for l in range(DEPTH)]
    dx2, dfw, loss_tile, do2, dg2 = _head(x2, final_norm_w[None, :], loss_target[0], (sv1["o2"], gate2[1]))
    loss = lax.psum(loss_tile[0, 0], ("x", "y", "c"))

    class Reduce:
        def __init__(self, names, tag):
            self.names, self.tag, self.rows = names, tag, _part_rows(names)

        def d2d_start(self, wg, after=None):
            self.sems, self.bufs, tok = _rdma_start([_pack_grads(wg, self.names), lax.empty((4, self.rows, 1024), BF)], 1,
                                                    _plan_reduce_d2d, name=f"rs_{self.tag}_d2d_start", after=after)
            return tok

        def d2d_wait_ici_start(self, after):
            g_t, land_a = _rdma_wait(self.sems, self.bufs, 1, _plan_reduce_d2d, after, name=f"rs_{self.tag}_d2d_wait")
            h = _sum_pair(g_t, land_a, cidx, name=f"rs_{self.tag}_sum_pair")
            self.sems, self.bufs, tok = _rdma_start([h, lax.empty((3, self.rows, 1024), BF)], 3, _plan_reduce_ici,
                                                    name=f"rs_{self.tag}_ici_start")
            return tok

        def ici_wait(self, after):
            h_t, land_b = _rdma_wait(self.sems, self.bufs, 3, _plan_reduce_ici, after, name=f"rs_{self.tag}_ici_wait")
            return _unpack_shard_grads(_sum_chips(h_t, land_b, chipidx, name=f"rs_{self.tag}_sum_chips"), self.names)

    (dx1, do2, dg2), wg1, sg1, dmod1 = _layer_bwd(1, dx2, do2, dg2, mod[1], W1, small, rope, sv1, below=(sv0["o2"], gate2[0]))
    r_all1, r_ffn0, r_mix0, r_in0 = Reduce(BIG, "g1"), Reduce(PART_FFN, "g0_ffn"), Reduce(PART_MIX, "g0_mix"), Reduce(PART_IN, "g0_in")
    tok1 = r_all1.d2d_start(wg1)
    wg0, shard1 = {}, {}

    def layer1_done_then_ffn0(after):
        shard1.update(r_all1.ici_wait(after))
        return r_ffn0.d2d_wait_ici_start(shard1["w_in"])

    def mix0_and_in0(after):
        tok = r_in0.d2d_start(wg0, r_mix0.d2d_start(wg0, after))
        return r_in0.d2d_wait_ici_start(r_mix0.d2d_wait_ici_start(tok))

    (grad_x,), _, sg0, dmod0 = _layer_bwd(
        0, dx1, do2, dg2, mod[0], W0, small, rope, sv0, wg=wg0,
        hooks={"mm_down_dx": lambda after: tok1, "mm_gu_dx": r_all1.d2d_wait_ici_start,
               "merge_bwd": lambda after: r_ffn0.d2d_start(wg0, after), "mm_pa_dx": layer1_done_then_ffn0,
               "mm_in_dx": mix0_and_in0})
    sg = {n: jnp.stack([sg0[n], sg1[n]]) for n in sg0}
    sg["final_norm_w"] = dfw[0]
    dmod = jnp.stack([dmod0, dmod1])
    vec_names = [n for n in SMALL if n not in ("ada_b", "sgu_w")] + ["ffn_conv_w"]
    vec_shapes = [(DEPTH, 6 * D_MODEL)] + [sg[n].shape for n in vec_names]
    vec_rows = -(-sum(int(np.prod(s)) for s in vec_shapes) // 1024 // 16) * 16
    sgu_rows = sgu_w.size // 1024
    g_small = Gather(jnp.concatenate([_flat_pack([dmod] + [sg[n] for n in vec_names], vec_rows),
                                      sg["sgu_w"].reshape(sgu_rows, 1024)], axis=0).astype(BF), "small")
    tok = g_small.ici_start(grad_x)

    shard0 = r_ffn0.ici_wait(tok)
    shard0.update(r_mix0.ici_wait(shard0["ffn_w_down"]))
    shard0.update(r_in0.ici_wait(shard0["w_out"]))
    grads, delta, new_m, new_v = {}, {}, {}, {}
    for n in BIG:
        two = lambda a: a.reshape(DEPTH, -1, a.shape[-1])
        out = _adamw_layers(two(wts[n]), [shard0[n], shard1[n]], two(mom[n]), two(var[n]), name=f"adamw_{n}")
        grads[n], delta[n], new_m[n], new_v[n] = [o.reshape(wts[n].shape) for o in out]

    sm_all = g_small.d2d_wait(g_small.ici_wait_d2d_start(delta["ffn_w_gate"]))
    sm_sum = _sum8(sm_all, name="sum_small")
    vec_sum = _flat_unpack(sm_sum[:vec_rows], vec_shapes)
    grads["ada_b"] = vec_sum[0]
    for n, gsum in zip(vec_names, vec_sum[1:]):
        grads[n] = gsum
    grads["sgu_w"] = sm_sum[vec_rows:].reshape(sgu_w.shape)
    grads["ffn_conv_w"] = lax.dynamic_slice_in_dim(grads["ffn_conv_w"], me * conv_cols, conv_cols, axis=2)
    dmod_all = sm_all[:, :DEPTH * 6, :].astype(F32).reshape(N_DEV, DEPTH, 6 * D_MODEL)
    dm_mine = lax.dynamic_slice_in_dim(dmod_all, me * ada_cols, ada_cols, axis=2).transpose(1, 0, 2)
    dm_mine = jnp.pad(dm_mine, ((0, 0), (0, 8), (0, 0)))
    grads["ada_w"] = _ada_bwd(jnp.pad(c_all, ((0, 8), (0, 0))), dm_mine)

    packed_small = [n for n in SMALL if n != "sgu_w"]
    pshapes = [wts[n].shape for n in packed_small]
    prow = -(-sum(int(np.prod(s)) for s in pshapes) // 1024 // 8) * 8
    pk = lambda d: _flat_pack([d[n] for n in packed_small], prow)
    d_s, m_s, v_s = _adamw(pk(wts), pk(grads), pk(mom), pk(var), name="adamw_small")
    for n, dd, mm, vv in zip(packed_small, _flat_unpack(d_s, pshapes), _flat_unpack(m_s, pshapes), _flat_unpack(v_s, pshapes)):
        delta[n], new_m[n], new_v[n] = dd, mm, vv
    for n in WEIGHT_ORDER:
        if n not in delta:
            delta[n], new_m[n], new_v[n] = _adam2d(wts[n], grads[n], mom[n], var[n], name=f"adamw_{n}")
    return (loss, grad_x[None], *[grads[n] for n in WEIGHT_ORDER], *[delta[n] for n in WEIGHT_ORDER],
            *[new_m[n] for n in WEIGHT_ORDER], *[new_v[n] for n in WEIGHT_ORDER])
```

```python
import functools

import jax
import jax.numpy as jnp
import numpy as np
from jax import lax
from jax.experimental import pallas as pl
from jax.experimental.pallas import tpu as pltpu

F32 = jnp.float32
BF = jnp.bfloat16

N_DEV = 8
D_MODEL = 1024
DEPTH = 2
N_Q_HEADS = 16
N_KV_HEADS = 2
HEAD_DIM = 64
Q_PER_KV = N_Q_HEADS // N_KV_HEADS
ATTN_BLOCK = 128
ROPE_THETA = 500000.0
ROT_DIM = HEAD_DIM // 4
SGU_WIDTH = 1024
SGU_GROUPS = 8
SGU_CHUNK = 128
FFN_DIM = 2816
NORM_EPS = 1e-6
Q_END = N_Q_HEADS * HEAD_DIM
K_END = Q_END + N_KV_HEADS * HEAD_DIM
V_END = K_END + N_KV_HEADS * HEAD_DIM
Z_END = V_END + 2 * SGU_WIDTH
IN_COLS = Z_END + 2 * D_MODEL
P_Z, P_G, P_Q, P_K, P_V = 0, 2048, 4096, 5120, 5248

ADAM_LR = 0.001
ADAM_B1 = 0.9
ADAM_B2 = 0.999
ADAM_EPS = 1e-08
ADAM_WD = 0.01
ADAM_STEP = 10

VMEM_LIMIT_BYTES = 56 * 1024 * 1024

BIG = ("w_in", "proj_a", "proj_b", "w_out", "ffn_w_gate", "ffn_w_up", "ffn_w_down")
COL_SHARDED = ("w_in", "ffn_w_gate", "ffn_w_up")
BIG_SHAPE = {"w_in": (D_MODEL, IN_COLS), "proj_a": (SGU_WIDTH, D_MODEL), "proj_b": (Q_END, D_MODEL),
             "w_out": (D_MODEL, D_MODEL), "ffn_w_gate": (D_MODEL, FFN_DIM), "ffn_w_up": (D_MODEL, FFN_DIM),
             "ffn_w_down": (FFN_DIM, D_MODEL)}
BIG_ROWS = {n: BIG_SHAPE[n][0] * BIG_SHAPE[n][1] // N_DEV // 1024 for n in BIG}
LAYER_ROWS = sum(BIG_ROWS.values())


def _pcall(body, **kw):
    return pl.pallas_call(body, **kw)


def _params(**kw):
    return pltpu.CompilerParams(vmem_limit_bytes=VMEM_LIMIT_BYTES, **kw)


def _tile(n, cap, unit=128):
    if n <= cap:
        return n
    best = 0
    t = unit
    while t <= cap:
        if n % t == 0:
            best = t
        t += unit
    assert best, (n, cap, unit)
    return best


def _mm(a, b, *, nt, out_dtype, name, res=None, gvec=None, after=None, tm=None, tn_cap=1024):
    a_list = list(a) if isinstance(a, (list, tuple)) else [a]
    b_list = list(b) if isinstance(b, (list, tuple)) else [b]
    a, b = a_list[0], b_list[0]
    M, K = a.shape
    N = b.shape[0] if nt else b.shape[1]
    k_total = sum(x.shape[1] for x in a_list)
    tm = _tile(M, tm or (1024 if k_total <= 1024 else 512), 8)
    tn = _tile(N, tn_cap)
    dn = (((1,), (1,)), ((), ())) if nt else (((1,), (0,)), ((), ()))

    def b_spec_of(x):
        k = x.shape[1] if nt else x.shape[0]
        return pl.BlockSpec((tn, k), lambda i, j: (j, 0)) if nt else pl.BlockSpec((k, tn), lambda i, j: (0, j))
    b_spec = b_spec_of(b)
    o_spec = pl.BlockSpec((tm, tn), lambda i, j: (i, j))
    if res is None:
        extra = [] if after is None else [after]
        n = len(a_list)

        def body(*refs):
            o_ref = refs[-1]
            acc = None
            for a_ref, b_ref in zip(refs[:n], refs[n:2 * n]):
                d = lax.dot_general(a_ref[...].astype(BF), b_ref[...].astype(BF), dn, preferred_element_type=F32)
                acc = d if acc is None else acc + d
            o_ref[...] = acc.astype(out_dtype)
        return _pcall(body, name=name, grid=(M // tm, N // tn),
                      in_specs=[pl.BlockSpec((tm, x.shape[1]), lambda i, j: (i, 0)) for x in a_list]
                      + [b_spec_of(x) for x in b_list] + [ANY] * len(extra), out_specs=o_spec,
                      out_shape=jax.ShapeDtypeStruct((M, N), out_dtype), compiler_params=_params())(
                          *a_list, *b_list, *extra)

    def body_res(a_ref, b_ref, r_ref, g_ref, o_ref, acc_ref):
        acc = lax.dot_general(a_ref[...].astype(BF), b_ref[...].astype(BF), dn, preferred_element_type=F32)
        acc_ref[...] = acc.astype(BF)
        o_ref[...] = r_ref[...] + g_ref[...] * acc
    return _pcall(body_res, name=name, grid=(M // tm, N // tn),
                  in_specs=[pl.BlockSpec((tm, K), lambda i, j: (i, 0)), b_spec, o_spec,
                            pl.BlockSpec((1, tn), lambda i, j: (0, j))],
                  out_specs=[o_spec, o_spec],
                  out_shape=[jax.ShapeDtypeStruct((M, N), F32), jax.ShapeDtypeStruct((M, N), BF)],
                  compiler_params=_params())(a, b, res, gvec)


def _mm_tn(a, b, *, name, out_dtype=BF, tk=1024, tm_cap=1408, tn_cap=1024):
    S, M = a.shape
    N = b.shape[1]
    tk = _tile(S, tk, 8)
    tm = _tile(M, tm_cap)
    tn = _tile(N, tn_cap)
    nk = S // tk

    def body(a_ref, b_ref, o_ref, acc_ref):
        k = pl.program_id(2)

        @pl.when(k == 0)
        def _():
            acc_ref[...] = jnp.zeros_like(acc_ref)
        acc_ref[...] += lax.dot_general(a_ref[...].astype(BF), b_ref[...].astype(BF), (((0,), (0,)), ((), ())),
                                        preferred_element_type=F32)

        @pl.when(k == nk - 1)
        def _():
            o_ref[...] = acc_ref[...].astype(out_dtype)
    return _pcall(body, name=name, grid=(M // tm, N // tn, nk),
                  in_specs=[pl.BlockSpec((tk, tm), lambda i, j, k: (k, i)),
                            pl.BlockSpec((tk, tn), lambda i, j, k: (k, j))],
                  out_specs=pl.BlockSpec((tm, tn), lambda i, j, k: (i, j)),
                  out_shape=jax.ShapeDtypeStruct((M, N), out_dtype), scratch_shapes=[pltpu.VMEM((tm, tn), F32)],
                  compiler_params=_params())(a, b)


def _rms(x, w):
    return x * lax.rsqrt(jnp.mean(x * x, axis=-1, keepdims=True) + NORM_EPS) * w


def _normmod_fn(x, nw, sc, sh):
    return _rms(x, nw) * (1.0 + sc) + sh


def _gelu(x):
    return 0.5 * x * (1.0 + lax.erf(x * (2.0 ** -0.5)))


def _ln_gelu_fn(zv, w, b):
    v = _gelu(zv)
    mu = jnp.mean(v, axis=-1, keepdims=True)
    var = jnp.mean(jnp.square(v - mu), axis=-1, keepdims=True)
    return (v - mu) * lax.rsqrt(var + NORM_EPS) * w + b


def _sigmoid(x):
    return 1.0 / (1.0 + jnp.exp(-x))


def _row_spec(tm, n):
    return pl.BlockSpec((tm, n), lambda i: (i, 0))


def _vec_spec(n):
    return pl.BlockSpec((1, n), lambda i: (0, 0))


def _acc(ref, val):
    @pl.when(pl.program_id(0) == 0)
    def _():
        ref[...] = jnp.zeros_like(ref)
    ref[...] += val


def _norm_mm(x, nw, sc, sh, ws, *, name, after=None, tm=1024, tn_cap=768):
    S, K = x.shape
    N = ws[0].shape[0]
    tm = _tile(S, tm, 8)
    tn = _tile(N, tn_cap)
    nw_, ne = len(ws), 0 if after is None else 1

    def body(x_ref, nw_ref, sc_ref, sh_ref, *rest):
        w_refs = rest[:nw_]
        h_ref = rest[nw_ + ne]
        o_refs = rest[nw_ + ne + 1:nw_ + ne + 1 + nw_]
        h_s = rest[-1]

        @pl.when(pl.program_id(1) == 0)
        def _():
            hv = _normmod_fn(x_ref[...], nw_ref[...], sc_ref[...], sh_ref[...]).astype(BF)
            h_s[...] = hv
            h_ref[...] = hv
        for w_ref, o_ref in zip(w_refs, o_refs):
            o_ref[...] = lax.dot_general(h_s[...], w_ref[...], (((1,), (1,)), ((), ())),
                                         preferred_element_type=F32).astype(BF)
    row = pl.BlockSpec((tm, K), lambda i, j: (i, 0))
    vec = pl.BlockSpec((1, K), lambda i, j: (0, 0))
    out = pl.BlockSpec((tm, tn), lambda i, j: (i, j))
    res = _pcall(body, name=name, grid=(S // tm, N // tn),
                 in_specs=[row, vec, vec, vec] + [pl.BlockSpec((tn, K), lambda i, j: (j, 0))] * nw_ + [ANY] * ne,
                 out_specs=[row] + [out] * nw_,
                 out_shape=[jax.ShapeDtypeStruct((S, K), BF)] + [jax.ShapeDtypeStruct((S, N), BF)] * nw_,
                 scratch_shapes=[pltpu.VMEM((tm, K), BF)], compiler_params=_params())(
                     x, nw, sc, sh, *ws, *([] if after is None else [after]))
    return res[0], list(res[1:])


def _gate_bwd(dxv, o_ref, g_ref, do_ref, dg_ref):
    do_ref[...] = (dxv * g_ref[...]).astype(BF)
    _acc(dg_ref, jnp.sum(dxv * o_ref[...].astype(F32), axis=0, keepdims=True))


def _normmod_bwd(dh, x, nw, sc, sh, dres, gate, *, name, tm=256):
    S, Dm = x.shape
    tm = _tile(S, tm, 8)
    ng = 0 if gate is None else 2

    def body(dh_ref, x_ref, nw_ref, sc_ref, sh_ref, dres_ref, *rest):
        dx_ref, dnw_ref, dsc_ref, dsh_ref = rest[ng:ng + 4]
        _, vjp = jax.vjp(_normmod_fn, x_ref[...], nw_ref[...], sc_ref[...], sh_ref[...])
        dx, dnw, dsc, dsh = vjp(dh_ref[...])
        dxv = dres_ref[...] + dx
        dx_ref[...] = dxv
        _acc(dnw_ref, dnw)
        _acc(dsc_ref, dsc)
        _acc(dsh_ref, dsh)
        if gate is not None:
            _gate_bwd(dxv, rest[0], rest[1], rest[ng + 4], rest[ng + 5])
    vec = jax.ShapeDtypeStruct((1, Dm), F32)
    gate_in = [] if gate is None else [_row_spec(tm, Dm), _vec_spec(Dm)]
    gate_out = [] if gate is None else [_row_spec(tm, Dm), _vec_spec(Dm)]
    gate_shape = [] if gate is None else [jax.ShapeDtypeStruct((S, Dm), BF), vec]
    return _pcall(body, name=name, grid=(S // tm,),
                  in_specs=[_row_spec(tm, Dm), _row_spec(tm, Dm), _vec_spec(Dm), _vec_spec(Dm), _vec_spec(Dm),
                            _row_spec(tm, Dm)] + gate_in,
                  out_specs=[_row_spec(tm, Dm), _vec_spec(Dm), _vec_spec(Dm), _vec_spec(Dm)] + gate_out,
                  out_shape=[jax.ShapeDtypeStruct((S, Dm), F32), vec, vec, vec] + gate_shape,
                  compiler_params=_params())(dh, x, nw, sc, sh, dres, *([] if gate is None else gate))


def _head(x, fw, target, gate, *, tm=256):
    S, Dm = x.shape
    tm = _tile(S, tm, 8)

    def body(x_ref, fw_ref, t_ref, o_ref, g_ref, dx_ref, dfw_ref, loss_ref, do_ref, dg_ref):
        y, vjp = jax.vjp(_rms, x_ref[...], fw_ref[...])
        err = y - t_ref[...]
        dx, dfw = vjp(err * (1.0 / Dm))
        dx_ref[...] = dx
        _acc(dfw_ref, dfw)
        part = 0.5 * jnp.sum(jnp.mean(err * err, axis=-1, keepdims=True), axis=0, keepdims=True)
        _acc(loss_ref, jnp.broadcast_to(part, (8, 128)))
        _gate_bwd(dx, o_ref, g_ref, do_ref, dg_ref)
    vec = jax.ShapeDtypeStruct((1, Dm), F32)
    return _pcall(body, name="head", grid=(S // tm,),
                  in_specs=[_row_spec(tm, Dm), _vec_spec(Dm), _row_spec(tm, Dm), _row_spec(tm, Dm), _vec_spec(Dm)],
                  out_specs=[_row_spec(tm, Dm), _vec_spec(Dm), pl.BlockSpec((8, 128), lambda i: (0, 0)),
                             _row_spec(tm, Dm), _vec_spec(Dm)],
                  out_shape=[jax.ShapeDtypeStruct((S, Dm), F32), vec, jax.ShapeDtypeStruct((8, 128), F32),
                             jax.ShapeDtypeStruct((S, Dm), BF), vec],
                  compiler_params=_params())(x, fw, target, *gate)


def _tril_mask():
    r = lax.broadcasted_iota(jnp.int32, (SGU_CHUNK, SGU_CHUNK), 0)
    c = lax.broadcasted_iota(jnp.int32, (SGU_CHUNK, SGU_CHUNK), 1)
    return c <= r


def _sgu_fwd(proj, lnw, lnb, w, b_t, *, name, after=None, tm=256):
    S = proj.shape[0]
    tm = _tile(S, tm, SGU_CHUNK)
    extra = [] if after is None else [after]

    def body(zu_ref, zv_ref, lnw_ref, lnb_ref, w_ref, bt_ref, *rest):
        o_ref = rest[-1]
        u = _gelu(zu_ref[...].astype(F32))
        vn = _ln_gelu_fn(zv_ref[...].astype(F32), lnw_ref[...], lnb_ref[...]).astype(BF)
        mask = _tril_mask()
        for g in range(SGU_GROUPS):
            wm = jnp.where(mask, w_ref[g], 0.0).astype(BF)
            cols = slice(g * 128, (g + 1) * 128)
            for ci in range(tm // SGU_CHUNK):
                rows = slice(ci * SGU_CHUNK, (ci + 1) * SGU_CHUNK)
                f = jnp.dot(wm, vn[rows, cols], preferred_element_type=F32) + bt_ref[:, g:g + 1]
                o_ref[rows, cols] = (u[rows, cols] * f).astype(BF)
    return _pcall(body, name=name, grid=(S // tm,),
                  in_specs=[pl.BlockSpec((tm, SGU_WIDTH), lambda i: (i, 0)), pl.BlockSpec((tm, SGU_WIDTH), lambda i: (i, 1)),
                            _vec_spec(SGU_WIDTH), _vec_spec(SGU_WIDTH),
                            pl.BlockSpec((SGU_GROUPS, 128, 128), lambda i: (0, 0, 0)),
                            pl.BlockSpec((128, SGU_GROUPS), lambda i: (0, 0))] + [ANY] * len(extra),
                  out_specs=_row_spec(tm, SGU_WIDTH), out_shape=jax.ShapeDtypeStruct((S, SGU_WIDTH), BF),
                  compiler_params=_params())(proj, proj, lnw, lnb, w, b_t, *extra)


def _sgu_bwd(dy, proj, lnw, lnb, w, b_t, dproj, *, name, tm=256):
    S = proj.shape[0]
    tm = _tile(S, tm, SGU_CHUNK)

    def body(dy_ref, zu_ref, zv_ref, lnw_ref, lnb_ref, w_ref, bt_ref, _, dz_ref, dlnw_ref, dlnb_ref, dw_ref, dbt_ref,
             f_s, dvn_s):
        first = pl.program_id(0) == 0

        @pl.when(first)
        def _():
            dw_ref[...] = jnp.zeros_like(dw_ref)
            dbt_ref[...] = jnp.zeros_like(dbt_ref)
        u, vjp_u = jax.vjp(_gelu, zu_ref[...].astype(F32))
        vn, vjp_v = jax.vjp(_ln_gelu_fn, zv_ref[...].astype(F32), lnw_ref[...], lnb_ref[...])
        vn = vn.astype(BF)
        dy_v = dy_ref[...]
        df = (dy_v * u).astype(BF)
        mask = _tril_mask()
        for g in range(SGU_GROUPS):
            wm = jnp.where(mask, w_ref[g], 0.0).astype(BF)
            cols = slice(g * 128, (g + 1) * 128)
            dwg = jnp.zeros((128, 128), F32)
            dbg = jnp.zeros((128, 1), F32)
            for ci in range(tm // SGU_CHUNK):
                rows = slice(ci * SGU_CHUNK, (ci + 1) * SGU_CHUNK)
                vn_c = vn[rows, cols]
                df_c = df[rows, cols]
                f_s[rows, cols] = jnp.dot(wm, vn_c, preferred_element_type=F32) + bt_ref[:, g:g + 1]
                dvn_s[rows, cols] = lax.dot_general(wm, df_c, (((0,), (0,)), ((), ())), preferred_element_type=F32)
                dwg = dwg + lax.dot_general(df_c, vn_c, (((1,), (1,)), ((), ())), preferred_element_type=F32)
                dbg = dbg + jnp.sum((dy_v[rows, cols] * u[rows, cols]), axis=1, keepdims=True)
            dw_ref[g] += jnp.where(mask, dwg, 0.0)
            dbt_ref[:, g:g + 1] += dbg
        (dzu,) = vjp_u(dy_v * f_s[...])
        dzv, dlnw, dlnb = vjp_v(dvn_s[...])
        dz_ref[:, :SGU_WIDTH] = dzu.astype(BF)
        dz_ref[:, SGU_WIDTH:] = dzv.astype(BF)
        _acc(dlnw_ref, dlnw)
        _acc(dlnb_ref, dlnb)
    vec = jax.ShapeDtypeStruct((1, SGU_WIDTH), F32)
    return _pcall(body, name=name, grid=(S // tm,),
                  in_specs=[_row_spec(tm, SGU_WIDTH),
                            pl.BlockSpec((tm, SGU_WIDTH), lambda i: (i, 0)), pl.BlockSpec((tm, SGU_WIDTH), lambda i: (i, 1)),
                            _vec_spec(SGU_WIDTH), _vec_spec(SGU_WIDTH),
                            pl.BlockSpec((SGU_GROUPS, 128, 128), lambda i: (0, 0, 0)),
                            pl.BlockSpec((128, SGU_GROUPS), lambda i: (0, 0)), ANY],
                  out_specs=[pl.BlockSpec((tm, 2 * SGU_WIDTH), lambda i: (i, P_Z // (2 * SGU_WIDTH))),
                             _vec_spec(SGU_WIDTH), _vec_spec(SGU_WIDTH),
                             pl.BlockSpec((SGU_GROUPS, 128, 128), lambda i: (0, 0, 0)),
                             pl.BlockSpec((128, SGU_GROUPS), lambda i: (0, 0))],
                  out_shape=[jax.ShapeDtypeStruct(dproj.shape, BF), vec, vec,
                             jax.ShapeDtypeStruct((SGU_GROUPS, 128, 128), F32),
                             jax.ShapeDtypeStruct((128, SGU_GROUPS), F32)],
                  scratch_shapes=[pltpu.VMEM((tm, SGU_WIDTH), F32), pltpu.VMEM((tm, SGU_WIDTH), F32)],
                  input_output_aliases={7: 0},
                  compiler_params=_params())(dy, proj, proj, lnw, lnb, w, b_t, dproj)


def _merge_fwd(y_sgu, y_attn, pa, pb, proj, *, name, after=None, tm=1024, tn=512):
    S, Dm = y_sgu.shape
    tm = _tile(S, tm, 8)
    nj = Dm // tn
    extra = [] if after is None else [after]

    def body(ys_ref, ya_ref, pa_ref, pb_ref, ga_ref, gb_ref, *rest):
        a_ref, b_ref, m_ref = rest[-3:]
        a = jnp.dot(ys_ref[...], pa_ref[...], preferred_element_type=F32)
        b = jnp.dot(ya_ref[...], pb_ref[...], preferred_element_type=F32)
        a_ref[...] = a.astype(BF)
        b_ref[...] = b.astype(BF)
        m_ref[...] = (_sigmoid(ga_ref[...].astype(F32)) * a + _sigmoid(gb_ref[...].astype(F32)) * b).astype(BF)
    row = pl.BlockSpec((tm, Dm), lambda i, j: (i, 0))
    col = pl.BlockSpec((Dm, tn), lambda i, j: (0, j))
    out = pl.BlockSpec((tm, tn), lambda i, j: (i, j))
    sh = jax.ShapeDtypeStruct((S, Dm), BF)
    return _pcall(body, name=name, grid=(S // tm, nj),
                  in_specs=[row, row, col, col, pl.BlockSpec((tm, tn), lambda i, j: (i, P_G // tn + j)),
                            pl.BlockSpec((tm, tn), lambda i, j: (i, (P_G + Dm) // tn + j))] + [ANY] * len(extra),
                  out_specs=[out, out, out], out_shape=[sh, sh, sh],
                  compiler_params=_params())(y_sgu, y_attn, pa, pb, proj, proj, *extra)


def _merge_bwd(do, w_out, a, b, proj, *, name, after=None, tm=512):
    S, Dm = a.shape
    tm = _tile(S, tm, 8)
    ga_blk, gb_blk = P_G // Dm, P_G // Dm + 1
    extra = [] if after is None else [after]

    def body(do_ref, w_ref, a_ref, b_ref, ga_ref, gb_ref, *rest):
        da_ref, db_ref, dg_ref = rest[-3:]
        dmv = lax.dot_general(do_ref[...], w_ref[...], (((1,), (1,)), ((), ())), preferred_element_type=F32)
        sa = _sigmoid(ga_ref[...].astype(F32))
        sb = _sigmoid(gb_ref[...].astype(F32))
        da_ref[...] = (dmv * sa).astype(BF)
        db_ref[...] = (dmv * sb).astype(BF)
        dg_ref[:, :Dm] = (dmv * a_ref[...].astype(F32) * sa * (1.0 - sa)).astype(BF)
        dg_ref[:, Dm:] = (dmv * b_ref[...].astype(F32) * sb * (1.0 - sb)).astype(BF)
    return _pcall(body, name=name, grid=(S // tm,),
                  in_specs=[_row_spec(tm, Dm), pl.BlockSpec((Dm, Dm), lambda i: (0, 0)), _row_spec(tm, Dm), _row_spec(tm, Dm),
                            pl.BlockSpec((tm, Dm), lambda i: (i, ga_blk)), pl.BlockSpec((tm, Dm), lambda i: (i, gb_blk))]
                  + [ANY] * len(extra),
                  out_specs=[_row_spec(tm, Dm), _row_spec(tm, Dm), pl.BlockSpec((tm, 2 * Dm), lambda i: (i, P_G // (2 * Dm)))],
                  out_shape=[jax.ShapeDtypeStruct((S, Dm), BF), jax.ShapeDtypeStruct((S, Dm), BF),
                             jax.ShapeDtypeStruct((S, IN_COLS), BF)],
                  compiler_params=_params())(do, w_out, a, b, proj, proj, *extra)


def _shift_rows(a, halo, k, up):
    n = a.shape[0]
    r8 = lax.broadcasted_iota(jnp.int32, (8, a.shape[1]), 0)
    if not up:
        rolled = pltpu.roll(a, k, 0)
        patch = jnp.where(r8 < k, pltpu.roll(halo, k, 0), rolled[:8])
        return jnp.concatenate([patch, rolled[8:]], axis=0)
    rolled = pltpu.roll(a, n - k, 0)
    patch = jnp.where(r8 >= 8 - k, pltpu.roll(halo, 8 - k, 0), rolled[n - 8:])
    return jnp.concatenate([rolled[:n - 8], patch], axis=0)


def _conv_taps(a, halo):
    return _shift_rows(a, halo, 2, False), _shift_rows(a, halo, 1, False), a


HALO = 16


def _prev_halo_spec(tm, Fd):
    return pl.BlockSpec((HALO, Fd), lambda i: (jnp.maximum(i * (tm // HALO) - 1, 0), 0))


def _conv_fwd(a_ref, halo_ref, cw_ref, cb_ref):
    halo = jnp.where(pl.program_id(0) > 0, halo_ref[...].astype(F32)[HALO - 8:], 0.0)
    t0, t1, t2 = _conv_taps(a_ref[...].astype(F32), halo)
    return t0, t1, t2, cb_ref[...] + cw_ref[0:1, :] * t0 + cw_ref[1:2, :] * t1 + cw_ref[2:3, :] * t2


def _ffn_act_fwd(a, up, cw, cb, *, name, tm=256):
    S, Fd = a.shape
    tm = _tile(S, tm, HALO)

    def body(a_ref, up_ref, halo_ref, cw_ref, cb_ref, o_ref):
        _, _, _, ac = _conv_fwd(a_ref, halo_ref, cw_ref, cb_ref)
        o_ref[...] = (ac * _sigmoid(ac) * up_ref[...].astype(F32)).astype(BF)
    return _pcall(body, name=name, grid=(S // tm,),
                  in_specs=[_row_spec(tm, Fd), _row_spec(tm, Fd), _prev_halo_spec(tm, Fd),
                            pl.BlockSpec((3, Fd), lambda i: (0, 0)), _vec_spec(Fd)],
                  out_specs=_row_spec(tm, Fd), out_shape=jax.ShapeDtypeStruct((S, Fd), BF),
                  compiler_params=_params())(a, up, a, cw, cb)


def _ffn_act_bwd_a(dhf, a, up, cw, cb, *, name, tm=256):
    S, Fd = a.shape
    tm = _tile(S, tm, HALO)

    def body(dhf_ref, a_ref, up_ref, halo_ref, cw_ref, cb_ref, dac_ref, dup_ref, dcw_ref, dcb_ref):
        t0, t1, t2, ac = _conv_fwd(a_ref, halo_ref, cw_ref, cb_ref)
        s = _sigmoid(ac)
        dhf_v = dhf_ref[...].astype(F32)
        dup_ref[...] = (dhf_v * ac * s).astype(BF)
        dac = dhf_v * up_ref[...].astype(F32) * (s * (1.0 + ac * (1.0 - s)))
        dac_ref[...] = dac.astype(BF)
        _acc(dcb_ref, jnp.sum(dac, axis=0, keepdims=True))
        _acc(dcw_ref, jnp.concatenate([jnp.sum(dac * t0, axis=0, keepdims=True),
                                       jnp.sum(dac * t1, axis=0, keepdims=True),
                                       jnp.sum(dac * t2, axis=0, keepdims=True)], axis=0))
    return _pcall(body, name=name, grid=(S // tm,),
                  in_specs=[_row_spec(tm, Fd), _row_spec(tm, Fd), _row_spec(tm, Fd), _prev_halo_spec(tm, Fd),
                            pl.BlockSpec((3, Fd), lambda i: (0, 0)), _vec_spec(Fd)],
                  out_specs=[_row_spec(tm, Fd), _row_spec(tm, Fd), pl.BlockSpec((3, Fd), lambda i: (0, 0)), _vec_spec(Fd)],
                  out_shape=[jax.ShapeDtypeStruct((S, Fd), BF), jax.ShapeDtypeStruct((S, Fd), BF),
                             jax.ShapeDtypeStruct((3, Fd), F32), jax.ShapeDtypeStruct((1, Fd), F32)],
                  compiler_params=_params())(dhf, a, up, a, cw, cb)


def _ffn_act_bwd_b(dac, cw, *, name, tm=256):
    S, Fd = dac.shape
    tm = _tile(S, tm, HALO)
    last = S // tm - 1

    def body(d_ref, halo_ref, cw_ref, o_ref):
        halo = jnp.where(pl.program_id(0) < last, halo_ref[...].astype(F32)[:8], 0.0)
        d = d_ref[...].astype(F32)
        o_ref[...] = (cw_ref[2:3, :] * d + cw_ref[1:2, :] * _shift_rows(d, halo, 1, True)
                      + cw_ref[0:1, :] * _shift_rows(d, halo, 2, True)).astype(BF)
    return _pcall(body, name=name, grid=(S // tm,),
                  in_specs=[_row_spec(tm, Fd),
                            pl.BlockSpec((HALO, Fd), lambda i: (jnp.minimum((i + 1) * (tm // HALO), S // HALO - 1), 0)),
                            pl.BlockSpec((3, Fd), lambda i: (0, 0))],
                  out_specs=_row_spec(tm, Fd), out_shape=jax.ShapeDtypeStruct((S, Fd), BF),
                  compiler_params=_params())(dac, dac, cw)


def _rope_tables(pos_col, inv_row, m1_row, m2_row):
    S = pos_col.shape[0]
    tm = _tile(S, 512, 8)

    def body(p_ref, inv_ref, m1_ref, m2_ref, c_ref, s1_ref, s2_ref):
        ang = p_ref[...] * inv_ref[...]
        sn = jnp.sin(ang)
        c_ref[...] = jnp.cos(ang)
        s1_ref[...] = -sn * m1_ref[...]
        s2_ref[...] = sn * m2_ref[...]
    sh = jax.ShapeDtypeStruct((S, 128), F32)
    return _pcall(body, name="rope_tables", grid=(S // tm,),
                  in_specs=[pl.BlockSpec((tm, 1), lambda i: (i, 0)), _vec_spec(128), _vec_spec(128), _vec_spec(128)],
                  out_specs=[_row_spec(tm, 128)] * 3, out_shape=[sh, sh, sh], compiler_params=_params())(
                      pos_col, inv_row, m1_row, m2_row)


def _rope_apply(x, c, s1, s2):
    outs = []
    for j in range(x.shape[1] // 128):
        xj = x[:, j * 128:(j + 1) * 128]
        outs.append(xj * c + pltpu.roll(xj, 120, 1) * s1 + pltpu.roll(xj, 8, 1) * s2)
    return outs[0] if len(outs) == 1 else jnp.concatenate(outs, axis=1)


def _rope_apply_t(d, c, s1, s2):
    outs = []
    for j in range(d.shape[1] // 128):
        dj = d[:, j * 128:(j + 1) * 128]
        outs.append(dj * c + pltpu.roll(dj * s1, 8, 1) + pltpu.roll(dj * s2, 120, 1))
    return outs[0] if len(outs) == 1 else jnp.concatenate(outs, axis=1)


def _rope_fwd(proj, c, s1, s2, *, name, tm=512):
    S = proj.shape[0]
    tm = _tile(S, tm, 8)

    def body(q_ref, k_ref, v_ref, c_ref, s1_ref, s2_ref, qo_ref, ko_ref, vo_ref):
        cv, s1v, s2v = c_ref[...], s1_ref[...], s2_ref[...]
        qo_ref[...] = (_rope_apply(q_ref[...].astype(F32), cv, s1v, s2v) * (HEAD_DIM ** -0.5)).astype(BF)
        ko_ref[...] = _rope_apply(k_ref[...].astype(F32), cv, s1v, s2v).astype(BF)
        vo_ref[...] = v_ref[...].astype(BF)
    return _pcall(body, name=name, grid=(S // tm,),
                  in_specs=[pl.BlockSpec((tm, Q_END), lambda i: (i, P_Q // Q_END)),
                            pl.BlockSpec((tm, 128), lambda i: (i, P_K // 128)),
                            pl.BlockSpec((tm, 128), lambda i: (i, P_V // 128)),
                            _row_spec(tm, 128), _row_spec(tm, 128), _row_spec(tm, 128)],
                  out_specs=[_row_spec(tm, Q_END), _row_spec(tm, 128), _row_spec(tm, 128)],
                  out_shape=[jax.ShapeDtypeStruct((S, Q_END), BF), jax.ShapeDtypeStruct((S, 128), BF),
                             jax.ShapeDtypeStruct((S, 128), BF)],
                  compiler_params=_params())(proj, proj, proj, c, s1, s2)


def _rope_bwd(dq, dk, dv, c, s1, s2, dproj, *, name, tm=512):
    S = dq.shape[0]
    tm = _tile(S, tm, 8)
    tabs = [_row_spec(tm, 128)] * 3
    shape = jax.ShapeDtypeStruct(dproj.shape, BF)

    def body_q(dq_ref, c_ref, s1_ref, s2_ref, _, o_ref):
        o_ref[...] = _rope_apply_t(dq_ref[...].astype(F32), c_ref[...], s1_ref[...], s2_ref[...]).astype(BF)
    dproj = _pcall(body_q, name=name + "_q", grid=(S // tm,), in_specs=[_row_spec(tm, Q_END)] + tabs + [ANY],
                   out_specs=pl.BlockSpec((tm, Q_END), lambda i: (i, P_Q // Q_END)), out_shape=shape,
                   input_output_aliases={4: 0}, compiler_params=_params())(dq, c, s1, s2, dproj)

    def body_kv(dk_ref, dv_ref, c_ref, s1_ref, s2_ref, _, o_ref):
        o_ref[:, :128] = _rope_apply_t(dk_ref[...], c_ref[...], s1_ref[...], s2_ref[...]).astype(BF)
        o_ref[:, 128:] = dv_ref[...].astype(BF)
    return _pcall(body_kv, name=name + "_kv", grid=(S // tm,),
                  in_specs=[_row_spec(tm, 128), _row_spec(tm, 128)] + tabs + [ANY],
                  out_specs=pl.BlockSpec((tm, 256), lambda i: (i, P_K // 256)), out_shape=shape,
                  input_output_aliases={5: 0}, compiler_params=_params())(dk, dv, c, s1, s2, dproj)


def _lane_lo(shape):
    return lax.broadcasted_iota(jnp.int32, shape, 1) < HEAD_DIM


def _stack_heads(x, g):
    lo = _lane_lo((ATTN_BLOCK, 128))
    zero = jnp.zeros((ATTN_BLOCK, 128), x.dtype)
    parts = []
    for p in range(Q_PER_KV // 2):
        xp = x[:, (g * 4 + p) * 128:(g * 4 + p + 1) * 128]
        parts += [jnp.where(lo, xp, zero), jnp.where(lo, zero, xp)]
    return jnp.concatenate(parts, axis=0)


def _unstack_heads(o2):
    lo = _lane_lo((ATTN_BLOCK, 128))
    return [jnp.where(lo, o2[2 * p * ATTN_BLOCK:(2 * p + 1) * ATTN_BLOCK], o2[(2 * p + 1) * ATTN_BLOCK:(2 * p + 2) * ATTN_BLOCK])
            for p in range(Q_PER_KV // 2)]


def _dup_half(prev, cur, g):
    x = jnp.concatenate([prev, cur], axis=0).astype(F32)
    lo = _lane_lo(x.shape)
    r = pltpu.roll(x, HEAD_DIM, 1)
    return (jnp.where(lo, x, r) if g == 0 else jnp.where(lo, r, x)).astype(BF)


def _fold_halves(x):
    return x + pltpu.roll(x, HEAD_DIM, 1)


def _attn_bias():
    i = lax.broadcasted_iota(jnp.int32, (Q_PER_KV * ATTN_BLOCK, 2 * ATTN_BLOCK), 0) & (ATTN_BLOCK - 1)
    j = lax.broadcasted_iota(jnp.int32, (Q_PER_KV * ATTN_BLOCK, 2 * ATTN_BLOCK), 1)
    band = (j > i) & (j <= i + ATTN_BLOCK)
    return jnp.stack([jnp.where(band & (j >= ATTN_BLOCK), 0.0, -jnp.inf), jnp.where(band, 0.0, -jnp.inf)]).astype(F32)


def _both(x):
    return jnp.concatenate([x, x], axis=1)


def _row_sums(x_bf):
    return jnp.dot(x_bf, jnp.ones((x_bf.shape[1], 128), BF), preferred_element_type=F32)


def _attn_probs(qs, kb, sink, bias):
    s = lax.dot_general(qs, kb, (((1,), (1,)), ((), ())), preferred_element_type=F32) + bias
    m = jnp.maximum(jnp.broadcast_to(jnp.max(s, axis=-1, keepdims=True), sink.shape), sink)
    return jnp.exp(s - _both(m)), jnp.exp(sink - m)


def _attn_specs(S):
    nb = S // ATTN_BLOCK
    qs = pl.BlockSpec((ATTN_BLOCK, Q_END), lambda n: (n, 0))
    cur = pl.BlockSpec((ATTN_BLOCK, 128), lambda n: (n, 0))
    prev = pl.BlockSpec((ATTN_BLOCK, 128), lambda n: (jnp.maximum(n - 1, 0), 0))
    sink = pl.BlockSpec((N_KV_HEADS, Q_PER_KV * ATTN_BLOCK, 128), lambda n: (0, 0, 0))
    bias = pl.BlockSpec((None, Q_PER_KV * ATTN_BLOCK, 2 * ATTN_BLOCK), lambda n: (jnp.minimum(n, 1), 0, 0))
    return nb, qs, cur, prev, sink, bias


def _attn_fwd(q, k, v, sink_rows, bias, *, name):
    S = q.shape[0]
    nb, qs, cur, prev, sink, bs = _attn_specs(S)

    def body(q_ref, kp_ref, kc_ref, vp_ref, vc_ref, sk_ref, b_ref, o_ref):
        for g in range(N_KV_HEADS):
            kb = _dup_half(kp_ref[...], kc_ref[...], g)
            vb = _dup_half(vp_ref[...], vc_ref[...], g)
            p, es = _attn_probs(_stack_heads(q_ref[...], g), kb, sk_ref[g], b_ref[...])
            ones = jnp.ones((2 * ATTN_BLOCK, 128), BF)
            o3 = jnp.dot(p.astype(BF), jnp.concatenate([vb, ones], axis=1), preferred_element_type=F32)
            o2 = o3[:, :128] / (o3[:, 128:] + es)
            for t, tile in enumerate(_unstack_heads(o2)):
                o_ref[:, (g * 4 + t) * 128:(g * 4 + t + 1) * 128] = tile.astype(BF)
    return _pcall(body, name=name, grid=(nb,), in_specs=[qs, prev, cur, prev, cur, sink, bs], out_specs=qs,
                  out_shape=jax.ShapeDtypeStruct(q.shape, BF), compiler_params=_params())(q, k, k, v, v, sink_rows, bias)


def _attn_bwd(do, q, k, v, sink_rows, bias, *, name):
    S = q.shape[0]
    nb, qs, cur, prev, sink, bs = _attn_specs(S)
    full = pl.BlockSpec((S, 128), lambda n: (0, 0))
    dsk_spec = pl.BlockSpec((N_KV_HEADS, Q_PER_KV, 128), lambda n: (0, 0, 0))

    def body(do_ref, q_ref, kp_ref, kc_ref, vp_ref, vc_ref, sk_ref, b_ref, dq_ref, dk_ref, dv_ref, dsk_ref):
        n = pl.program_id(0)

        @pl.when(n == 0)
        def _():
            dk_ref[...] = jnp.zeros_like(dk_ref)
            dv_ref[...] = jnp.zeros_like(dv_ref)
            dsk_ref[...] = jnp.zeros_like(dsk_ref)
        sub = lax.broadcasted_iota(jnp.int32, (Q_PER_KV, 128), 0)
        dkf, dvf = [], []
        for g in range(N_KV_HEADS):
            qst = _stack_heads(q_ref[...], g)
            dos = _stack_heads(do_ref[...], g)
            kb = _dup_half(kp_ref[...], kc_ref[...], g)
            vb = _dup_half(vp_ref[...], vc_ref[...], g)
            pu, es = _attn_probs(qst, kb, sk_ref[g], b_ref[...])
            inv = 1.0 / (_row_sums(pu.astype(BF)) + es)
            p = pu * _both(inv)
            dp = lax.dot_general(dos, vb, (((1,), (1,)), ((), ())), preferred_element_type=F32)
            dd = _row_sums((p * dp).astype(BF))
            ds = (p * (dp - _both(dd))).astype(BF)
            dq2 = jnp.dot(ds, kb, preferred_element_type=F32) * (HEAD_DIM ** -0.5)
            for t, tile in enumerate(_unstack_heads(dq2)):
                dq_ref[:, (g * 4 + t) * 128:(g * 4 + t + 1) * 128] = tile.astype(BF)
            dkf.append(_fold_halves(lax.dot_general(ds, qst, (((0,), (0,)), ((), ())), preferred_element_type=F32)))
            dvf.append(_fold_halves(lax.dot_general(p.astype(BF), dos, (((0,), (0,)), ((), ())),
                                                    preferred_element_type=F32)))
            dsr = -(es * inv * dd)
            upd = jnp.zeros((Q_PER_KV, 128), F32)
            for h in range(Q_PER_KV):
                upd = jnp.where(sub == h, jnp.sum(dsr[h * ATTN_BLOCK:(h + 1) * ATTN_BLOCK], axis=0, keepdims=True), upd)
            dsk_ref[g] += upd
        lo = _lane_lo((2 * ATTN_BLOCK, 128))
        dkb = jnp.where(lo, dkf[0], dkf[1])
        dvb = jnp.where(lo, dvf[0], dvf[1])
        r0 = pl.multiple_of(n * ATTN_BLOCK, ATTN_BLOCK)
        dk_ref[pl.ds(r0, ATTN_BLOCK), :] += dkb[ATTN_BLOCK:]
        dv_ref[pl.ds(r0, ATTN_BLOCK), :] += dvb[ATTN_BLOCK:]

        @pl.when(n > 0)
        def _():
            rp = pl.multiple_of((n - 1) * ATTN_BLOCK, ATTN_BLOCK)
            dk_ref[pl.ds(rp, ATTN_BLOCK), :] += dkb[:ATTN_BLOCK]
            dv_ref[pl.ds(rp, ATTN_BLOCK), :] += dvb[:ATTN_BLOCK]
    return _pcall(body, name=name, grid=(nb,), in_specs=[qs, qs, prev, cur, prev, cur, sink, bs],
                  out_specs=[qs, full, full, dsk_spec],
                  out_shape=[jax.ShapeDtypeStruct(q.shape, BF), jax.ShapeDtypeStruct((S, 128), F32),
                             jax.ShapeDtypeStruct((S, 128), F32), jax.ShapeDtypeStruct((N_KV_HEADS, Q_PER_KV, 128), F32)],
                  compiler_params=_params())(do, q, k, k, v, v, sink_rows, bias)


def _ada_fwd(c_all, ada_w):
    ncol = ada_w.shape[2]

    def body(c_ref, w_ref, o_ref):
        cv = c_ref[...]
        ca = (cv * _sigmoid(cv)).astype(BF)
        for l in range(DEPTH):
            o_ref[:, l * ncol:(l + 1) * ncol] = jnp.dot(ca, w_ref[l].astype(BF), preferred_element_type=F32)
    return _pcall(body, name="ada_fwd", out_shape=jax.ShapeDtypeStruct((N_DEV, DEPTH * ncol), F32),
                  compiler_params=_params())(c_all, ada_w)


def _ada_bwd(c_all, dm):
    ncol = dm.shape[2]

    def body(c_ref, dm_ref, o_ref):
        cv = c_ref[...]
        ca = (cv * _sigmoid(cv)).astype(BF)
        for l in range(DEPTH):
            o_ref[l] = lax.dot_general(ca, dm_ref[l].astype(BF), (((0,), (0,)), ((), ())), preferred_element_type=F32)
    return _pcall(body, name="ada_bwd", out_shape=jax.ShapeDtypeStruct((DEPTH, D_MODEL, ncol), F32),
                  compiler_params=_params())(c_all, dm)


def _adamw(w, g, m, v, *, name):
    R, C = w.shape
    tr = R
    for t in range(8, 513, 8):
        if R % t == 0:
            tr = t
    c1 = 1.0 - ADAM_B1 ** ADAM_STEP
    c2 = 1.0 - ADAM_B2 ** ADAM_STEP

    def body(w_ref, g_ref, m_ref, v_ref, d_ref, mo_ref, vo_ref):
        gv = g_ref[...]
        mn = ADAM_B1 * m_ref[...] + (1.0 - ADAM_B1) * gv
        vn = ADAM_B2 * v_ref[...] + (1.0 - ADAM_B2) * (gv * gv)
        mo_ref[...] = mn
        vo_ref[...] = vn
        d_ref[...] = -ADAM_LR * ((mn * (1.0 / c1)) / (jnp.sqrt(vn * (1.0 / c2)) + ADAM_EPS) + ADAM_WD * w_ref[...])
    spec = pl.BlockSpec((tr, C), lambda i: (i, 0))
    sh = jax.ShapeDtypeStruct((R, C), F32)
    return _pcall(body, name=name, grid=(R // tr,), in_specs=[spec] * 4, out_specs=[spec] * 3, out_shape=[sh, sh, sh],
                  compiler_params=_params())(w, g, m, v)


def _adamw_layers(w, g_layers, m, v, *, name):
    L, R, C = w.shape
    assert L == 2 and len(g_layers) == 2
    tr = R
    for t in range(8, 513, 8):
        if R % t == 0:
            tr = t
    c1 = 1.0 - ADAM_B1 ** ADAM_STEP
    c2 = 1.0 - ADAM_B2 ** ADAM_STEP

    def body(w_ref, g0_ref, g1_ref, m_ref, v_ref, go_ref, d_ref, mo_ref, vo_ref):
        gv = jnp.where(pl.program_id(0) == 0, g0_ref[...], g1_ref[...])
        go_ref[...] = gv
        mn = ADAM_B1 * m_ref[...] + (1.0 - ADAM_B1) * gv
        vn = ADAM_B2 * v_ref[...] + (1.0 - ADAM_B2) * (gv * gv)
        mo_ref[...] = mn
        vo_ref[...] = vn
        d_ref[...] = -ADAM_LR * ((mn * (1.0 / c1)) / (jnp.sqrt(vn * (1.0 / c2)) + ADAM_EPS) + ADAM_WD * w_ref[...])
    spec = pl.BlockSpec((None, tr, C), lambda l, i: (l, i, 0))
    sh = jax.ShapeDtypeStruct((L, R, C), F32)
    g_specs = [pl.BlockSpec((tr, C), lambda l, i, k=k: (jnp.where(l == k, i, 0), 0)) for k in range(L)]
    return _pcall(body, name=name, grid=(L, R // tr), in_specs=[spec] + g_specs + [spec, spec], out_specs=[spec] * 4,
                  out_shape=[sh] * 4, compiler_params=_params())(w, *g_layers, m, v)


def _sum8(parts, *, name):
    _, R, C = parts.shape
    tr = _tile(R, 512, 16)

    def body(p_ref, o_ref):
        acc = p_ref[0].astype(F32)
        for k in range(1, N_DEV):
            acc = acc + p_ref[k].astype(F32)
        o_ref[...] = acc
    return _pcall(body, name=name, grid=(R // tr,), in_specs=[pl.BlockSpec((N_DEV, tr, C), lambda i: (0, i, 0))],
                  out_specs=pl.BlockSpec((tr, C), lambda i: (i, 0)), out_shape=jax.ShapeDtypeStruct((R, C), F32),
                  compiler_params=_params())(parts)


MESH_ID = pl.DeviceIdType.MESH
ANY = pl.BlockSpec(memory_space=pl.ANY)


def _all_gather(x, *, name, after=None):
    R, C = x.shape
    extra = [] if after is None else [after]

    def body(x_ref, *rest):
        out_ref, send_sems, recv_sems, local_sem = rest[-4:]
        mx, my, mc = lax.axis_index("x"), lax.axis_index("y"), lax.axis_index("c")
        me, sibling = (mx, my, mc), (mx, my, 1 - mc)
        chips = [(1 - mx, my), (mx, 1 - my), (1 - mx, 1 - my)]

        def blk(px, py, pc):
            return out_ref.at[4 * px + 2 * py + pc]

        def copy(k, block, to, src=None):
            return pltpu.make_async_remote_copy(
                src_ref=blk(*block) if src is None else src, dst_ref=blk(*block),
                send_sem=send_sems.at[k], recv_sem=recv_sems.at[k], device_id=to, device_id_type=MESH_ID)

        mine = pltpu.make_async_copy(x_ref, blk(*me), local_sem)
        mine.start()
        first = [copy(0, me, sibling, src=x_ref)]
        first += [copy(1 + j, me, (*chip, mc), src=x_ref) for j, chip in enumerate(chips)]
        for cp in first:
            cp.start()
        passed = [copy(4 + j, (*chip, mc), sibling) for j, chip in enumerate(chips)]
        for j, chip in enumerate(chips):
            copy(1 + j, (*chip, mc), me).wait_recv()
            passed[j].start()
        copy(0, sibling, me).wait_recv()
        for j, chip in enumerate(chips):
            copy(4 + j, (*chip, 1 - mc), me).wait_recv()
        for cp in first + passed:
            cp.wait_send()
        mine.wait()
    return _pcall(body, name=name, in_specs=[ANY] * (1 + len(extra)), out_specs=ANY,
                  out_shape=jax.ShapeDtypeStruct((N_DEV, R, C), x.dtype),
                  scratch_shapes=[pltpu.SemaphoreType.DMA((7,)), pltpu.SemaphoreType.DMA((7,)), pltpu.SemaphoreType.DMA],
                  compiler_params=pltpu.CompilerParams(has_side_effects=True))(x, *extra)


HBM_SPEC = pl.BlockSpec(memory_space=pltpu.HBM)
SEM_SPEC = pl.BlockSpec(memory_space=pltpu.SEMAPHORE)
DATAFLOW = pltpu.SideEffectType.DATAFLOW_SIDE_EFFECTING


def _coords():
    return lax.axis_index("x"), lax.axis_index("y"), lax.axis_index("c")


def _other_chips(mx, my):
    return [(1 - mx, my), (mx, 1 - my), (1 - mx, 1 - my)]


def _plan_gather_ici(refs, send, recv):
    src, land = refs
    mx, my, mc = _coords()
    return [pltpu.make_async_remote_copy(src_ref=src, dst_ref=land.at[mc, 2 * mx + my], send_sem=send[j], recv_sem=recv[j],
                                         device_id=(px, py, mc), device_id_type=MESH_ID)
            for j, (px, py) in enumerate(_other_chips(mx, my))]


def _plan_gather_d2d(refs, send, recv):
    (land,) = refs
    mx, my, mc = _coords()
    return [pltpu.make_async_remote_copy(src_ref=land.at[mc], dst_ref=land.at[mc], send_sem=send[0], recv_sem=recv[0],
                                         device_id=(mx, my, 1 - mc), device_id_type=MESH_ID)]


def _plan_reduce_d2d(refs, send, recv):
    g, land = refs
    mx, my, mc = _coords()
    return [pltpu.make_async_remote_copy(src_ref=g.at[1 - mc], dst_ref=land, send_sem=send[0], recv_sem=recv[0],
                                         device_id=(mx, my, 1 - mc), device_id_type=MESH_ID)]


def _plan_reduce_ici(refs, send, recv):
    h, land = refs
    mx, my, mc = _coords()
    return [pltpu.make_async_remote_copy(src_ref=h.at[2 * px + py], dst_ref=land.at[j], send_sem=send[j], recv_sem=recv[j],
                                         device_id=(px, py, mc), device_id_type=MESH_ID)
            for j, (px, py) in enumerate(_other_chips(mx, my))]


def _rdma_start(bufs, n, plan, *, name, after=None):
    nb = len(bufs)
    extra = [] if after is None else [after]
    ne = len(extra)

    def body(*refs):
        ins, send, recv = refs[:nb], refs[nb + ne:nb + ne + n], refs[nb + ne + n:nb + ne + 2 * n]
        token = refs[-1]
        for cp in plan(ins, send, recv):
            cp.start()
        token[...] = jnp.zeros_like(token)
    out = _pcall(body, name=name,
                 out_shape=tuple([pltpu.SemaphoreType.DMA(())] * (2 * n) + [pltpu.HBM(b.shape, b.dtype) for b in bufs]
                                 + [jax.ShapeDtypeStruct((8, 128), F32)]),
                 in_specs=tuple([HBM_SPEC] * nb + [ANY] * ne),
                 out_specs=tuple([SEM_SPEC] * (2 * n) + [HBM_SPEC] * nb + [pl.BlockSpec(memory_space=pltpu.VMEM)]),
                 input_output_aliases={i: 2 * n + i for i in range(nb)},
                 compiler_params=pltpu.CompilerParams(has_side_effects=DATAFLOW))(
                     *[pltpu.with_memory_space_constraint(b, pltpu.HBM) for b in bufs], *extra)
    return list(out[:2 * n]), list(out[2 * n:2 * n + nb]), out[-1]


def _rdma_wait(sems, bufs, n, plan, after, *, name):
    nb = len(bufs)

    def body(*refs):
        ins, send, recv = refs[:nb], refs[nb:nb + n], refs[nb + n:nb + 2 * n]
        for cp in plan(ins, send, recv):
            cp.wait_send()
            cp.wait_recv()
    out = _pcall(body, name=name, out_shape=tuple(pltpu.HBM(b.shape, b.dtype) for b in bufs),
                 in_specs=tuple([HBM_SPEC] * nb + [SEM_SPEC] * (2 * n) + [ANY]), out_specs=tuple([HBM_SPEC] * nb),
                 input_output_aliases={i: i for i in range(nb)},
                 compiler_params=pltpu.CompilerParams(has_side_effects=DATAFLOW))(*bufs, *sems, after)
    return list(out)


def _sum_pair(g, land, cidx, *, name):
    _, nchip, R, C = g.shape
    tr = _tile(R, 1056, 16)

    def body(c_ref, g_ref, l_ref, o_ref):
        o_ref[...] = g_ref[...] + l_ref[...]
    grid_spec = pltpu.PrefetchScalarGridSpec(
        num_scalar_prefetch=1, grid=(nchip, R // tr),
        in_specs=[pl.BlockSpec((None, None, tr, C), lambda p, i, c_ref: (c_ref[0], p, i, 0)),
                  pl.BlockSpec((None, tr, C), lambda p, i, c_ref: (p, i, 0))],
        out_specs=pl.BlockSpec((None, tr, C), lambda p, i, c_ref: (p, i, 0)))
    return _pcall(body, name=name, grid_spec=grid_spec, out_shape=jax.ShapeDtypeStruct((nchip, R, C), BF),
                  compiler_params=_params())(cidx, g, land)


def _sum_chips(h, land, chipidx, *, name):
    _, R, C = h.shape
    tr = _tile(R, 1056, 16)

    def body(c_ref, h_ref, l_ref, o_ref):
        acc = h_ref[...].astype(F32)
        for j in range(3):
            acc = acc + l_ref[j].astype(F32)
        o_ref[...] = acc
    grid_spec = pltpu.PrefetchScalarGridSpec(
        num_scalar_prefetch=1, grid=(R // tr,),
        in_specs=[pl.BlockSpec((None, tr, C), lambda i, c_ref: (c_ref[0], i, 0)),
                  pl.BlockSpec((3, tr, C), lambda i, c_ref: (0, i, 0))],
        out_specs=pl.BlockSpec((tr, C), lambda i, c_ref: (i, 0)))
    return _pcall(body, name=name, grid_spec=grid_spec, out_shape=jax.ShapeDtypeStruct((R, C), F32),
                  compiler_params=_params())(chipidx, h, land)


PART_IN = ("w_in",)
PART_MIX = ("proj_a", "proj_b", "w_out")
PART_FFN = ("ffn_w_gate", "ffn_w_up", "ffn_w_down")


def _part_rows(names):
    return sum(BIG_ROWS[n] for n in names)


def _part_offsets(names):
    off, r = {}, 0
    for n in names:
        off[n] = r
        r += BIG_ROWS[n]
    return off


def _pack_shards(shards, l, names):
    return jnp.concatenate([(shards[n][l].T if n in COL_SHARDED else shards[n][l]).astype(BF) for n in names], axis=0)


def _unpack_weights(full8, names):
    off = _part_offsets(names)

    def whole(n):
        return full8[:, off[n]:off[n] + BIG_ROWS[n], :].reshape(N_DEV * BIG_ROWS[n], 1024)
    out = {}
    if "w_in" in names:
        wt_in = whole("w_in")
        out["wt_in"] = jnp.concatenate([wt_in[V_END:], wt_in[:V_END]], axis=0)
    for n in ("proj_a", "proj_b", "w_out"):
        if n in names:
            out[n] = whole(n)
    if "ffn_w_gate" in names:
        out["wt_gate"], out["wt_up"], out["w_down"] = whole("ffn_w_gate"), whole("ffn_w_up"), whole("ffn_w_down")
    return out


def _from_land(land):
    return land.transpose(1, 0, 2, 3).reshape(N_DEV, land.shape[2], 1024)


def _pack_grads(wg, names):
    full = {"proj_a": wg.get("proj_a"), "proj_b": wg.get("proj_b"), "w_out": wg.get("w_out"), "ffn_w_down": wg.get("w_down"),
            "ffn_w_gate": wg.get("wt_gate"), "ffn_w_up": wg.get("wt_up")}
    if "w_in" in names:
        full["w_in"] = jnp.concatenate([wg["wt_in"][P_Q:], wg["wt_in"][:P_Q]], axis=0)
    blocks = jnp.concatenate([full[n].reshape(N_DEV, BIG_ROWS[n], 1024) for n in names], axis=1)
    return blocks.reshape(4, 2, _part_rows(names), 1024).transpose(1, 0, 2, 3)


def _unpack_shard_grads(gs, names):
    off = _part_offsets(names)
    out = {}
    for n in names:
        blk = gs[off[n]:off[n] + BIG_ROWS[n]]
        out[n] = blk.T if n in COL_SHARDED else blk
    return out


def _rope_setup(positions):
    S = positions.shape[0]
    inv = ROPE_THETA ** (-jnp.arange(0, ROT_DIM, 2, dtype=F32) / ROT_DIM)
    lane = np.arange(128) % HEAD_DIM
    half = ROT_DIM // 2
    inv_row = jnp.where(lane < ROT_DIM, jnp.tile(inv, 128 // half), 0.0)[None, :].astype(F32)
    m1_row = jnp.asarray((lane < half).astype(np.float32))[None, :]
    m2_row = jnp.asarray(((lane >= half) & (lane < ROT_DIM)).astype(np.float32))[None, :]
    return (*_rope_tables(positions.astype(F32).reshape(S, 1), inv_row, m1_row, m2_row), _attn_bias())


def _hook(hooks, point, after):
    f = None if hooks is None else hooks.get(point)
    return None if f is None else f(after)


def _layer_fwd(l, x, mod_l, W, small, rope, hooks=None):
    rc, rs1, rs2, bias = rope
    sh1, sc1, g1, sh2, sc2, g2 = [mod_l[i * D_MODEL:(i + 1) * D_MODEL][None, :] for i in range(6)]
    nw1, nw2 = small["norm1_w"][l][None, :], small["norm2_w"][l][None, :]
    tok = _hook(hooks, "mm_in", x)
    h, (proj,) = _norm_mm(x, nw1, sc1, sh1, [W["wt_in"]], name=f"mm_in{l}", after=tok, tn_cap=768)
    q_r, k_r, v_b = _rope_fwd(proj, rc, rs1, rs2, name=f"rope_fwd{l}")
    sink_rows = jnp.repeat(small["attn_sinks"][l].reshape(N_KV_HEADS, Q_PER_KV), ATTN_BLOCK, axis=1)
    sink_rows = jnp.broadcast_to(sink_rows[..., None], sink_rows.shape + (128,))
    y_attn = _attn_fwd(q_r, k_r, v_b, sink_rows, bias, name=f"attn_fwd{l}")
    lnw, lnb = small["sgu_ln_w"][l][None, :], small["sgu_ln_b"][l][None, :]
    sgu_bt = small["sgu_b"][l].T
    y_sgu = _sgu_fwd(proj, lnw, lnb, small["sgu_w"][l], sgu_bt, name=f"sgu_fwd{l}", after=_hook(hooks, "sgu", y_attn))
    tok = _hook(hooks, "mm_pa", y_sgu)
    a_br, b_br, merged = _merge_fwd(y_sgu, y_attn, W["proj_a"], W["proj_b"], proj, name=f"merge_fwd{l}", after=tok)
    x1, o1 = _mm(merged, W["w_out"], nt=False, out_dtype=F32, name=f"mm_out{l}", res=x, gvec=g1)
    h2, (a_g, a_u) = _norm_mm(x1, nw2, sc2, sh2, [W["wt_gate"], W["wt_up"]], name=f"mm_gu{l}", tn_cap=1408)
    cw, cb = small["ffn_conv_w"][l], small["ffn_conv_b"][l][None, :]
    hf = _ffn_act_fwd(a_g, a_u, cw, cb, name=f"ffn_act_fwd{l}")
    x2, o2 = _mm(hf, W["w_down"], nt=False, out_dtype=F32, name=f"mm_down{l}", res=x1, gvec=g2)
    saved = dict(x=x, h=h, proj=proj, q_r=q_r, k_r=k_r, v_b=v_b, sink_rows=sink_rows, y_attn=y_attn, y_sgu=y_sgu,
                 a_br=a_br, b_br=b_br, merged=merged, x1=x1, o1=o1, h2=h2, a_g=a_g, a_u=a_u, hf=hf, o2=o2)
    return x2, saved


def _layer_bwd(l, dx, do2, dg2, mod_l, W, small, rope, sv, below=None, hooks=None, wg=None):
    rc, rs1, rs2, bias = rope
    sh1, sc1, g1, sh2, sc2, g2 = [mod_l[i * D_MODEL:(i + 1) * D_MODEL][None, :] for i in range(6)]
    nw1, nw2 = small["norm1_w"][l][None, :], small["norm2_w"][l][None, :]
    cw, cb = small["ffn_conv_w"][l], small["ffn_conv_b"][l][None, :]
    lnw, lnb = small["sgu_ln_w"][l][None, :], small["sgu_ln_b"][l][None, :]
    sgu_bt = small["sgu_b"][l].T
    wg = {} if wg is None else wg
    dhf = _mm(do2, W["w_down"], nt=True, out_dtype=BF, name=f"mm_down_dx{l}", after=_hook(hooks, "mm_down_dx", do2),
              tn_cap=1408)
    wg["w_down"] = _mm_tn(sv["hf"], do2, name=f"mm_down_dw{l}")
    dac, dup, dcw, dcb = _ffn_act_bwd_a(dhf, sv["a_g"], sv["a_u"], cw, cb, name=f"ffn_act_bwd_a{l}")
    da = _ffn_act_bwd_b(dac, cw, name=f"ffn_act_bwd_b{l}")
    dh2 = _mm([da, dup], [W["wt_gate"], W["wt_up"]], nt=False, out_dtype=F32, name=f"mm_gu_dx{l}",
              after=_hook(hooks, "mm_gu_dx", da))
    wg["wt_gate"] = _mm_tn(da, sv["h2"], name=f"mm_gate_dw{l}")
    wg["wt_up"] = _mm_tn(dup, sv["h2"], name=f"mm_up_dw{l}")
    dx1, dnw2, dsc2, dsh2, do1, dg1 = _normmod_bwd(dh2, sv["x1"], nw2, sc2, sh2, dx, (sv["o1"], g1), name=f"normmod2_bwd{l}")
    d_a, d_b, dproj = _merge_bwd(do1, W["w_out"], sv["a_br"], sv["b_br"], sv["proj"], name=f"merge_bwd{l}",
                                 after=_hook(hooks, "merge_bwd", do1))
    wg["w_out"] = _mm_tn(sv["merged"], do1, name=f"mm_out_dw{l}")
    dysgu = _mm(d_a, W["proj_a"], nt=True, out_dtype=F32, name=f"mm_pa_dx{l}", after=_hook(hooks, "mm_pa_dx", d_a))
    dyattn = _mm(d_b, W["proj_b"], nt=True, out_dtype=BF, name=f"mm_pb_dx{l}")
    wg["proj_a"] = _mm_tn(sv["y_sgu"], d_a, name=f"mm_pa_dw{l}")
    wg["proj_b"] = _mm_tn(sv["y_attn"], d_b, name=f"mm_pb_dw{l}")
    dproj, dlnw, dlnb, dsguw, dsgubt = _sgu_bwd(dysgu, sv["proj"], lnw, lnb, small["sgu_w"][l], sgu_bt, dproj,
                                                name=f"sgu_bwd{l}")
    dq_r, dk_r, dv_b, dsk = _attn_bwd(dyattn, sv["q_r"], sv["k_r"], sv["v_b"], sv["sink_rows"], bias, name=f"attn_bwd{l}")
    dproj = _rope_bwd(dq_r, dk_r, dv_b, rc, rs1, rs2, dproj, name=f"rope_bwd{l}")
    wg["wt_in"] = _mm_tn(dproj, sv["h"], name=f"mm_in_dw{l}")
    dh = _mm(dproj, W["wt_in"], nt=False, out_dtype=F32, name=f"mm_in_dx{l}", after=_hook(hooks, "mm_in_dx", wg["wt_in"]))
    dx0, dnw1, dsc1, dsh1, *gate_below = _normmod_bwd(dh, sv["x"], nw1, sc1, sh1, dx1, below, name=f"normmod1_bwd{l}")
    dmod = jnp.concatenate([dsh1, dsc1, dg1, dsh2, dsc2, dg2], axis=1)[0]
    sg = {"norm1_w": dnw1[0], "norm2_w": dnw2[0], "attn_sinks": dsk[:, :, 0].reshape(N_Q_HEADS),
          "sgu_ln_w": dlnw[0], "sgu_ln_b": dlnb[0], "sgu_w": dsguw, "sgu_b": dsgubt.T,
          "ffn_conv_w": dcw, "ffn_conv_b": dcb[0]}
    return (dx0, *gate_below), wg, sg, dmod


SMALL = ("ada_b", "norm1_w", "attn_sinks", "sgu_ln_w", "sgu_ln_b", "sgu_w", "sgu_b", "norm2_w", "ffn_conv_b", "final_norm_w")
WEIGHT_ORDER = ("ada_w", "ada_b", "norm1_w", "w_in", "attn_sinks", "sgu_ln_w", "sgu_ln_b", "sgu_w", "sgu_b", "proj_a", "proj_b",
                "w_out", "norm2_w", "ffn_w_gate", "ffn_w_up", "ffn_conv_w", "ffn_conv_b", "ffn_w_down", "final_norm_w")


def _flat_pack(arrs, rows):
    flat = jnp.concatenate([a.reshape(-1) for a in arrs])
    return jnp.pad(flat, (0, rows * 1024 - flat.shape[0])).reshape(rows, 1024)


def _flat_unpack(buf, shapes):
    flat = buf.reshape(-1)
    out, o = [], 0
    for s in shapes:
        n = int(np.prod(s))
        out.append(flat[o:o + n].reshape(s))
        o += n
    return out


def _adam2d(w, g, m, v, *, name):
    shp = w.shape
    r2 = (int(np.prod(shp[:-1])), shp[-1]) if len(shp) > 1 else (1, shp[0])
    d, mn, vn = _adamw(w.reshape(r2), g.reshape(r2), m.reshape(r2), v.reshape(r2), name=name)
    return d.reshape(shp), mn.reshape(shp), vn.reshape(shp)


def kernel(x, c, positions, ada_w, ada_b, norm1_w, w_in, attn_sinks, sgu_ln_w, sgu_ln_b, sgu_w, sgu_b, proj_a, proj_b, w_out, norm2_w, ffn_w_gate, ffn_w_up, ffn_conv_w, ffn_conv_b, ffn_w_down, final_norm_w, loss_target, m_ada_w, m_ada_b, m_norm1_w, m_w_in, m_attn_sinks, m_sgu_ln_w, m_sgu_ln_b, m_sgu_w, m_sgu_b, m_proj_a, m_proj_b, m_w_out, m_norm2_w, m_ffn_w_gate, m_ffn_w_up, m_ffn_conv_w, m_ffn_conv_b, m_ffn_w_down, m_final_norm_w, v_ada_w, v_ada_b, v_norm1_w, v_w_in, v_attn_sinks, v_sgu_ln_w, v_sgu_ln_b, v_sgu_w, v_sgu_b, v_proj_a, v_proj_b, v_w_out, v_norm2_w, v_ffn_w_gate, v_ffn_w_up, v_ffn_conv_w, v_ffn_conv_b, v_ffn_w_down, v_final_norm_w):
    wts = dict(ada_w=ada_w, ada_b=ada_b, norm1_w=norm1_w, w_in=w_in, attn_sinks=attn_sinks, sgu_ln_w=sgu_ln_w,
               sgu_ln_b=sgu_ln_b, sgu_w=sgu_w, sgu_b=sgu_b, proj_a=proj_a, proj_b=proj_b, w_out=w_out, norm2_w=norm2_w,
               ffn_w_gate=ffn_w_gate, ffn_w_up=ffn_w_up, ffn_conv_w=ffn_conv_w, ffn_conv_b=ffn_conv_b,
               ffn_w_down=ffn_w_down, final_norm_w=final_norm_w)
    mom = dict(ada_w=m_ada_w, ada_b=m_ada_b, norm1_w=m_norm1_w, w_in=m_w_in, attn_sinks=m_attn_sinks, sgu_ln_w=m_sgu_ln_w,
               sgu_ln_b=m_sgu_ln_b, sgu_w=m_sgu_w, sgu_b=m_sgu_b, proj_a=m_proj_a, proj_b=m_proj_b, w_out=m_w_out,
               norm2_w=m_norm2_w, ffn_w_gate=m_ffn_w_gate, ffn_w_up=m_ffn_w_up, ffn_conv_w=m_ffn_conv_w,
               ffn_conv_b=m_ffn_conv_b, ffn_w_down=m_ffn_w_down, final_norm_w=m_final_norm_w)
    var = dict(ada_w=v_ada_w, ada_b=v_ada_b, norm1_w=v_norm1_w, w_in=v_w_in, attn_sinks=v_attn_sinks, sgu_ln_w=v_sgu_ln_w,
               sgu_ln_b=v_sgu_ln_b, sgu_w=v_sgu_w, sgu_b=v_sgu_b, proj_a=v_proj_a, proj_b=v_proj_b, w_out=v_w_out,
               norm2_w=v_norm2_w, ffn_w_gate=v_ffn_w_gate, ffn_w_up=v_ffn_w_up, ffn_conv_w=v_ffn_conv_w,
               ffn_conv_b=v_ffn_conv_b, ffn_w_down=v_ffn_w_down, final_norm_w=v_final_norm_w)
    me = 4 * lax.axis_index("x") + 2 * lax.axis_index("y") + lax.axis_index("c")
    ada_cols = ada_w.shape[2]

    c_all = _all_gather(jnp.broadcast_to(c, (8, D_MODEL)), name="ag_c")[:, 0, :]
    prod = _ada_fwd(c_all, ada_w)
    prod_all = _all_gather(prod, name="ag_mod")
    mine = lax.dynamic_index_in_dim(prod_all, me, axis=1, keepdims=False)
    mod = jnp.stack([mine[:, l * ada_cols:(l + 1) * ada_cols].reshape(-1) for l in range(DEPTH)]) + ada_b

    conv_cols = ffn_conv_w.shape[2]
    conv_all = _all_gather(_flat_pack([ffn_conv_w], 8), name="ag_conv", after=mod)
    conv_full = jnp.stack([a.reshape(DEPTH, 3, conv_cols) for a in
                           [conv_all[j].reshape(-1)[:DEPTH * 3 * conv_cols] for j in range(N_DEV)]], axis=2)
    conv_full = conv_full.reshape(DEPTH, 3, FFN_DIM)
    small = {n: wts[n] for n in SMALL}
    small["ffn_conv_w"] = conv_full

    mx, my, mc = _coords()
    cidx = jnp.reshape(mc, (1,)).astype(jnp.int32)
    chipidx = jnp.reshape(2 * mx + my, (1,)).astype(jnp.int32)
    rope = _rope_setup(positions[0])

    class Gather:
        def __init__(self, src, tag):
            self.tag, self.src = tag, src
            self.land = lax.dynamic_update_slice(lax.empty((2, 4) + src.shape, src.dtype), src[None, None],
                                                 (mc, 2 * mx + my, 0, 0))

        def ici_start(self, after):
            self.sems, (self.src, self.land), tok = _rdma_start([self.src, self.land], 3, _plan_gather_ici,
                                                                name=f"ag_{self.tag}_ici_start", after=after)
            return tok

        def ici_wait_d2d_start(self, after):
            _, land = _rdma_wait(self.sems, [self.src, self.land], 3, _plan_gather_ici, after, name=f"ag_{self.tag}_ici_wait")
            self.sems, (self.land,), tok = _rdma_start([land], 1, _plan_gather_d2d, name=f"ag_{self.tag}_d2d_start")
            return tok

        def d2d_wait(self, after):
            (land,) = _rdma_wait(self.sems, [self.land], 1, _plan_gather_d2d, after, name=f"ag_{self.tag}_d2d_wait")
            return _from_land(land)

    def weights_job(names, l, tag):
        job = Gather(_pack_shards(wts, l, names), tag)
        job.weights = lambda after: _unpack_weights(job.d2d_wait(after), names)
        return job

    W0 = _unpack_weights(_all_gather(_pack_shards(wts, 0, PART_IN), name="ag_w0_in", after=conv_all), PART_IN)
    W1 = {}
    g_mix0, g_ffn0 = weights_job(PART_MIX, 0, "w0_mix"), weights_job(PART_FFN, 0, "w0_ffn")
    g_all1 = weights_job(BIG, 1, "w1")
    toks = {}

    def rest0_to_sibling(after):
        toks["mix"] = g_mix0.ici_wait_d2d_start(after)
        toks["ffn"] = g_ffn0.ici_wait_d2d_start(toks["mix"])
        return toks["ffn"]

    def rest0_then_layer1(after):
        W0.update(g_mix0.weights(after))
        W0.update(g_ffn0.weights(W0["proj_a"]))
        return g_all1.ici_start(W0["w_down"])

    x1, sv0 = _layer_fwd(0, x[0], mod[0], W0, small, rope,
                         {"mm_in": lambda after: g_ffn0.ici_start(g_mix0.ici_start(W0["wt_in"])),
                          "sgu": rest0_to_sibling, "mm_pa": rest0_then_layer1})
    g_all1.ici_wait_d2d_start(x1)
    x2, sv1 = _layer_fwd(1, x1, mod[1], W1, small, rope, {"mm_in": lambda after: W1.update(g_all1.weights(after))})
    gate2 = [mod[l][5 * D_MODEL:][None, :] for l in range(DEPTH)]
    dx2, dfw, loss_tile, do2, dg2 = _head(x2, final_norm_w[None, :], loss_target[0], (sv1["o2"], gate2[1]))
    loss = lax.psum(loss_tile[0, 0], ("x", "y", "c"))

    class Reduce:
        def __init__(self, names, tag):
            self.names, self.tag, self.rows = names, tag, _part_rows(names)

        def d2d_start(self, wg, after=None):
            self.sems, self.bufs, tok = _rdma_start([_pack_grads(wg, self.names), lax.empty((4, self.rows, 1024), BF)], 1,
                                                    _plan_reduce_d2d, name=f"rs_{self.tag}_d2d_start", after=after)
            return tok

        def d2d_wait_ici_start(self, after):
            g_t, land_a = _rdma_wait(self.sems, self.bufs, 1, _plan_reduce_d2d, after, name=f"rs_{self.tag}_d2d_wait")
            h = _sum_pair(g_t, land_a, cidx, name=f"rs_{self.tag}_sum_pair")
            self.sems, self.bufs, tok = _rdma_start([h, lax.empty((3, self.rows, 1024), BF)], 3, _plan_reduce_ici,
                                                    name=f"rs_{self.tag}_ici_start")
            return tok

        def ici_wait(self, after):
            h_t, land_b = _rdma_wait(self.sems, self.bufs, 3, _plan_reduce_ici, after, name=f"rs_{self.tag}_ici_wait")
            return _unpack_shard_grads(_sum_chips(h_t, land_b, chipidx, name=f"rs_{self.tag}_sum_chips"), self.names)

    (dx1, do2, dg2), wg1, sg1, dmod1 = _layer_bwd(1, dx2, do2, dg2, mod[1], W1, small, rope, sv1, below=(sv0["o2"], gate2[0]))
    r_all1, r_ffn0, r_mix0, r_in0 = Reduce(BIG, "g1"), Reduce(PART_FFN, "g0_ffn"), Reduce(PART_MIX, "g0_mix"), Reduce(PART_IN, "g0_in")
    tok1 = r_all1.d2d_start(wg1)
    wg0, shard1 = {}, {}

    def layer1_done_then_ffn0(after):
        shard1.update(r_all1.ici_wait(after))
        return r_ffn0.d2d_wait_ici_start(shard1["w_in"])

    def mix0_and_in0(after):
        tok = r_in0.d2d_start(wg0, r_mix0.d2d_start(wg0, after))
        return r_in0.d2d_wait_ici_start(r_mix0.d2d_wait_ici_start(tok))

    (grad_x,), _, sg0, dmod0 = _layer_bwd(
        0, dx1, do2, dg2, mod[0], W0, small, rope, sv0, wg=wg0,
        hooks={"mm_down_dx": lambda after: tok1, "mm_gu_dx": r_all1.d2d_wait_ici_start,
               "merge_bwd": lambda after: r_ffn0.d2d_start(wg0, after), "mm_pa_dx": layer1_done_then_ffn0,
               "mm_in_dx": mix0_and_in0})
    sg = {n: jnp.stack([sg0[n], sg1[n]]) for n in sg0}
    sg["final_norm_w"] = dfw[0]
    dmod = jnp.stack([dmod0, dmod1])
    vec_names = [n for n in SMALL if n not in ("ada_b", "sgu_w")] + ["ffn_conv_w"]
    vec_shapes = [(DEPTH, 6 * D_MODEL)] + [sg[n].shape for n in vec_names]
    vec_rows = -(-sum(int(np.prod(s)) for s in vec_shapes) // 1024 // 16) * 16
    sgu_rows = sgu_w.size // 1024
    g_small = Gather(jnp.concatenate([_flat_pack([dmod] + [sg[n] for n in vec_names], vec_rows),
                                      sg["sgu_w"].reshape(sgu_rows, 1024)], axis=0).astype(BF), "small")
    tok = g_small.ici_start(grad_x)

    shard0 = r_ffn0.ici_wait(tok)
    shard0.update(r_mix0.ici_wait(shard0["ffn_w_down"]))
    shard0.update(r_in0.ici_wait(shard0["w_out"]))
    grads, delta, new_m, new_v = {}, {}, {}, {}
    for n in BIG:
        two = lambda a: a.reshape(DEPTH, -1, a.shape[-1])
        out = _adamw_layers(two(wts[n]), [shard0[n], shard1[n]], two(mom[n]), two(var[n]), name=f"adamw_{n}")
        grads[n], delta[n], new_m[n], new_v[n] = [o.reshape(wts[n].shape) for o in out]

    sm_all = g_small.d2d_wait(g_small.ici_wait_d2d_start(delta["ffn_w_gate"]))
    sm_sum = _sum8(sm_all, name="sum_small")
    vec_sum = _flat_unpack(sm_sum[:vec_rows], vec_shapes)
    grads["ada_b"] = vec_sum[0]
    for n, gsum in zip(vec_names, vec_sum[1:]):
        grads[n] = gsum
    grads["sgu_w"] = sm_sum[vec_rows:].reshape(sgu_w.shape)
    grads["ffn_conv_w"] = lax.dynamic_slice_in_dim(grads["ffn_conv_w"], me * conv_cols, conv_cols, axis=2)
    dmod_all = sm_all[:, :DEPTH * 6, :].astype(F32).reshape(N_DEV, DEPTH, 6 * D_MODEL)
    dm_mine = lax.dynamic_slice_in_dim(dmod_all, me * ada_cols, ada_cols, axis=2).transpose(1, 0, 2)
    dm_mine = jnp.pad(dm_mine, ((0, 0), (0, 8), (0, 0)))
    grads["ada_w"] = _ada_bwd(jnp.pad(c_all, ((0, 8), (0, 0))), dm_mine)

    packed_small = [n for n in SMALL if n != "sgu_w"]
    pshapes = [wts[n].shape for n in packed_small]
    prow = -(-sum(int(np.prod(s)) for s in pshapes) // 1024 // 8) * 8
    pk = lambda d: _flat_pack([d[n] for n in packed_small], prow)
    d_s, m_s, v_s = _adamw(pk(wts), pk(grads), pk(mom), pk(var), name="adamw_small")
    for n, dd, mm, vv in zip(packed_small, _flat_unpack(d_s, pshapes), _flat_unpack(m_s, pshapes), _flat_unpack(v_s, pshapes)):
        delta[n], new_m[n], new_v[n] = dd, mm, vv
    for n in WEIGHT_ORDER:
        if n not in delta:
            delta[n], new_m[n], new_v[n] = _adam2d(wts[n], grads[n], mom[n], var[n], name=f"adamw_{n}")
    return (loss, grad_x[None], *[grads[n] for n in WEIGHT_ORDER], *[delta[n] for n in WEIGHT_ORDER],
            *[new_m[n] for n in WEIGHT_ORDER], *[new_v[n] for n in WEIGHT_ORDER])
```

```python
import functools

import jax
import jax.numpy as jnp
import numpy as np
from jax import lax
from jax.experimental import pallas as pl
from jax.experimental.pallas import tpu as pltpu

F32 = jnp.float32
BF = jnp.bfloat16

N_DEV = 8
D_MODEL = 1024
DEPTH = 2
N_Q_HEADS = 16
N_KV_HEADS = 2
HEAD_DIM = 64
Q_PER_KV = N_Q_HEADS // N_KV_HEADS
ATTN_BLOCK = 128
ROPE_THETA = 500000.0
ROT_DIM = HEAD_DIM // 4
SGU_WIDTH = 1024
SGU_GROUPS = 8
SGU_CHUNK = 128
FFN_DIM = 2816
NORM_EPS = 1e-6
Q_END = N_Q_HEADS * HEAD_DIM
K_END = Q_END + N_KV_HEADS * HEAD_DIM
V_END = K_END + N_KV_HEADS * HEAD_DIM
Z_END = V_END + 2 * SGU_WIDTH
IN_COLS = Z_END + 2 * D_MODEL
P_Z, P_G, P_Q, P_K, P_V = 0, 2048, 4096, 5120, 5248

ADAM_LR = 0.001
ADAM_B1 = 0.9
ADAM_B2 = 0.999
ADAM_EPS = 1e-08
ADAM_WD = 0.01
ADAM_STEP = 10

VMEM_LIMIT_BYTES = 56 * 1024 * 1024

BIG = ("w_in", "proj_a", "proj_b", "w_out", "ffn_w_gate", "ffn_w_up", "ffn_w_down")
COL_SHARDED = ("w_in", "ffn_w_gate", "ffn_w_up")
BIG_SHAPE = {"w_in": (D_MODEL, IN_COLS), "proj_a": (SGU_WIDTH, D_MODEL), "proj_b": (Q_END, D_MODEL),
             "w_out": (D_MODEL, D_MODEL), "ffn_w_gate": (D_MODEL, FFN_DIM), "ffn_w_up": (D_MODEL, FFN_DIM),
             "ffn_w_down": (FFN_DIM, D_MODEL)}
BIG_ROWS = {n: BIG_SHAPE[n][0] * BIG_SHAPE[n][1] // N_DEV // 1024 for n in BIG}
LAYER_ROWS = sum(BIG_ROWS.values())


def _pcall(body, **kw):
    return pl.pallas_call(body, **kw)


def _params(**kw):
    return pltpu.CompilerParams(vmem_limit_bytes=VMEM_LIMIT_BYTES, **kw)


def _tile(n, cap, unit=128):
    if n <= cap:
        return n
    best = 0
    t = unit
    while t <= cap:
        if n % t == 0:
            best = t
        t += unit
    assert best, (n, cap, unit)
    return best


def _mm(a, b, *, nt, out_dtype, name, res=None, gvec=None, after=None, tm=None, tn_cap=1024):
    a_list = list(a) if isinstance(a, (list, tuple)) else [a]
    b_list = list(b) if isinstance(b, (list, tuple)) else [b]
    a, b = a_list[0], b_list[0]
    M, K = a.shape
    N = b.shape[0] if nt else b.shape[1]
    k_total = sum(x.shape[1] for x in a_list)
    tm = _tile(M, tm or (1024 if k_total <= 1024 else 512), 8)
    tn = _tile(N, tn_cap)
    dn = (((1,), (1,)), ((), ())) if nt else (((1,), (0,)), ((), ()))

    def b_spec_of(x):
        k = x.shape[1] if nt else x.shape[0]
        return pl.BlockSpec((tn, k), lambda i, j: (j, 0)) if nt else pl.BlockSpec((k, tn), lambda i, j: (0, j))
    b_spec = b_spec_of(b)
    o_spec = pl.BlockSpec((tm, tn), lambda i, j: (i, j))
    if res is None:
        extra = [] if after is None else [after]
        n = len(a_list)

        def body(*refs):
            o_ref = refs[-1]
            acc = None
            for a_ref, b_ref in zip(refs[:n], refs[n:2 * n]):
                d = lax.dot_general(a_ref[...].astype(BF), b_ref[...].astype(BF), dn, preferred_element_type=F32)
                acc = d if acc is None else acc + d
            o_ref[...] = acc.astype(out_dtype)
        return _pcall(body, name=name, grid=(M // tm, N // tn),
                      in_specs=[pl.BlockSpec((tm, x.shape[1]), lambda i, j: (i, 0)) for x in a_list]
                      + [b_spec_of(x) for x in b_list] + [ANY] * len(extra), out_specs=o_spec,
                      out_shape=jax.ShapeDtypeStruct((M, N), out_dtype), compiler_params=_params())(
                          *a_list, *b_list, *extra)

    def body_res(a_ref, b_ref, r_ref, g_ref, o_ref, acc_ref):
        acc = lax.dot_general(a_ref[...].astype(BF), b_ref[...].astype(BF), dn, preferred_element_type=F32)
        acc_ref[...] = acc.astype(BF)
        o_ref[...] = r_ref[...] + g_ref[...] * acc
    return _pcall(body_res, name=name, grid=(M // tm, N // tn),
                  in_specs=[pl.BlockSpec((tm, K), lambda i, j: (i, 0)), b_spec, o_spec,
                            pl.BlockSpec((1, tn), lambda i, j: (0, j))],
                  out_specs=[o_spec, o_spec],
                  out_shape=[jax.ShapeDtypeStruct((M, N), F32), jax.ShapeDtypeStruct((M, N), BF)],
                  compiler_params=_params())(a, b, res, gvec)


def _mm_tn(a, b, *, name, out_dtype=BF, tk=1024, tm_cap=1408, tn_cap=1024):
    S, M = a.shape
    N = b.shape[1]
    tk = _tile(S, tk, 8)
    tm = _tile(M, tm_cap)
    tn = _tile(N, tn_cap)
    nk = S // tk

    def body(a_ref, b_ref, o_ref, acc_ref):
        k = pl.program_id(2)

        @pl.when(k == 0)
        def _():
            acc_ref[...] = jnp.zeros_like(acc_ref)
        acc_ref[...] += lax.dot_general(a_ref[...].astype(BF), b_ref[...].astype(BF), (((0,), (0,)), ((), ())),
                                        preferred_element_type=F32)

        @pl.when(k == nk - 1)
        def _():
            o_ref[...] = acc_ref[...].astype(out_dtype)
    return _pcall(body, name=name, grid=(M // tm, N // tn, nk),
                  in_specs=[pl.BlockSpec((tk, tm), lambda i, j, k: (k, i)),
                            pl.BlockSpec((tk, tn), lambda i, j, k: (k, j))],
                  out_specs=pl.BlockSpec((tm, tn), lambda i, j, k: (i, j)),
                  out_shape=jax.ShapeDtypeStruct((M, N), out_dtype), scratch_shapes=[pltpu.VMEM((tm, tn), F32)],
                  compiler_params=_params())(a, b)


def _rms(x, w):
    return x * lax.rsqrt(jnp.mean(x * x, axis=-1, keepdims=True) + NORM_EPS) * w


def _normmod_fn(x, nw, sc, sh):
    return _rms(x, nw) * (1.0 + sc) + sh


def _gelu(x):
    return 0.5 * x * (1.0 + lax.erf(x * (2.0 ** -0.5)))


def _ln_gelu_fn(zv, w, b):
    v = _gelu(zv)
    mu = jnp.mean(v, axis=-1, keepdims=True)
    var = jnp.mean(jnp.square(v - mu), axis=-1, keepdims=True)
    return (v - mu) * lax.rsqrt(var + NORM_EPS) * w + b


def _sigmoid(x):
    return 1.0 / (1.0 + jnp.exp(-x))


def _row_spec(tm, n):
    return pl.BlockSpec((tm, n), lambda i: (i, 0))


def _vec_spec(n):
    return pl.BlockSpec((1, n), lambda i: (0, 0))


def _acc(ref, val):
    @pl.when(pl.program_id(0) == 0)
    def _():
        ref[...] = jnp.zeros_like(ref)
    ref[...] += val


def _norm_mm(x, nw, sc, sh, ws, *, name, after=None, tm=1024, tn_cap=768):
    S, K = x.shape
    N = ws[0].shape[0]
    tm = _tile(S, tm, 8)
    tn = _tile(N, tn_cap)
    nw_, ne = len(ws), 0 if after is None else 1

    def body(x_ref, nw_ref, sc_ref, sh_ref, *rest):
        w_refs = rest[:nw_]
        h_ref = rest[nw_ + ne]
        o_refs = rest[nw_ + ne + 1:nw_ + ne + 1 + nw_]
        h_s = rest[-1]

        @pl.when(pl.program_id(1) == 0)
        def _():
            hv = _normmod_fn(x_ref[...], nw_ref[...], sc_ref[...], sh_ref[...]).astype(BF)
            h_s[...] = hv
            h_ref[...] = hv
        for w_ref, o_ref in zip(w_refs, o_refs):
            o_ref[...] = lax.dot_general(h_s[...], w_ref[...], (((1,), (1,)), ((), ())),
                                         preferred_element_type=F32).astype(BF)
    row = pl.BlockSpec((tm, K), lambda i, j: (i, 0))
    vec = pl.BlockSpec((1, K), lambda i, j: (0, 0))
    out = pl.BlockSpec((tm, tn), lambda i, j: (i, j))
    res = _pcall(body, name=name, grid=(S // tm, N // tn),
                 in_specs=[row, vec, vec, vec] + [pl.BlockSpec((tn, K), lambda i, j: (j, 0))] * nw_ + [ANY] * ne,
                 out_specs=[row] + [out] * nw_,
                 out_shape=[jax.ShapeDtypeStruct((S, K), BF)] + [jax.ShapeDtypeStruct((S, N), BF)] * nw_,
                 scratch_shapes=[pltpu.VMEM((tm, K), BF)], compiler_params=_params())(
                     x, nw, sc, sh, *ws, *([] if after is None else [after]))
    return res[0], list(res[1:])


def _gate_bwd(dxv, o_ref, g_ref, do_ref, dg_ref):
    do_ref[...] = (dxv * g_ref[...]).astype(BF)
    _acc(dg_ref, jnp.sum(dxv * o_ref[...].astype(F32), axis=0, keepdims=True))


def _normmod_bwd(dh, x, nw, sc, sh, dres, gate, *, name, tm=256):
    S, Dm = x.shape
    tm = _tile(S, tm, 8)
    ng = 0 if gate is None else 2

    def body(dh_ref, x_ref, nw_ref, sc_ref, sh_ref, dres_ref, *rest):
        dx_ref, dnw_ref, dsc_ref, dsh_ref = rest[ng:ng + 4]
        _, vjp = jax.vjp(_normmod_fn, x_ref[...], nw_ref[...], sc_ref[...], sh_ref[...])
        dx, dnw, dsc, dsh = vjp(dh_ref[...])
        dxv = dres_ref[...] + dx
        dx_ref[...] = dxv
        _acc(dnw_ref, dnw)
        _acc(dsc_ref, dsc)
        _acc(dsh_ref, dsh)
        if gate is not None:
            _gate_bwd(dxv, rest[0], rest[1], rest[ng + 4], rest[ng + 5])
    vec = jax.ShapeDtypeStruct((1, Dm), F32)
    gate_in = [] if gate is None else [_row_spec(tm, Dm), _vec_spec(Dm)]
    gate_out = [] if gate is None else [_row_spec(tm, Dm), _vec_spec(Dm)]
    gate_shape = [] if gate is None else [jax.ShapeDtypeStruct((S, Dm), BF), vec]
    return _pcall(body, name=name, grid=(S // tm,),
                  in_specs=[_row_spec(tm, Dm), _row_spec(tm, Dm), _vec_spec(Dm), _vec_spec(Dm), _vec_spec(Dm),
                            _row_spec(tm, Dm)] + gate_in,
                  out_specs=[_row_spec(tm, Dm), _vec_spec(Dm), _vec_spec(Dm), _vec_spec(Dm)] + gate_out,
                  out_shape=[jax.ShapeDtypeStruct((S, Dm), F32), vec, vec, vec] + gate_shape,
                  compiler_params=_params())(dh, x, nw, sc, sh, dres, *([] if gate is None else gate))


def _head(x, fw, target, gate, *, tm=256):
    S, Dm = x.shape
    tm = _tile(S, tm, 8)

    def body(x_ref, fw_ref, t_ref, o_ref, g_ref, dx_ref, dfw_ref, loss_ref, do_ref, dg_ref):
        y, vjp = jax.vjp(_rms, x_ref[...], fw_ref[...])
        err = y - t_ref[...]
        dx, dfw = vjp(err * (1.0 / Dm))
        dx_ref[...] = dx
        _acc(dfw_ref, dfw)
        part = 0.5 * jnp.sum(jnp.mean(err * err, axis=-1, keepdims=True), axis=0, keepdims=True)
        _acc(loss_ref, jnp.broadcast_to(part, (8, 128)))
        _gate_bwd(dx, o_ref, g_ref, do_ref, dg_ref)
    vec = jax.ShapeDtypeStruct((1, Dm), F32)
    return _pcall(body, name="head", grid=(S // tm,),
                  in_specs=[_row_spec(tm, Dm), _vec_spec(Dm), _row_spec(tm, Dm), _row_spec(tm, Dm), _vec_spec(Dm)],
                  out_specs=[_row_spec(tm, Dm), _vec_spec(Dm), pl.BlockSpec((8, 128), lambda i: (0, 0)),
                             _row_spec(tm, Dm), _vec_spec(Dm)],
                  out_shape=[jax.ShapeDtypeStruct((S, Dm), F32), vec, jax.ShapeDtypeStruct((8, 128), F32),
                             jax.ShapeDtypeStruct((S, Dm), BF), vec],
                  compiler_params=_params())(x, fw, target, *gate)


def _tril_mask():
    r = lax.broadcasted_iota(jnp.int32, (SGU_CHUNK, SGU_CHUNK), 0)
    c = lax.broadcasted_iota(jnp.int32, (SGU_CHUNK, SGU_CHUNK), 1)
    return c <= r


def _sgu_fwd(proj, lnw, lnb, w, b_t, *, name, after=None, tm=256):
    S = proj.shape[0]
    tm = _tile(S, tm, SGU_CHUNK)
    extra = [] if after is None else [after]

    def body(zu_ref, zv_ref, lnw_ref, lnb_ref, w_ref, bt_ref, *rest):
        o_ref = rest[-1]
        u = _gelu(zu_ref[...].astype(F32))
        vn = _ln_gelu_fn(zv_ref[...].astype(F32), lnw_ref[...], lnb_ref[...]).astype(BF)
        mask = _tril_mask()
        for g in range(SGU_GROUPS):
            wm = jnp.where(mask, w_ref[g], 0.0).astype(BF)
            cols = slice(g * 128, (g + 1) * 128)
            for ci in range(tm // SGU_CHUNK):
                rows = slice(ci * SGU_CHUNK, (ci + 1) * SGU_CHUNK)
                f = jnp.dot(wm, vn[rows, cols], preferred_element_type=F32) + bt_ref[:, g:g + 1]
                o_ref[rows, cols] = (u[rows, cols] * f).astype(BF)
    return _pcall(body, name=name, grid=(S // tm,),
                  in_specs=[pl.BlockSpec((tm, SGU_WIDTH), lambda i: (i, 0)), pl.BlockSpec((tm, SGU_WIDTH), lambda i: (i, 1)),
                            _vec_spec(SGU_WIDTH), _vec_spec(SGU_WIDTH),
                            pl.BlockSpec((SGU_GROUPS, 128, 128), lambda i: (0, 0, 0)),
                            pl.BlockSpec((128, SGU_GROUPS), lambda i: (0, 0))] + [ANY] * len(extra),
                  out_specs=_row_spec(tm, SGU_WIDTH), out_shape=jax.ShapeDtypeStruct((S, SGU_WIDTH), BF),
                  compiler_params=_params())(proj, proj, lnw, lnb, w, b_t, *extra)


def _sgu_bwd(dy, proj, lnw, lnb, w, b_t, dproj, *, name, tm=256):
    S = proj.shape[0]
    tm = _tile(S, tm, SGU_CHUNK)

    def body(dy_ref, zu_ref, zv_ref, lnw_ref, lnb_ref, w_ref, bt_ref, _, dz_ref, dlnw_ref, dlnb_ref, dw_ref, dbt_ref,
             f_s, dvn_s):
        first = pl.program_id(0) == 0

        @pl.when(first)
        def _():
            dw_ref[...] = jnp.zeros_like(dw_ref)
            dbt_ref[...] = jnp.zeros_like(dbt_ref)
        u, vjp_u = jax.vjp(_gelu, zu_ref[...].astype(F32))
        vn, vjp_v = jax.vjp(_ln_gelu_fn, zv_ref[...].astype(F32), lnw_ref[...], lnb_ref[...])
        vn = vn.astype(BF)
        dy_v = dy_ref[...]
        df = (dy_v * u).astype(BF)
        mask = _tril_mask()
        for g in range(SGU_GROUPS):
            wm = jnp.where(mask, w_ref[g], 0.0).astype(BF)
            cols = slice(g * 128, (g + 1) * 128)
            dwg = jnp.zeros((128, 128), F32)
            dbg = jnp.zeros((128, 1), F32)
            for ci in range(tm // SGU_CHUNK):
                rows = slice(ci * SGU_CHUNK, (ci + 1) * SGU_CHUNK)
                vn_c = vn[rows, cols]
                df_c = df[rows, cols]
                f_s[rows, cols] = jnp.dot(wm, vn_c, preferred_element_type=F32) + bt_ref[:, g:g + 1]
                dvn_s[rows, cols] = lax.dot_general(wm, df_c, (((0,), (0,)), ((), ())), preferred_element_type=F32)
                dwg = dwg + lax.dot_general(df_c, vn_c, (((1,), (1,)), ((), ())), preferred_element_type=F32)
                dbg = dbg + jnp.sum((dy_v[rows, cols] * u[rows, cols]), axis=1, keepdims=True)
            dw_ref[g] += jnp.where(mask, dwg, 0.0)
            dbt_ref[:, g:g + 1] += dbg
        (dzu,) = vjp_u(dy_v * f_s[...])
        dzv, dlnw, dlnb = vjp_v(dvn_s[...])
        dz_ref[:, :SGU_WIDTH] = dzu.astype(BF)
        dz_ref[:, SGU_WIDTH:] = dzv.astype(BF)
        _acc(dlnw_ref, dlnw)
        _acc(dlnb_ref, dlnb)
    vec = jax.ShapeDtypeStruct((1, SGU_WIDTH), F32)
    return _pcall(body, name=name, grid=(S // tm,),
                  in_specs=[_row_spec(tm, SGU_WIDTH),
                            pl.BlockSpec((tm, SGU_WIDTH), lambda i: (i, 0)), pl.BlockSpec((tm, SGU_WIDTH), lambda i: (i, 1)),
                            _vec_spec(SGU_WIDTH), _vec_spec(SGU_WIDTH),
                            pl.BlockSpec((SGU_GROUPS, 128, 128), lambda i: (0, 0, 0)),
                            pl.BlockSpec((128, SGU_GROUPS), lambda i: (0, 0)), ANY],
                  out_specs=[pl.BlockSpec((tm, 2 * SGU_WIDTH), lambda i: (i, P_Z // (2 * SGU_WIDTH))),
                             _vec_spec(SGU_WIDTH), _vec_spec(SGU_WIDTH),
                             pl.BlockSpec((SGU_GROUPS, 128, 128), lambda i: (0, 0, 0)),
                             pl.BlockSpec((128, SGU_GROUPS), lambda i: (0, 0))],
                  out_shape=[jax.ShapeDtypeStruct(dproj.shape, BF), vec, vec,
                             jax.ShapeDtypeStruct((SGU_GROUPS, 128, 128), F32),
                             jax.ShapeDtypeStruct((128, SGU_GROUPS), F32)],
                  scratch_shapes=[pltpu.VMEM((tm, SGU_WIDTH), F32), pltpu.VMEM((tm, SGU_WIDTH), F32)],
                  input_output_aliases={7: 0},
                  compiler_params=_params())(dy, proj, proj, lnw, lnb, w, b_t, dproj)


def _merge_fwd(y_sgu, y_attn, pa, pb, proj, *, name, after=None, tm=1024, tn=512):
    S, Dm = y_sgu.shape
    tm = _tile(S, tm, 8)
    nj = Dm // tn
    extra = [] if after is None else [after]

    def body(ys_ref, ya_ref, pa_ref, pb_ref, ga_ref, gb_ref, *rest):
        a_ref, b_ref, m_ref = rest[-3:]
        a = jnp.dot(ys_ref[...], pa_ref[...], preferred_element_type=F32)
        b = jnp.dot(ya_ref[...], pb_ref[...], preferred_element_type=F32)
        a_ref[...] = a.astype(BF)
        b_ref[...] = b.astype(BF)
        m_ref[...] = (_sigmoid(ga_ref[...].astype(F32)) * a + _sigmoid(gb_ref[...].astype(F32)) * b).astype(BF)
    row = pl.BlockSpec((tm, Dm), lambda i, j: (i, 0))
    col = pl.BlockSpec((Dm, tn), lambda i, j: (0, j))
    out = pl.BlockSpec((tm, tn), lambda i, j: (i, j))
    sh = jax.ShapeDtypeStruct((S, Dm), BF)
    return _pcall(body, name=name, grid=(S // tm, nj),
                  in_specs=[row, row, col, col, pl.BlockSpec((tm, tn), lambda i, j: (i, P_G // tn + j)),
                            pl.BlockSpec((tm, tn), lambda i, j: (i, (P_G + Dm) // tn + j))] + [ANY] * len(extra),
                  out_specs=[out, out, out], out_shape=[sh, sh, sh],
                  compiler_params=_params())(y_sgu, y_attn, pa, pb, proj, proj, *extra)


def _merge_bwd(do, w_out, a, b, proj, *, name, after=None, tm=512):
    S, Dm = a.shape
    tm = _tile(S, tm, 8)
    ga_blk, gb_blk = P_G // Dm, P_G // Dm + 1
    extra = [] if after is None else [after]

    def body(do_ref, w_ref, a_ref, b_ref, ga_ref, gb_ref, *rest):
        da_ref, db_ref, dg_ref = rest[-3:]
        dmv = lax.dot_general(do_ref[...], w_ref[...], (((1,), (1,)), ((), ())), preferred_element_type=F32)
        sa = _sigmoid(ga_ref[...].astype(F32))
        sb = _sigmoid(gb_ref[...].astype(F32))
        da_ref[...] = (dmv * sa).astype(BF)
        db_ref[...] = (dmv * sb).astype(BF)
        dg_ref[:, :Dm] = (dmv * a_ref[...].astype(F32) * sa * (1.0 - sa)).astype(BF)
        dg_ref[:, Dm:] = (dmv * b_ref[...].astype(F32) * sb * (1.0 - sb)).astype(BF)
    return _pcall(body, name=name, grid=(S // tm,),
                  in_specs=[_row_spec(tm, Dm), pl.BlockSpec((Dm, Dm), lambda i: (0, 0)), _row_spec(tm, Dm), _row_spec(tm, Dm),
                            pl.BlockSpec((tm, Dm), lambda i: (i, ga_blk)), pl.BlockSpec((tm, Dm), lambda i: (i, gb_blk))]
                  + [ANY] * len(extra),
                  out_specs=[_row_spec(tm, Dm), _row_spec(tm, Dm), pl.BlockSpec((tm, 2 * Dm), lambda i: (i, P_G // (2 * Dm)))],
                  out_shape=[jax.ShapeDtypeStruct((S, Dm), BF), jax.ShapeDtypeStruct((S, Dm), BF),
                             jax.ShapeDtypeStruct((S, IN_COLS), BF)],
                  compiler_params=_params())(do, w_out, a, b, proj, proj, *extra)


def _shift_rows(a, halo, k, up):
    n = a.shape[0]
    r8 = lax.broadcasted_iota(jnp.int32, (8, a.shape[1]), 0)
    if not up:
        rolled = pltpu.roll(a, k, 0)
        patch = jnp.where(r8 < k, pltpu.roll(halo, k, 0), rolled[:8])
        return jnp.concatenate([patch, rolled[8:]], axis=0)
    rolled = pltpu.roll(a, n - k, 0)
    patch = jnp.where(r8 >= 8 - k, pltpu.roll(halo, 8 - k, 0), rolled[n - 8:])
    return jnp.concatenate([rolled[:n - 8], patch], axis=0)


def _conv_taps(a, halo):
    return _shift_rows(a, halo, 2, False), _shift_rows(a, halo, 1, False), a


HALO = 16


def _prev_halo_spec(tm, Fd):
    return pl.BlockSpec((HALO, Fd), lambda i: (jnp.maximum(i * (tm // HALO) - 1, 0), 0))


def _conv_fwd(a_ref, halo_ref, cw_ref, cb_ref):
    halo = jnp.where(pl.program_id(0) > 0, halo_ref[...].astype(F32)[HALO - 8:], 0.0)
    t0, t1, t2 = _conv_taps(a_ref[...].astype(F32), halo)
    return t0, t1, t2, cb_ref[...] + cw_ref[0:1, :] * t0 + cw_ref[1:2, :] * t1 + cw_ref[2:3, :] * t2


def _ffn_act_fwd(a, up, cw, cb, *, name, tm=256):
    S, Fd = a.shape
    tm = _tile(S, tm, HALO)

    def body(a_ref, up_ref, halo_ref, cw_ref, cb_ref, o_ref):
        _, _, _, ac = _conv_fwd(a_ref, halo_ref, cw_ref, cb_ref)
        o_ref[...] = (ac * _sigmoid(ac) * up_ref[...].astype(F32)).astype(BF)
    return _pcall(body, name=name, grid=(S // tm,),
                  in_specs=[_row_spec(tm, Fd), _row_spec(tm, Fd), _prev_halo_spec(tm, Fd),
                            pl.BlockSpec((3, Fd), lambda i: (0, 0)), _vec_spec(Fd)],
                  out_specs=_row_spec(tm, Fd), out_shape=jax.ShapeDtypeStruct((S, Fd), BF),
                  compiler_params=_params())(a, up, a, cw, cb)


def _ffn_act_bwd_a(dhf, a, up, cw, cb, *, name, tm=256):
    S, Fd = a.shape
    tm = _tile(S, tm, HALO)

    def body(dhf_ref, a_ref, up_ref, halo_ref, cw_ref, cb_ref, dac_ref, dup_ref, dcw_ref, dcb_ref):
        t0, t1, t2, ac = _conv_fwd(a_ref, halo_ref, cw_ref, cb_ref)
        s = _sigmoid(ac)
        dhf_v = dhf_ref[...].astype(F32)
        dup_ref[...] = (dhf_v * ac * s).astype(BF)
        dac = dhf_v * up_ref[...].astype(F32) * (s * (1.0 + ac * (1.0 - s)))
        dac_ref[...] = dac.astype(BF)
        _acc(dcb_ref, jnp.sum(dac, axis=0, keepdims=True))
        _acc(dcw_ref, jnp.concatenate([jnp.sum(dac * t0, axis=0, keepdims=True),
                                       jnp.sum(dac * t1, axis=0, keepdims=True),
                                       jnp.sum(dac * t2, axis=0, keepdims=True)], axis=0))
    return _pcall(body, name=name, grid=(S // tm,),
                  in_specs=[_row_spec(tm, Fd), _row_spec(tm, Fd), _row_spec(tm, Fd), _prev_halo_spec(tm, Fd),
                            pl.BlockSpec((3, Fd), lambda i: (0, 0)), _vec_spec(Fd)],
                  out_specs=[_row_spec(tm, Fd), _row_spec(tm, Fd), pl.BlockSpec((3, Fd), lambda i: (0, 0)), _vec_spec(Fd)],
                  out_shape=[jax.ShapeDtypeStruct((S, Fd), BF), jax.ShapeDtypeStruct((S, Fd), BF),
                             jax.ShapeDtypeStruct((3, Fd), F32), jax.ShapeDtypeStruct((1, Fd), F32)],
                  compiler_params=_params())(dhf, a, up, a, cw, cb)


def _ffn_act_bwd_b(dac, cw, *, name, tm=256):
    S, Fd = dac.shape
    tm = _tile(S, tm, HALO)
    last = S // tm - 1

    def body(d_ref, halo_ref, cw_ref, o_ref):
        halo = jnp.where(pl.program_id(0) < last, halo_ref[...].astype(F32)[:8], 0.0)
        d = d_ref[...].astype(F32)
        o_ref[...] = (cw_ref[2:3, :] * d + cw_ref[1:2, :] * _shift_rows(d, halo, 1, True)
                      + cw_ref[0:1, :] * _shift_rows(d, halo, 2, True)).astype(BF)
    return _pcall(body, name=name, grid=(S // tm,),
                  in_specs=[_row_spec(tm, Fd),
                            pl.BlockSpec((HALO, Fd), lambda i: (jnp.minimum((i + 1) * (tm // HALO), S // HALO - 1), 0)),
                            pl.BlockSpec((3, Fd), lambda i: (0, 0))],
                  out_specs=_row_spec(tm, Fd), out_shape=jax.ShapeDtypeStruct((S, Fd), BF),
                  compiler_params=_params())(dac, dac, cw)


def _rope_tables(pos_col, inv_row, m1_row, m2_row):
    S = pos_col.shape[0]
    tm = _tile(S, 512, 8)

    def body(p_ref, inv_ref, m1_ref, m2_ref, c_ref, s1_ref, s2_ref):
        ang = p_ref[...] * inv_ref[...]
        sn = jnp.sin(ang)
        c_ref[...] = jnp.cos(ang)
        s1_ref[...] = -sn * m1_ref[...]
        s2_ref[...] = sn * m2_ref[...]
    sh = jax.ShapeDtypeStruct((S, 128), F32)
    return _pcall(body, name="rope_tables", grid=(S // tm,),
                  in_specs=[pl.BlockSpec((tm, 1), lambda i: (i, 0)), _vec_spec(128), _vec_spec(128), _vec_spec(128)],
                  out_specs=[_row_spec(tm, 128)] * 3, out_shape=[sh, sh, sh], compiler_params=_params())(
                      pos_col, inv_row, m1_row, m2_row)


def _rope_apply(x, c, s1, s2):
    outs = []
    for j in range(x.shape[1] // 128):
        xj = x[:, j * 128:(j + 1) * 128]
        outs.append(xj * c + pltpu.roll(xj, 120, 1) * s1 + pltpu.roll(xj, 8, 1) * s2)
    return outs[0] if len(outs) == 1 else jnp.concatenate(outs, axis=1)


def _rope_apply_t(d, c, s1, s2):
    outs = []
    for j in range(d.shape[1] // 128):
        dj = d[:, j * 128:(j + 1) * 128]
        outs.append(dj * c + pltpu.roll(dj * s1, 8, 1) + pltpu.roll(dj * s2, 120, 1))
    return outs[0] if len(outs) == 1 else jnp.concatenate(outs, axis=1)


def _rope_fwd(proj, c, s1, s2, *, name, tm=512):
    S = proj.shape[0]
    tm = _tile(S, tm, 8)

    def body(q_ref, k_ref, v_ref, c_ref, s1_ref, s2_ref, qo_ref, ko_ref, vo_ref):
        cv, s1v, s2v = c_ref[...], s1_ref[...], s2_ref[...]
        qo_ref[...] = (_rope_apply(q_ref[...].astype(F32), cv, s1v, s2v) * (HEAD_DIM ** -0.5)).astype(BF)
        ko_ref[...] = _rope_apply(k_ref[...].astype(F32), cv, s1v, s2v).astype(BF)
        vo_ref[...] = v_ref[...].astype(BF)
    return _pcall(body, name=name, grid=(S // tm,),
                  in_specs=[pl.BlockSpec((tm, Q_END), lambda i: (i, P_Q // Q_END)),
                            pl.BlockSpec((tm, 128), lambda i: (i, P_K // 128)),
                            pl.BlockSpec((tm, 128), lambda i: (i, P_V // 128)),
                            _row_spec(tm, 128), _row_spec(tm, 128), _row_spec(tm, 128)],
                  out_specs=[_row_spec(tm, Q_END), _row_spec(tm, 128), _row_spec(tm, 128)],
                  out_shape=[jax.ShapeDtypeStruct((S, Q_END), BF), jax.ShapeDtypeStruct((S, 128), BF),
                             jax.ShapeDtypeStruct((S, 128), BF)],
                  compiler_params=_params())(proj, proj, proj, c, s1, s2)


def _rope_bwd(dq, dk, dv, c, s1, s2, dproj, *, name, tm=512):
    S = dq.shape[0]
    tm = _tile(S, tm, 8)
    tabs = [_row_spec(tm, 128)] * 3
    shape = jax.ShapeDtypeStruct(dproj.shape, BF)

    def body_q(dq_ref, c_ref, s1_ref, s2_ref, _, o_ref):
        o_ref[...] = _rope_apply_t(dq_ref[...].astype(F32), c_ref[...], s1_ref[...], s2_ref[...]).astype(BF)
    dproj = _pcall(body_q, name=name + "_q", grid=(S // tm,), in_specs=[_row_spec(tm, Q_END)] + tabs + [ANY],
                   out_specs=pl.BlockSpec((tm, Q_END), lambda i: (i, P_Q // Q_END)), out_shape=shape,
                   input_output_aliases={4: 0}, compiler_params=_params())(dq, c, s1, s2, dproj)

    def body_kv(dk_ref, dv_ref, c_ref, s1_ref, s2_ref, _, o_ref):
        o_ref[:, :128] = _rope_apply_t(dk_ref[...], c_ref[...], s1_ref[...], s2_ref[...]).astype(BF)
        o_ref[:, 128:] = dv_ref[...].astype(BF)
    return _pcall(body_kv, name=name + "_kv", grid=(S // tm,),
                  in_specs=[_row_spec(tm, 128), _row_spec(tm, 128)] + tabs + [ANY],
                  out_specs=pl.BlockSpec((tm, 256), lambda i: (i, P_K // 256)), out_shape=shape,
                  input_output_aliases={5: 0}, compiler_params=_params())(dk, dv, c, s1, s2, dproj)


def _lane_lo(shape):
    return lax.broadcasted_iota(jnp.int32, shape, 1) < HEAD_DIM


def _stack_heads(x, g):
    lo = _lane_lo((ATTN_BLOCK, 128))
    zero = jnp.zeros((ATTN_BLOCK, 128), x.dtype)
    parts = []
    for p in range(Q_PER_KV // 2):
        xp = x[:, (g * 4 + p) * 128:(g * 4 + p + 1) * 128]
        parts += [jnp.where(lo, xp, zero), jnp.where(lo, zero, xp)]
    return jnp.concatenate(parts, axis=0)


def _unstack_heads(o2):
    lo = _lane_lo((ATTN_BLOCK, 128))
    return [jnp.where(lo, o2[2 * p * ATTN_BLOCK:(2 * p + 1) * ATTN_BLOCK], o2[(2 * p + 1) * ATTN_BLOCK:(2 * p + 2) * ATTN_BLOCK])
            for p in range(Q_PER_KV // 2)]


def _dup_half(prev, cur, g):
    x = jnp.concatenate([prev, cur], axis=0).astype(F32)
    lo = _lane_lo(x.shape)
    r = pltpu.roll(x, HEAD_DIM, 1)
    return (jnp.where(lo, x, r) if g == 0 else jnp.where(lo, r, x)).astype(BF)


def _fold_halves(x):
    return x + pltpu.roll(x, HEAD_DIM, 1)


def _attn_bias():
    i = lax.broadcasted_iota(jnp.int32, (Q_PER_KV * ATTN_BLOCK, 2 * ATTN_BLOCK), 0) & (ATTN_BLOCK - 1)
    j = lax.broadcasted_iota(jnp.int32, (Q_PER_KV * ATTN_BLOCK, 2 * ATTN_BLOCK), 1)
    band = (j > i) & (j <= i + ATTN_BLOCK)
    return jnp.stack([jnp.where(band & (j >= ATTN_BLOCK), 0.0, -jnp.inf), jnp.where(band, 0.0, -jnp.inf)]).astype(F32)


def _both(x):
    return jnp.concatenate([x, x], axis=1)


def _row_sums(x_bf):
    return jnp.dot(x_bf, jnp.ones((x_bf.shape[1], 128), BF), preferred_element_type=F32)


def _attn_probs(qs, kb, sink, bias):
    s = lax.dot_general(qs, kb, (((1,), (1,)), ((), ())), preferred_element_type=F32) + bias
    m = jnp.maximum(jnp.broadcast_to(jnp.max(s, axis=-1, keepdims=True), sink.shape), sink)
    return jnp.exp(s - _both(m)), jnp.exp(sink - m)


def _attn_specs(S):
    nb = S // ATTN_BLOCK
    qs = pl.BlockSpec((ATTN_BLOCK, Q_END), lambda n: (n, 0))
    cur = pl.BlockSpec((ATTN_BLOCK, 128), lambda n: (n, 0))
    prev = pl.BlockSpec((ATTN_BLOCK, 128), lambda n: (jnp.maximum(n - 1, 0), 0))
    sink = pl.BlockSpec((N_KV_HEADS, Q_PER_KV * ATTN_BLOCK, 128), lambda n: (0, 0, 0))
    bias = pl.BlockSpec((None, Q_PER_KV * ATTN_BLOCK, 2 * ATTN_BLOCK), lambda n: (jnp.minimum(n, 1), 0, 0))
    return nb, qs, cur, prev, sink, bias


def _attn_fwd(q, k, v, sink_rows, bias, *, name):
    S = q.shape[0]
    nb, qs, cur, prev, sink, bs = _attn_specs(S)

    def body(q_ref, kp_ref, kc_ref, vp_ref, vc_ref, sk_ref, b_ref, o_ref):
        for g in range(N_KV_HEADS):
            kb = _dup_half(kp_ref[...], kc_ref[...], g)
            vb = _dup_half(vp_ref[...], vc_ref[...], g)
            p, es = _attn_probs(_stack_heads(q_ref[...], g), kb, sk_ref[g], b_ref[...])
            ones = jnp.ones((2 * ATTN_BLOCK, 128), BF)
            o3 = jnp.dot(p.astype(BF), jnp.concatenate([vb, ones], axis=1), preferred_element_type=F32)
            o2 = o3[:, :128] / (o3[:, 128:] + es)
            for t, tile in enumerate(_unstack_heads(o2)):
                o_ref[:, (g * 4 + t) * 128:(g * 4 + t + 1) * 128] = tile.astype(BF)
    return _pcall(body, name=name, grid=(nb,), in_specs=[qs, prev, cur, prev, cur, sink, bs], out_specs=qs,
                  out_shape=jax.ShapeDtypeStruct(q.shape, BF), compiler_params=_params())(q, k, k, v, v, sink_rows, bias)


def _attn_bwd(do, q, k, v, sink_rows, bias, *, name):
    S = q.shape[0]
    nb, qs, cur, prev, sink, bs = _attn_specs(S)
    full = pl.BlockSpec((S, 128), lambda n: (0, 0))
    dsk_spec = pl.BlockSpec((N_KV_HEADS, Q_PER_KV, 128), lambda n: (0, 0, 0))

    def body(do_ref, q_ref, kp_ref, kc_ref, vp_ref, vc_ref, sk_ref, b_ref, dq_ref, dk_ref, dv_ref, dsk_ref):
        n = pl.program_id(0)

        @pl.when(n == 0)
        def _():
            dk_ref[...] = jnp.zeros_like(dk_ref)
            dv_ref[...] = jnp.zeros_like(dv_ref)
            dsk_ref[...] = jnp.zeros_like(dsk_ref)
        sub = lax.broadcasted_iota(jnp.int32, (Q_PER_KV, 128), 0)
        dkf, dvf = [], []
        for g in range(N_KV_HEADS):
            qst = _stack_heads(q_ref[...], g)
            dos = _stack_heads(do_ref[...], g)
            kb = _dup_half(kp_ref[...], kc_ref[...], g)
            vb = _dup_half(vp_ref[...], vc_ref[...], g)
            pu, es = _attn_probs(qst, kb, sk_ref[g], b_ref[...])
            inv = 1.0 / (_row_sums(pu.astype(BF)) + es)
            p = pu * _both(inv)
            dp = lax.dot_general(dos, vb, (((1,), (1,)), ((), ())), preferred_element_type=F32)
            dd = _row_sums((p * dp).astype(BF))
            ds = (p * (dp - _both(dd))).astype(BF)
            dq2 = jnp.dot(ds, kb, preferred_element_type=F32) * (HEAD_DIM ** -0.5)
            for t, tile in enumerate(_unstack_heads(dq2)):
                dq_ref[:, (g * 4 + t) * 128:(g * 4 + t + 1) * 128] = tile.astype(BF)
            dkf.append(_fold_halves(lax.dot_general(ds, qst, (((0,), (0,)), ((), ())), preferred_element_type=F32)))
            dvf.append(_fold_halves(lax.dot_general(p.astype(BF), dos, (((0,), (0,)), ((), ())),
                                                    preferred_element_type=F32)))
            dsr = -(es * inv * dd)
            upd = jnp.zeros((Q_PER_KV, 128), F32)
            for h in range(Q_PER_KV):
                upd = jnp.where(sub == h, jnp.sum(dsr[h * ATTN_BLOCK:(h + 1) * ATTN_BLOCK], axis=0, keepdims=True), upd)
            dsk_ref[g] += upd
        lo = _lane_lo((2 * ATTN_BLOCK, 128))
        dkb = jnp.where(lo, dkf[0], dkf[1])
        dvb = jnp.where(lo, dvf[0], dvf[1])
        r0 = pl.multiple_of(n * ATTN_BLOCK, ATTN_BLOCK)
        dk_ref[pl.ds(r0, ATTN_BLOCK), :] += dkb[ATTN_BLOCK:]
        dv_ref[pl.ds(r0, ATTN_BLOCK), :] += dvb[ATTN_BLOCK:]

        @pl.when(n > 0)
        def _():
            rp = pl.multiple_of((n - 1) * ATTN_BLOCK, ATTN_BLOCK)
            dk_ref[pl.ds(rp, ATTN_BLOCK), :] += dkb[:ATTN_BLOCK]
            dv_ref[pl.ds(rp, ATTN_BLOCK), :] += dvb[:ATTN_BLOCK]
    return _pcall(body, name=name, grid=(nb,), in_specs=[qs, qs, prev, cur, prev, cur, sink, bs],
                  out_specs=[qs, full, full, dsk_spec],
                  out_shape=[jax.ShapeDtypeStruct(q.shape, BF), jax.ShapeDtypeStruct((S, 128), F32),
                             jax.ShapeDtypeStruct((S, 128), F32), jax.ShapeDtypeStruct((N_KV_HEADS, Q_PER_KV, 128), F32)],
                  compiler_params=_params())(do, q, k, k, v, v, sink_rows, bias)


def _ada_fwd(c_all, ada_w):
    ncol = ada_w.shape[2]

    def body(c_ref, w_ref, o_ref):
        cv = c_ref[...]
        ca = (cv * _sigmoid(cv)).astype(BF)
        for l in range(DEPTH):
            o_ref[:, l * ncol:(l + 1) * ncol] = jnp.dot(ca, w_ref[l].astype(BF), preferred_element_type=F32)
    return _pcall(body, name="ada_fwd", out_shape=jax.ShapeDtypeStruct((N_DEV, DEPTH * ncol), F32),
                  compiler_params=_params())(c_all, ada_w)


def _ada_bwd(c_all, dm):
    ncol = dm.shape[2]

    def body(c_ref, dm_ref, o_ref):
        cv = c_ref[...]
        ca = (cv * _sigmoid(cv)).astype(BF)
        for l in range(DEPTH):
            o_ref[l] = lax.dot_general(ca, dm_ref[l].astype(BF), (((0,), (0,)), ((), ())), preferred_element_type=F32)
    return _pcall(body, name="ada_bwd", out_shape=jax.ShapeDtypeStruct((DEPTH, D_MODEL, ncol), F32),
                  compiler_params=_params())(c_all, dm)


def _adamw(w, g, m, v, *, name):
    R, C = w.shape
    tr = R
    for t in range(8, 513, 8):
        if R % t == 0:
            tr = t
    c1 = 1.0 - ADAM_B1 ** ADAM_STEP
    c2 = 1.0 - ADAM_B2 ** ADAM_STEP

    def body(w_ref, g_ref, m_ref, v_ref, d_ref, mo_ref, vo_ref):
        gv = g_ref[...]
        mn = ADAM_B1 * m_ref[...] + (1.0 - ADAM_B1) * gv
        vn = ADAM_B2 * v_ref[...] + (1.0 - ADAM_B2) * (gv * gv)
        mo_ref[...] = mn
        vo_ref[...] = vn
        d_ref[...] = -ADAM_LR * ((mn * (1.0 / c1)) / (jnp.sqrt(vn * (1.0 / c2)) + ADAM_EPS) + ADAM_WD * w_ref[...])
    spec = pl.BlockSpec((tr, C), lambda i: (i, 0))
    sh = jax.ShapeDtypeStruct((R, C), F32)
    return _pcall(body, name=name, grid=(R // tr,), in_specs=[spec] * 4, out_specs=[spec] * 3, out_shape=[sh, sh, sh],
                  compiler_params=_params())(w, g, m, v)


def _adamw_layers(w, g_layers, m, v, *, name):
    L, R, C = w.shape
    assert L == 2 and len(g_layers) == 2
    tr = R
    for t in range(8, 513, 8):
        if R % t == 0:
            tr = t
    c1 = 1.0 - ADAM_B1 ** ADAM_STEP
    c2 = 1.0 - ADAM_B2 ** ADAM_STEP

    def body(w_ref, g0_ref, g1_ref, m_ref, v_ref, go_ref, d_ref, mo_ref, vo_ref):
        gv = jnp.where(pl.program_id(0) == 0, g0_ref[...], g1_ref[...])
        go_ref[...] = gv
        mn = ADAM_B1 * m_ref[...] + (1.0 - ADAM_B1) * gv
        vn = ADAM_B2 * v_ref[...] + (1.0 - ADAM_B2) * (gv * gv)
        mo_ref[...] = mn
        vo_ref[...] = vn
        d_ref[...] = -ADAM_LR * ((mn * (1.0 / c1)) / (jnp.sqrt(vn * (1.0 / c2)) + ADAM_EPS) + ADAM_WD * w_ref[...])
    spec = pl.BlockSpec((None, tr, C), lambda l, i: (l, i, 0))
    sh = jax.ShapeDtypeStruct((L, R, C), F32)
    g_specs = [pl.BlockSpec((tr, C), lambda l, i, k=k: (jnp.where(l == k, i, 0), 0)) for k in range(L)]
    return _pcall(body, name=name, grid=(L, R // tr), in_specs=[spec] + g_specs + [spec, spec], out_specs=[spec] * 4,
                  out_shape=[sh] * 4, compiler_params=_params())(w, *g_layers, m, v)


def _sum8(parts, *, name):
    _, R, C = parts.shape
    tr = _tile(R, 512, 16)

    def body(p_ref, o_ref):
        acc = p_ref[0].astype(F32)
        for k in range(1, N_DEV):
            acc = acc + p_ref[k].astype(F32)
        o_ref[...] = acc
    return _pcall(body, name=name, grid=(R // tr,), in_specs=[pl.BlockSpec((N_DEV, tr, C), lambda i: (0, i, 0))],
                  out_specs=pl.BlockSpec((tr, C), lambda i: (i, 0)), out_shape=jax.ShapeDtypeStruct((R, C), F32),
                  compiler_params=_params())(parts)


MESH_ID = pl.DeviceIdType.MESH
ANY = pl.BlockSpec(memory_space=pl.ANY)


def _all_gather(x, *, name, after=None):
    R, C = x.shape
    extra = [] if after is None else [after]

    def body(x_ref, *rest):
        out_ref, send_sems, recv_sems, local_sem = rest[-4:]
        mx, my, mc = lax.axis_index("x"), lax.axis_index("y"), lax.axis_index("c")
        me, sibling = (mx, my, mc), (mx, my, 1 - mc)
        chips = [(1 - mx, my), (mx, 1 - my), (1 - mx, 1 - my)]

        def blk(px, py, pc):
            return out_ref.at[4 * px + 2 * py + pc]

        def copy(k, block, to, src=None):
            return pltpu.make_async_remote_copy(
                src_ref=blk(*block) if src is None else src, dst_ref=blk(*block),
                send_sem=send_sems.at[k], recv_sem=recv_sems.at[k], device_id=to, device_id_type=MESH_ID)

        mine = pltpu.make_async_copy(x_ref, blk(*me), local_sem)
        mine.start()
        first = [copy(0, me, sibling, src=x_ref)]
        first += [copy(1 + j, me, (*chip, mc), src=x_ref) for j, chip in enumerate(chips)]
        for cp in first:
            cp.start()
        passed = [copy(4 + j, (*chip, mc), sibling) for j, chip in enumerate(chips)]
        for j, chip in enumerate(chips):
            copy(1 + j, (*chip, mc), me).wait_recv()
            passed[j].start()
        copy(0, sibling, me).wait_recv()
        for j, chip in enumerate(chips):
            copy(4 + j, (*chip, 1 - mc), me).wait_recv()
        for cp in first + passed:
            cp.wait_send()
        mine.wait()
    return _pcall(body, name=name, in_specs=[ANY] * (1 + len(extra)), out_specs=ANY,
                  out_shape=jax.ShapeDtypeStruct((N_DEV, R, C), x.dtype),
                  scratch_shapes=[pltpu.SemaphoreType.DMA((7,)), pltpu.SemaphoreType.DMA((7,)), pltpu.SemaphoreType.DMA],
                  compiler_params=pltpu.CompilerParams(has_side_effects=True))(x, *extra)


HBM_SPEC = pl.BlockSpec(memory_space=pltpu.HBM)
SEM_SPEC = pl.BlockSpec(memory_space=pltpu.SEMAPHORE)
DATAFLOW = pltpu.SideEffectType.DATAFLOW_SIDE_EFFECTING


def _coords():
    return lax.axis_index("x"), lax.axis_index("y"), lax.axis_index("c")


def _other_chips(mx, my):
    return [(1 - mx, my), (mx, 1 - my), (1 - mx, 1 - my)]


def _plan_gather_ici(refs, send, recv):
    src, land = refs
    mx, my, mc = _coords()
    return [pltpu.make_async_remote_copy(src_ref=src, dst_ref=land.at[mc, 2 * mx + my], send_sem=send[j], recv_sem=recv[j],
                                         device_id=(px, py, mc), device_id_type=MESH_ID)
            for j, (px, py) in enumerate(_other_chips(mx, my))]


def _plan_gather_d2d(refs, send, recv):
    (land,) = refs
    mx, my, mc = _coords()
    return [pltpu.make_async_remote_copy(src_ref=land.at[mc], dst_ref=land.at[mc], send_sem=send[0], recv_sem=recv[0],
                                         device_id=(mx, my, 1 - mc), device_id_type=MESH_ID)]


def _plan_reduce_d2d(refs, send, recv):
    g, land = refs
    mx, my, mc = _coords()
    return [pltpu.make_async_remote_copy(src_ref=g.at[1 - mc], dst_ref=land, send_sem=send[0], recv_sem=recv[0],
                                         device_id=(mx, my, 1 - mc), device_id_type=MESH_ID)]


def _plan_reduce_ici(refs, send, recv):
    h, land = refs
    mx, my, mc = _coords()
    return [pltpu.make_async_remote_copy(src_ref=h.at[2 * px + py], dst_ref=land.at[j], send_sem=send[j], recv_sem=recv[j],
                                         device_id=(px, py, mc), device_id_type=MESH_ID)
            for j, (px, py) in enumerate(_other_chips(mx, my))]


def _rdma_start(bufs, n, plan, *, name, after=None):
    nb = len(bufs)
    extra = [] if after is None else [after]
    ne = len(extra)

    def body(*refs):
        ins, send, recv = refs[:nb], refs[nb + ne:nb + ne + n], refs[nb + ne + n:nb + ne + 2 * n]
        token = refs[-1]
        for cp in plan(ins, send, recv):
            cp.start()
        token[...] = jnp.zeros_like(token)
    out = _pcall(body, name=name,
                 out_shape=tuple([pltpu.SemaphoreType.DMA(())] * (2 * n) + [pltpu.HBM(b.shape, b.dtype) for b in bufs]
                                 + [jax.ShapeDtypeStruct((8, 128), F32)]),
                 in_specs=tuple([HBM_SPEC] * nb + [ANY] * ne),
                 out_specs=tuple([SEM_SPEC] * (2 * n) + [HBM_SPEC] * nb + [pl.BlockSpec(memory_space=pltpu.VMEM)]),
                 input_output_aliases={i: 2 * n + i for i in range(nb)},
                 compiler_params=pltpu.CompilerParams(has_side_effects=DATAFLOW))(
                     *[pltpu.with_memory_space_constraint(b, pltpu.HBM) for b in bufs], *extra)
    return list(out[:2 * n]), list(out[2 * n:2 * n + nb]), out[-1]


def _rdma_wait(sems, bufs, n, plan, after, *, name):
    nb = len(bufs)

    def body(*refs):
        ins, send, recv = refs[:nb], refs[nb:nb + n], refs[nb + n:nb + 2 * n]
        for cp in plan(ins, send, recv):
            cp.wait_send()
            cp.wait_recv()
    out = _pcall(body, name=name, out_shape=tuple(pltpu.HBM(b.shape, b.dtype) for b in bufs),
                 in_specs=tuple([HBM_SPEC] * nb + [SEM_SPEC] * (2 * n) + [ANY]), out_specs=tuple([HBM_SPEC] * nb),
                 input_output_aliases={i: i for i in range(nb)},
                 compiler_params=pltpu.CompilerParams(has_side_effects=DATAFLOW))(*bufs, *sems, after)
    return list(out)


def _sum_pair(g, land, cidx, *, name):
    _, nchip, R, C = g.shape
    tr = _tile(R, 1056, 16)

    def body(c_ref, g_ref, l_ref, o_ref):
        o_ref[...] = g_ref[...] + l_ref[...]
    grid_spec = pltpu.PrefetchScalarGridSpec(
        num_scalar_prefetch=1, grid=(nchip, R // tr),
        in_specs=[pl.BlockSpec((None, None, tr, C), lambda p, i, c_ref: (c_ref[0], p, i, 0)),
                  pl.BlockSpec((None, tr, C), lambda p, i, c_ref: (p, i, 0))],
        out_specs=pl.BlockSpec((None, tr, C), lambda p, i, c_ref: (p, i, 0)))
    return _pcall(body, name=name, grid_spec=grid_spec, out_shape=jax.ShapeDtypeStruct((nchip, R, C), BF),
                  compiler_params=_params())(cidx, g, land)


def _sum_chips(h, land, chipidx, *, name):
    _, R, C = h.shape
    tr = _tile(R, 1056, 16)

    def body(c_ref, h_ref, l_ref, o_ref):
        acc = h_ref[...].astype(F32)
        for j in range(3):
            acc = acc + l_ref[j].astype(F32)
        o_ref[...] = acc
    grid_spec = pltpu.PrefetchScalarGridSpec(
        num_scalar_prefetch=1, grid=(R // tr,),
        in_specs=[pl.BlockSpec((None, tr, C), lambda i, c_ref: (c_ref[0], i, 0)),
                  pl.BlockSpec((3, tr, C), lambda i, c_ref: (0, i, 0))],
        out_specs=pl.BlockSpec((tr, C), lambda i, c_ref: (i, 0)))
    return _pcall(body, name=name, grid_spec=grid_spec, out_shape=jax.ShapeDtypeStruct((R, C), F32),
                  compiler_params=_params())(chipidx, h, land)


PART_IN = ("w_in",)
PART_MIX = ("proj_a", "proj_b", "w_out")
PART_FFN = ("ffn_w_gate", "ffn_w_up", "ffn_w_down")


def _part_rows(names):
    return sum(BIG_ROWS[n] for n in names)


def _part_offsets(names):
    off, r = {}, 0
    for n in names:
        off[n] = r
        r += BIG_ROWS[n]
    return off


def _pack_shards(shards, l, names):
    return jnp.concatenate([(shards[n][l].T if n in COL_SHARDED else shards[n][l]).astype(BF) for n in names], axis=0)


def _unpack_weights(full8, names):
    off = _part_offsets(names)

    def whole(n):
        return full8[:, off[n]:off[n] + BIG_ROWS[n], :].reshape(N_DEV * BIG_ROWS[n], 1024)
    out = {}
    if "w_in" in names:
        wt_in = whole("w_in")
        out["wt_in"] = jnp.concatenate([wt_in[V_END:], wt_in[:V_END]], axis=0)
    for n in ("proj_a", "proj_b", "w_out"):
        if n in names:
            out[n] = whole(n)
    if "ffn_w_gate" in names:
        out["wt_gate"], out["wt_up"], out["w_down"] = whole("ffn_w_gate"), whole("ffn_w_up"), whole("ffn_w_down")
    return out


def _from_land(land):
    return land.transpose(1, 0, 2, 3).reshape(N_DEV, land.shape[2], 1024)


def _pack_grads(wg, names):
    full = {"proj_a": wg.get("proj_a"), "proj_b": wg.get("proj_b"), "w_out": wg.get("w_out"), "ffn_w_down": wg.get("w_down"),
            "ffn_w_gate": wg.get("wt_gate"), "ffn_w_up": wg.get("wt_up")}
    if "w_in" in names:
        full["w_in"] = jnp.concatenate([wg["wt_in"][P_Q:], wg["wt_in"][:P_Q]], axis=0)
    blocks = jnp.concatenate([full[n].reshape(N_DEV, BIG_ROWS[n], 1024) for n in names], axis=1)
    return blocks.reshape(4, 2, _part_rows(names), 1024).transpose(1, 0, 2, 3)


def _unpack_shard_grads(gs, names):
    off = _part_offsets(names)
    out = {}
    for n in names:
        blk = gs[off[n]:off[n] + BIG_ROWS[n]]
        out[n] = blk.T if n in COL_SHARDED else blk
    return out


def _rope_setup(positions):
    S = positions.shape[0]
    inv = ROPE_THETA ** (-jnp.arange(0, ROT_DIM, 2, dtype=F32) / ROT_DIM)
    lane = np.arange(128) % HEAD_DIM
    half = ROT_DIM // 2
    inv_row = jnp.where(lane < ROT_DIM, jnp.tile(inv, 128 // half), 0.0)[None, :].astype(F32)
    m1_row = jnp.asarray((lane < half).astype(np.float32))[None, :]
    m2_row = jnp.asarray(((lane >= half) & (lane < ROT_DIM)).astype(np.float32))[None, :]
    return (*_rope_tables(positions.astype(F32).reshape(S, 1), inv_row, m1_row, m2_row), _attn_bias())


def _hook(hooks, point, after):
    f = None if hooks is None else hooks.get(point)
    return None if f is None else f(after)


def _layer_fwd(l, x, mod_l, W, small, rope, hooks=None):
    rc, rs1, rs2, bias = rope
    sh1, sc1, g1, sh2, sc2, g2 = [mod_l[i * D_MODEL:(i + 1) * D_MODEL][None, :] for i in range(6)]
    nw1, nw2 = small["norm1_w"][l][None, :], small["norm2_w"][l][None, :]
    tok = _hook(hooks, "mm_in", x)
    h, (proj,) = _norm_mm(x, nw1, sc1, sh1, [W["wt_in"]], name=f"mm_in{l}", after=tok, tm=2048, tn_cap=768)
    q_r, k_r, v_b = _rope_fwd(proj, rc, rs1, rs2, name=f"rope_fwd{l}")
    sink_rows = jnp.repeat(small["attn_sinks"][l].reshape(N_KV_HEADS, Q_PER_KV), ATTN_BLOCK, axis=1)
    sink_rows = jnp.broadcast_to(sink_rows[..., None], sink_rows.shape + (128,))
    y_attn = _attn_fwd(q_r, k_r, v_b, sink_rows, bias, name=f"attn_fwd{l}")
    lnw, lnb = small["sgu_ln_w"][l][None, :], small["sgu_ln_b"][l][None, :]
    sgu_bt = small["sgu_b"][l].T
    y_sgu = _sgu_fwd(proj, lnw, lnb, small["sgu_w"][l], sgu_bt, name=f"sgu_fwd{l}", after=_hook(hooks, "sgu", y_attn))
    tok = _hook(hooks, "mm_pa", y_sgu)
    a_br, b_br, merged = _merge_fwd(y_sgu, y_attn, W["proj_a"], W["proj_b"], proj, name=f"merge_fwd{l}", after=tok)
    x1, o1 = _mm(merged, W["w_out"], nt=False, out_dtype=F32, name=f"mm_out{l}", res=x, gvec=g1)
    tok = _hook(hooks, "mm_gu", x1)
    h2, (a_g, a_u) = _norm_mm(x1, nw2, sc2, sh2, [W["wt_gate"], W["wt_up"]], name=f"mm_gu{l}", after=tok, tn_cap=1408)
    cw, cb = small["ffn_conv_w"][l], small["ffn_conv_b"][l][None, :]
    hf = _ffn_act_fwd(a_g, a_u, cw, cb, name=f"ffn_act_fwd{l}")
    x2, o2 = _mm(hf, W["w_down"], nt=False, out_dtype=F32, name=f"mm_down{l}", res=x1, gvec=g2)
    saved = dict(x=x, h=h, proj=proj, q_r=q_r, k_r=k_r, v_b=v_b, sink_rows=sink_rows, y_attn=y_attn, y_sgu=y_sgu,
                 a_br=a_br, b_br=b_br, merged=merged, x1=x1, o1=o1, h2=h2, a_g=a_g, a_u=a_u, hf=hf, o2=o2)
    return x2, saved


def _layer_bwd(l, dx, do2, dg2, mod_l, W, small, rope, sv, below=None, hooks=None, wg=None):
    rc, rs1, rs2, bias = rope
    sh1, sc1, g1, sh2, sc2, g2 = [mod_l[i * D_MODEL:(i + 1) * D_MODEL][None, :] for i in range(6)]
    nw1, nw2 = small["norm1_w"][l][None, :], small["norm2_w"][l][None, :]
    cw, cb = small["ffn_conv_w"][l], small["ffn_conv_b"][l][None, :]
    lnw, lnb = small["sgu_ln_w"][l][None, :], small["sgu_ln_b"][l][None, :]
    sgu_bt = small["sgu_b"][l].T
    wg = {} if wg is None else wg
    dhf = _mm(do2, W["w_down"], nt=True, out_dtype=BF, name=f"mm_down_dx{l}", after=_hook(hooks, "mm_down_dx", do2),
              tn_cap=1408)
    wg["w_down"] = _mm_tn(sv["hf"], do2, name=f"mm_down_dw{l}")
    dac, dup, dcw, dcb = _ffn_act_bwd_a(dhf, sv["a_g"], sv["a_u"], cw, cb, name=f"ffn_act_bwd_a{l}")
    da = _ffn_act_bwd_b(dac, cw, name=f"ffn_act_bwd_b{l}")
    dh2 = _mm([da, dup], [W["wt_gate"], W["wt_up"]], nt=False, out_dtype=F32, name=f"mm_gu_dx{l}",
              after=_hook(hooks, "mm_gu_dx", da))
    wg["wt_gate"] = _mm_tn(da, sv["h2"], name=f"mm_gate_dw{l}")
    wg["wt_up"] = _mm_tn(dup, sv["h2"], name=f"mm_up_dw{l}")
    dx1, dnw2, dsc2, dsh2, do1, dg1 = _normmod_bwd(dh2, sv["x1"], nw2, sc2, sh2, dx, (sv["o1"], g1), name=f"normmod2_bwd{l}")
    d_a, d_b, dproj = _merge_bwd(do1, W["w_out"], sv["a_br"], sv["b_br"], sv["proj"], name=f"merge_bwd{l}",
                                 after=_hook(hooks, "merge_bwd", do1))
    wg["w_out"] = _mm_tn(sv["merged"], do1, name=f"mm_out_dw{l}")
    dysgu = _mm(d_a, W["proj_a"], nt=True, out_dtype=F32, name=f"mm_pa_dx{l}", after=_hook(hooks, "mm_pa_dx", d_a))
    dyattn = _mm(d_b, W["proj_b"], nt=True, out_dtype=BF, name=f"mm_pb_dx{l}")
    wg["proj_a"] = _mm_tn(sv["y_sgu"], d_a, name=f"mm_pa_dw{l}")
    wg["proj_b"] = _mm_tn(sv["y_attn"], d_b, name=f"mm_pb_dw{l}")
    dproj, dlnw, dlnb, dsguw, dsgubt = _sgu_bwd(dysgu, sv["proj"], lnw, lnb, small["sgu_w"][l], sgu_bt, dproj,
                                                name=f"sgu_bwd{l}")
    dq_r, dk_r, dv_b, dsk = _attn_bwd(dyattn, sv["q_r"], sv["k_r"], sv["v_b"], sv["sink_rows"], bias, name=f"attn_bwd{l}")
    dproj = _rope_bwd(dq_r, dk_r, dv_b, rc, rs1, rs2, dproj, name=f"rope_bwd{l}")
    wg["wt_in"] = _mm_tn(dproj, sv["h"], name=f"mm_in_dw{l}")
    dh = _mm(dproj, W["wt_in"], nt=False, out_dtype=F32, name=f"mm_in_dx{l}", after=_hook(hooks, "mm_in_dx", wg["wt_in"]))
    dx0, dnw1, dsc1, dsh1, *gate_below = _normmod_bwd(dh, sv["x"], nw1, sc1, sh1, dx1, below, name=f"normmod1_bwd{l}")
    dmod = jnp.concatenate([dsh1, dsc1, dg1, dsh2, dsc2, dg2], axis=1)[0]
    sg = {"norm1_w": dnw1[0], "norm2_w": dnw2[0], "attn_sinks": dsk[:, :, 0].reshape(N_Q_HEADS),
          "sgu_ln_w": dlnw[0], "sgu_ln_b": dlnb[0], "sgu_w": dsguw, "sgu_b": dsgubt.T,
          "ffn_conv_w": dcw, "ffn_conv_b": dcb[0]}
    return (dx0, *gate_below), wg, sg, dmod


SMALL = ("ada_b", "norm1_w", "attn_sinks", "sgu_ln_w", "sgu_ln_b", "sgu_w", "sgu_b", "norm2_w", "ffn_conv_b", "final_norm_w")
WEIGHT_ORDER = ("ada_w", "ada_b", "norm1_w", "w_in", "attn_sinks", "sgu_ln_w", "sgu_ln_b", "sgu_w", "sgu_b", "proj_a", "proj_b",
                "w_out", "norm2_w", "ffn_w_gate", "ffn_w_up", "ffn_conv_w", "ffn_conv_b", "ffn_w_down", "final_norm_w")


def _flat_pack(arrs, rows):
    flat = jnp.concatenate([a.reshape(-1) for a in arrs])
    return jnp.pad(flat, (0, rows * 1024 - flat.shape[0])).reshape(rows, 1024)


def _flat_unpack(buf, shapes):
    flat = buf.reshape(-1)
    out, o = [], 0
    for s in shapes:
        n = int(np.prod(s))
        out.append(flat[o:o + n].reshape(s))
        o += n
    return out


def _adam2d(w, g, m, v, *, name):
    shp = w.shape
    r2 = (int(np.prod(shp[:-1])), shp[-1]) if len(shp) > 1 else (1, shp[0])
    d, mn, vn = _adamw(w.reshape(r2), g.reshape(r2), m.reshape(r2), v.reshape(r2), name=name)
    return d.reshape(shp), mn.reshape(shp), vn.reshape(shp)


def kernel(x, c, positions, ada_w, ada_b, norm1_w, w_in, attn_sinks, sgu_ln_w, sgu_ln_b, sgu_w, sgu_b, proj_a, proj_b, w_out, norm2_w, ffn_w_gate, ffn_w_up, ffn_conv_w, ffn_conv_b, ffn_w_down, final_norm_w, loss_target, m_ada_w, m_ada_b, m_norm1_w, m_w_in, m_attn_sinks, m_sgu_ln_w, m_sgu_ln_b, m_sgu_w, m_sgu_b, m_proj_a, m_proj_b, m_w_out, m_norm2_w, m_ffn_w_gate, m_ffn_w_up, m_ffn_conv_w, m_ffn_conv_b, m_ffn_w_down, m_final_norm_w, v_ada_w, v_ada_b, v_norm1_w, v_w_in, v_attn_sinks, v_sgu_ln_w, v_sgu_ln_b, v_sgu_w, v_sgu_b, v_proj_a, v_proj_b, v_w_out, v_norm2_w, v_ffn_w_gate, v_ffn_w_up, v_ffn_conv_w, v_ffn_conv_b, v_ffn_w_down, v_final_norm_w):
    wts = dict(ada_w=ada_w, ada_b=ada_b, norm1_w=norm1_w, w_in=w_in, attn_sinks=attn_sinks, sgu_ln_w=sgu_ln_w,
               sgu_ln_b=sgu_ln_b, sgu_w=sgu_w, sgu_b=sgu_b, proj_a=proj_a, proj_b=proj_b, w_out=w_out, norm2_w=norm2_w,
               ffn_w_gate=ffn_w_gate, ffn_w_up=ffn_w_up, ffn_conv_w=ffn_conv_w, ffn_conv_b=ffn_conv_b,
               ffn_w_down=ffn_w_down, final_norm_w=final_norm_w)
    mom = dict(ada_w=m_ada_w, ada_b=m_ada_b, norm1_w=m_norm1_w, w_in=m_w_in, attn_sinks=m_attn_sinks, sgu_ln_w=m_sgu_ln_w,
               sgu_ln_b=m_sgu_ln_b, sgu_w=m_sgu_w, sgu_b=m_sgu_b, proj_a=m_proj_a, proj_b=m_proj_b, w_out=m_w_out,
               norm2_w=m_norm2_w, ffn_w_gate=m_ffn_w_gate, ffn_w_up=m_ffn_w_up, ffn_conv_w=m_ffn_conv_w,
               ffn_conv_b=m_ffn_conv_b, ffn_w_down=m_ffn_w_down, final_norm_w=m_final_norm_w)
    var = dict(ada_w=v_ada_w, ada_b=v_ada_b, norm1_w=v_norm1_w, w_in=v_w_in, attn_sinks=v_attn_sinks, sgu_ln_w=v_sgu_ln_w,
               sgu_ln_b=v_sgu_ln_b, sgu_w=v_sgu_w, sgu_b=v_sgu_b, proj_a=v_proj_a, proj_b=v_proj_b, w_out=v_w_out,
               norm2_w=v_norm2_w, ffn_w_gate=v_ffn_w_gate, ffn_w_up=v_ffn_w_up, ffn_conv_w=v_ffn_conv_w,
               ffn_conv_b=v_ffn_conv_b, ffn_w_down=v_ffn_w_down, final_norm_w=v_final_norm_w)
    me = 4 * lax.axis_index("x") + 2 * lax.axis_index("y") + lax.axis_index("c")
    ada_cols = ada_w.shape[2]

    c_all = _all_gather(jnp.broadcast_to(c, (8, D_MODEL)), name="ag_c")[:, 0, :]
    prod = _ada_fwd(c_all, ada_w)
    prod_all = _all_gather(prod, name="ag_mod")
    mine = lax.dynamic_index_in_dim(prod_all, me, axis=1, keepdims=False)
    mod = jnp.stack([mine[:, l * ada_cols:(l + 1) * ada_cols].reshape(-1) for l in range(DEPTH)]) + ada_b

    conv_cols = ffn_conv_w.shape[2]
    conv_all = _all_gather(_flat_pack([ffn_conv_w], 8), name="ag_conv", after=mod)
    conv_full = jnp.stack([a.reshape(DEPTH, 3, conv_cols) for a in
                           [conv_all[j].reshape(-1)[:DEPTH * 3 * conv_cols] for j in range(N_DEV)]], axis=2)
    conv_full = conv_full.reshape(DEPTH, 3, FFN_DIM)
    small = {n: wts[n] for n in SMALL}
    small["ffn_conv_w"] = conv_full

    mx, my, mc = _coords()
    cidx = jnp.reshape(mc, (1,)).astype(jnp.int32)
    chipidx = jnp.reshape(2 * mx + my, (1,)).astype(jnp.int32)
    rope = _rope_setup(positions[0])

    class Gather:
        def __init__(self, src, tag):
            self.tag, self.src = tag, src
            self.land = lax.dynamic_update_slice(lax.empty((2, 4) + src.shape, src.dtype), src[None, None],
                                                 (mc, 2 * mx + my, 0, 0))

        def ici_start(self, after):
            self.sems, (self.src, self.land), tok = _rdma_start([self.src, self.land], 3, _plan_gather_ici,
                                                                name=f"ag_{self.tag}_ici_start", after=after)
            return tok

        def ici_wait_d2d_start(self, after):
            _, land = _rdma_wait(self.sems, [self.src, self.land], 3, _plan_gather_ici, after, name=f"ag_{self.tag}_ici_wait")
            self.sems, (self.land,), tok = _rdma_start([land], 1, _plan_gather_d2d, name=f"ag_{self.tag}_d2d_start")
            return tok

        def d2d_wait(self, after):
            (land,) = _rdma_wait(self.sems, [self.land], 1, _plan_gather_d2d, after, name=f"ag_{self.tag}_d2d_wait")
            return _from_land(land)

    def weights_job(names, l, tag):
        job = Gather(_pack_shards(wts, l, names), tag)
        job.weights = lambda after: _unpack_weights(job.d2d_wait(after), names)
        return job

    W0 = _unpack_weights(_all_gather(_pack_shards(wts, 0, PART_IN), name="ag_w0_in", after=conv_all), PART_IN)
    W1 = {}
    rest = PART_MIX + PART_FFN
    g_rest0 = weights_job(rest, 0, "w0_rest")
    g_in1, g_rest1 = weights_job(PART_IN, 1, "w1_in"), weights_job(rest, 1, "w1_rest")

    def rest0_then_layer1(after):
        W0.update(g_rest0.weights(after))
        return g_rest1.ici_start(g_in1.ici_start(W0["w_down"]))

    x1, sv0 = _layer_fwd(0, x[0], mod[0], W0, small, rope,
                         {"mm_in": lambda after: g_rest0.ici_start(W0["wt_in"]), "sgu": g_rest0.ici_wait_d2d_start,
                          "mm_pa": rest0_then_layer1, "mm_gu": g_in1.ici_wait_d2d_start})
    g_rest1.ici_wait_d2d_start(x1)
    x2, sv1 = _layer_fwd(1, x1, mod[1], W1, small, rope,
                         {"mm_in": lambda after: W1.update(g_in1.weights(after)),
                          "mm_pa": lambda after: W1.update(g_rest1.weights(after))})
    gate2 = [mod[l][5 * D_MODEL:][None, :] for l in range(DEPTH)]
    dx2, dfw, loss_tile, do2, dg2 = _head(x2, final_norm_w[None, :], loss_target[0], (sv1["o2"], gate2[1]))
    loss = lax.psum(loss_tile[0, 0], ("x", "y", "c"))

    class Reduce:
        def __init__(self, names, tag):
            self.names, self.tag, self.rows = names, tag, _part_rows(names)

        def d2d_start(self, wg, after=None):
            self.sems, self.bufs, tok = _rdma_start([_pack_grads(wg, self.names), lax.empty((4, self.rows, 1024), BF)], 1,
                                                    _plan_reduce_d2d, name=f"rs_{self.tag}_d2d_start", after=after)
            return tok

        def d2d_wait_ici_start(self, after):
            g_t, land_a = _rdma_wait(self.sems, self.bufs, 1, _plan_reduce_d2d, after, name=f"rs_{self.tag}_d2d_wait")
            h = _sum_pair(g_t, land_a, cidx, name=f"rs_{self.tag}_sum_pair")
            self.sems, self.bufs, tok = _rdma_start([h, lax.empty((3, self.rows, 1024), BF)], 3, _plan_reduce_ici,
                                                    name=f"rs_{self.tag}_ici_start")
            return tok

        def ici_wait(self, after):
            h_t, land_b = _rdma_wait(self.sems, self.bufs, 3, _plan_reduce_ici, after, name=f"rs_{self.tag}_ici_wait")
            return _unpack_shard_grads(_sum_chips(h_t, land_b, chipidx, name=f"rs_{self.tag}_sum_chips"), self.names)

    (dx1, do2, dg2), wg1, sg1, dmod1 = _layer_bwd(1, dx2, do2, dg2, mod[1], W1, small, rope, sv1, below=(sv0["o2"], gate2[0]))
    r_all1, r_ffn0, r_mix0 = Reduce(BIG, "g1"), Reduce(PART_FFN, "g0_ffn"), Reduce(PART_IN + PART_MIX, "g0_mix")
    tok1 = r_all1.d2d_start(wg1)
    wg0, shard1 = {}, {}

    def layer1_done_then_ffn0(after):
        shard1.update(r_all1.ici_wait(after))
        return r_ffn0.d2d_wait_ici_start(shard1["w_in"])

    (grad_x,), _, sg0, dmod0 = _layer_bwd(
        0, dx1, do2, dg2, mod[0], W0, small, rope, sv0, wg=wg0,
        hooks={"mm_down_dx": lambda after: tok1, "mm_gu_dx": r_all1.d2d_wait_ici_start,
               "merge_bwd": lambda after: r_ffn0.d2d_start(wg0, after), "mm_pa_dx": layer1_done_then_ffn0,
               "mm_in_dx": lambda after: r_mix0.d2d_wait_ici_start(r_mix0.d2d_start(wg0, after))})
    sg = {n: jnp.stack([sg0[n], sg1[n]]) for n in sg0}
    sg["final_norm_w"] = dfw[0]
    dmod = jnp.stack([dmod0, dmod1])
    vec_names = [n for n in SMALL if n not in ("ada_b", "sgu_w")] + ["ffn_conv_w"]
    vec_shapes = [(DEPTH, 6 * D_MODEL)] + [sg[n].shape for n in vec_names]
    vec_rows = -(-sum(int(np.prod(s)) for s in vec_shapes) // 1024 // 16) * 16
    sgu_rows = sgu_w.size // 1024
    g_small = Gather(jnp.concatenate([_flat_pack([dmod] + [sg[n] for n in vec_names], vec_rows),
                                      sg["sgu_w"].reshape(sgu_rows, 1024)], axis=0).astype(BF), "small")
    tok = g_small.ici_start(grad_x)

    shard0 = r_ffn0.ici_wait(tok)
    shard0.update(r_mix0.ici_wait(shard0["ffn_w_down"]))
    grads, delta, new_m, new_v = {}, {}, {}, {}
    for n in BIG:
        two = lambda a: a.reshape(DEPTH, -1, a.shape[-1])
        out = _adamw_layers(two(wts[n]), [shard0[n], shard1[n]], two(mom[n]), two(var[n]), name=f"adamw_{n}")
        grads[n], delta[n], new_m[n], new_v[n] = [o.reshape(wts[n].shape) for o in out]

    sm_all = g_small.d2d_wait(g_small.ici_wait_d2d_start(delta["ffn_w_gate"]))
    sm_sum = _sum8(sm_all, name="sum_small")
    vec_sum = _flat_unpack(sm_sum[:vec_rows], vec_shapes)
    grads["ada_b"] = vec_sum[0]
    for n, gsum in zip(vec_names, vec_sum[1:]):
        grads[n] = gsum
    grads["sgu_w"] = sm_sum[vec_rows:].reshape(sgu_w.shape)
    grads["ffn_conv_w"] = lax.dynamic_slice_in_dim(grads["ffn_conv_w"], me * conv_cols, conv_cols, axis=2)
    dmod_all = sm_all[:, :DEPTH * 6, :].astype(F32).reshape(N_DEV, DEPTH, 6 * D_MODEL)
    dm_mine = lax.dynamic_slice_in_dim(dmod_all, me * ada_cols, ada_cols, axis=2).transpose(1, 0, 2)
    dm_mine = jnp.pad(dm_mine, ((0, 0), (0, 8), (0, 0)))
    grads["ada_w"] = _ada_bwd(jnp.pad(c_all, ((0, 8), (0, 0))), dm_mine)

    packed_small = [n for n in SMALL if n != "sgu_w"]
    pshapes = [wts[n].shape for n in packed_small]
    prow = -(-sum(int(np.prod(s)) for s in pshapes) // 1024 // 8) * 8
    pk = lambda d: _flat_pack([d[n] for n in packed_small], prow)
    d_s, m_s, v_s = _adamw(pk(wts), pk(grads), pk(mom), pk(var), name="adamw_small")
    for n, dd, mm, vv in zip(packed_small, _flat_unpack(d_s, pshapes), _flat_unpack(m_s, pshapes), _flat_unpack(v_s, pshapes)):
        delta[n], new_m[n], new_v[n] = dd, mm, vv
    for n in WEIGHT_ORDER:
        if n not in delta:
            delta[n], new_m[n], new_v[n] = _adam2d(wts[n], grads[n], mom[n], var[n], name=f"adamw_{n}")
    return (loss, grad_x[None], *[grads[n] for n in WEIGHT_ORDER], *[delta[n] for n in WEIGHT_ORDER],
            *[new_m[n] for n in WEIGHT_ORDER], *[new_v[n] for n in WEIGHT_ORDER])
```

```python
import functools

import jax
import jax.numpy as jnp
import numpy as np
from jax import lax
from jax.experimental import pallas as pl
from jax.experimental.pallas import tpu as pltpu

F32 = jnp.float32
BF = jnp.bfloat16

N_DEV = 8
D_MODEL = 1024
DEPTH = 2
N_Q_HEADS = 16
N_KV_HEADS = 2
HEAD_DIM = 64
Q_PER_KV = N_Q_HEADS // N_KV_HEADS
ATTN_BLOCK = 128
ROPE_THETA = 500000.0
ROT_DIM = HEAD_DIM // 4
SGU_WIDTH = 1024
SGU_GROUPS = 8
SGU_CHUNK = 128
FFN_DIM = 2816
NORM_EPS = 1e-6
Q_END = N_Q_HEADS * HEAD_DIM
K_END = Q_END + N_KV_HEADS * HEAD_DIM
V_END = K_END + N_KV_HEADS * HEAD_DIM
Z_END = V_END + 2 * SGU_WIDTH
IN_COLS = Z_END + 2 * D_MODEL
P_Z, P_G, P_Q, P_K, P_V = 0, 2048, 4096, 5120, 5248

ADAM_LR = 0.001
ADAM_B1 = 0.9
ADAM_B2 = 0.999
ADAM_EPS = 1e-08
ADAM_WD = 0.01
ADAM_STEP = 10

VMEM_LIMIT_BYTES = 56 * 1024 * 1024

BIG = ("w_in", "proj_a", "proj_b", "w_out", "ffn_w_gate", "ffn_w_up", "ffn_w_down")
COL_SHARDED = ("w_in", "ffn_w_gate", "ffn_w_up")
BIG_SHAPE = {"w_in": (D_MODEL, IN_COLS), "proj_a": (SGU_WIDTH, D_MODEL), "proj_b": (Q_END, D_MODEL),
             "w_out": (D_MODEL, D_MODEL), "ffn_w_gate": (D_MODEL, FFN_DIM), "ffn_w_up": (D_MODEL, FFN_DIM),
             "ffn_w_down": (FFN_DIM, D_MODEL)}
BIG_ROWS = {n: BIG_SHAPE[n][0] * BIG_SHAPE[n][1] // N_DEV // 1024 for n in BIG}
LAYER_ROWS = sum(BIG_ROWS.values())


def _pcall(body, **kw):
    return pl.pallas_call(body, **kw)


def _params(**kw):
    return pltpu.CompilerParams(vmem_limit_bytes=VMEM_LIMIT_BYTES, **kw)


def _tile(n, cap, unit=128):
    if n <= cap:
        return n
    best = 0
    t = unit
    while t <= cap:
        if n % t == 0:
            best = t
        t += unit
    assert best, (n, cap, unit)
    return best


def _mm(a, b, *, nt, out_dtype, name, res=None, gvec=None, after=None, tm=None, tn_cap=1024):
    a_list = list(a) if isinstance(a, (list, tuple)) else [a]
    b_list = list(b) if isinstance(b, (list, tuple)) else [b]
    a, b = a_list[0], b_list[0]
    M, K = a.shape
    N = b.shape[0] if nt else b.shape[1]
    k_total = sum(x.shape[1] for x in a_list)
    tm = _tile(M, tm or (1024 if k_total <= 1024 else 512), 8)
    tn = _tile(N, tn_cap)
    dn = (((1,), (1,)), ((), ())) if nt else (((1,), (0,)), ((), ()))

    def b_spec_of(x):
        k = x.shape[1] if nt else x.shape[0]
        return pl.BlockSpec((tn, k), lambda i, j: (j, 0)) if nt else pl.BlockSpec((k, tn), lambda i, j: (0, j))
    b_spec = b_spec_of(b)
    o_spec = pl.BlockSpec((tm, tn), lambda i, j: (i, j))
    if res is None:
        extra = [] if after is None else [after]
        n = len(a_list)

        def body(*refs):
            o_ref = refs[-1]
            acc = None
            for a_ref, b_ref in zip(refs[:n], refs[n:2 * n]):
                d = lax.dot_general(a_ref[...].astype(BF), b_ref[...].astype(BF), dn, preferred_element_type=F32)
                acc = d if acc is None else acc + d
            o_ref[...] = acc.astype(out_dtype)
        return _pcall(body, name=name, grid=(M // tm, N // tn),
                      in_specs=[pl.BlockSpec((tm, x.shape[1]), lambda i, j: (i, 0)) for x in a_list]
                      + [b_spec_of(x) for x in b_list] + [ANY] * len(extra), out_specs=o_spec,
                      out_shape=jax.ShapeDtypeStruct((M, N), out_dtype), compiler_params=_params())(
                          *a_list, *b_list, *extra)

    def body_res(a_ref, b_ref, r_ref, g_ref, o_ref, acc_ref):
        acc = lax.dot_general(a_ref[...].astype(BF), b_ref[...].astype(BF), dn, preferred_element_type=F32)
        acc_ref[...] = acc.astype(BF)
        o_ref[...] = r_ref[...] + g_ref[...] * acc
    return _pcall(body_res, name=name, grid=(M // tm, N // tn),
                  in_specs=[pl.BlockSpec((tm, K), lambda i, j: (i, 0)), b_spec, o_spec,
                            pl.BlockSpec((1, tn), lambda i, j: (0, j))],
                  out_specs=[o_spec, o_spec],
                  out_shape=[jax.ShapeDtypeStruct((M, N), F32), jax.ShapeDtypeStruct((M, N), BF)],
                  compiler_params=_params())(a, b, res, gvec)


def _mm_tn(a, b, *, name, out_dtype=BF, tk=1024, tm_cap=1408, tn_cap=1024):
    S, M = a.shape
    N = b.shape[1]
    tk = _tile(S, tk, 8)
    tm = _tile(M, tm_cap)
    tn = _tile(N, tn_cap)
    nk = S // tk

    def body(a_ref, b_ref, o_ref, acc_ref):
        k = pl.program_id(2)

        @pl.when(k == 0)
        def _():
            acc_ref[...] = jnp.zeros_like(acc_ref)
        acc_ref[...] += lax.dot_general(a_ref[...].astype(BF), b_ref[...].astype(BF), (((0,), (0,)), ((), ())),
                                        preferred_element_type=F32)

        @pl.when(k == nk - 1)
        def _():
            o_ref[...] = acc_ref[...].astype(out_dtype)
    return _pcall(body, name=name, grid=(M // tm, N // tn, nk),
                  in_specs=[pl.BlockSpec((tk, tm), lambda i, j, k: (k, i)),
                            pl.BlockSpec((tk, tn), lambda i, j, k: (k, j))],
                  out_specs=pl.BlockSpec((tm, tn), lambda i, j, k: (i, j)),
                  out_shape=jax.ShapeDtypeStruct((M, N), out_dtype), scratch_shapes=[pltpu.VMEM((tm, tn), F32)],
                  compiler_params=_params())(a, b)


def _rms(x, w):
    return x * lax.rsqrt(jnp.mean(x * x, axis=-1, keepdims=True) + NORM_EPS) * w


def _normmod_fn(x, nw, sc, sh):
    return _rms(x, nw) * (1.0 + sc) + sh


def _gelu(x):
    return 0.5 * x * (1.0 + lax.erf(x * (2.0 ** -0.5)))


def _ln_gelu_fn(zv, w, b):
    v = _gelu(zv)
    mu = jnp.mean(v, axis=-1, keepdims=True)
    var = jnp.mean(jnp.square(v - mu), axis=-1, keepdims=True)
    return (v - mu) * lax.rsqrt(var + NORM_EPS) * w + b


def _sigmoid(x):
    return 1.0 / (1.0 + jnp.exp(-x))


def _row_spec(tm, n):
    return pl.BlockSpec((tm, n), lambda i: (i, 0))


def _vec_spec(n):
    return pl.BlockSpec((1, n), lambda i: (0, 0))


def _acc(ref, val):
    @pl.when(pl.program_id(0) == 0)
    def _():
        ref[...] = jnp.zeros_like(ref)
    ref[...] += val


def _norm_mm(x, nw, sc, sh, ws, *, name, after=None, tm=1024, tn_cap=768):
    S, K = x.shape
    N = ws[0].shape[0]
    tm = _tile(S, tm, 8)
    tn = _tile(N, tn_cap)
    nw_, ne = len(ws), 0 if after is None else 1

    def body(x_ref, nw_ref, sc_ref, sh_ref, *rest):
        w_refs = rest[:nw_]
        h_ref = rest[nw_ + ne]
        o_refs = rest[nw_ + ne + 1:nw_ + ne + 1 + nw_]
        h_s = rest[-1]

        @pl.when(pl.program_id(1) == 0)
        def _():
            hv = _normmod_fn(x_ref[...], nw_ref[...], sc_ref[...], sh_ref[...]).astype(BF)
            h_s[...] = hv
            h_ref[...] = hv
        for w_ref, o_ref in zip(w_refs, o_refs):
            o_ref[...] = lax.dot_general(h_s[...], w_ref[...], (((1,), (1,)), ((), ())),
                                         preferred_element_type=F32).astype(BF)
    row = pl.BlockSpec((tm, K), lambda i, j: (i, 0))
    vec = pl.BlockSpec((1, K), lambda i, j: (0, 0))
    out = pl.BlockSpec((tm, tn), lambda i, j: (i, j))
    res = _pcall(body, name=name, grid=(S // tm, N // tn),
                 in_specs=[row, vec, vec, vec] + [pl.BlockSpec((tn, K), lambda i, j: (j, 0))] * nw_ + [ANY] * ne,
                 out_specs=[row] + [out] * nw_,
                 out_shape=[jax.ShapeDtypeStruct((S, K), BF)] + [jax.ShapeDtypeStruct((S, N), BF)] * nw_,
                 scratch_shapes=[pltpu.VMEM((tm, K), BF)], compiler_params=_params())(
                     x, nw, sc, sh, *ws, *([] if after is None else [after]))
    return res[0], list(res[1:])


def _gate_bwd(dxv, o_ref, g_ref, do_ref, dg_ref):
    do_ref[...] = (dxv * g_ref[...]).astype(BF)
    _acc(dg_ref, jnp.sum(dxv * o_ref[...].astype(F32), axis=0, keepdims=True))


def _normmod_bwd(dh, x, nw, sc, sh, dres, gate, *, name, tm=256):
    S, Dm = x.shape
    tm = _tile(S, tm, 8)
    ng = 0 if gate is None else 2

    def body(dh_ref, x_ref, nw_ref, sc_ref, sh_ref, dres_ref, *rest):
        dx_ref, dnw_ref, dsc_ref, dsh_ref = rest[ng:ng + 4]
        xv, dy = x_ref[...], dh_ref[...]
        r = lax.rsqrt(jnp.mean(xv * xv, axis=-1, keepdims=True) + NORM_EPS)
        xn = xv * r
        t = dy * xn
        a = nw_ref[...] * (1.0 + sc_ref[...])
        dxv = dres_ref[...] + r * (dy * a - xn * jnp.mean(t * a, axis=-1, keepdims=True))
        dx_ref[...] = dxv
        ts = jnp.sum(t, axis=0, keepdims=True)
        _acc(dnw_ref, ts * (1.0 + sc_ref[...]))
        _acc(dsc_ref, ts * nw_ref[...])
        _acc(dsh_ref, jnp.sum(dy, axis=0, keepdims=True))
        if gate is not None:
            _gate_bwd(dxv, rest[0], rest[1], rest[ng + 4], rest[ng + 5])
    vec = jax.ShapeDtypeStruct((1, Dm), F32)
    gate_in = [] if gate is None else [_row_spec(tm, Dm), _vec_spec(Dm)]
    gate_out = [] if gate is None else [_row_spec(tm, Dm), _vec_spec(Dm)]
    gate_shape = [] if gate is None else [jax.ShapeDtypeStruct((S, Dm), BF), vec]
    return _pcall(body, name=name, grid=(S // tm,),
                  in_specs=[_row_spec(tm, Dm), _row_spec(tm, Dm), _vec_spec(Dm), _vec_spec(Dm), _vec_spec(Dm),
                            _row_spec(tm, Dm)] + gate_in,
                  out_specs=[_row_spec(tm, Dm), _vec_spec(Dm), _vec_spec(Dm), _vec_spec(Dm)] + gate_out,
                  out_shape=[jax.ShapeDtypeStruct((S, Dm), F32), vec, vec, vec] + gate_shape,
                  compiler_params=_params())(dh, x, nw, sc, sh, dres, *([] if gate is None else gate))


def _head(x, fw, target, gate, *, tm=256):
    S, Dm = x.shape
    tm = _tile(S, tm, 8)

    def body(x_ref, fw_ref, t_ref, o_ref, g_ref, dx_ref, dfw_ref, loss_ref, do_ref, dg_ref):
        xv, w = x_ref[...], fw_ref[...]
        r = lax.rsqrt(jnp.mean(xv * xv, axis=-1, keepdims=True) + NORM_EPS)
        xn = xv * r
        err = xn * w - t_ref[...]
        dy = err * (1.0 / Dm)
        t = dy * xn
        dx = r * (dy * w - xn * jnp.mean(t * w, axis=-1, keepdims=True))
        dx_ref[...] = dx
        _acc(dfw_ref, jnp.sum(t, axis=0, keepdims=True))
        part = 0.5 * jnp.sum(jnp.mean(err * err, axis=-1, keepdims=True), axis=0, keepdims=True)
        _acc(loss_ref, jnp.broadcast_to(part, (8, 128)))
        _gate_bwd(dx, o_ref, g_ref, do_ref, dg_ref)
    vec = jax.ShapeDtypeStruct((1, Dm), F32)
    return _pcall(body, name="head", grid=(S // tm,),
                  in_specs=[_row_spec(tm, Dm), _vec_spec(Dm), _row_spec(tm, Dm), _row_spec(tm, Dm), _vec_spec(Dm)],
                  out_specs=[_row_spec(tm, Dm), _vec_spec(Dm), pl.BlockSpec((8, 128), lambda i: (0, 0)),
                             _row_spec(tm, Dm), _vec_spec(Dm)],
                  out_shape=[jax.ShapeDtypeStruct((S, Dm), F32), vec, jax.ShapeDtypeStruct((8, 128), F32),
                             jax.ShapeDtypeStruct((S, Dm), BF), vec],
                  compiler_params=_params())(x, fw, target, *gate)


def _tril_mask():
    r = lax.broadcasted_iota(jnp.int32, (SGU_CHUNK, SGU_CHUNK), 0)
    c = lax.broadcasted_iota(jnp.int32, (SGU_CHUNK, SGU_CHUNK), 1)
    return c <= r


def _sgu_fwd(proj, lnw, lnb, w, b_t, *, name, after=None, tm=256):
    S = proj.shape[0]
    tm = _tile(S, tm, SGU_CHUNK)
    extra = [] if after is None else [after]

    def body(zu_ref, zv_ref, lnw_ref, lnb_ref, w_ref, bt_ref, *rest):
        o_ref = rest[-1]
        u = _gelu(zu_ref[...].astype(F32))
        vn = _ln_gelu_fn(zv_ref[...].astype(F32), lnw_ref[...], lnb_ref[...]).astype(BF)
        mask = _tril_mask()
        for g in range(SGU_GROUPS):
            wm = jnp.where(mask, w_ref[g], 0.0).astype(BF)
            cols = slice(g * 128, (g + 1) * 128)
            for ci in range(tm // SGU_CHUNK):
                rows = slice(ci * SGU_CHUNK, (ci + 1) * SGU_CHUNK)
                f = jnp.dot(wm, vn[rows, cols], preferred_element_type=F32) + bt_ref[:, g:g + 1]
                o_ref[rows, cols] = (u[rows, cols] * f).astype(BF)
    return _pcall(body, name=name, grid=(S // tm,),
                  in_specs=[pl.BlockSpec((tm, SGU_WIDTH), lambda i: (i, 0)), pl.BlockSpec((tm, SGU_WIDTH), lambda i: (i, 1)),
                            _vec_spec(SGU_WIDTH), _vec_spec(SGU_WIDTH),
                            pl.BlockSpec((SGU_GROUPS, 128, 128), lambda i: (0, 0, 0)),
                            pl.BlockSpec((128, SGU_GROUPS), lambda i: (0, 0))] + [ANY] * len(extra),
                  out_specs=_row_spec(tm, SGU_WIDTH), out_shape=jax.ShapeDtypeStruct((S, SGU_WIDTH), BF),
                  compiler_params=_params())(proj, proj, lnw, lnb, w, b_t, *extra)


def _sgu_bwd(dy, proj, lnw, lnb, w, b_t, dproj, *, name, tm=256):
    S = proj.shape[0]
    tm = _tile(S, tm, SGU_CHUNK)

    def body(dy_ref, zu_ref, zv_ref, lnw_ref, lnb_ref, w_ref, bt_ref, _, dz_ref, dlnw_ref, dlnb_ref, dw_ref, dbt_ref,
             f_s, dvn_s):
        first = pl.program_id(0) == 0

        @pl.when(first)
        def _():
            dw_ref[...] = jnp.zeros_like(dw_ref)
            dbt_ref[...] = jnp.zeros_like(dbt_ref)
        u, vjp_u = jax.vjp(_gelu, zu_ref[...].astype(F32))
        vn, vjp_v = jax.vjp(_ln_gelu_fn, zv_ref[...].astype(F32), lnw_ref[...], lnb_ref[...])
        vn = vn.astype(BF)
        dy_v = dy_ref[...]
        df = (dy_v * u).astype(BF)
        mask = _tril_mask()
        for g in range(SGU_GROUPS):
            wm = jnp.where(mask, w_ref[g], 0.0).astype(BF)
            cols = slice(g * 128, (g + 1) * 128)
            dwg = jnp.zeros((128, 128), F32)
            dbg = jnp.zeros((128, 1), F32)
            for ci in range(tm // SGU_CHUNK):
                rows = slice(ci * SGU_CHUNK, (ci + 1) * SGU_CHUNK)
                vn_c = vn[rows, cols]
                df_c = df[rows, cols]
                f_s[rows, cols] = jnp.dot(wm, vn_c, preferred_element_type=F32) + bt_ref[:, g:g + 1]
                dvn_s[rows, cols] = lax.dot_general(wm, df_c, (((0,), (0,)), ((), ())), preferred_element_type=F32)
                dwg = dwg + lax.dot_general(df_c, vn_c, (((1,), (1,)), ((), ())), preferred_element_type=F32)
                dbg = dbg + jnp.sum((dy_v[rows, cols] * u[rows, cols]), axis=1, keepdims=True)
            dw_ref[g] += jnp.where(mask, dwg, 0.0)
            dbt_ref[:, g:g + 1] += dbg
        (dzu,) = vjp_u(dy_v * f_s[...])
        dzv, dlnw, dlnb = vjp_v(dvn_s[...])
        dz_ref[:, :SGU_WIDTH] = dzu.astype(BF)
        dz_ref[:, SGU_WIDTH:] = dzv.astype(BF)
        _acc(dlnw_ref, dlnw)
        _acc(dlnb_ref, dlnb)
    vec = jax.ShapeDtypeStruct((1, SGU_WIDTH), F32)
    return _pcall(body, name=name, grid=(S // tm,),
                  in_specs=[_row_spec(tm, SGU_WIDTH),
                            pl.BlockSpec((tm, SGU_WIDTH), lambda i: (i, 0)), pl.BlockSpec((tm, SGU_WIDTH), lambda i: (i, 1)),
                            _vec_spec(SGU_WIDTH), _vec_spec(SGU_WIDTH),
                            pl.BlockSpec((SGU_GROUPS, 128, 128), lambda i: (0, 0, 0)),
                            pl.BlockSpec((128, SGU_GROUPS), lambda i: (0, 0)), ANY],
                  out_specs=[pl.BlockSpec((tm, 2 * SGU_WIDTH), lambda i: (i, P_Z // (2 * SGU_WIDTH))),
                             _vec_spec(SGU_WIDTH), _vec_spec(SGU_WIDTH),
                             pl.BlockSpec((SGU_GROUPS, 128, 128), lambda i: (0, 0, 0)),
                             pl.BlockSpec((128, SGU_GROUPS), lambda i: (0, 0))],
                  out_shape=[jax.ShapeDtypeStruct(dproj.shape, BF), vec, vec,
                             jax.ShapeDtypeStruct((SGU_GROUPS, 128, 128), F32),
                             jax.ShapeDtypeStruct((128, SGU_GROUPS), F32)],
                  scratch_shapes=[pltpu.VMEM((tm, SGU_WIDTH), F32), pltpu.VMEM((tm, SGU_WIDTH), F32)],
                  input_output_aliases={7: 0},
                  compiler_params=_params())(dy, proj, proj, lnw, lnb, w, b_t, dproj)


def _merge_fwd(y_sgu, y_attn, pa, pb, proj, *, name, after=None, tm=1024, tn=512):
    S, Dm = y_sgu.shape
    tm = _tile(S, tm, 8)
    nj = Dm // tn
    extra = [] if after is None else [after]

    def body(ys_ref, ya_ref, pa_ref, pb_ref, ga_ref, gb_ref, *rest):
        a_ref, b_ref, m_ref = rest[-3:]
        a = jnp.dot(ys_ref[...], pa_ref[...], preferred_element_type=F32)
        b = jnp.dot(ya_ref[...], pb_ref[...], preferred_element_type=F32)
        a_ref[...] = a.astype(BF)
        b_ref[...] = b.astype(BF)
        m_ref[...] = (_sigmoid(ga_ref[...].astype(F32)) * a + _sigmoid(gb_ref[...].astype(F32)) * b).astype(BF)
    row = pl.BlockSpec((tm, Dm), lambda i, j: (i, 0))
    col = pl.BlockSpec((Dm, tn), lambda i, j: (0, j))
    out = pl.BlockSpec((tm, tn), lambda i, j: (i, j))
    sh = jax.ShapeDtypeStruct((S, Dm), BF)
    return _pcall(body, name=name, grid=(S // tm, nj),
                  in_specs=[row, row, col, col, pl.BlockSpec((tm, tn), lambda i, j: (i, P_G // tn + j)),
                            pl.BlockSpec((tm, tn), lambda i, j: (i, (P_G + Dm) // tn + j))] + [ANY] * len(extra),
                  out_specs=[out, out, out], out_shape=[sh, sh, sh],
                  compiler_params=_params())(y_sgu, y_attn, pa, pb, proj, proj, *extra)


def _merge_bwd(do, w_out, a, b, proj, *, name, after=None, tm=512):
    S, Dm = a.shape
    tm = _tile(S, tm, 8)
    ga_blk, gb_blk = P_G // Dm, P_G // Dm + 1
    extra = [] if after is None else [after]

    def body(do_ref, w_ref, a_ref, b_ref, ga_ref, gb_ref, *rest):
        da_ref, db_ref, dg_ref = rest[-3:]
        dmv = lax.dot_general(do_ref[...], w_ref[...], (((1,), (1,)), ((), ())), preferred_element_type=F32)
        sa = _sigmoid(ga_ref[...].astype(F32))
        sb = _sigmoid(gb_ref[...].astype(F32))
        da_ref[...] = (dmv * sa).astype(BF)
        db_ref[...] = (dmv * sb).astype(BF)
        dg_ref[:, :Dm] = (dmv * a_ref[...].astype(F32) * sa * (1.0 - sa)).astype(BF)
        dg_ref[:, Dm:] = (dmv * b_ref[...].astype(F32) * sb * (1.0 - sb)).astype(BF)
    return _pcall(body, name=name, grid=(S // tm,),
                  in_specs=[_row_spec(tm, Dm), pl.BlockSpec((Dm, Dm), lambda i: (0, 0)), _row_spec(tm, Dm), _row_spec(tm, Dm),
                            pl.BlockSpec((tm, Dm), lambda i: (i, ga_blk)), pl.BlockSpec((tm, Dm), lambda i: (i, gb_blk))]
                  + [ANY] * len(extra),
                  out_specs=[_row_spec(tm, Dm), _row_spec(tm, Dm), pl.BlockSpec((tm, 2 * Dm), lambda i: (i, P_G // (2 * Dm)))],
                  out_shape=[jax.ShapeDtypeStruct((S, Dm), BF), jax.ShapeDtypeStruct((S, Dm), BF),
                             jax.ShapeDtypeStruct((S, IN_COLS), BF)],
                  compiler_params=_params())(do, w_out, a, b, proj, proj, *extra)


def _shift_rows(a, halo, k, up):
    n = a.shape[0]
    r8 = lax.broadcasted_iota(jnp.int32, (8, a.shape[1]), 0)
    if not up:
        rolled = pltpu.roll(a, k, 0)
        patch = jnp.where(r8 < k, pltpu.roll(halo, k, 0), rolled[:8])
        return jnp.concatenate([patch, rolled[8:]], axis=0)
    rolled = pltpu.roll(a, n - k, 0)
    patch = jnp.where(r8 >= 8 - k, pltpu.roll(halo, 8 - k, 0), rolled[n - 8:])
    return jnp.concatenate([rolled[:n - 8], patch], axis=0)


def _conv_taps(a, halo):
    return _shift_rows(a, halo, 2, False), _shift_rows(a, halo, 1, False), a


HALO = 16


def _prev_halo_spec(tm, Fd):
    return pl.BlockSpec((HALO, Fd), lambda i: (jnp.maximum(i * (tm // HALO) - 1, 0), 0))


def _conv_fwd(a_ref, halo_ref, cw_ref, cb_ref):
    halo = jnp.where(pl.program_id(0) > 0, halo_ref[...].astype(F32)[HALO - 8:], 0.0)
    t0, t1, t2 = _conv_taps(a_ref[...].astype(F32), halo)
    return t0, t1, t2, cb_ref[...] + cw_ref[0:1, :] * t0 + cw_ref[1:2, :] * t1 + cw_ref[2:3, :] * t2


def _ffn_act_fwd(a, up, cw, cb, *, name, tm=256):
    S, Fd = a.shape
    tm = _tile(S, tm, HALO)

    def body(a_ref, up_ref, halo_ref, cw_ref, cb_ref, o_ref, ac_ref):
        _, _, _, ac = _conv_fwd(a_ref, halo_ref, cw_ref, cb_ref)
        ac_ref[...] = ac.astype(BF)
        o_ref[...] = (ac * _sigmoid(ac) * up_ref[...].astype(F32)).astype(BF)
    sh = jax.ShapeDtypeStruct((S, Fd), BF)
    return _pcall(body, name=name, grid=(S // tm,),
                  in_specs=[_row_spec(tm, Fd), _row_spec(tm, Fd), _prev_halo_spec(tm, Fd),
                            pl.BlockSpec((3, Fd), lambda i: (0, 0)), _vec_spec(Fd)],
                  out_specs=[_row_spec(tm, Fd), _row_spec(tm, Fd)], out_shape=[sh, sh],
                  compiler_params=_params())(a, up, a, cw, cb)


def _ffn_act_bwd_a(dhf, ac, up, *, name, tm=512):
    S, Fd = ac.shape
    tm = _tile(S, tm, HALO)

    def body(dhf_ref, ac_ref, up_ref, dac_ref, dup_ref, dcb_ref):
        acv = ac_ref[...].astype(F32)
        s = _sigmoid(acv)
        dhf_v = dhf_ref[...].astype(F32)
        dup_ref[...] = (dhf_v * acv * s).astype(BF)
        dac = dhf_v * up_ref[...].astype(F32) * (s * (1.0 + acv * (1.0 - s)))
        dac_ref[...] = dac.astype(BF)
        _acc(dcb_ref, jnp.sum(dac, axis=0, keepdims=True))
    sh = jax.ShapeDtypeStruct((S, Fd), BF)
    return _pcall(body, name=name, grid=(S // tm,), in_specs=[_row_spec(tm, Fd)] * 3,
                  out_specs=[_row_spec(tm, Fd), _row_spec(tm, Fd), _vec_spec(Fd)],
                  out_shape=[sh, sh, jax.ShapeDtypeStruct((1, Fd), F32)], compiler_params=_params())(dhf, ac, up)


def _ffn_act_bwd_b(dac, a, cw, *, name, tm=256):
    S, Fd = dac.shape
    tm = _tile(S, tm, HALO)
    last = S // tm - 1

    def body(d_ref, halo_ref, a_ref, cw_ref, o_ref, dcw_ref):
        halo = jnp.where(pl.program_id(0) < last, halo_ref[...].astype(F32)[:8], 0.0)
        d = d_ref[...].astype(F32)
        d1, d2 = _shift_rows(d, halo, 1, True), _shift_rows(d, halo, 2, True)
        o_ref[...] = (cw_ref[2:3, :] * d + cw_ref[1:2, :] * d1 + cw_ref[0:1, :] * d2).astype(BF)
        av = a_ref[...].astype(F32)
        _acc(dcw_ref, jnp.concatenate([jnp.sum(av * d2, axis=0, keepdims=True),
                                       jnp.sum(av * d1, axis=0, keepdims=True),
                                       jnp.sum(av * d, axis=0, keepdims=True)], axis=0))
    return _pcall(body, name=name, grid=(S // tm,),
                  in_specs=[_row_spec(tm, Fd),
                            pl.BlockSpec((HALO, Fd), lambda i: (jnp.minimum((i + 1) * (tm // HALO), S // HALO - 1), 0)),
                            _row_spec(tm, Fd), pl.BlockSpec((3, Fd), lambda i: (0, 0))],
                  out_specs=[_row_spec(tm, Fd), pl.BlockSpec((3, Fd), lambda i: (0, 0))],
                  out_shape=[jax.ShapeDtypeStruct((S, Fd), BF), jax.ShapeDtypeStruct((3, Fd), F32)],
                  compiler_params=_params())(dac, dac, a, cw)


def _rope_tables(pos_col, inv_row, m1_row, m2_row):
    S = pos_col.shape[0]
    tm = _tile(S, 512, 8)

    def body(p_ref, inv_ref, m1_ref, m2_ref, c_ref, s1_ref, s2_ref):
        ang = p_ref[...] * inv_ref[...]
        sn = jnp.sin(ang)
        c_ref[...] = jnp.cos(ang)
        s1_ref[...] = -sn * m1_ref[...]
        s2_ref[...] = sn * m2_ref[...]
    sh = jax.ShapeDtypeStruct((S, 128), F32)
    return _pcall(body, name="rope_tables", grid=(S // tm,),
                  in_specs=[pl.BlockSpec((tm, 1), lambda i: (i, 0)), _vec_spec(128), _vec_spec(128), _vec_spec(128)],
                  out_specs=[_row_spec(tm, 128)] * 3, out_shape=[sh, sh, sh], compiler_params=_params())(
                      pos_col, inv_row, m1_row, m2_row)


def _rope_apply(x, c, s1, s2):
    outs = []
    for j in range(x.shape[1] // 128):
        xj = x[:, j * 128:(j + 1) * 128]
        outs.append(xj * c + pltpu.roll(xj, 120, 1) * s1 + pltpu.roll(xj, 8, 1) * s2)
    return outs[0] if len(outs) == 1 else jnp.concatenate(outs, axis=1)


def _rope_apply_t(d, c, s1, s2):
    outs = []
    for j in range(d.shape[1] // 128):
        dj = d[:, j * 128:(j + 1) * 128]
        outs.append(dj * c + pltpu.roll(dj * s1, 8, 1) + pltpu.roll(dj * s2, 120, 1))
    return outs[0] if len(outs) == 1 else jnp.concatenate(outs, axis=1)


def _rope_fwd(proj, c, s1, s2, *, name, tm=512):
    S = proj.shape[0]
    tm = _tile(S, tm, 8)

    def body(q_ref, k_ref, v_ref, c_ref, s1_ref, s2_ref, qo_ref, ko_ref, vo_ref):
        cv, s1v, s2v = c_ref[...], s1_ref[...], s2_ref[...]
        qo_ref[...] = (_rope_apply(q_ref[...].astype(F32), cv, s1v, s2v) * (HEAD_DIM ** -0.5)).astype(BF)
        ko_ref[...] = _rope_apply(k_ref[...].astype(F32), cv, s1v, s2v).astype(BF)
        vo_ref[...] = v_ref[...].astype(BF)
    return _pcall(body, name=name, grid=(S // tm,),
                  in_specs=[pl.BlockSpec((tm, Q_END), lambda i: (i, P_Q // Q_END)),
                            pl.BlockSpec((tm, 128), lambda i: (i, P_K // 128)),
                            pl.BlockSpec((tm, 128), lambda i: (i, P_V // 128)),
                            _row_spec(tm, 128), _row_spec(tm, 128), _row_spec(tm, 128)],
                  out_specs=[_row_spec(tm, Q_END), _row_spec(tm, 128), _row_spec(tm, 128)],
                  out_shape=[jax.ShapeDtypeStruct((S, Q_END), BF), jax.ShapeDtypeStruct((S, 128), BF),
                             jax.ShapeDtypeStruct((S, 128), BF)],
                  compiler_params=_params())(proj, proj, proj, c, s1, s2)


def _rope_bwd(dq, dk, dv, c, s1, s2, dproj, *, name, tm=512):
    S = dq.shape[0]
    tm = _tile(S, tm, 8)
    tabs = [_row_spec(tm, 128)] * 3
    shape = jax.ShapeDtypeStruct(dproj.shape, BF)

    def body_q(dq_ref, c_ref, s1_ref, s2_ref, _, o_ref):
        o_ref[...] = _rope_apply_t(dq_ref[...].astype(F32), c_ref[...], s1_ref[...], s2_ref[...]).astype(BF)
    dproj = _pcall(body_q, name=name + "_q", grid=(S // tm,), in_specs=[_row_spec(tm, Q_END)] + tabs + [ANY],
                   out_specs=pl.BlockSpec((tm, Q_END), lambda i: (i, P_Q // Q_END)), out_shape=shape,
                   input_output_aliases={4: 0}, compiler_params=_params())(dq, c, s1, s2, dproj)

    def body_kv(dk_ref, dv_ref, c_ref, s1_ref, s2_ref, _, o_ref):
        o_ref[:, :128] = _rope_apply_t(dk_ref[...], c_ref[...], s1_ref[...], s2_ref[...]).astype(BF)
        o_ref[:, 128:] = dv_ref[...].astype(BF)
    return _pcall(body_kv, name=name + "_kv", grid=(S // tm,),
                  in_specs=[_row_spec(tm, 128), _row_spec(tm, 128)] + tabs + [ANY],
                  out_specs=pl.BlockSpec((tm, 256), lambda i: (i, P_K // 256)), out_shape=shape,
                  input_output_aliases={5: 0}, compiler_params=_params())(dk, dv, c, s1, s2, dproj)


def _lane_lo(shape):
    return lax.broadcasted_iota(jnp.int32, shape, 1) < HEAD_DIM


def _stack_heads(x, g):
    lo = _lane_lo((ATTN_BLOCK, 128))
    zero = jnp.zeros((ATTN_BLOCK, 128), x.dtype)
    parts = []
    for p in range(Q_PER_KV // 2):
        xp = x[:, (g * 4 + p) * 128:(g * 4 + p + 1) * 128]
        parts += [jnp.where(lo, xp, zero), jnp.where(lo, zero, xp)]
    return jnp.concatenate(parts, axis=0)


def _unstack_heads(o2):
    lo = _lane_lo((ATTN_BLOCK, 128))
    return [jnp.where(lo, o2[2 * p * ATTN_BLOCK:(2 * p + 1) * ATTN_BLOCK], o2[(2 * p + 1) * ATTN_BLOCK:(2 * p + 2) * ATTN_BLOCK])
            for p in range(Q_PER_KV // 2)]


def _dup_half(prev, cur, g):
    x = jnp.concatenate([prev, cur], axis=0).astype(F32)
    lo = _lane_lo(x.shape)
    r = pltpu.roll(x, HEAD_DIM, 1)
    return (jnp.where(lo, x, r) if g == 0 else jnp.where(lo, r, x)).astype(BF)


def _fold_halves(x):
    return x + pltpu.roll(x, HEAD_DIM, 1)


def _attn_bias():
    i = lax.broadcasted_iota(jnp.int32, (Q_PER_KV * ATTN_BLOCK, 2 * ATTN_BLOCK), 0) & (ATTN_BLOCK - 1)
    j = lax.broadcasted_iota(jnp.int32, (Q_PER_KV * ATTN_BLOCK, 2 * ATTN_BLOCK), 1)
    band = (j > i) & (j <= i + ATTN_BLOCK)
    return jnp.stack([jnp.where(band & (j >= ATTN_BLOCK), 0.0, -jnp.inf), jnp.where(band, 0.0, -jnp.inf)]).astype(F32)


def _both(x):
    return jnp.concatenate([x, x], axis=1)


def _row_sums(x_bf):
    return jnp.dot(x_bf, jnp.ones((x_bf.shape[1], 128), BF), preferred_element_type=F32)


def _attn_probs(qs, kb, sink, bias):
    s = lax.dot_general(qs, kb, (((1,), (1,)), ((), ())), preferred_element_type=F32) + bias
    m = jnp.maximum(jnp.broadcast_to(jnp.max(s, axis=-1, keepdims=True), sink.shape), sink)
    return jnp.exp(s - _both(m)), jnp.exp(sink - m)


def _attn_specs(S):
    nb = S // ATTN_BLOCK
    qs = pl.BlockSpec((ATTN_BLOCK, Q_END), lambda n: (n, 0))
    cur = pl.BlockSpec((ATTN_BLOCK, 128), lambda n: (n, 0))
    prev = pl.BlockSpec((ATTN_BLOCK, 128), lambda n: (jnp.maximum(n - 1, 0), 0))
    sink = pl.BlockSpec((N_KV_HEADS, Q_PER_KV * ATTN_BLOCK, 128), lambda n: (0, 0, 0))
    bias = pl.BlockSpec((None, Q_PER_KV * ATTN_BLOCK, 2 * ATTN_BLOCK), lambda n: (jnp.minimum(n, 1), 0, 0))
    return nb, qs, cur, prev, sink, bias


def _attn_fwd(q, k, v, sink_rows, bias, *, name):
    S = q.shape[0]
    nb, qs, cur, prev, sink, bs = _attn_specs(S)

    def body(q_ref, kp_ref, kc_ref, vp_ref, vc_ref, sk_ref, b_ref, o_ref):
        for g in range(N_KV_HEADS):
            kb = _dup_half(kp_ref[...], kc_ref[...], g)
            vb = _dup_half(vp_ref[...], vc_ref[...], g)
            p, es = _attn_probs(_stack_heads(q_ref[...], g), kb, sk_ref[g], b_ref[...])
            ones = jnp.ones((2 * ATTN_BLOCK, 128), BF)
            o3 = jnp.dot(p.astype(BF), jnp.concatenate([vb, ones], axis=1), preferred_element_type=F32)
            o2 = o3[:, :128] / (o3[:, 128:] + es)
            for t, tile in enumerate(_unstack_heads(o2)):
                o_ref[:, (g * 4 + t) * 128:(g * 4 + t + 1) * 128] = tile.astype(BF)
    return _pcall(body, name=name, grid=(nb,), in_specs=[qs, prev, cur, prev, cur, sink, bs], out_specs=qs,
                  out_shape=jax.ShapeDtypeStruct(q.shape, BF), compiler_params=_params())(q, k, k, v, v, sink_rows, bias)


def _attn_bwd(do, q, k, v, sink_rows, bias, *, name):
    S = q.shape[0]
    nb, qs, cur, prev, sink, bs = _attn_specs(S)
    full = pl.BlockSpec((S, 128), lambda n: (0, 0))
    dsk_spec = pl.BlockSpec((N_KV_HEADS, Q_PER_KV, 128), lambda n: (0, 0, 0))

    def body(do_ref, q_ref, kp_ref, kc_ref, vp_ref, vc_ref, sk_ref, b_ref, dq_ref, dk_ref, dv_ref, dsk_ref):
        n = pl.program_id(0)

        @pl.when(n == 0)
        def _():
            dk_ref[...] = jnp.zeros_like(dk_ref)
            dv_ref[...] = jnp.zeros_like(dv_ref)
            dsk_ref[...] = jnp.zeros_like(dsk_ref)
        sub = lax.broadcasted_iota(jnp.int32, (Q_PER_KV, 128), 0)
        dkf, dvf = [], []
        for g in range(N_KV_HEADS):
            qst = _stack_heads(q_ref[...], g)
            dos = _stack_heads(do_ref[...], g)
            kb = _dup_half(kp_ref[...], kc_ref[...], g)
            vb = _dup_half(vp_ref[...], vc_ref[...], g)
            pu, es = _attn_probs(qst, kb, sk_ref[g], b_ref[...])
            inv = 1.0 / (_row_sums(pu.astype(BF)) + es)
            p = pu * _both(inv)
            dp = lax.dot_general(dos, vb, (((1,), (1,)), ((), ())), preferred_element_type=F32)
            dd = _row_sums((p * dp).astype(BF))
            ds = (p * (dp - _both(dd))).astype(BF)
            dq2 = jnp.dot(ds, kb, preferred_element_type=F32) * (HEAD_DIM ** -0.5)
            for t, tile in enumerate(_unstack_heads(dq2)):
                dq_ref[:, (g * 4 + t) * 128:(g * 4 + t + 1) * 128] = tile.astype(BF)
            dkf.append(_fold_halves(lax.dot_general(ds, qst, (((0,), (0,)), ((), ())), preferred_element_type=F32)))
            dvf.append(_fold_halves(lax.dot_general(p.astype(BF), dos, (((0,), (0,)), ((), ())),
                                                    preferred_element_type=F32)))
            dsr = -(es * inv * dd)
            upd = jnp.zeros((Q_PER_KV, 128), F32)
            for h in range(Q_PER_KV):
                upd = jnp.where(sub == h, jnp.sum(dsr[h * ATTN_BLOCK:(h + 1) * ATTN_BLOCK], axis=0, keepdims=True), upd)
            dsk_ref[g] += upd
        lo = _lane_lo((2 * ATTN_BLOCK, 128))
        dkb = jnp.where(lo, dkf[0], dkf[1])
        dvb = jnp.where(lo, dvf[0], dvf[1])
        r0 = pl.multiple_of(n * ATTN_BLOCK, ATTN_BLOCK)
        dk_ref[pl.ds(r0, ATTN_BLOCK), :] += dkb[ATTN_BLOCK:]
        dv_ref[pl.ds(r0, ATTN_BLOCK), :] += dvb[ATTN_BLOCK:]

        @pl.when(n > 0)
        def _():
            rp = pl.multiple_of((n - 1) * ATTN_BLOCK, ATTN_BLOCK)
            dk_ref[pl.ds(rp, ATTN_BLOCK), :] += dkb[:ATTN_BLOCK]
            dv_ref[pl.ds(rp, ATTN_BLOCK), :] += dvb[:ATTN_BLOCK]
    return _pcall(body, name=name, grid=(nb,), in_specs=[qs, qs, prev, cur, prev, cur, sink, bs],
                  out_specs=[qs, full, full, dsk_spec],
                  out_shape=[jax.ShapeDtypeStruct(q.shape, BF), jax.ShapeDtypeStruct((S, 128), F32),
                             jax.ShapeDtypeStruct((S, 128), F32), jax.ShapeDtypeStruct((N_KV_HEADS, Q_PER_KV, 128), F32)],
                  compiler_params=_params())(do, q, k, k, v, v, sink_rows, bias)


def _ada_fwd(c_all, ada_w):
    ncol = ada_w.shape[2]

    def body(c_ref, w_ref, o_ref):
        cv = c_ref[...]
        ca = (cv * _sigmoid(cv)).astype(BF)
        for l in range(DEPTH):
            o_ref[:, l * ncol:(l + 1) * ncol] = jnp.dot(ca, w_ref[l].astype(BF), preferred_element_type=F32)
    return _pcall(body, name="ada_fwd", out_shape=jax.ShapeDtypeStruct((N_DEV, DEPTH * ncol), F32),
                  compiler_params=_params())(c_all, ada_w)


def _ada_bwd(c_all, dm):
    ncol = dm.shape[2]

    def body(c_ref, dm_ref, o_ref):
        cv = c_ref[...]
        ca = (cv * _sigmoid(cv)).astype(BF)
        for l in range(DEPTH):
            o_ref[l] = lax.dot_general(ca, dm_ref[l].astype(BF), (((0,), (0,)), ((), ())), preferred_element_type=F32)
    return _pcall(body, name="ada_bwd", out_shape=jax.ShapeDtypeStruct((DEPTH, D_MODEL, ncol), F32),
                  compiler_params=_params())(c_all, dm)


def _adamw(w, g, m, v, *, name):
    R, C = w.shape
    tr = R
    for t in range(8, 513, 8):
        if R % t == 0:
            tr = t
    c1 = 1.0 - ADAM_B1 ** ADAM_STEP
    c2 = 1.0 - ADAM_B2 ** ADAM_STEP

    def body(w_ref, g_ref, m_ref, v_ref, d_ref, mo_ref, vo_ref):
        gv = g_ref[...]
        mn = ADAM_B1 * m_ref[...] + (1.0 - ADAM_B1) * gv
        vn = ADAM_B2 * v_ref[...] + (1.0 - ADAM_B2) * (gv * gv)
        mo_ref[...] = mn
        vo_ref[...] = vn
        d_ref[...] = -ADAM_LR * ((mn * (1.0 / c1)) / (jnp.sqrt(vn * (1.0 / c2)) + ADAM_EPS) + ADAM_WD * w_ref[...])
    spec = pl.BlockSpec((tr, C), lambda i: (i, 0))
    sh = jax.ShapeDtypeStruct((R, C), F32)
    return _pcall(body, name=name, grid=(R // tr,), in_specs=[spec] * 4, out_specs=[spec] * 3, out_shape=[sh, sh, sh],
                  compiler_params=_params())(w, g, m, v)


def _adamw_layers(w, g_layers, m, v, *, name):
    L, R, C = w.shape
    assert L == 2 and len(g_layers) == 2
    tr = R
    for t in range(8, 513, 8):
        if R % t == 0:
            tr = t
    c1 = 1.0 - ADAM_B1 ** ADAM_STEP
    c2 = 1.0 - ADAM_B2 ** ADAM_STEP

    def body(w_ref, g0_ref, g1_ref, m_ref, v_ref, go_ref, d_ref, mo_ref, vo_ref):
        gv = jnp.where(pl.program_id(0) == 0, g0_ref[...], g1_ref[...])
        go_ref[...] = gv
        mn = ADAM_B1 * m_ref[...] + (1.0 - ADAM_B1) * gv
        vn = ADAM_B2 * v_ref[...] + (1.0 - ADAM_B2) * (gv * gv)
        mo_ref[...] = mn
        vo_ref[...] = vn
        d_ref[...] = -ADAM_LR * ((mn * (1.0 / c1)) / (jnp.sqrt(vn * (1.0 / c2)) + ADAM_EPS) + ADAM_WD * w_ref[...])
    spec = pl.BlockSpec((None, tr, C), lambda l, i: (l, i, 0))
    sh = jax.ShapeDtypeStruct((L, R, C), F32)
    g_specs = [pl.BlockSpec((tr, C), lambda l, i, k=k: (jnp.where(l == k, i, 0), 0)) for k in range(L)]
    return _pcall(body, name=name, grid=(L, R // tr), in_specs=[spec] + g_specs + [spec, spec], out_specs=[spec] * 4,
                  out_shape=[sh] * 4, compiler_params=_params())(w, *g_layers, m, v)


def _sum8(parts, *, name):
    _, R, C = parts.shape
    tr = _tile(R, 512, 16)

    def body(p_ref, o_ref):
        acc = p_ref[0].astype(F32)
        for k in range(1, N_DEV):
            acc = acc + p_ref[k].astype(F32)
        o_ref[...] = acc
    return _pcall(body, name=name, grid=(R // tr,), in_specs=[pl.BlockSpec((N_DEV, tr, C), lambda i: (0, i, 0))],
                  out_specs=pl.BlockSpec((tr, C), lambda i: (i, 0)), out_shape=jax.ShapeDtypeStruct((R, C), F32),
                  compiler_params=_params())(parts)


MESH_ID = pl.DeviceIdType.MESH
ANY = pl.BlockSpec(memory_space=pl.ANY)


def _all_gather(x, *, name, after=None):
    R, C = x.shape
    extra = [] if after is None else [after]

    def body(x_ref, *rest):
        out_ref, send_sems, recv_sems, local_sem = rest[-4:]
        mx, my, mc = lax.axis_index("x"), lax.axis_index("y"), lax.axis_index("c")
        me, sibling = (mx, my, mc), (mx, my, 1 - mc)
        chips = [(1 - mx, my), (mx, 1 - my), (1 - mx, 1 - my)]

        def blk(px, py, pc):
            return out_ref.at[4 * px + 2 * py + pc]

        def copy(k, block, to, src=None):
            return pltpu.make_async_remote_copy(
                src_ref=blk(*block) if src is None else src, dst_ref=blk(*block),
                send_sem=send_sems.at[k], recv_sem=recv_sems.at[k], device_id=to, device_id_type=MESH_ID)

        mine = pltpu.make_async_copy(x_ref, blk(*me), local_sem)
        mine.start()
        first = [copy(0, me, sibling, src=x_ref)]
        first += [copy(1 + j, me, (*chip, mc), src=x_ref) for j, chip in enumerate(chips)]
        for cp in first:
            cp.start()
        passed = [copy(4 + j, (*chip, mc), sibling) for j, chip in enumerate(chips)]
        for j, chip in enumerate(chips):
            copy(1 + j, (*chip, mc), me).wait_recv()
            passed[j].start()
        copy(0, sibling, me).wait_recv()
        for j, chip in enumerate(chips):
            copy(4 + j, (*chip, 1 - mc), me).wait_recv()
        for cp in first + passed:
            cp.wait_send()
        mine.wait()
    return _pcall(body, name=name, in_specs=[ANY] * (1 + len(extra)), out_specs=ANY,
                  out_shape=jax.ShapeDtypeStruct((N_DEV, R, C), x.dtype),
                  scratch_shapes=[pltpu.SemaphoreType.DMA((7,)), pltpu.SemaphoreType.DMA((7,)), pltpu.SemaphoreType.DMA],
                  compiler_params=pltpu.CompilerParams(has_side_effects=True))(x, *extra)


HBM_SPEC = pl.BlockSpec(memory_space=pltpu.HBM)
SEM_SPEC = pl.BlockSpec(memory_space=pltpu.SEMAPHORE)
DATAFLOW = pltpu.SideEffectType.DATAFLOW_SIDE_EFFECTING


def _coords():
    return lax.axis_index("x"), lax.axis_index("y"), lax.axis_index("c")


def _other_chips(mx, my):
    return [(1 - mx, my), (mx, 1 - my), (1 - mx, 1 - my)]


def _plan_gather_ici(refs, send, recv):
    src, land = refs
    mx, my, mc = _coords()
    return [pltpu.make_async_remote_copy(src_ref=src, dst_ref=land.at[mc, 2 * mx + my], send_sem=send[j], recv_sem=recv[j],
                                         device_id=(px, py, mc), device_id_type=MESH_ID)
            for j, (px, py) in enumerate(_other_chips(mx, my))]


def _plan_gather_d2d(refs, send, recv):
    (land,) = refs
    mx, my, mc = _coords()
    return [pltpu.make_async_remote_copy(src_ref=land.at[mc], dst_ref=land.at[mc], send_sem=send[0], recv_sem=recv[0],
                                         device_id=(mx, my, 1 - mc), device_id_type=MESH_ID)]


def _plan_reduce_d2d(refs, send, recv):
    g, land = refs
    mx, my, mc = _coords()
    return [pltpu.make_async_remote_copy(src_ref=g.at[1 - mc], dst_ref=land, send_sem=send[0], recv_sem=recv[0],
                                         device_id=(mx, my, 1 - mc), device_id_type=MESH_ID)]


def _plan_reduce_ici(refs, send, recv):
    h, land = refs
    mx, my, mc = _coords()
    return [pltpu.make_async_remote_copy(src_ref=h.at[2 * px + py], dst_ref=land.at[j], send_sem=send[j], recv_sem=recv[j],
                                         device_id=(px, py, mc), device_id_type=MESH_ID)
            for j, (px, py) in enumerate(_other_chips(mx, my))]


def _rdma_start(bufs, n, plan, *, name, after=None):
    nb = len(bufs)
    extra = [] if after is None else [after]
    ne = len(extra)

    def body(*refs):
        ins, send, recv = refs[:nb], refs[nb + ne:nb + ne + n], refs[nb + ne + n:nb + ne + 2 * n]
        token = refs[-1]
        for cp in plan(ins, send, recv):
            cp.start()
        token[...] = jnp.zeros_like(token)
    out = _pcall(body, name=name,
                 out_shape=tuple([pltpu.SemaphoreType.DMA(())] * (2 * n) + [pltpu.HBM(b.shape, b.dtype) for b in bufs]
                                 + [jax.ShapeDtypeStruct((8, 128), F32)]),
                 in_specs=tuple([HBM_SPEC] * nb + [ANY] * ne),
                 out_specs=tuple([SEM_SPEC] * (2 * n) + [HBM_SPEC] * nb + [pl.BlockSpec(memory_space=pltpu.VMEM)]),
                 input_output_aliases={i: 2 * n + i for i in range(nb)},
                 compiler_params=pltpu.CompilerParams(has_side_effects=DATAFLOW))(
                     *[pltpu.with_memory_space_constraint(b, pltpu.HBM) for b in bufs], *extra)
    return list(out[:2 * n]), list(out[2 * n:2 * n + nb]), out[-1]


def _rdma_wait(sems, bufs, n, plan, after, *, name):
    nb = len(bufs)

    def body(*refs):
        ins, send, recv = refs[:nb], refs[nb:nb + n], refs[nb + n:nb + 2 * n]
        for cp in plan(ins, send, recv):
            cp.wait_send()
            cp.wait_recv()
    out = _pcall(body, name=name, out_shape=tuple(pltpu.HBM(b.shape, b.dtype) for b in bufs),
                 in_specs=tuple([HBM_SPEC] * nb + [SEM_SPEC] * (2 * n) + [ANY]), out_specs=tuple([HBM_SPEC] * nb),
                 input_output_aliases={i: i for i in range(nb)},
                 compiler_params=pltpu.CompilerParams(has_side_effects=DATAFLOW))(*bufs, *sems, after)
    return list(out)


def _sum_pair(g, land, cidx, *, name):
    _, nchip, R, C = g.shape
    tr = _tile(R, 1056, 16)

    def body(c_ref, g_ref, l_ref, o_ref):
        o_ref[...] = g_ref[...] + l_ref[...]
    grid_spec = pltpu.PrefetchScalarGridSpec(
        num_scalar_prefetch=1, grid=(nchip, R // tr),
        in_specs=[pl.BlockSpec((None, None, tr, C), lambda p, i, c_ref: (c_ref[0], p, i, 0)),
                  pl.BlockSpec((None, tr, C), lambda p, i, c_ref: (p, i, 0))],
        out_specs=pl.BlockSpec((None, tr, C), lambda p, i, c_ref: (p, i, 0)))
    return _pcall(body, name=name, grid_spec=grid_spec, out_shape=jax.ShapeDtypeStruct((nchip, R, C), BF),
                  compiler_params=_params())(cidx, g, land)


def _sum_chips(h, land, chipidx, *, name):
    _, R, C = h.shape
    tr = _tile(R, 1056, 16)

    def body(c_ref, h_ref, l_ref, o_ref):
        acc = h_ref[...].astype(F32)
        for j in range(3):
            acc = acc + l_ref[j].astype(F32)
        o_ref[...] = acc
    grid_spec = pltpu.PrefetchScalarGridSpec(
        num_scalar_prefetch=1, grid=(R // tr,),
        in_specs=[pl.BlockSpec((None, tr, C), lambda i, c_ref: (c_ref[0], i, 0)),
                  pl.BlockSpec((3, tr, C), lambda i, c_ref: (0, i, 0))],
        out_specs=pl.BlockSpec((tr, C), lambda i, c_ref: (i, 0)))
    return _pcall(body, name=name, grid_spec=grid_spec, out_shape=jax.ShapeDtypeStruct((R, C), F32),
                  compiler_params=_params())(chipidx, h, land)


PART_IN = ("w_in",)
PART_MIX = ("proj_a", "proj_b", "w_out")
PART_FFN = ("ffn_w_gate", "ffn_w_up", "ffn_w_down")


def _part_rows(names):
    return sum(BIG_ROWS[n] for n in names)


def _part_offsets(names):
    off, r = {}, 0
    for n in names:
        off[n] = r
        r += BIG_ROWS[n]
    return off


def _pack_shards(shards, l, names):
    return jnp.concatenate([(shards[n][l].T if n in COL_SHARDED else shards[n][l]).astype(BF) for n in names], axis=0)


def _unpack_weights(full8, names):
    off = _part_offsets(names)

    def whole(n):
        return full8[:, off[n]:off[n] + BIG_ROWS[n], :].reshape(N_DEV * BIG_ROWS[n], 1024)
    out = {}
    if "w_in" in names:
        wt_in = whole("w_in")
        out["wt_in"] = jnp.concatenate([wt_in[V_END:], wt_in[:V_END]], axis=0)
    for n in ("proj_a", "proj_b", "w_out"):
        if n in names:
            out[n] = whole(n)
    if "ffn_w_gate" in names:
        out["wt_gate"], out["wt_up"], out["w_down"] = whole("ffn_w_gate"), whole("ffn_w_up"), whole("ffn_w_down")
    return out


def _from_land(land):
    return land.transpose(1, 0, 2, 3).reshape(N_DEV, land.shape[2], 1024)


def _pack_grads(wg, names):
    full = {"proj_a": wg.get("proj_a"), "proj_b": wg.get("proj_b"), "w_out": wg.get("w_out"), "ffn_w_down": wg.get("w_down"),
            "ffn_w_gate": wg.get("wt_gate"), "ffn_w_up": wg.get("wt_up")}
    if "w_in" in names:
        full["w_in"] = jnp.concatenate([wg["wt_in"][P_Q:], wg["wt_in"][:P_Q]], axis=0)
    blocks = jnp.concatenate([full[n].reshape(N_DEV, BIG_ROWS[n], 1024) for n in names], axis=1)
    return blocks.reshape(4, 2, _part_rows(names), 1024).transpose(1, 0, 2, 3)


def _unpack_shard_grads(gs, names):
    off = _part_offsets(names)
    out = {}
    for n in names:
        blk = gs[off[n]:off[n] + BIG_ROWS[n]]
        out[n] = blk.T if n in COL_SHARDED else blk
    return out


def _rope_setup(positions):
    S = positions.shape[0]
    inv = ROPE_THETA ** (-jnp.arange(0, ROT_DIM, 2, dtype=F32) / ROT_DIM)
    lane = np.arange(128) % HEAD_DIM
    half = ROT_DIM // 2
    inv_row = jnp.where(lane < ROT_DIM, jnp.tile(inv, 128 // half), 0.0)[None, :].astype(F32)
    m1_row = jnp.asarray((lane < half).astype(np.float32))[None, :]
    m2_row = jnp.asarray(((lane >= half) & (lane < ROT_DIM)).astype(np.float32))[None, :]
    return (*_rope_tables(positions.astype(F32).reshape(S, 1), inv_row, m1_row, m2_row), _attn_bias())


def _hook(hooks, point, after):
    f = None if hooks is None else hooks.get(point)
    return None if f is None else f(after)


def _layer_fwd(l, x, mod_l, W, small, rope, hooks=None):
    rc, rs1, rs2, bias = rope
    sh1, sc1, g1, sh2, sc2, g2 = [mod_l[i * D_MODEL:(i + 1) * D_MODEL][None, :] for i in range(6)]
    nw1, nw2 = small["norm1_w"][l][None, :], small["norm2_w"][l][None, :]
    tok = _hook(hooks, "mm_in", x)
    h, (proj,) = _norm_mm(x, nw1, sc1, sh1, [W["wt_in"]], name=f"mm_in{l}", after=tok, tm=2048, tn_cap=768)
    q_r, k_r, v_b = _rope_fwd(proj, rc, rs1, rs2, name=f"rope_fwd{l}")
    sink_rows = jnp.repeat(small["attn_sinks"][l].reshape(N_KV_HEADS, Q_PER_KV), ATTN_BLOCK, axis=1)
    sink_rows = jnp.broadcast_to(sink_rows[..., None], sink_rows.shape + (128,))
    y_attn = _attn_fwd(q_r, k_r, v_b, sink_rows, bias, name=f"attn_fwd{l}")
    lnw, lnb = small["sgu_ln_w"][l][None, :], small["sgu_ln_b"][l][None, :]
    sgu_bt = small["sgu_b"][l].T
    y_sgu = _sgu_fwd(proj, lnw, lnb, small["sgu_w"][l], sgu_bt, name=f"sgu_fwd{l}", after=_hook(hooks, "sgu", y_attn))
    tok = _hook(hooks, "mm_pa", y_sgu)
    a_br, b_br, merged = _merge_fwd(y_sgu, y_attn, W["proj_a"], W["proj_b"], proj, name=f"merge_fwd{l}", after=tok)
    x1, o1 = _mm(merged, W["w_out"], nt=False, out_dtype=F32, name=f"mm_out{l}", res=x, gvec=g1)
    tok = _hook(hooks, "mm_gu", x1)
    h2, (a_g, a_u) = _norm_mm(x1, nw2, sc2, sh2, [W["wt_gate"], W["wt_up"]], name=f"mm_gu{l}", after=tok, tn_cap=1408)
    cw, cb = small["ffn_conv_w"][l], small["ffn_conv_b"][l][None, :]
    hf, a_c = _ffn_act_fwd(a_g, a_u, cw, cb, name=f"ffn_act_fwd{l}")
    x2, o2 = _mm(hf, W["w_down"], nt=False, out_dtype=F32, name=f"mm_down{l}", res=x1, gvec=g2)
    saved = dict(x=x, h=h, proj=proj, q_r=q_r, k_r=k_r, v_b=v_b, sink_rows=sink_rows, y_attn=y_attn, y_sgu=y_sgu,
                 a_br=a_br, b_br=b_br, merged=merged, x1=x1, o1=o1, h2=h2, a_g=a_g, a_u=a_u, a_c=a_c, hf=hf, o2=o2)
    return x2, saved


def _layer_bwd(l, dx, do2, dg2, mod_l, W, small, rope, sv, below=None, hooks=None, wg=None):
    rc, rs1, rs2, bias = rope
    sh1, sc1, g1, sh2, sc2, g2 = [mod_l[i * D_MODEL:(i + 1) * D_MODEL][None, :] for i in range(6)]
    nw1, nw2 = small["norm1_w"][l][None, :], small["norm2_w"][l][None, :]
    cw = small["ffn_conv_w"][l]
    lnw, lnb = small["sgu_ln_w"][l][None, :], small["sgu_ln_b"][l][None, :]
    sgu_bt = small["sgu_b"][l].T
    wg = {} if wg is None else wg
    dhf = _mm(do2, W["w_down"], nt=True, out_dtype=BF, name=f"mm_down_dx{l}", after=_hook(hooks, "mm_down_dx", do2),
              tn_cap=1408)
    wg["w_down"] = _mm_tn(sv["hf"], do2, name=f"mm_down_dw{l}")
    dac, dup, dcb = _ffn_act_bwd_a(dhf, sv["a_c"], sv["a_u"], name=f"ffn_act_bwd_a{l}")
    da, dcw = _ffn_act_bwd_b(dac, sv["a_g"], cw, name=f"ffn_act_bwd_b{l}")
    dh2 = _mm([da, dup], [W["wt_gate"], W["wt_up"]], nt=False, out_dtype=F32, name=f"mm_gu_dx{l}",
              after=_hook(hooks, "mm_gu_dx", da))
    wg["wt_gate"] = _mm_tn(da, sv["h2"], name=f"mm_gate_dw{l}")
    wg["wt_up"] = _mm_tn(dup, sv["h2"], name=f"mm_up_dw{l}")
    dx1, dnw2, dsc2, dsh2, do1, dg1 = _normmod_bwd(dh2, sv["x1"], nw2, sc2, sh2, dx, (sv["o1"], g1), name=f"normmod2_bwd{l}")
    d_a, d_b, dproj = _merge_bwd(do1, W["w_out"], sv["a_br"], sv["b_br"], sv["proj"], name=f"merge_bwd{l}",
                                 after=_hook(hooks, "merge_bwd", do1))
    wg["w_out"] = _mm_tn(sv["merged"], do1, name=f"mm_out_dw{l}")
    dysgu = _mm(d_a, W["proj_a"], nt=True, out_dtype=F32, name=f"mm_pa_dx{l}", after=_hook(hooks, "mm_pa_dx", d_a))
    dyattn = _mm(d_b, W["proj_b"], nt=True, out_dtype=BF, name=f"mm_pb_dx{l}")
    wg["proj_a"] = _mm_tn(sv["y_sgu"], d_a, name=f"mm_pa_dw{l}")
    wg["proj_b"] = _mm_tn(sv["y_attn"], d_b, name=f"mm_pb_dw{l}")
    dproj, dlnw, dlnb, dsguw, dsgubt = _sgu_bwd(dysgu, sv["proj"], lnw, lnb, small["sgu_w"][l], sgu_bt, dproj,
                                                name=f"sgu_bwd{l}")
    dq_r, dk_r, dv_b, dsk = _attn_bwd(dyattn, sv["q_r"], sv["k_r"], sv["v_b"], sv["sink_rows"], bias, name=f"attn_bwd{l}")
    dproj = _rope_bwd(dq_r, dk_r, dv_b, rc, rs1, rs2, dproj, name=f"rope_bwd{l}")
    wg["wt_in"] = _mm_tn(dproj, sv["h"], name=f"mm_in_dw{l}")
    dh = _mm(dproj, W["wt_in"], nt=False, out_dtype=F32, name=f"mm_in_dx{l}", after=_hook(hooks, "mm_in_dx", wg["wt_in"]))
    dx0, dnw1, dsc1, dsh1, *gate_below = _normmod_bwd(dh, sv["x"], nw1, sc1, sh1, dx1, below, name=f"normmod1_bwd{l}")
    dmod = jnp.concatenate([dsh1, dsc1, dg1, dsh2, dsc2, dg2], axis=1)[0]
    sg = {"norm1_w": dnw1[0], "norm2_w": dnw2[0], "attn_sinks": dsk[:, :, 0].reshape(N_Q_HEADS),
          "sgu_ln_w": dlnw[0], "sgu_ln_b": dlnb[0], "sgu_w": dsguw, "sgu_b": dsgubt.T,
          "ffn_conv_w": dcw, "ffn_conv_b": dcb[0]}
    return (dx0, *gate_below), wg, sg, dmod


SMALL = ("ada_b", "norm1_w", "attn_sinks", "sgu_ln_w", "sgu_ln_b", "sgu_w", "sgu_b", "norm2_w", "ffn_conv_b", "final_norm_w")
WEIGHT_ORDER = ("ada_w", "ada_b", "norm1_w", "w_in", "attn_sinks", "sgu_ln_w", "sgu_ln_b", "sgu_w", "sgu_b", "proj_a", "proj_b",
                "w_out", "norm2_w", "ffn_w_gate", "ffn_w_up", "ffn_conv_w", "ffn_conv_b", "ffn_w_down", "final_norm_w")


def _flat_pack(arrs, rows):
    flat = jnp.concatenate([a.reshape(-1) for a in arrs])
    return jnp.pad(flat, (0, rows * 1024 - flat.shape[0])).reshape(rows, 1024)


def _flat_unpack(buf, shapes):
    flat = buf.reshape(-1)
    out, o = [], 0
    for s in shapes:
        n = int(np.prod(s))
        out.append(flat[o:o + n].reshape(s))
        o += n
    return out


def _adam2d(w, g, m, v, *, name):
    shp = w.shape
    r2 = (int(np.prod(shp[:-1])), shp[-1]) if len(shp) > 1 else (1, shp[0])
    d, mn, vn = _adamw(w.reshape(r2), g.reshape(r2), m.reshape(r2), v.reshape(r2), name=name)
    return d.reshape(shp), mn.reshape(shp), vn.reshape(shp)


def kernel(x, c, positions, ada_w, ada_b, norm1_w, w_in, attn_sinks, sgu_ln_w, sgu_ln_b, sgu_w, sgu_b, proj_a, proj_b, w_out, norm2_w, ffn_w_gate, ffn_w_up, ffn_conv_w, ffn_conv_b, ffn_w_down, final_norm_w, loss_target, m_ada_w, m_ada_b, m_norm1_w, m_w_in, m_attn_sinks, m_sgu_ln_w, m_sgu_ln_b, m_sgu_w, m_sgu_b, m_proj_a, m_proj_b, m_w_out, m_norm2_w, m_ffn_w_gate, m_ffn_w_up, m_ffn_conv_w, m_ffn_conv_b, m_ffn_w_down, m_final_norm_w, v_ada_w, v_ada_b, v_norm1_w, v_w_in, v_attn_sinks, v_sgu_ln_w, v_sgu_ln_b, v_sgu_w, v_sgu_b, v_proj_a, v_proj_b, v_w_out, v_norm2_w, v_ffn_w_gate, v_ffn_w_up, v_ffn_conv_w, v_ffn_conv_b, v_ffn_w_down, v_final_norm_w):
    wts = dict(ada_w=ada_w, ada_b=ada_b, norm1_w=norm1_w, w_in=w_in, attn_sinks=attn_sinks, sgu_ln_w=sgu_ln_w,
               sgu_ln_b=sgu_ln_b, sgu_w=sgu_w, sgu_b=sgu_b, proj_a=proj_a, proj_b=proj_b, w_out=w_out, norm2_w=norm2_w,
               ffn_w_gate=ffn_w_gate, ffn_w_up=ffn_w_up, ffn_conv_w=ffn_conv_w, ffn_conv_b=ffn_conv_b,
               ffn_w_down=ffn_w_down, final_norm_w=final_norm_w)
    mom = dict(ada_w=m_ada_w, ada_b=m_ada_b, norm1_w=m_norm1_w, w_in=m_w_in, attn_sinks=m_attn_sinks, sgu_ln_w=m_sgu_ln_w,
               sgu_ln_b=m_sgu_ln_b, sgu_w=m_sgu_w, sgu_b=m_sgu_b, proj_a=m_proj_a, proj_b=m_proj_b, w_out=m_w_out,
               norm2_w=m_norm2_w, ffn_w_gate=m_ffn_w_gate, ffn_w_up=m_ffn_w_up, ffn_conv_w=m_ffn_conv_w,
               ffn_conv_b=m_ffn_conv_b, ffn_w_down=m_ffn_w_down, final_norm_w=m_final_norm_w)
    var = dict(ada_w=v_ada_w, ada_b=v_ada_b, norm1_w=v_norm1_w, w_in=v_w_in, attn_sinks=v_attn_sinks, sgu_ln_w=v_sgu_ln_w,
               sgu_ln_b=v_sgu_ln_b, sgu_w=v_sgu_w, sgu_b=v_sgu_b, proj_a=v_proj_a, proj_b=v_proj_b, w_out=v_w_out,
               norm2_w=v_norm2_w, ffn_w_gate=v_ffn_w_gate, ffn_w_up=v_ffn_w_up, ffn_conv_w=v_ffn_conv_w,
               ffn_conv_b=v_ffn_conv_b, ffn_w_down=v_ffn_w_down, final_norm_w=v_final_norm_w)
    me = 4 * lax.axis_index("x") + 2 * lax.axis_index("y") + lax.axis_index("c")
    ada_cols = ada_w.shape[2]

    c_all = _all_gather(jnp.broadcast_to(c, (8, D_MODEL)), name="ag_c")[:, 0, :]
    prod = _ada_fwd(c_all, ada_w)
    prod_all = _all_gather(prod, name="ag_mod")
    mine = lax.dynamic_index_in_dim(prod_all, me, axis=1, keepdims=False)
    mod = jnp.stack([mine[:, l * ada_cols:(l + 1) * ada_cols].reshape(-1) for l in range(DEPTH)]) + ada_b

    conv_cols = ffn_conv_w.shape[2]
    conv_all = _all_gather(_flat_pack([ffn_conv_w], 8), name="ag_conv", after=mod)
    conv_full = jnp.stack([a.reshape(DEPTH, 3, conv_cols) for a in
                           [conv_all[j].reshape(-1)[:DEPTH * 3 * conv_cols] for j in range(N_DEV)]], axis=2)
    conv_full = conv_full.reshape(DEPTH, 3, FFN_DIM)
    small = {n: wts[n] for n in SMALL}
    small["ffn_conv_w"] = conv_full

    mx, my, mc = _coords()
    cidx = jnp.reshape(mc, (1,)).astype(jnp.int32)
    chipidx = jnp.reshape(2 * mx + my, (1,)).astype(jnp.int32)
    rope = _rope_setup(positions[0])

    class Gather:
        def __init__(self, src, tag):
            self.tag, self.src = tag, src
            self.land = lax.dynamic_update_slice(lax.empty((2, 4) + src.shape, src.dtype), src[None, None],
                                                 (mc, 2 * mx + my, 0, 0))

        def ici_start(self, after):
            self.sems, (self.src, self.land), tok = _rdma_start([self.src, self.land], 3, _plan_gather_ici,
                                                                name=f"ag_{self.tag}_ici_start", after=after)
            return tok

        def ici_wait_d2d_start(self, after):
            _, land = _rdma_wait(self.sems, [self.src, self.land], 3, _plan_gather_ici, after, name=f"ag_{self.tag}_ici_wait")
            self.sems, (self.land,), tok = _rdma_start([land], 1, _plan_gather_d2d, name=f"ag_{self.tag}_d2d_start")
            return tok

        def d2d_wait(self, after):
            (land,) = _rdma_wait(self.sems, [self.land], 1, _plan_gather_d2d, after, name=f"ag_{self.tag}_d2d_wait")
            return _from_land(land)

    def weights_job(names, l, tag):
        job = Gather(_pack_shards(wts, l, names), tag)
        job.weights = lambda after: _unpack_weights(job.d2d_wait(after), names)
        return job

    W0 = _unpack_weights(_all_gather(_pack_shards(wts, 0, PART_IN), name="ag_w0_in", after=conv_all), PART_IN)
    W1 = {}
    rest = PART_MIX + PART_FFN
    g_rest0 = weights_job(rest, 0, "w0_rest")
    g_in1, g_rest1 = weights_job(PART_IN, 1, "w1_in"), weights_job(rest, 1, "w1_rest")

    def rest0_then_layer1(after):
        W0.update(g_rest0.weights(after))
        return g_rest1.ici_start(g_in1.ici_start(W0["w_down"]))

    x1, sv0 = _layer_fwd(0, x[0], mod[0], W0, small, rope,
                         {"mm_in": lambda after: g_rest0.ici_start(W0["wt_in"]), "sgu": g_rest0.ici_wait_d2d_start,
                          "mm_pa": rest0_then_layer1, "mm_gu": g_in1.ici_wait_d2d_start})
    g_rest1.ici_wait_d2d_start(x1)
    x2, sv1 = _layer_fwd(1, x1, mod[1], W1, small, rope,
                         {"mm_in": lambda after: W1.update(g_in1.weights(after)),
                          "mm_pa": lambda after: W1.update(g_rest1.weights(after))})
    gate2 = [mod[l][5 * D_MODEL:][None, :] for l in range(DEPTH)]
    dx2, dfw, loss_tile, do2, dg2 = _head(x2, final_norm_w[None, :], loss_target[0], (sv1["o2"], gate2[1]))
    loss = lax.psum(loss_tile[0, 0], ("x", "y", "c"))

    class Reduce:
        def __init__(self, names, tag):
            self.names, self.tag, self.rows = names, tag, _part_rows(names)

        def d2d_start(self, wg, after=None):
            self.sems, self.bufs, tok = _rdma_start([_pack_grads(wg, self.names), lax.empty((4, self.rows, 1024), BF)], 1,
                                                    _plan_reduce_d2d, name=f"rs_{self.tag}_d2d_start", after=after)
            return tok

        def d2d_wait_ici_start(self, after):
            g_t, land_a = _rdma_wait(self.sems, self.bufs, 1, _plan_reduce_d2d, after, name=f"rs_{self.tag}_d2d_wait")
            h = _sum_pair(g_t, land_a, cidx, name=f"rs_{self.tag}_sum_pair")
            self.sems, self.bufs, tok = _rdma_start([h, lax.empty((3, self.rows, 1024), BF)], 3, _plan_reduce_ici,
                                                    name=f"rs_{self.tag}_ici_start")
            return tok

        def ici_wait(self, after):
            h_t, land_b = _rdma_wait(self.sems, self.bufs, 3, _plan_reduce_ici, after, name=f"rs_{self.tag}_ici_wait")
            return _unpack_shard_grads(_sum_chips(h_t, land_b, chipidx, name=f"rs_{self.tag}_sum_chips"), self.names)

    (dx1, do2, dg2), wg1, sg1, dmod1 = _layer_bwd(1, dx2, do2, dg2, mod[1], W1, small, rope, sv1, below=(sv0["o2"], gate2[0]))
    r_all1, r_ffn0, r_mix0 = Reduce(BIG, "g1"), Reduce(PART_FFN, "g0_ffn"), Reduce(PART_IN + PART_MIX, "g0_mix")
    tok1 = r_all1.d2d_start(wg1)
    wg0, shard1 = {}, {}

    def layer1_done_then_mix0(after):
        shard1.update(r_all1.ici_wait(after))
        return r_mix0.d2d_wait_ici_start(r_mix0.d2d_start(wg0, shard1["w_in"]))

    (grad_x,), _, sg0, dmod0 = _layer_bwd(
        0, dx1, do2, dg2, mod[0], W0, small, rope, sv0, wg=wg0,
        hooks={"mm_down_dx": lambda after: tok1, "mm_gu_dx": r_all1.d2d_wait_ici_start,
               "merge_bwd": lambda after: r_ffn0.d2d_start(wg0, after), "mm_pa_dx": r_ffn0.d2d_wait_ici_start,
               "mm_in_dx": layer1_done_then_mix0})
    sg = {n: jnp.stack([sg0[n], sg1[n]]) for n in sg0}
    sg["final_norm_w"] = dfw[0]
    dmod = jnp.stack([dmod0, dmod1])
    vec_names = [n for n in SMALL if n not in ("ada_b", "sgu_w")] + ["ffn_conv_w"]
    vec_shapes = [(DEPTH, 6 * D_MODEL)] + [sg[n].shape for n in vec_names]
    vec_rows = -(-sum(int(np.prod(s)) for s in vec_shapes) // 1024 // 16) * 16
    sgu_rows = sgu_w.size // 1024
    g_small = Gather(jnp.concatenate([_flat_pack([dmod] + [sg[n] for n in vec_names], vec_rows),
                                      sg["sgu_w"].reshape(sgu_rows, 1024)], axis=0).astype(BF), "small")
    tok = g_small.ici_start(grad_x)

    shard0 = r_ffn0.ici_wait(tok)
    shard0.update(r_mix0.ici_wait(shard0["ffn_w_down"]))
    grads, delta, new_m, new_v = {}, {}, {}, {}
    for n in BIG:
        two = lambda a: a.reshape(DEPTH, -1, a.shape[-1])
        out = _adamw_layers(two(wts[n]), [shard0[n], shard1[n]], two(mom[n]), two(var[n]), name=f"adamw_{n}")
        grads[n], delta[n], new_m[n], new_v[n] = [o.reshape(wts[n].shape) for o in out]

    sm_all = g_small.d2d_wait(g_small.ici_wait_d2d_start(delta["ffn_w_gate"]))
    sm_sum = _sum8(sm_all, name="sum_small")
    vec_sum = _flat_unpack(sm_sum[:vec_rows], vec_shapes)
    grads["ada_b"] = vec_sum[0]
    for n, gsum in zip(vec_names, vec_sum[1:]):
        grads[n] = gsum
    grads["sgu_w"] = sm_sum[vec_rows:].reshape(sgu_w.shape)
    grads["ffn_conv_w"] = lax.dynamic_slice_in_dim(grads["ffn_conv_w"], me * conv_cols, conv_cols, axis=2)
    dmod_all = sm_all[:, :DEPTH * 6, :].astype(F32).reshape(N_DEV, DEPTH, 6 * D_MODEL)
    dm_mine = lax.dynamic_slice_in_dim(dmod_all, me * ada_cols, ada_cols, axis=2).transpose(1, 0, 2)
    dm_mine = jnp.pad(dm_mine, ((0, 0), (0, 8), (0, 0)))
    grads["ada_w"] = _ada_bwd(jnp.pad(c_all, ((0, 8), (0, 0))), dm_mine)

    packed_small = [n for n in SMALL if n != "sgu_w"]
    pshapes = [wts[n].shape for n in packed_small]
    prow = -(-sum(int(np.prod(s)) for s in pshapes) // 1024 // 8) * 8
    pk = lambda d: _flat_pack([d[n] for n in packed_small], prow)
    d_s, m_s, v_s = _adamw(pk(wts), pk(grads), pk(mom), pk(var), name="adamw_small")
    for n, dd, mm, vv in zip(packed_small, _flat_unpack(d_s, pshapes), _flat_unpack(m_s, pshapes), _flat_unpack(v_s, pshapes)):
        delta[n], new_m[n], new_v[n] = dd, mm, vv
    for n in WEIGHT_ORDER:
        if n not in delta:
            delta[n], new_m[n], new_v[n] = _adam2d(wts[n], grads[n], mom[n], var[n], name=f"adamw_{n}")
    return (loss, grad_x[None], *[grads[n] for n in WEIGHT_ORDER], *[delta[n] for n in WEIGHT_ORDER],
            *[new_m[n] for n in WEIGHT_ORDER], *[new_v[n] for n in WEIGHT_ORDER])
```

```python
import functools

import jax
import jax.numpy as jnp
import numpy as np
from jax import lax
from jax.experimental import pallas as pl
from jax.experimental.pallas import tpu as pltpu

F32 = jnp.float32
BF = jnp.bfloat16

N_DEV = 8
D_MODEL = 1024
DEPTH = 2
N_Q_HEADS = 16
N_KV_HEADS = 2
HEAD_DIM = 64
Q_PER_KV = N_Q_HEADS // N_KV_HEADS
ATTN_BLOCK = 128
ROPE_THETA = 500000.0
ROT_DIM = HEAD_DIM // 4
SGU_WIDTH = 1024
SGU_GROUPS = 8
SGU_CHUNK = 128
FFN_DIM = 2816
NORM_EPS = 1e-6
Q_END = N_Q_HEADS * HEAD_DIM
K_END = Q_END + N_KV_HEADS * HEAD_DIM
V_END = K_END + N_KV_HEADS * HEAD_DIM
Z_END = V_END + 2 * SGU_WIDTH
IN_COLS = Z_END + 2 * D_MODEL
P_Z, P_G, P_Q, P_K, P_V = 0, 2048, 4096, 5120, 5248

ADAM_LR = 0.001
ADAM_B1 = 0.9
ADAM_B2 = 0.999
ADAM_EPS = 1e-08
ADAM_WD = 0.01
ADAM_STEP = 10

VMEM_LIMIT_BYTES = 56 * 1024 * 1024

BIG = ("w_in", "proj_a", "proj_b", "w_out", "ffn_w_gate", "ffn_w_up", "ffn_w_down")
COL_SHARDED = ("w_in", "ffn_w_gate", "ffn_w_up")
BIG_SHAPE = {"w_in": (D_MODEL, IN_COLS), "proj_a": (SGU_WIDTH, D_MODEL), "proj_b": (Q_END, D_MODEL),
             "w_out": (D_MODEL, D_MODEL), "ffn_w_gate": (D_MODEL, FFN_DIM), "ffn_w_up": (D_MODEL, FFN_DIM),
             "ffn_w_down": (FFN_DIM, D_MODEL)}
BIG_ROWS = {n: BIG_SHAPE[n][0] * BIG_SHAPE[n][1] // N_DEV // 1024 for n in BIG}
LAYER_ROWS = sum(BIG_ROWS.values())


def _pcall(body, **kw):
    return pl.pallas_call(body, **kw)


def _params(**kw):
    return pltpu.CompilerParams(vmem_limit_bytes=VMEM_LIMIT_BYTES, **kw)


def _tile(n, cap, unit=128):
    if n <= cap:
        return n
    best = 0
    t = unit
    while t <= cap:
        if n % t == 0:
            best = t
        t += unit
    assert best, (n, cap, unit)
    return best


def _mm(a, b, *, nt, out_dtype, name, res=None, gvec=None, after=None, tm=None, tn_cap=1024):
    a_list = list(a) if isinstance(a, (list, tuple)) else [a]
    b_list = list(b) if isinstance(b, (list, tuple)) else [b]
    a, b = a_list[0], b_list[0]
    M, K = a.shape
    N = b.shape[0] if nt else b.shape[1]
    k_total = sum(x.shape[1] for x in a_list)
    tm = _tile(M, tm or (1024 if k_total <= 1024 else 512), 8)
    tn = _tile(N, tn_cap)
    dn = (((1,), (1,)), ((), ())) if nt else (((1,), (0,)), ((), ()))

    def b_spec_of(x):
        k = x.shape[1] if nt else x.shape[0]
        return pl.BlockSpec((tn, k), lambda i, j: (j, 0)) if nt else pl.BlockSpec((k, tn), lambda i, j: (0, j))
    b_spec = b_spec_of(b)
    o_spec = pl.BlockSpec((tm, tn), lambda i, j: (i, j))
    if res is None:
        extra = [] if after is None else [after]
        n = len(a_list)

        def body(*refs):
            o_ref = refs[-1]
            acc = None
            for a_ref, b_ref in zip(refs[:n], refs[n:2 * n]):
                d = lax.dot_general(a_ref[...].astype(BF), b_ref[...].astype(BF), dn, preferred_element_type=F32)
                acc = d if acc is None else acc + d
            o_ref[...] = acc.astype(out_dtype)
        return _pcall(body, name=name, grid=(M // tm, N // tn),
                      in_specs=[pl.BlockSpec((tm, x.shape[1]), lambda i, j: (i, 0)) for x in a_list]
                      + [b_spec_of(x) for x in b_list] + [ANY] * len(extra), out_specs=o_spec,
                      out_shape=jax.ShapeDtypeStruct((M, N), out_dtype), compiler_params=_params())(
                          *a_list, *b_list, *extra)

    def body_res(a_ref, b_ref, r_ref, g_ref, o_ref, acc_ref):
        acc = lax.dot_general(a_ref[...].astype(BF), b_ref[...].astype(BF), dn, preferred_element_type=F32)
        acc_ref[...] = acc.astype(BF)
        o_ref[...] = r_ref[...] + g_ref[...] * acc
    return _pcall(body_res, name=name, grid=(M // tm, N // tn),
                  in_specs=[pl.BlockSpec((tm, K), lambda i, j: (i, 0)), b_spec, o_spec,
                            pl.BlockSpec((1, tn), lambda i, j: (0, j))],
                  out_specs=[o_spec, o_spec],
                  out_shape=[jax.ShapeDtypeStruct((M, N), F32), jax.ShapeDtypeStruct((M, N), BF)],
                  compiler_params=_params())(a, b, res, gvec)


def _mm_tn(a, b, *, name, out_dtype=BF, tk=2048, tm_cap=1408, tn_cap=1024):
    S, M = a.shape
    N = b.shape[1]
    tk = _tile(S, tk, 8)
    tm = _tile(M, tm_cap)
    tn = _tile(N, tn_cap)
    nk = S // tk

    def body(a_ref, b_ref, o_ref, acc_ref):
        k = pl.program_id(2)

        @pl.when(k == 0)
        def _():
            acc_ref[...] = jnp.zeros_like(acc_ref)
        acc_ref[...] += lax.dot_general(a_ref[...].astype(BF), b_ref[...].astype(BF), (((0,), (0,)), ((), ())),
                                        preferred_element_type=F32)

        @pl.when(k == nk - 1)
        def _():
            o_ref[...] = acc_ref[...].astype(out_dtype)
    return _pcall(body, name=name, grid=(M // tm, N // tn, nk),
                  in_specs=[pl.BlockSpec((tk, tm), lambda i, j, k: (k, i)),
                            pl.BlockSpec((tk, tn), lambda i, j, k: (k, j))],
                  out_specs=pl.BlockSpec((tm, tn), lambda i, j, k: (i, j)),
                  out_shape=jax.ShapeDtypeStruct((M, N), out_dtype), scratch_shapes=[pltpu.VMEM((tm, tn), F32)],
                  compiler_params=_params())(a, b)


def _rms(x, w):
    return x * lax.rsqrt(jnp.mean(x * x, axis=-1, keepdims=True) + NORM_EPS) * w


def _normmod_fn(x, nw, sc, sh):
    return _rms(x, nw) * (1.0 + sc) + sh


def _gelu(x):
    return 0.5 * x * (1.0 + lax.erf(x * (2.0 ** -0.5)))


def _ln_gelu_fn(zv, w, b):
    v = _gelu(zv)
    mu = jnp.mean(v, axis=-1, keepdims=True)
    var = jnp.mean(jnp.square(v - mu), axis=-1, keepdims=True)
    return (v - mu) * lax.rsqrt(var + NORM_EPS) * w + b


def _sigmoid(x):
    return 1.0 / (1.0 + jnp.exp(-x))


def _row_spec(tm, n):
    return pl.BlockSpec((tm, n), lambda i: (i, 0))


def _vec_spec(n):
    return pl.BlockSpec((1, n), lambda i: (0, 0))


def _acc(ref, val):
    @pl.when(pl.program_id(0) == 0)
    def _():
        ref[...] = jnp.zeros_like(ref)
    ref[...] += val


def _norm_mm(x, nw, sc, sh, ws, *, name, after=None, tm=1024, tn_cap=768):
    S, K = x.shape
    N = ws[0].shape[0]
    tm = _tile(S, tm, 8)
    tn = _tile(N, tn_cap)
    nw_, ne = len(ws), 0 if after is None else 1

    def body(x_ref, nw_ref, sc_ref, sh_ref, *rest):
        w_refs = rest[:nw_]
        h_ref = rest[nw_ + ne]
        o_refs = rest[nw_ + ne + 1:nw_ + ne + 1 + nw_]
        h_s = rest[-1]

        @pl.when(pl.program_id(1) == 0)
        def _():
            hv = _normmod_fn(x_ref[...], nw_ref[...], sc_ref[...], sh_ref[...]).astype(BF)
            h_s[...] = hv
            h_ref[...] = hv
        for w_ref, o_ref in zip(w_refs, o_refs):
            o_ref[...] = lax.dot_general(h_s[...], w_ref[...], (((1,), (1,)), ((), ())),
                                         preferred_element_type=F32).astype(BF)
    row = pl.BlockSpec((tm, K), lambda i, j: (i, 0))
    vec = pl.BlockSpec((1, K), lambda i, j: (0, 0))
    out = pl.BlockSpec((tm, tn), lambda i, j: (i, j))
    res = _pcall(body, name=name, grid=(S // tm, N // tn),
                 in_specs=[row, vec, vec, vec] + [pl.BlockSpec((tn, K), lambda i, j: (j, 0))] * nw_ + [ANY] * ne,
                 out_specs=[row] + [out] * nw_,
                 out_shape=[jax.ShapeDtypeStruct((S, K), BF)] + [jax.ShapeDtypeStruct((S, N), BF)] * nw_,
                 scratch_shapes=[pltpu.VMEM((tm, K), BF)], compiler_params=_params())(
                     x, nw, sc, sh, *ws, *([] if after is None else [after]))
    return res[0], list(res[1:])


def _gate_bwd(dxv, o_ref, g_ref, do_ref, dg_ref):
    do_ref[...] = (dxv * g_ref[...]).astype(BF)
    _acc(dg_ref, jnp.sum(dxv * o_ref[...].astype(F32), axis=0, keepdims=True))


def _normmod_bwd(dh, x, nw, sc, sh, dres, gate, *, name, tm=512):
    S, Dm = x.shape
    tm = _tile(S, tm, 8)
    ng = 0 if gate is None else 2

    def body(dh_ref, x_ref, nw_ref, sc_ref, sh_ref, dres_ref, *rest):
        dx_ref, dnw_ref, dsc_ref, dsh_ref = rest[ng:ng + 4]
        xv, dy = x_ref[...], dh_ref[...]
        r = lax.rsqrt(jnp.mean(xv * xv, axis=-1, keepdims=True) + NORM_EPS)
        xn = xv * r
        t = dy * xn
        a = nw_ref[...] * (1.0 + sc_ref[...])
        dxv = dres_ref[...] + r * (dy * a - xn * jnp.mean(t * a, axis=-1, keepdims=True))
        dx_ref[...] = dxv
        ts = jnp.sum(t, axis=0, keepdims=True)
        _acc(dnw_ref, ts * (1.0 + sc_ref[...]))
        _acc(dsc_ref, ts * nw_ref[...])
        _acc(dsh_ref, jnp.sum(dy, axis=0, keepdims=True))
        if gate is not None:
            _gate_bwd(dxv, rest[0], rest[1], rest[ng + 4], rest[ng + 5])
    vec = jax.ShapeDtypeStruct((1, Dm), F32)
    gate_in = [] if gate is None else [_row_spec(tm, Dm), _vec_spec(Dm)]
    gate_out = [] if gate is None else [_row_spec(tm, Dm), _vec_spec(Dm)]
    gate_shape = [] if gate is None else [jax.ShapeDtypeStruct((S, Dm), BF), vec]
    return _pcall(body, name=name, grid=(S // tm,),
                  in_specs=[_row_spec(tm, Dm), _row_spec(tm, Dm), _vec_spec(Dm), _vec_spec(Dm), _vec_spec(Dm),
                            _row_spec(tm, Dm)] + gate_in,
                  out_specs=[_row_spec(tm, Dm), _vec_spec(Dm), _vec_spec(Dm), _vec_spec(Dm)] + gate_out,
                  out_shape=[jax.ShapeDtypeStruct((S, Dm), F32), vec, vec, vec] + gate_shape,
                  compiler_params=_params())(dh, x, nw, sc, sh, dres, *([] if gate is None else gate))


def _head(x, fw, target, gate, *, tm=512):
    S, Dm = x.shape
    tm = _tile(S, tm, 8)

    def body(x_ref, fw_ref, t_ref, o_ref, g_ref, dx_ref, dfw_ref, loss_ref, do_ref, dg_ref):
        xv, w = x_ref[...], fw_ref[...]
        r = lax.rsqrt(jnp.mean(xv * xv, axis=-1, keepdims=True) + NORM_EPS)
        xn = xv * r
        err = xn * w - t_ref[...]
        dy = err * (1.0 / Dm)
        t = dy * xn
        dx = r * (dy * w - xn * jnp.mean(t * w, axis=-1, keepdims=True))
        dx_ref[...] = dx
        _acc(dfw_ref, jnp.sum(t, axis=0, keepdims=True))
        part = 0.5 * jnp.sum(jnp.mean(err * err, axis=-1, keepdims=True), axis=0, keepdims=True)
        _acc(loss_ref, jnp.broadcast_to(part, (8, 128)))
        _gate_bwd(dx, o_ref, g_ref, do_ref, dg_ref)
    vec = jax.ShapeDtypeStruct((1, Dm), F32)
    return _pcall(body, name="head", grid=(S // tm,),
                  in_specs=[_row_spec(tm, Dm), _vec_spec(Dm), _row_spec(tm, Dm), _row_spec(tm, Dm), _vec_spec(Dm)],
                  out_specs=[_row_spec(tm, Dm), _vec_spec(Dm), pl.BlockSpec((8, 128), lambda i: (0, 0)),
                             _row_spec(tm, Dm), _vec_spec(Dm)],
                  out_shape=[jax.ShapeDtypeStruct((S, Dm), F32), vec, jax.ShapeDtypeStruct((8, 128), F32),
                             jax.ShapeDtypeStruct((S, Dm), BF), vec],
                  compiler_params=_params())(x, fw, target, *gate)


def _tril_mask():
    r = lax.broadcasted_iota(jnp.int32, (SGU_CHUNK, SGU_CHUNK), 0)
    c = lax.broadcasted_iota(jnp.int32, (SGU_CHUNK, SGU_CHUNK), 1)
    return c <= r


def _sgu_fwd(proj, lnw, lnb, w, b_t, *, name, after=None, tm=512):
    S = proj.shape[0]
    tm = _tile(S, tm, SGU_CHUNK)
    extra = [] if after is None else [after]

    def body(zu_ref, zv_ref, lnw_ref, lnb_ref, w_ref, bt_ref, *rest):
        o_ref = rest[-1]
        u = _gelu(zu_ref[...].astype(F32))
        vn = _ln_gelu_fn(zv_ref[...].astype(F32), lnw_ref[...], lnb_ref[...]).astype(BF)
        mask = _tril_mask()
        for g in range(SGU_GROUPS):
            wm = jnp.where(mask, w_ref[g], 0.0).astype(BF)
            cols = slice(g * 128, (g + 1) * 128)
            for ci in range(tm // SGU_CHUNK):
                rows = slice(ci * SGU_CHUNK, (ci + 1) * SGU_CHUNK)
                f = jnp.dot(wm, vn[rows, cols], preferred_element_type=F32) + bt_ref[:, g:g + 1]
                o_ref[rows, cols] = (u[rows, cols] * f).astype(BF)
    return _pcall(body, name=name, grid=(S // tm,),
                  in_specs=[pl.BlockSpec((tm, SGU_WIDTH), lambda i: (i, 0)), pl.BlockSpec((tm, SGU_WIDTH), lambda i: (i, 1)),
                            _vec_spec(SGU_WIDTH), _vec_spec(SGU_WIDTH),
                            pl.BlockSpec((SGU_GROUPS, 128, 128), lambda i: (0, 0, 0)),
                            pl.BlockSpec((128, SGU_GROUPS), lambda i: (0, 0))] + [ANY] * len(extra),
                  out_specs=_row_spec(tm, SGU_WIDTH), out_shape=jax.ShapeDtypeStruct((S, SGU_WIDTH), BF),
                  compiler_params=_params())(proj, proj, lnw, lnb, w, b_t, *extra)


def _sgu_bwd(dy, proj, lnw, lnb, w, b_t, dproj, *, name, tm=512):
    S = proj.shape[0]
    tm = _tile(S, tm, SGU_CHUNK)

    def body(dy_ref, zu_ref, zv_ref, lnw_ref, lnb_ref, w_ref, bt_ref, _, dz_ref, dlnw_ref, dlnb_ref, dw_ref, dbt_ref,
             f_s, dvn_s):
        first = pl.program_id(0) == 0

        @pl.when(first)
        def _():
            dw_ref[...] = jnp.zeros_like(dw_ref)
            dbt_ref[...] = jnp.zeros_like(dbt_ref)
        u, vjp_u = jax.vjp(_gelu, zu_ref[...].astype(F32))
        vn, vjp_v = jax.vjp(_ln_gelu_fn, zv_ref[...].astype(F32), lnw_ref[...], lnb_ref[...])
        vn = vn.astype(BF)
        dy_v = dy_ref[...]
        df = (dy_v * u).astype(BF)
        mask = _tril_mask()
        for g in range(SGU_GROUPS):
            wm = jnp.where(mask, w_ref[g], 0.0).astype(BF)
            cols = slice(g * 128, (g + 1) * 128)
            dwg = jnp.zeros((128, 128), F32)
            dbg = jnp.zeros((128, 1), F32)
            for ci in range(tm // SGU_CHUNK):
                rows = slice(ci * SGU_CHUNK, (ci + 1) * SGU_CHUNK)
                vn_c = vn[rows, cols]
                df_c = df[rows, cols]
                f_s[rows, cols] = jnp.dot(wm, vn_c, preferred_element_type=F32) + bt_ref[:, g:g + 1]
                dvn_s[rows, cols] = lax.dot_general(wm, df_c, (((0,), (0,)), ((), ())), preferred_element_type=F32)
                dwg = dwg + lax.dot_general(df_c, vn_c, (((1,), (1,)), ((), ())), preferred_element_type=F32)
                dbg = dbg + jnp.sum((dy_v[rows, cols] * u[rows, cols]), axis=1, keepdims=True)
            dw_ref[g] += jnp.where(mask, dwg, 0.0)
            dbt_ref[:, g:g + 1] += dbg
        (dzu,) = vjp_u(dy_v * f_s[...])
        dzv, dlnw, dlnb = vjp_v(dvn_s[...])
        dz_ref[:, :SGU_WIDTH] = dzu.astype(BF)
        dz_ref[:, SGU_WIDTH:] = dzv.astype(BF)
        _acc(dlnw_ref, dlnw)
        _acc(dlnb_ref, dlnb)
    vec = jax.ShapeDtypeStruct((1, SGU_WIDTH), F32)
    return _pcall(body, name=name, grid=(S // tm,),
                  in_specs=[_row_spec(tm, SGU_WIDTH),
                            pl.BlockSpec((tm, SGU_WIDTH), lambda i: (i, 0)), pl.BlockSpec((tm, SGU_WIDTH), lambda i: (i, 1)),
                            _vec_spec(SGU_WIDTH), _vec_spec(SGU_WIDTH),
                            pl.BlockSpec((SGU_GROUPS, 128, 128), lambda i: (0, 0, 0)),
                            pl.BlockSpec((128, SGU_GROUPS), lambda i: (0, 0)), ANY],
                  out_specs=[pl.BlockSpec((tm, 2 * SGU_WIDTH), lambda i: (i, P_Z // (2 * SGU_WIDTH))),
                             _vec_spec(SGU_WIDTH), _vec_spec(SGU_WIDTH),
                             pl.BlockSpec((SGU_GROUPS, 128, 128), lambda i: (0, 0, 0)),
                             pl.BlockSpec((128, SGU_GROUPS), lambda i: (0, 0))],
                  out_shape=[jax.ShapeDtypeStruct(dproj.shape, BF), vec, vec,
                             jax.ShapeDtypeStruct((SGU_GROUPS, 128, 128), F32),
                             jax.ShapeDtypeStruct((128, SGU_GROUPS), F32)],
                  scratch_shapes=[pltpu.VMEM((tm, SGU_WIDTH), F32), pltpu.VMEM((tm, SGU_WIDTH), F32)],
                  input_output_aliases={7: 0},
                  compiler_params=_params())(dy, proj, proj, lnw, lnb, w, b_t, dproj)


def _merge_fwd(y_sgu, y_attn, pa, pb, proj, *, name, after=None, tm=1024, tn=512):
    S, Dm = y_sgu.shape
    tm = _tile(S, tm, 8)
    nj = Dm // tn
    extra = [] if after is None else [after]

    def body(ys_ref, ya_ref, pa_ref, pb_ref, ga_ref, gb_ref, *rest):
        a_ref, b_ref, m_ref = rest[-3:]
        a = jnp.dot(ys_ref[...], pa_ref[...], preferred_element_type=F32)
        b = jnp.dot(ya_ref[...], pb_ref[...], preferred_element_type=F32)
        a_ref[...] = a.astype(BF)
        b_ref[...] = b.astype(BF)
        m_ref[...] = (_sigmoid(ga_ref[...].astype(F32)) * a + _sigmoid(gb_ref[...].astype(F32)) * b).astype(BF)
    row = pl.BlockSpec((tm, Dm), lambda i, j: (i, 0))
    col = pl.BlockSpec((Dm, tn), lambda i, j: (0, j))
    out = pl.BlockSpec((tm, tn), lambda i, j: (i, j))
    sh = jax.ShapeDtypeStruct((S, Dm), BF)
    return _pcall(body, name=name, grid=(S // tm, nj),
                  in_specs=[row, row, col, col, pl.BlockSpec((tm, tn), lambda i, j: (i, P_G // tn + j)),
                            pl.BlockSpec((tm, tn), lambda i, j: (i, (P_G + Dm) // tn + j))] + [ANY] * len(extra),
                  out_specs=[out, out, out], out_shape=[sh, sh, sh],
                  compiler_params=_params())(y_sgu, y_attn, pa, pb, proj, proj, *extra)


def _merge_bwd(do, w_out, a, b, proj, *, name, after=None, tm=512):
    S, Dm = a.shape
    tm = _tile(S, tm, 8)
    ga_blk, gb_blk = P_G // Dm, P_G // Dm + 1
    extra = [] if after is None else [after]

    def body(do_ref, w_ref, a_ref, b_ref, ga_ref, gb_ref, *rest):
        da_ref, db_ref, dg_ref = rest[-3:]
        dmv = lax.dot_general(do_ref[...], w_ref[...], (((1,), (1,)), ((), ())), preferred_element_type=F32)
        sa = _sigmoid(ga_ref[...].astype(F32))
        sb = _sigmoid(gb_ref[...].astype(F32))
        da_ref[...] = (dmv * sa).astype(BF)
        db_ref[...] = (dmv * sb).astype(BF)
        dg_ref[:, :Dm] = (dmv * a_ref[...].astype(F32) * sa * (1.0 - sa)).astype(BF)
        dg_ref[:, Dm:] = (dmv * b_ref[...].astype(F32) * sb * (1.0 - sb)).astype(BF)
    return _pcall(body, name=name, grid=(S // tm,),
                  in_specs=[_row_spec(tm, Dm), pl.BlockSpec((Dm, Dm), lambda i: (0, 0)), _row_spec(tm, Dm), _row_spec(tm, Dm),
                            pl.BlockSpec((tm, Dm), lambda i: (i, ga_blk)), pl.BlockSpec((tm, Dm), lambda i: (i, gb_blk))]
                  + [ANY] * len(extra),
                  out_specs=[_row_spec(tm, Dm), _row_spec(tm, Dm), pl.BlockSpec((tm, 2 * Dm), lambda i: (i, P_G // (2 * Dm)))],
                  out_shape=[jax.ShapeDtypeStruct((S, Dm), BF), jax.ShapeDtypeStruct((S, Dm), BF),
                             jax.ShapeDtypeStruct((S, IN_COLS), BF)],
                  compiler_params=_params())(do, w_out, a, b, proj, proj, *extra)


def _shift_rows(a, halo, k, up):
    n = a.shape[0]
    r8 = lax.broadcasted_iota(jnp.int32, (8, a.shape[1]), 0)
    if not up:
        rolled = pltpu.roll(a, k, 0)
        patch = jnp.where(r8 < k, pltpu.roll(halo, k, 0), rolled[:8])
        return jnp.concatenate([patch, rolled[8:]], axis=0)
    rolled = pltpu.roll(a, n - k, 0)
    patch = jnp.where(r8 >= 8 - k, pltpu.roll(halo, 8 - k, 0), rolled[n - 8:])
    return jnp.concatenate([rolled[:n - 8], patch], axis=0)


def _conv_taps(a, halo):
    return _shift_rows(a, halo, 2, False), _shift_rows(a, halo, 1, False), a


HALO = 16


def _prev_halo_spec(tm, Fd):
    return pl.BlockSpec((HALO, Fd), lambda i: (jnp.maximum(i * (tm // HALO) - 1, 0), 0))


def _conv_fwd(a_ref, halo_ref, cw_ref, cb_ref):
    halo = jnp.where(pl.program_id(0) > 0, halo_ref[...].astype(F32)[HALO - 8:], 0.0)
    t0, t1, t2 = _conv_taps(a_ref[...].astype(F32), halo)
    return t0, t1, t2, cb_ref[...] + cw_ref[0:1, :] * t0 + cw_ref[1:2, :] * t1 + cw_ref[2:3, :] * t2


def _ffn_act_fwd(a, up, cw, cb, *, name, tm=256):
    S, Fd = a.shape
    tm = _tile(S, tm, HALO)

    def body(a_ref, up_ref, halo_ref, cw_ref, cb_ref, o_ref, ac_ref):
        _, _, _, ac = _conv_fwd(a_ref, halo_ref, cw_ref, cb_ref)
        ac_ref[...] = ac.astype(BF)
        o_ref[...] = (ac * _sigmoid(ac) * up_ref[...].astype(F32)).astype(BF)
    sh = jax.ShapeDtypeStruct((S, Fd), BF)
    return _pcall(body, name=name, grid=(S // tm,),
                  in_specs=[_row_spec(tm, Fd), _row_spec(tm, Fd), _prev_halo_spec(tm, Fd),
                            pl.BlockSpec((3, Fd), lambda i: (0, 0)), _vec_spec(Fd)],
                  out_specs=[_row_spec(tm, Fd), _row_spec(tm, Fd)], out_shape=[sh, sh],
                  compiler_params=_params())(a, up, a, cw, cb)


def _ffn_act_bwd_a(dhf, ac, up, *, name, tm=512):
    S, Fd = ac.shape
    tm = _tile(S, tm, HALO)

    def body(dhf_ref, ac_ref, up_ref, dac_ref, dup_ref, dcb_ref):
        acv = ac_ref[...].astype(F32)
        s = _sigmoid(acv)
        dhf_v = dhf_ref[...].astype(F32)
        dup_ref[...] = (dhf_v * acv * s).astype(BF)
        dac = dhf_v * up_ref[...].astype(F32) * (s * (1.0 + acv * (1.0 - s)))
        dac_ref[...] = dac.astype(BF)
        _acc(dcb_ref, jnp.sum(dac, axis=0, keepdims=True))
    sh = jax.ShapeDtypeStruct((S, Fd), BF)
    return _pcall(body, name=name, grid=(S // tm,), in_specs=[_row_spec(tm, Fd)] * 3,
                  out_specs=[_row_spec(tm, Fd), _row_spec(tm, Fd), _vec_spec(Fd)],
                  out_shape=[sh, sh, jax.ShapeDtypeStruct((1, Fd), F32)], compiler_params=_params())(dhf, ac, up)


def _ffn_act_bwd_b(dac, a, cw, *, name, tm=256):
    S, Fd = dac.shape
    tm = _tile(S, tm, HALO)
    last = S // tm - 1

    def body(d_ref, halo_ref, a_ref, cw_ref, o_ref, dcw_ref):
        halo = jnp.where(pl.program_id(0) < last, halo_ref[...].astype(F32)[:8], 0.0)
        d = d_ref[...].astype(F32)
        d1, d2 = _shift_rows(d, halo, 1, True), _shift_rows(d, halo, 2, True)
        o_ref[...] = (cw_ref[2:3, :] * d + cw_ref[1:2, :] * d1 + cw_ref[0:1, :] * d2).astype(BF)
        av = a_ref[...].astype(F32)
        _acc(dcw_ref, jnp.concatenate([jnp.sum(av * d2, axis=0, keepdims=True),
                                       jnp.sum(av * d1, axis=0, keepdims=True),
                                       jnp.sum(av * d, axis=0, keepdims=True)], axis=0))
    return _pcall(body, name=name, grid=(S // tm,),
                  in_specs=[_row_spec(tm, Fd),
                            pl.BlockSpec((HALO, Fd), lambda i: (jnp.minimum((i + 1) * (tm // HALO), S // HALO - 1), 0)),
                            _row_spec(tm, Fd), pl.BlockSpec((3, Fd), lambda i: (0, 0))],
                  out_specs=[_row_spec(tm, Fd), pl.BlockSpec((3, Fd), lambda i: (0, 0))],
                  out_shape=[jax.ShapeDtypeStruct((S, Fd), BF), jax.ShapeDtypeStruct((3, Fd), F32)],
                  compiler_params=_params())(dac, dac, a, cw)


def _rope_tables(pos_col, inv_row, m1_row, m2_row):
    S = pos_col.shape[0]
    tm = _tile(S, 512, 8)

    def body(p_ref, inv_ref, m1_ref, m2_ref, c_ref, s1_ref, s2_ref):
        ang = p_ref[...] * inv_ref[...]
        sn = jnp.sin(ang)
        c_ref[...] = jnp.cos(ang)
        s1_ref[...] = -sn * m1_ref[...]
        s2_ref[...] = sn * m2_ref[...]
    sh = jax.ShapeDtypeStruct((S, 128), F32)
    return _pcall(body, name="rope_tables", grid=(S // tm,),
                  in_specs=[pl.BlockSpec((tm, 1), lambda i: (i, 0)), _vec_spec(128), _vec_spec(128), _vec_spec(128)],
                  out_specs=[_row_spec(tm, 128)] * 3, out_shape=[sh, sh, sh], compiler_params=_params())(
                      pos_col, inv_row, m1_row, m2_row)


def _rope_apply(x, c, s1, s2):
    outs = []
    for j in range(x.shape[1] // 128):
        xj = x[:, j * 128:(j + 1) * 128]
        outs.append(xj * c + pltpu.roll(xj, 120, 1) * s1 + pltpu.roll(xj, 8, 1) * s2)
    return outs[0] if len(outs) == 1 else jnp.concatenate(outs, axis=1)


def _rope_apply_t(d, c, s1, s2):
    outs = []
    for j in range(d.shape[1] // 128):
        dj = d[:, j * 128:(j + 1) * 128]
        outs.append(dj * c + pltpu.roll(dj * s1, 8, 1) + pltpu.roll(dj * s2, 120, 1))
    return outs[0] if len(outs) == 1 else jnp.concatenate(outs, axis=1)


def _rope_fwd(proj, c, s1, s2, *, name, tm=512):
    S = proj.shape[0]
    tm = _tile(S, tm, 8)

    def body(q_ref, k_ref, v_ref, c_ref, s1_ref, s2_ref, qo_ref, ko_ref, vo_ref):
        cv, s1v, s2v = c_ref[...], s1_ref[...], s2_ref[...]
        qo_ref[...] = (_rope_apply(q_ref[...].astype(F32), cv, s1v, s2v) * (HEAD_DIM ** -0.5)).astype(BF)
        ko_ref[...] = _rope_apply(k_ref[...].astype(F32), cv, s1v, s2v).astype(BF)
        vo_ref[...] = v_ref[...].astype(BF)
    return _pcall(body, name=name, grid=(S // tm,),
                  in_specs=[pl.BlockSpec((tm, Q_END), lambda i: (i, P_Q // Q_END)),
                            pl.BlockSpec((tm, 128), lambda i: (i, P_K // 128)),
                            pl.BlockSpec((tm, 128), lambda i: (i, P_V // 128)),
                            _row_spec(tm, 128), _row_spec(tm, 128), _row_spec(tm, 128)],
                  out_specs=[_row_spec(tm, Q_END), _row_spec(tm, 128), _row_spec(tm, 128)],
                  out_shape=[jax.ShapeDtypeStruct((S, Q_END), BF), jax.ShapeDtypeStruct((S, 128), BF),
                             jax.ShapeDtypeStruct((S, 128), BF)],
                  compiler_params=_params())(proj, proj, proj, c, s1, s2)


def _rope_bwd(dq, dk, dv, c, s1, s2, dproj, *, name, tm=512):
    S = dq.shape[0]
    tm = _tile(S, tm, 8)
    tabs = [_row_spec(tm, 128)] * 3
    shape = jax.ShapeDtypeStruct(dproj.shape, BF)

    def body_q(dq_ref, c_ref, s1_ref, s2_ref, _, o_ref):
        o_ref[...] = _rope_apply_t(dq_ref[...].astype(F32), c_ref[...], s1_ref[...], s2_ref[...]).astype(BF)
    dproj = _pcall(body_q, name=name + "_q", grid=(S // tm,), in_specs=[_row_spec(tm, Q_END)] + tabs + [ANY],
                   out_specs=pl.BlockSpec((tm, Q_END), lambda i: (i, P_Q // Q_END)), out_shape=shape,
                   input_output_aliases={4: 0}, compiler_params=_params())(dq, c, s1, s2, dproj)

    def body_kv(dk_ref, dv_ref, c_ref, s1_ref, s2_ref, _, o_ref):
        o_ref[:, :128] = _rope_apply_t(dk_ref[...], c_ref[...], s1_ref[...], s2_ref[...]).astype(BF)
        o_ref[:, 128:] = dv_ref[...].astype(BF)
    return _pcall(body_kv, name=name + "_kv", grid=(S // tm,),
                  in_specs=[_row_spec(tm, 128), _row_spec(tm, 128)] + tabs + [ANY],
                  out_specs=pl.BlockSpec((tm, 256), lambda i: (i, P_K // 256)), out_shape=shape,
                  input_output_aliases={5: 0}, compiler_params=_params())(dk, dv, c, s1, s2, dproj)


def _lane_lo(shape):
    return lax.broadcasted_iota(jnp.int32, shape, 1) < HEAD_DIM


def _stack_heads(x, g):
    lo = _lane_lo((ATTN_BLOCK, 128))
    zero = jnp.zeros((ATTN_BLOCK, 128), x.dtype)
    parts = []
    for p in range(Q_PER_KV // 2):
        xp = x[:, (g * 4 + p) * 128:(g * 4 + p + 1) * 128]
        parts += [jnp.where(lo, xp, zero), jnp.where(lo, zero, xp)]
    return jnp.concatenate(parts, axis=0)


def _unstack_heads(o2):
    lo = _lane_lo((ATTN_BLOCK, 128))
    return [jnp.where(lo, o2[2 * p * ATTN_BLOCK:(2 * p + 1) * ATTN_BLOCK], o2[(2 * p + 1) * ATTN_BLOCK:(2 * p + 2) * ATTN_BLOCK])
            for p in range(Q_PER_KV // 2)]


def _dup_half(prev, cur, g):
    x = jnp.concatenate([prev, cur], axis=0).astype(F32)
    lo = _lane_lo(x.shape)
    r = pltpu.roll(x, HEAD_DIM, 1)
    return (jnp.where(lo, x, r) if g == 0 else jnp.where(lo, r, x)).astype(BF)


def _fold_halves(x):
    return x + pltpu.roll(x, HEAD_DIM, 1)


def _attn_bias():
    i = lax.broadcasted_iota(jnp.int32, (Q_PER_KV * ATTN_BLOCK, 2 * ATTN_BLOCK), 0) & (ATTN_BLOCK - 1)
    j = lax.broadcasted_iota(jnp.int32, (Q_PER_KV * ATTN_BLOCK, 2 * ATTN_BLOCK), 1)
    band = (j > i) & (j <= i + ATTN_BLOCK)
    return jnp.stack([jnp.where(band & (j >= ATTN_BLOCK), 0.0, -jnp.inf), jnp.where(band, 0.0, -jnp.inf)]).astype(F32)


def _both(x):
    return jnp.concatenate([x, x], axis=1)


def _row_sums(x_bf):
    return jnp.dot(x_bf, jnp.ones((x_bf.shape[1], 128), BF), preferred_element_type=F32)


def _attn_probs(qs, kb, sink, bias):
    s = lax.dot_general(qs, kb, (((1,), (1,)), ((), ())), preferred_element_type=F32) + bias
    m = jnp.maximum(jnp.broadcast_to(jnp.max(s, axis=-1, keepdims=True), sink.shape), sink)
    return jnp.exp(s - _both(m)), jnp.exp(sink - m)


def _attn_specs(S):
    nb = S // ATTN_BLOCK
    qs = pl.BlockSpec((ATTN_BLOCK, Q_END), lambda n: (n, 0))
    cur = pl.BlockSpec((ATTN_BLOCK, 128), lambda n: (n, 0))
    prev = pl.BlockSpec((ATTN_BLOCK, 128), lambda n: (jnp.maximum(n - 1, 0), 0))
    sink = pl.BlockSpec((N_KV_HEADS, Q_PER_KV * ATTN_BLOCK, 128), lambda n: (0, 0, 0))
    bias = pl.BlockSpec((None, Q_PER_KV * ATTN_BLOCK, 2 * ATTN_BLOCK), lambda n: (jnp.minimum(n, 1), 0, 0))
    return nb, qs, cur, prev, sink, bias


def _attn_fwd(q, k, v, sink_rows, bias, *, name):
    S = q.shape[0]
    nb, qs, cur, prev, sink, bs = _attn_specs(S)

    def body(q_ref, kp_ref, kc_ref, vp_ref, vc_ref, sk_ref, b_ref, o_ref):
        for g in range(N_KV_HEADS):
            kb = _dup_half(kp_ref[...], kc_ref[...], g)
            vb = _dup_half(vp_ref[...], vc_ref[...], g)
            p, es = _attn_probs(_stack_heads(q_ref[...], g), kb, sk_ref[g], b_ref[...])
            ones = jnp.ones((2 * ATTN_BLOCK, 128), BF)
            o3 = jnp.dot(p.astype(BF), jnp.concatenate([vb, ones], axis=1), preferred_element_type=F32)
            o2 = o3[:, :128] / (o3[:, 128:] + es)
            for t, tile in enumerate(_unstack_heads(o2)):
                o_ref[:, (g * 4 + t) * 128:(g * 4 + t + 1) * 128] = tile.astype(BF)
    return _pcall(body, name=name, grid=(nb,), in_specs=[qs, prev, cur, prev, cur, sink, bs], out_specs=qs,
                  out_shape=jax.ShapeDtypeStruct(q.shape, BF), compiler_params=_params())(q, k, k, v, v, sink_rows, bias)


def _attn_bwd(do, q, k, v, sink_rows, bias, *, name):
    S = q.shape[0]
    nb, qs, cur, prev, sink, bs = _attn_specs(S)
    full = pl.BlockSpec((S, 128), lambda n: (0, 0))
    dsk_spec = pl.BlockSpec((N_KV_HEADS, Q_PER_KV, 128), lambda n: (0, 0, 0))

    def body(do_ref, q_ref, kp_ref, kc_ref, vp_ref, vc_ref, sk_ref, b_ref, dq_ref, dk_ref, dv_ref, dsk_ref):
        n = pl.program_id(0)

        @pl.when(n == 0)
        def _():
            dk_ref[...] = jnp.zeros_like(dk_ref)
            dv_ref[...] = jnp.zeros_like(dv_ref)
            dsk_ref[...] = jnp.zeros_like(dsk_ref)
        sub = lax.broadcasted_iota(jnp.int32, (Q_PER_KV, 128), 0)
        dkf, dvf = [], []
        for g in range(N_KV_HEADS):
            qst = _stack_heads(q_ref[...], g)
            dos = _stack_heads(do_ref[...], g)
            kb = _dup_half(kp_ref[...], kc_ref[...], g)
            vb = _dup_half(vp_ref[...], vc_ref[...], g)
            pu, es = _attn_probs(qst, kb, sk_ref[g], b_ref[...])
            inv = 1.0 / (_row_sums(pu.astype(BF)) + es)
            p = pu * _both(inv)
            dp = lax.dot_general(dos, vb, (((1,), (1,)), ((), ())), preferred_element_type=F32)
            dd = _row_sums((p * dp).astype(BF))
            ds = (p * (dp - _both(dd))).astype(BF)
            dq2 = jnp.dot(ds, kb, preferred_element_type=F32) * (HEAD_DIM ** -0.5)
            for t, tile in enumerate(_unstack_heads(dq2)):
                dq_ref[:, (g * 4 + t) * 128:(g * 4 + t + 1) * 128] = tile.astype(BF)
            dkf.append(_fold_halves(lax.dot_general(ds, qst, (((0,), (0,)), ((), ())), preferred_element_type=F32)))
            dvf.append(_fold_halves(lax.dot_general(p.astype(BF), dos, (((0,), (0,)), ((), ())),
                                                    preferred_element_type=F32)))
            dsr = -(es * inv * dd)
            upd = jnp.zeros((Q_PER_KV, 128), F32)
            for h in range(Q_PER_KV):
                upd = jnp.where(sub == h, jnp.sum(dsr[h * ATTN_BLOCK:(h + 1) * ATTN_BLOCK], axis=0, keepdims=True), upd)
            dsk_ref[g] += upd
        lo = _lane_lo((2 * ATTN_BLOCK, 128))
        dkb = jnp.where(lo, dkf[0], dkf[1])
        dvb = jnp.where(lo, dvf[0], dvf[1])
        r0 = pl.multiple_of(n * ATTN_BLOCK, ATTN_BLOCK)
        dk_ref[pl.ds(r0, ATTN_BLOCK), :] += dkb[ATTN_BLOCK:]
        dv_ref[pl.ds(r0, ATTN_BLOCK), :] += dvb[ATTN_BLOCK:]

        @pl.when(n > 0)
        def _():
            rp = pl.multiple_of((n - 1) * ATTN_BLOCK, ATTN_BLOCK)
            dk_ref[pl.ds(rp, ATTN_BLOCK), :] += dkb[:ATTN_BLOCK]
            dv_ref[pl.ds(rp, ATTN_BLOCK), :] += dvb[:ATTN_BLOCK]
    return _pcall(body, name=name, grid=(nb,), in_specs=[qs, qs, prev, cur, prev, cur, sink, bs],
                  out_specs=[qs, full, full, dsk_spec],
                  out_shape=[jax.ShapeDtypeStruct(q.shape, BF), jax.ShapeDtypeStruct((S, 128), F32),
                             jax.ShapeDtypeStruct((S, 128), F32), jax.ShapeDtypeStruct((N_KV_HEADS, Q_PER_KV, 128), F32)],
                  compiler_params=_params())(do, q, k, k, v, v, sink_rows, bias)


def _ada_fwd(c_all, ada_w):
    ncol = ada_w.shape[2]

    def body(c_ref, w_ref, o_ref):
        cv = c_ref[...]
        ca = (cv * _sigmoid(cv)).astype(BF)
        for l in range(DEPTH):
            o_ref[:, l * ncol:(l + 1) * ncol] = jnp.dot(ca, w_ref[l].astype(BF), preferred_element_type=F32)
    return _pcall(body, name="ada_fwd", out_shape=jax.ShapeDtypeStruct((N_DEV, DEPTH * ncol), F32),
                  compiler_params=_params())(c_all, ada_w)


def _ada_bwd(c_all, dm):
    ncol = dm.shape[2]

    def body(c_ref, dm_ref, o_ref):
        cv = c_ref[...]
        ca = (cv * _sigmoid(cv)).astype(BF)
        for l in range(DEPTH):
            o_ref[l] = lax.dot_general(ca, dm_ref[l].astype(BF), (((0,), (0,)), ((), ())), preferred_element_type=F32)
    return _pcall(body, name="ada_bwd", out_shape=jax.ShapeDtypeStruct((DEPTH, D_MODEL, ncol), F32),
                  compiler_params=_params())(c_all, dm)


def _adamw(w, g, m, v, *, name):
    R, C = w.shape
    tr = R
    for t in range(8, 513, 8):
        if R % t == 0:
            tr = t
    c1 = 1.0 - ADAM_B1 ** ADAM_STEP
    c2 = 1.0 - ADAM_B2 ** ADAM_STEP

    def body(w_ref, g_ref, m_ref, v_ref, d_ref, mo_ref, vo_ref):
        gv = g_ref[...]
        mn = ADAM_B1 * m_ref[...] + (1.0 - ADAM_B1) * gv
        vn = ADAM_B2 * v_ref[...] + (1.0 - ADAM_B2) * (gv * gv)
        mo_ref[...] = mn
        vo_ref[...] = vn
        d_ref[...] = -ADAM_LR * ((mn * (1.0 / c1)) / (jnp.sqrt(vn * (1.0 / c2)) + ADAM_EPS) + ADAM_WD * w_ref[...])
    spec = pl.BlockSpec((tr, C), lambda i: (i, 0))
    sh = jax.ShapeDtypeStruct((R, C), F32)
    return _pcall(body, name=name, grid=(R // tr,), in_specs=[spec] * 4, out_specs=[spec] * 3, out_shape=[sh, sh, sh],
                  compiler_params=_params())(w, g, m, v)


def _adamw_layers(w, g_layers, m, v, *, name):
    L, R, C = w.shape
    assert L == 2 and len(g_layers) == 2
    tr = R
    for t in range(8, 513, 8):
        if R % t == 0:
            tr = t
    c1 = 1.0 - ADAM_B1 ** ADAM_STEP
    c2 = 1.0 - ADAM_B2 ** ADAM_STEP

    def body(w_ref, g0_ref, g1_ref, m_ref, v_ref, go_ref, d_ref, mo_ref, vo_ref):
        gv = jnp.where(pl.program_id(0) == 0, g0_ref[...], g1_ref[...])
        go_ref[...] = gv
        mn = ADAM_B1 * m_ref[...] + (1.0 - ADAM_B1) * gv
        vn = ADAM_B2 * v_ref[...] + (1.0 - ADAM_B2) * (gv * gv)
        mo_ref[...] = mn
        vo_ref[...] = vn
        d_ref[...] = -ADAM_LR * ((mn * (1.0 / c1)) / (jnp.sqrt(vn * (1.0 / c2)) + ADAM_EPS) + ADAM_WD * w_ref[...])
    spec = pl.BlockSpec((None, tr, C), lambda l, i: (l, i, 0))
    sh = jax.ShapeDtypeStruct((L, R, C), F32)
    g_specs = [pl.BlockSpec((tr, C), lambda l, i, k=k: (jnp.where(l == k, i, 0), 0)) for k in range(L)]
    return _pcall(body, name=name, grid=(L, R // tr), in_specs=[spec] + g_specs + [spec, spec], out_specs=[spec] * 4,
                  out_shape=[sh] * 4, compiler_params=_params())(w, *g_layers, m, v)


def _sum8(parts, *, name):
    _, R, C = parts.shape
    tr = _tile(R, 512, 16)

    def body(p_ref, o_ref):
        acc = p_ref[0].astype(F32)
        for k in range(1, N_DEV):
            acc = acc + p_ref[k].astype(F32)
        o_ref[...] = acc
    return _pcall(body, name=name, grid=(R // tr,), in_specs=[pl.BlockSpec((N_DEV, tr, C), lambda i: (0, i, 0))],
                  out_specs=pl.BlockSpec((tr, C), lambda i: (i, 0)), out_shape=jax.ShapeDtypeStruct((R, C), F32),
                  compiler_params=_params())(parts)


MESH_ID = pl.DeviceIdType.MESH
ANY = pl.BlockSpec(memory_space=pl.ANY)


def _all_gather(x, *, name, after=None):
    R, C = x.shape
    extra = [] if after is None else [after]

    def body(x_ref, *rest):
        out_ref, send_sems, recv_sems, local_sem = rest[-4:]
        mx, my, mc = lax.axis_index("x"), lax.axis_index("y"), lax.axis_index("c")
        me, sibling = (mx, my, mc), (mx, my, 1 - mc)
        chips = [(1 - mx, my), (mx, 1 - my), (1 - mx, 1 - my)]

        def blk(px, py, pc):
            return out_ref.at[4 * px + 2 * py + pc]

        def copy(k, block, to, src=None):
            return pltpu.make_async_remote_copy(
                src_ref=blk(*block) if src is None else src, dst_ref=blk(*block),
                send_sem=send_sems.at[k], recv_sem=recv_sems.at[k], device_id=to, device_id_type=MESH_ID)

        mine = pltpu.make_async_copy(x_ref, blk(*me), local_sem)
        mine.start()
        first = [copy(0, me, sibling, src=x_ref)]
        first += [copy(1 + j, me, (*chip, mc), src=x_ref) for j, chip in enumerate(chips)]
        for cp in first:
            cp.start()
        passed = [copy(4 + j, (*chip, mc), sibling) for j, chip in enumerate(chips)]
        for j, chip in enumerate(chips):
            copy(1 + j, (*chip, mc), me).wait_recv()
            passed[j].start()
        copy(0, sibling, me).wait_recv()
        for j, chip in enumerate(chips):
            copy(4 + j, (*chip, 1 - mc), me).wait_recv()
        for cp in first + passed:
            cp.wait_send()
        mine.wait()
    return _pcall(body, name=name, in_specs=[ANY] * (1 + len(extra)), out_specs=ANY,
                  out_shape=jax.ShapeDtypeStruct((N_DEV, R, C), x.dtype),
                  scratch_shapes=[pltpu.SemaphoreType.DMA((7,)), pltpu.SemaphoreType.DMA((7,)), pltpu.SemaphoreType.DMA],
                  compiler_params=pltpu.CompilerParams(has_side_effects=True))(x, *extra)


HBM_SPEC = pl.BlockSpec(memory_space=pltpu.HBM)
SEM_SPEC = pl.BlockSpec(memory_space=pltpu.SEMAPHORE)
DATAFLOW = pltpu.SideEffectType.DATAFLOW_SIDE_EFFECTING


def _coords():
    return lax.axis_index("x"), lax.axis_index("y"), lax.axis_index("c")


def _other_chips(mx, my):
    return [(1 - mx, my), (mx, 1 - my), (1 - mx, 1 - my)]


def _plan_gather_ici(refs, send, recv):
    src, land = refs
    mx, my, mc = _coords()
    return [pltpu.make_async_remote_copy(src_ref=src, dst_ref=land.at[mc, 2 * mx + my], send_sem=send[j], recv_sem=recv[j],
                                         device_id=(px, py, mc), device_id_type=MESH_ID)
            for j, (px, py) in enumerate(_other_chips(mx, my))]


def _plan_gather_d2d(refs, send, recv):
    (land,) = refs
    mx, my, mc = _coords()
    return [pltpu.make_async_remote_copy(src_ref=land.at[mc], dst_ref=land.at[mc], send_sem=send[0], recv_sem=recv[0],
                                         device_id=(mx, my, 1 - mc), device_id_type=MESH_ID)]


def _plan_reduce_d2d(refs, send, recv):
    g, land = refs
    mx, my, mc = _coords()
    return [pltpu.make_async_remote_copy(src_ref=g.at[1 - mc], dst_ref=land, send_sem=send[0], recv_sem=recv[0],
                                         device_id=(mx, my, 1 - mc), device_id_type=MESH_ID)]


def _plan_reduce_ici(refs, send, recv):
    h, land = refs
    mx, my, mc = _coords()
    return [pltpu.make_async_remote_copy(src_ref=h.at[2 * px + py], dst_ref=land.at[j], send_sem=send[j], recv_sem=recv[j],
                                         device_id=(px, py, mc), device_id_type=MESH_ID)
            for j, (px, py) in enumerate(_other_chips(mx, my))]


def _rdma_start(bufs, n, plan, *, name, after=None):
    nb = len(bufs)
    extra = [] if after is None else [after]
    ne = len(extra)

    def body(*refs):
        ins, send, recv = refs[:nb], refs[nb + ne:nb + ne + n], refs[nb + ne + n:nb + ne + 2 * n]
        token = refs[-1]
        for cp in plan(ins, send, recv):
            cp.start()
        token[...] = jnp.zeros_like(token)
    out = _pcall(body, name=name,
                 out_shape=tuple([pltpu.SemaphoreType.DMA(())] * (2 * n) + [pltpu.HBM(b.shape, b.dtype) for b in bufs]
                                 + [jax.ShapeDtypeStruct((8, 128), F32)]),
                 in_specs=tuple([HBM_SPEC] * nb + [ANY] * ne),
                 out_specs=tuple([SEM_SPEC] * (2 * n) + [HBM_SPEC] * nb + [pl.BlockSpec(memory_space=pltpu.VMEM)]),
                 input_output_aliases={i: 2 * n + i for i in range(nb)},
                 compiler_params=pltpu.CompilerParams(has_side_effects=DATAFLOW))(
                     *[pltpu.with_memory_space_constraint(b, pltpu.HBM) for b in bufs], *extra)
    return list(out[:2 * n]), list(out[2 * n:2 * n + nb]), out[-1]


def _rdma_wait(sems, bufs, n, plan, after, *, name):
    nb = len(bufs)

    def body(*refs):
        ins, send, recv = refs[:nb], refs[nb:nb + n], refs[nb + n:nb + 2 * n]
        for cp in plan(ins, send, recv):
            cp.wait_send()
            cp.wait_recv()
    out = _pcall(body, name=name, out_shape=tuple(pltpu.HBM(b.shape, b.dtype) for b in bufs),
                 in_specs=tuple([HBM_SPEC] * nb + [SEM_SPEC] * (2 * n) + [ANY]), out_specs=tuple([HBM_SPEC] * nb),
                 input_output_aliases={i: i for i in range(nb)},
                 compiler_params=pltpu.CompilerParams(has_side_effects=DATAFLOW))(*bufs, *sems, after)
    return list(out)


def _sum_pair(g, land, cidx, *, name):
    _, nchip, R, C = g.shape
    tr = _tile(R, 1056, 16)

    def body(c_ref, g_ref, l_ref, o_ref):
        o_ref[...] = g_ref[...] + l_ref[...]
    grid_spec = pltpu.PrefetchScalarGridSpec(
        num_scalar_prefetch=1, grid=(nchip, R // tr),
        in_specs=[pl.BlockSpec((None, None, tr, C), lambda p, i, c_ref: (c_ref[0], p, i, 0)),
                  pl.BlockSpec((None, tr, C), lambda p, i, c_ref: (p, i, 0))],
        out_specs=pl.BlockSpec((None, tr, C), lambda p, i, c_ref: (p, i, 0)))
    return _pcall(body, name=name, grid_spec=grid_spec, out_shape=jax.ShapeDtypeStruct((nchip, R, C), BF),
                  compiler_params=_params())(cidx, g, land)


def _sum_chips(h, land, chipidx, *, name):
    _, R, C = h.shape
    tr = _tile(R, 1056, 16)

    def body(c_ref, h_ref, l_ref, o_ref):
        acc = h_ref[...].astype(F32)
        for j in range(3):
            acc = acc + l_ref[j].astype(F32)
        o_ref[...] = acc
    grid_spec = pltpu.PrefetchScalarGridSpec(
        num_scalar_prefetch=1, grid=(R // tr,),
        in_specs=[pl.BlockSpec((None, tr, C), lambda i, c_ref: (c_ref[0], i, 0)),
                  pl.BlockSpec((3, tr, C), lambda i, c_ref: (0, i, 0))],
        out_specs=pl.BlockSpec((tr, C), lambda i, c_ref: (i, 0)))
    return _pcall(body, name=name, grid_spec=grid_spec, out_shape=jax.ShapeDtypeStruct((R, C), F32),
                  compiler_params=_params())(chipidx, h, land)


PART_IN = ("w_in",)
PART_MIX = ("proj_a", "proj_b", "w_out")
PART_FFN = ("ffn_w_gate", "ffn_w_up", "ffn_w_down")


def _part_rows(names):
    return sum(BIG_ROWS[n] for n in names)


def _part_offsets(names):
    off, r = {}, 0
    for n in names:
        off[n] = r
        r += BIG_ROWS[n]
    return off


def _pack_shards(shards, l, names):
    return jnp.concatenate([(shards[n][l].T if n in COL_SHARDED else shards[n][l]).astype(BF) for n in names], axis=0)


def _unpack_weights(full8, names):
    off = _part_offsets(names)

    def whole(n):
        return full8[:, off[n]:off[n] + BIG_ROWS[n], :].reshape(N_DEV * BIG_ROWS[n], 1024)
    out = {}
    if "w_in" in names:
        wt_in = whole("w_in")
        out["wt_in"] = jnp.concatenate([wt_in[V_END:], wt_in[:V_END]], axis=0)
    for n in ("proj_a", "proj_b", "w_out"):
        if n in names:
            out[n] = whole(n)
    if "ffn_w_gate" in names:
        out["wt_gate"], out["wt_up"], out["w_down"] = whole("ffn_w_gate"), whole("ffn_w_up"), whole("ffn_w_down")
    return out


def _from_land(land):
    return land.transpose(1, 0, 2, 3).reshape(N_DEV, land.shape[2], 1024)


def _pack_grads(wg, names):
    full = {"proj_a": wg.get("proj_a"), "proj_b": wg.get("proj_b"), "w_out": wg.get("w_out"), "ffn_w_down": wg.get("w_down"),
            "ffn_w_gate": wg.get("wt_gate"), "ffn_w_up": wg.get("wt_up")}
    if "w_in" in names:
        full["w_in"] = jnp.concatenate([wg["wt_in"][P_Q:], wg["wt_in"][:P_Q]], axis=0)
    blocks = jnp.concatenate([full[n].reshape(N_DEV, BIG_ROWS[n], 1024) for n in names], axis=1)
    return blocks.reshape(4, 2, _part_rows(names), 1024).transpose(1, 0, 2, 3)


def _unpack_shard_grads(gs, names):
    off = _part_offsets(names)
    out = {}
    for n in names:
        blk = gs[off[n]:off[n] + BIG_ROWS[n]]
        out[n] = blk.T if n in COL_SHARDED else blk
    return out


def _rope_setup(positions):
    S = positions.shape[0]
    inv = ROPE_THETA ** (-jnp.arange(0, ROT_DIM, 2, dtype=F32) / ROT_DIM)
    lane = np.arange(128) % HEAD_DIM
    half = ROT_DIM // 2
    inv_row = jnp.where(lane < ROT_DIM, jnp.tile(inv, 128 // half), 0.0)[None, :].astype(F32)
    m1_row = jnp.asarray((lane < half).astype(np.float32))[None, :]
    m2_row = jnp.asarray(((lane >= half) & (lane < ROT_DIM)).astype(np.float32))[None, :]
    return (*_rope_tables(positions.astype(F32).reshape(S, 1), inv_row, m1_row, m2_row), _attn_bias())


def _hook(hooks, point, after):
    f = None if hooks is None else hooks.get(point)
    return None if f is None else f(after)


def _layer_fwd(l, x, mod_l, W, small, rope, hooks=None):
    rc, rs1, rs2, bias = rope
    sh1, sc1, g1, sh2, sc2, g2 = [mod_l[i * D_MODEL:(i + 1) * D_MODEL][None, :] for i in range(6)]
    nw1, nw2 = small["norm1_w"][l][None, :], small["norm2_w"][l][None, :]
    tok = _hook(hooks, "mm_in", x)
    h, (proj,) = _norm_mm(x, nw1, sc1, sh1, [W["wt_in"]], name=f"mm_in{l}", after=tok, tm=2048, tn_cap=768)
    q_r, k_r, v_b = _rope_fwd(proj, rc, rs1, rs2, name=f"rope_fwd{l}")
    sink_rows = jnp.repeat(small["attn_sinks"][l].reshape(N_KV_HEADS, Q_PER_KV), ATTN_BLOCK, axis=1)
    sink_rows = jnp.broadcast_to(sink_rows[..., None], sink_rows.shape + (128,))
    y_attn = _attn_fwd(q_r, k_r, v_b, sink_rows, bias, name=f"attn_fwd{l}")
    lnw, lnb = small["sgu_ln_w"][l][None, :], small["sgu_ln_b"][l][None, :]
    sgu_bt = small["sgu_b"][l].T
    y_sgu = _sgu_fwd(proj, lnw, lnb, small["sgu_w"][l], sgu_bt, name=f"sgu_fwd{l}", after=_hook(hooks, "sgu", y_attn))
    tok = _hook(hooks, "mm_pa", y_sgu)
    a_br, b_br, merged = _merge_fwd(y_sgu, y_attn, W["proj_a"], W["proj_b"], proj, name=f"merge_fwd{l}", after=tok)
    x1, o1 = _mm(merged, W["w_out"], nt=False, out_dtype=F32, name=f"mm_out{l}", res=x, gvec=g1)
    tok = _hook(hooks, "mm_gu", x1)
    h2, (a_g, a_u) = _norm_mm(x1, nw2, sc2, sh2, [W["wt_gate"], W["wt_up"]], name=f"mm_gu{l}", after=tok, tn_cap=1408)
    cw, cb = small["ffn_conv_w"][l], small["ffn_conv_b"][l][None, :]
    hf, a_c = _ffn_act_fwd(a_g, a_u, cw, cb, name=f"ffn_act_fwd{l}")
    x2, o2 = _mm(hf, W["w_down"], nt=False, out_dtype=F32, name=f"mm_down{l}", res=x1, gvec=g2)
    saved = dict(x=x, h=h, proj=proj, q_r=q_r, k_r=k_r, v_b=v_b, sink_rows=sink_rows, y_attn=y_attn, y_sgu=y_sgu,
                 a_br=a_br, b_br=b_br, merged=merged, x1=x1, o1=o1, h2=h2, a_g=a_g, a_u=a_u, a_c=a_c, hf=hf, o2=o2)
    return x2, saved


def _layer_bwd(l, dx, do2, dg2, mod_l, W, small, rope, sv, below=None, hooks=None, wg=None):
    rc, rs1, rs2, bias = rope
    sh1, sc1, g1, sh2, sc2, g2 = [mod_l[i * D_MODEL:(i + 1) * D_MODEL][None, :] for i in range(6)]
    nw1, nw2 = small["norm1_w"][l][None, :], small["norm2_w"][l][None, :]
    cw = small["ffn_conv_w"][l]
    lnw, lnb = small["sgu_ln_w"][l][None, :], small["sgu_ln_b"][l][None, :]
    sgu_bt = small["sgu_b"][l].T
    wg = {} if wg is None else wg
    dhf = _mm(do2, W["w_down"], nt=True, out_dtype=BF, name=f"mm_down_dx{l}", after=_hook(hooks, "mm_down_dx", do2),
              tn_cap=1408)
    wg["w_down"] = _mm_tn(sv["hf"], do2, name=f"mm_down_dw{l}")
    dac, dup, dcb = _ffn_act_bwd_a(dhf, sv["a_c"], sv["a_u"], name=f"ffn_act_bwd_a{l}")
    da, dcw = _ffn_act_bwd_b(dac, sv["a_g"], cw, name=f"ffn_act_bwd_b{l}")
    dh2 = _mm([da, dup], [W["wt_gate"], W["wt_up"]], nt=False, out_dtype=F32, name=f"mm_gu_dx{l}",
              after=_hook(hooks, "mm_gu_dx", da))
    wg["wt_gate"] = _mm_tn(da, sv["h2"], name=f"mm_gate_dw{l}")
    wg["wt_up"] = _mm_tn(dup, sv["h2"], name=f"mm_up_dw{l}")
    dx1, dnw2, dsc2, dsh2, do1, dg1 = _normmod_bwd(dh2, sv["x1"], nw2, sc2, sh2, dx, (sv["o1"], g1), name=f"normmod2_bwd{l}")
    d_a, d_b, dproj = _merge_bwd(do1, W["w_out"], sv["a_br"], sv["b_br"], sv["proj"], name=f"merge_bwd{l}",
                                 after=_hook(hooks, "merge_bwd", do1))
    wg["w_out"] = _mm_tn(sv["merged"], do1, name=f"mm_out_dw{l}")
    dysgu = _mm(d_a, W["proj_a"], nt=True, out_dtype=F32, name=f"mm_pa_dx{l}", after=_hook(hooks, "mm_pa_dx", d_a))
    dyattn = _mm(d_b, W["proj_b"], nt=True, out_dtype=BF, name=f"mm_pb_dx{l}")
    wg["proj_a"] = _mm_tn(sv["y_sgu"], d_a, name=f"mm_pa_dw{l}")
    wg["proj_b"] = _mm_tn(sv["y_attn"], d_b, name=f"mm_pb_dw{l}")
    dproj, dlnw, dlnb, dsguw, dsgubt = _sgu_bwd(dysgu, sv["proj"], lnw, lnb, small["sgu_w"][l], sgu_bt, dproj,
                                                name=f"sgu_bwd{l}")
    dq_r, dk_r, dv_b, dsk = _attn_bwd(dyattn, sv["q_r"], sv["k_r"], sv["v_b"], sv["sink_rows"], bias, name=f"attn_bwd{l}")
    dproj = _rope_bwd(dq_r, dk_r, dv_b, rc, rs1, rs2, dproj, name=f"rope_bwd{l}")
    wg["wt_in"] = _mm_tn(dproj, sv["h"], name=f"mm_in_dw{l}")
    dh = _mm(dproj, W["wt_in"], nt=False, out_dtype=F32, name=f"mm_in_dx{l}", after=_hook(hooks, "mm_in_dx", wg["wt_in"]))
    dx0, dnw1, dsc1, dsh1, *gate_below = _normmod_bwd(dh, sv["x"], nw1, sc1, sh1, dx1, below, name=f"normmod1_bwd{l}")
    dmod = jnp.concatenate([dsh1, dsc1, dg1, dsh2, dsc2, dg2], axis=1)[0]
    sg = {"norm1_w": dnw1[0], "norm2_w": dnw2[0], "attn_sinks": dsk[:, :, 0].reshape(N_Q_HEADS),
          "sgu_ln_w": dlnw[0], "sgu_ln_b": dlnb[0], "sgu_w": dsguw, "sgu_b": dsgubt.T,
          "ffn_conv_w": dcw, "ffn_conv_b": dcb[0]}
    return (dx0, *gate_below), wg, sg, dmod


SMALL = ("ada_b", "norm1_w", "attn_sinks", "sgu_ln_w", "sgu_ln_b", "sgu_w", "sgu_b", "norm2_w", "ffn_conv_b", "final_norm_w")
WEIGHT_ORDER = ("ada_w", "ada_b", "norm1_w", "w_in", "attn_sinks", "sgu_ln_w", "sgu_ln_b", "sgu_w", "sgu_b", "proj_a", "proj_b",
                "w_out", "norm2_w", "ffn_w_gate", "ffn_w_up", "ffn_conv_w", "ffn_conv_b", "ffn_w_down", "final_norm_w")


def _flat_pack(arrs, rows):
    flat = jnp.concatenate([a.reshape(-1) for a in arrs])
    return jnp.pad(flat, (0, rows * 1024 - flat.shape[0])).reshape(rows, 1024)


def _flat_unpack(buf, shapes):
    flat = buf.reshape(-1)
    out, o = [], 0
    for s in shapes:
        n = int(np.prod(s))
        out.append(flat[o:o + n].reshape(s))
        o += n
    return out


def _adam2d(w, g, m, v, *, name):
    shp = w.shape
    r2 = (int(np.prod(shp[:-1])), shp[-1]) if len(shp) > 1 else (1, shp[0])
    d, mn, vn = _adamw(w.reshape(r2), g.reshape(r2), m.reshape(r2), v.reshape(r2), name=name)
    return d.reshape(shp), mn.reshape(shp), vn.reshape(shp)


def kernel(x, c, positions, ada_w, ada_b, norm1_w, w_in, attn_sinks, sgu_ln_w, sgu_ln_b, sgu_w, sgu_b, proj_a, proj_b, w_out, norm2_w, ffn_w_gate, ffn_w_up, ffn_conv_w, ffn_conv_b, ffn_w_down, final_norm_w, loss_target, m_ada_w, m_ada_b, m_norm1_w, m_w_in, m_attn_sinks, m_sgu_ln_w, m_sgu_ln_b, m_sgu_w, m_sgu_b, m_proj_a, m_proj_b, m_w_out, m_norm2_w, m_ffn_w_gate, m_ffn_w_up, m_ffn_conv_w, m_ffn_conv_b, m_ffn_w_down, m_final_norm_w, v_ada_w, v_ada_b, v_norm1_w, v_w_in, v_attn_sinks, v_sgu_ln_w, v_sgu_ln_b, v_sgu_w, v_sgu_b, v_proj_a, v_proj_b, v_w_out, v_norm2_w, v_ffn_w_gate, v_ffn_w_up, v_ffn_conv_w, v_ffn_conv_b, v_ffn_w_down, v_final_norm_w):
    wts = dict(ada_w=ada_w, ada_b=ada_b, norm1_w=norm1_w, w_in=w_in, attn_sinks=attn_sinks, sgu_ln_w=sgu_ln_w,
               sgu_ln_b=sgu_ln_b, sgu_w=sgu_w, sgu_b=sgu_b, proj_a=proj_a, proj_b=proj_b, w_out=w_out, norm2_w=norm2_w,
               ffn_w_gate=ffn_w_gate, ffn_w_up=ffn_w_up, ffn_conv_w=ffn_conv_w, ffn_conv_b=ffn_conv_b,
               ffn_w_down=ffn_w_down, final_norm_w=final_norm_w)
    mom = dict(ada_w=m_ada_w, ada_b=m_ada_b, norm1_w=m_norm1_w, w_in=m_w_in, attn_sinks=m_attn_sinks, sgu_ln_w=m_sgu_ln_w,
               sgu_ln_b=m_sgu_ln_b, sgu_w=m_sgu_w, sgu_b=m_sgu_b, proj_a=m_proj_a, proj_b=m_proj_b, w_out=m_w_out,
               norm2_w=m_norm2_w, ffn_w_gate=m_ffn_w_gate, ffn_w_up=m_ffn_w_up, ffn_conv_w=m_ffn_conv_w,
               ffn_conv_b=m_ffn_conv_b, ffn_w_down=m_ffn_w_down, final_norm_w=m_final_norm_w)
    var = dict(ada_w=v_ada_w, ada_b=v_ada_b, norm1_w=v_norm1_w, w_in=v_w_in, attn_sinks=v_attn_sinks, sgu_ln_w=v_sgu_ln_w,
               sgu_ln_b=v_sgu_ln_b, sgu_w=v_sgu_w, sgu_b=v_sgu_b, proj_a=v_proj_a, proj_b=v_proj_b, w_out=v_w_out,
               norm2_w=v_norm2_w, ffn_w_gate=v_ffn_w_gate, ffn_w_up=v_ffn_w_up, ffn_conv_w=v_ffn_conv_w,
               ffn_conv_b=v_ffn_conv_b, ffn_w_down=v_ffn_w_down, final_norm_w=v_final_norm_w)
    me = 4 * lax.axis_index("x") + 2 * lax.axis_index("y") + lax.axis_index("c")
    ada_cols = ada_w.shape[2]

    c_all = _all_gather(jnp.broadcast_to(c, (8, D_MODEL)), name="ag_c")[:, 0, :]
    prod = _ada_fwd(c_all, ada_w)
    prod_all = _all_gather(prod, name="ag_mod")
    mine = lax.dynamic_index_in_dim(prod_all, me, axis=1, keepdims=False)
    mod = jnp.stack([mine[:, l * ada_cols:(l + 1) * ada_cols].reshape(-1) for l in range(DEPTH)]) + ada_b

    conv_cols = ffn_conv_w.shape[2]
    conv_all = _all_gather(_flat_pack([ffn_conv_w], 8), name="ag_conv", after=mod)
    conv_full = jnp.stack([a.reshape(DEPTH, 3, conv_cols) for a in
                           [conv_all[j].reshape(-1)[:DEPTH * 3 * conv_cols] for j in range(N_DEV)]], axis=2)
    conv_full = conv_full.reshape(DEPTH, 3, FFN_DIM)
    small = {n: wts[n] for n in SMALL}
    small["ffn_conv_w"] = conv_full

    mx, my, mc = _coords()
    cidx = jnp.reshape(mc, (1,)).astype(jnp.int32)
    chipidx = jnp.reshape(2 * mx + my, (1,)).astype(jnp.int32)
    rope = _rope_setup(positions[0])

    class Gather:
        def __init__(self, src, tag):
            self.tag, self.src = tag, src
            self.land = lax.dynamic_update_slice(lax.empty((2, 4) + src.shape, src.dtype), src[None, None],
                                                 (mc, 2 * mx + my, 0, 0))

        def ici_start(self, after):
            self.sems, (self.src, self.land), tok = _rdma_start([self.src, self.land], 3, _plan_gather_ici,
                                                                name=f"ag_{self.tag}_ici_start", after=after)
            return tok

        def ici_wait_d2d_start(self, after):
            _, land = _rdma_wait(self.sems, [self.src, self.land], 3, _plan_gather_ici, after, name=f"ag_{self.tag}_ici_wait")
            self.sems, (self.land,), tok = _rdma_start([land], 1, _plan_gather_d2d, name=f"ag_{self.tag}_d2d_start")
            return tok

        def d2d_wait(self, after):
            (land,) = _rdma_wait(self.sems, [self.land], 1, _plan_gather_d2d, after, name=f"ag_{self.tag}_d2d_wait")
            return _from_land(land)

    def weights_job(names, l, tag):
        job = Gather(_pack_shards(wts, l, names), tag)
        job.weights = lambda after: _unpack_weights(job.d2d_wait(after), names)
        return job

    W0 = _unpack_weights(_all_gather(_pack_shards(wts, 0, PART_IN), name="ag_w0_in", after=conv_all), PART_IN)
    W1 = {}
    rest = PART_MIX + PART_FFN
    g_rest0 = weights_job(rest, 0, "w0_rest")
    g_in1, g_rest1 = weights_job(PART_IN, 1, "w1_in"), weights_job(rest, 1, "w1_rest")

    def rest0_then_layer1(after):
        W0.update(g_rest0.weights(after))
        return g_rest1.ici_start(g_in1.ici_start(W0["w_down"]))

    x1, sv0 = _layer_fwd(0, x[0], mod[0], W0, small, rope,
                         {"mm_in": lambda after: g_rest0.ici_start(W0["wt_in"]), "sgu": g_rest0.ici_wait_d2d_start,
                          "mm_pa": rest0_then_layer1, "mm_gu": g_in1.ici_wait_d2d_start})
    g_rest1.ici_wait_d2d_start(x1)
    x2, sv1 = _layer_fwd(1, x1, mod[1], W1, small, rope,
                         {"mm_in": lambda after: W1.update(g_in1.weights(after)),
                          "mm_pa": lambda after: W1.update(g_rest1.weights(after))})
    gate2 = [mod[l][5 * D_MODEL:][None, :] for l in range(DEPTH)]
    dx2, dfw, loss_tile, do2, dg2 = _head(x2, final_norm_w[None, :], loss_target[0], (sv1["o2"], gate2[1]))
    loss = lax.psum(loss_tile[0, 0], ("x", "y", "c"))

    class Reduce:
        def __init__(self, names, tag):
            self.names, self.tag, self.rows = names, tag, _part_rows(names)

        def d2d_start(self, wg, after=None):
            self.sems, self.bufs, tok = _rdma_start([_pack_grads(wg, self.names), lax.empty((4, self.rows, 1024), BF)], 1,
                                                    _plan_reduce_d2d, name=f"rs_{self.tag}_d2d_start", after=after)
            return tok

        def d2d_wait_ici_start(self, after):
            g_t, land_a = _rdma_wait(self.sems, self.bufs, 1, _plan_reduce_d2d, after, name=f"rs_{self.tag}_d2d_wait")
            h = _sum_pair(g_t, land_a, cidx, name=f"rs_{self.tag}_sum_pair")
            self.sems, self.bufs, tok = _rdma_start([h, lax.empty((3, self.rows, 1024), BF)], 3, _plan_reduce_ici,
                                                    name=f"rs_{self.tag}_ici_start")
            return tok

        def ici_wait(self, after):
            h_t, land_b = _rdma_wait(self.sems, self.bufs, 3, _plan_reduce_ici, after, name=f"rs_{self.tag}_ici_wait")
            return _unpack_shard_grads(_sum_chips(h_t, land_b, chipidx, name=f"rs_{self.tag}_sum_chips"), self.names)

    (dx1, do2, dg2), wg1, sg1, dmod1 = _layer_bwd(1, dx2, do2, dg2, mod[1], W1, small, rope, sv1, below=(sv0["o2"], gate2[0]))
    r_all1, r_ffn0, r_mix0 = Reduce(BIG, "g1"), Reduce(PART_FFN, "g0_ffn"), Reduce(PART_IN + PART_MIX, "g0_mix")
    tok1 = r_all1.d2d_start(wg1)
    wg0, shard1 = {}, {}

    def layer1_done_then_mix0(after):
        shard1.update(r_all1.ici_wait(after))
        return r_mix0.d2d_wait_ici_start(r_mix0.d2d_start(wg0, shard1["w_in"]))

    (grad_x,), _, sg0, dmod0 = _layer_bwd(
        0, dx1, do2, dg2, mod[0], W0, small, rope, sv0, wg=wg0,
        hooks={"mm_down_dx": lambda after: tok1, "mm_gu_dx": r_all1.d2d_wait_ici_start,
               "merge_bwd": lambda after: r_ffn0.d2d_start(wg0, after), "mm_pa_dx": r_ffn0.d2d_wait_ici_start,
               "mm_in_dx": layer1_done_then_mix0})
    sg = {n: jnp.stack([sg0[n], sg1[n]]) for n in sg0}
    sg["final_norm_w"] = dfw[0]
    dmod = jnp.stack([dmod0, dmod1])
    vec_names = [n for n in SMALL if n not in ("ada_b", "sgu_w")] + ["ffn_conv_w"]
    vec_shapes = [(DEPTH, 6 * D_MODEL)] + [sg[n].shape for n in vec_names]
    vec_rows = -(-sum(int(np.prod(s)) for s in vec_shapes) // 1024 // 16) * 16
    sgu_rows = sgu_w.size // 1024
    g_small = Gather(jnp.concatenate([_flat_pack([dmod] + [sg[n] for n in vec_names], vec_rows),
                                      sg["sgu_w"].reshape(sgu_rows, 1024)], axis=0).astype(BF), "small")
    tok = g_small.ici_start(grad_x)

    shard0 = r_ffn0.ici_wait(tok)
    shard0.update(r_mix0.ici_wait(shard0["ffn_w_down"]))
    grads, delta, new_m, new_v = {}, {}, {}, {}
    for n in BIG:
        two = lambda a: a.reshape(DEPTH, -1, a.shape[-1])
        out = _adamw_layers(two(wts[n]), [shard0[n], shard1[n]], two(mom[n]), two(var[n]), name=f"adamw_{n}")
        grads[n], delta[n], new_m[n], new_v[n] = [o.reshape(wts[n].shape) for o in out]

    sm_all = g_small.d2d_wait(g_small.ici_wait_d2d_start(delta["ffn_w_gate"]))
    sm_sum = _sum8(sm_all, name="sum_small")
    vec_sum = _flat_unpack(sm_sum[:vec_rows], vec_shapes)
    grads["ada_b"] = vec_sum[0]
    for n, gsum in zip(vec_names, vec_sum[1:]):
        grads[n] = gsum
    grads["sgu_w"] = sm_sum[vec_rows:].reshape(sgu_w.shape)
    grads["ffn_conv_w"] = lax.dynamic_slice_in_dim(grads["ffn_conv_w"], me * conv_cols, conv_cols, axis=2)
    dmod_all = sm_all[:, :DEPTH * 6, :].astype(F32).reshape(N_DEV, DEPTH, 6 * D_MODEL)
    dm_mine = lax.dynamic_slice_in_dim(dmod_all, me * ada_cols, ada_cols, axis=2).transpose(1, 0, 2)
    dm_mine = jnp.pad(dm_mine, ((0, 0), (0, 8), (0, 0)))
    grads["ada_w"] = _ada_bwd(jnp.pad(c_all, ((0, 8), (0, 0))), dm_mine)

    packed_small = [n for n in SMALL if n != "sgu_w"]
    pshapes = [wts[n].shape for n in packed_small]
    prow = -(-sum(int(np.prod(s)) for s in pshapes) // 1024 // 8) * 8
    pk = lambda d: _flat_pack([d[n] for n in packed_small], prow)
    d_s, m_s, v_s = _adamw(pk(wts), pk(grads), pk(mom), pk(var), name="adamw_small")
    for n, dd, mm, vv in zip(packed_small, _flat_unpack(d_s, pshapes), _flat_unpack(m_s, pshapes), _flat_unpack(v_s, pshapes)):
        delta[n], new_m[n], new_v[n] = dd, mm, vv
    for n in WEIGHT_ORDER:
        if n not in delta:
            delta[n], new_m[n], new_v[n] = _adam2d(wts[n], grads[n], mom[n], var[n], name=f"adamw_{n}")
    return (loss, grad_x[None], *[grads[n] for n in WEIGHT_ORDER], *[delta[n] for n in WEIGHT_ORDER],
            *[new_m[n] for n in WEIGHT_ORDER], *[new_v[n] for n in WEIGHT_ORDER])
```

```python
import jax
import jax.numpy as jnp
import numpy as np
from jax import lax
from jax.experimental import pallas as pl
from jax.experimental.pallas import tpu as pltpu

F32 = jnp.float32
BF = jnp.bfloat16

N_DEV = 8
D_MODEL = 1024
DEPTH = 2
N_Q_HEADS = 16
N_KV_HEADS = 2
HEAD_DIM = 64
Q_PER_KV = N_Q_HEADS // N_KV_HEADS
ATTN_BLOCK = 128
ROPE_THETA = 500000.0
ROT_DIM = HEAD_DIM // 4
SGU_WIDTH = 1024
SGU_GROUPS = 8
SGU_CHUNK = 128
FFN_DIM = 2816
NORM_EPS = 1e-6
Q_END = N_Q_HEADS * HEAD_DIM
K_END = Q_END + N_KV_HEADS * HEAD_DIM
V_END = K_END + N_KV_HEADS * HEAD_DIM
Z_END = V_END + 2 * SGU_WIDTH
IN_COLS = Z_END + 2 * D_MODEL
P_Z, P_G, P_Q, P_K, P_V = 0, 2048, 4096, 5120, 5248

ADAM_LR = 0.001
ADAM_B1 = 0.9
ADAM_B2 = 0.999
ADAM_EPS = 1e-08
ADAM_WD = 0.01
ADAM_STEP = 10

VMEM_LIMIT_BYTES = 56 * 1024 * 1024

BIG = ("w_in", "proj_a", "proj_b", "w_out", "ffn_w_gate", "ffn_w_up", "ffn_w_down")
COL_SHARDED = ("w_in", "ffn_w_gate", "ffn_w_up")
BIG_SHAPE = {"w_in": (D_MODEL, IN_COLS), "proj_a": (SGU_WIDTH, D_MODEL), "proj_b": (Q_END, D_MODEL),
             "w_out": (D_MODEL, D_MODEL), "ffn_w_gate": (D_MODEL, FFN_DIM), "ffn_w_up": (D_MODEL, FFN_DIM),
             "ffn_w_down": (FFN_DIM, D_MODEL)}
BIG_ROWS = {n: BIG_SHAPE[n][0] * BIG_SHAPE[n][1] // N_DEV // 1024 for n in BIG}


def _pcall(body, **kw):
    return pl.pallas_call(body, **kw)


def _params(**kw):
    return pltpu.CompilerParams(vmem_limit_bytes=VMEM_LIMIT_BYTES, **kw)


def _tile(n, cap, unit=128):
    if n <= cap:
        return n
    best = 0
    t = unit
    while t <= cap:
        if n % t == 0:
            best = t
        t += unit
    assert best, (n, cap, unit)
    return best


def _mm(a, b, *, nt, out_dtype, name, res=None, gvec=None, after=None, tm=None, tn_cap=1024):
    a_list = list(a) if isinstance(a, (list, tuple)) else [a]
    b_list = list(b) if isinstance(b, (list, tuple)) else [b]
    a, b = a_list[0], b_list[0]
    M, K = a.shape
    N = b.shape[0] if nt else b.shape[1]
    k_total = sum(x.shape[1] for x in a_list)
    tm = _tile(M, tm or (1024 if k_total <= 1024 else 512), 8)
    tn = _tile(N, tn_cap)
    dn = (((1,), (1,)), ((), ())) if nt else (((1,), (0,)), ((), ()))

    def b_spec_of(x):
        k = x.shape[1] if nt else x.shape[0]
        return pl.BlockSpec((tn, k), lambda i, j: (j, 0)) if nt else pl.BlockSpec((k, tn), lambda i, j: (0, j))
    b_spec = b_spec_of(b)
    o_spec = pl.BlockSpec((tm, tn), lambda i, j: (i, j))
    if res is None:
        extra = [] if after is None else [after]
        n = len(a_list)

        def body(*refs):
            o_ref = refs[-1]
            acc = None
            for a_ref, b_ref in zip(refs[:n], refs[n:2 * n]):
                d = lax.dot_general(a_ref[...].astype(BF), b_ref[...].astype(BF), dn, preferred_element_type=F32)
                acc = d if acc is None else acc + d
            o_ref[...] = acc.astype(out_dtype)
        return _pcall(body, name=name, grid=(M // tm, N // tn),
                      in_specs=[pl.BlockSpec((tm, x.shape[1]), lambda i, j: (i, 0)) for x in a_list]
                      + [b_spec_of(x) for x in b_list] + [ANY] * len(extra), out_specs=o_spec,
                      out_shape=jax.ShapeDtypeStruct((M, N), out_dtype), compiler_params=_params())(
                          *a_list, *b_list, *extra)

    def body_res(a_ref, b_ref, r_ref, g_ref, o_ref, acc_ref):
        acc = lax.dot_general(a_ref[...].astype(BF), b_ref[...].astype(BF), dn, preferred_element_type=F32)
        acc_ref[...] = acc.astype(BF)
        o_ref[...] = r_ref[...] + g_ref[...] * acc
    return _pcall(body_res, name=name, grid=(M // tm, N // tn),
                  in_specs=[pl.BlockSpec((tm, K), lambda i, j: (i, 0)), b_spec, o_spec,
                            pl.BlockSpec((1, tn), lambda i, j: (0, j))],
                  out_specs=[o_spec, o_spec],
                  out_shape=[jax.ShapeDtypeStruct((M, N), F32), jax.ShapeDtypeStruct((M, N), BF)],
                  compiler_params=_params())(a, b, res, gvec)


def _mm_tn(a, b, *, name, out_dtype=BF, tk=2048, tm_cap=1408, tn_cap=1024):
    S, M = a.shape
    N = b.shape[1]
    tm = _tile(M, tm_cap)
    tn = _tile(N, tn_cap)
    if 2 * 2 * S * (tm + tn) <= VMEM_LIMIT_BYTES * 3 // 5:
        tk = S
    tk = _tile(S, tk, 8)
    nk = S // tk

    def body(a_ref, b_ref, o_ref, acc_ref):
        k = pl.program_id(2)

        @pl.when(k == 0)
        def _():
            acc_ref[...] = jnp.zeros_like(acc_ref)
        acc_ref[...] += lax.dot_general(a_ref[...].astype(BF), b_ref[...].astype(BF), (((0,), (0,)), ((), ())),
                                        preferred_element_type=F32)

        @pl.when(k == nk - 1)
        def _():
            o_ref[...] = acc_ref[...].astype(out_dtype)
    return _pcall(body, name=name, grid=(M // tm, N // tn, nk),
                  in_specs=[pl.BlockSpec((tk, tm), lambda i, j, k: (k, i)),
                            pl.BlockSpec((tk, tn), lambda i, j, k: (k, j))],
                  out_specs=pl.BlockSpec((tm, tn), lambda i, j, k: (i, j)),
                  out_shape=jax.ShapeDtypeStruct((M, N), out_dtype), scratch_shapes=[pltpu.VMEM((tm, tn), F32)],
                  compiler_params=_params())(a, b)


def _rms(x, w):
    return x * lax.rsqrt(jnp.mean(x * x, axis=-1, keepdims=True) + NORM_EPS) * w


def _normmod_fn(x, nw, sc, sh):
    return _rms(x, nw) * (1.0 + sc) + sh


def _gelu(x):
    return 0.5 * x * (1.0 + lax.erf(x * (2.0 ** -0.5)))


def _ln_gelu_fn(zv, w, b):
    v = _gelu(zv)
    mu = jnp.mean(v, axis=-1, keepdims=True)
    var = jnp.mean(jnp.square(v - mu), axis=-1, keepdims=True)
    return (v - mu) * lax.rsqrt(var + NORM_EPS) * w + b


def _sigmoid(x):
    return 1.0 / (1.0 + jnp.exp(-x))


def _row_spec(tm, n):
    return pl.BlockSpec((tm, n), lambda i: (i, 0))


def _vec_spec(n):
    return pl.BlockSpec((1, n), lambda i: (0, 0))


def _acc(ref, val):
    @pl.when(pl.program_id(0) == 0)
    def _():
        ref[...] = jnp.zeros_like(ref)
    ref[...] += val


def _norm_mm(x, nw, sc, sh, ws, *, name, after=None, tm=1024, tn_cap=768):
    S, K = x.shape
    N = ws[0].shape[0]
    tm = _tile(S, tm, 8)
    tn = _tile(N, tn_cap)
    nw_, ne = len(ws), 0 if after is None else 1

    def body(x_ref, nw_ref, sc_ref, sh_ref, *rest):
        w_refs = rest[:nw_]
        h_ref = rest[nw_ + ne]
        o_refs = rest[nw_ + ne + 1:nw_ + ne + 1 + nw_]
        h_s = rest[-1]

        @pl.when(pl.program_id(1) == 0)
        def _():
            hv = _normmod_fn(x_ref[...], nw_ref[...], sc_ref[...], sh_ref[...]).astype(BF)
            h_s[...] = hv
            h_ref[...] = hv
        for w_ref, o_ref in zip(w_refs, o_refs):
            o_ref[...] = lax.dot_general(h_s[...], w_ref[...], (((1,), (1,)), ((), ())),
                                         preferred_element_type=F32).astype(BF)
    row = pl.BlockSpec((tm, K), lambda i, j: (i, 0))
    vec = pl.BlockSpec((1, K), lambda i, j: (0, 0))
    out = pl.BlockSpec((tm, tn), lambda i, j: (i, j))
    res = _pcall(body, name=name, grid=(S // tm, N // tn),
                 in_specs=[row, vec, vec, vec] + [pl.BlockSpec((tn, K), lambda i, j: (j, 0))] * nw_ + [ANY] * ne,
                 out_specs=[row] + [out] * nw_,
                 out_shape=[jax.ShapeDtypeStruct((S, K), BF)] + [jax.ShapeDtypeStruct((S, N), BF)] * nw_,
                 scratch_shapes=[pltpu.VMEM((tm, K), BF)], compiler_params=_params())(
                     x, nw, sc, sh, *ws, *([] if after is None else [after]))
    return res[0], list(res[1:])


def _gate_bwd(dxv, o_ref, g_ref, do_ref, dg_ref):
    do_ref[...] = (dxv * g_ref[...]).astype(BF)
    _acc(dg_ref, jnp.sum(dxv * o_ref[...].astype(F32), axis=0, keepdims=True))


def _normmod_bwd(dh, x, nw, sc, sh, dres, gate, *, name, tm=512):
    S, Dm = x.shape
    tm = _tile(S, tm, 8)
    ng = 0 if gate is None else 2

    def body(dh_ref, x_ref, nw_ref, sc_ref, sh_ref, dres_ref, *rest):
        dx_ref, dnw_ref, dsc_ref, dsh_ref = rest[ng:ng + 4]
        xv, dy = x_ref[...], dh_ref[...]
        r = lax.rsqrt(jnp.mean(xv * xv, axis=-1, keepdims=True) + NORM_EPS)
        xn = xv * r
        t = dy * xn
        a = nw_ref[...] * (1.0 + sc_ref[...])
        dxv = dres_ref[...] + r * (dy * a - xn * jnp.mean(t * a, axis=-1, keepdims=True))
        dx_ref[...] = dxv
        ts = jnp.sum(t, axis=0, keepdims=True)
        _acc(dnw_ref, ts * (1.0 + sc_ref[...]))
        _acc(dsc_ref, ts * nw_ref[...])
        _acc(dsh_ref, jnp.sum(dy, axis=0, keepdims=True))
        if gate is not None:
            _gate_bwd(dxv, rest[0], rest[1], rest[ng + 4], rest[ng + 5])
    vec = jax.ShapeDtypeStruct((1, Dm), F32)
    gate_in = [] if gate is None else [_row_spec(tm, Dm), _vec_spec(Dm)]
    gate_out = [] if gate is None else [_row_spec(tm, Dm), _vec_spec(Dm)]
    gate_shape = [] if gate is None else [jax.ShapeDtypeStruct((S, Dm), BF), vec]
    return _pcall(body, name=name, grid=(S // tm,),
                  in_specs=[_row_spec(tm, Dm), _row_spec(tm, Dm), _vec_spec(Dm), _vec_spec(Dm), _vec_spec(Dm),
                            _row_spec(tm, Dm)] + gate_in,
                  out_specs=[_row_spec(tm, Dm), _vec_spec(Dm), _vec_spec(Dm), _vec_spec(Dm)] + gate_out,
                  out_shape=[jax.ShapeDtypeStruct((S, Dm), F32), vec, vec, vec] + gate_shape,
                  compiler_params=_params())(dh, x, nw, sc, sh, dres, *([] if gate is None else gate))


def _head(x, fw, target, gate, *, tm=512):
    S, Dm = x.shape
    tm = _tile(S, tm, 8)

    def body(x_ref, fw_ref, t_ref, o_ref, g_ref, dx_ref, dfw_ref, loss_ref, do_ref, dg_ref):
        xv, w = x_ref[...], fw_ref[...]
        r = lax.rsqrt(jnp.mean(xv * xv, axis=-1, keepdims=True) + NORM_EPS)
        xn = xv * r
        err = xn * w - t_ref[...]
        dy = err * (1.0 / Dm)
        t = dy * xn
        dx = r * (dy * w - xn * jnp.mean(t * w, axis=-1, keepdims=True))
        dx_ref[...] = dx
        _acc(dfw_ref, jnp.sum(t, axis=0, keepdims=True))
        part = 0.5 * jnp.sum(jnp.mean(err * err, axis=-1, keepdims=True), axis=0, keepdims=True)
        _acc(loss_ref, jnp.broadcast_to(part, (8, 128)))
        _gate_bwd(dx, o_ref, g_ref, do_ref, dg_ref)
    vec = jax.ShapeDtypeStruct((1, Dm), F32)
    return _pcall(body, name="head", grid=(S // tm,),
                  in_specs=[_row_spec(tm, Dm), _vec_spec(Dm), _row_spec(tm, Dm), _row_spec(tm, Dm), _vec_spec(Dm)],
                  out_specs=[_row_spec(tm, Dm), _vec_spec(Dm), pl.BlockSpec((8, 128), lambda i: (0, 0)),
                             _row_spec(tm, Dm), _vec_spec(Dm)],
                  out_shape=[jax.ShapeDtypeStruct((S, Dm), F32), vec, jax.ShapeDtypeStruct((8, 128), F32),
                             jax.ShapeDtypeStruct((S, Dm), BF), vec],
                  compiler_params=_params())(x, fw, target, *gate)


def _tril_mask():
    r = lax.broadcasted_iota(jnp.int32, (SGU_CHUNK, SGU_CHUNK), 0)
    c = lax.broadcasted_iota(jnp.int32, (SGU_CHUNK, SGU_CHUNK), 1)
    return c <= r


def _sgu_fwd(proj, lnw, lnb, w, b_t, *, name, after=None, tm=512):
    S = proj.shape[0]
    tm = _tile(S, tm, SGU_CHUNK)
    extra = [] if after is None else [after]

    def body(zu_ref, zv_ref, lnw_ref, lnb_ref, w_ref, bt_ref, *rest):
        o_ref = rest[-1]
        u = _gelu(zu_ref[...].astype(F32))
        vn = _ln_gelu_fn(zv_ref[...].astype(F32), lnw_ref[...], lnb_ref[...]).astype(BF)
        mask = _tril_mask()
        for g in range(SGU_GROUPS):
            wm = jnp.where(mask, w_ref[g], 0.0).astype(BF)
            cols = slice(g * 128, (g + 1) * 128)
            for ci in range(tm // SGU_CHUNK):
                rows = slice(ci * SGU_CHUNK, (ci + 1) * SGU_CHUNK)
                f = jnp.dot(wm, vn[rows, cols], preferred_element_type=F32) + bt_ref[:, g:g + 1]
                o_ref[rows, cols] = (u[rows, cols] * f).astype(BF)
    return _pcall(body, name=name, grid=(S // tm,),
                  in_specs=[pl.BlockSpec((tm, SGU_WIDTH), lambda i: (i, 0)), pl.BlockSpec((tm, SGU_WIDTH), lambda i: (i, 1)),
                            _vec_spec(SGU_WIDTH), _vec_spec(SGU_WIDTH),
                            pl.BlockSpec((SGU_GROUPS, 128, 128), lambda i: (0, 0, 0)),
                            pl.BlockSpec((128, SGU_GROUPS), lambda i: (0, 0))] + [ANY] * len(extra),
                  out_specs=_row_spec(tm, SGU_WIDTH), out_shape=jax.ShapeDtypeStruct((S, SGU_WIDTH), BF),
                  compiler_params=_params())(proj, proj, lnw, lnb, w, b_t, *extra)


def _sgu_bwd(dy, proj, lnw, lnb, w, b_t, dproj, *, name, tm=512):
    S = proj.shape[0]
    tm = _tile(S, tm, SGU_CHUNK)

    def body(dy_ref, zu_ref, zv_ref, lnw_ref, lnb_ref, w_ref, bt_ref, _, dz_ref, dlnw_ref, dlnb_ref, dw_ref, dbt_ref,
             f_s, dvn_s):
        first = pl.program_id(0) == 0

        @pl.when(first)
        def _():
            dw_ref[...] = jnp.zeros_like(dw_ref)
            dbt_ref[...] = jnp.zeros_like(dbt_ref)
        u, vjp_u = jax.vjp(_gelu, zu_ref[...].astype(F32))
        vn, vjp_v = jax.vjp(_ln_gelu_fn, zv_ref[...].astype(F32), lnw_ref[...], lnb_ref[...])
        vn = vn.astype(BF)
        dy_v = dy_ref[...]
        df = (dy_v * u).astype(BF)
        mask = _tril_mask()
        for g in range(SGU_GROUPS):
            wm = jnp.where(mask, w_ref[g], 0.0).astype(BF)
            cols = slice(g * 128, (g + 1) * 128)
            dwg = jnp.zeros((128, 128), F32)
            dbg = jnp.zeros((128, 1), F32)
            for ci in range(tm // SGU_CHUNK):
                rows = slice(ci * SGU_CHUNK, (ci + 1) * SGU_CHUNK)
                vn_c = vn[rows, cols]
                df_c = df[rows, cols]
                f_s[rows, cols] = jnp.dot(wm, vn_c, preferred_element_type=F32) + bt_ref[:, g:g + 1]
                dvn_s[rows, cols] = lax.dot_general(wm, df_c, (((0,), (0,)), ((), ())), preferred_element_type=F32)
                dwg = dwg + lax.dot_general(df_c, vn_c, (((1,), (1,)), ((), ())), preferred_element_type=F32)
                dbg = dbg + jnp.sum((dy_v[rows, cols] * u[rows, cols]), axis=1, keepdims=True)
            dw_ref[g] += jnp.where(mask, dwg, 0.0)
            dbt_ref[:, g:g + 1] += dbg
        (dzu,) = vjp_u(dy_v * f_s[...])
        dzv, dlnw, dlnb = vjp_v(dvn_s[...])
        dz_ref[:, :SGU_WIDTH] = dzu.astype(BF)
        dz_ref[:, SGU_WIDTH:] = dzv.astype(BF)
        _acc(dlnw_ref, dlnw)
        _acc(dlnb_ref, dlnb)
    vec = jax.ShapeDtypeStruct((1, SGU_WIDTH), F32)
    return _pcall(body, name=name, grid=(S // tm,),
                  in_specs=[_row_spec(tm, SGU_WIDTH),
                            pl.BlockSpec((tm, SGU_WIDTH), lambda i: (i, 0)), pl.BlockSpec((tm, SGU_WIDTH), lambda i: (i, 1)),
                            _vec_spec(SGU_WIDTH), _vec_spec(SGU_WIDTH),
                            pl.BlockSpec((SGU_GROUPS, 128, 128), lambda i: (0, 0, 0)),
                            pl.BlockSpec((128, SGU_GROUPS), lambda i: (0, 0)), ANY],
                  out_specs=[pl.BlockSpec((tm, 2 * SGU_WIDTH), lambda i: (i, P_Z // (2 * SGU_WIDTH))),
                             _vec_spec(SGU_WIDTH), _vec_spec(SGU_WIDTH),
                             pl.BlockSpec((SGU_GROUPS, 128, 128), lambda i: (0, 0, 0)),
                             pl.BlockSpec((128, SGU_GROUPS), lambda i: (0, 0))],
                  out_shape=[jax.ShapeDtypeStruct(dproj.shape, BF), vec, vec,
                             jax.ShapeDtypeStruct((SGU_GROUPS, 128, 128), F32),
                             jax.ShapeDtypeStruct((128, SGU_GROUPS), F32)],
                  scratch_shapes=[pltpu.VMEM((tm, SGU_WIDTH), F32), pltpu.VMEM((tm, SGU_WIDTH), F32)],
                  input_output_aliases={7: 0},
                  compiler_params=_params())(dy, proj, proj, lnw, lnb, w, b_t, dproj)


def _merge_fwd(y_sgu, y_attn, pa, pb, proj, *, name, after=None, tm=1024, tn=512):
    S, Dm = y_sgu.shape
    tm = _tile(S, tm, 8)
    nj = Dm // tn
    extra = [] if after is None else [after]

    def body(ys_ref, ya_ref, pa_ref, pb_ref, ga_ref, gb_ref, *rest):
        a_ref, b_ref, m_ref = rest[-3:]
        a = jnp.dot(ys_ref[...], pa_ref[...], preferred_element_type=F32)
        b = jnp.dot(ya_ref[...], pb_ref[...], preferred_element_type=F32)
        a_ref[...] = a.astype(BF)
        b_ref[...] = b.astype(BF)
        m_ref[...] = (_sigmoid(ga_ref[...].astype(F32)) * a + _sigmoid(gb_ref[...].astype(F32)) * b).astype(BF)
    row = pl.BlockSpec((tm, Dm), lambda i, j: (i, 0))
    col = pl.BlockSpec((Dm, tn), lambda i, j: (0, j))
    out = pl.BlockSpec((tm, tn), lambda i, j: (i, j))
    sh = jax.ShapeDtypeStruct((S, Dm), BF)
    return _pcall(body, name=name, grid=(S // tm, nj),
                  in_specs=[row, row, col, col, pl.BlockSpec((tm, tn), lambda i, j: (i, P_G // tn + j)),
                            pl.BlockSpec((tm, tn), lambda i, j: (i, (P_G + Dm) // tn + j))] + [ANY] * len(extra),
                  out_specs=[out, out, out], out_shape=[sh, sh, sh],
                  compiler_params=_params())(y_sgu, y_attn, pa, pb, proj, proj, *extra)


def _merge_bwd(do, w_out, a, b, proj, *, name, after=None, tm=512):
    S, Dm = a.shape
    tm = _tile(S, tm, 8)
    ga_blk, gb_blk = P_G // Dm, P_G // Dm + 1
    extra = [] if after is None else [after]

    def body(do_ref, w_ref, a_ref, b_ref, ga_ref, gb_ref, *rest):
        da_ref, db_ref, dg_ref = rest[-3:]
        dmv = lax.dot_general(do_ref[...], w_ref[...], (((1,), (1,)), ((), ())), preferred_element_type=F32)
        sa = _sigmoid(ga_ref[...].astype(F32))
        sb = _sigmoid(gb_ref[...].astype(F32))
        da_ref[...] = (dmv * sa).astype(BF)
        db_ref[...] = (dmv * sb).astype(BF)
        dg_ref[:, :Dm] = (dmv * a_ref[...].astype(F32) * sa * (1.0 - sa)).astype(BF)
        dg_ref[:, Dm:] = (dmv * b_ref[...].astype(F32) * sb * (1.0 - sb)).astype(BF)
    return _pcall(body, name=name, grid=(S // tm,),
                  in_specs=[_row_spec(tm, Dm), pl.BlockSpec((Dm, Dm), lambda i: (0, 0)), _row_spec(tm, Dm), _row_spec(tm, Dm),
                            pl.BlockSpec((tm, Dm), lambda i: (i, ga_blk)), pl.BlockSpec((tm, Dm), lambda i: (i, gb_blk))]
                  + [ANY] * len(extra),
                  out_specs=[_row_spec(tm, Dm), _row_spec(tm, Dm), pl.BlockSpec((tm, 2 * Dm), lambda i: (i, P_G // (2 * Dm)))],
                  out_shape=[jax.ShapeDtypeStruct((S, Dm), BF), jax.ShapeDtypeStruct((S, Dm), BF),
                             jax.ShapeDtypeStruct((S, IN_COLS), BF)],
                  compiler_params=_params())(do, w_out, a, b, proj, proj, *extra)


def _shift_rows(a, halo, k, up):
    n = a.shape[0]
    r8 = lax.broadcasted_iota(jnp.int32, (8, a.shape[1]), 0)
    if not up:
        rolled = pltpu.roll(a, k, 0)
        patch = jnp.where(r8 < k, pltpu.roll(halo, k, 0), rolled[:8])
        return jnp.concatenate([patch, rolled[8:]], axis=0)
    rolled = pltpu.roll(a, n - k, 0)
    patch = jnp.where(r8 >= 8 - k, pltpu.roll(halo, 8 - k, 0), rolled[n - 8:])
    return jnp.concatenate([rolled[:n - 8], patch], axis=0)


def _conv_taps(a, halo):
    return _shift_rows(a, halo, 2, False), _shift_rows(a, halo, 1, False), a


HALO = 16


def _prev_halo_spec(tm, Fd):
    return pl.BlockSpec((HALO, Fd), lambda i: (jnp.maximum(i * (tm // HALO) - 1, 0), 0))


def _conv_fwd(a_ref, halo_ref, cw_ref, cb_ref):
    halo = jnp.where(pl.program_id(0) > 0, halo_ref[...].astype(F32)[HALO - 8:], 0.0)
    t0, t1, t2 = _conv_taps(a_ref[...].astype(F32), halo)
    return t0, t1, t2, cb_ref[...] + cw_ref[0:1, :] * t0 + cw_ref[1:2, :] * t1 + cw_ref[2:3, :] * t2


def _ffn_act_fwd(a, up, cw, cb, *, name, tm=256):
    S, Fd = a.shape
    tm = _tile(S, tm, HALO)

    def body(a_ref, up_ref, halo_ref, cw_ref, cb_ref, o_ref, ac_ref):
        _, _, _, ac = _conv_fwd(a_ref, halo_ref, cw_ref, cb_ref)
        ac_ref[...] = ac.astype(BF)
        o_ref[...] = (ac * _sigmoid(ac) * up_ref[...].astype(F32)).astype(BF)
    sh = jax.ShapeDtypeStruct((S, Fd), BF)
    return _pcall(body, name=name, grid=(S // tm,),
                  in_specs=[_row_spec(tm, Fd), _row_spec(tm, Fd), _prev_halo_spec(tm, Fd),
                            pl.BlockSpec((3, Fd), lambda i: (0, 0)), _vec_spec(Fd)],
                  out_specs=[_row_spec(tm, Fd), _row_spec(tm, Fd)], out_shape=[sh, sh],
                  compiler_params=_params())(a, up, a, cw, cb)


def _ffn_act_bwd_a(dhf, ac, up, *, name, tm=512):
    S, Fd = ac.shape
    tm = _tile(S, tm, HALO)

    def body(dhf_ref, ac_ref, up_ref, dac_ref, dup_ref, dcb_ref):
        acv = ac_ref[...].astype(F32)
        s = _sigmoid(acv)
        dhf_v = dhf_ref[...].astype(F32)
        dup_ref[...] = (dhf_v * acv * s).astype(BF)
        dac = dhf_v * up_ref[...].astype(F32) * (s * (1.0 + acv * (1.0 - s)))
        dac_ref[...] = dac.astype(BF)
        _acc(dcb_ref, jnp.sum(dac, axis=0, keepdims=True))
    sh = jax.ShapeDtypeStruct((S, Fd), BF)
    return _pcall(body, name=name, grid=(S // tm,), in_specs=[_row_spec(tm, Fd)] * 3,
                  out_specs=[_row_spec(tm, Fd), _row_spec(tm, Fd), _vec_spec(Fd)],
                  out_shape=[sh, sh, jax.ShapeDtypeStruct((1, Fd), F32)], compiler_params=_params())(dhf, ac, up)


def _ffn_act_bwd_b(dac, a, cw, *, name, tm=256):
    S, Fd = dac.shape
    tm = _tile(S, tm, HALO)
    last = S // tm - 1

    def body(d_ref, halo_ref, a_ref, cw_ref, o_ref, dcw_ref):
        halo = jnp.where(pl.program_id(0) < last, halo_ref[...].astype(F32)[:8], 0.0)
        d = d_ref[...].astype(F32)
        d1, d2 = _shift_rows(d, halo, 1, True), _shift_rows(d, halo, 2, True)
        o_ref[...] = (cw_ref[2:3, :] * d + cw_ref[1:2, :] * d1 + cw_ref[0:1, :] * d2).astype(BF)
        av = a_ref[...].astype(F32)
        _acc(dcw_ref, jnp.concatenate([jnp.sum(av * d2, axis=0, keepdims=True),
                                       jnp.sum(av * d1, axis=0, keepdims=True),
                                       jnp.sum(av * d, axis=0, keepdims=True)], axis=0))
    return _pcall(body, name=name, grid=(S // tm,),
                  in_specs=[_row_spec(tm, Fd),
                            pl.BlockSpec((HALO, Fd), lambda i: (jnp.minimum((i + 1) * (tm // HALO), S // HALO - 1), 0)),
                            _row_spec(tm, Fd), pl.BlockSpec((3, Fd), lambda i: (0, 0))],
                  out_specs=[_row_spec(tm, Fd), pl.BlockSpec((3, Fd), lambda i: (0, 0))],
                  out_shape=[jax.ShapeDtypeStruct((S, Fd), BF), jax.ShapeDtypeStruct((3, Fd), F32)],
                  compiler_params=_params())(dac, dac, a, cw)


def _rope_tables(pos_col, inv_row, m1_row, m2_row):
    S = pos_col.shape[0]
    tm = _tile(S, 512, 8)

    def body(p_ref, inv_ref, m1_ref, m2_ref, c_ref, s1_ref, s2_ref):
        ang = p_ref[...] * inv_ref[...]
        sn = jnp.sin(ang)
        c_ref[...] = jnp.cos(ang)
        s1_ref[...] = -sn * m1_ref[...]
        s2_ref[...] = sn * m2_ref[...]
    sh = jax.ShapeDtypeStruct((S, 128), F32)
    return _pcall(body, name="rope_tables", grid=(S // tm,),
                  in_specs=[pl.BlockSpec((tm, 1), lambda i: (i, 0)), _vec_spec(128), _vec_spec(128), _vec_spec(128)],
                  out_specs=[_row_spec(tm, 128)] * 3, out_shape=[sh, sh, sh], compiler_params=_params())(
                      pos_col, inv_row, m1_row, m2_row)


def _rope_apply(x, c, s1, s2):
    outs = []
    for j in range(x.shape[1] // 128):
        xj = x[:, j * 128:(j + 1) * 128]
        outs.append(xj * c + pltpu.roll(xj, 120, 1) * s1 + pltpu.roll(xj, 8, 1) * s2)
    return outs[0] if len(outs) == 1 else jnp.concatenate(outs, axis=1)


def _rope_apply_t(d, c, s1, s2):
    outs = []
    for j in range(d.shape[1] // 128):
        dj = d[:, j * 128:(j + 1) * 128]
        outs.append(dj * c + pltpu.roll(dj * s1, 8, 1) + pltpu.roll(dj * s2, 120, 1))
    return outs[0] if len(outs) == 1 else jnp.concatenate(outs, axis=1)


def _rope_fwd(proj, c, s1, s2, *, name, tm=512):
    S = proj.shape[0]
    tm = _tile(S, tm, 8)

    def body(q_ref, k_ref, v_ref, c_ref, s1_ref, s2_ref, qo_ref, ko_ref, vo_ref):
        cv, s1v, s2v = c_ref[...], s1_ref[...], s2_ref[...]
        qo_ref[...] = (_rope_apply(q_ref[...].astype(F32), cv, s1v, s2v) * (HEAD_DIM ** -0.5)).astype(BF)
        ko_ref[...] = _rope_apply(k_ref[...].astype(F32), cv, s1v, s2v).astype(BF)
        vo_ref[...] = v_ref[...].astype(BF)
    return _pcall(body, name=name, grid=(S // tm,),
                  in_specs=[pl.BlockSpec((tm, Q_END), lambda i: (i, P_Q // Q_END)),
                            pl.BlockSpec((tm, 128), lambda i: (i, P_K // 128)),
                            pl.BlockSpec((tm, 128), lambda i: (i, P_V // 128)),
                            _row_spec(tm, 128), _row_spec(tm, 128), _row_spec(tm, 128)],
                  out_specs=[_row_spec(tm, Q_END), _row_spec(tm, 128), _row_spec(tm, 128)],
                  out_shape=[jax.ShapeDtypeStruct((S, Q_END), BF), jax.ShapeDtypeStruct((S, 128), BF),
                             jax.ShapeDtypeStruct((S, 128), BF)],
                  compiler_params=_params())(proj, proj, proj, c, s1, s2)


def _rope_bwd(dq, dk, dv, c, s1, s2, dproj, *, name, tm=512):
    S = dq.shape[0]
    tm = _tile(S, tm, 8)
    tabs = [_row_spec(tm, 128)] * 3
    shape = jax.ShapeDtypeStruct(dproj.shape, BF)

    def body_q(dq_ref, c_ref, s1_ref, s2_ref, _, o_ref):
        o_ref[...] = _rope_apply_t(dq_ref[...].astype(F32), c_ref[...], s1_ref[...], s2_ref[...]).astype(BF)
    dproj = _pcall(body_q, name=name + "_q", grid=(S // tm,), in_specs=[_row_spec(tm, Q_END)] + tabs + [ANY],
                   out_specs=pl.BlockSpec((tm, Q_END), lambda i: (i, P_Q // Q_END)), out_shape=shape,
                   input_output_aliases={4: 0}, compiler_params=_params())(dq, c, s1, s2, dproj)

    def body_kv(dk_ref, dv_ref, c_ref, s1_ref, s2_ref, _, o_ref):
        o_ref[:, :128] = _rope_apply_t(dk_ref[...], c_ref[...], s1_ref[...], s2_ref[...]).astype(BF)
        o_ref[:, 128:] = dv_ref[...].astype(BF)
    return _pcall(body_kv, name=name + "_kv", grid=(S // tm,),
                  in_specs=[_row_spec(tm, 128), _row_spec(tm, 128)] + tabs + [ANY],
                  out_specs=pl.BlockSpec((tm, 256), lambda i: (i, P_K // 256)), out_shape=shape,
                  input_output_aliases={5: 0}, compiler_params=_params())(dk, dv, c, s1, s2, dproj)


def _lane_lo(shape):
    return lax.broadcasted_iota(jnp.int32, shape, 1) < HEAD_DIM


def _stack_heads(x, g):
    lo = _lane_lo((ATTN_BLOCK, 128))
    zero = jnp.zeros((ATTN_BLOCK, 128), x.dtype)
    parts = []
    for p in range(Q_PER_KV // 2):
        xp = x[:, (g * 4 + p) * 128:(g * 4 + p + 1) * 128]
        parts += [jnp.where(lo, xp, zero), jnp.where(lo, zero, xp)]
    return jnp.concatenate(parts, axis=0)


def _unstack_heads(o2):
    lo = _lane_lo((ATTN_BLOCK, 128))
    return [jnp.where(lo, o2[2 * p * ATTN_BLOCK:(2 * p + 1) * ATTN_BLOCK], o2[(2 * p + 1) * ATTN_BLOCK:(2 * p + 2) * ATTN_BLOCK])
            for p in range(Q_PER_KV // 2)]


def _dup_half(prev, cur, g):
    x = jnp.concatenate([prev, cur], axis=0).astype(F32)
    lo = _lane_lo(x.shape)
    r = pltpu.roll(x, HEAD_DIM, 1)
    return (jnp.where(lo, x, r) if g == 0 else jnp.where(lo, r, x)).astype(BF)


def _fold_halves(x):
    return x + pltpu.roll(x, HEAD_DIM, 1)


def _attn_bias():
    i = lax.broadcasted_iota(jnp.int32, (Q_PER_KV * ATTN_BLOCK, 2 * ATTN_BLOCK), 0) & (ATTN_BLOCK - 1)
    j = lax.broadcasted_iota(jnp.int32, (Q_PER_KV * ATTN_BLOCK, 2 * ATTN_BLOCK), 1)
    band = (j > i) & (j <= i + ATTN_BLOCK)
    return jnp.stack([jnp.where(band & (j >= ATTN_BLOCK), 0.0, -jnp.inf), jnp.where(band, 0.0, -jnp.inf)]).astype(F32)


def _both(x):
    return jnp.concatenate([x, x], axis=1)


def _row_sums(x_bf):
    return jnp.dot(x_bf, jnp.ones((x_bf.shape[1], 128), BF), preferred_element_type=F32)


def _attn_probs(qs, kb, sink, bias):
    s = lax.dot_general(qs, kb, (((1,), (1,)), ((), ())), preferred_element_type=F32) + bias
    m = jnp.maximum(jnp.broadcast_to(jnp.max(s, axis=-1, keepdims=True), sink.shape), sink)
    return jnp.exp(s - _both(m)), jnp.exp(sink - m)


def _attn_specs(S):
    nb = S // ATTN_BLOCK
    qs = pl.BlockSpec((ATTN_BLOCK, Q_END), lambda n: (n, 0))
    cur = pl.BlockSpec((ATTN_BLOCK, 128), lambda n: (n, 0))
    prev = pl.BlockSpec((ATTN_BLOCK, 128), lambda n: (jnp.maximum(n - 1, 0), 0))
    sink = pl.BlockSpec((N_KV_HEADS, Q_PER_KV * ATTN_BLOCK, 128), lambda n: (0, 0, 0))
    bias = pl.BlockSpec((None, Q_PER_KV * ATTN_BLOCK, 2 * ATTN_BLOCK), lambda n: (jnp.minimum(n, 1), 0, 0))
    return nb, qs, cur, prev, sink, bias


def _attn_fwd(q, k, v, sink_rows, bias, *, name):
    S = q.shape[0]
    nb, qs, cur, prev, sink, bs = _attn_specs(S)

    def body(q_ref, kp_ref, kc_ref, vp_ref, vc_ref, sk_ref, b_ref, o_ref):
        for g in range(N_KV_HEADS):
            kb = _dup_half(kp_ref[...], kc_ref[...], g)
            vb = _dup_half(vp_ref[...], vc_ref[...], g)
            p, es = _attn_probs(_stack_heads(q_ref[...], g), kb, sk_ref[g], b_ref[...])
            ones = jnp.ones((2 * ATTN_BLOCK, 128), BF)
            o3 = jnp.dot(p.astype(BF), jnp.concatenate([vb, ones], axis=1), preferred_element_type=F32)
            o2 = o3[:, :128] / (o3[:, 128:] + es)
            for t, tile in enumerate(_unstack_heads(o2)):
                o_ref[:, (g * 4 + t) * 128:(g * 4 + t + 1) * 128] = tile.astype(BF)
    return _pcall(body, name=name, grid=(nb,), in_specs=[qs, prev, cur, prev, cur, sink, bs], out_specs=qs,
                  out_shape=jax.ShapeDtypeStruct(q.shape, BF), compiler_params=_params())(q, k, k, v, v, sink_rows, bias)


def _attn_bwd(do, q, k, v, sink_rows, bias, *, name):
    S = q.shape[0]
    nb, qs, cur, prev, sink, bs = _attn_specs(S)
    full = pl.BlockSpec((S, 128), lambda n: (0, 0))
    dsk_spec = pl.BlockSpec((N_KV_HEADS, Q_PER_KV, 128), lambda n: (0, 0, 0))

    def body(do_ref, q_ref, kp_ref, kc_ref, vp_ref, vc_ref, sk_ref, b_ref, dq_ref, dk_ref, dv_ref, dsk_ref):
        n = pl.program_id(0)

        @pl.when(n == 0)
        def _():
            dk_ref[...] = jnp.zeros_like(dk_ref)
            dv_ref[...] = jnp.zeros_like(dv_ref)
            dsk_ref[...] = jnp.zeros_like(dsk_ref)
        sub = lax.broadcasted_iota(jnp.int32, (Q_PER_KV, 128), 0)
        dkf, dvf = [], []
        for g in range(N_KV_HEADS):
            qst = _stack_heads(q_ref[...], g)
            dos = _stack_heads(do_ref[...], g)
            kb = _dup_half(kp_ref[...], kc_ref[...], g)
            vb = _dup_half(vp_ref[...], vc_ref[...], g)
            pu, es = _attn_probs(qst, kb, sk_ref[g], b_ref[...])
            inv = 1.0 / (_row_sums(pu.astype(BF)) + es)
            p = pu * _both(inv)
            dp = lax.dot_general(dos, vb, (((1,), (1,)), ((), ())), preferred_element_type=F32)
            dd = _row_sums((p * dp).astype(BF))
            ds = (p * (dp - _both(dd))).astype(BF)
            dq2 = jnp.dot(ds, kb, preferred_element_type=F32) * (HEAD_DIM ** -0.5)
            for t, tile in enumerate(_unstack_heads(dq2)):
                dq_ref[:, (g * 4 + t) * 128:(g * 4 + t + 1) * 128] = tile.astype(BF)
            dkf.append(_fold_halves(lax.dot_general(ds, qst, (((0,), (0,)), ((), ())), preferred_element_type=F32)))
            dvf.append(_fold_halves(lax.dot_general(p.astype(BF), dos, (((0,), (0,)), ((), ())),
                                                    preferred_element_type=F32)))
            dsr = -(es * inv * dd)
            upd = jnp.zeros((Q_PER_KV, 128), F32)
            for h in range(Q_PER_KV):
                upd = jnp.where(sub == h, jnp.sum(dsr[h * ATTN_BLOCK:(h + 1) * ATTN_BLOCK], axis=0, keepdims=True), upd)
            dsk_ref[g] += upd
        lo = _lane_lo((2 * ATTN_BLOCK, 128))
        dkb = jnp.where(lo, dkf[0], dkf[1])
        dvb = jnp.where(lo, dvf[0], dvf[1])
        r0 = pl.multiple_of(n * ATTN_BLOCK, ATTN_BLOCK)
        dk_ref[pl.ds(r0, ATTN_BLOCK), :] += dkb[ATTN_BLOCK:]
        dv_ref[pl.ds(r0, ATTN_BLOCK), :] += dvb[ATTN_BLOCK:]

        @pl.when(n > 0)
        def _():
            rp = pl.multiple_of((n - 1) * ATTN_BLOCK, ATTN_BLOCK)
            dk_ref[pl.ds(rp, ATTN_BLOCK), :] += dkb[:ATTN_BLOCK]
            dv_ref[pl.ds(rp, ATTN_BLOCK), :] += dvb[:ATTN_BLOCK]
    return _pcall(body, name=name, grid=(nb,), in_specs=[qs, qs, prev, cur, prev, cur, sink, bs],
                  out_specs=[qs, full, full, dsk_spec],
                  out_shape=[jax.ShapeDtypeStruct(q.shape, BF), jax.ShapeDtypeStruct((S, 128), F32),
                             jax.ShapeDtypeStruct((S, 128), F32), jax.ShapeDtypeStruct((N_KV_HEADS, Q_PER_KV, 128), F32)],
                  compiler_params=_params())(do, q, k, k, v, v, sink_rows, bias)


def _ada_fwd(c_all, ada_w):
    ncol = ada_w.shape[2]

    def body(c_ref, w_ref, o_ref):
        cv = c_ref[...]
        ca = (cv * _sigmoid(cv)).astype(BF)
        for l in range(DEPTH):
            o_ref[:, l * ncol:(l + 1) * ncol] = jnp.dot(ca, w_ref[l].astype(BF), preferred_element_type=F32)
    return _pcall(body, name="ada_fwd", out_shape=jax.ShapeDtypeStruct((N_DEV, DEPTH * ncol), F32),
                  compiler_params=_params())(c_all, ada_w)


def _ada_bwd(c_all, dm):
    ncol = dm.shape[2]

    def body(c_ref, dm_ref, o_ref):
        cv = c_ref[...]
        ca = (cv * _sigmoid(cv)).astype(BF)
        for l in range(DEPTH):
            o_ref[l] = lax.dot_general(ca, dm_ref[l].astype(BF), (((0,), (0,)), ((), ())), preferred_element_type=F32)
    return _pcall(body, name="ada_bwd", out_shape=jax.ShapeDtypeStruct((DEPTH, D_MODEL, ncol), F32),
                  compiler_params=_params())(c_all, dm)


def _adamw(w, g, m, v, *, name):
    R, C = w.shape
    tr = R
    for t in range(8, 513, 8):
        if R % t == 0:
            tr = t
    c1 = 1.0 - ADAM_B1 ** ADAM_STEP
    c2 = 1.0 - ADAM_B2 ** ADAM_STEP

    def body(w_ref, g_ref, m_ref, v_ref, d_ref, mo_ref, vo_ref):
        gv = g_ref[...]
        mn = ADAM_B1 * m_ref[...] + (1.0 - ADAM_B1) * gv
        vn = ADAM_B2 * v_ref[...] + (1.0 - ADAM_B2) * (gv * gv)
        mo_ref[...] = mn
        vo_ref[...] = vn
        d_ref[...] = -ADAM_LR * ((mn * (1.0 / c1)) / (jnp.sqrt(vn * (1.0 / c2)) + ADAM_EPS) + ADAM_WD * w_ref[...])
    spec = pl.BlockSpec((tr, C), lambda i: (i, 0))
    sh = jax.ShapeDtypeStruct((R, C), F32)
    return _pcall(body, name=name, grid=(R // tr,), in_specs=[spec] * 4, out_specs=[spec] * 3, out_shape=[sh, sh, sh],
                  compiler_params=_params())(w, g, m, v)


def _adamw_layers(w, g_layers, m, v, *, name):
    L, R, C = w.shape
    assert L == 2 and len(g_layers) == 2
    tr = R
    for t in range(8, 513, 8):
        if R % t == 0:
            tr = t
    c1 = 1.0 - ADAM_B1 ** ADAM_STEP
    c2 = 1.0 - ADAM_B2 ** ADAM_STEP

    def body(w_ref, g0_ref, g1_ref, m_ref, v_ref, go_ref, d_ref, mo_ref, vo_ref):
        gv = jnp.where(pl.program_id(0) == 0, g0_ref[...], g1_ref[...])
        go_ref[...] = gv
        mn = ADAM_B1 * m_ref[...] + (1.0 - ADAM_B1) * gv
        vn = ADAM_B2 * v_ref[...] + (1.0 - ADAM_B2) * (gv * gv)
        mo_ref[...] = mn
        vo_ref[...] = vn
        d_ref[...] = -ADAM_LR * ((mn * (1.0 / c1)) / (jnp.sqrt(vn * (1.0 / c2)) + ADAM_EPS) + ADAM_WD * w_ref[...])
    spec = pl.BlockSpec((None, tr, C), lambda l, i: (l, i, 0))
    sh = jax.ShapeDtypeStruct((L, R, C), F32)
    g_specs = [pl.BlockSpec((tr, C), lambda l, i, k=k: (jnp.where(l == k, i, 0), 0)) for k in range(L)]
    return _pcall(body, name=name, grid=(L, R // tr), in_specs=[spec] + g_specs + [spec, spec], out_specs=[spec] * 4,
                  out_shape=[sh] * 4, compiler_params=_params())(w, *g_layers, m, v)


def _sum8(parts, *, name):
    _, R, C = parts.shape
    tr = _tile(R, 512, 16)

    def body(p_ref, o_ref):
        acc = p_ref[0].astype(F32)
        for k in range(1, N_DEV):
            acc = acc + p_ref[k].astype(F32)
        o_ref[...] = acc
    return _pcall(body, name=name, grid=(R // tr,), in_specs=[pl.BlockSpec((N_DEV, tr, C), lambda i: (0, i, 0))],
                  out_specs=pl.BlockSpec((tr, C), lambda i: (i, 0)), out_shape=jax.ShapeDtypeStruct((R, C), F32),
                  compiler_params=_params())(parts)


MESH_ID = pl.DeviceIdType.MESH
ANY = pl.BlockSpec(memory_space=pl.ANY)


def _all_gather(x, *, name, after=None):
    R, C = x.shape
    extra = [] if after is None else [after]

    def body(x_ref, *rest):
        out_ref, send_sems, recv_sems, local_sem = rest[-4:]
        mx, my, mc = lax.axis_index("x"), lax.axis_index("y"), lax.axis_index("c")
        me, sibling = (mx, my, mc), (mx, my, 1 - mc)
        chips = [(1 - mx, my), (mx, 1 - my), (1 - mx, 1 - my)]

        def blk(px, py, pc):
            return out_ref.at[4 * px + 2 * py + pc]

        def copy(k, block, to, src=None):
            return pltpu.make_async_remote_copy(
                src_ref=blk(*block) if src is None else src, dst_ref=blk(*block),
                send_sem=send_sems.at[k], recv_sem=recv_sems.at[k], device_id=to, device_id_type=MESH_ID)

        mine = pltpu.make_async_copy(x_ref, blk(*me), local_sem)
        mine.start()
        first = [copy(0, me, sibling, src=x_ref)]
        first += [copy(1 + j, me, (*chip, mc), src=x_ref) for j, chip in enumerate(chips)]
        for cp in first:
            cp.start()
        passed = [copy(4 + j, (*chip, mc), sibling) for j, chip in enumerate(chips)]
        for j, chip in enumerate(chips):
            copy(1 + j, (*chip, mc), me).wait_recv()
            passed[j].start()
        copy(0, sibling, me).wait_recv()
        for j, chip in enumerate(chips):
            copy(4 + j, (*chip, 1 - mc), me).wait_recv()
        for cp in first + passed:
            cp.wait_send()
        mine.wait()
    return _pcall(body, name=name, in_specs=[ANY] * (1 + len(extra)), out_specs=ANY,
                  out_shape=jax.ShapeDtypeStruct((N_DEV, R, C), x.dtype),
                  scratch_shapes=[pltpu.SemaphoreType.DMA((7,)), pltpu.SemaphoreType.DMA((7,)), pltpu.SemaphoreType.DMA],
                  compiler_params=pltpu.CompilerParams(has_side_effects=True))(x, *extra)


HBM_SPEC = pl.BlockSpec(memory_space=pltpu.HBM)
SEM_SPEC = pl.BlockSpec(memory_space=pltpu.SEMAPHORE)
DATAFLOW = pltpu.SideEffectType.DATAFLOW_SIDE_EFFECTING


def _coords():
    return lax.axis_index("x"), lax.axis_index("y"), lax.axis_index("c")


def _other_chips(mx, my):
    return [(1 - mx, my), (mx, 1 - my), (1 - mx, 1 - my)]


def _plan_gather_ici(refs, send, recv):
    src, land = refs
    mx, my, mc = _coords()
    return [pltpu.make_async_remote_copy(src_ref=src, dst_ref=land.at[mc, 2 * mx + my], send_sem=send[j], recv_sem=recv[j],
                                         device_id=(px, py, mc), device_id_type=MESH_ID)
            for j, (px, py) in enumerate(_other_chips(mx, my))]


def _plan_gather_d2d(refs, send, recv):
    (land,) = refs
    mx, my, mc = _coords()
    return [pltpu.make_async_remote_copy(src_ref=land.at[mc], dst_ref=land.at[mc], send_sem=send[0], recv_sem=recv[0],
                                         device_id=(mx, my, 1 - mc), device_id_type=MESH_ID)]


def _plan_reduce_d2d(refs, send, recv):
    g, land = refs
    mx, my, mc = _coords()
    return [pltpu.make_async_remote_copy(src_ref=g.at[1 - mc], dst_ref=land, send_sem=send[0], recv_sem=recv[0],
                                         device_id=(mx, my, 1 - mc), device_id_type=MESH_ID)]


def _plan_reduce_ici(refs, send, recv):
    h, land = refs
    mx, my, mc = _coords()
    return [pltpu.make_async_remote_copy(src_ref=h.at[2 * px + py], dst_ref=land.at[j], send_sem=send[j], recv_sem=recv[j],
                                         device_id=(px, py, mc), device_id_type=MESH_ID)
            for j, (px, py) in enumerate(_other_chips(mx, my))]


def _rdma_start(bufs, n, plan, *, name, after=None):
    nb = len(bufs)
    extra = [] if after is None else [after]
    ne = len(extra)

    def body(*refs):
        ins, send, recv = refs[:nb], refs[nb + ne:nb + ne + n], refs[nb + ne + n:nb + ne + 2 * n]
        token = refs[-1]
        for cp in plan(ins, send, recv):
            cp.start()
        token[...] = jnp.zeros_like(token)
    out = _pcall(body, name=name,
                 out_shape=tuple([pltpu.SemaphoreType.DMA(())] * (2 * n) + [pltpu.HBM(b.shape, b.dtype) for b in bufs]
                                 + [jax.ShapeDtypeStruct((8, 128), F32)]),
                 in_specs=tuple([HBM_SPEC] * nb + [ANY] * ne),
                 out_specs=tuple([SEM_SPEC] * (2 * n) + [HBM_SPEC] * nb + [pl.BlockSpec(memory_space=pltpu.VMEM)]),
                 input_output_aliases={i: 2 * n + i for i in range(nb)},
                 compiler_params=pltpu.CompilerParams(has_side_effects=DATAFLOW))(
                     *[pltpu.with_memory_space_constraint(b, pltpu.HBM) for b in bufs], *extra)
    return list(out[:2 * n]), list(out[2 * n:2 * n + nb]), out[-1]


def _rdma_wait(sems, bufs, n, plan, after, *, name):
    nb = len(bufs)

    def body(*refs):
        ins, send, recv = refs[:nb], refs[nb:nb + n], refs[nb + n:nb + 2 * n]
        for cp in plan(ins, send, recv):
            cp.wait_send()
            cp.wait_recv()
    out = _pcall(body, name=name, out_shape=tuple(pltpu.HBM(b.shape, b.dtype) for b in bufs),
                 in_specs=tuple([HBM_SPEC] * nb + [SEM_SPEC] * (2 * n) + [ANY]), out_specs=tuple([HBM_SPEC] * nb),
                 input_output_aliases={i: i for i in range(nb)},
                 compiler_params=pltpu.CompilerParams(has_side_effects=DATAFLOW))(*bufs, *sems, after)
    return list(out)


def _sum_pair(g, land, cidx, *, name):
    _, nchip, R, C = g.shape
    tr = _tile(R, 1056, 16)

    def body(c_ref, g_ref, l_ref, o_ref):
        o_ref[...] = g_ref[...] + l_ref[...]
    grid_spec = pltpu.PrefetchScalarGridSpec(
        num_scalar_prefetch=1, grid=(nchip, R // tr),
        in_specs=[pl.BlockSpec((None, None, tr, C), lambda p, i, c_ref: (c_ref[0], p, i, 0)),
                  pl.BlockSpec((None, tr, C), lambda p, i, c_ref: (p, i, 0))],
        out_specs=pl.BlockSpec((None, tr, C), lambda p, i, c_ref: (p, i, 0)))
    return _pcall(body, name=name, grid_spec=grid_spec, out_shape=jax.ShapeDtypeStruct((nchip, R, C), BF),
                  compiler_params=_params())(cidx, g, land)


def _sum_chips(h, land, chipidx, *, name):
    _, R, C = h.shape
    tr = _tile(R, 1056, 16)

    def body(c_ref, h_ref, l_ref, o_ref):
        acc = h_ref[...].astype(F32)
        for j in range(3):
            acc = acc + l_ref[j].astype(F32)
        o_ref[...] = acc
    grid_spec = pltpu.PrefetchScalarGridSpec(
        num_scalar_prefetch=1, grid=(R // tr,),
        in_specs=[pl.BlockSpec((None, tr, C), lambda i, c_ref: (c_ref[0], i, 0)),
                  pl.BlockSpec((3, tr, C), lambda i, c_ref: (0, i, 0))],
        out_specs=pl.BlockSpec((tr, C), lambda i, c_ref: (i, 0)))
    return _pcall(body, name=name, grid_spec=grid_spec, out_shape=jax.ShapeDtypeStruct((R, C), F32),
                  compiler_params=_params())(chipidx, h, land)


PART_IN = ("w_in",)
PART_MIX = ("proj_a", "proj_b", "w_out")
PART_FFN = ("ffn_w_gate", "ffn_w_up", "ffn_w_down")


def _part_rows(names):
    return sum(BIG_ROWS[n] for n in names)


def _part_offsets(names):
    off, r = {}, 0
    for n in names:
        off[n] = r
        r += BIG_ROWS[n]
    return off


def _pack_shards(shards, l, names):
    return jnp.concatenate([(shards[n][l].T if n in COL_SHARDED else shards[n][l]).astype(BF) for n in names], axis=0)


def _unpack_weights(full8, names):
    off = _part_offsets(names)

    def whole(n):
        return full8[:, off[n]:off[n] + BIG_ROWS[n], :].reshape(N_DEV * BIG_ROWS[n], 1024)
    out = {}
    if "w_in" in names:
        wt_in = whole("w_in")
        out["wt_in"] = jnp.concatenate([wt_in[V_END:], wt_in[:V_END]], axis=0)
    for n in ("proj_a", "proj_b", "w_out"):
        if n in names:
            out[n] = whole(n)
    if "ffn_w_gate" in names:
        out["wt_gate"], out["wt_up"], out["w_down"] = whole("ffn_w_gate"), whole("ffn_w_up"), whole("ffn_w_down")
    return out


def _from_land(land):
    return land.transpose(1, 0, 2, 3).reshape(N_DEV, land.shape[2], 1024)


def _pack_grads(wg, names):
    full = {"proj_a": wg.get("proj_a"), "proj_b": wg.get("proj_b"), "w_out": wg.get("w_out"), "ffn_w_down": wg.get("w_down"),
            "ffn_w_gate": wg.get("wt_gate"), "ffn_w_up": wg.get("wt_up")}
    if "w_in" in names:
        full["w_in"] = jnp.concatenate([wg["wt_in"][P_Q:], wg["wt_in"][:P_Q]], axis=0)
    blocks = jnp.concatenate([full[n].reshape(N_DEV, BIG_ROWS[n], 1024) for n in names], axis=1)
    return blocks.reshape(4, 2, _part_rows(names), 1024).transpose(1, 0, 2, 3)


def _unpack_shard_grads(gs, names):
    off = _part_offsets(names)
    out = {}
    for n in names:
        blk = gs[off[n]:off[n] + BIG_ROWS[n]]
        out[n] = blk.T if n in COL_SHARDED else blk
    return out


def _rope_setup(positions):
    S = positions.shape[0]
    inv = ROPE_THETA ** (-jnp.arange(0, ROT_DIM, 2, dtype=F32) / ROT_DIM)
    lane = np.arange(128) % HEAD_DIM
    half = ROT_DIM // 2
    inv_row = jnp.where(lane < ROT_DIM, jnp.tile(inv, 128 // half), 0.0)[None, :].astype(F32)
    m1_row = jnp.asarray((lane < half).astype(np.float32))[None, :]
    m2_row = jnp.asarray(((lane >= half) & (lane < ROT_DIM)).astype(np.float32))[None, :]
    return (*_rope_tables(positions.astype(F32).reshape(S, 1), inv_row, m1_row, m2_row), _attn_bias())


def _hook(hooks, point, after):
    f = None if hooks is None else hooks.get(point)
    return None if f is None else f(after)


def _layer_fwd(l, x, mod_l, W, small, rope, hooks=None):
    rc, rs1, rs2, bias = rope
    sh1, sc1, g1, sh2, sc2, g2 = [mod_l[i * D_MODEL:(i + 1) * D_MODEL][None, :] for i in range(6)]
    nw1, nw2 = small["norm1_w"][l][None, :], small["norm2_w"][l][None, :]
    tok = _hook(hooks, "mm_in", x)
    h, (proj,) = _norm_mm(x, nw1, sc1, sh1, [W["wt_in"]], name=f"mm_in{l}", after=tok, tm=2048, tn_cap=768)
    q_r, k_r, v_b = _rope_fwd(proj, rc, rs1, rs2, name=f"rope_fwd{l}")
    sink_rows = jnp.repeat(small["attn_sinks"][l].reshape(N_KV_HEADS, Q_PER_KV), ATTN_BLOCK, axis=1)
    sink_rows = jnp.broadcast_to(sink_rows[..., None], sink_rows.shape + (128,))
    y_attn = _attn_fwd(q_r, k_r, v_b, sink_rows, bias, name=f"attn_fwd{l}")
    lnw, lnb = small["sgu_ln_w"][l][None, :], small["sgu_ln_b"][l][None, :]
    sgu_bt = small["sgu_b"][l].T
    y_sgu = _sgu_fwd(proj, lnw, lnb, small["sgu_w"][l], sgu_bt, name=f"sgu_fwd{l}", after=_hook(hooks, "sgu", y_attn))
    tok = _hook(hooks, "mm_pa", y_sgu)
    a_br, b_br, merged = _merge_fwd(y_sgu, y_attn, W["proj_a"], W["proj_b"], proj, name=f"merge_fwd{l}", after=tok)
    x1, o1 = _mm(merged, W["w_out"], nt=False, out_dtype=F32, name=f"mm_out{l}", res=x, gvec=g1)
    tok = _hook(hooks, "mm_gu", x1)
    h2, (a_g, a_u) = _norm_mm(x1, nw2, sc2, sh2, [W["wt_gate"], W["wt_up"]], name=f"mm_gu{l}", after=tok, tn_cap=1408)
    cw, cb = small["ffn_conv_w"][l], small["ffn_conv_b"][l][None, :]
    hf, a_c = _ffn_act_fwd(a_g, a_u, cw, cb, name=f"ffn_act_fwd{l}")
    x2, o2 = _mm(hf, W["w_down"], nt=False, out_dtype=F32, name=f"mm_down{l}", res=x1, gvec=g2)
    saved = dict(x=x, h=h, proj=proj, q_r=q_r, k_r=k_r, v_b=v_b, sink_rows=sink_rows, y_attn=y_attn, y_sgu=y_sgu,
                 a_br=a_br, b_br=b_br, merged=merged, x1=x1, o1=o1, h2=h2, a_g=a_g, a_u=a_u, a_c=a_c, hf=hf, o2=o2)
    return x2, saved


def _layer_bwd(l, dx, do2, dg2, mod_l, W, small, rope, sv, below=None, hooks=None, wg=None):
    rc, rs1, rs2, bias = rope
    sh1, sc1, g1, sh2, sc2, g2 = [mod_l[i * D_MODEL:(i + 1) * D_MODEL][None, :] for i in range(6)]
    nw1, nw2 = small["norm1_w"][l][None, :], small["norm2_w"][l][None, :]
    cw = small["ffn_conv_w"][l]
    lnw, lnb = small["sgu_ln_w"][l][None, :], small["sgu_ln_b"][l][None, :]
    sgu_bt = small["sgu_b"][l].T
    wg = {} if wg is None else wg
    dhf = _mm(do2, W["w_down"], nt=True, out_dtype=BF, name=f"mm_down_dx{l}", after=_hook(hooks, "mm_down_dx", do2),
              tn_cap=1408)
    wg["w_down"] = _mm_tn(sv["hf"], do2, name=f"mm_down_dw{l}")
    dac, dup, dcb = _ffn_act_bwd_a(dhf, sv["a_c"], sv["a_u"], name=f"ffn_act_bwd_a{l}")
    da, dcw = _ffn_act_bwd_b(dac, sv["a_g"], cw, name=f"ffn_act_bwd_b{l}")
    dh2 = _mm([da, dup], [W["wt_gate"], W["wt_up"]], nt=False, out_dtype=F32, name=f"mm_gu_dx{l}",
              after=_hook(hooks, "mm_gu_dx", da))
    wg["wt_gate"] = _mm_tn(da, sv["h2"], name=f"mm_gate_dw{l}")
    wg["wt_up"] = _mm_tn(dup, sv["h2"], name=f"mm_up_dw{l}")
    dx1, dnw2, dsc2, dsh2, do1, dg1 = _normmod_bwd(dh2, sv["x1"], nw2, sc2, sh2, dx, (sv["o1"], g1), name=f"normmod2_bwd{l}")
    d_a, d_b, dproj = _merge_bwd(do1, W["w_out"], sv["a_br"], sv["b_br"], sv["proj"], name=f"merge_bwd{l}",
                                 after=_hook(hooks, "merge_bwd", do1))
    wg["w_out"] = _mm_tn(sv["merged"], do1, name=f"mm_out_dw{l}")
    dysgu = _mm(d_a, W["proj_a"], nt=True, out_dtype=F32, name=f"mm_pa_dx{l}", after=_hook(hooks, "mm_pa_dx", d_a))
    dyattn = _mm(d_b, W["proj_b"], nt=True, out_dtype=BF, name=f"mm_pb_dx{l}")
    wg["proj_a"] = _mm_tn(sv["y_sgu"], d_a, name=f"mm_pa_dw{l}")
    wg["proj_b"] = _mm_tn(sv["y_attn"], d_b, name=f"mm_pb_dw{l}")
    dproj, dlnw, dlnb, dsguw, dsgubt = _sgu_bwd(dysgu, sv["proj"], lnw, lnb, small["sgu_w"][l], sgu_bt, dproj,
                                                name=f"sgu_bwd{l}")
    dq_r, dk_r, dv_b, dsk = _attn_bwd(dyattn, sv["q_r"], sv["k_r"], sv["v_b"], sv["sink_rows"], bias, name=f"attn_bwd{l}")
    dproj = _rope_bwd(dq_r, dk_r, dv_b, rc, rs1, rs2, dproj, name=f"rope_bwd{l}")
    wg["wt_in"] = _mm_tn(dproj, sv["h"], name=f"mm_in_dw{l}")
    dh = _mm(dproj, W["wt_in"], nt=False, out_dtype=F32, name=f"mm_in_dx{l}", after=_hook(hooks, "mm_in_dx", wg["wt_in"]))
    dx0, dnw1, dsc1, dsh1, *gate_below = _normmod_bwd(dh, sv["x"], nw1, sc1, sh1, dx1, below, name=f"normmod1_bwd{l}")
    dmod = jnp.concatenate([dsh1, dsc1, dg1, dsh2, dsc2, dg2], axis=1)[0]
    sg = {"norm1_w": dnw1[0], "norm2_w": dnw2[0], "attn_sinks": dsk[:, :, 0].reshape(N_Q_HEADS),
          "sgu_ln_w": dlnw[0], "sgu_ln_b": dlnb[0], "sgu_w": dsguw, "sgu_b": dsgubt.T,
          "ffn_conv_w": dcw, "ffn_conv_b": dcb[0]}
    return (dx0, *gate_below), wg, sg, dmod


SMALL = ("ada_b", "norm1_w", "attn_sinks", "sgu_ln_w", "sgu_ln_b", "sgu_w", "sgu_b", "norm2_w", "ffn_conv_b", "final_norm_w")
WEIGHT_ORDER = ("ada_w", "ada_b", "norm1_w", "w_in", "attn_sinks", "sgu_ln_w", "sgu_ln_b", "sgu_w", "sgu_b", "proj_a", "proj_b",
                "w_out", "norm2_w", "ffn_w_gate", "ffn_w_up", "ffn_conv_w", "ffn_conv_b", "ffn_w_down", "final_norm_w")


def _flat_pack(arrs, rows):
    flat = jnp.concatenate([a.reshape(-1) for a in arrs])
    return jnp.pad(flat, (0, rows * 1024 - flat.shape[0])).reshape(rows, 1024)


def _flat_unpack(buf, shapes):
    flat = buf.reshape(-1)
    out, o = [], 0
    for s in shapes:
        n = int(np.prod(s))
        out.append(flat[o:o + n].reshape(s))
        o += n
    return out


def _adam2d(w, g, m, v, *, name):
    shp = w.shape
    r2 = (int(np.prod(shp[:-1])), shp[-1]) if len(shp) > 1 else (1, shp[0])
    d, mn, vn = _adamw(w.reshape(r2), g.reshape(r2), m.reshape(r2), v.reshape(r2), name=name)
    return d.reshape(shp), mn.reshape(shp), vn.reshape(shp)


def kernel(x, c, positions, ada_w, ada_b, norm1_w, w_in, attn_sinks, sgu_ln_w, sgu_ln_b, sgu_w, sgu_b, proj_a, proj_b, w_out, norm2_w, ffn_w_gate, ffn_w_up, ffn_conv_w, ffn_conv_b, ffn_w_down, final_norm_w, loss_target, m_ada_w, m_ada_b, m_norm1_w, m_w_in, m_attn_sinks, m_sgu_ln_w, m_sgu_ln_b, m_sgu_w, m_sgu_b, m_proj_a, m_proj_b, m_w_out, m_norm2_w, m_ffn_w_gate, m_ffn_w_up, m_ffn_conv_w, m_ffn_conv_b, m_ffn_w_down, m_final_norm_w, v_ada_w, v_ada_b, v_norm1_w, v_w_in, v_attn_sinks, v_sgu_ln_w, v_sgu_ln_b, v_sgu_w, v_sgu_b, v_proj_a, v_proj_b, v_w_out, v_norm2_w, v_ffn_w_gate, v_ffn_w_up, v_ffn_conv_w, v_ffn_conv_b, v_ffn_w_down, v_final_norm_w):
    wts = dict(ada_w=ada_w, ada_b=ada_b, norm1_w=norm1_w, w_in=w_in, attn_sinks=attn_sinks, sgu_ln_w=sgu_ln_w,
               sgu_ln_b=sgu_ln_b, sgu_w=sgu_w, sgu_b=sgu_b, proj_a=proj_a, proj_b=proj_b, w_out=w_out, norm2_w=norm2_w,
               ffn_w_gate=ffn_w_gate, ffn_w_up=ffn_w_up, ffn_conv_w=ffn_conv_w, ffn_conv_b=ffn_conv_b,
               ffn_w_down=ffn_w_down, final_norm_w=final_norm_w)
    mom = dict(ada_w=m_ada_w, ada_b=m_ada_b, norm1_w=m_norm1_w, w_in=m_w_in, attn_sinks=m_attn_sinks, sgu_ln_w=m_sgu_ln_w,
               sgu_ln_b=m_sgu_ln_b, sgu_w=m_sgu_w, sgu_b=m_sgu_b, proj_a=m_proj_a, proj_b=m_proj_b, w_out=m_w_out,
               norm2_w=m_norm2_w, ffn_w_gate=m_ffn_w_gate, ffn_w_up=m_ffn_w_up, ffn_conv_w=m_ffn_conv_w,
               ffn_conv_b=m_ffn_conv_b, ffn_w_down=m_ffn_w_down, final_norm_w=m_final_norm_w)
    var = dict(ada_w=v_ada_w, ada_b=v_ada_b, norm1_w=v_norm1_w, w_in=v_w_in, attn_sinks=v_attn_sinks, sgu_ln_w=v_sgu_ln_w,
               sgu_ln_b=v_sgu_ln_b, sgu_w=v_sgu_w, sgu_b=v_sgu_b, proj_a=v_proj_a, proj_b=v_proj_b, w_out=v_w_out,
               norm2_w=v_norm2_w, ffn_w_gate=v_ffn_w_gate, ffn_w_up=v_ffn_w_up, ffn_conv_w=v_ffn_conv_w,
               ffn_conv_b=v_ffn_conv_b, ffn_w_down=v_ffn_w_down, final_norm_w=v_final_norm_w)
    me = 4 * lax.axis_index("x") + 2 * lax.axis_index("y") + lax.axis_index("c")
    ada_cols = ada_w.shape[2]

    c_all = _all_gather(jnp.broadcast_to(c, (8, D_MODEL)), name="ag_c")[:, 0, :]
    prod = _ada_fwd(c_all, ada_w)
    prod_all = _all_gather(prod, name="ag_mod")
    mine = lax.dynamic_index_in_dim(prod_all, me, axis=1, keepdims=False)
    mod = jnp.stack([mine[:, l * ada_cols:(l + 1) * ada_cols].reshape(-1) for l in range(DEPTH)]) + ada_b

    conv_cols = ffn_conv_w.shape[2]
    conv_all = _all_gather(_flat_pack([ffn_conv_w], 8), name="ag_conv", after=mod)
    conv_full = jnp.stack([a.reshape(DEPTH, 3, conv_cols) for a in
                           [conv_all[j].reshape(-1)[:DEPTH * 3 * conv_cols] for j in range(N_DEV)]], axis=2)
    conv_full = conv_full.reshape(DEPTH, 3, FFN_DIM)
    small = {n: wts[n] for n in SMALL}
    small["ffn_conv_w"] = conv_full

    mx, my, mc = _coords()
    cidx = jnp.reshape(mc, (1,)).astype(jnp.int32)
    chipidx = jnp.reshape(2 * mx + my, (1,)).astype(jnp.int32)
    rope = _rope_setup(positions[0])

    class Gather:
        def __init__(self, src, tag):
            self.tag, self.src = tag, src
            self.land = lax.dynamic_update_slice(lax.empty((2, 4) + src.shape, src.dtype), src[None, None],
                                                 (mc, 2 * mx + my, 0, 0))

        def ici_start(self, after):
            self.sems, (self.src, self.land), tok = _rdma_start([self.src, self.land], 3, _plan_gather_ici,
                                                                name=f"ag_{self.tag}_ici_start", after=after)
            return tok

        def ici_wait_d2d_start(self, after):
            _, land = _rdma_wait(self.sems, [self.src, self.land], 3, _plan_gather_ici, after, name=f"ag_{self.tag}_ici_wait")
            self.sems, (self.land,), tok = _rdma_start([land], 1, _plan_gather_d2d, name=f"ag_{self.tag}_d2d_start")
            return tok

        def d2d_wait(self, after):
            (land,) = _rdma_wait(self.sems, [self.land], 1, _plan_gather_d2d, after, name=f"ag_{self.tag}_d2d_wait")
            return _from_land(land)

    def weights_job(names, l, tag):
        job = Gather(_pack_shards(wts, l, names), tag)
        job.weights = lambda after: _unpack_weights(job.d2d_wait(after), names)
        return job

    W0 = _unpack_weights(_all_gather(_pack_shards(wts, 0, PART_IN), name="ag_w0_in", after=conv_all), PART_IN)
    W1 = {}
    rest = PART_MIX + PART_FFN
    g_rest0 = weights_job(rest, 0, "w0_rest")
    g_in1, g_rest1 = weights_job(PART_IN, 1, "w1_in"), weights_job(rest, 1, "w1_rest")

    def rest0_then_layer1(after):
        W0.update(g_rest0.weights(after))
        return g_rest1.ici_start(g_in1.ici_start(W0["w_down"]))

    x1, sv0 = _layer_fwd(0, x[0], mod[0], W0, small, rope,
                         {"mm_in": lambda after: g_rest0.ici_start(W0["wt_in"]), "sgu": g_rest0.ici_wait_d2d_start,
                          "mm_pa": rest0_then_layer1, "mm_gu": g_in1.ici_wait_d2d_start})
    g_rest1.ici_wait_d2d_start(x1)
    x2, sv1 = _layer_fwd(1, x1, mod[1], W1, small, rope,
                         {"mm_in": lambda after: W1.update(g_in1.weights(after)),
                          "mm_pa": lambda after: W1.update(g_rest1.weights(after))})
    gate2 = [mod[l][5 * D_MODEL:][None, :] for l in range(DEPTH)]
    dx2, dfw, loss_tile, do2, dg2 = _head(x2, final_norm_w[None, :], loss_target[0], (sv1["o2"], gate2[1]))
    loss = lax.psum(loss_tile[0, 0], ("x", "y", "c"))

    class Reduce:
        def __init__(self, names, tag):
            self.names, self.tag, self.rows = names, tag, _part_rows(names)

        def d2d_start(self, wg, after=None):
            self.sems, self.bufs, tok = _rdma_start([_pack_grads(wg, self.names), lax.empty((4, self.rows, 1024), BF)], 1,
                                                    _plan_reduce_d2d, name=f"rs_{self.tag}_d2d_start", after=after)
            return tok

        def d2d_wait_ici_start(self, after):
            g_t, land_a = _rdma_wait(self.sems, self.bufs, 1, _plan_reduce_d2d, after, name=f"rs_{self.tag}_d2d_wait")
            h = _sum_pair(g_t, land_a, cidx, name=f"rs_{self.tag}_sum_pair")
            self.sems, self.bufs, tok = _rdma_start([h, lax.empty((3, self.rows, 1024), BF)], 3, _plan_reduce_ici,
                                                    name=f"rs_{self.tag}_ici_start")
            return tok

        def ici_wait(self, after):
            h_t, land_b = _rdma_wait(self.sems, self.bufs, 3, _plan_reduce_ici, after, name=f"rs_{self.tag}_ici_wait")
            return _unpack_shard_grads(_sum_chips(h_t, land_b, chipidx, name=f"rs_{self.tag}_sum_chips"), self.names)

    (dx1, do2, dg2), wg1, sg1, dmod1 = _layer_bwd(1, dx2, do2, dg2, mod[1], W1, small, rope, sv1, below=(sv0["o2"], gate2[0]))
    r_all1, r_ffn0, r_mix0 = Reduce(BIG, "g1"), Reduce(PART_FFN, "g0_ffn"), Reduce(PART_IN + PART_MIX, "g0_mix")
    tok1 = r_all1.d2d_start(wg1)
    wg0, shard1 = {}, {}

    def layer1_done_then_mix0(after):
        shard1.update(r_all1.ici_wait(after))
        return r_mix0.d2d_wait_ici_start(r_mix0.d2d_start(wg0, shard1["w_in"]))

    (grad_x,), _, sg0, dmod0 = _layer_bwd(
        0, dx1, do2, dg2, mod[0], W0, small, rope, sv0, wg=wg0,
        hooks={"mm_down_dx": lambda after: tok1, "mm_gu_dx": r_all1.d2d_wait_ici_start,
               "merge_bwd": lambda after: r_ffn0.d2d_start(wg0, after), "mm_pa_dx": r_ffn0.d2d_wait_ici_start,
               "mm_in_dx": layer1_done_then_mix0})
    sg = {n: jnp.stack([sg0[n], sg1[n]]) for n in sg0}
    sg["final_norm_w"] = dfw[0]
    dmod = jnp.stack([dmod0, dmod1])
    vec_names = [n for n in SMALL if n not in ("ada_b", "sgu_w")] + ["ffn_conv_w"]
    vec_shapes = [(DEPTH, 6 * D_MODEL)] + [sg[n].shape for n in vec_names]
    vec_rows = -(-sum(int(np.prod(s)) for s in vec_shapes) // 1024 // 16) * 16
    sgu_rows = sgu_w.size // 1024
    g_small = Gather(jnp.concatenate([_flat_pack([dmod] + [sg[n] for n in vec_names], vec_rows),
                                      sg["sgu_w"].reshape(sgu_rows, 1024)], axis=0).astype(BF), "small")
    tok = g_small.ici_start(grad_x)

    shard0 = r_ffn0.ici_wait(tok)
    shard0.update(r_mix0.ici_wait(shard0["ffn_w_down"]))
    grads, delta, new_m, new_v = {}, {}, {}, {}
    for n in BIG:
        two = lambda a: a.reshape(DEPTH, -1, a.shape[-1])
        out = _adamw_layers(two(wts[n]), [shard0[n], shard1[n]], two(mom[n]), two(var[n]), name=f"adamw_{n}")
        grads[n], delta[n], new_m[n], new_v[n] = [o.reshape(wts[n].shape) for o in out]

    sm_all = g_small.d2d_wait(g_small.ici_wait_d2d_start(delta["ffn_w_gate"]))
    sm_sum = _sum8(sm_all, name="sum_small")
    vec_sum = _flat_unpack(sm_sum[:vec_rows], vec_shapes)
    grads["ada_b"] = vec_sum[0]
    for n, gsum in zip(vec_names, vec_sum[1:]):
        grads[n] = gsum
    grads["sgu_w"] = sm_sum[vec_rows:].reshape(sgu_w.shape)
    grads["ffn_conv_w"] = lax.dynamic_slice_in_dim(grads["ffn_conv_w"], me * conv_cols, conv_cols, axis=2)
    dmod_all = sm_all[:, :DEPTH * 6, :].astype(F32).reshape(N_DEV, DEPTH, 6 * D_MODEL)
    dm_mine = lax.dynamic_slice_in_dim(dmod_all, me * ada_cols, ada_cols, axis=2).transpose(1, 0, 2)
    dm_mine = jnp.pad(dm_mine, ((0, 0), (0, 8), (0, 0)))
    grads["ada_w"] = _ada_bwd(jnp.pad(c_all, ((0, 8), (0, 0))), dm_mine)

    for n in WEIGHT_ORDER:
        if n not in delta:
            delta[n], new_m[n], new_v[n] = _adam2d(wts[n], grads[n], mom[n], var[n], name=f"adamw_{n}")
    return (loss, grad_x[None], *[grads[n] for n in WEIGHT_ORDER], *[delta[n] for n in WEIGHT_ORDER],
            *[new_m[n] for n in WEIGHT_ORDER], *[new_v[n] for n in WEIGHT_ORDER])
```

```python
import jax
import jax.numpy as jnp
import numpy as np
from jax import lax
from jax.experimental import pallas as pl
from jax.experimental.pallas import tpu as pltpu

F32 = jnp.float32
BF = jnp.bfloat16

N_DEV = 8
D_MODEL = 1024
DEPTH = 2
N_Q_HEADS = 16
N_KV_HEADS = 2
HEAD_DIM = 64
Q_PER_KV = N_Q_HEADS // N_KV_HEADS
ATTN_BLOCK = 128
ROPE_THETA = 500000.0
ROT_DIM = HEAD_DIM // 4
SGU_WIDTH = 1024
SGU_GROUPS = 8
SGU_CHUNK = 128
FFN_DIM = 2816
NORM_EPS = 1e-6
Q_END = N_Q_HEADS * HEAD_DIM
K_END = Q_END + N_KV_HEADS * HEAD_DIM
V_END = K_END + N_KV_HEADS * HEAD_DIM
Z_END = V_END + 2 * SGU_WIDTH
IN_COLS = Z_END + 2 * D_MODEL
P_Z, P_G, P_Q, P_K, P_V = 0, 2048, 4096, 5120, 5248

ADAM_LR = 0.001
ADAM_B1 = 0.9
ADAM_B2 = 0.999
ADAM_EPS = 1e-08
ADAM_WD = 0.01
ADAM_STEP = 10

VMEM_LIMIT_BYTES = 56 * 1024 * 1024

BIG = ("w_in", "proj_a", "proj_b", "w_out", "ffn_w_gate", "ffn_w_up", "ffn_w_down")
COL_SHARDED = ("w_in", "ffn_w_gate", "ffn_w_up")
BIG_SHAPE = {"w_in": (D_MODEL, IN_COLS), "proj_a": (SGU_WIDTH, D_MODEL), "proj_b": (Q_END, D_MODEL),
             "w_out": (D_MODEL, D_MODEL), "ffn_w_gate": (D_MODEL, FFN_DIM), "ffn_w_up": (D_MODEL, FFN_DIM),
             "ffn_w_down": (FFN_DIM, D_MODEL)}
BIG_ROWS = {n: BIG_SHAPE[n][0] * BIG_SHAPE[n][1] // N_DEV // 1024 for n in BIG}


def _pcall(body, **kw):
    return pl.pallas_call(body, **kw)


def _params(**kw):
    return pltpu.CompilerParams(vmem_limit_bytes=VMEM_LIMIT_BYTES, **kw)


def _tile(n, cap, unit=128):
    if n <= cap:
        return n
    best = 0
    t = unit
    while t <= cap:
        if n % t == 0:
            best = t
        t += unit
    assert best, (n, cap, unit)
    return best


def _mm(a, b, *, nt, out_dtype, name, res=None, gvec=None, after=None, tm=None, tn_cap=1024):
    a_list = list(a) if isinstance(a, (list, tuple)) else [a]
    b_list = list(b) if isinstance(b, (list, tuple)) else [b]
    a, b = a_list[0], b_list[0]
    M, K = a.shape
    N = b.shape[0] if nt else b.shape[1]
    k_total = sum(x.shape[1] for x in a_list)
    tm = _tile(M, tm or (1024 if k_total <= 1024 else 512), 8)
    tn = _tile(N, tn_cap)
    dn = (((1,), (1,)), ((), ())) if nt else (((1,), (0,)), ((), ()))

    def b_spec_of(x):
        k = x.shape[1] if nt else x.shape[0]
        return pl.BlockSpec((tn, k), lambda i, j: (j, 0)) if nt else pl.BlockSpec((k, tn), lambda i, j: (0, j))
    b_spec = b_spec_of(b)
    o_spec = pl.BlockSpec((tm, tn), lambda i, j: (i, j))
    if res is None:
        extra = [] if after is None else [after]
        n = len(a_list)

        def body(*refs):
            o_ref = refs[-1]
            acc = None
            for a_ref, b_ref in zip(refs[:n], refs[n:2 * n]):
                d = lax.dot_general(a_ref[...].astype(BF), b_ref[...].astype(BF), dn, preferred_element_type=F32)
                acc = d if acc is None else acc + d
            o_ref[...] = acc.astype(out_dtype)
        return _pcall(body, name=name, grid=(M // tm, N // tn),
                      in_specs=[pl.BlockSpec((tm, x.shape[1]), lambda i, j: (i, 0)) for x in a_list]
                      + [b_spec_of(x) for x in b_list] + [ANY] * len(extra), out_specs=o_spec,
                      out_shape=jax.ShapeDtypeStruct((M, N), out_dtype), compiler_params=_params())(
                          *a_list, *b_list, *extra)

    def body_res(a_ref, b_ref, r_ref, g_ref, o_ref, acc_ref):
        acc = lax.dot_general(a_ref[...].astype(BF), b_ref[...].astype(BF), dn, preferred_element_type=F32)
        acc_ref[...] = acc.astype(BF)
        o_ref[...] = r_ref[...] + g_ref[...] * acc
    return _pcall(body_res, name=name, grid=(M // tm, N // tn),
                  in_specs=[pl.BlockSpec((tm, K), lambda i, j: (i, 0)), b_spec, o_spec,
                            pl.BlockSpec((1, tn), lambda i, j: (0, j))],
                  out_specs=[o_spec, o_spec],
                  out_shape=[jax.ShapeDtypeStruct((M, N), F32), jax.ShapeDtypeStruct((M, N), BF)],
                  compiler_params=_params())(a, b, res, gvec)


def _mm_tn(a, b, *, name, out_dtype=BF, tk=2048, tm_cap=1408, tn_cap=1024):
    S, M = a.shape
    N = b.shape[1]
    tm = _tile(M, tm_cap)
    tn = _tile(N, tn_cap)
    if 2 * 2 * S * (tm + tn) <= VMEM_LIMIT_BYTES * 3 // 5:
        tk = S
    tk = _tile(S, tk, 8)
    nk = S // tk

    def body(a_ref, b_ref, o_ref, acc_ref):
        k = pl.program_id(2)

        @pl.when(k == 0)
        def _():
            acc_ref[...] = jnp.zeros_like(acc_ref)
        acc_ref[...] += lax.dot_general(a_ref[...].astype(BF), b_ref[...].astype(BF), (((0,), (0,)), ((), ())),
                                        preferred_element_type=F32)

        @pl.when(k == nk - 1)
        def _():
            o_ref[...] = acc_ref[...].astype(out_dtype)
    return _pcall(body, name=name, grid=(M // tm, N // tn, nk),
                  in_specs=[pl.BlockSpec((tk, tm), lambda i, j, k: (k, i)),
                            pl.BlockSpec((tk, tn), lambda i, j, k: (k, j))],
                  out_specs=pl.BlockSpec((tm, tn), lambda i, j, k: (i, j)),
                  out_shape=jax.ShapeDtypeStruct((M, N), out_dtype), scratch_shapes=[pltpu.VMEM((tm, tn), F32)],
                  compiler_params=_params())(a, b)


def _rms(x, w):
    return x * lax.rsqrt(jnp.mean(x * x, axis=-1, keepdims=True) + NORM_EPS) * w


def _normmod_fn(x, nw, sc, sh):
    return _rms(x, nw) * (1.0 + sc) + sh


def _gelu(x):
    return 0.5 * x * (1.0 + lax.erf(x * (2.0 ** -0.5)))


def _ln_gelu_fn(zv, w, b):
    v = _gelu(zv)
    mu = jnp.mean(v, axis=-1, keepdims=True)
    var = jnp.mean(jnp.square(v - mu), axis=-1, keepdims=True)
    return (v - mu) * lax.rsqrt(var + NORM_EPS) * w + b


def _sigmoid(x):
    return 1.0 / (1.0 + jnp.exp(-x))


def _row_spec(tm, n):
    return pl.BlockSpec((tm, n), lambda i: (i, 0))


def _vec_spec(n):
    return pl.BlockSpec((1, n), lambda i: (0, 0))


def _acc(ref, val):
    @pl.when(pl.program_id(0) == 0)
    def _():
        ref[...] = jnp.zeros_like(ref)
    ref[...] += val


def _norm_mm(x, nw, sc, sh, ws, *, name, after=None, tm=1024, tn_cap=768):
    S, K = x.shape
    N = ws[0].shape[0]
    tm = _tile(S, tm, 8)
    tn = _tile(N, tn_cap)
    nw_, ne = len(ws), 0 if after is None else 1

    def body(x_ref, nw_ref, sc_ref, sh_ref, *rest):
        w_refs = rest[:nw_]
        h_ref = rest[nw_ + ne]
        o_refs = rest[nw_ + ne + 1:nw_ + ne + 1 + nw_]
        h_s = rest[-1]

        @pl.when(pl.program_id(1) == 0)
        def _():
            hv = _normmod_fn(x_ref[...], nw_ref[...], sc_ref[...], sh_ref[...]).astype(BF)
            h_s[...] = hv
            h_ref[...] = hv
        for w_ref, o_ref in zip(w_refs, o_refs):
            o_ref[...] = lax.dot_general(h_s[...], w_ref[...], (((1,), (1,)), ((), ())),
                                         preferred_element_type=F32).astype(BF)
    row = pl.BlockSpec((tm, K), lambda i, j: (i, 0))
    vec = pl.BlockSpec((1, K), lambda i, j: (0, 0))
    out = pl.BlockSpec((tm, tn), lambda i, j: (i, j))
    res = _pcall(body, name=name, grid=(S // tm, N // tn),
                 in_specs=[row, vec, vec, vec] + [pl.BlockSpec((tn, K), lambda i, j: (j, 0))] * nw_ + [ANY] * ne,
                 out_specs=[row] + [out] * nw_,
                 out_shape=[jax.ShapeDtypeStruct((S, K), BF)] + [jax.ShapeDtypeStruct((S, N), BF)] * nw_,
                 scratch_shapes=[pltpu.VMEM((tm, K), BF)], compiler_params=_params())(
                     x, nw, sc, sh, *ws, *([] if after is None else [after]))
    return res[0], list(res[1:])


def _gate_bwd(dxv, o_ref, g_ref, do_ref, dg_ref):
    do_ref[...] = (dxv * g_ref[...]).astype(BF)
    _acc(dg_ref, jnp.sum(dxv * o_ref[...].astype(F32), axis=0, keepdims=True))


def _normmod_bwd(dh, x, nw, sc, sh, dres, gate, *, name, tm=512):
    S, Dm = x.shape
    tm = _tile(S, tm, 8)
    ng = 0 if gate is None else 2

    def body(dh_ref, x_ref, nw_ref, sc_ref, sh_ref, dres_ref, *rest):
        dx_ref, dnw_ref, dsc_ref, dsh_ref = rest[ng:ng + 4]
        xv, dy = x_ref[...], dh_ref[...]
        r = lax.rsqrt(jnp.mean(xv * xv, axis=-1, keepdims=True) + NORM_EPS)
        xn = xv * r
        t = dy * xn
        a = nw_ref[...] * (1.0 + sc_ref[...])
        dxv = dres_ref[...] + r * (dy * a - xn * jnp.mean(t * a, axis=-1, keepdims=True))
        dx_ref[...] = dxv
        ts = jnp.sum(t, axis=0, keepdims=True)
        _acc(dnw_ref, ts * (1.0 + sc_ref[...]))
        _acc(dsc_ref, ts * nw_ref[...])
        _acc(dsh_ref, jnp.sum(dy, axis=0, keepdims=True))
        if gate is not None:
            _gate_bwd(dxv, rest[0], rest[1], rest[ng + 4], rest[ng + 5])
    vec = jax.ShapeDtypeStruct((1, Dm), F32)
    gate_in = [] if gate is None else [_row_spec(tm, Dm), _vec_spec(Dm)]
    gate_out = [] if gate is None else [_row_spec(tm, Dm), _vec_spec(Dm)]
    gate_shape = [] if gate is None else [jax.ShapeDtypeStruct((S, Dm), BF), vec]
    return _pcall(body, name=name, grid=(S // tm,),
                  in_specs=[_row_spec(tm, Dm), _row_spec(tm, Dm), _vec_spec(Dm), _vec_spec(Dm), _vec_spec(Dm),
                            _row_spec(tm, Dm)] + gate_in,
                  out_specs=[_row_spec(tm, Dm), _vec_spec(Dm), _vec_spec(Dm), _vec_spec(Dm)] + gate_out,
                  out_shape=[jax.ShapeDtypeStruct((S, Dm), F32), vec, vec, vec] + gate_shape,
                  compiler_params=_params())(dh, x, nw, sc, sh, dres, *([] if gate is None else gate))


def _head(x, fw, target, gate, *, tm=512):
    S, Dm = x.shape
    tm = _tile(S, tm, 8)

    def body(x_ref, fw_ref, t_ref, o_ref, g_ref, dx_ref, dfw_ref, loss_ref, do_ref, dg_ref):
        xv, w = x_ref[...], fw_ref[...]
        r = lax.rsqrt(jnp.mean(xv * xv, axis=-1, keepdims=True) + NORM_EPS)
        xn = xv * r
        err = xn * w - t_ref[...]
        dy = err * (1.0 / Dm)
        t = dy * xn
        dx = r * (dy * w - xn * jnp.mean(t * w, axis=-1, keepdims=True))
        dx_ref[...] = dx
        _acc(dfw_ref, jnp.sum(t, axis=0, keepdims=True))
        part = 0.5 * jnp.sum(jnp.mean(err * err, axis=-1, keepdims=True), axis=0, keepdims=True)
        _acc(loss_ref, jnp.broadcast_to(part, (8, 128)))
        _gate_bwd(dx, o_ref, g_ref, do_ref, dg_ref)
    vec = jax.ShapeDtypeStruct((1, Dm), F32)
    return _pcall(body, name="head", grid=(S // tm,),
                  in_specs=[_row_spec(tm, Dm), _vec_spec(Dm), _row_spec(tm, Dm), _row_spec(tm, Dm), _vec_spec(Dm)],
                  out_specs=[_row_spec(tm, Dm), _vec_spec(Dm), pl.BlockSpec((8, 128), lambda i: (0, 0)),
                             _row_spec(tm, Dm), _vec_spec(Dm)],
                  out_shape=[jax.ShapeDtypeStruct((S, Dm), F32), vec, jax.ShapeDtypeStruct((8, 128), F32),
                             jax.ShapeDtypeStruct((S, Dm), BF), vec],
                  compiler_params=_params())(x, fw, target, *gate)


def _tril_mask():
    r = lax.broadcasted_iota(jnp.int32, (SGU_CHUNK, SGU_CHUNK), 0)
    c = lax.broadcasted_iota(jnp.int32, (SGU_CHUNK, SGU_CHUNK), 1)
    return c <= r


def _sgu_fwd(proj, lnw, lnb, w, b_t, *, name, after=None, tm=512):
    S = proj.shape[0]
    tm = _tile(S, tm, SGU_CHUNK)
    extra = [] if after is None else [after]

    def body(zu_ref, zv_ref, lnw_ref, lnb_ref, w_ref, bt_ref, *rest):
        o_ref = rest[-1]
        u = _gelu(zu_ref[...].astype(F32))
        vn = _ln_gelu_fn(zv_ref[...].astype(F32), lnw_ref[...], lnb_ref[...]).astype(BF)
        mask = _tril_mask()
        for g in range(SGU_GROUPS):
            wm = jnp.where(mask, w_ref[g], 0.0).astype(BF)
            cols = slice(g * 128, (g + 1) * 128)
            for ci in range(tm // SGU_CHUNK):
                rows = slice(ci * SGU_CHUNK, (ci + 1) * SGU_CHUNK)
                f = jnp.dot(wm, vn[rows, cols], preferred_element_type=F32) + bt_ref[:, g:g + 1]
                o_ref[rows, cols] = (u[rows, cols] * f).astype(BF)
    return _pcall(body, name=name, grid=(S // tm,),
                  in_specs=[pl.BlockSpec((tm, SGU_WIDTH), lambda i: (i, 0)), pl.BlockSpec((tm, SGU_WIDTH), lambda i: (i, 1)),
                            _vec_spec(SGU_WIDTH), _vec_spec(SGU_WIDTH),
                            pl.BlockSpec((SGU_GROUPS, 128, 128), lambda i: (0, 0, 0)),
                            pl.BlockSpec((128, SGU_GROUPS), lambda i: (0, 0))] + [ANY] * len(extra),
                  out_specs=_row_spec(tm, SGU_WIDTH), out_shape=jax.ShapeDtypeStruct((S, SGU_WIDTH), BF),
                  compiler_params=_params())(proj, proj, lnw, lnb, w, b_t, *extra)


def _sgu_bwd(dy, proj, lnw, lnb, w, b_t, dproj, *, name, tm=512):
    S = proj.shape[0]
    tm = _tile(S, tm, SGU_CHUNK)

    def body(dy_ref, zu_ref, zv_ref, lnw_ref, lnb_ref, w_ref, bt_ref, _, dz_ref, dlnw_ref, dlnb_ref, dw_ref, dbt_ref,
             f_s, dvn_s):
        first = pl.program_id(0) == 0

        @pl.when(first)
        def _():
            dw_ref[...] = jnp.zeros_like(dw_ref)
            dbt_ref[...] = jnp.zeros_like(dbt_ref)
        u, vjp_u = jax.vjp(_gelu, zu_ref[...].astype(F32))
        vn, vjp_v = jax.vjp(_ln_gelu_fn, zv_ref[...].astype(F32), lnw_ref[...], lnb_ref[...])
        vn = vn.astype(BF)
        dy_v = dy_ref[...]
        df = (dy_v * u).astype(BF)
        mask = _tril_mask()
        for g in range(SGU_GROUPS):
            wm = jnp.where(mask, w_ref[g], 0.0).astype(BF)
            cols = slice(g * 128, (g + 1) * 128)
            dwg = jnp.zeros((128, 128), F32)
            dbg = jnp.zeros((128, 1), F32)
            for ci in range(tm // SGU_CHUNK):
                rows = slice(ci * SGU_CHUNK, (ci + 1) * SGU_CHUNK)
                vn_c = vn[rows, cols]
                df_c = df[rows, cols]
                f_s[rows, cols] = jnp.dot(wm, vn_c, preferred_element_type=F32) + bt_ref[:, g:g + 1]
                dvn_s[rows, cols] = lax.dot_general(wm, df_c, (((0,), (0,)), ((), ())), preferred_element_type=F32)
                dwg = dwg + lax.dot_general(df_c, vn_c, (((1,), (1,)), ((), ())), preferred_element_type=F32)
                dbg = dbg + jnp.sum((dy_v[rows, cols] * u[rows, cols]), axis=1, keepdims=True)
            dw_ref[g] += jnp.where(mask, dwg, 0.0)
            dbt_ref[:, g:g + 1] += dbg
        (dzu,) = vjp_u(dy_v * f_s[...])
        dzv, dlnw, dlnb = vjp_v(dvn_s[...])
        dz_ref[:, :SGU_WIDTH] = dzu.astype(BF)
        dz_ref[:, SGU_WIDTH:] = dzv.astype(BF)
        _acc(dlnw_ref, dlnw)
        _acc(dlnb_ref, dlnb)
    vec = jax.ShapeDtypeStruct((1, SGU_WIDTH), F32)
    return _pcall(body, name=name, grid=(S // tm,),
                  in_specs=[_row_spec(tm, SGU_WIDTH),
                            pl.BlockSpec((tm, SGU_WIDTH), lambda i: (i, 0)), pl.BlockSpec((tm, SGU_WIDTH), lambda i: (i, 1)),
                            _vec_spec(SGU_WIDTH), _vec_spec(SGU_WIDTH),
                            pl.BlockSpec((SGU_GROUPS, 128, 128), lambda i: (0, 0, 0)),
                            pl.BlockSpec((128, SGU_GROUPS), lambda i: (0, 0)), ANY],
                  out_specs=[pl.BlockSpec((tm, 2 * SGU_WIDTH), lambda i: (i, P_Z // (2 * SGU_WIDTH))),
                             _vec_spec(SGU_WIDTH), _vec_spec(SGU_WIDTH),
                             pl.BlockSpec((SGU_GROUPS, 128, 128), lambda i: (0, 0, 0)),
                             pl.BlockSpec((128, SGU_GROUPS), lambda i: (0, 0))],
                  out_shape=[jax.ShapeDtypeStruct(dproj.shape, BF), vec, vec,
                             jax.ShapeDtypeStruct((SGU_GROUPS, 128, 128), F32),
                             jax.ShapeDtypeStruct((128, SGU_GROUPS), F32)],
                  scratch_shapes=[pltpu.VMEM((tm, SGU_WIDTH), F32), pltpu.VMEM((tm, SGU_WIDTH), F32)],
                  input_output_aliases={7: 0},
                  compiler_params=_params())(dy, proj, proj, lnw, lnb, w, b_t, dproj)


def _merge_fwd(y_sgu, y_attn, pa, pb, proj, *, name, after=None, tm=1024, tn=512):
    S, Dm = y_sgu.shape
    tm = _tile(S, tm, 8)
    nj = Dm // tn
    extra = [] if after is None else [after]

    def body(ys_ref, ya_ref, pa_ref, pb_ref, ga_ref, gb_ref, *rest):
        a_ref, b_ref, m_ref = rest[-3:]
        a = jnp.dot(ys_ref[...], pa_ref[...], preferred_element_type=F32)
        b = jnp.dot(ya_ref[...], pb_ref[...], preferred_element_type=F32)
        a_ref[...] = a.astype(BF)
        b_ref[...] = b.astype(BF)
        m_ref[...] = (_sigmoid(ga_ref[...].astype(F32)) * a + _sigmoid(gb_ref[...].astype(F32)) * b).astype(BF)
    row = pl.BlockSpec((tm, Dm), lambda i, j: (i, 0))
    col = pl.BlockSpec((Dm, tn), lambda i, j: (0, j))
    out = pl.BlockSpec((tm, tn), lambda i, j: (i, j))
    sh = jax.ShapeDtypeStruct((S, Dm), BF)
    return _pcall(body, name=name, grid=(S // tm, nj),
                  in_specs=[row, row, col, col, pl.BlockSpec((tm, tn), lambda i, j: (i, P_G // tn + j)),
                            pl.BlockSpec((tm, tn), lambda i, j: (i, (P_G + Dm) // tn + j))] + [ANY] * len(extra),
                  out_specs=[out, out, out], out_shape=[sh, sh, sh],
                  compiler_params=_params())(y_sgu, y_attn, pa, pb, proj, proj, *extra)


def _merge_bwd(do, w_out, a, b, proj, *, name, after=None, tm=512):
    S, Dm = a.shape
    tm = _tile(S, tm, 8)
    ga_blk, gb_blk = P_G // Dm, P_G // Dm + 1
    extra = [] if after is None else [after]

    def body(do_ref, w_ref, a_ref, b_ref, ga_ref, gb_ref, *rest):
        da_ref, db_ref, dg_ref = rest[-3:]
        dmv = lax.dot_general(do_ref[...], w_ref[...], (((1,), (1,)), ((), ())), preferred_element_type=F32)
        sa = _sigmoid(ga_ref[...].astype(F32))
        sb = _sigmoid(gb_ref[...].astype(F32))
        da_ref[...] = (dmv * sa).astype(BF)
        db_ref[...] = (dmv * sb).astype(BF)
        dg_ref[:, :Dm] = (dmv * a_ref[...].astype(F32) * sa * (1.0 - sa)).astype(BF)
        dg_ref[:, Dm:] = (dmv * b_ref[...].astype(F32) * sb * (1.0 - sb)).astype(BF)
    return _pcall(body, name=name, grid=(S // tm,),
                  in_specs=[_row_spec(tm, Dm), pl.BlockSpec((Dm, Dm), lambda i: (0, 0)), _row_spec(tm, Dm), _row_spec(tm, Dm),
                            pl.BlockSpec((tm, Dm), lambda i: (i, ga_blk)), pl.BlockSpec((tm, Dm), lambda i: (i, gb_blk))]
                  + [ANY] * len(extra),
                  out_specs=[_row_spec(tm, Dm), _row_spec(tm, Dm), pl.BlockSpec((tm, 2 * Dm), lambda i: (i, P_G // (2 * Dm)))],
                  out_shape=[jax.ShapeDtypeStruct((S, Dm), BF), jax.ShapeDtypeStruct((S, Dm), BF),
                             jax.ShapeDtypeStruct((S, IN_COLS), BF)],
                  compiler_params=_params())(do, w_out, a, b, proj, proj, *extra)


def _shift_rows(a, halo, k, up):
    n = a.shape[0]
    r8 = lax.broadcasted_iota(jnp.int32, (8, a.shape[1]), 0)
    if not up:
        rolled = pltpu.roll(a, k, 0)
        patch = jnp.where(r8 < k, pltpu.roll(halo, k, 0), rolled[:8])
        return jnp.concatenate([patch, rolled[8:]], axis=0)
    rolled = pltpu.roll(a, n - k, 0)
    patch = jnp.where(r8 >= 8 - k, pltpu.roll(halo, 8 - k, 0), rolled[n - 8:])
    return jnp.concatenate([rolled[:n - 8], patch], axis=0)


def _conv_taps(a, halo):
    return _shift_rows(a, halo, 2, False), _shift_rows(a, halo, 1, False), a


HALO = 16


def _prev_halo_spec(tm, Fd):
    return pl.BlockSpec((HALO, Fd), lambda i: (jnp.maximum(i * (tm // HALO) - 1, 0), 0))


def _conv_fwd(a_ref, halo_ref, cw_ref, cb_ref):
    halo = jnp.where(pl.program_id(0) > 0, halo_ref[...].astype(F32)[HALO - 8:], 0.0)
    t0, t1, t2 = _conv_taps(a_ref[...].astype(F32), halo)
    return t0, t1, t2, cb_ref[...] + cw_ref[0:1, :] * t0 + cw_ref[1:2, :] * t1 + cw_ref[2:3, :] * t2


def _ffn_act_fwd(a, up, cw, cb, *, name, tm=256):
    S, Fd = a.shape
    tm = _tile(S, tm, HALO)

    def body(a_ref, up_ref, halo_ref, cw_ref, cb_ref, o_ref, ac_ref):
        _, _, _, ac = _conv_fwd(a_ref, halo_ref, cw_ref, cb_ref)
        ac_ref[...] = ac.astype(BF)
        o_ref[...] = (ac * _sigmoid(ac) * up_ref[...].astype(F32)).astype(BF)
    sh = jax.ShapeDtypeStruct((S, Fd), BF)
    return _pcall(body, name=name, grid=(S // tm,),
                  in_specs=[_row_spec(tm, Fd), _row_spec(tm, Fd), _prev_halo_spec(tm, Fd),
                            pl.BlockSpec((3, Fd), lambda i: (0, 0)), _vec_spec(Fd)],
                  out_specs=[_row_spec(tm, Fd), _row_spec(tm, Fd)], out_shape=[sh, sh],
                  compiler_params=_params())(a, up, a, cw, cb)


def _ffn_act_bwd_a(dhf, ac, up, *, name, tm=512):
    S, Fd = ac.shape
    tm = _tile(S, tm, HALO)

    def body(dhf_ref, ac_ref, up_ref, dac_ref, dup_ref, dcb_ref):
        acv = ac_ref[...].astype(F32)
        s = _sigmoid(acv)
        dhf_v = dhf_ref[...].astype(F32)
        dup_ref[...] = (dhf_v * acv * s).astype(BF)
        dac = dhf_v * up_ref[...].astype(F32) * (s * (1.0 + acv * (1.0 - s)))
        dac_ref[...] = dac.astype(BF)
        _acc(dcb_ref, jnp.sum(dac, axis=0, keepdims=True))
    sh = jax.ShapeDtypeStruct((S, Fd), BF)
    return _pcall(body, name=name, grid=(S // tm,), in_specs=[_row_spec(tm, Fd)] * 3,
                  out_specs=[_row_spec(tm, Fd), _row_spec(tm, Fd), _vec_spec(Fd)],
                  out_shape=[sh, sh, jax.ShapeDtypeStruct((1, Fd), F32)], compiler_params=_params())(dhf, ac, up)


def _ffn_act_bwd_b(dac, a, cw, *, name, tm=256):
    S, Fd = dac.shape
    tm = _tile(S, tm, HALO)
    last = S // tm - 1

    def body(d_ref, halo_ref, a_ref, cw_ref, o_ref, dcw_ref):
        halo = jnp.where(pl.program_id(0) < last, halo_ref[...].astype(F32)[:8], 0.0)
        d = d_ref[...].astype(F32)
        d1, d2 = _shift_rows(d, halo, 1, True), _shift_rows(d, halo, 2, True)
        o_ref[...] = (cw_ref[2:3, :] * d + cw_ref[1:2, :] * d1 + cw_ref[0:1, :] * d2).astype(BF)
        av = a_ref[...].astype(F32)
        _acc(dcw_ref, jnp.concatenate([jnp.sum(av * d2, axis=0, keepdims=True),
                                       jnp.sum(av * d1, axis=0, keepdims=True),
                                       jnp.sum(av * d, axis=0, keepdims=True)], axis=0))
    return _pcall(body, name=name, grid=(S // tm,),
                  in_specs=[_row_spec(tm, Fd),
                            pl.BlockSpec((HALO, Fd), lambda i: (jnp.minimum((i + 1) * (tm // HALO), S // HALO - 1), 0)),
                            _row_spec(tm, Fd), pl.BlockSpec((3, Fd), lambda i: (0, 0))],
                  out_specs=[_row_spec(tm, Fd), pl.BlockSpec((3, Fd), lambda i: (0, 0))],
                  out_shape=[jax.ShapeDtypeStruct((S, Fd), BF), jax.ShapeDtypeStruct((3, Fd), F32)],
                  compiler_params=_params())(dac, dac, a, cw)


def _rope_tables(pos_col, inv_row, m1_row, m2_row):
    S = pos_col.shape[0]
    tm = _tile(S, 512, 8)

    def body(p_ref, inv_ref, m1_ref, m2_ref, c_ref, s1_ref, s2_ref):
        ang = p_ref[...] * inv_ref[...]
        sn = jnp.sin(ang)
        c_ref[...] = jnp.cos(ang)
        s1_ref[...] = -sn * m1_ref[...]
        s2_ref[...] = sn * m2_ref[...]
    sh = jax.ShapeDtypeStruct((S, 128), F32)
    return _pcall(body, name="rope_tables", grid=(S // tm,),
                  in_specs=[pl.BlockSpec((tm, 1), lambda i: (i, 0)), _vec_spec(128), _vec_spec(128), _vec_spec(128)],
                  out_specs=[_row_spec(tm, 128)] * 3, out_shape=[sh, sh, sh], compiler_params=_params())(
                      pos_col, inv_row, m1_row, m2_row)


def _rope_apply(x, c, s1, s2):
    outs = []
    for j in range(x.shape[1] // 128):
        xj = x[:, j * 128:(j + 1) * 128]
        outs.append(xj * c + pltpu.roll(xj, 120, 1) * s1 + pltpu.roll(xj, 8, 1) * s2)
    return outs[0] if len(outs) == 1 else jnp.concatenate(outs, axis=1)


def _rope_apply_t(d, c, s1, s2):
    outs = []
    for j in range(d.shape[1] // 128):
        dj = d[:, j * 128:(j + 1) * 128]
        outs.append(dj * c + pltpu.roll(dj * s1, 8, 1) + pltpu.roll(dj * s2, 120, 1))
    return outs[0] if len(outs) == 1 else jnp.concatenate(outs, axis=1)


def _rope_fwd(proj, c, s1, s2, *, name, tm=512):
    S = proj.shape[0]
    tm = _tile(S, tm, 8)

    def body(q_ref, k_ref, v_ref, c_ref, s1_ref, s2_ref, qo_ref, ko_ref, vo_ref):
        cv, s1v, s2v = c_ref[...], s1_ref[...], s2_ref[...]
        qo_ref[...] = (_rope_apply(q_ref[...].astype(F32), cv, s1v, s2v) * (HEAD_DIM ** -0.5)).astype(BF)
        ko_ref[...] = _rope_apply(k_ref[...].astype(F32), cv, s1v, s2v).astype(BF)
        vo_ref[...] = v_ref[...].astype(BF)
    return _pcall(body, name=name, grid=(S // tm,),
                  in_specs=[pl.BlockSpec((tm, Q_END), lambda i: (i, P_Q // Q_END)),
                            pl.BlockSpec((tm, 128), lambda i: (i, P_K // 128)),
                            pl.BlockSpec((tm, 128), lambda i: (i, P_V // 128)),
                            _row_spec(tm, 128), _row_spec(tm, 128), _row_spec(tm, 128)],
                  out_specs=[_row_spec(tm, Q_END), _row_spec(tm, 128), _row_spec(tm, 128)],
                  out_shape=[jax.ShapeDtypeStruct((S, Q_END), BF), jax.ShapeDtypeStruct((S, 128), BF),
                             jax.ShapeDtypeStruct((S, 128), BF)],
                  compiler_params=_params())(proj, proj, proj, c, s1, s2)


def _rope_bwd(dq, dk, dv, c, s1, s2, dproj, *, name, tm=512):
    S = dq.shape[0]
    tm = _tile(S, tm, 8)
    tabs = [_row_spec(tm, 128)] * 3
    shape = jax.ShapeDtypeStruct(dproj.shape, BF)

    def body_q(dq_ref, c_ref, s1_ref, s2_ref, _, o_ref):
        o_ref[...] = _rope_apply_t(dq_ref[...].astype(F32), c_ref[...], s1_ref[...], s2_ref[...]).astype(BF)
    dproj = _pcall(body_q, name=name + "_q", grid=(S // tm,), in_specs=[_row_spec(tm, Q_END)] + tabs + [ANY],
                   out_specs=pl.BlockSpec((tm, Q_END), lambda i: (i, P_Q // Q_END)), out_shape=shape,
                   input_output_aliases={4: 0}, compiler_params=_params())(dq, c, s1, s2, dproj)

    def body_kv(dk_ref, dv_ref, c_ref, s1_ref, s2_ref, _, o_ref):
        o_ref[:, :128] = _rope_apply_t(dk_ref[...], c_ref[...], s1_ref[...], s2_ref[...]).astype(BF)
        o_ref[:, 128:] = dv_ref[...].astype(BF)
    return _pcall(body_kv, name=name + "_kv", grid=(S // tm,),
                  in_specs=[_row_spec(tm, 128), _row_spec(tm, 128)] + tabs + [ANY],
                  out_specs=pl.BlockSpec((tm, 256), lambda i: (i, P_K // 256)), out_shape=shape,
                  input_output_aliases={5: 0}, compiler_params=_params())(dk, dv, c, s1, s2, dproj)


def _lane_lo(shape):
    return lax.broadcasted_iota(jnp.int32, shape, 1) < HEAD_DIM


def _stack_heads(x, g):
    lo = _lane_lo((ATTN_BLOCK, 128))
    zero = jnp.zeros((ATTN_BLOCK, 128), x.dtype)
    parts = []
    for p in range(Q_PER_KV // 2):
        xp = x[:, (g * 4 + p) * 128:(g * 4 + p + 1) * 128]
        parts += [jnp.where(lo, xp, zero), jnp.where(lo, zero, xp)]
    return jnp.concatenate(parts, axis=0)


def _unstack_heads(o2):
    lo = _lane_lo((ATTN_BLOCK, 128))
    return [jnp.where(lo, o2[2 * p * ATTN_BLOCK:(2 * p + 1) * ATTN_BLOCK], o2[(2 * p + 1) * ATTN_BLOCK:(2 * p + 2) * ATTN_BLOCK])
            for p in range(Q_PER_KV // 2)]


def _dup_half(prev, cur, g):
    x = jnp.concatenate([prev, cur], axis=0).astype(F32)
    lo = _lane_lo(x.shape)
    r = pltpu.roll(x, HEAD_DIM, 1)
    return (jnp.where(lo, x, r) if g == 0 else jnp.where(lo, r, x)).astype(BF)


def _fold_halves(x):
    return x + pltpu.roll(x, HEAD_DIM, 1)


def _attn_bias():
    i = lax.broadcasted_iota(jnp.int32, (Q_PER_KV * ATTN_BLOCK, 2 * ATTN_BLOCK), 0) & (ATTN_BLOCK - 1)
    j = lax.broadcasted_iota(jnp.int32, (Q_PER_KV * ATTN_BLOCK, 2 * ATTN_BLOCK), 1)
    band = (j > i) & (j <= i + ATTN_BLOCK)
    return jnp.stack([jnp.where(band & (j >= ATTN_BLOCK), 0.0, -jnp.inf), jnp.where(band, 0.0, -jnp.inf)]).astype(F32)


def _both(x):
    return jnp.concatenate([x, x], axis=1)


def _row_sums(x_bf):
    return jnp.dot(x_bf, jnp.ones((x_bf.shape[1], 128), BF), preferred_element_type=F32)


def _attn_probs(qs, kb, sink, bias):
    s = lax.dot_general(qs, kb, (((1,), (1,)), ((), ())), preferred_element_type=F32) + bias
    m = jnp.maximum(jnp.broadcast_to(jnp.max(s, axis=-1, keepdims=True), sink.shape), sink)
    return jnp.exp(s - _both(m)), jnp.exp(sink - m)


def _attn_specs(S):
    nb = S // ATTN_BLOCK
    qs = pl.BlockSpec((ATTN_BLOCK, Q_END), lambda n: (n, 0))
    cur = pl.BlockSpec((ATTN_BLOCK, 128), lambda n: (n, 0))
    prev = pl.BlockSpec((ATTN_BLOCK, 128), lambda n: (jnp.maximum(n - 1, 0), 0))
    sink = pl.BlockSpec((N_KV_HEADS, Q_PER_KV * ATTN_BLOCK, 128), lambda n: (0, 0, 0))
    bias = pl.BlockSpec((None, Q_PER_KV * ATTN_BLOCK, 2 * ATTN_BLOCK), lambda n: (jnp.minimum(n, 1), 0, 0))
    return nb, qs, cur, prev, sink, bias


def _attn_fwd(q, k, v, sink_rows, bias, *, name):
    S = q.shape[0]
    nb, qs, cur, prev, sink, bs = _attn_specs(S)

    def body(q_ref, kp_ref, kc_ref, vp_ref, vc_ref, sk_ref, b_ref, o_ref):
        for g in range(N_KV_HEADS):
            kb = _dup_half(kp_ref[...], kc_ref[...], g)
            vb = _dup_half(vp_ref[...], vc_ref[...], g)
            p, es = _attn_probs(_stack_heads(q_ref[...], g), kb, sk_ref[g], b_ref[...])
            ones = jnp.ones((2 * ATTN_BLOCK, 128), BF)
            o3 = jnp.dot(p.astype(BF), jnp.concatenate([vb, ones], axis=1), preferred_element_type=F32)
            o2 = o3[:, :128] / (o3[:, 128:] + es)
            for t, tile in enumerate(_unstack_heads(o2)):
                o_ref[:, (g * 4 + t) * 128:(g * 4 + t + 1) * 128] = tile.astype(BF)
    return _pcall(body, name=name, grid=(nb,), in_specs=[qs, prev, cur, prev, cur, sink, bs], out_specs=qs,
                  out_shape=jax.ShapeDtypeStruct(q.shape, BF), compiler_params=_params())(q, k, k, v, v, sink_rows, bias)


def _attn_bwd(do, q, k, v, sink_rows, bias, *, name):
    S = q.shape[0]
    nb, qs, cur, prev, sink, bs = _attn_specs(S)
    full = pl.BlockSpec((S, 128), lambda n: (0, 0))
    dsk_spec = pl.BlockSpec((N_KV_HEADS, Q_PER_KV, 128), lambda n: (0, 0, 0))

    def body(do_ref, q_ref, kp_ref, kc_ref, vp_ref, vc_ref, sk_ref, b_ref, dq_ref, dk_ref, dv_ref, dsk_ref):
        n = pl.program_id(0)

        @pl.when(n == 0)
        def _():
            dk_ref[...] = jnp.zeros_like(dk_ref)
            dv_ref[...] = jnp.zeros_like(dv_ref)
            dsk_ref[...] = jnp.zeros_like(dsk_ref)
        sub = lax.broadcasted_iota(jnp.int32, (Q_PER_KV, 128), 0)
        dkf, dvf = [], []
        for g in range(N_KV_HEADS):
            qst = _stack_heads(q_ref[...], g)
            dos = _stack_heads(do_ref[...], g)
            kb = _dup_half(kp_ref[...], kc_ref[...], g)
            vb = _dup_half(vp_ref[...], vc_ref[...], g)
            pu, es = _attn_probs(qst, kb, sk_ref[g], b_ref[...])
            inv = 1.0 / (_row_sums(pu.astype(BF)) + es)
            p = pu * _both(inv)
            dp = lax.dot_general(dos, vb, (((1,), (1,)), ((), ())), preferred_element_type=F32)
            dd = _row_sums((p * dp).astype(BF))
            ds = (p * (dp - _both(dd))).astype(BF)
            dq2 = jnp.dot(ds, kb, preferred_element_type=F32) * (HEAD_DIM ** -0.5)
            for t, tile in enumerate(_unstack_heads(dq2)):
                dq_ref[:, (g * 4 + t) * 128:(g * 4 + t + 1) * 128] = tile.astype(BF)
            dkf.append(_fold_halves(lax.dot_general(ds, qst, (((0,), (0,)), ((), ())), preferred_element_type=F32)))
            dvf.append(_fold_halves(lax.dot_general(p.astype(BF), dos, (((0,), (0,)), ((), ())),
                                                    preferred_element_type=F32)))
            dsr = -(es * inv * dd)
            upd = jnp.zeros((Q_PER_KV, 128), F32)
            for h in range(Q_PER_KV):
                upd = jnp.where(sub == h, jnp.sum(dsr[h * ATTN_BLOCK:(h + 1) * ATTN_BLOCK], axis=0, keepdims=True), upd)
            dsk_ref[g] += upd
        lo = _lane_lo((2 * ATTN_BLOCK, 128))
        dkb = jnp.where(lo, dkf[0], dkf[1])
        dvb = jnp.where(lo, dvf[0], dvf[1])
        r0 = pl.multiple_of(n * ATTN_BLOCK, ATTN_BLOCK)
        dk_ref[pl.ds(r0, ATTN_BLOCK), :] += dkb[ATTN_BLOCK:]
        dv_ref[pl.ds(r0, ATTN_BLOCK), :] += dvb[ATTN_BLOCK:]

        @pl.when(n > 0)
        def _():
            rp = pl.multiple_of((n - 1) * ATTN_BLOCK, ATTN_BLOCK)
            dk_ref[pl.ds(rp, ATTN_BLOCK), :] += dkb[:ATTN_BLOCK]
            dv_ref[pl.ds(rp, ATTN_BLOCK), :] += dvb[:ATTN_BLOCK]
    return _pcall(body, name=name, grid=(nb,), in_specs=[qs, qs, prev, cur, prev, cur, sink, bs],
                  out_specs=[qs, full, full, dsk_spec],
                  out_shape=[jax.ShapeDtypeStruct(q.shape, BF), jax.ShapeDtypeStruct((S, 128), F32),
                             jax.ShapeDtypeStruct((S, 128), F32), jax.ShapeDtypeStruct((N_KV_HEADS, Q_PER_KV, 128), F32)],
                  compiler_params=_params())(do, q, k, k, v, v, sink_rows, bias)


def _ada_fwd(c_all, ada_w):
    ncol = ada_w.shape[2]

    def body(c_ref, w_ref, o_ref):
        cv = c_ref[...]
        ca = (cv * _sigmoid(cv)).astype(BF)
        for l in range(DEPTH):
            o_ref[:, l * ncol:(l + 1) * ncol] = jnp.dot(ca, w_ref[l].astype(BF), preferred_element_type=F32)
    return _pcall(body, name="ada_fwd", out_shape=jax.ShapeDtypeStruct((N_DEV, DEPTH * ncol), F32),
                  compiler_params=_params())(c_all, ada_w)


def _ada_bwd(c_all, dm):
    ncol = dm.shape[2]

    def body(c_ref, dm_ref, o_ref):
        cv = c_ref[...]
        ca = (cv * _sigmoid(cv)).astype(BF)
        for l in range(DEPTH):
            o_ref[l] = lax.dot_general(ca, dm_ref[l].astype(BF), (((0,), (0,)), ((), ())), preferred_element_type=F32)
    return _pcall(body, name="ada_bwd", out_shape=jax.ShapeDtypeStruct((DEPTH, D_MODEL, ncol), F32),
                  compiler_params=_params())(c_all, dm)


def _adamw(w, g, m, v, *, name):
    R, C = w.shape
    tr = R
    for t in range(8, 513, 8):
        if R % t == 0:
            tr = t
    c1 = 1.0 - ADAM_B1 ** ADAM_STEP
    c2 = 1.0 - ADAM_B2 ** ADAM_STEP

    def body(w_ref, g_ref, m_ref, v_ref, d_ref, mo_ref, vo_ref):
        gv = g_ref[...]
        mn = ADAM_B1 * m_ref[...] + (1.0 - ADAM_B1) * gv
        vn = ADAM_B2 * v_ref[...] + (1.0 - ADAM_B2) * (gv * gv)
        mo_ref[...] = mn
        vo_ref[...] = vn
        d_ref[...] = -ADAM_LR * ((mn * (1.0 / c1)) / (jnp.sqrt(vn * (1.0 / c2)) + ADAM_EPS) + ADAM_WD * w_ref[...])
    spec = pl.BlockSpec((tr, C), lambda i: (i, 0))
    sh = jax.ShapeDtypeStruct((R, C), F32)
    return _pcall(body, name=name, grid=(R // tr,), in_specs=[spec] * 4, out_specs=[spec] * 3, out_shape=[sh, sh, sh],
                  compiler_params=_params())(w, g, m, v)


def _adamw_layers(w, g_layers, m, v, *, name):
    L, R, C = w.shape
    assert L == 2 and len(g_layers) == 2
    tr = R
    for t in range(8, 513, 8):
        if R % t == 0:
            tr = t
    c1 = 1.0 - ADAM_B1 ** ADAM_STEP
    c2 = 1.0 - ADAM_B2 ** ADAM_STEP

    def body(w_ref, g0_ref, g1_ref, m_ref, v_ref, go_ref, d_ref, mo_ref, vo_ref):
        gv = jnp.where(pl.program_id(0) == 0, g0_ref[...], g1_ref[...])
        go_ref[...] = gv
        mn = ADAM_B1 * m_ref[...] + (1.0 - ADAM_B1) * gv
        vn = ADAM_B2 * v_ref[...] + (1.0 - ADAM_B2) * (gv * gv)
        mo_ref[...] = mn
        vo_ref[...] = vn
        d_ref[...] = -ADAM_LR * ((mn * (1.0 / c1)) / (jnp.sqrt(vn * (1.0 / c2)) + ADAM_EPS) + ADAM_WD * w_ref[...])
    spec = pl.BlockSpec((None, tr, C), lambda l, i: (l, i, 0))
    sh = jax.ShapeDtypeStruct((L, R, C), F32)
    g_specs = [pl.BlockSpec((tr, C), lambda l, i, k=k: (jnp.where(l == k, i, 0), 0)) for k in range(L)]
    return _pcall(body, name=name, grid=(L, R // tr), in_specs=[spec] + g_specs + [spec, spec], out_specs=[spec] * 4,
                  out_shape=[sh] * 4, compiler_params=_params())(w, *g_layers, m, v)


def _sum8(parts, *, name):
    _, R, C = parts.shape
    tr = _tile(R, 512, 16)

    def body(p_ref, o_ref):
        acc = p_ref[0].astype(F32)
        for k in range(1, N_DEV):
            acc = acc + p_ref[k].astype(F32)
        o_ref[...] = acc
    return _pcall(body, name=name, grid=(R // tr,), in_specs=[pl.BlockSpec((N_DEV, tr, C), lambda i: (0, i, 0))],
                  out_specs=pl.BlockSpec((tr, C), lambda i: (i, 0)), out_shape=jax.ShapeDtypeStruct((R, C), F32),
                  compiler_params=_params())(parts)


MESH_ID = pl.DeviceIdType.MESH
ANY = pl.BlockSpec(memory_space=pl.ANY)


def _all_gather(x, *, name, after=None):
    R, C = x.shape
    extra = [] if after is None else [after]

    def body(x_ref, *rest):
        out_ref, send_sems, recv_sems, local_sem = rest[-4:]
        mx, my, mc = lax.axis_index("x"), lax.axis_index("y"), lax.axis_index("c")
        me, sibling = (mx, my, mc), (mx, my, 1 - mc)
        chips = [(1 - mx, my), (mx, 1 - my), (1 - mx, 1 - my)]

        def blk(px, py, pc):
            return out_ref.at[4 * px + 2 * py + pc]

        def copy(k, block, to, src=None):
            return pltpu.make_async_remote_copy(
                src_ref=blk(*block) if src is None else src, dst_ref=blk(*block),
                send_sem=send_sems.at[k], recv_sem=recv_sems.at[k], device_id=to, device_id_type=MESH_ID)

        mine = pltpu.make_async_copy(x_ref, blk(*me), local_sem)
        mine.start()
        first = [copy(0, me, sibling, src=x_ref)]
        first += [copy(1 + j, me, (*chip, mc), src=x_ref) for j, chip in enumerate(chips)]
        for cp in first:
            cp.start()
        passed = [copy(4 + j, (*chip, mc), sibling) for j, chip in enumerate(chips)]
        for j, chip in enumerate(chips):
            copy(1 + j, (*chip, mc), me).wait_recv()
            passed[j].start()
        copy(0, sibling, me).wait_recv()
        for j, chip in enumerate(chips):
            copy(4 + j, (*chip, 1 - mc), me).wait_recv()
        for cp in first + passed:
            cp.wait_send()
        mine.wait()
    return _pcall(body, name=name, in_specs=[ANY] * (1 + len(extra)), out_specs=ANY,
                  out_shape=jax.ShapeDtypeStruct((N_DEV, R, C), x.dtype),
                  scratch_shapes=[pltpu.SemaphoreType.DMA((7,)), pltpu.SemaphoreType.DMA((7,)), pltpu.SemaphoreType.DMA],
                  compiler_params=pltpu.CompilerParams(has_side_effects=True))(x, *extra)


HBM_SPEC = pl.BlockSpec(memory_space=pltpu.HBM)
SEM_SPEC = pl.BlockSpec(memory_space=pltpu.SEMAPHORE)
DATAFLOW = pltpu.SideEffectType.DATAFLOW_SIDE_EFFECTING


def _coords():
    return lax.axis_index("x"), lax.axis_index("y"), lax.axis_index("c")


def _other_chips(mx, my):
    return [(1 - mx, my), (mx, 1 - my), (1 - mx, 1 - my)]


def _plan_gather_ici(refs, send, recv):
    src, land = refs
    mx, my, mc = _coords()
    return [pltpu.make_async_remote_copy(src_ref=src, dst_ref=land.at[mc, 2 * mx + my], send_sem=send[j], recv_sem=recv[j],
                                         device_id=(px, py, mc), device_id_type=MESH_ID)
            for j, (px, py) in enumerate(_other_chips(mx, my))]


def _plan_gather_d2d(refs, send, recv):
    (land,) = refs
    mx, my, mc = _coords()
    return [pltpu.make_async_remote_copy(src_ref=land.at[mc], dst_ref=land.at[mc], send_sem=send[0], recv_sem=recv[0],
                                         device_id=(mx, my, 1 - mc), device_id_type=MESH_ID)]


def _plan_reduce_d2d(refs, send, recv):
    g, land = refs
    mx, my, mc = _coords()
    return [pltpu.make_async_remote_copy(src_ref=g.at[1 - mc], dst_ref=land, send_sem=send[0], recv_sem=recv[0],
                                         device_id=(mx, my, 1 - mc), device_id_type=MESH_ID)]


def _plan_reduce_ici(refs, send, recv):
    h, land = refs
    mx, my, mc = _coords()
    return [pltpu.make_async_remote_copy(src_ref=h.at[2 * px + py], dst_ref=land.at[j], send_sem=send[j], recv_sem=recv[j],
                                         device_id=(px, py, mc), device_id_type=MESH_ID)
            for j, (px, py) in enumerate(_other_chips(mx, my))]


def _rdma_start(bufs, n, plan, *, name, after=None):
    nb = len(bufs)
    extra = [] if after is None else [after]
    ne = len(extra)

    def body(*refs):
        ins, send, recv = refs[:nb], refs[nb + ne:nb + ne + n], refs[nb + ne + n:nb + ne + 2 * n]
        token = refs[-1]
        for cp in plan(ins, send, recv):
            cp.start()
        token[...] = jnp.zeros_like(token)
    out = _pcall(body, name=name,
                 out_shape=tuple([pltpu.SemaphoreType.DMA(())] * (2 * n) + [pltpu.HBM(b.shape, b.dtype) for b in bufs]
                                 + [jax.ShapeDtypeStruct((8, 128), F32)]),
                 in_specs=tuple([HBM_SPEC] * nb + [ANY] * ne),
                 out_specs=tuple([SEM_SPEC] * (2 * n) + [HBM_SPEC] * nb + [pl.BlockSpec(memory_space=pltpu.VMEM)]),
                 input_output_aliases={i: 2 * n + i for i in range(nb)},
                 compiler_params=pltpu.CompilerParams(has_side_effects=DATAFLOW))(
                     *[pltpu.with_memory_space_constraint(b, pltpu.HBM) for b in bufs], *extra)
    return list(out[:2 * n]), list(out[2 * n:2 * n + nb]), out[-1]


def _rdma_wait(sems, bufs, n, plan, after, *, name):
    nb = len(bufs)

    def body(*refs):
        ins, send, recv = refs[:nb], refs[nb:nb + n], refs[nb + n:nb + 2 * n]
        for cp in plan(ins, send, recv):
            cp.wait_send()
            cp.wait_recv()
    out = _pcall(body, name=name, out_shape=tuple(pltpu.HBM(b.shape, b.dtype) for b in bufs),
                 in_specs=tuple([HBM_SPEC] * nb + [SEM_SPEC] * (2 * n) + [ANY]), out_specs=tuple([HBM_SPEC] * nb),
                 input_output_aliases={i: i for i in range(nb)},
                 compiler_params=pltpu.CompilerParams(has_side_effects=DATAFLOW))(*bufs, *sems, after)
    return list(out)


def _sum_pair(g, land, cidx, *, name):
    _, nchip, R, C = g.shape
    tr = _tile(R, 1056, 16)

    def body(c_ref, g_ref, l_ref, o_ref):
        o_ref[...] = g_ref[...] + l_ref[...]
    grid_spec = pltpu.PrefetchScalarGridSpec(
        num_scalar_prefetch=1, grid=(nchip, R // tr),
        in_specs=[pl.BlockSpec((None, None, tr, C), lambda p, i, c_ref: (c_ref[0], p, i, 0)),
                  pl.BlockSpec((None, tr, C), lambda p, i, c_ref: (p, i, 0))],
        out_specs=pl.BlockSpec((None, tr, C), lambda p, i, c_ref: (p, i, 0)))
    return _pcall(body, name=name, grid_spec=grid_spec, out_shape=jax.ShapeDtypeStruct((nchip, R, C), BF),
                  compiler_params=_params())(cidx, g, land)


def _sum_chips(h, land, chipidx, *, name):
    _, R, C = h.shape
    tr = _tile(R, 1056, 16)

    def body(c_ref, h_ref, l_ref, o_ref):
        acc = h_ref[...].astype(F32)
        for j in range(3):
            acc = acc + l_ref[j].astype(F32)
        o_ref[...] = acc
    grid_spec = pltpu.PrefetchScalarGridSpec(
        num_scalar_prefetch=1, grid=(R // tr,),
        in_specs=[pl.BlockSpec((None, tr, C), lambda i, c_ref: (c_ref[0], i, 0)),
                  pl.BlockSpec((3, tr, C), lambda i, c_ref: (0, i, 0))],
        out_specs=pl.BlockSpec((tr, C), lambda i, c_ref: (i, 0)))
    return _pcall(body, name=name, grid_spec=grid_spec, out_shape=jax.ShapeDtypeStruct((R, C), F32),
                  compiler_params=_params())(chipidx, h, land)


PART_IN = ("w_in",)
PART_MIX = ("proj_a", "proj_b", "w_out")
PART_FFN = ("ffn_w_gate", "ffn_w_up", "ffn_w_down")


def _part_rows(names):
    return sum(BIG_ROWS[n] for n in names)


def _part_offsets(names):
    off, r = {}, 0
    for n in names:
        off[n] = r
        r += BIG_ROWS[n]
    return off


def _pack_shards(shards, l, names):
    return jnp.concatenate([(shards[n][l].T if n in COL_SHARDED else shards[n][l]).astype(BF) for n in names], axis=0)


def _unpack_weights(full8, names):
    off = _part_offsets(names)

    def whole(n):
        return full8[:, off[n]:off[n] + BIG_ROWS[n], :].reshape(N_DEV * BIG_ROWS[n], 1024)
    out = {}
    if "w_in" in names:
        wt_in = whole("w_in")
        out["wt_in"] = jnp.concatenate([wt_in[V_END:], wt_in[:V_END]], axis=0)
    for n in ("proj_a", "proj_b", "w_out"):
        if n in names:
            out[n] = whole(n)
    if "ffn_w_gate" in names:
        out["wt_gate"], out["wt_up"], out["w_down"] = whole("ffn_w_gate"), whole("ffn_w_up"), whole("ffn_w_down")
    return out


def _from_land(land):
    return land.transpose(1, 0, 2, 3).reshape(N_DEV, land.shape[2], 1024)


def _pack_grads(wg, names):
    full = {"proj_a": wg.get("proj_a"), "proj_b": wg.get("proj_b"), "w_out": wg.get("w_out"), "ffn_w_down": wg.get("w_down"),
            "ffn_w_gate": wg.get("wt_gate"), "ffn_w_up": wg.get("wt_up")}
    if "w_in" in names:
        full["w_in"] = jnp.concatenate([wg["wt_in"][P_Q:], wg["wt_in"][:P_Q]], axis=0)
    blocks = jnp.concatenate([full[n].reshape(N_DEV, BIG_ROWS[n], 1024) for n in names], axis=1)
    return blocks.reshape(4, 2, _part_rows(names), 1024).transpose(1, 0, 2, 3)


def _unpack_shard_grads(gs, names):
    off = _part_offsets(names)
    return {n: gs[off[n]:off[n] + BIG_ROWS[n]] for n in names}


def _rope_setup(positions):
    S = positions.shape[0]
    inv = ROPE_THETA ** (-jnp.arange(0, ROT_DIM, 2, dtype=F32) / ROT_DIM)
    lane = np.arange(128) % HEAD_DIM
    half = ROT_DIM // 2
    inv_row = jnp.where(lane < ROT_DIM, jnp.tile(inv, 128 // half), 0.0)[None, :].astype(F32)
    m1_row = jnp.asarray((lane < half).astype(np.float32))[None, :]
    m2_row = jnp.asarray(((lane >= half) & (lane < ROT_DIM)).astype(np.float32))[None, :]
    return (*_rope_tables(positions.astype(F32).reshape(S, 1), inv_row, m1_row, m2_row), _attn_bias())


def _hook(hooks, point, after):
    f = None if hooks is None else hooks.get(point)
    return None if f is None else f(after)


def _layer_fwd(l, x, mod_l, W, small, rope, hooks=None):
    rc, rs1, rs2, bias = rope
    sh1, sc1, g1, sh2, sc2, g2 = [mod_l[i * D_MODEL:(i + 1) * D_MODEL][None, :] for i in range(6)]
    nw1, nw2 = small["norm1_w"][l][None, :], small["norm2_w"][l][None, :]
    tok = _hook(hooks, "mm_in", x)
    h, (proj,) = _norm_mm(x, nw1, sc1, sh1, [W["wt_in"]], name=f"mm_in{l}", after=tok, tm=2048, tn_cap=768)
    q_r, k_r, v_b = _rope_fwd(proj, rc, rs1, rs2, name=f"rope_fwd{l}")
    sink_rows = jnp.repeat(small["attn_sinks"][l].reshape(N_KV_HEADS, Q_PER_KV), ATTN_BLOCK, axis=1)
    sink_rows = jnp.broadcast_to(sink_rows[..., None], sink_rows.shape + (128,))
    y_attn = _attn_fwd(q_r, k_r, v_b, sink_rows, bias, name=f"attn_fwd{l}")
    lnw, lnb = small["sgu_ln_w"][l][None, :], small["sgu_ln_b"][l][None, :]
    sgu_bt = small["sgu_b"][l].T
    y_sgu = _sgu_fwd(proj, lnw, lnb, small["sgu_w"][l], sgu_bt, name=f"sgu_fwd{l}", after=_hook(hooks, "sgu", y_attn))
    tok = _hook(hooks, "mm_pa", y_sgu)
    a_br, b_br, merged = _merge_fwd(y_sgu, y_attn, W["proj_a"], W["proj_b"], proj, name=f"merge_fwd{l}", after=tok)
    x1, o1 = _mm(merged, W["w_out"], nt=False, out_dtype=F32, name=f"mm_out{l}", res=x, gvec=g1)
    tok = _hook(hooks, "mm_gu", x1)
    h2, (a_g, a_u) = _norm_mm(x1, nw2, sc2, sh2, [W["wt_gate"], W["wt_up"]], name=f"mm_gu{l}", after=tok, tn_cap=1408)
    cw, cb = small["ffn_conv_w"][l], small["ffn_conv_b"][l][None, :]
    hf, a_c = _ffn_act_fwd(a_g, a_u, cw, cb, name=f"ffn_act_fwd{l}")
    x2, o2 = _mm(hf, W["w_down"], nt=False, out_dtype=F32, name=f"mm_down{l}", res=x1, gvec=g2)
    saved = dict(x=x, h=h, proj=proj, q_r=q_r, k_r=k_r, v_b=v_b, sink_rows=sink_rows, y_attn=y_attn, y_sgu=y_sgu,
                 a_br=a_br, b_br=b_br, merged=merged, x1=x1, o1=o1, h2=h2, a_g=a_g, a_u=a_u, a_c=a_c, hf=hf, o2=o2)
    return x2, saved


def _layer_bwd(l, dx, do2, dg2, mod_l, W, small, rope, sv, below=None, hooks=None, wg=None):
    rc, rs1, rs2, bias = rope
    sh1, sc1, g1, sh2, sc2, g2 = [mod_l[i * D_MODEL:(i + 1) * D_MODEL][None, :] for i in range(6)]
    nw1, nw2 = small["norm1_w"][l][None, :], small["norm2_w"][l][None, :]
    cw = small["ffn_conv_w"][l]
    lnw, lnb = small["sgu_ln_w"][l][None, :], small["sgu_ln_b"][l][None, :]
    sgu_bt = small["sgu_b"][l].T
    wg = {} if wg is None else wg
    dhf = _mm(do2, W["w_down"], nt=True, out_dtype=BF, name=f"mm_down_dx{l}", after=_hook(hooks, "mm_down_dx", do2),
              tn_cap=1408)
    wg["w_down"] = _mm_tn(sv["hf"], do2, name=f"mm_down_dw{l}")
    dac, dup, dcb = _ffn_act_bwd_a(dhf, sv["a_c"], sv["a_u"], name=f"ffn_act_bwd_a{l}")
    da, dcw = _ffn_act_bwd_b(dac, sv["a_g"], cw, name=f"ffn_act_bwd_b{l}")
    dh2 = _mm([da, dup], [W["wt_gate"], W["wt_up"]], nt=False, out_dtype=F32, name=f"mm_gu_dx{l}",
              after=_hook(hooks, "mm_gu_dx", da))
    wg["wt_gate"] = _mm_tn(da, sv["h2"], name=f"mm_gate_dw{l}")
    wg["wt_up"] = _mm_tn(dup, sv["h2"], name=f"mm_up_dw{l}")
    dx1, dnw2, dsc2, dsh2, do1, dg1 = _normmod_bwd(dh2, sv["x1"], nw2, sc2, sh2, dx, (sv["o1"], g1), name=f"normmod2_bwd{l}")
    d_a, d_b, dproj = _merge_bwd(do1, W["w_out"], sv["a_br"], sv["b_br"], sv["proj"], name=f"merge_bwd{l}",
                                 after=_hook(hooks, "merge_bwd", do1))
    wg["w_out"] = _mm_tn(sv["merged"], do1, name=f"mm_out_dw{l}")
    dysgu = _mm(d_a, W["proj_a"], nt=True, out_dtype=F32, name=f"mm_pa_dx{l}", after=_hook(hooks, "mm_pa_dx", d_a))
    dyattn = _mm(d_b, W["proj_b"], nt=True, out_dtype=BF, name=f"mm_pb_dx{l}")
    wg["proj_a"] = _mm_tn(sv["y_sgu"], d_a, name=f"mm_pa_dw{l}")
    wg["proj_b"] = _mm_tn(sv["y_attn"], d_b, name=f"mm_pb_dw{l}")
    dproj, dlnw, dlnb, dsguw, dsgubt = _sgu_bwd(dysgu, sv["proj"], lnw, lnb, small["sgu_w"][l], sgu_bt, dproj,
                                                name=f"sgu_bwd{l}")
    dq_r, dk_r, dv_b, dsk = _attn_bwd(dyattn, sv["q_r"], sv["k_r"], sv["v_b"], sv["sink_rows"], bias, name=f"attn_bwd{l}")
    dproj = _rope_bwd(dq_r, dk_r, dv_b, rc, rs1, rs2, dproj, name=f"rope_bwd{l}")
    wg["wt_in"] = _mm_tn(dproj, sv["h"], name=f"mm_in_dw{l}")
    dh = _mm(dproj, W["wt_in"], nt=False, out_dtype=F32, name=f"mm_in_dx{l}", after=_hook(hooks, "mm_in_dx", wg["wt_in"]))
    dx0, dnw1, dsc1, dsh1, *gate_below = _normmod_bwd(dh, sv["x"], nw1, sc1, sh1, dx1, below, name=f"normmod1_bwd{l}")
    dmod = jnp.concatenate([dsh1, dsc1, dg1, dsh2, dsc2, dg2], axis=1)[0]
    sg = {"norm1_w": dnw1[0], "norm2_w": dnw2[0], "attn_sinks": dsk[:, :, 0].reshape(N_Q_HEADS),
          "sgu_ln_w": dlnw[0], "sgu_ln_b": dlnb[0], "sgu_w": dsguw, "sgu_b": dsgubt.T,
          "ffn_conv_w": dcw, "ffn_conv_b": dcb[0]}
    return (dx0, *gate_below), wg, sg, dmod


SMALL = ("ada_b", "norm1_w", "attn_sinks", "sgu_ln_w", "sgu_ln_b", "sgu_w", "sgu_b", "norm2_w", "ffn_conv_b", "final_norm_w")
WEIGHT_ORDER = ("ada_w", "ada_b", "norm1_w", "w_in", "attn_sinks", "sgu_ln_w", "sgu_ln_b", "sgu_w", "sgu_b", "proj_a", "proj_b",
                "w_out", "norm2_w", "ffn_w_gate", "ffn_w_up", "ffn_conv_w", "ffn_conv_b", "ffn_w_down", "final_norm_w")


def _flat_pack(arrs, rows):
    flat = jnp.concatenate([a.reshape(-1) for a in arrs])
    return jnp.pad(flat, (0, rows * 1024 - flat.shape[0])).reshape(rows, 1024)


def _flat_unpack(buf, shapes):
    flat = buf.reshape(-1)
    out, o = [], 0
    for s in shapes:
        n = int(np.prod(s))
        out.append(flat[o:o + n].reshape(s))
        o += n
    return out


def _adam2d(w, g, m, v, *, name):
    shp = w.shape
    r2 = (int(np.prod(shp[:-1])), shp[-1]) if len(shp) > 1 else (1, shp[0])
    d, mn, vn = _adamw(w.reshape(r2), g.reshape(r2), m.reshape(r2), v.reshape(r2), name=name)
    return d.reshape(shp), mn.reshape(shp), vn.reshape(shp)


def kernel(x, c, positions, ada_w, ada_b, norm1_w, w_in, attn_sinks, sgu_ln_w, sgu_ln_b, sgu_w, sgu_b, proj_a, proj_b, w_out, norm2_w, ffn_w_gate, ffn_w_up, ffn_conv_w, ffn_conv_b, ffn_w_down, final_norm_w, loss_target, m_ada_w, m_ada_b, m_norm1_w, m_w_in, m_attn_sinks, m_sgu_ln_w, m_sgu_ln_b, m_sgu_w, m_sgu_b, m_proj_a, m_proj_b, m_w_out, m_norm2_w, m_ffn_w_gate, m_ffn_w_up, m_ffn_conv_w, m_ffn_conv_b, m_ffn_w_down, m_final_norm_w, v_ada_w, v_ada_b, v_norm1_w, v_w_in, v_attn_sinks, v_sgu_ln_w, v_sgu_ln_b, v_sgu_w, v_sgu_b, v_proj_a, v_proj_b, v_w_out, v_norm2_w, v_ffn_w_gate, v_ffn_w_up, v_ffn_conv_w, v_ffn_conv_b, v_ffn_w_down, v_final_norm_w):
    wts = dict(ada_w=ada_w, ada_b=ada_b, norm1_w=norm1_w, w_in=w_in, attn_sinks=attn_sinks, sgu_ln_w=sgu_ln_w,
               sgu_ln_b=sgu_ln_b, sgu_w=sgu_w, sgu_b=sgu_b, proj_a=proj_a, proj_b=proj_b, w_out=w_out, norm2_w=norm2_w,
               ffn_w_gate=ffn_w_gate, ffn_w_up=ffn_w_up, ffn_conv_w=ffn_conv_w, ffn_conv_b=ffn_conv_b,
               ffn_w_down=ffn_w_down, final_norm_w=final_norm_w)
    mom = dict(ada_w=m_ada_w, ada_b=m_ada_b, norm1_w=m_norm1_w, w_in=m_w_in, attn_sinks=m_attn_sinks, sgu_ln_w=m_sgu_ln_w,
               sgu_ln_b=m_sgu_ln_b, sgu_w=m_sgu_w, sgu_b=m_sgu_b, proj_a=m_proj_a, proj_b=m_proj_b, w_out=m_w_out,
               norm2_w=m_norm2_w, ffn_w_gate=m_ffn_w_gate, ffn_w_up=m_ffn_w_up, ffn_conv_w=m_ffn_conv_w,
               ffn_conv_b=m_ffn_conv_b, ffn_w_down=m_ffn_w_down, final_norm_w=m_final_norm_w)
    var = dict(ada_w=v_ada_w, ada_b=v_ada_b, norm1_w=v_norm1_w, w_in=v_w_in, attn_sinks=v_attn_sinks, sgu_ln_w=v_sgu_ln_w,
               sgu_ln_b=v_sgu_ln_b, sgu_w=v_sgu_w, sgu_b=v_sgu_b, proj_a=v_proj_a, proj_b=v_proj_b, w_out=v_w_out,
               norm2_w=v_norm2_w, ffn_w_gate=v_ffn_w_gate, ffn_w_up=v_ffn_w_up, ffn_conv_w=v_ffn_conv_w,
               ffn_conv_b=v_ffn_conv_b, ffn_w_down=v_ffn_w_down, final_norm_w=v_final_norm_w)
    me = 4 * lax.axis_index("x") + 2 * lax.axis_index("y") + lax.axis_index("c")
    ada_cols = ada_w.shape[2]

    c_all = _all_gather(jnp.broadcast_to(c, (8, D_MODEL)), name="ag_c")[:, 0, :]
    prod = _ada_fwd(c_all, ada_w)
    prod_all = _all_gather(prod, name="ag_mod")
    mine = lax.dynamic_index_in_dim(prod_all, me, axis=1, keepdims=False)
    mod = jnp.stack([mine[:, l * ada_cols:(l + 1) * ada_cols].reshape(-1) for l in range(DEPTH)]) + ada_b

    conv_cols = ffn_conv_w.shape[2]
    conv_all = _all_gather(_flat_pack([ffn_conv_w], 8), name="ag_conv", after=mod)
    conv_full = jnp.stack([a.reshape(DEPTH, 3, conv_cols) for a in
                           [conv_all[j].reshape(-1)[:DEPTH * 3 * conv_cols] for j in range(N_DEV)]], axis=2)
    conv_full = conv_full.reshape(DEPTH, 3, FFN_DIM)
    small = {n: wts[n] for n in SMALL}
    small["ffn_conv_w"] = conv_full

    mx, my, mc = _coords()
    cidx = jnp.reshape(mc, (1,)).astype(jnp.int32)
    chipidx = jnp.reshape(2 * mx + my, (1,)).astype(jnp.int32)
    rope = _rope_setup(positions[0])

    class Gather:
        def __init__(self, src, tag):
            self.tag, self.src = tag, src
            self.land = lax.dynamic_update_slice(lax.empty((2, 4) + src.shape, src.dtype), src[None, None],
                                                 (mc, 2 * mx + my, 0, 0))

        def ici_start(self, after):
            self.sems, (self.src, self.land), tok = _rdma_start([self.src, self.land], 3, _plan_gather_ici,
                                                                name=f"ag_{self.tag}_ici_start", after=after)
            return tok

        def ici_wait_d2d_start(self, after):
            _, land = _rdma_wait(self.sems, [self.src, self.land], 3, _plan_gather_ici, after, name=f"ag_{self.tag}_ici_wait")
            self.sems, (self.land,), tok = _rdma_start([land], 1, _plan_gather_d2d, name=f"ag_{self.tag}_d2d_start")
            return tok

        def d2d_wait(self, after):
            (land,) = _rdma_wait(self.sems, [self.land], 1, _plan_gather_d2d, after, name=f"ag_{self.tag}_d2d_wait")
            return _from_land(land)

    def weights_job(names, l, tag):
        job = Gather(_pack_shards(wts, l, names), tag)
        job.weights = lambda after: _unpack_weights(job.d2d_wait(after), names)
        return job

    W0 = _unpack_weights(_all_gather(_pack_shards(wts, 0, PART_IN), name="ag_w0_in", after=conv_all), PART_IN)
    W1 = {}
    rest = PART_MIX + PART_FFN
    g_rest0 = weights_job(rest, 0, "w0_rest")
    g_in1, g_rest1 = weights_job(PART_IN, 1, "w1_in"), weights_job(rest, 1, "w1_rest")

    def rest0_then_layer1(after):
        W0.update(g_rest0.weights(after))
        return g_rest1.ici_start(g_in1.ici_start(W0["w_down"]))

    x1, sv0 = _layer_fwd(0, x[0], mod[0], W0, small, rope,
                         {"mm_in": lambda after: g_rest0.ici_start(W0["wt_in"]), "sgu": g_rest0.ici_wait_d2d_start,
                          "mm_pa": rest0_then_layer1, "mm_gu": g_in1.ici_wait_d2d_start})
    g_rest1.ici_wait_d2d_start(x1)
    x2, sv1 = _layer_fwd(1, x1, mod[1], W1, small, rope,
                         {"mm_in": lambda after: W1.update(g_in1.weights(after)),
                          "mm_pa": lambda after: W1.update(g_rest1.weights(after))})
    gate2 = [mod[l][5 * D_MODEL:][None, :] for l in range(DEPTH)]
    dx2, dfw, loss_tile, do2, dg2 = _head(x2, final_norm_w[None, :], loss_target[0], (sv1["o2"], gate2[1]))
    loss = lax.psum(loss_tile[0, 0], ("x", "y", "c"))

    class Reduce:
        def __init__(self, names, tag):
            self.names, self.tag, self.rows = names, tag, _part_rows(names)

        def d2d_start(self, wg, after=None):
            self.sems, self.bufs, tok = _rdma_start([_pack_grads(wg, self.names), lax.empty((4, self.rows, 1024), BF)], 1,
                                                    _plan_reduce_d2d, name=f"rs_{self.tag}_d2d_start", after=after)
            return tok

        def d2d_wait_ici_start(self, after):
            g_t, land_a = _rdma_wait(self.sems, self.bufs, 1, _plan_reduce_d2d, after, name=f"rs_{self.tag}_d2d_wait")
            h = _sum_pair(g_t, land_a, cidx, name=f"rs_{self.tag}_sum_pair")
            self.sems, self.bufs, tok = _rdma_start([h, lax.empty((3, self.rows, 1024), BF)], 3, _plan_reduce_ici,
                                                    name=f"rs_{self.tag}_ici_start")
            return tok

        def ici_wait(self, after):
            h_t, land_b = _rdma_wait(self.sems, self.bufs, 3, _plan_reduce_ici, after, name=f"rs_{self.tag}_ici_wait")
            return _unpack_shard_grads(_sum_chips(h_t, land_b, chipidx, name=f"rs_{self.tag}_sum_chips"), self.names)

    (dx1, do2, dg2), wg1, sg1, dmod1 = _layer_bwd(1, dx2, do2, dg2, mod[1], W1, small, rope, sv1, below=(sv0["o2"], gate2[0]))
    r_all1, r_ffn0, r_mix0 = Reduce(BIG, "g1"), Reduce(PART_FFN, "g0_ffn"), Reduce(PART_IN + PART_MIX, "g0_mix")
    tok1 = r_all1.d2d_start(wg1)
    wg0, shard1 = {}, {}

    def layer1_done_then_mix0(after):
        shard1.update(r_all1.ici_wait(after))
        return r_mix0.d2d_wait_ici_start(r_mix0.d2d_start(wg0, shard1["w_in"]))

    (grad_x,), _, sg0, dmod0 = _layer_bwd(
        0, dx1, do2, dg2, mod[0], W0, small, rope, sv0, wg=wg0,
        hooks={"mm_down_dx": lambda after: tok1, "mm_gu_dx": r_all1.d2d_wait_ici_start,
               "merge_bwd": lambda after: r_ffn0.d2d_start(wg0, after), "mm_pa_dx": r_ffn0.d2d_wait_ici_start,
               "mm_in_dx": layer1_done_then_mix0})
    sg = {n: jnp.stack([sg0[n], sg1[n]]) for n in sg0}
    sg["final_norm_w"] = dfw[0]
    dmod = jnp.stack([dmod0, dmod1])
    vec_names = [n for n in SMALL if n not in ("ada_b", "sgu_w")] + ["ffn_conv_w"]
    vec_shapes = [(DEPTH, 6 * D_MODEL)] + [sg[n].shape for n in vec_names]
    vec_rows = -(-sum(int(np.prod(s)) for s in vec_shapes) // 1024 // 16) * 16
    sgu_rows = sgu_w.size // 1024
    g_small = Gather(jnp.concatenate([_flat_pack([dmod] + [sg[n] for n in vec_names], vec_rows),
                                      sg["sgu_w"].reshape(sgu_rows, 1024)], axis=0).astype(BF), "small")
    tok = g_small.ici_start(grad_x)

    shard0 = r_ffn0.ici_wait(tok)
    shard0.update(r_mix0.ici_wait(shard0["ffn_w_down"]))
    grads, delta, new_m, new_v = {}, {}, {}, {}
    for n in BIG:
        view = (lambda a: jnp.swapaxes(a, 1, 2)) if n in COL_SHARDED else (lambda a: a)
        out = _adamw_layers(view(wts[n]), [shard0[n], shard1[n]], view(mom[n]), view(var[n]), name=f"adamw_{n}")
        grads[n], delta[n], new_m[n], new_v[n] = [view(o) for o in out]

    sm_all = g_small.d2d_wait(g_small.ici_wait_d2d_start(delta["ffn_w_gate"]))
    sm_sum = _sum8(sm_all, name="sum_small")
    vec_sum = _flat_unpack(sm_sum[:vec_rows], vec_shapes)
    grads["ada_b"] = vec_sum[0]
    for n, gsum in zip(vec_names, vec_sum[1:]):
        grads[n] = gsum
    grads["sgu_w"] = sm_sum[vec_rows:].reshape(sgu_w.shape)
    grads["ffn_conv_w"] = lax.dynamic_slice_in_dim(grads["ffn_conv_w"], me * conv_cols, conv_cols, axis=2)
    dmod_all = sm_all[:, :DEPTH * 6, :].astype(F32).reshape(N_DEV, DEPTH, 6 * D_MODEL)
    dm_mine = lax.dynamic_slice_in_dim(dmod_all, me * ada_cols, ada_cols, axis=2).transpose(1, 0, 2)
    dm_mine = jnp.pad(dm_mine, ((0, 0), (0, 8), (0, 0)))
    grads["ada_w"] = _ada_bwd(jnp.pad(c_all, ((0, 8), (0, 0))), dm_mine)

    for n in WEIGHT_ORDER:
        if n not in delta:
            delta[n], new_m[n], new_v[n] = _adam2d(wts[n], grads[n], mom[n], var[n], name=f"adamw_{n}")
    return (loss, grad_x[None], *[grads[n] for n in WEIGHT_ORDER], *[delta[n] for n in WEIGHT_ORDER],
            *[new_m[n] for n in WEIGHT_ORDER], *[new_v[n] for n in WEIGHT_ORDER])
```

```python
import jax
import jax.numpy as jnp
import numpy as np
from jax import lax
from jax.experimental import pallas as pl
from jax.experimental.pallas import tpu as pltpu

F32 = jnp.float32
BF = jnp.bfloat16

N_DEV = 8
D_MODEL = 1024
DEPTH = 2
N_Q_HEADS = 16
N_KV_HEADS = 2
HEAD_DIM = 64
Q_PER_KV = N_Q_HEADS // N_KV_HEADS
ATTN_BLOCK = 128
ROPE_THETA = 500000.0
ROT_DIM = HEAD_DIM // 4
SGU_WIDTH = 1024
SGU_GROUPS = 8
SGU_CHUNK = 128
FFN_DIM = 2816
NORM_EPS = 1e-6
Q_END = N_Q_HEADS * HEAD_DIM
K_END = Q_END + N_KV_HEADS * HEAD_DIM
V_END = K_END + N_KV_HEADS * HEAD_DIM
Z_END = V_END + 2 * SGU_WIDTH
IN_COLS = Z_END + 2 * D_MODEL
P_Z, P_G, P_Q, P_K, P_V = 0, 2048, 4096, 5120, 5248

ADAM_LR = 0.001
ADAM_B1 = 0.9
ADAM_B2 = 0.999
ADAM_EPS = 1e-08
ADAM_WD = 0.01
ADAM_STEP = 10

VMEM_LIMIT_BYTES = 56 * 1024 * 1024

BIG = ("w_in", "proj_a", "proj_b", "w_out", "ffn_w_gate", "ffn_w_up", "ffn_w_down")
COL_SHARDED = ("w_in", "ffn_w_gate", "ffn_w_up")
BIG_SHAPE = {"w_in": (D_MODEL, IN_COLS), "proj_a": (SGU_WIDTH, D_MODEL), "proj_b": (Q_END, D_MODEL),
             "w_out": (D_MODEL, D_MODEL), "ffn_w_gate": (D_MODEL, FFN_DIM), "ffn_w_up": (D_MODEL, FFN_DIM),
             "ffn_w_down": (FFN_DIM, D_MODEL)}
BIG_ROWS = {n: BIG_SHAPE[n][0] * BIG_SHAPE[n][1] // N_DEV // 1024 for n in BIG}


def _pcall(body, **kw):
    return pl.pallas_call(body, **kw)


def _params(**kw):
    return pltpu.CompilerParams(vmem_limit_bytes=VMEM_LIMIT_BYTES, **kw)


def _tile(n, cap, unit=128):
    if n <= cap:
        return n
    best = 0
    t = unit
    while t <= cap:
        if n % t == 0:
            best = t
        t += unit
    assert best, (n, cap, unit)
    return best


def _mm(a, b, *, nt, out_dtype, name, res=None, gvec=None, after=None, tm=None, tn_cap=1024):
    a_list = list(a) if isinstance(a, (list, tuple)) else [a]
    b_list = list(b) if isinstance(b, (list, tuple)) else [b]
    a, b = a_list[0], b_list[0]
    M, K = a.shape
    N = b.shape[0] if nt else b.shape[1]
    k_total = sum(x.shape[1] for x in a_list)
    tm = _tile(M, tm or (1024 if k_total <= 1024 else 512), 8)
    tn = _tile(N, tn_cap)
    dn = (((1,), (1,)), ((), ())) if nt else (((1,), (0,)), ((), ()))

    def b_spec_of(x):
        k = x.shape[1] if nt else x.shape[0]
        return pl.BlockSpec((tn, k), lambda i, j: (j, 0)) if nt else pl.BlockSpec((k, tn), lambda i, j: (0, j))
    b_spec = b_spec_of(b)
    o_spec = pl.BlockSpec((tm, tn), lambda i, j: (i, j))
    if res is None:
        extra = [] if after is None else [after]
        n = len(a_list)

        def body(*refs):
            o_ref = refs[-1]
            acc = None
            for a_ref, b_ref in zip(refs[:n], refs[n:2 * n]):
                d = lax.dot_general(a_ref[...].astype(BF), b_ref[...].astype(BF), dn, preferred_element_type=F32)
                acc = d if acc is None else acc + d
            o_ref[...] = acc.astype(out_dtype)
        return _pcall(body, name=name, grid=(M // tm, N // tn),
                      in_specs=[pl.BlockSpec((tm, x.shape[1]), lambda i, j: (i, 0)) for x in a_list]
                      + [b_spec_of(x) for x in b_list] + [ANY] * len(extra), out_specs=o_spec,
                      out_shape=jax.ShapeDtypeStruct((M, N), out_dtype), compiler_params=_params())(
                          *a_list, *b_list, *extra)

    def body_res(a_ref, b_ref, r_ref, g_ref, o_ref, acc_ref):
        acc = lax.dot_general(a_ref[...].astype(BF), b_ref[...].astype(BF), dn, preferred_element_type=F32)
        acc_ref[...] = acc.astype(BF)
        o_ref[...] = r_ref[...] + g_ref[...] * acc
    return _pcall(body_res, name=name, grid=(M // tm, N // tn),
                  in_specs=[pl.BlockSpec((tm, K), lambda i, j: (i, 0)), b_spec, o_spec,
                            pl.BlockSpec((1, tn), lambda i, j: (0, j))],
                  out_specs=[o_spec, o_spec],
                  out_shape=[jax.ShapeDtypeStruct((M, N), F32), jax.ShapeDtypeStruct((M, N), BF)],
                  compiler_params=_params())(a, b, res, gvec)


def _mm_tn(a, b, *, name, out_dtype=BF, tk=2048, tm_cap=1408, tn_cap=1024):
    S, M = a.shape
    N = b.shape[1]
    tm = _tile(M, tm_cap)
    tn = _tile(N, tn_cap)
    if 2 * 2 * S * (tm + tn) <= VMEM_LIMIT_BYTES * 3 // 5:
        tk = S
    tk = _tile(S, tk, 8)
    nk = S // tk

    def body(a_ref, b_ref, o_ref, acc_ref):
        k = pl.program_id(2)

        @pl.when(k == 0)
        def _():
            acc_ref[...] = jnp.zeros_like(acc_ref)
        acc_ref[...] += lax.dot_general(a_ref[...].astype(BF), b_ref[...].astype(BF), (((0,), (0,)), ((), ())),
                                        preferred_element_type=F32)

        @pl.when(k == nk - 1)
        def _():
            o_ref[...] = acc_ref[...].astype(out_dtype)
    return _pcall(body, name=name, grid=(M // tm, N // tn, nk),
                  in_specs=[pl.BlockSpec((tk, tm), lambda i, j, k: (k, i)),
                            pl.BlockSpec((tk, tn), lambda i, j, k: (k, j))],
                  out_specs=pl.BlockSpec((tm, tn), lambda i, j, k: (i, j)),
                  out_shape=jax.ShapeDtypeStruct((M, N), out_dtype), scratch_shapes=[pltpu.VMEM((tm, tn), F32)],
                  compiler_params=_params())(a, b)


def _rms(x, w):
    return x * lax.rsqrt(jnp.mean(x * x, axis=-1, keepdims=True) + NORM_EPS) * w


def _normmod_fn(x, nw, sc, sh):
    return _rms(x, nw) * (1.0 + sc) + sh


def _gelu(x):
    return 0.5 * x * (1.0 + lax.erf(x * (2.0 ** -0.5)))


def _ln_gelu_fn(zv, w, b):
    v = _gelu(zv)
    mu = jnp.mean(v, axis=-1, keepdims=True)
    var = jnp.mean(jnp.square(v - mu), axis=-1, keepdims=True)
    return (v - mu) * lax.rsqrt(var + NORM_EPS) * w + b


def _sigmoid(x):
    return 1.0 / (1.0 + jnp.exp(-x))


def _row_spec(tm, n):
    return pl.BlockSpec((tm, n), lambda i: (i, 0))


def _vec_spec(n):
    return pl.BlockSpec((1, n), lambda i: (0, 0))


def _acc(ref, val):
    @pl.when(pl.program_id(0) == 0)
    def _():
        ref[...] = jnp.zeros_like(ref)
    ref[...] += val


def _norm_mm(x, nw, sc, sh, ws, *, name, after=None, tm=1024, tn_cap=768):
    S, K = x.shape
    N = ws[0].shape[0]
    tm = _tile(S, tm, 8)
    tn = _tile(N, tn_cap)
    nw_, ne = len(ws), 0 if after is None else 1

    def body(x_ref, nw_ref, sc_ref, sh_ref, *rest):
        w_refs = rest[:nw_]
        h_ref = rest[nw_ + ne]
        o_refs = rest[nw_ + ne + 1:nw_ + ne + 1 + nw_]
        h_s = rest[-1]

        @pl.when(pl.program_id(1) == 0)
        def _():
            hv = _normmod_fn(x_ref[...], nw_ref[...], sc_ref[...], sh_ref[...]).astype(BF)
            h_s[...] = hv
            h_ref[...] = hv
        for w_ref, o_ref in zip(w_refs, o_refs):
            o_ref[...] = lax.dot_general(h_s[...], w_ref[...], (((1,), (1,)), ((), ())),
                                         preferred_element_type=F32).astype(BF)
    row = pl.BlockSpec((tm, K), lambda i, j: (i, 0))
    vec = pl.BlockSpec((1, K), lambda i, j: (0, 0))
    out = pl.BlockSpec((tm, tn), lambda i, j: (i, j))
    res = _pcall(body, name=name, grid=(S // tm, N // tn),
                 in_specs=[row, vec, vec, vec] + [pl.BlockSpec((tn, K), lambda i, j: (j, 0))] * nw_ + [ANY] * ne,
                 out_specs=[row] + [out] * nw_,
                 out_shape=[jax.ShapeDtypeStruct((S, K), BF)] + [jax.ShapeDtypeStruct((S, N), BF)] * nw_,
                 scratch_shapes=[pltpu.VMEM((tm, K), BF)], compiler_params=_params())(
                     x, nw, sc, sh, *ws, *([] if after is None else [after]))
    return res[0], list(res[1:])


def _gate_bwd(dxv, o_ref, g_ref, do_ref, dg_ref):
    do_ref[...] = (dxv * g_ref[...]).astype(BF)
    _acc(dg_ref, jnp.sum(dxv * o_ref[...].astype(F32), axis=0, keepdims=True))


def _normmod_bwd(dh, x, nw, sc, sh, dres, gate, *, name, tm=512):
    S, Dm = x.shape
    tm = _tile(S, tm, 8)
    ng = 0 if gate is None else 2

    def body(dh_ref, x_ref, nw_ref, sc_ref, sh_ref, dres_ref, *rest):
        dx_ref, dnw_ref, dsc_ref, dsh_ref = rest[ng:ng + 4]
        xv, dy = x_ref[...], dh_ref[...]
        r = lax.rsqrt(jnp.mean(xv * xv, axis=-1, keepdims=True) + NORM_EPS)
        xn = xv * r
        t = dy * xn
        a = nw_ref[...] * (1.0 + sc_ref[...])
        dxv = dres_ref[...] + r * (dy * a - xn * jnp.mean(t * a, axis=-1, keepdims=True))
        dx_ref[...] = dxv
        ts = jnp.sum(t, axis=0, keepdims=True)
        _acc(dnw_ref, ts * (1.0 + sc_ref[...]))
        _acc(dsc_ref, ts * nw_ref[...])
        _acc(dsh_ref, jnp.sum(dy, axis=0, keepdims=True))
        if gate is not None:
            _gate_bwd(dxv, rest[0], rest[1], rest[ng + 4], rest[ng + 5])
    vec = jax.ShapeDtypeStruct((1, Dm), F32)
    gate_in = [] if gate is None else [_row_spec(tm, Dm), _vec_spec(Dm)]
    gate_out = [] if gate is None else [_row_spec(tm, Dm), _vec_spec(Dm)]
    gate_shape = [] if gate is None else [jax.ShapeDtypeStruct((S, Dm), BF), vec]
    return _pcall(body, name=name, grid=(S // tm,),
                  in_specs=[_row_spec(tm, Dm), _row_spec(tm, Dm), _vec_spec(Dm), _vec_spec(Dm), _vec_spec(Dm),
                            _row_spec(tm, Dm)] + gate_in,
                  out_specs=[_row_spec(tm, Dm), _vec_spec(Dm), _vec_spec(Dm), _vec_spec(Dm)] + gate_out,
                  out_shape=[jax.ShapeDtypeStruct((S, Dm), F32), vec, vec, vec] + gate_shape,
                  compiler_params=_params())(dh, x, nw, sc, sh, dres, *([] if gate is None else gate))


def _head(x, fw, target, gate, *, tm=512):
    S, Dm = x.shape
    tm = _tile(S, tm, 8)

    def body(x_ref, fw_ref, t_ref, o_ref, g_ref, dx_ref, dfw_ref, loss_ref, do_ref, dg_ref):
        xv, w = x_ref[...], fw_ref[...]
        r = lax.rsqrt(jnp.mean(xv * xv, axis=-1, keepdims=True) + NORM_EPS)
        xn = xv * r
        err = xn * w - t_ref[...]
        dy = err * (1.0 / Dm)
        t = dy * xn
        dx = r * (dy * w - xn * jnp.mean(t * w, axis=-1, keepdims=True))
        dx_ref[...] = dx
        _acc(dfw_ref, jnp.sum(t, axis=0, keepdims=True))
        part = 0.5 * jnp.sum(jnp.mean(err * err, axis=-1, keepdims=True), axis=0, keepdims=True)
        _acc(loss_ref, jnp.broadcast_to(part, (8, 128)))
        _gate_bwd(dx, o_ref, g_ref, do_ref, dg_ref)
    vec = jax.ShapeDtypeStruct((1, Dm), F32)
    return _pcall(body, name="head", grid=(S // tm,),
                  in_specs=[_row_spec(tm, Dm), _vec_spec(Dm), _row_spec(tm, Dm), _row_spec(tm, Dm), _vec_spec(Dm)],
                  out_specs=[_row_spec(tm, Dm), _vec_spec(Dm), pl.BlockSpec((8, 128), lambda i: (0, 0)),
                             _row_spec(tm, Dm), _vec_spec(Dm)],
                  out_shape=[jax.ShapeDtypeStruct((S, Dm), F32), vec, jax.ShapeDtypeStruct((8, 128), F32),
                             jax.ShapeDtypeStruct((S, Dm), BF), vec],
                  compiler_params=_params())(x, fw, target, *gate)


def _tril_mask():
    r = lax.broadcasted_iota(jnp.int32, (SGU_CHUNK, SGU_CHUNK), 0)
    c = lax.broadcasted_iota(jnp.int32, (SGU_CHUNK, SGU_CHUNK), 1)
    return c <= r


def _sgu_fwd(proj, lnw, lnb, w, b_t, *, name, after=None, tm=512):
    S = proj.shape[0]
    tm = _tile(S, tm, SGU_CHUNK)
    extra = [] if after is None else [after]

    def body(zu_ref, zv_ref, lnw_ref, lnb_ref, w_ref, bt_ref, *rest):
        o_ref = rest[-1]
        u = _gelu(zu_ref[...].astype(F32))
        vn = _ln_gelu_fn(zv_ref[...].astype(F32), lnw_ref[...], lnb_ref[...]).astype(BF)
        mask = _tril_mask()
        for g in range(SGU_GROUPS):
            wm = jnp.where(mask, w_ref[g], 0.0).astype(BF)
            cols = slice(g * 128, (g + 1) * 128)
            for ci in range(tm // SGU_CHUNK):
                rows = slice(ci * SGU_CHUNK, (ci + 1) * SGU_CHUNK)
                f = jnp.dot(wm, vn[rows, cols], preferred_element_type=F32) + bt_ref[:, g:g + 1]
                o_ref[rows, cols] = (u[rows, cols] * f).astype(BF)
    return _pcall(body, name=name, grid=(S // tm,),
                  in_specs=[pl.BlockSpec((tm, SGU_WIDTH), lambda i: (i, 0)), pl.BlockSpec((tm, SGU_WIDTH), lambda i: (i, 1)),
                            _vec_spec(SGU_WIDTH), _vec_spec(SGU_WIDTH),
                            pl.BlockSpec((SGU_GROUPS, 128, 128), lambda i: (0, 0, 0)),
                            pl.BlockSpec((128, SGU_GROUPS), lambda i: (0, 0))] + [ANY] * len(extra),
                  out_specs=_row_spec(tm, SGU_WIDTH), out_shape=jax.ShapeDtypeStruct((S, SGU_WIDTH), BF),
                  compiler_params=_params())(proj, proj, lnw, lnb, w, b_t, *extra)


def _sgu_bwd(dy, proj, lnw, lnb, w, b_t, dproj, *, name, tm=512):
    S = proj.shape[0]
    tm = _tile(S, tm, SGU_CHUNK)

    def body(dy_ref, zu_ref, zv_ref, lnw_ref, lnb_ref, w_ref, bt_ref, _, dz_ref, dlnw_ref, dlnb_ref, dw_ref, dbt_ref,
             f_s, dvn_s):
        first = pl.program_id(0) == 0

        @pl.when(first)
        def _():
            dw_ref[...] = jnp.zeros_like(dw_ref)
            dbt_ref[...] = jnp.zeros_like(dbt_ref)
        u, vjp_u = jax.vjp(_gelu, zu_ref[...].astype(F32))
        vn, vjp_v = jax.vjp(_ln_gelu_fn, zv_ref[...].astype(F32), lnw_ref[...], lnb_ref[...])
        vn = vn.astype(BF)
        dy_v = dy_ref[...]
        df = (dy_v * u).astype(BF)
        mask = _tril_mask()
        for g in range(SGU_GROUPS):
            wm = jnp.where(mask, w_ref[g], 0.0).astype(BF)
            cols = slice(g * 128, (g + 1) * 128)
            dwg = jnp.zeros((128, 128), F32)
            dbg = jnp.zeros((128, 1), F32)
            for ci in range(tm // SGU_CHUNK):
                rows = slice(ci * SGU_CHUNK, (ci + 1) * SGU_CHUNK)
                vn_c = vn[rows, cols]
                df_c = df[rows, cols]
                f_s[rows, cols] = jnp.dot(wm, vn_c, preferred_element_type=F32) + bt_ref[:, g:g + 1]
                dvn_s[rows, cols] = lax.dot_general(wm, df_c, (((0,), (0,)), ((), ())), preferred_element_type=F32)
                dwg = dwg + lax.dot_general(df_c, vn_c, (((1,), (1,)), ((), ())), preferred_element_type=F32)
                dbg = dbg + jnp.sum((dy_v[rows, cols] * u[rows, cols]), axis=1, keepdims=True)
            dw_ref[g] += jnp.where(mask, dwg, 0.0)
            dbt_ref[:, g:g + 1] += dbg
        (dzu,) = vjp_u(dy_v * f_s[...])
        dzv, dlnw, dlnb = vjp_v(dvn_s[...])
        dz_ref[:, :SGU_WIDTH] = dzu.astype(BF)
        dz_ref[:, SGU_WIDTH:] = dzv.astype(BF)
        _acc(dlnw_ref, dlnw)
        _acc(dlnb_ref, dlnb)
    vec = jax.ShapeDtypeStruct((1, SGU_WIDTH), F32)
    return _pcall(body, name=name, grid=(S // tm,),
                  in_specs=[_row_spec(tm, SGU_WIDTH),
                            pl.BlockSpec((tm, SGU_WIDTH), lambda i: (i, 0)), pl.BlockSpec((tm, SGU_WIDTH), lambda i: (i, 1)),
                            _vec_spec(SGU_WIDTH), _vec_spec(SGU_WIDTH),
                            pl.BlockSpec((SGU_GROUPS, 128, 128), lambda i: (0, 0, 0)),
                            pl.BlockSpec((128, SGU_GROUPS), lambda i: (0, 0)), ANY],
                  out_specs=[pl.BlockSpec((tm, 2 * SGU_WIDTH), lambda i: (i, P_Z // (2 * SGU_WIDTH))),
                             _vec_spec(SGU_WIDTH), _vec_spec(SGU_WIDTH),
                             pl.BlockSpec((SGU_GROUPS, 128, 128), lambda i: (0, 0, 0)),
                             pl.BlockSpec((128, SGU_GROUPS), lambda i: (0, 0))],
                  out_shape=[jax.ShapeDtypeStruct(dproj.shape, BF), vec, vec,
                             jax.ShapeDtypeStruct((SGU_GROUPS, 128, 128), F32),
                             jax.ShapeDtypeStruct((128, SGU_GROUPS), F32)],
                  scratch_shapes=[pltpu.VMEM((tm, SGU_WIDTH), F32), pltpu.VMEM((tm, SGU_WIDTH), F32)],
                  input_output_aliases={7: 0},
                  compiler_params=_params())(dy, proj, proj, lnw, lnb, w, b_t, dproj)


def _merge_fwd(y_sgu, y_attn, pa, pb, proj, *, name, after=None, tm=1024, tn=512):
    S, Dm = y_sgu.shape
    tm = _tile(S, tm, 8)
    nj = Dm // tn
    extra = [] if after is None else [after]

    def body(ys_ref, ya_ref, pa_ref, pb_ref, ga_ref, gb_ref, *rest):
        a_ref, b_ref, m_ref = rest[-3:]
        a = jnp.dot(ys_ref[...], pa_ref[...], preferred_element_type=F32)
        b = jnp.dot(ya_ref[...], pb_ref[...], preferred_element_type=F32)
        a_ref[...] = a.astype(BF)
        b_ref[...] = b.astype(BF)
        m_ref[...] = (_sigmoid(ga_ref[...].astype(F32)) * a + _sigmoid(gb_ref[...].astype(F32)) * b).astype(BF)
    row = pl.BlockSpec((tm, Dm), lambda i, j: (i, 0))
    col = pl.BlockSpec((Dm, tn), lambda i, j: (0, j))
    out = pl.BlockSpec((tm, tn), lambda i, j: (i, j))
    sh = jax.ShapeDtypeStruct((S, Dm), BF)
    return _pcall(body, name=name, grid=(S // tm, nj),
                  in_specs=[row, row, col, col, pl.BlockSpec((tm, tn), lambda i, j: (i, P_G // tn + j)),
                            pl.BlockSpec((tm, tn), lambda i, j: (i, (P_G + Dm) // tn + j))] + [ANY] * len(extra),
                  out_specs=[out, out, out], out_shape=[sh, sh, sh],
                  compiler_params=_params())(y_sgu, y_attn, pa, pb, proj, proj, *extra)


def _merge_bwd(do, w_out, a, b, proj, *, name, after=None, tm=512):
    S, Dm = a.shape
    tm = _tile(S, tm, 8)
    ga_blk, gb_blk = P_G // Dm, P_G // Dm + 1
    extra = [] if after is None else [after]

    def body(do_ref, w_ref, a_ref, b_ref, ga_ref, gb_ref, *rest):
        da_ref, db_ref, dg_ref = rest[-3:]
        dmv = lax.dot_general(do_ref[...], w_ref[...], (((1,), (1,)), ((), ())), preferred_element_type=F32)
        sa = _sigmoid(ga_ref[...].astype(F32))
        sb = _sigmoid(gb_ref[...].astype(F32))
        da_ref[...] = (dmv * sa).astype(BF)
        db_ref[...] = (dmv * sb).astype(BF)
        dg_ref[:, :Dm] = (dmv * a_ref[...].astype(F32) * sa * (1.0 - sa)).astype(BF)
        dg_ref[:, Dm:] = (dmv * b_ref[...].astype(F32) * sb * (1.0 - sb)).astype(BF)
    return _pcall(body, name=name, grid=(S // tm,),
                  in_specs=[_row_spec(tm, Dm), pl.BlockSpec((Dm, Dm), lambda i: (0, 0)), _row_spec(tm, Dm), _row_spec(tm, Dm),
                            pl.BlockSpec((tm, Dm), lambda i: (i, ga_blk)), pl.BlockSpec((tm, Dm), lambda i: (i, gb_blk))]
                  + [ANY] * len(extra),
                  out_specs=[_row_spec(tm, Dm), _row_spec(tm, Dm), pl.BlockSpec((tm, 2 * Dm), lambda i: (i, P_G // (2 * Dm)))],
                  out_shape=[jax.ShapeDtypeStruct((S, Dm), BF), jax.ShapeDtypeStruct((S, Dm), BF),
                             jax.ShapeDtypeStruct((S, IN_COLS), BF)],
                  compiler_params=_params())(do, w_out, a, b, proj, proj, *extra)


def _shift_rows(a, halo, k, up):
    n = a.shape[0]
    r8 = lax.broadcasted_iota(jnp.int32, (8, a.shape[1]), 0)
    if not up:
        rolled = pltpu.roll(a, k, 0)
        patch = jnp.where(r8 < k, pltpu.roll(halo, k, 0), rolled[:8])
        return jnp.concatenate([patch, rolled[8:]], axis=0)
    rolled = pltpu.roll(a, n - k, 0)
    patch = jnp.where(r8 >= 8 - k, pltpu.roll(halo, 8 - k, 0), rolled[n - 8:])
    return jnp.concatenate([rolled[:n - 8], patch], axis=0)


def _conv_taps(a, halo):
    return _shift_rows(a, halo, 2, False), _shift_rows(a, halo, 1, False), a


HALO = 16


def _prev_halo_spec(tm, Fd):
    return pl.BlockSpec((HALO, Fd), lambda i: (jnp.maximum(i * (tm // HALO) - 1, 0), 0))


def _conv_fwd(a_ref, halo_ref, cw_ref, cb_ref):
    halo = jnp.where(pl.program_id(0) > 0, halo_ref[...].astype(F32)[HALO - 8:], 0.0)
    t0, t1, t2 = _conv_taps(a_ref[...].astype(F32), halo)
    return t0, t1, t2, cb_ref[...] + cw_ref[0:1, :] * t0 + cw_ref[1:2, :] * t1 + cw_ref[2:3, :] * t2


def _ffn_act_fwd(a, up, cw, cb, *, name, tm=256):
    S, Fd = a.shape
    tm = _tile(S, tm, HALO)

    def body(a_ref, up_ref, halo_ref, cw_ref, cb_ref, o_ref, ac_ref):
        _, _, _, ac = _conv_fwd(a_ref, halo_ref, cw_ref, cb_ref)
        ac_ref[...] = ac.astype(BF)
        o_ref[...] = (ac * _sigmoid(ac) * up_ref[...].astype(F32)).astype(BF)
    sh = jax.ShapeDtypeStruct((S, Fd), BF)
    return _pcall(body, name=name, grid=(S // tm,),
                  in_specs=[_row_spec(tm, Fd), _row_spec(tm, Fd), _prev_halo_spec(tm, Fd),
                            pl.BlockSpec((3, Fd), lambda i: (0, 0)), _vec_spec(Fd)],
                  out_specs=[_row_spec(tm, Fd), _row_spec(tm, Fd)], out_shape=[sh, sh],
                  compiler_params=_params())(a, up, a, cw, cb)


def _ffn_act_bwd_a(dhf, ac, up, *, name, tm=512):
    S, Fd = ac.shape
    tm = _tile(S, tm, HALO)

    def body(dhf_ref, ac_ref, up_ref, dac_ref, dup_ref, dcb_ref):
        acv = ac_ref[...].astype(F32)
        s = _sigmoid(acv)
        dhf_v = dhf_ref[...].astype(F32)
        dup_ref[...] = (dhf_v * acv * s).astype(BF)
        dac = dhf_v * up_ref[...].astype(F32) * (s * (1.0 + acv * (1.0 - s)))
        dac_ref[...] = dac.astype(BF)
        _acc(dcb_ref, jnp.sum(dac, axis=0, keepdims=True))
    sh = jax.ShapeDtypeStruct((S, Fd), BF)
    return _pcall(body, name=name, grid=(S // tm,), in_specs=[_row_spec(tm, Fd)] * 3,
                  out_specs=[_row_spec(tm, Fd), _row_spec(tm, Fd), _vec_spec(Fd)],
                  out_shape=[sh, sh, jax.ShapeDtypeStruct((1, Fd), F32)], compiler_params=_params())(dhf, ac, up)


def _ffn_act_bwd_b(dac, a, cw, *, name, tm=256):
    S, Fd = dac.shape
    tm = _tile(S, tm, HALO)
    last = S // tm - 1

    def body(d_ref, halo_ref, a_ref, cw_ref, o_ref, dcw_ref):
        halo = jnp.where(pl.program_id(0) < last, halo_ref[...].astype(F32)[:8], 0.0)
        d = d_ref[...].astype(F32)
        d1, d2 = _shift_rows(d, halo, 1, True), _shift_rows(d, halo, 2, True)
        o_ref[...] = (cw_ref[2:3, :] * d + cw_ref[1:2, :] * d1 + cw_ref[0:1, :] * d2).astype(BF)
        av = a_ref[...].astype(F32)
        _acc(dcw_ref, jnp.concatenate([jnp.sum(av * d2, axis=0, keepdims=True),
                                       jnp.sum(av * d1, axis=0, keepdims=True),
                                       jnp.sum(av * d, axis=0, keepdims=True)], axis=0))
    return _pcall(body, name=name, grid=(S // tm,),
                  in_specs=[_row_spec(tm, Fd),
                            pl.BlockSpec((HALO, Fd), lambda i: (jnp.minimum((i + 1) * (tm // HALO), S // HALO - 1), 0)),
                            _row_spec(tm, Fd), pl.BlockSpec((3, Fd), lambda i: (0, 0))],
                  out_specs=[_row_spec(tm, Fd), pl.BlockSpec((3, Fd), lambda i: (0, 0))],
                  out_shape=[jax.ShapeDtypeStruct((S, Fd), BF), jax.ShapeDtypeStruct((3, Fd), F32)],
                  compiler_params=_params())(dac, dac, a, cw)


def _rope_tables(pos_col, inv_row, m1_row, m2_row):
    S = pos_col.shape[0]
    tm = _tile(S, 512, 8)

    def body(p_ref, inv_ref, m1_ref, m2_ref, c_ref, s1_ref, s2_ref):
        ang = p_ref[...] * inv_ref[...]
        sn = jnp.sin(ang)
        c_ref[...] = jnp.cos(ang)
        s1_ref[...] = -sn * m1_ref[...]
        s2_ref[...] = sn * m2_ref[...]
    sh = jax.ShapeDtypeStruct((S, 128), F32)
    return _pcall(body, name="rope_tables", grid=(S // tm,),
                  in_specs=[pl.BlockSpec((tm, 1), lambda i: (i, 0)), _vec_spec(128), _vec_spec(128), _vec_spec(128)],
                  out_specs=[_row_spec(tm, 128)] * 3, out_shape=[sh, sh, sh], compiler_params=_params())(
                      pos_col, inv_row, m1_row, m2_row)


def _rope_apply(x, c, s1, s2):
    outs = []
    for j in range(x.shape[1] // 128):
        xj = x[:, j * 128:(j + 1) * 128]
        outs.append(xj * c + pltpu.roll(xj, 120, 1) * s1 + pltpu.roll(xj, 8, 1) * s2)
    return outs[0] if len(outs) == 1 else jnp.concatenate(outs, axis=1)


def _rope_apply_t(d, c, s1, s2):
    outs = []
    for j in range(d.shape[1] // 128):
        dj = d[:, j * 128:(j + 1) * 128]
        outs.append(dj * c + pltpu.roll(dj * s1, 8, 1) + pltpu.roll(dj * s2, 120, 1))
    return outs[0] if len(outs) == 1 else jnp.concatenate(outs, axis=1)


def _rope_fwd(proj, c, s1, s2, *, name, tm=512):
    S = proj.shape[0]
    tm = _tile(S, tm, 8)

    def body(q_ref, k_ref, v_ref, c_ref, s1_ref, s2_ref, qo_ref, ko_ref, vo_ref):
        cv, s1v, s2v = c_ref[...], s1_ref[...], s2_ref[...]
        qo_ref[...] = (_rope_apply(q_ref[...].astype(F32), cv, s1v, s2v) * (HEAD_DIM ** -0.5)).astype(BF)
        ko_ref[...] = _rope_apply(k_ref[...].astype(F32), cv, s1v, s2v).astype(BF)
        vo_ref[...] = v_ref[...].astype(BF)
    return _pcall(body, name=name, grid=(S // tm,),
                  in_specs=[pl.BlockSpec((tm, Q_END), lambda i: (i, P_Q // Q_END)),
                            pl.BlockSpec((tm, 128), lambda i: (i, P_K // 128)),
                            pl.BlockSpec((tm, 128), lambda i: (i, P_V // 128)),
                            _row_spec(tm, 128), _row_spec(tm, 128), _row_spec(tm, 128)],
                  out_specs=[_row_spec(tm, Q_END), _row_spec(tm, 128), _row_spec(tm, 128)],
                  out_shape=[jax.ShapeDtypeStruct((S, Q_END), BF), jax.ShapeDtypeStruct((S, 128), BF),
                             jax.ShapeDtypeStruct((S, 128), BF)],
                  compiler_params=_params())(proj, proj, proj, c, s1, s2)


def _rope_bwd(dq, dk, dv, c, s1, s2, dproj, *, name, tm=512):
    S = dq.shape[0]
    tm = _tile(S, tm, 8)
    tabs = [_row_spec(tm, 128)] * 3
    shape = jax.ShapeDtypeStruct(dproj.shape, BF)

    def body_q(dq_ref, c_ref, s1_ref, s2_ref, _, o_ref):
        o_ref[...] = _rope_apply_t(dq_ref[...].astype(F32), c_ref[...], s1_ref[...], s2_ref[...]).astype(BF)
    dproj = _pcall(body_q, name=name + "_q", grid=(S // tm,), in_specs=[_row_spec(tm, Q_END)] + tabs + [ANY],
                   out_specs=pl.BlockSpec((tm, Q_END), lambda i: (i, P_Q // Q_END)), out_shape=shape,
                   input_output_aliases={4: 0}, compiler_params=_params())(dq, c, s1, s2, dproj)

    def body_kv(dk_ref, dv_ref, c_ref, s1_ref, s2_ref, _, o_ref):
        o_ref[:, :128] = _rope_apply_t(dk_ref[...], c_ref[...], s1_ref[...], s2_ref[...]).astype(BF)
        o_ref[:, 128:] = dv_ref[...].astype(BF)
    return _pcall(body_kv, name=name + "_kv", grid=(S // tm,),
                  in_specs=[_row_spec(tm, 128), _row_spec(tm, 128)] + tabs + [ANY],
                  out_specs=pl.BlockSpec((tm, 256), lambda i: (i, P_K // 256)), out_shape=shape,
                  input_output_aliases={5: 0}, compiler_params=_params())(dk, dv, c, s1, s2, dproj)


def _lane_lo(shape):
    return lax.broadcasted_iota(jnp.int32, shape, 1) < HEAD_DIM


def _stack_heads(x, g):
    lo = _lane_lo((ATTN_BLOCK, 128))
    zero = jnp.zeros((ATTN_BLOCK, 128), x.dtype)
    parts = []
    for p in range(Q_PER_KV // 2):
        xp = x[:, (g * 4 + p) * 128:(g * 4 + p + 1) * 128]
        parts += [jnp.where(lo, xp, zero), jnp.where(lo, zero, xp)]
    return jnp.concatenate(parts, axis=0)


def _unstack_heads(o2):
    lo = _lane_lo((ATTN_BLOCK, 128))
    return [jnp.where(lo, o2[2 * p * ATTN_BLOCK:(2 * p + 1) * ATTN_BLOCK], o2[(2 * p + 1) * ATTN_BLOCK:(2 * p + 2) * ATTN_BLOCK])
            for p in range(Q_PER_KV // 2)]


def _dup_half(prev, cur, g):
    x = jnp.concatenate([prev, cur], axis=0).astype(F32)
    lo = _lane_lo(x.shape)
    r = pltpu.roll(x, HEAD_DIM, 1)
    return (jnp.where(lo, x, r) if g == 0 else jnp.where(lo, r, x)).astype(BF)


def _fold_halves(x):
    return x + pltpu.roll(x, HEAD_DIM, 1)


def _attn_bias():
    i = lax.broadcasted_iota(jnp.int32, (Q_PER_KV * ATTN_BLOCK, 2 * ATTN_BLOCK), 0) & (ATTN_BLOCK - 1)
    j = lax.broadcasted_iota(jnp.int32, (Q_PER_KV * ATTN_BLOCK, 2 * ATTN_BLOCK), 1)
    band = (j > i) & (j <= i + ATTN_BLOCK)
    return jnp.stack([jnp.where(band & (j >= ATTN_BLOCK), 0.0, -jnp.inf), jnp.where(band, 0.0, -jnp.inf)]).astype(F32)


def _both(x):
    return jnp.concatenate([x, x], axis=1)


def _row_sums(x_bf):
    return jnp.dot(x_bf, jnp.ones((x_bf.shape[1], 128), BF), preferred_element_type=F32)


def _attn_probs(qs, kb, sink, bias):
    s = lax.dot_general(qs, kb, (((1,), (1,)), ((), ())), preferred_element_type=F32) + bias
    m = jnp.maximum(jnp.broadcast_to(jnp.max(s, axis=-1, keepdims=True), sink.shape), sink)
    return jnp.exp(s - _both(m)), jnp.exp(sink - m)


def _attn_specs(S):
    nb = S // ATTN_BLOCK
    qs = pl.BlockSpec((ATTN_BLOCK, Q_END), lambda n: (n, 0))
    cur = pl.BlockSpec((ATTN_BLOCK, 128), lambda n: (n, 0))
    prev = pl.BlockSpec((ATTN_BLOCK, 128), lambda n: (jnp.maximum(n - 1, 0), 0))
    sink = pl.BlockSpec((N_KV_HEADS, Q_PER_KV * ATTN_BLOCK, 128), lambda n: (0, 0, 0))
    bias = pl.BlockSpec((None, Q_PER_KV * ATTN_BLOCK, 2 * ATTN_BLOCK), lambda n: (jnp.minimum(n, 1), 0, 0))
    return nb, qs, cur, prev, sink, bias


def _attn_fwd(q, k, v, sink_rows, bias, *, name):
    S = q.shape[0]
    nb, qs, cur, prev, sink, bs = _attn_specs(S)

    def body(q_ref, kp_ref, kc_ref, vp_ref, vc_ref, sk_ref, b_ref, o_ref):
        for g in range(N_KV_HEADS):
            kb = _dup_half(kp_ref[...], kc_ref[...], g)
            vb = _dup_half(vp_ref[...], vc_ref[...], g)
            p, es = _attn_probs(_stack_heads(q_ref[...], g), kb, sk_ref[g], b_ref[...])
            ones = jnp.ones((2 * ATTN_BLOCK, 128), BF)
            o3 = jnp.dot(p.astype(BF), jnp.concatenate([vb, ones], axis=1), preferred_element_type=F32)
            o2 = o3[:, :128] / (o3[:, 128:] + es)
            for t, tile in enumerate(_unstack_heads(o2)):
                o_ref[:, (g * 4 + t) * 128:(g * 4 + t + 1) * 128] = tile.astype(BF)
    return _pcall(body, name=name, grid=(nb,), in_specs=[qs, prev, cur, prev, cur, sink, bs], out_specs=qs,
                  out_shape=jax.ShapeDtypeStruct(q.shape, BF), compiler_params=_params())(q, k, k, v, v, sink_rows, bias)


def _attn_bwd(do, q, k, v, sink_rows, bias, *, name):
    S = q.shape[0]
    nb, qs, cur, prev, sink, bs = _attn_specs(S)
    full = pl.BlockSpec((S, 128), lambda n: (0, 0))
    dsk_spec = pl.BlockSpec((N_KV_HEADS, Q_PER_KV, 128), lambda n: (0, 0, 0))

    def body(do_ref, q_ref, kp_ref, kc_ref, vp_ref, vc_ref, sk_ref, b_ref, dq_ref, dk_ref, dv_ref, dsk_ref):
        n = pl.program_id(0)

        @pl.when(n == 0)
        def _():
            dk_ref[...] = jnp.zeros_like(dk_ref)
            dv_ref[...] = jnp.zeros_like(dv_ref)
            dsk_ref[...] = jnp.zeros_like(dsk_ref)
        sub = lax.broadcasted_iota(jnp.int32, (Q_PER_KV, 128), 0)
        dkf, dvf = [], []
        for g in range(N_KV_HEADS):
            qst = _stack_heads(q_ref[...], g)
            dos = _stack_heads(do_ref[...], g)
            kb = _dup_half(kp_ref[...], kc_ref[...], g)
            vb = _dup_half(vp_ref[...], vc_ref[...], g)
            pu, es = _attn_probs(qst, kb, sk_ref[g], b_ref[...])
            inv = 1.0 / (_row_sums(pu.astype(BF)) + es)
            p = pu * _both(inv)
            dp = lax.dot_general(dos, vb, (((1,), (1,)), ((), ())), preferred_element_type=F32)
            dd = _row_sums((p * dp).astype(BF))
            ds = (p * (dp - _both(dd))).astype(BF)
            dq2 = jnp.dot(ds, kb, preferred_element_type=F32) * (HEAD_DIM ** -0.5)
            for t, tile in enumerate(_unstack_heads(dq2)):
                dq_ref[:, (g * 4 + t) * 128:(g * 4 + t + 1) * 128] = tile.astype(BF)
            dkf.append(_fold_halves(lax.dot_general(ds, qst, (((0,), (0,)), ((), ())), preferred_element_type=F32)))
            dvf.append(_fold_halves(lax.dot_general(p.astype(BF), dos, (((0,), (0,)), ((), ())),
                                                    preferred_element_type=F32)))
            dsr = -(es * inv * dd)
            upd = jnp.zeros((Q_PER_KV, 128), F32)
            for h in range(Q_PER_KV):
                upd = jnp.where(sub == h, jnp.sum(dsr[h * ATTN_BLOCK:(h + 1) * ATTN_BLOCK], axis=0, keepdims=True), upd)
            dsk_ref[g] += upd
        lo = _lane_lo((2 * ATTN_BLOCK, 128))
        dkb = jnp.where(lo, dkf[0], dkf[1])
        dvb = jnp.where(lo, dvf[0], dvf[1])
        r0 = pl.multiple_of(n * ATTN_BLOCK, ATTN_BLOCK)
        dk_ref[pl.ds(r0, ATTN_BLOCK), :] += dkb[ATTN_BLOCK:]
        dv_ref[pl.ds(r0, ATTN_BLOCK), :] += dvb[ATTN_BLOCK:]

        @pl.when(n > 0)
        def _():
            rp = pl.multiple_of((n - 1) * ATTN_BLOCK, ATTN_BLOCK)
            dk_ref[pl.ds(rp, ATTN_BLOCK), :] += dkb[:ATTN_BLOCK]
            dv_ref[pl.ds(rp, ATTN_BLOCK), :] += dvb[:ATTN_BLOCK]
    return _pcall(body, name=name, grid=(nb,), in_specs=[qs, qs, prev, cur, prev, cur, sink, bs],
                  out_specs=[qs, full, full, dsk_spec],
                  out_shape=[jax.ShapeDtypeStruct(q.shape, BF), jax.ShapeDtypeStruct((S, 128), F32),
                             jax.ShapeDtypeStruct((S, 128), F32), jax.ShapeDtypeStruct((N_KV_HEADS, Q_PER_KV, 128), F32)],
                  compiler_params=_params())(do, q, k, k, v, v, sink_rows, bias)


def _ada_fwd(c_all, ada_w):
    ncol = ada_w.shape[2]

    def body(c_ref, w_ref, o_ref):
        cv = c_ref[...]
        ca = (cv * _sigmoid(cv)).astype(BF)
        for l in range(DEPTH):
            o_ref[:, l * ncol:(l + 1) * ncol] = jnp.dot(ca, w_ref[l].astype(BF), preferred_element_type=F32)
    return _pcall(body, name="ada_fwd", out_shape=jax.ShapeDtypeStruct((N_DEV, DEPTH * ncol), F32),
                  compiler_params=_params())(c_all, ada_w)


def _ada_bwd(c_all, dm):
    ncol = dm.shape[2]

    def body(c_ref, dm_ref, o_ref):
        cv = c_ref[...]
        ca = (cv * _sigmoid(cv)).astype(BF)
        for l in range(DEPTH):
            o_ref[l] = lax.dot_general(ca, dm_ref[l].astype(BF), (((0,), (0,)), ((), ())), preferred_element_type=F32)
    return _pcall(body, name="ada_bwd", out_shape=jax.ShapeDtypeStruct((DEPTH, D_MODEL, ncol), F32),
                  compiler_params=_params())(c_all, dm)


def _adamw(w, g, m, v, *, name):
    R, C = w.shape
    tr = R
    for t in range(8, 513, 8):
        if R % t == 0:
            tr = t
    c1 = 1.0 - ADAM_B1 ** ADAM_STEP
    c2 = 1.0 - ADAM_B2 ** ADAM_STEP

    def body(w_ref, g_ref, m_ref, v_ref, d_ref, mo_ref, vo_ref):
        gv = g_ref[...]
        mn = ADAM_B1 * m_ref[...] + (1.0 - ADAM_B1) * gv
        vn = ADAM_B2 * v_ref[...] + (1.0 - ADAM_B2) * (gv * gv)
        mo_ref[...] = mn
        vo_ref[...] = vn
        d_ref[...] = -ADAM_LR * ((mn * (1.0 / c1)) / (jnp.sqrt(vn * (1.0 / c2)) + ADAM_EPS) + ADAM_WD * w_ref[...])
    spec = pl.BlockSpec((tr, C), lambda i: (i, 0))
    sh = jax.ShapeDtypeStruct((R, C), F32)
    return _pcall(body, name=name, grid=(R // tr,), in_specs=[spec] * 4, out_specs=[spec] * 3, out_shape=[sh, sh, sh],
                  compiler_params=_params())(w, g, m, v)


def _adamw_layers(w, g_layers, m, v, *, name):
    L, R, C = w.shape
    assert L == 2 and len(g_layers) == 2
    tr = R
    for t in range(8, 513, 8):
        if R % t == 0:
            tr = t
    c1 = 1.0 - ADAM_B1 ** ADAM_STEP
    c2 = 1.0 - ADAM_B2 ** ADAM_STEP

    def body(w_ref, g0_ref, g1_ref, m_ref, v_ref, go_ref, d_ref, mo_ref, vo_ref):
        gv = jnp.where(pl.program_id(0) == 0, g0_ref[...], g1_ref[...])
        go_ref[...] = gv
        mn = ADAM_B1 * m_ref[...] + (1.0 - ADAM_B1) * gv
        vn = ADAM_B2 * v_ref[...] + (1.0 - ADAM_B2) * (gv * gv)
        mo_ref[...] = mn
        vo_ref[...] = vn
        d_ref[...] = -ADAM_LR * ((mn * (1.0 / c1)) / (jnp.sqrt(vn * (1.0 / c2)) + ADAM_EPS) + ADAM_WD * w_ref[...])
    spec = pl.BlockSpec((None, tr, C), lambda l, i: (l, i, 0))
    sh = jax.ShapeDtypeStruct((L, R, C), F32)
    g_specs = [pl.BlockSpec((tr, C), lambda l, i, k=k: (jnp.where(l == k, i, 0), 0)) for k in range(L)]
    return _pcall(body, name=name, grid=(L, R // tr), in_specs=[spec] + g_specs + [spec, spec], out_specs=[spec] * 4,
                  out_shape=[sh] * 4, compiler_params=_params())(w, *g_layers, m, v)


def _sum8(parts, *, name):
    _, R, C = parts.shape
    tr = _tile(R, 512, 16)

    def body(p_ref, o_ref):
        acc = p_ref[0].astype(F32)
        for k in range(1, N_DEV):
            acc = acc + p_ref[k].astype(F32)
        o_ref[...] = acc
    return _pcall(body, name=name, grid=(R // tr,), in_specs=[pl.BlockSpec((N_DEV, tr, C), lambda i: (0, i, 0))],
                  out_specs=pl.BlockSpec((tr, C), lambda i: (i, 0)), out_shape=jax.ShapeDtypeStruct((R, C), F32),
                  compiler_params=_params())(parts)


MESH_ID = pl.DeviceIdType.MESH
ANY = pl.BlockSpec(memory_space=pl.ANY)


def _all_gather(x, *, name, after=None):
    R, C = x.shape
    extra = [] if after is None else [after]

    def body(x_ref, *rest):
        out_ref, send_sems, recv_sems, local_sem = rest[-4:]
        mx, my, mc = lax.axis_index("x"), lax.axis_index("y"), lax.axis_index("c")
        me, sibling = (mx, my, mc), (mx, my, 1 - mc)
        chips = [(1 - mx, my), (mx, 1 - my), (1 - mx, 1 - my)]

        def blk(px, py, pc):
            return out_ref.at[4 * px + 2 * py + pc]

        def copy(k, block, to, src=None):
            return pltpu.make_async_remote_copy(
                src_ref=blk(*block) if src is None else src, dst_ref=blk(*block),
                send_sem=send_sems.at[k], recv_sem=recv_sems.at[k], device_id=to, device_id_type=MESH_ID)

        mine = pltpu.make_async_copy(x_ref, blk(*me), local_sem)
        mine.start()
        first = [copy(0, me, sibling, src=x_ref)]
        first += [copy(1 + j, me, (*chip, mc), src=x_ref) for j, chip in enumerate(chips)]
        for cp in first:
            cp.start()
        passed = [copy(4 + j, (*chip, mc), sibling) for j, chip in enumerate(chips)]
        for j, chip in enumerate(chips):
            copy(1 + j, (*chip, mc), me).wait_recv()
            passed[j].start()
        copy(0, sibling, me).wait_recv()
        for j, chip in enumerate(chips):
            copy(4 + j, (*chip, 1 - mc), me).wait_recv()
        for cp in first + passed:
            cp.wait_send()
        mine.wait()
    return _pcall(body, name=name, in_specs=[ANY] * (1 + len(extra)), out_specs=ANY,
                  out_shape=jax.ShapeDtypeStruct((N_DEV, R, C), x.dtype),
                  scratch_shapes=[pltpu.SemaphoreType.DMA((7,)), pltpu.SemaphoreType.DMA((7,)), pltpu.SemaphoreType.DMA],
                  compiler_params=pltpu.CompilerParams(has_side_effects=True))(x, *extra)


HBM_SPEC = pl.BlockSpec(memory_space=pltpu.HBM)
SEM_SPEC = pl.BlockSpec(memory_space=pltpu.SEMAPHORE)
DATAFLOW = pltpu.SideEffectType.DATAFLOW_SIDE_EFFECTING


def _coords():
    return lax.axis_index("x"), lax.axis_index("y"), lax.axis_index("c")


def _other_chips(mx, my):
    return [(1 - mx, my), (mx, 1 - my), (1 - mx, 1 - my)]


def _plan_gather_ici(refs, send, recv):
    src, land = refs
    mx, my, mc = _coords()
    return [pltpu.make_async_remote_copy(src_ref=src, dst_ref=land.at[mc, 2 * mx + my], send_sem=send[j], recv_sem=recv[j],
                                         device_id=(px, py, mc), device_id_type=MESH_ID)
            for j, (px, py) in enumerate(_other_chips(mx, my))]


def _plan_gather_d2d(refs, send, recv):
    (land,) = refs
    mx, my, mc = _coords()
    return [pltpu.make_async_remote_copy(src_ref=land.at[mc], dst_ref=land.at[mc], send_sem=send[0], recv_sem=recv[0],
                                         device_id=(mx, my, 1 - mc), device_id_type=MESH_ID)]


def _plan_reduce_d2d(refs, send, recv):
    g, land = refs
    mx, my, mc = _coords()
    return [pltpu.make_async_remote_copy(src_ref=g.at[1 - mc], dst_ref=land, send_sem=send[0], recv_sem=recv[0],
                                         device_id=(mx, my, 1 - mc), device_id_type=MESH_ID)]


def _plan_reduce_ici(refs, send, recv):
    h, land = refs
    mx, my, mc = _coords()
    return [pltpu.make_async_remote_copy(src_ref=h.at[2 * px + py], dst_ref=land.at[j], send_sem=send[j], recv_sem=recv[j],
                                         device_id=(px, py, mc), device_id_type=MESH_ID)
            for j, (px, py) in enumerate(_other_chips(mx, my))]


def _rdma_start(bufs, n, plan, *, name, after=None):
    nb = len(bufs)
    extra = [] if after is None else [after]
    ne = len(extra)

    def body(*refs):
        ins, send, recv = refs[:nb], refs[nb + ne:nb + ne + n], refs[nb + ne + n:nb + ne + 2 * n]
        token = refs[-1]
        for cp in plan(ins, send, recv):
            cp.start()
        token[...] = jnp.zeros_like(token)
    out = _pcall(body, name=name,
                 out_shape=tuple([pltpu.SemaphoreType.DMA(())] * (2 * n) + [pltpu.HBM(b.shape, b.dtype) for b in bufs]
                                 + [jax.ShapeDtypeStruct((8, 128), F32)]),
                 in_specs=tuple([HBM_SPEC] * nb + [ANY] * ne),
                 out_specs=tuple([SEM_SPEC] * (2 * n) + [HBM_SPEC] * nb + [pl.BlockSpec(memory_space=pltpu.VMEM)]),
                 input_output_aliases={i: 2 * n + i for i in range(nb)},
                 compiler_params=pltpu.CompilerParams(has_side_effects=DATAFLOW))(
                     *[pltpu.with_memory_space_constraint(b, pltpu.HBM) for b in bufs], *extra)
    return list(out[:2 * n]), list(out[2 * n:2 * n + nb]), out[-1]


def _rdma_wait(sems, bufs, n, plan, after, *, name):
    nb = len(bufs)

    def body(*refs):
        ins, send, recv = refs[:nb], refs[nb:nb + n], refs[nb + n:nb + 2 * n]
        for cp in plan(ins, send, recv):
            cp.wait_send()
            cp.wait_recv()
    out = _pcall(body, name=name, out_shape=tuple(pltpu.HBM(b.shape, b.dtype) for b in bufs),
                 in_specs=tuple([HBM_SPEC] * nb + [SEM_SPEC] * (2 * n) + [ANY]), out_specs=tuple([HBM_SPEC] * nb),
                 input_output_aliases={i: i for i in range(nb)},
                 compiler_params=pltpu.CompilerParams(has_side_effects=DATAFLOW))(*bufs, *sems, after)
    return list(out)


def _sum_pair(g, land, cidx, *, name):
    _, nchip, R, C = g.shape
    tr = _tile(R, 1056, 16)

    def body(c_ref, g_ref, l_ref, o_ref):
        o_ref[...] = g_ref[...] + l_ref[...]
    grid_spec = pltpu.PrefetchScalarGridSpec(
        num_scalar_prefetch=1, grid=(nchip, R // tr),
        in_specs=[pl.BlockSpec((None, None, tr, C), lambda p, i, c_ref: (c_ref[0], p, i, 0)),
                  pl.BlockSpec((None, tr, C), lambda p, i, c_ref: (p, i, 0))],
        out_specs=pl.BlockSpec((None, tr, C), lambda p, i, c_ref: (p, i, 0)))
    return _pcall(body, name=name, grid_spec=grid_spec, out_shape=jax.ShapeDtypeStruct((nchip, R, C), BF),
                  compiler_params=_params())(cidx, g, land)


def _sum_chips(h, land, chipidx, *, name):
    _, R, C = h.shape
    tr = _tile(R, 1056, 16)

    def body(c_ref, h_ref, l_ref, o_ref):
        acc = h_ref[...].astype(F32)
        for j in range(3):
            acc = acc + l_ref[j].astype(F32)
        o_ref[...] = acc
    grid_spec = pltpu.PrefetchScalarGridSpec(
        num_scalar_prefetch=1, grid=(R // tr,),
        in_specs=[pl.BlockSpec((None, tr, C), lambda i, c_ref: (c_ref[0], i, 0)),
                  pl.BlockSpec((3, tr, C), lambda i, c_ref: (0, i, 0))],
        out_specs=pl.BlockSpec((tr, C), lambda i, c_ref: (i, 0)))
    return _pcall(body, name=name, grid_spec=grid_spec, out_shape=jax.ShapeDtypeStruct((R, C), F32),
                  compiler_params=_params())(chipidx, h, land)


PART_IN = ("w_in",)
PART_MIX = ("proj_a", "proj_b", "w_out")
PART_FFN = ("ffn_w_gate", "ffn_w_up", "ffn_w_down")


def _part_rows(names):
    return sum(BIG_ROWS[n] for n in names)


def _part_offsets(names):
    off, r = {}, 0
    for n in names:
        off[n] = r
        r += BIG_ROWS[n]
    return off


def _pack_shards(shards, l, names):
    return jnp.concatenate([(shards[n][l].T if n in COL_SHARDED else shards[n][l]).astype(BF) for n in names], axis=0)


def _unpack_weights(full8, names):
    off = _part_offsets(names)

    def whole(n):
        return full8[:, off[n]:off[n] + BIG_ROWS[n], :].reshape(N_DEV * BIG_ROWS[n], 1024)
    out = {}
    if "w_in" in names:
        wt_in = whole("w_in")
        out["wt_in"] = jnp.concatenate([wt_in[V_END:], wt_in[:V_END]], axis=0)
    for n in ("proj_a", "proj_b", "w_out"):
        if n in names:
            out[n] = whole(n)
    if "ffn_w_gate" in names:
        out["wt_gate"], out["wt_up"], out["w_down"] = whole("ffn_w_gate"), whole("ffn_w_up"), whole("ffn_w_down")
    return out


def _from_land(land):
    return land.transpose(1, 0, 2, 3).reshape(N_DEV, land.shape[2], 1024)


def _pack_grads(wg, names):
    full = {"proj_a": wg.get("proj_a"), "proj_b": wg.get("proj_b"), "w_out": wg.get("w_out"), "ffn_w_down": wg.get("w_down"),
            "ffn_w_gate": wg.get("wt_gate"), "ffn_w_up": wg.get("wt_up")}
    if "w_in" in names:
        full["w_in"] = jnp.concatenate([wg["wt_in"][P_Q:], wg["wt_in"][:P_Q]], axis=0)
    blocks = jnp.concatenate([full[n].reshape(N_DEV, BIG_ROWS[n], 1024) for n in names], axis=1)
    return blocks.reshape(4, 2, _part_rows(names), 1024).transpose(1, 0, 2, 3)


def _unpack_shard_grads(gs, names):
    off = _part_offsets(names)
    return {n: gs[off[n]:off[n] + BIG_ROWS[n]] for n in names}


def _rope_setup(positions):
    S = positions.shape[0]
    inv = ROPE_THETA ** (-jnp.arange(0, ROT_DIM, 2, dtype=F32) / ROT_DIM)
    lane = np.arange(128) % HEAD_DIM
    half = ROT_DIM // 2
    inv_row = jnp.where(lane < ROT_DIM, jnp.tile(inv, 128 // half), 0.0)[None, :].astype(F32)
    m1_row = jnp.asarray((lane < half).astype(np.float32))[None, :]
    m2_row = jnp.asarray(((lane >= half) & (lane < ROT_DIM)).astype(np.float32))[None, :]
    return (*_rope_tables(positions.astype(F32).reshape(S, 1), inv_row, m1_row, m2_row), _attn_bias())


def _hook(hooks, point, after):
    f = None if hooks is None else hooks.get(point)
    return None if f is None else f(after)


def _layer_fwd(l, x, mod_l, W, small, rope, hooks=None):
    rc, rs1, rs2, bias = rope
    sh1, sc1, g1, sh2, sc2, g2 = [mod_l[i * D_MODEL:(i + 1) * D_MODEL][None, :] for i in range(6)]
    nw1, nw2 = small["norm1_w"][l][None, :], small["norm2_w"][l][None, :]
    tok = _hook(hooks, "mm_in", x)
    h, (proj,) = _norm_mm(x, nw1, sc1, sh1, [W["wt_in"]], name=f"mm_in{l}", after=tok, tm=2048, tn_cap=768)
    q_r, k_r, v_b = _rope_fwd(proj, rc, rs1, rs2, name=f"rope_fwd{l}")
    sink_rows = jnp.repeat(small["attn_sinks"][l].reshape(N_KV_HEADS, Q_PER_KV), ATTN_BLOCK, axis=1)
    sink_rows = jnp.broadcast_to(sink_rows[..., None], sink_rows.shape + (128,))
    y_attn = _attn_fwd(q_r, k_r, v_b, sink_rows, bias, name=f"attn_fwd{l}")
    lnw, lnb = small["sgu_ln_w"][l][None, :], small["sgu_ln_b"][l][None, :]
    sgu_bt = small["sgu_b"][l].T
    y_sgu = _sgu_fwd(proj, lnw, lnb, small["sgu_w"][l], sgu_bt, name=f"sgu_fwd{l}", after=_hook(hooks, "sgu", y_attn))
    tok = _hook(hooks, "mm_pa", y_sgu)
    a_br, b_br, merged = _merge_fwd(y_sgu, y_attn, W["proj_a"], W["proj_b"], proj, name=f"merge_fwd{l}", after=tok)
    x1, o1 = _mm(merged, W["w_out"], nt=False, out_dtype=F32, name=f"mm_out{l}", res=x, gvec=g1, tm=512)
    tok = _hook(hooks, "mm_gu", x1)
    h2, (a_g, a_u) = _norm_mm(x1, nw2, sc2, sh2, [W["wt_gate"], W["wt_up"]], name=f"mm_gu{l}", after=tok, tn_cap=1408)
    cw, cb = small["ffn_conv_w"][l], small["ffn_conv_b"][l][None, :]
    hf, a_c = _ffn_act_fwd(a_g, a_u, cw, cb, name=f"ffn_act_fwd{l}")
    x2, o2 = _mm(hf, W["w_down"], nt=False, out_dtype=F32, name=f"mm_down{l}", res=x1, gvec=g2)
    saved = dict(x=x, h=h, proj=proj, q_r=q_r, k_r=k_r, v_b=v_b, sink_rows=sink_rows, y_attn=y_attn, y_sgu=y_sgu,
                 a_br=a_br, b_br=b_br, merged=merged, x1=x1, o1=o1, h2=h2, a_g=a_g, a_u=a_u, a_c=a_c, hf=hf, o2=o2)
    return x2, saved


def _layer_bwd(l, dx, do2, dg2, mod_l, W, small, rope, sv, below=None, hooks=None, wg=None):
    rc, rs1, rs2, bias = rope
    sh1, sc1, g1, sh2, sc2, g2 = [mod_l[i * D_MODEL:(i + 1) * D_MODEL][None, :] for i in range(6)]
    nw1, nw2 = small["norm1_w"][l][None, :], small["norm2_w"][l][None, :]
    cw = small["ffn_conv_w"][l]
    lnw, lnb = small["sgu_ln_w"][l][None, :], small["sgu_ln_b"][l][None, :]
    sgu_bt = small["sgu_b"][l].T
    wg = {} if wg is None else wg
    dhf = _mm(do2, W["w_down"], nt=True, out_dtype=BF, name=f"mm_down_dx{l}", after=_hook(hooks, "mm_down_dx", do2),
              tn_cap=1408)
    wg["w_down"] = _mm_tn(sv["hf"], do2, name=f"mm_down_dw{l}")
    dac, dup, dcb = _ffn_act_bwd_a(dhf, sv["a_c"], sv["a_u"], name=f"ffn_act_bwd_a{l}")
    da, dcw = _ffn_act_bwd_b(dac, sv["a_g"], cw, name=f"ffn_act_bwd_b{l}")
    dh2 = _mm([da, dup], [W["wt_gate"], W["wt_up"]], nt=False, out_dtype=F32, name=f"mm_gu_dx{l}",
              after=_hook(hooks, "mm_gu_dx", da))
    wg["wt_gate"] = _mm_tn(da, sv["h2"], name=f"mm_gate_dw{l}")
    wg["wt_up"] = _mm_tn(dup, sv["h2"], name=f"mm_up_dw{l}")
    dx1, dnw2, dsc2, dsh2, do1, dg1 = _normmod_bwd(dh2, sv["x1"], nw2, sc2, sh2, dx, (sv["o1"], g1), name=f"normmod2_bwd{l}")
    d_a, d_b, dproj = _merge_bwd(do1, W["w_out"], sv["a_br"], sv["b_br"], sv["proj"], name=f"merge_bwd{l}",
                                 after=_hook(hooks, "merge_bwd", do1))
    wg["w_out"] = _mm_tn(sv["merged"], do1, name=f"mm_out_dw{l}")
    dysgu = _mm(d_a, W["proj_a"], nt=True, out_dtype=F32, name=f"mm_pa_dx{l}", after=_hook(hooks, "mm_pa_dx", d_a))
    dyattn = _mm(d_b, W["proj_b"], nt=True, out_dtype=BF, name=f"mm_pb_dx{l}")
    wg["proj_a"] = _mm_tn(sv["y_sgu"], d_a, name=f"mm_pa_dw{l}")
    wg["proj_b"] = _mm_tn(sv["y_attn"], d_b, name=f"mm_pb_dw{l}")
    dproj, dlnw, dlnb, dsguw, dsgubt = _sgu_bwd(dysgu, sv["proj"], lnw, lnb, small["sgu_w"][l], sgu_bt, dproj,
                                                name=f"sgu_bwd{l}")
    dq_r, dk_r, dv_b, dsk = _attn_bwd(dyattn, sv["q_r"], sv["k_r"], sv["v_b"], sv["sink_rows"], bias, name=f"attn_bwd{l}")
    dproj = _rope_bwd(dq_r, dk_r, dv_b, rc, rs1, rs2, dproj, name=f"rope_bwd{l}")
    wg["wt_in"] = _mm_tn(dproj, sv["h"], name=f"mm_in_dw{l}")
    dh = _mm(dproj, W["wt_in"], nt=False, out_dtype=F32, name=f"mm_in_dx{l}", after=_hook(hooks, "mm_in_dx", wg["wt_in"]))
    dx0, dnw1, dsc1, dsh1, *gate_below = _normmod_bwd(dh, sv["x"], nw1, sc1, sh1, dx1, below, name=f"normmod1_bwd{l}")
    dmod = jnp.concatenate([dsh1, dsc1, dg1, dsh2, dsc2, dg2], axis=1)[0]
    sg = {"norm1_w": dnw1[0], "norm2_w": dnw2[0], "attn_sinks": dsk[:, :, 0].reshape(N_Q_HEADS),
          "sgu_ln_w": dlnw[0], "sgu_ln_b": dlnb[0], "sgu_w": dsguw, "sgu_b": dsgubt.T,
          "ffn_conv_w": dcw, "ffn_conv_b": dcb[0]}
    return (dx0, *gate_below), wg, sg, dmod


SMALL = ("ada_b", "norm1_w", "attn_sinks", "sgu_ln_w", "sgu_ln_b", "sgu_w", "sgu_b", "norm2_w", "ffn_conv_b", "final_norm_w")
WEIGHT_ORDER = ("ada_w", "ada_b", "norm1_w", "w_in", "attn_sinks", "sgu_ln_w", "sgu_ln_b", "sgu_w", "sgu_b", "proj_a", "proj_b",
                "w_out", "norm2_w", "ffn_w_gate", "ffn_w_up", "ffn_conv_w", "ffn_conv_b", "ffn_w_down", "final_norm_w")


def _flat_pack(arrs, rows):
    flat = jnp.concatenate([a.reshape(-1) for a in arrs])
    return jnp.pad(flat, (0, rows * 1024 - flat.shape[0])).reshape(rows, 1024)


def _flat_unpack(buf, shapes):
    flat = buf.reshape(-1)
    out, o = [], 0
    for s in shapes:
        n = int(np.prod(s))
        out.append(flat[o:o + n].reshape(s))
        o += n
    return out


def _adam2d(w, g, m, v, *, name):
    shp = w.shape
    r2 = (int(np.prod(shp[:-1])), shp[-1]) if len(shp) > 1 else (1, shp[0])
    d, mn, vn = _adamw(w.reshape(r2), g.reshape(r2), m.reshape(r2), v.reshape(r2), name=name)
    return d.reshape(shp), mn.reshape(shp), vn.reshape(shp)


def kernel(x, c, positions, ada_w, ada_b, norm1_w, w_in, attn_sinks, sgu_ln_w, sgu_ln_b, sgu_w, sgu_b, proj_a, proj_b, w_out, norm2_w, ffn_w_gate, ffn_w_up, ffn_conv_w, ffn_conv_b, ffn_w_down, final_norm_w, loss_target, m_ada_w, m_ada_b, m_norm1_w, m_w_in, m_attn_sinks, m_sgu_ln_w, m_sgu_ln_b, m_sgu_w, m_sgu_b, m_proj_a, m_proj_b, m_w_out, m_norm2_w, m_ffn_w_gate, m_ffn_w_up, m_ffn_conv_w, m_ffn_conv_b, m_ffn_w_down, m_final_norm_w, v_ada_w, v_ada_b, v_norm1_w, v_w_in, v_attn_sinks, v_sgu_ln_w, v_sgu_ln_b, v_sgu_w, v_sgu_b, v_proj_a, v_proj_b, v_w_out, v_norm2_w, v_ffn_w_gate, v_ffn_w_up, v_ffn_conv_w, v_ffn_conv_b, v_ffn_w_down, v_final_norm_w):
    wts = dict(ada_w=ada_w, ada_b=ada_b, norm1_w=norm1_w, w_in=w_in, attn_sinks=attn_sinks, sgu_ln_w=sgu_ln_w,
               sgu_ln_b=sgu_ln_b, sgu_w=sgu_w, sgu_b=sgu_b, proj_a=proj_a, proj_b=proj_b, w_out=w_out, norm2_w=norm2_w,
               ffn_w_gate=ffn_w_gate, ffn_w_up=ffn_w_up, ffn_conv_w=ffn_conv_w, ffn_conv_b=ffn_conv_b,
               ffn_w_down=ffn_w_down, final_norm_w=final_norm_w)
    mom = dict(ada_w=m_ada_w, ada_b=m_ada_b, norm1_w=m_norm1_w, w_in=m_w_in, attn_sinks=m_attn_sinks, sgu_ln_w=m_sgu_ln_w,
               sgu_ln_b=m_sgu_ln_b, sgu_w=m_sgu_w, sgu_b=m_sgu_b, proj_a=m_proj_a, proj_b=m_proj_b, w_out=m_w_out,
               norm2_w=m_norm2_w, ffn_w_gate=m_ffn_w_gate, ffn_w_up=m_ffn_w_up, ffn_conv_w=m_ffn_conv_w,
               ffn_conv_b=m_ffn_conv_b, ffn_w_down=m_ffn_w_down, final_norm_w=m_final_norm_w)
    var = dict(ada_w=v_ada_w, ada_b=v_ada_b, norm1_w=v_norm1_w, w_in=v_w_in, attn_sinks=v_attn_sinks, sgu_ln_w=v_sgu_ln_w,
               sgu_ln_b=v_sgu_ln_b, sgu_w=v_sgu_w, sgu_b=v_sgu_b, proj_a=v_proj_a, proj_b=v_proj_b, w_out=v_w_out,
               norm2_w=v_norm2_w, ffn_w_gate=v_ffn_w_gate, ffn_w_up=v_ffn_w_up, ffn_conv_w=v_ffn_conv_w,
               ffn_conv_b=v_ffn_conv_b, ffn_w_down=v_ffn_w_down, final_norm_w=v_final_norm_w)
    me = 4 * lax.axis_index("x") + 2 * lax.axis_index("y") + lax.axis_index("c")
    ada_cols = ada_w.shape[2]

    conv_cols = ffn_conv_w.shape[2]
    first = _all_gather(jnp.concatenate([c, _flat_pack([ffn_conv_w], 7)], axis=0), name="ag_c")
    c_all = first[:, 0, :]
    conv_full = jnp.stack([first[j, 1:].reshape(-1)[:DEPTH * 3 * conv_cols].reshape(DEPTH, 3, conv_cols)
                           for j in range(N_DEV)], axis=2).reshape(DEPTH, 3, FFN_DIM)
    prod = _ada_fwd(c_all, ada_w)
    prod_all = _all_gather(prod, name="ag_mod")
    mine = lax.dynamic_index_in_dim(prod_all, me, axis=1, keepdims=False)
    mod = jnp.stack([mine[:, l * ada_cols:(l + 1) * ada_cols].reshape(-1) for l in range(DEPTH)]) + ada_b
    small = {n: wts[n] for n in SMALL}
    small["ffn_conv_w"] = conv_full

    mx, my, mc = _coords()
    cidx = jnp.reshape(mc, (1,)).astype(jnp.int32)
    chipidx = jnp.reshape(2 * mx + my, (1,)).astype(jnp.int32)
    rope = _rope_setup(positions[0])

    class Gather:
        def __init__(self, src, tag):
            self.tag, self.src = tag, src
            self.land = lax.dynamic_update_slice(lax.empty((2, 4) + src.shape, src.dtype), src[None, None],
                                                 (mc, 2 * mx + my, 0, 0))

        def ici_start(self, after):
            self.sems, (self.src, self.land), tok = _rdma_start([self.src, self.land], 3, _plan_gather_ici,
                                                                name=f"ag_{self.tag}_ici_start", after=after)
            return tok

        def ici_wait_d2d_start(self, after):
            _, land = _rdma_wait(self.sems, [self.src, self.land], 3, _plan_gather_ici, after, name=f"ag_{self.tag}_ici_wait")
            self.sems, (self.land,), tok = _rdma_start([land], 1, _plan_gather_d2d, name=f"ag_{self.tag}_d2d_start")
            return tok

        def d2d_wait(self, after):
            (land,) = _rdma_wait(self.sems, [self.land], 1, _plan_gather_d2d, after, name=f"ag_{self.tag}_d2d_wait")
            return _from_land(land)

    def weights_job(names, l, tag):
        job = Gather(_pack_shards(wts, l, names), tag)
        job.weights = lambda after: _unpack_weights(job.d2d_wait(after), names)
        return job

    W0 = _unpack_weights(_all_gather(_pack_shards(wts, 0, PART_IN), name="ag_w0_in", after=mod), PART_IN)
    W1 = {}
    rest = PART_MIX + PART_FFN
    g_rest0 = weights_job(rest, 0, "w0_rest")
    g_in1, g_rest1 = weights_job(PART_IN, 1, "w1_in"), weights_job(rest, 1, "w1_rest")

    def rest0_then_layer1(after):
        W0.update(g_rest0.weights(after))
        return g_rest1.ici_start(g_in1.ici_start(W0["w_down"]))

    x1, sv0 = _layer_fwd(0, x[0], mod[0], W0, small, rope,
                         {"mm_in": lambda after: g_rest0.ici_start(W0["wt_in"]), "sgu": g_rest0.ici_wait_d2d_start,
                          "mm_pa": rest0_then_layer1, "mm_gu": g_in1.ici_wait_d2d_start})
    g_rest1.ici_wait_d2d_start(x1)
    x2, sv1 = _layer_fwd(1, x1, mod[1], W1, small, rope,
                         {"mm_in": lambda after: W1.update(g_in1.weights(after)),
                          "mm_pa": lambda after: W1.update(g_rest1.weights(after))})
    gate2 = [mod[l][5 * D_MODEL:][None, :] for l in range(DEPTH)]
    dx2, dfw, loss_tile, do2, dg2 = _head(x2, final_norm_w[None, :], loss_target[0], (sv1["o2"], gate2[1]))
    loss = lax.psum(loss_tile[0, 0], ("x", "y", "c"))

    class Reduce:
        def __init__(self, names, tag):
            self.names, self.tag, self.rows = names, tag, _part_rows(names)

        def d2d_start(self, wg, after=None):
            self.sems, self.bufs, tok = _rdma_start([_pack_grads(wg, self.names), lax.empty((4, self.rows, 1024), BF)], 1,
                                                    _plan_reduce_d2d, name=f"rs_{self.tag}_d2d_start", after=after)
            return tok

        def d2d_wait_ici_start(self, after):
            g_t, land_a = _rdma_wait(self.sems, self.bufs, 1, _plan_reduce_d2d, after, name=f"rs_{self.tag}_d2d_wait")
            h = _sum_pair(g_t, land_a, cidx, name=f"rs_{self.tag}_sum_pair")
            self.sems, self.bufs, tok = _rdma_start([h, lax.empty((3, self.rows, 1024), BF)], 3, _plan_reduce_ici,
                                                    name=f"rs_{self.tag}_ici_start")
            return tok

        def ici_wait(self, after):
            h_t, land_b = _rdma_wait(self.sems, self.bufs, 3, _plan_reduce_ici, after, name=f"rs_{self.tag}_ici_wait")
            return _unpack_shard_grads(_sum_chips(h_t, land_b, chipidx, name=f"rs_{self.tag}_sum_chips"), self.names)

    (dx1, do2, dg2), wg1, sg1, dmod1 = _layer_bwd(1, dx2, do2, dg2, mod[1], W1, small, rope, sv1, below=(sv0["o2"], gate2[0]))
    r_all1, r_ffn0, r_mix0 = Reduce(BIG, "g1"), Reduce(PART_FFN, "g0_ffn"), Reduce(PART_IN + PART_MIX, "g0_mix")
    tok1 = r_all1.d2d_start(wg1)
    wg0, shard1 = {}, {}

    def layer1_done_then_mix0(after):
        shard1.update(r_all1.ici_wait(after))
        return r_mix0.d2d_wait_ici_start(r_mix0.d2d_start(wg0, shard1["w_in"]))

    (grad_x,), _, sg0, dmod0 = _layer_bwd(
        0, dx1, do2, dg2, mod[0], W0, small, rope, sv0, wg=wg0,
        hooks={"mm_down_dx": lambda after: tok1, "mm_gu_dx": r_all1.d2d_wait_ici_start,
               "merge_bwd": lambda after: r_ffn0.d2d_start(wg0, after), "mm_pa_dx": r_ffn0.d2d_wait_ici_start,
               "mm_in_dx": layer1_done_then_mix0})
    sg = {n: jnp.stack([sg0[n], sg1[n]]) for n in sg0}
    sg["final_norm_w"] = dfw[0]
    dmod = jnp.stack([dmod0, dmod1])
    vec_names = [n for n in SMALL if n not in ("ada_b", "sgu_w")] + ["ffn_conv_w"]
    vec_shapes = [(DEPTH, 6 * D_MODEL)] + [sg[n].shape for n in vec_names]
    vec_rows = -(-sum(int(np.prod(s)) for s in vec_shapes) // 1024 // 16) * 16
    sgu_rows = sgu_w.size // 1024
    g_small = Gather(jnp.concatenate([_flat_pack([dmod] + [sg[n] for n in vec_names], vec_rows),
                                      sg["sgu_w"].reshape(sgu_rows, 1024)], axis=0).astype(BF), "small")
    tok = g_small.ici_start(grad_x)

    shard0 = r_ffn0.ici_wait(tok)
    shard0.update(r_mix0.ici_wait(shard0["ffn_w_down"]))
    grads, delta, new_m, new_v = {}, {}, {}, {}
    for n in BIG:
        view = (lambda a: jnp.swapaxes(a, 1, 2)) if n in COL_SHARDED else (lambda a: a)
        out = _adamw_layers(view(wts[n]), [shard0[n], shard1[n]], view(mom[n]), view(var[n]), name=f"adamw_{n}")
        grads[n], delta[n], new_m[n], new_v[n] = [view(o) for o in out]

    sm_all = g_small.d2d_wait(g_small.ici_wait_d2d_start(delta["ffn_w_gate"]))
    sm_sum = _sum8(sm_all, name="sum_small")
    vec_sum = _flat_unpack(sm_sum[:vec_rows], vec_shapes)
    grads["ada_b"] = vec_sum[0]
    for n, gsum in zip(vec_names, vec_sum[1:]):
        grads[n] = gsum
    grads["sgu_w"] = sm_sum[vec_rows:].reshape(sgu_w.shape)
    grads["ffn_conv_w"] = lax.dynamic_slice_in_dim(grads["ffn_conv_w"], me * conv_cols, conv_cols, axis=2)
    dmod_all = sm_all[:, :DEPTH * 6, :].astype(F32).reshape(N_DEV, DEPTH, 6 * D_MODEL)
    dm_mine = lax.dynamic_slice_in_dim(dmod_all, me * ada_cols, ada_cols, axis=2).transpose(1, 0, 2)
    dm_mine = jnp.pad(dm_mine, ((0, 0), (0, 8), (0, 0)))
    grads["ada_w"] = _ada_bwd(jnp.pad(c_all, ((0, 8), (0, 0))), dm_mine)

    for n in WEIGHT_ORDER:
        if n not in delta:
            delta[n], new_m[n], new_v[n] = _adam2d(wts[n], grads[n], mom[n], var[n], name=f"adamw_{n}")
    return (loss, grad_x[None], *[grads[n] for n in WEIGHT_ORDER], *[delta[n] for n in WEIGHT_ORDER],
            *[new_m[n] for n in WEIGHT_ORDER], *[new_v[n] for n in WEIGHT_ORDER])
```

```python
import jax
import jax.numpy as jnp
import numpy as np
from jax import lax
from jax.experimental import pallas as pl
from jax.experimental.pallas import tpu as pltpu

F32 = jnp.float32
BF = jnp.bfloat16

N_DEV = 8
D_MODEL = 1024
DEPTH = 2
N_Q_HEADS = 16
N_KV_HEADS = 2
HEAD_DIM = 64
Q_PER_KV = N_Q_HEADS // N_KV_HEADS
ATTN_BLOCK = 128
ROPE_THETA = 500000.0
ROT_DIM = HEAD_DIM // 4
SGU_WIDTH = 1024
SGU_GROUPS = 8
SGU_CHUNK = 128
FFN_DIM = 2816
NORM_EPS = 1e-6
Q_END = N_Q_HEADS * HEAD_DIM
K_END = Q_END + N_KV_HEADS * HEAD_DIM
V_END = K_END + N_KV_HEADS * HEAD_DIM
Z_END = V_END + 2 * SGU_WIDTH
IN_COLS = Z_END + 2 * D_MODEL
P_Z, P_G, P_Q, P_K, P_V = 0, 2048, 4096, 5120, 5248

ADAM_LR = 0.001
ADAM_B1 = 0.9
ADAM_B2 = 0.999
ADAM_EPS = 1e-08
ADAM_WD = 0.01
ADAM_STEP = 10

VMEM_LIMIT_BYTES = 56 * 1024 * 1024

BIG = ("w_in", "proj_a", "proj_b", "w_out", "ffn_w_gate", "ffn_w_up", "ffn_w_down")
COL_SHARDED = ("w_in", "ffn_w_gate", "ffn_w_up")
BIG_SHAPE = {"w_in": (D_MODEL, IN_COLS), "proj_a": (SGU_WIDTH, D_MODEL), "proj_b": (Q_END, D_MODEL),
             "w_out": (D_MODEL, D_MODEL), "ffn_w_gate": (D_MODEL, FFN_DIM), "ffn_w_up": (D_MODEL, FFN_DIM),
             "ffn_w_down": (FFN_DIM, D_MODEL)}
BIG_ROWS = {n: BIG_SHAPE[n][0] * BIG_SHAPE[n][1] // N_DEV // 1024 for n in BIG}


def _pcall(body, **kw):
    return pl.pallas_call(body, **kw)


def _params(**kw):
    return pltpu.CompilerParams(vmem_limit_bytes=VMEM_LIMIT_BYTES, **kw)


def _tile(n, cap, unit=128):
    if n <= cap:
        return n
    best = 0
    t = unit
    while t <= cap:
        if n % t == 0:
            best = t
        t += unit
    assert best, (n, cap, unit)
    return best


def _mm(a, b, *, nt, out_dtype, name, res=None, gvec=None, after=None, tm=None, tn_cap=1024):
    a_list = list(a) if isinstance(a, (list, tuple)) else [a]
    b_list = list(b) if isinstance(b, (list, tuple)) else [b]
    a, b = a_list[0], b_list[0]
    M, K = a.shape
    N = b.shape[0] if nt else b.shape[1]
    k_total = sum(x.shape[1] for x in a_list)
    tm = _tile(M, tm or (1024 if k_total <= 1024 else 512), 8)
    tn = _tile(N, tn_cap)
    dn = (((1,), (1,)), ((), ())) if nt else (((1,), (0,)), ((), ()))

    def b_spec_of(x):
        k = x.shape[1] if nt else x.shape[0]
        return pl.BlockSpec((tn, k), lambda i, j: (j, 0)) if nt else pl.BlockSpec((k, tn), lambda i, j: (0, j))
    b_spec = b_spec_of(b)
    o_spec = pl.BlockSpec((tm, tn), lambda i, j: (i, j))
    if res is None:
        extra = [] if after is None else [after]
        n = len(a_list)

        def body(*refs):
            o_ref = refs[-1]
            acc = None
            for a_ref, b_ref in zip(refs[:n], refs[n:2 * n]):
                d = lax.dot_general(a_ref[...].astype(BF), b_ref[...].astype(BF), dn, preferred_element_type=F32)
                acc = d if acc is None else acc + d
            o_ref[...] = acc.astype(out_dtype)
        return _pcall(body, name=name, grid=(M // tm, N // tn),
                      in_specs=[pl.BlockSpec((tm, x.shape[1]), lambda i, j: (i, 0)) for x in a_list]
                      + [b_spec_of(x) for x in b_list] + [ANY] * len(extra), out_specs=o_spec,
                      out_shape=jax.ShapeDtypeStruct((M, N), out_dtype), compiler_params=_params())(
                          *a_list, *b_list, *extra)

    def body_res(a_ref, b_ref, r_ref, g_ref, o_ref, acc_ref):
        acc = lax.dot_general(a_ref[...].astype(BF), b_ref[...].astype(BF), dn, preferred_element_type=F32)
        acc_ref[...] = acc.astype(BF)
        o_ref[...] = r_ref[...] + g_ref[...] * acc
    return _pcall(body_res, name=name, grid=(M // tm, N // tn),
                  in_specs=[pl.BlockSpec((tm, K), lambda i, j: (i, 0)), b_spec, o_spec,
                            pl.BlockSpec((1, tn), lambda i, j: (0, j))],
                  out_specs=[o_spec, o_spec],
                  out_shape=[jax.ShapeDtypeStruct((M, N), F32), jax.ShapeDtypeStruct((M, N), BF)],
                  compiler_params=_params())(a, b, res, gvec)


def _mm_tn(a, b, *, name, out_dtype=BF, tk=2048, tm_cap=1408, tn_cap=1024):
    S, M = a.shape
    N = b.shape[1]
    tm = _tile(M, tm_cap)
    tn = _tile(N, tn_cap)
    if 2 * 2 * S * (tm + tn) <= VMEM_LIMIT_BYTES * 3 // 5:
        tk = S
    tk = _tile(S, tk, 8)
    nk = S // tk

    def body(a_ref, b_ref, o_ref, acc_ref):
        k = pl.program_id(2)

        @pl.when(k == 0)
        def _():
            acc_ref[...] = jnp.zeros_like(acc_ref)
        acc_ref[...] += lax.dot_general(a_ref[...].astype(BF), b_ref[...].astype(BF), (((0,), (0,)), ((), ())),
                                        preferred_element_type=F32)

        @pl.when(k == nk - 1)
        def _():
            o_ref[...] = acc_ref[...].astype(out_dtype)
    return _pcall(body, name=name, grid=(M // tm, N // tn, nk),
                  in_specs=[pl.BlockSpec((tk, tm), lambda i, j, k: (k, i)),
                            pl.BlockSpec((tk, tn), lambda i, j, k: (k, j))],
                  out_specs=pl.BlockSpec((tm, tn), lambda i, j, k: (i, j)),
                  out_shape=jax.ShapeDtypeStruct((M, N), out_dtype), scratch_shapes=[pltpu.VMEM((tm, tn), F32)],
                  compiler_params=_params())(a, b)


def _rms(x, w):
    return x * lax.rsqrt(jnp.mean(x * x, axis=-1, keepdims=True) + NORM_EPS) * w


def _normmod_fn(x, nw, sc, sh):
    return _rms(x, nw) * (1.0 + sc) + sh


def _gelu(x):
    return 0.5 * x * (1.0 + lax.erf(x * (2.0 ** -0.5)))


def _ln_gelu_fn(zv, w, b):
    v = _gelu(zv)
    mu = jnp.mean(v, axis=-1, keepdims=True)
    var = jnp.mean(jnp.square(v - mu), axis=-1, keepdims=True)
    return (v - mu) * lax.rsqrt(var + NORM_EPS) * w + b


def _sigmoid(x):
    return 1.0 / (1.0 + jnp.exp(-x))


def _row_spec(tm, n):
    return pl.BlockSpec((tm, n), lambda i: (i, 0))


def _vec_spec(n):
    return pl.BlockSpec((1, n), lambda i: (0, 0))


def _acc(ref, val):
    @pl.when(pl.program_id(0) == 0)
    def _():
        ref[...] = jnp.zeros_like(ref)
    ref[...] += val


def _norm_mm(x, nw, sc, sh, ws, *, name, after=None, tm=1024, tn_cap=768):
    S, K = x.shape
    N = ws[0].shape[0]
    tm = _tile(S, tm, 8)
    tn = _tile(N, tn_cap)
    nw_, ne = len(ws), 0 if after is None else 1

    def body(x_ref, nw_ref, sc_ref, sh_ref, *rest):
        w_refs = rest[:nw_]
        h_ref = rest[nw_ + ne]
        o_refs = rest[nw_ + ne + 1:nw_ + ne + 1 + nw_]
        h_s = rest[-1]

        @pl.when(pl.program_id(1) == 0)
        def _():
            hv = _normmod_fn(x_ref[...], nw_ref[...], sc_ref[...], sh_ref[...]).astype(BF)
            h_s[...] = hv
            h_ref[...] = hv
        for w_ref, o_ref in zip(w_refs, o_refs):
            o_ref[...] = lax.dot_general(h_s[...], w_ref[...], (((1,), (1,)), ((), ())),
                                         preferred_element_type=F32).astype(BF)
    row = pl.BlockSpec((tm, K), lambda i, j: (i, 0))
    vec = pl.BlockSpec((1, K), lambda i, j: (0, 0))
    out = pl.BlockSpec((tm, tn), lambda i, j: (i, j))
    res = _pcall(body, name=name, grid=(S // tm, N // tn),
                 in_specs=[row, vec, vec, vec] + [pl.BlockSpec((tn, K), lambda i, j: (j, 0))] * nw_ + [ANY] * ne,
                 out_specs=[row] + [out] * nw_,
                 out_shape=[jax.ShapeDtypeStruct((S, K), BF)] + [jax.ShapeDtypeStruct((S, N), BF)] * nw_,
                 scratch_shapes=[pltpu.VMEM((tm, K), BF)], compiler_params=_params())(
                     x, nw, sc, sh, *ws, *([] if after is None else [after]))
    return res[0], list(res[1:])


def _gate_bwd(dxv, o_ref, g_ref, do_ref, dg_ref):
    do_ref[...] = (dxv * g_ref[...]).astype(BF)
    _acc(dg_ref, jnp.sum(dxv * o_ref[...].astype(F32), axis=0, keepdims=True))


def _normmod_bwd(dh, x, nw, sc, sh, dres, gate, *, name, tm=512):
    S, Dm = x.shape
    tm = _tile(S, tm, 8)
    ng = 0 if gate is None else 2

    def body(dh_ref, x_ref, nw_ref, sc_ref, sh_ref, dres_ref, *rest):
        dx_ref, dnw_ref, dsc_ref, dsh_ref = rest[ng:ng + 4]
        xv, dy = x_ref[...], dh_ref[...]
        r = lax.rsqrt(jnp.mean(xv * xv, axis=-1, keepdims=True) + NORM_EPS)
        xn = xv * r
        t = dy * xn
        a = nw_ref[...] * (1.0 + sc_ref[...])
        dxv = dres_ref[...] + r * (dy * a - xn * jnp.mean(t * a, axis=-1, keepdims=True))
        dx_ref[...] = dxv
        ts = jnp.sum(t, axis=0, keepdims=True)
        _acc(dnw_ref, ts * (1.0 + sc_ref[...]))
        _acc(dsc_ref, ts * nw_ref[...])
        _acc(dsh_ref, jnp.sum(dy, axis=0, keepdims=True))
        if gate is not None:
            _gate_bwd(dxv, rest[0], rest[1], rest[ng + 4], rest[ng + 5])
    vec = jax.ShapeDtypeStruct((1, Dm), F32)
    gate_in = [] if gate is None else [_row_spec(tm, Dm), _vec_spec(Dm)]
    gate_out = [] if gate is None else [_row_spec(tm, Dm), _vec_spec(Dm)]
    gate_shape = [] if gate is None else [jax.ShapeDtypeStruct((S, Dm), BF), vec]
    return _pcall(body, name=name, grid=(S // tm,),
                  in_specs=[_row_spec(tm, Dm), _row_spec(tm, Dm), _vec_spec(Dm), _vec_spec(Dm), _vec_spec(Dm),
                            _row_spec(tm, Dm)] + gate_in,
                  out_specs=[_row_spec(tm, Dm), _vec_spec(Dm), _vec_spec(Dm), _vec_spec(Dm)] + gate_out,
                  out_shape=[jax.ShapeDtypeStruct((S, Dm), F32), vec, vec, vec] + gate_shape,
                  compiler_params=_params())(dh, x, nw, sc, sh, dres, *([] if gate is None else gate))


def _head(x, fw, target, gate, *, tm=512):
    S, Dm = x.shape
    tm = _tile(S, tm, 8)

    def body(x_ref, fw_ref, t_ref, o_ref, g_ref, dx_ref, dfw_ref, loss_ref, do_ref, dg_ref):
        xv, w = x_ref[...], fw_ref[...]
        r = lax.rsqrt(jnp.mean(xv * xv, axis=-1, keepdims=True) + NORM_EPS)
        xn = xv * r
        err = xn * w - t_ref[...]
        dy = err * (1.0 / Dm)
        t = dy * xn
        dx = r * (dy * w - xn * jnp.mean(t * w, axis=-1, keepdims=True))
        dx_ref[...] = dx
        _acc(dfw_ref, jnp.sum(t, axis=0, keepdims=True))
        part = 0.5 * jnp.sum(jnp.mean(err * err, axis=-1, keepdims=True), axis=0, keepdims=True)
        _acc(loss_ref, jnp.broadcast_to(part, (8, 128)))
        _gate_bwd(dx, o_ref, g_ref, do_ref, dg_ref)
    vec = jax.ShapeDtypeStruct((1, Dm), F32)
    return _pcall(body, name="head", grid=(S // tm,),
                  in_specs=[_row_spec(tm, Dm), _vec_spec(Dm), _row_spec(tm, Dm), _row_spec(tm, Dm), _vec_spec(Dm)],
                  out_specs=[_row_spec(tm, Dm), _vec_spec(Dm), pl.BlockSpec((8, 128), lambda i: (0, 0)),
                             _row_spec(tm, Dm), _vec_spec(Dm)],
                  out_shape=[jax.ShapeDtypeStruct((S, Dm), F32), vec, jax.ShapeDtypeStruct((8, 128), F32),
                             jax.ShapeDtypeStruct((S, Dm), BF), vec],
                  compiler_params=_params())(x, fw, target, *gate)


def _tril_mask():
    r = lax.broadcasted_iota(jnp.int32, (SGU_CHUNK, SGU_CHUNK), 0)
    c = lax.broadcasted_iota(jnp.int32, (SGU_CHUNK, SGU_CHUNK), 1)
    return c <= r


def _sgu_fwd(proj, lnw, lnb, w, b_t, *, name, after=None, tm=512):
    S = proj.shape[0]
    tm = _tile(S, tm, SGU_CHUNK)
    extra = [] if after is None else [after]

    def body(zu_ref, zv_ref, lnw_ref, lnb_ref, w_ref, bt_ref, *rest):
        o_ref = rest[-1]
        u = _gelu(zu_ref[...].astype(F32))
        vn = _ln_gelu_fn(zv_ref[...].astype(F32), lnw_ref[...], lnb_ref[...]).astype(BF)
        mask = _tril_mask()
        for g in range(SGU_GROUPS):
            wm = jnp.where(mask, w_ref[g], 0.0).astype(BF)
            cols = slice(g * 128, (g + 1) * 128)
            for ci in range(tm // SGU_CHUNK):
                rows = slice(ci * SGU_CHUNK, (ci + 1) * SGU_CHUNK)
                f = jnp.dot(wm, vn[rows, cols], preferred_element_type=F32) + bt_ref[:, g:g + 1]
                o_ref[rows, cols] = (u[rows, cols] * f).astype(BF)
    return _pcall(body, name=name, grid=(S // tm,),
                  in_specs=[pl.BlockSpec((tm, SGU_WIDTH), lambda i: (i, 0)), pl.BlockSpec((tm, SGU_WIDTH), lambda i: (i, 1)),
                            _vec_spec(SGU_WIDTH), _vec_spec(SGU_WIDTH),
                            pl.BlockSpec((SGU_GROUPS, 128, 128), lambda i: (0, 0, 0)),
                            pl.BlockSpec((128, SGU_GROUPS), lambda i: (0, 0))] + [ANY] * len(extra),
                  out_specs=_row_spec(tm, SGU_WIDTH), out_shape=jax.ShapeDtypeStruct((S, SGU_WIDTH), BF),
                  compiler_params=_params())(proj, proj, lnw, lnb, w, b_t, *extra)


def _sgu_bwd(dy, proj, lnw, lnb, w, b_t, dproj, *, name, tm=512):
    S = proj.shape[0]
    tm = _tile(S, tm, SGU_CHUNK)

    def body(dy_ref, zu_ref, zv_ref, lnw_ref, lnb_ref, w_ref, bt_ref, _, dz_ref, dlnw_ref, dlnb_ref, dw_ref, dbt_ref,
             f_s, dvn_s):
        first = pl.program_id(0) == 0

        @pl.when(first)
        def _():
            dw_ref[...] = jnp.zeros_like(dw_ref)
            dbt_ref[...] = jnp.zeros_like(dbt_ref)
        u, vjp_u = jax.vjp(_gelu, zu_ref[...].astype(F32))
        vn, vjp_v = jax.vjp(_ln_gelu_fn, zv_ref[...].astype(F32), lnw_ref[...], lnb_ref[...])
        vn = vn.astype(BF)
        dy_v = dy_ref[...]
        df = (dy_v * u).astype(BF)
        mask = _tril_mask()
        for g in range(SGU_GROUPS):
            wm = jnp.where(mask, w_ref[g], 0.0).astype(BF)
            cols = slice(g * 128, (g + 1) * 128)
            dwg = jnp.zeros((128, 128), F32)
            dbg = jnp.zeros((128, 1), F32)
            for ci in range(tm // SGU_CHUNK):
                rows = slice(ci * SGU_CHUNK, (ci + 1) * SGU_CHUNK)
                vn_c = vn[rows, cols]
                df_c = df[rows, cols]
                f_s[rows, cols] = jnp.dot(wm, vn_c, preferred_element_type=F32) + bt_ref[:, g:g + 1]
                dvn_s[rows, cols] = lax.dot_general(wm, df_c, (((0,), (0,)), ((), ())), preferred_element_type=F32)
                dwg = dwg + lax.dot_general(df_c, vn_c, (((1,), (1,)), ((), ())), preferred_element_type=F32)
                dbg = dbg + jnp.sum((dy_v[rows, cols] * u[rows, cols]), axis=1, keepdims=True)
            dw_ref[g] += jnp.where(mask, dwg, 0.0)
            dbt_ref[:, g:g + 1] += dbg
        (dzu,) = vjp_u(dy_v * f_s[...])
        dzv, dlnw, dlnb = vjp_v(dvn_s[...])
        dz_ref[:, :SGU_WIDTH] = dzu.astype(BF)
        dz_ref[:, SGU_WIDTH:] = dzv.astype(BF)
        _acc(dlnw_ref, dlnw)
        _acc(dlnb_ref, dlnb)
    vec = jax.ShapeDtypeStruct((1, SGU_WIDTH), F32)
    return _pcall(body, name=name, grid=(S // tm,),
                  in_specs=[_row_spec(tm, SGU_WIDTH),
                            pl.BlockSpec((tm, SGU_WIDTH), lambda i: (i, 0)), pl.BlockSpec((tm, SGU_WIDTH), lambda i: (i, 1)),
                            _vec_spec(SGU_WIDTH), _vec_spec(SGU_WIDTH),
                            pl.BlockSpec((SGU_GROUPS, 128, 128), lambda i: (0, 0, 0)),
                            pl.BlockSpec((128, SGU_GROUPS), lambda i: (0, 0)), ANY],
                  out_specs=[pl.BlockSpec((tm, 2 * SGU_WIDTH), lambda i: (i, P_Z // (2 * SGU_WIDTH))),
                             _vec_spec(SGU_WIDTH), _vec_spec(SGU_WIDTH),
                             pl.BlockSpec((SGU_GROUPS, 128, 128), lambda i: (0, 0, 0)),
                             pl.BlockSpec((128, SGU_GROUPS), lambda i: (0, 0))],
                  out_shape=[jax.ShapeDtypeStruct(dproj.shape, BF), vec, vec,
                             jax.ShapeDtypeStruct((SGU_GROUPS, 128, 128), F32),
                             jax.ShapeDtypeStruct((128, SGU_GROUPS), F32)],
                  scratch_shapes=[pltpu.VMEM((tm, SGU_WIDTH), F32), pltpu.VMEM((tm, SGU_WIDTH), F32)],
                  input_output_aliases={7: 0},
                  compiler_params=_params())(dy, proj, proj, lnw, lnb, w, b_t, dproj)


def _merge_fwd(y_sgu, y_attn, pa, pb, proj, *, name, after=None, tm=1024, tn=512):
    S, Dm = y_sgu.shape
    tm = _tile(S, tm, 8)
    nj = Dm // tn
    extra = [] if after is None else [after]

    def body(ys_ref, ya_ref, pa_ref, pb_ref, ga_ref, gb_ref, *rest):
        a_ref, b_ref, m_ref = rest[-3:]
        a = jnp.dot(ys_ref[...], pa_ref[...], preferred_element_type=F32)
        b = jnp.dot(ya_ref[...], pb_ref[...], preferred_element_type=F32)
        a_ref[...] = a.astype(BF)
        b_ref[...] = b.astype(BF)
        m_ref[...] = (_sigmoid(ga_ref[...].astype(F32)) * a + _sigmoid(gb_ref[...].astype(F32)) * b).astype(BF)
    row = pl.BlockSpec((tm, Dm), lambda i, j: (i, 0))
    col = pl.BlockSpec((Dm, tn), lambda i, j: (0, j))
    out = pl.BlockSpec((tm, tn), lambda i, j: (i, j))
    sh = jax.ShapeDtypeStruct((S, Dm), BF)
    return _pcall(body, name=name, grid=(S // tm, nj),
                  in_specs=[row, row, col, col, pl.BlockSpec((tm, tn), lambda i, j: (i, P_G // tn + j)),
                            pl.BlockSpec((tm, tn), lambda i, j: (i, (P_G + Dm) // tn + j))] + [ANY] * len(extra),
                  out_specs=[out, out, out], out_shape=[sh, sh, sh],
                  compiler_params=_params())(y_sgu, y_attn, pa, pb, proj, proj, *extra)


def _merge_bwd(do, w_out, a, b, proj, *, name, after=None, tm=512):
    S, Dm = a.shape
    tm = _tile(S, tm, 8)
    ga_blk, gb_blk = P_G // Dm, P_G // Dm + 1
    extra = [] if after is None else [after]

    def body(do_ref, w_ref, a_ref, b_ref, ga_ref, gb_ref, *rest):
        da_ref, db_ref, dg_ref = rest[-3:]
        dmv = lax.dot_general(do_ref[...], w_ref[...], (((1,), (1,)), ((), ())), preferred_element_type=F32)
        sa = _sigmoid(ga_ref[...].astype(F32))
        sb = _sigmoid(gb_ref[...].astype(F32))
        da_ref[...] = (dmv * sa).astype(BF)
        db_ref[...] = (dmv * sb).astype(BF)
        dg_ref[:, :Dm] = (dmv * a_ref[...].astype(F32) * sa * (1.0 - sa)).astype(BF)
        dg_ref[:, Dm:] = (dmv * b_ref[...].astype(F32) * sb * (1.0 - sb)).astype(BF)
    return _pcall(body, name=name, grid=(S // tm,),
                  in_specs=[_row_spec(tm, Dm), pl.BlockSpec((Dm, Dm), lambda i: (0, 0)), _row_spec(tm, Dm), _row_spec(tm, Dm),
                            pl.BlockSpec((tm, Dm), lambda i: (i, ga_blk)), pl.BlockSpec((tm, Dm), lambda i: (i, gb_blk))]
                  + [ANY] * len(extra),
                  out_specs=[_row_spec(tm, Dm), _row_spec(tm, Dm), pl.BlockSpec((tm, 2 * Dm), lambda i: (i, P_G // (2 * Dm)))],
                  out_shape=[jax.ShapeDtypeStruct((S, Dm), BF), jax.ShapeDtypeStruct((S, Dm), BF),
                             jax.ShapeDtypeStruct((S, IN_COLS), BF)],
                  compiler_params=_params())(do, w_out, a, b, proj, proj, *extra)


def _shift_rows(a, halo, k, up):
    n = a.shape[0]
    r8 = lax.broadcasted_iota(jnp.int32, (8, a.shape[1]), 0)
    if not up:
        rolled = pltpu.roll(a, k, 0)
        patch = jnp.where(r8 < k, pltpu.roll(halo, k, 0), rolled[:8])
        return jnp.concatenate([patch, rolled[8:]], axis=0)
    rolled = pltpu.roll(a, n - k, 0)
    patch = jnp.where(r8 >= 8 - k, pltpu.roll(halo, 8 - k, 0), rolled[n - 8:])
    return jnp.concatenate([rolled[:n - 8], patch], axis=0)


def _conv_taps(a, halo):
    return _shift_rows(a, halo, 2, False), _shift_rows(a, halo, 1, False), a


HALO = 16


def _prev_halo_spec(tm, Fd):
    return pl.BlockSpec((HALO, Fd), lambda i: (jnp.maximum(i * (tm // HALO) - 1, 0), 0))


def _conv_fwd(a_ref, halo_ref, cw_ref, cb_ref):
    halo = jnp.where(pl.program_id(0) > 0, halo_ref[...].astype(F32)[HALO - 8:], 0.0)
    t0, t1, t2 = _conv_taps(a_ref[...].astype(F32), halo)
    return t0, t1, t2, cb_ref[...] + cw_ref[0:1, :] * t0 + cw_ref[1:2, :] * t1 + cw_ref[2:3, :] * t2


def _ffn_act_fwd(a, up, cw, cb, *, name, tm=256):
    S, Fd = a.shape
    tm = _tile(S, tm, HALO)

    def body(a_ref, up_ref, halo_ref, cw_ref, cb_ref, o_ref, ac_ref):
        _, _, _, ac = _conv_fwd(a_ref, halo_ref, cw_ref, cb_ref)
        ac_ref[...] = ac.astype(BF)
        o_ref[...] = (ac * _sigmoid(ac) * up_ref[...].astype(F32)).astype(BF)
    sh = jax.ShapeDtypeStruct((S, Fd), BF)
    return _pcall(body, name=name, grid=(S // tm,),
                  in_specs=[_row_spec(tm, Fd), _row_spec(tm, Fd), _prev_halo_spec(tm, Fd),
                            pl.BlockSpec((3, Fd), lambda i: (0, 0)), _vec_spec(Fd)],
                  out_specs=[_row_spec(tm, Fd), _row_spec(tm, Fd)], out_shape=[sh, sh],
                  compiler_params=_params())(a, up, a, cw, cb)


def _ffn_act_bwd_a(dhf, ac, up, *, name, tm=512):
    S, Fd = ac.shape
    tm = _tile(S, tm, HALO)

    def body(dhf_ref, ac_ref, up_ref, dac_ref, dup_ref, dcb_ref):
        acv = ac_ref[...].astype(F32)
        s = _sigmoid(acv)
        dhf_v = dhf_ref[...].astype(F32)
        dup_ref[...] = (dhf_v * acv * s).astype(BF)
        dac = dhf_v * up_ref[...].astype(F32) * (s * (1.0 + acv * (1.0 - s)))
        dac_ref[...] = dac.astype(BF)
        _acc(dcb_ref, jnp.sum(dac, axis=0, keepdims=True))
    sh = jax.ShapeDtypeStruct((S, Fd), BF)
    return _pcall(body, name=name, grid=(S // tm,), in_specs=[_row_spec(tm, Fd)] * 3,
                  out_specs=[_row_spec(tm, Fd), _row_spec(tm, Fd), _vec_spec(Fd)],
                  out_shape=[sh, sh, jax.ShapeDtypeStruct((1, Fd), F32)], compiler_params=_params())(dhf, ac, up)


def _ffn_act_bwd_b(dac, a, cw, *, name, tm=256):
    S, Fd = dac.shape
    tm = _tile(S, tm, HALO)
    last = S // tm - 1

    def body(d_ref, halo_ref, a_ref, cw_ref, o_ref, dcw_ref):
        halo = jnp.where(pl.program_id(0) < last, halo_ref[...].astype(F32)[:8], 0.0)
        d = d_ref[...].astype(F32)
        d1, d2 = _shift_rows(d, halo, 1, True), _shift_rows(d, halo, 2, True)
        o_ref[...] = (cw_ref[2:3, :] * d + cw_ref[1:2, :] * d1 + cw_ref[0:1, :] * d2).astype(BF)
        av = a_ref[...].astype(F32)
        _acc(dcw_ref, jnp.concatenate([jnp.sum(av * d2, axis=0, keepdims=True),
                                       jnp.sum(av * d1, axis=0, keepdims=True),
                                       jnp.sum(av * d, axis=0, keepdims=True)], axis=0))
    return _pcall(body, name=name, grid=(S // tm,),
                  in_specs=[_row_spec(tm, Fd),
                            pl.BlockSpec((HALO, Fd), lambda i: (jnp.minimum((i + 1) * (tm // HALO), S // HALO - 1), 0)),
                            _row_spec(tm, Fd), pl.BlockSpec((3, Fd), lambda i: (0, 0))],
                  out_specs=[_row_spec(tm, Fd), pl.BlockSpec((3, Fd), lambda i: (0, 0))],
                  out_shape=[jax.ShapeDtypeStruct((S, Fd), BF), jax.ShapeDtypeStruct((3, Fd), F32)],
                  compiler_params=_params())(dac, dac, a, cw)


def _rope_tables(pos_col, inv_row, m1_row, m2_row):
    S = pos_col.shape[0]
    tm = _tile(S, 512, 8)

    def body(p_ref, inv_ref, m1_ref, m2_ref, c_ref, s1_ref, s2_ref):
        ang = p_ref[...] * inv_ref[...]
        sn = jnp.sin(ang)
        c_ref[...] = jnp.cos(ang)
        s1_ref[...] = -sn * m1_ref[...]
        s2_ref[...] = sn * m2_ref[...]
    sh = jax.ShapeDtypeStruct((S, 128), F32)
    return _pcall(body, name="rope_tables", grid=(S // tm,),
                  in_specs=[pl.BlockSpec((tm, 1), lambda i: (i, 0)), _vec_spec(128), _vec_spec(128), _vec_spec(128)],
                  out_specs=[_row_spec(tm, 128)] * 3, out_shape=[sh, sh, sh], compiler_params=_params())(
                      pos_col, inv_row, m1_row, m2_row)


def _rope_apply(x, c, s1, s2):
    outs = []
    for j in range(x.shape[1] // 128):
        xj = x[:, j * 128:(j + 1) * 128]
        outs.append(xj * c + pltpu.roll(xj, 120, 1) * s1 + pltpu.roll(xj, 8, 1) * s2)
    return outs[0] if len(outs) == 1 else jnp.concatenate(outs, axis=1)


def _rope_apply_t(d, c, s1, s2):
    outs = []
    for j in range(d.shape[1] // 128):
        dj = d[:, j * 128:(j + 1) * 128]
        outs.append(dj * c + pltpu.roll(dj * s1, 8, 1) + pltpu.roll(dj * s2, 120, 1))
    return outs[0] if len(outs) == 1 else jnp.concatenate(outs, axis=1)


def _rope_fwd(proj, c, s1, s2, *, name, tm=512):
    S = proj.shape[0]
    tm = _tile(S, tm, 8)

    def body(q_ref, k_ref, v_ref, c_ref, s1_ref, s2_ref, qo_ref, ko_ref, vo_ref):
        cv, s1v, s2v = c_ref[...], s1_ref[...], s2_ref[...]
        qo_ref[...] = (_rope_apply(q_ref[...].astype(F32), cv, s1v, s2v) * (HEAD_DIM ** -0.5)).astype(BF)
        ko_ref[...] = _rope_apply(k_ref[...].astype(F32), cv, s1v, s2v).astype(BF)
        vo_ref[...] = v_ref[...].astype(BF)
    return _pcall(body, name=name, grid=(S // tm,),
                  in_specs=[pl.BlockSpec((tm, Q_END), lambda i: (i, P_Q // Q_END)),
                            pl.BlockSpec((tm, 128), lambda i: (i, P_K // 128)),
                            pl.BlockSpec((tm, 128), lambda i: (i, P_V // 128)),
                            _row_spec(tm, 128), _row_spec(tm, 128), _row_spec(tm, 128)],
                  out_specs=[_row_spec(tm, Q_END), _row_spec(tm, 128), _row_spec(tm, 128)],
                  out_shape=[jax.ShapeDtypeStruct((S, Q_END), BF), jax.ShapeDtypeStruct((S, 128), BF),
                             jax.ShapeDtypeStruct((S, 128), BF)],
                  compiler_params=_params())(proj, proj, proj, c, s1, s2)


def _rope_bwd(dq, dk, dv, c, s1, s2, dproj, *, name, tm=512):
    S = dq.shape[0]
    tm = _tile(S, tm, 8)
    tabs = [_row_spec(tm, 128)] * 3
    shape = jax.ShapeDtypeStruct(dproj.shape, BF)

    def body_q(dq_ref, c_ref, s1_ref, s2_ref, _, o_ref):
        o_ref[...] = _rope_apply_t(dq_ref[...].astype(F32), c_ref[...], s1_ref[...], s2_ref[...]).astype(BF)
    dproj = _pcall(body_q, name=name + "_q", grid=(S // tm,), in_specs=[_row_spec(tm, Q_END)] + tabs + [ANY],
                   out_specs=pl.BlockSpec((tm, Q_END), lambda i: (i, P_Q // Q_END)), out_shape=shape,
                   input_output_aliases={4: 0}, compiler_params=_params())(dq, c, s1, s2, dproj)

    def body_kv(dk_ref, dv_ref, c_ref, s1_ref, s2_ref, _, o_ref):
        o_ref[:, :128] = _rope_apply_t(dk_ref[...], c_ref[...], s1_ref[...], s2_ref[...]).astype(BF)
        o_ref[:, 128:] = dv_ref[...].astype(BF)
    return _pcall(body_kv, name=name + "_kv", grid=(S // tm,),
                  in_specs=[_row_spec(tm, 128), _row_spec(tm, 128)] + tabs + [ANY],
                  out_specs=pl.BlockSpec((tm, 256), lambda i: (i, P_K // 256)), out_shape=shape,
                  input_output_aliases={5: 0}, compiler_params=_params())(dk, dv, c, s1, s2, dproj)


def _lane_lo(shape):
    return lax.broadcasted_iota(jnp.int32, shape, 1) < HEAD_DIM


def _stack_heads(x, g):
    lo = _lane_lo((ATTN_BLOCK, 128))
    zero = jnp.zeros((ATTN_BLOCK, 128), x.dtype)
    parts = []
    for p in range(Q_PER_KV // 2):
        xp = x[:, (g * 4 + p) * 128:(g * 4 + p + 1) * 128]
        parts += [jnp.where(lo, xp, zero), jnp.where(lo, zero, xp)]
    return jnp.concatenate(parts, axis=0)


def _unstack_heads(o2):
    lo = _lane_lo((ATTN_BLOCK, 128))
    return [jnp.where(lo, o2[2 * p * ATTN_BLOCK:(2 * p + 1) * ATTN_BLOCK], o2[(2 * p + 1) * ATTN_BLOCK:(2 * p + 2) * ATTN_BLOCK])
            for p in range(Q_PER_KV // 2)]


def _dup_half(prev, cur, g):
    x = jnp.concatenate([prev, cur], axis=0).astype(F32)
    lo = _lane_lo(x.shape)
    r = pltpu.roll(x, HEAD_DIM, 1)
    return (jnp.where(lo, x, r) if g == 0 else jnp.where(lo, r, x)).astype(BF)


def _fold_halves(x):
    return x + pltpu.roll(x, HEAD_DIM, 1)


def _attn_bias():
    i = lax.broadcasted_iota(jnp.int32, (Q_PER_KV * ATTN_BLOCK, 2 * ATTN_BLOCK), 0) & (ATTN_BLOCK - 1)
    j = lax.broadcasted_iota(jnp.int32, (Q_PER_KV * ATTN_BLOCK, 2 * ATTN_BLOCK), 1)
    band = (j > i) & (j <= i + ATTN_BLOCK)
    return jnp.stack([jnp.where(band & (j >= ATTN_BLOCK), 0.0, -jnp.inf), jnp.where(band, 0.0, -jnp.inf)]).astype(F32)


def _both(x):
    return jnp.concatenate([x, x], axis=1)


def _row_sums(x_bf):
    return jnp.dot(x_bf, jnp.ones((x_bf.shape[1], 128), BF), preferred_element_type=F32)


def _attn_probs(qs, kb, sink, bias):
    s = lax.dot_general(qs, kb, (((1,), (1,)), ((), ())), preferred_element_type=F32) + bias
    m = jnp.maximum(jnp.broadcast_to(jnp.max(s, axis=-1, keepdims=True), sink.shape), sink)
    return jnp.exp(s - _both(m)), jnp.exp(sink - m)


def _attn_specs(S):
    nb = S // ATTN_BLOCK
    qs = pl.BlockSpec((ATTN_BLOCK, Q_END), lambda n: (n, 0))
    cur = pl.BlockSpec((ATTN_BLOCK, 128), lambda n: (n, 0))
    prev = pl.BlockSpec((ATTN_BLOCK, 128), lambda n: (jnp.maximum(n - 1, 0), 0))
    sink = pl.BlockSpec((N_KV_HEADS, Q_PER_KV * ATTN_BLOCK, 128), lambda n: (0, 0, 0))
    bias = pl.BlockSpec((None, Q_PER_KV * ATTN_BLOCK, 2 * ATTN_BLOCK), lambda n: (jnp.minimum(n, 1), 0, 0))
    return nb, qs, cur, prev, sink, bias


def _attn_fwd(q, k, v, sink_rows, bias, *, name):
    S = q.shape[0]
    nb, qs, cur, prev, sink, bs = _attn_specs(S)

    def body(q_ref, kp_ref, kc_ref, vp_ref, vc_ref, sk_ref, b_ref, o_ref):
        for g in range(N_KV_HEADS):
            kb = _dup_half(kp_ref[...], kc_ref[...], g)
            vb = _dup_half(vp_ref[...], vc_ref[...], g)
            p, es = _attn_probs(_stack_heads(q_ref[...], g), kb, sk_ref[g], b_ref[...])
            ones = jnp.ones((2 * ATTN_BLOCK, 128), BF)
            o3 = jnp.dot(p.astype(BF), jnp.concatenate([vb, ones], axis=1), preferred_element_type=F32)
            o2 = o3[:, :128] / (o3[:, 128:] + es)
            for t, tile in enumerate(_unstack_heads(o2)):
                o_ref[:, (g * 4 + t) * 128:(g * 4 + t + 1) * 128] = tile.astype(BF)
    return _pcall(body, name=name, grid=(nb,), in_specs=[qs, prev, cur, prev, cur, sink, bs], out_specs=qs,
                  out_shape=jax.ShapeDtypeStruct(q.shape, BF), compiler_params=_params())(q, k, k, v, v, sink_rows, bias)


def _attn_bwd(do, q, k, v, sink_rows, bias, *, name):
    S = q.shape[0]
    nb, qs, cur, prev, sink, bs = _attn_specs(S)
    full = pl.BlockSpec((S, 128), lambda n: (0, 0))
    dsk_spec = pl.BlockSpec((N_KV_HEADS, Q_PER_KV, 128), lambda n: (0, 0, 0))

    def body(do_ref, q_ref, kp_ref, kc_ref, vp_ref, vc_ref, sk_ref, b_ref, dq_ref, dk_ref, dv_ref, dsk_ref):
        n = pl.program_id(0)

        @pl.when(n == 0)
        def _():
            dk_ref[...] = jnp.zeros_like(dk_ref)
            dv_ref[...] = jnp.zeros_like(dv_ref)
            dsk_ref[...] = jnp.zeros_like(dsk_ref)
        sub = lax.broadcasted_iota(jnp.int32, (Q_PER_KV, 128), 0)
        dkf, dvf = [], []
        for g in range(N_KV_HEADS):
            qst = _stack_heads(q_ref[...], g)
            dos = _stack_heads(do_ref[...], g)
            kb = _dup_half(kp_ref[...], kc_ref[...], g)
            vb = _dup_half(vp_ref[...], vc_ref[...], g)
            pu, es = _attn_probs(qst, kb, sk_ref[g], b_ref[...])
            inv = 1.0 / (_row_sums(pu.astype(BF)) + es)
            p = pu * _both(inv)
            dp = lax.dot_general(dos, vb, (((1,), (1,)), ((), ())), preferred_element_type=F32)
            dd = _row_sums((p * dp).astype(BF))
            ds = (p * (dp - _both(dd))).astype(BF)
            dq2 = jnp.dot(ds, kb, preferred_element_type=F32) * (HEAD_DIM ** -0.5)
            for t, tile in enumerate(_unstack_heads(dq2)):
                dq_ref[:, (g * 4 + t) * 128:(g * 4 + t + 1) * 128] = tile.astype(BF)
            dkf.append(_fold_halves(lax.dot_general(ds, qst, (((0,), (0,)), ((), ())), preferred_element_type=F32)))
            dvf.append(_fold_halves(lax.dot_general(p.astype(BF), dos, (((0,), (0,)), ((), ())),
                                                    preferred_element_type=F32)))
            dsr = -(es * inv * dd)
            upd = jnp.zeros((Q_PER_KV, 128), F32)
            for h in range(Q_PER_KV):
                upd = jnp.where(sub == h, jnp.sum(dsr[h * ATTN_BLOCK:(h + 1) * ATTN_BLOCK], axis=0, keepdims=True), upd)
            dsk_ref[g] += upd
        lo = _lane_lo((2 * ATTN_BLOCK, 128))
        dkb = jnp.where(lo, dkf[0], dkf[1])
        dvb = jnp.where(lo, dvf[0], dvf[1])
        r0 = pl.multiple_of(n * ATTN_BLOCK, ATTN_BLOCK)
        dk_ref[pl.ds(r0, ATTN_BLOCK), :] += dkb[ATTN_BLOCK:]
        dv_ref[pl.ds(r0, ATTN_BLOCK), :] += dvb[ATTN_BLOCK:]

        @pl.when(n > 0)
        def _():
            rp = pl.multiple_of((n - 1) * ATTN_BLOCK, ATTN_BLOCK)
            dk_ref[pl.ds(rp, ATTN_BLOCK), :] += dkb[:ATTN_BLOCK]
            dv_ref[pl.ds(rp, ATTN_BLOCK), :] += dvb[:ATTN_BLOCK]
    return _pcall(body, name=name, grid=(nb,), in_specs=[qs, qs, prev, cur, prev, cur, sink, bs],
                  out_specs=[qs, full, full, dsk_spec],
                  out_shape=[jax.ShapeDtypeStruct(q.shape, BF), jax.ShapeDtypeStruct((S, 128), F32),
                             jax.ShapeDtypeStruct((S, 128), F32), jax.ShapeDtypeStruct((N_KV_HEADS, Q_PER_KV, 128), F32)],
                  compiler_params=_params())(do, q, k, k, v, v, sink_rows, bias)


def _ada_fwd(c_all, ada_w):
    ncol = ada_w.shape[2]

    def body(c_ref, w_ref, o_ref):
        cv = c_ref[...]
        ca = (cv * _sigmoid(cv)).astype(BF)
        for l in range(DEPTH):
            o_ref[:, l * ncol:(l + 1) * ncol] = jnp.dot(ca, w_ref[l].astype(BF), preferred_element_type=F32)
    return _pcall(body, name="ada_fwd", out_shape=jax.ShapeDtypeStruct((N_DEV, DEPTH * ncol), F32),
                  compiler_params=_params())(c_all, ada_w)


def _ada_bwd(c_all, dm):
    ncol = dm.shape[2]

    def body(c_ref, dm_ref, o_ref):
        cv = c_ref[...]
        ca = (cv * _sigmoid(cv)).astype(BF)
        for l in range(DEPTH):
            o_ref[l] = lax.dot_general(ca, dm_ref[l].astype(BF), (((0,), (0,)), ((), ())), preferred_element_type=F32)
    return _pcall(body, name="ada_bwd", out_shape=jax.ShapeDtypeStruct((DEPTH, D_MODEL, ncol), F32),
                  compiler_params=_params())(c_all, dm)


def _adamw(w, g, m, v, *, name):
    R, C = w.shape
    tr = R
    for t in range(8, 513, 8):
        if R % t == 0:
            tr = t
    c1 = 1.0 - ADAM_B1 ** ADAM_STEP
    c2 = 1.0 - ADAM_B2 ** ADAM_STEP

    def body(w_ref, g_ref, m_ref, v_ref, d_ref, mo_ref, vo_ref):
        gv = g_ref[...]
        mn = ADAM_B1 * m_ref[...] + (1.0 - ADAM_B1) * gv
        vn = ADAM_B2 * v_ref[...] + (1.0 - ADAM_B2) * (gv * gv)
        mo_ref[...] = mn
        vo_ref[...] = vn
        d_ref[...] = -ADAM_LR * ((mn * (1.0 / c1)) / (jnp.sqrt(vn * (1.0 / c2)) + ADAM_EPS) + ADAM_WD * w_ref[...])
    spec = pl.BlockSpec((tr, C), lambda i: (i, 0))
    sh = jax.ShapeDtypeStruct((R, C), F32)
    return _pcall(body, name=name, grid=(R // tr,), in_specs=[spec] * 4, out_specs=[spec] * 3, out_shape=[sh, sh, sh],
                  compiler_params=_params())(w, g, m, v)


def _adamw_layers(w, g_layers, m, v, *, name):
    L, R, C = w.shape
    assert L == 2 and len(g_layers) == 2
    tr = R
    for t in range(8, 513, 8):
        if R % t == 0:
            tr = t
    c1 = 1.0 - ADAM_B1 ** ADAM_STEP
    c2 = 1.0 - ADAM_B2 ** ADAM_STEP

    def body(w_ref, g0_ref, g1_ref, m_ref, v_ref, go_ref, d_ref, mo_ref, vo_ref):
        gv = jnp.where(pl.program_id(0) == 0, g0_ref[...], g1_ref[...])
        go_ref[...] = gv
        mn = ADAM_B1 * m_ref[...] + (1.0 - ADAM_B1) * gv
        vn = ADAM_B2 * v_ref[...] + (1.0 - ADAM_B2) * (gv * gv)
        mo_ref[...] = mn
        vo_ref[...] = vn
        d_ref[...] = -ADAM_LR * ((mn * (1.0 / c1)) / (jnp.sqrt(vn * (1.0 / c2)) + ADAM_EPS) + ADAM_WD * w_ref[...])
    spec = pl.BlockSpec((None, tr, C), lambda l, i: (l, i, 0))
    sh = jax.ShapeDtypeStruct((L, R, C), F32)
    g_specs = [pl.BlockSpec((tr, C), lambda l, i, k=k: (jnp.where(l == k, i, 0), 0)) for k in range(L)]
    return _pcall(body, name=name, grid=(L, R // tr), in_specs=[spec] + g_specs + [spec, spec], out_specs=[spec] * 4,
                  out_shape=[sh] * 4, compiler_params=_params())(w, *g_layers, m, v)


def _sum8(parts, *, name):
    _, R, C = parts.shape
    tr = _tile(R, 512, 16)

    def body(p_ref, o_ref):
        acc = p_ref[0].astype(F32)
        for k in range(1, N_DEV):
            acc = acc + p_ref[k].astype(F32)
        o_ref[...] = acc
    return _pcall(body, name=name, grid=(R // tr,), in_specs=[pl.BlockSpec((N_DEV, tr, C), lambda i: (0, i, 0))],
                  out_specs=pl.BlockSpec((tr, C), lambda i: (i, 0)), out_shape=jax.ShapeDtypeStruct((R, C), F32),
                  compiler_params=_params())(parts)


MESH_ID = pl.DeviceIdType.MESH
ANY = pl.BlockSpec(memory_space=pl.ANY)


def _all_gather(x, *, name, after=None):
    R, C = x.shape
    extra = [] if after is None else [after]

    def body(x_ref, *rest):
        out_ref, send_sems, recv_sems, local_sem = rest[-4:]
        mx, my, mc = lax.axis_index("x"), lax.axis_index("y"), lax.axis_index("c")
        me, sibling = (mx, my, mc), (mx, my, 1 - mc)
        chips = [(1 - mx, my), (mx, 1 - my), (1 - mx, 1 - my)]

        def blk(px, py, pc):
            return out_ref.at[4 * px + 2 * py + pc]

        def copy(k, block, to, src=None):
            return pltpu.make_async_remote_copy(
                src_ref=blk(*block) if src is None else src, dst_ref=blk(*block),
                send_sem=send_sems.at[k], recv_sem=recv_sems.at[k], device_id=to, device_id_type=MESH_ID)

        mine = pltpu.make_async_copy(x_ref, blk(*me), local_sem)
        mine.start()
        first = [copy(0, me, sibling, src=x_ref)]
        first += [copy(1 + j, me, (*chip, mc), src=x_ref) for j, chip in enumerate(chips)]
        for cp in first:
            cp.start()
        passed = [copy(4 + j, (*chip, mc), sibling) for j, chip in enumerate(chips)]
        for j, chip in enumerate(chips):
            copy(1 + j, (*chip, mc), me).wait_recv()
            passed[j].start()
        copy(0, sibling, me).wait_recv()
        for j, chip in enumerate(chips):
            copy(4 + j, (*chip, 1 - mc), me).wait_recv()
        for cp in first + passed:
            cp.wait_send()
        mine.wait()
    return _pcall(body, name=name, in_specs=[ANY] * (1 + len(extra)), out_specs=ANY,
                  out_shape=jax.ShapeDtypeStruct((N_DEV, R, C), x.dtype),
                  scratch_shapes=[pltpu.SemaphoreType.DMA((7,)), pltpu.SemaphoreType.DMA((7,)), pltpu.SemaphoreType.DMA],
                  compiler_params=pltpu.CompilerParams(has_side_effects=True))(x, *extra)


HBM_SPEC = pl.BlockSpec(memory_space=pltpu.HBM)
SEM_SPEC = pl.BlockSpec(memory_space=pltpu.SEMAPHORE)
DATAFLOW = pltpu.SideEffectType.DATAFLOW_SIDE_EFFECTING


def _coords():
    return lax.axis_index("x"), lax.axis_index("y"), lax.axis_index("c")


def _other_chips(mx, my):
    return [(1 - mx, my), (mx, 1 - my), (1 - mx, 1 - my)]


def _plan_gather_ici(refs, send, recv):
    src, land = refs
    mx, my, mc = _coords()
    return [pltpu.make_async_remote_copy(src_ref=src, dst_ref=land.at[2 * mx + my, mc], send_sem=send[j], recv_sem=recv[j],
                                         device_id=(px, py, mc), device_id_type=MESH_ID)
            for j, (px, py) in enumerate(_other_chips(mx, my))]


N_CHIPS = 4


def _plan_gather_d2d(refs, send, recv):
    (land,) = refs
    mx, my, mc = _coords()
    return [pltpu.make_async_remote_copy(src_ref=land.at[q, mc], dst_ref=land.at[q, mc], send_sem=send[q], recv_sem=recv[q],
                                         device_id=(mx, my, 1 - mc), device_id_type=MESH_ID) for q in range(N_CHIPS)]


def _plan_reduce_d2d(refs, send, recv):
    g, land = refs
    mx, my, mc = _coords()
    return [pltpu.make_async_remote_copy(src_ref=g.at[q, 1 - mc], dst_ref=land.at[q], send_sem=send[q], recv_sem=recv[q],
                                         device_id=(mx, my, 1 - mc), device_id_type=MESH_ID) for q in range(N_CHIPS)]


def _plan_reduce_ici(refs, send, recv):
    h, land = refs
    mx, my, mc = _coords()
    return [pltpu.make_async_remote_copy(src_ref=h.at[2 * px + py], dst_ref=land.at[j], send_sem=send[j], recv_sem=recv[j],
                                         device_id=(px, py, mc), device_id_type=MESH_ID)
            for j, (px, py) in enumerate(_other_chips(mx, my))]


def _rdma_start(bufs, n, plan, *, name, after=None):
    nb = len(bufs)
    extra = [] if after is None else [after]
    ne = len(extra)

    def body(*refs):
        ins, send, recv = refs[:nb], refs[nb + ne:nb + ne + n], refs[nb + ne + n:nb + ne + 2 * n]
        token = refs[-1]
        for cp in plan(ins, send, recv):
            cp.start()
        token[...] = jnp.zeros_like(token)
    out = _pcall(body, name=name,
                 out_shape=tuple([pltpu.SemaphoreType.DMA(())] * (2 * n) + [pltpu.HBM(b.shape, b.dtype) for b in bufs]
                                 + [jax.ShapeDtypeStruct((8, 128), F32)]),
                 in_specs=tuple([HBM_SPEC] * nb + [ANY] * ne),
                 out_specs=tuple([SEM_SPEC] * (2 * n) + [HBM_SPEC] * nb + [pl.BlockSpec(memory_space=pltpu.VMEM)]),
                 input_output_aliases={i: 2 * n + i for i in range(nb)},
                 compiler_params=pltpu.CompilerParams(has_side_effects=DATAFLOW))(
                     *[pltpu.with_memory_space_constraint(b, pltpu.HBM) for b in bufs], *extra)
    return list(out[:2 * n]), list(out[2 * n:2 * n + nb]), out[-1]


def _rdma_wait(sems, bufs, n, plan, after, *, name):
    nb = len(bufs)

    def body(*refs):
        ins, send, recv = refs[:nb], refs[nb:nb + n], refs[nb + n:nb + 2 * n]
        for cp in plan(ins, send, recv):
            cp.wait_send()
            cp.wait_recv()
    out = _pcall(body, name=name, out_shape=tuple(pltpu.HBM(b.shape, b.dtype) for b in bufs),
                 in_specs=tuple([HBM_SPEC] * nb + [SEM_SPEC] * (2 * n) + [ANY]), out_specs=tuple([HBM_SPEC] * nb),
                 input_output_aliases={i: i for i in range(nb)},
                 compiler_params=pltpu.CompilerParams(has_side_effects=DATAFLOW))(*bufs, *sems, after)
    return list(out)


def _sum_pair(g, land, cidx, *, name):
    nchip, _, R, C = g.shape
    tr = _tile(R, 1056, 16)

    def body(c_ref, g_ref, l_ref, o_ref):
        o_ref[...] = g_ref[...] + l_ref[...]
    grid_spec = pltpu.PrefetchScalarGridSpec(
        num_scalar_prefetch=1, grid=(nchip, R // tr),
        in_specs=[pl.BlockSpec((None, None, tr, C), lambda p, i, c_ref: (p, c_ref[0], i, 0)),
                  pl.BlockSpec((None, tr, C), lambda p, i, c_ref: (p, i, 0))],
        out_specs=pl.BlockSpec((None, tr, C), lambda p, i, c_ref: (p, i, 0)))
    return _pcall(body, name=name, grid_spec=grid_spec, out_shape=jax.ShapeDtypeStruct((nchip, R, C), BF),
                  compiler_params=_params())(cidx, g, land)


def _sum_chips(h, land, chipidx, *, name):
    _, R, C = h.shape
    tr = _tile(R, 1056, 16)

    def body(c_ref, h_ref, l_ref, o_ref):
        acc = h_ref[...].astype(F32)
        for j in range(3):
            acc = acc + l_ref[j].astype(F32)
        o_ref[...] = acc
    grid_spec = pltpu.PrefetchScalarGridSpec(
        num_scalar_prefetch=1, grid=(R // tr,),
        in_specs=[pl.BlockSpec((None, tr, C), lambda i, c_ref: (c_ref[0], i, 0)),
                  pl.BlockSpec((3, tr, C), lambda i, c_ref: (0, i, 0))],
        out_specs=pl.BlockSpec((tr, C), lambda i, c_ref: (i, 0)))
    return _pcall(body, name=name, grid_spec=grid_spec, out_shape=jax.ShapeDtypeStruct((R, C), F32),
                  compiler_params=_params())(chipidx, h, land)


PART_IN = ("w_in",)
PART_MIX = ("proj_a", "proj_b", "w_out")
PART_FFN = ("ffn_w_gate", "ffn_w_up", "ffn_w_down")


def _part_rows(names):
    return sum(BIG_ROWS[n] for n in names)


def _part_offsets(names):
    off, r = {}, 0
    for n in names:
        off[n] = r
        r += BIG_ROWS[n]
    return off


def _pack_shards(shards, l, names):
    return jnp.concatenate([(shards[n][l].T if n in COL_SHARDED else shards[n][l]).astype(BF) for n in names], axis=0)


def _unpack_weights(full8, names):
    off = _part_offsets(names)

    def whole(n):
        return full8[:, off[n]:off[n] + BIG_ROWS[n], :].reshape(N_DEV * BIG_ROWS[n], 1024)
    out = {}
    if "w_in" in names:
        wt_in = whole("w_in")
        out["wt_in"] = jnp.concatenate([wt_in[V_END:], wt_in[:V_END]], axis=0)
    for n in ("proj_a", "proj_b", "w_out"):
        if n in names:
            out[n] = whole(n)
    if "ffn_w_gate" in names:
        out["wt_gate"], out["wt_up"], out["w_down"] = whole("ffn_w_gate"), whole("ffn_w_up"), whole("ffn_w_down")
    return out


def _from_land(land):
    return land.reshape(N_DEV, land.shape[2], 1024)


def _pack_grads(wg, names):
    full = {"proj_a": wg.get("proj_a"), "proj_b": wg.get("proj_b"), "w_out": wg.get("w_out"), "ffn_w_down": wg.get("w_down"),
            "ffn_w_gate": wg.get("wt_gate"), "ffn_w_up": wg.get("wt_up")}
    if "w_in" in names:
        full["w_in"] = jnp.concatenate([wg["wt_in"][P_Q:], wg["wt_in"][:P_Q]], axis=0)
    blocks = jnp.concatenate([full[n].reshape(N_DEV, BIG_ROWS[n], 1024) for n in names], axis=1)
    return blocks.reshape(N_CHIPS, 2, _part_rows(names), 1024)


def _unpack_shard_grads(gs, names):
    off = _part_offsets(names)
    return {n: gs[off[n]:off[n] + BIG_ROWS[n]] for n in names}


def _rope_setup(positions):
    S = positions.shape[0]
    inv = ROPE_THETA ** (-jnp.arange(0, ROT_DIM, 2, dtype=F32) / ROT_DIM)
    lane = np.arange(128) % HEAD_DIM
    half = ROT_DIM // 2
    inv_row = jnp.where(lane < ROT_DIM, jnp.tile(inv, 128 // half), 0.0)[None, :].astype(F32)
    m1_row = jnp.asarray((lane < half).astype(np.float32))[None, :]
    m2_row = jnp.asarray(((lane >= half) & (lane < ROT_DIM)).astype(np.float32))[None, :]
    return (*_rope_tables(positions.astype(F32).reshape(S, 1), inv_row, m1_row, m2_row), _attn_bias())


def _hook(hooks, point, after):
    f = None if hooks is None else hooks.get(point)
    return None if f is None else f(after)


def _layer_fwd(l, x, mod_l, W, small, rope, hooks=None):
    rc, rs1, rs2, bias = rope
    sh1, sc1, g1, sh2, sc2, g2 = [mod_l[i * D_MODEL:(i + 1) * D_MODEL][None, :] for i in range(6)]
    nw1, nw2 = small["norm1_w"][l][None, :], small["norm2_w"][l][None, :]
    tok = _hook(hooks, "mm_in", x)
    h, (proj,) = _norm_mm(x, nw1, sc1, sh1, [W["wt_in"]], name=f"mm_in{l}", after=tok, tm=2048, tn_cap=768)
    q_r, k_r, v_b = _rope_fwd(proj, rc, rs1, rs2, name=f"rope_fwd{l}")
    sink_rows = jnp.repeat(small["attn_sinks"][l].reshape(N_KV_HEADS, Q_PER_KV), ATTN_BLOCK, axis=1)
    sink_rows = jnp.broadcast_to(sink_rows[..., None], sink_rows.shape + (128,))
    y_attn = _attn_fwd(q_r, k_r, v_b, sink_rows, bias, name=f"attn_fwd{l}")
    lnw, lnb = small["sgu_ln_w"][l][None, :], small["sgu_ln_b"][l][None, :]
    sgu_bt = small["sgu_b"][l].T
    y_sgu = _sgu_fwd(proj, lnw, lnb, small["sgu_w"][l], sgu_bt, name=f"sgu_fwd{l}", after=_hook(hooks, "sgu", y_attn))
    tok = _hook(hooks, "mm_pa", y_sgu)
    a_br, b_br, merged = _merge_fwd(y_sgu, y_attn, W["proj_a"], W["proj_b"], proj, name=f"merge_fwd{l}", after=tok)
    x1, o1 = _mm(merged, W["w_out"], nt=False, out_dtype=F32, name=f"mm_out{l}", res=x, gvec=g1, tm=512)
    tok = _hook(hooks, "mm_gu", x1)
    h2, (a_g, a_u) = _norm_mm(x1, nw2, sc2, sh2, [W["wt_gate"], W["wt_up"]], name=f"mm_gu{l}", after=tok, tn_cap=1408)
    cw, cb = small["ffn_conv_w"][l], small["ffn_conv_b"][l][None, :]
    hf, a_c = _ffn_act_fwd(a_g, a_u, cw, cb, name=f"ffn_act_fwd{l}")
    x2, o2 = _mm(hf, W["w_down"], nt=False, out_dtype=F32, name=f"mm_down{l}", res=x1, gvec=g2)
    saved = dict(x=x, h=h, proj=proj, q_r=q_r, k_r=k_r, v_b=v_b, sink_rows=sink_rows, y_attn=y_attn, y_sgu=y_sgu,
                 a_br=a_br, b_br=b_br, merged=merged, x1=x1, o1=o1, h2=h2, a_g=a_g, a_u=a_u, a_c=a_c, hf=hf, o2=o2)
    return x2, saved


def _layer_bwd(l, dx, do2, dg2, mod_l, W, small, rope, sv, below=None, hooks=None, wg=None):
    rc, rs1, rs2, bias = rope
    sh1, sc1, g1, sh2, sc2, g2 = [mod_l[i * D_MODEL:(i + 1) * D_MODEL][None, :] for i in range(6)]
    nw1, nw2 = small["norm1_w"][l][None, :], small["norm2_w"][l][None, :]
    cw = small["ffn_conv_w"][l]
    lnw, lnb = small["sgu_ln_w"][l][None, :], small["sgu_ln_b"][l][None, :]
    sgu_bt = small["sgu_b"][l].T
    wg = {} if wg is None else wg
    dhf = _mm(do2, W["w_down"], nt=True, out_dtype=BF, name=f"mm_down_dx{l}", after=_hook(hooks, "mm_down_dx", do2),
              tn_cap=1408)
    wg["w_down"] = _mm_tn(sv["hf"], do2, name=f"mm_down_dw{l}")
    dac, dup, dcb = _ffn_act_bwd_a(dhf, sv["a_c"], sv["a_u"], name=f"ffn_act_bwd_a{l}")
    da, dcw = _ffn_act_bwd_b(dac, sv["a_g"], cw, name=f"ffn_act_bwd_b{l}")
    dh2 = _mm([da, dup], [W["wt_gate"], W["wt_up"]], nt=False, out_dtype=F32, name=f"mm_gu_dx{l}",
              after=_hook(hooks, "mm_gu_dx", da))
    wg["wt_gate"] = _mm_tn(da, sv["h2"], name=f"mm_gate_dw{l}")
    wg["wt_up"] = _mm_tn(dup, sv["h2"], name=f"mm_up_dw{l}")
    dx1, dnw2, dsc2, dsh2, do1, dg1 = _normmod_bwd(dh2, sv["x1"], nw2, sc2, sh2, dx, (sv["o1"], g1), name=f"normmod2_bwd{l}")
    d_a, d_b, dproj = _merge_bwd(do1, W["w_out"], sv["a_br"], sv["b_br"], sv["proj"], name=f"merge_bwd{l}",
                                 after=_hook(hooks, "merge_bwd", do1))
    wg["w_out"] = _mm_tn(sv["merged"], do1, name=f"mm_out_dw{l}")
    dysgu = _mm(d_a, W["proj_a"], nt=True, out_dtype=F32, name=f"mm_pa_dx{l}", after=_hook(hooks, "mm_pa_dx", d_a))
    dyattn = _mm(d_b, W["proj_b"], nt=True, out_dtype=BF, name=f"mm_pb_dx{l}")
    wg["proj_a"] = _mm_tn(sv["y_sgu"], d_a, name=f"mm_pa_dw{l}")
    wg["proj_b"] = _mm_tn(sv["y_attn"], d_b, name=f"mm_pb_dw{l}")
    dproj, dlnw, dlnb, dsguw, dsgubt = _sgu_bwd(dysgu, sv["proj"], lnw, lnb, small["sgu_w"][l], sgu_bt, dproj,
                                                name=f"sgu_bwd{l}")
    dq_r, dk_r, dv_b, dsk = _attn_bwd(dyattn, sv["q_r"], sv["k_r"], sv["v_b"], sv["sink_rows"], bias, name=f"attn_bwd{l}")
    dproj = _rope_bwd(dq_r, dk_r, dv_b, rc, rs1, rs2, dproj, name=f"rope_bwd{l}")
    wg["wt_in"] = _mm_tn(dproj, sv["h"], name=f"mm_in_dw{l}")
    dh = _mm(dproj, W["wt_in"], nt=False, out_dtype=F32, name=f"mm_in_dx{l}", after=_hook(hooks, "mm_in_dx", wg["wt_in"]))
    dx0, dnw1, dsc1, dsh1, *gate_below = _normmod_bwd(dh, sv["x"], nw1, sc1, sh1, dx1, below, name=f"normmod1_bwd{l}")
    dmod = jnp.concatenate([dsh1, dsc1, dg1, dsh2, dsc2, dg2], axis=1)[0]
    sg = {"norm1_w": dnw1[0], "norm2_w": dnw2[0], "attn_sinks": dsk[:, :, 0].reshape(N_Q_HEADS),
          "sgu_ln_w": dlnw[0], "sgu_ln_b": dlnb[0], "sgu_w": dsguw, "sgu_b": dsgubt.T,
          "ffn_conv_w": dcw, "ffn_conv_b": dcb[0]}
    return (dx0, *gate_below), wg, sg, dmod


SMALL = ("ada_b", "norm1_w", "attn_sinks", "sgu_ln_w", "sgu_ln_b", "sgu_w", "sgu_b", "norm2_w", "ffn_conv_b", "final_norm_w")
WEIGHT_ORDER = ("ada_w", "ada_b", "norm1_w", "w_in", "attn_sinks", "sgu_ln_w", "sgu_ln_b", "sgu_w", "sgu_b", "proj_a", "proj_b",
                "w_out", "norm2_w", "ffn_w_gate", "ffn_w_up", "ffn_conv_w", "ffn_conv_b", "ffn_w_down", "final_norm_w")


def _flat_pack(arrs, rows):
    flat = jnp.concatenate([a.reshape(-1) for a in arrs])
    return jnp.pad(flat, (0, rows * 1024 - flat.shape[0])).reshape(rows, 1024)


def _flat_unpack(buf, shapes):
    flat = buf.reshape(-1)
    out, o = [], 0
    for s in shapes:
        n = int(np.prod(s))
        out.append(flat[o:o + n].reshape(s))
        o += n
    return out


def _adam2d(w, g, m, v, *, name):
    shp = w.shape
    r2 = (int(np.prod(shp[:-1])), shp[-1]) if len(shp) > 1 else (1, shp[0])
    d, mn, vn = _adamw(w.reshape(r2), g.reshape(r2), m.reshape(r2), v.reshape(r2), name=name)
    return d.reshape(shp), mn.reshape(shp), vn.reshape(shp)


def kernel(x, c, positions, ada_w, ada_b, norm1_w, w_in, attn_sinks, sgu_ln_w, sgu_ln_b, sgu_w, sgu_b, proj_a, proj_b, w_out, norm2_w, ffn_w_gate, ffn_w_up, ffn_conv_w, ffn_conv_b, ffn_w_down, final_norm_w, loss_target, m_ada_w, m_ada_b, m_norm1_w, m_w_in, m_attn_sinks, m_sgu_ln_w, m_sgu_ln_b, m_sgu_w, m_sgu_b, m_proj_a, m_proj_b, m_w_out, m_norm2_w, m_ffn_w_gate, m_ffn_w_up, m_ffn_conv_w, m_ffn_conv_b, m_ffn_w_down, m_final_norm_w, v_ada_w, v_ada_b, v_norm1_w, v_w_in, v_attn_sinks, v_sgu_ln_w, v_sgu_ln_b, v_sgu_w, v_sgu_b, v_proj_a, v_proj_b, v_w_out, v_norm2_w, v_ffn_w_gate, v_ffn_w_up, v_ffn_conv_w, v_ffn_conv_b, v_ffn_w_down, v_final_norm_w):
    wts = dict(ada_w=ada_w, ada_b=ada_b, norm1_w=norm1_w, w_in=w_in, attn_sinks=attn_sinks, sgu_ln_w=sgu_ln_w,
               sgu_ln_b=sgu_ln_b, sgu_w=sgu_w, sgu_b=sgu_b, proj_a=proj_a, proj_b=proj_b, w_out=w_out, norm2_w=norm2_w,
               ffn_w_gate=ffn_w_gate, ffn_w_up=ffn_w_up, ffn_conv_w=ffn_conv_w, ffn_conv_b=ffn_conv_b,
               ffn_w_down=ffn_w_down, final_norm_w=final_norm_w)
    mom = dict(ada_w=m_ada_w, ada_b=m_ada_b, norm1_w=m_norm1_w, w_in=m_w_in, attn_sinks=m_attn_sinks, sgu_ln_w=m_sgu_ln_w,
               sgu_ln_b=m_sgu_ln_b, sgu_w=m_sgu_w, sgu_b=m_sgu_b, proj_a=m_proj_a, proj_b=m_proj_b, w_out=m_w_out,
               norm2_w=m_norm2_w, ffn_w_gate=m_ffn_w_gate, ffn_w_up=m_ffn_w_up, ffn_conv_w=m_ffn_conv_w,
               ffn_conv_b=m_ffn_conv_b, ffn_w_down=m_ffn_w_down, final_norm_w=m_final_norm_w)
    var = dict(ada_w=v_ada_w, ada_b=v_ada_b, norm1_w=v_norm1_w, w_in=v_w_in, attn_sinks=v_attn_sinks, sgu_ln_w=v_sgu_ln_w,
               sgu_ln_b=v_sgu_ln_b, sgu_w=v_sgu_w, sgu_b=v_sgu_b, proj_a=v_proj_a, proj_b=v_proj_b, w_out=v_w_out,
               norm2_w=v_norm2_w, ffn_w_gate=v_ffn_w_gate, ffn_w_up=v_ffn_w_up, ffn_conv_w=v_ffn_conv_w,
               ffn_conv_b=v_ffn_conv_b, ffn_w_down=v_ffn_w_down, final_norm_w=v_final_norm_w)
    me = 4 * lax.axis_index("x") + 2 * lax.axis_index("y") + lax.axis_index("c")
    ada_cols = ada_w.shape[2]

    conv_cols = ffn_conv_w.shape[2]
    first = _all_gather(jnp.concatenate([c, _flat_pack([ffn_conv_w], 7)], axis=0), name="ag_c")
    c_all = first[:, 0, :]
    conv_full = jnp.stack([first[j, 1:].reshape(-1)[:DEPTH * 3 * conv_cols].reshape(DEPTH, 3, conv_cols)
                           for j in range(N_DEV)], axis=2).reshape(DEPTH, 3, FFN_DIM)
    prod = _ada_fwd(c_all, ada_w)
    prod_all = _all_gather(prod, name="ag_mod")
    mine = lax.dynamic_index_in_dim(prod_all, me, axis=1, keepdims=False)
    mod = jnp.stack([mine[:, l * ada_cols:(l + 1) * ada_cols].reshape(-1) for l in range(DEPTH)]) + ada_b
    small = {n: wts[n] for n in SMALL}
    small["ffn_conv_w"] = conv_full

    mx, my, mc = _coords()
    cidx = jnp.reshape(mc, (1,)).astype(jnp.int32)
    chipidx = jnp.reshape(2 * mx + my, (1,)).astype(jnp.int32)
    rope = _rope_setup(positions[0])

    class Gather:
        def __init__(self, src, tag):
            self.tag, self.src = tag, src
            self.land = lax.dynamic_update_slice(lax.empty((N_CHIPS, 2) + src.shape, src.dtype), src[None, None],
                                                 (2 * mx + my, mc, 0, 0))

        def ici_start(self, after):
            self.sems, (self.src, self.land), tok = _rdma_start([self.src, self.land], 3, _plan_gather_ici,
                                                                name=f"ag_{self.tag}_ici_start", after=after)
            return tok

        def ici_wait_d2d_start(self, after):
            _, land = _rdma_wait(self.sems, [self.src, self.land], 3, _plan_gather_ici, after, name=f"ag_{self.tag}_ici_wait")
            self.sems, (self.land,), tok = _rdma_start([land], N_CHIPS, _plan_gather_d2d, name=f"ag_{self.tag}_d2d_start")
            return tok

        def d2d_wait(self, after):
            (land,) = _rdma_wait(self.sems, [self.land], N_CHIPS, _plan_gather_d2d, after, name=f"ag_{self.tag}_d2d_wait")
            return _from_land(land)

    def weights_job(names, l, tag):
        job = Gather(_pack_shards(wts, l, names), tag)
        job.weights = lambda after: _unpack_weights(job.d2d_wait(after), names)
        return job

    W0 = _unpack_weights(_all_gather(_pack_shards(wts, 0, PART_IN), name="ag_w0_in", after=mod), PART_IN)
    W1 = {}
    rest = PART_MIX + PART_FFN
    g_rest0 = weights_job(rest, 0, "w0_rest")
    g_in1, g_rest1 = weights_job(PART_IN, 1, "w1_in"), weights_job(rest, 1, "w1_rest")

    def rest0_then_layer1(after):
        W0.update(g_rest0.weights(after))
        return g_rest1.ici_start(g_in1.ici_start(W0["w_down"]))

    x1, sv0 = _layer_fwd(0, x[0], mod[0], W0, small, rope,
                         {"mm_in": lambda after: g_rest0.ici_start(W0["wt_in"]), "sgu": g_rest0.ici_wait_d2d_start,
                          "mm_pa": rest0_then_layer1, "mm_gu": g_in1.ici_wait_d2d_start})
    g_rest1.ici_wait_d2d_start(x1)
    x2, sv1 = _layer_fwd(1, x1, mod[1], W1, small, rope,
                         {"mm_in": lambda after: W1.update(g_in1.weights(after)),
                          "mm_pa": lambda after: W1.update(g_rest1.weights(after))})
    gate2 = [mod[l][5 * D_MODEL:][None, :] for l in range(DEPTH)]
    dx2, dfw, loss_tile, do2, dg2 = _head(x2, final_norm_w[None, :], loss_target[0], (sv1["o2"], gate2[1]))
    loss = lax.psum(loss_tile[0, 0], ("x", "y", "c"))

    class Reduce:
        def __init__(self, names, tag):
            self.names, self.tag, self.rows = names, tag, _part_rows(names)

        def d2d_start(self, wg, after=None):
            self.sems, self.bufs, tok = _rdma_start([_pack_grads(wg, self.names), lax.empty((N_CHIPS, self.rows, 1024), BF)],
                                                    N_CHIPS, _plan_reduce_d2d, name=f"rs_{self.tag}_d2d_start", after=after)
            return tok

        def d2d_wait_ici_start(self, after):
            g_t, land_a = _rdma_wait(self.sems, self.bufs, N_CHIPS, _plan_reduce_d2d, after, name=f"rs_{self.tag}_d2d_wait")
            h = _sum_pair(g_t, land_a, cidx, name=f"rs_{self.tag}_sum_pair")
            self.sems, self.bufs, tok = _rdma_start([h, lax.empty((3, self.rows, 1024), BF)], 3, _plan_reduce_ici,
                                                    name=f"rs_{self.tag}_ici_start")
            return tok

        def ici_wait(self, after):
            h_t, land_b = _rdma_wait(self.sems, self.bufs, 3, _plan_reduce_ici, after, name=f"rs_{self.tag}_ici_wait")
            return _unpack_shard_grads(_sum_chips(h_t, land_b, chipidx, name=f"rs_{self.tag}_sum_chips"), self.names)

    (dx1, do2, dg2), wg1, sg1, dmod1 = _layer_bwd(1, dx2, do2, dg2, mod[1], W1, small, rope, sv1, below=(sv0["o2"], gate2[0]))
    r_all1, r_ffn0, r_mix0 = Reduce(BIG, "g1"), Reduce(PART_FFN, "g0_ffn"), Reduce(PART_IN + PART_MIX, "g0_mix")
    tok1 = r_all1.d2d_start(wg1)
    wg0, shard1 = {}, {}

    def layer1_done_then_mix0(after):
        shard1.update(r_all1.ici_wait(after))
        return r_mix0.d2d_wait_ici_start(r_mix0.d2d_start(wg0, shard1["w_in"]))

    (grad_x,), _, sg0, dmod0 = _layer_bwd(
        0, dx1, do2, dg2, mod[0], W0, small, rope, sv0, wg=wg0,
        hooks={"mm_down_dx": lambda after: tok1, "mm_gu_dx": r_all1.d2d_wait_ici_start,
               "merge_bwd": lambda after: r_ffn0.d2d_start(wg0, after), "mm_pa_dx": r_ffn0.d2d_wait_ici_start,
               "mm_in_dx": layer1_done_then_mix0})
    sg = {n: jnp.stack([sg0[n], sg1[n]]) for n in sg0}
    sg["final_norm_w"] = dfw[0]
    dmod = jnp.stack([dmod0, dmod1])
    vec_names = [n for n in SMALL if n not in ("ada_b", "sgu_w")] + ["ffn_conv_w"]
    vec_shapes = [(DEPTH, 6 * D_MODEL)] + [sg[n].shape for n in vec_names]
    vec_rows = -(-sum(int(np.prod(s)) for s in vec_shapes) // 1024 // 16) * 16
    sgu_rows = sgu_w.size // 1024
    g_small = Gather(jnp.concatenate([_flat_pack([dmod] + [sg[n] for n in vec_names], vec_rows),
                                      sg["sgu_w"].reshape(sgu_rows, 1024)], axis=0).astype(BF), "small")
    tok = g_small.ici_start(grad_x)

    shard0 = r_ffn0.ici_wait(tok)
    shard0.update(r_mix0.ici_wait(shard0["ffn_w_down"]))
    grads, delta, new_m, new_v = {}, {}, {}, {}
    for n in BIG:
        view = (lambda a: jnp.swapaxes(a, 1, 2)) if n in COL_SHARDED else (lambda a: a)
        out = _adamw_layers(view(wts[n]), [shard0[n], shard1[n]], view(mom[n]), view(var[n]), name=f"adamw_{n}")
        grads[n], delta[n], new_m[n], new_v[n] = [view(o) for o in out]

    sm_all = g_small.d2d_wait(g_small.ici_wait_d2d_start(delta["ffn_w_gate"]))
    sm_sum = _sum8(sm_all, name="sum_small")
    vec_sum = _flat_unpack(sm_sum[:vec_rows], vec_shapes)
    grads["ada_b"] = vec_sum[0]
    for n, gsum in zip(vec_names, vec_sum[1:]):
        grads[n] = gsum
    grads["sgu_w"] = sm_sum[vec_rows:].reshape(sgu_w.shape)
    grads["ffn_conv_w"] = lax.dynamic_slice_in_dim(grads["ffn_conv_w"], me * conv_cols, conv_cols, axis=2)
    dmod_all = sm_all[:, :DEPTH * 6, :].astype(F32).reshape(N_DEV, DEPTH, 6 * D_MODEL)
    dm_mine = lax.dynamic_slice_in_dim(dmod_all, me * ada_cols, ada_cols, axis=2).transpose(1, 0, 2)
    dm_mine = jnp.pad(dm_mine, ((0, 0), (0, 8), (0, 0)))
    grads["ada_w"] = _ada_bwd(jnp.pad(c_all, ((0, 8), (0, 0))), dm_mine)

    for n in WEIGHT_ORDER:
        if n not in delta:
            delta[n], new_m[n], new_v[n] = _adam2d(wts[n], grads[n], mom[n], var[n], name=f"adamw_{n}")
    return (loss, grad_x[None], *[grads[n] for n in WEIGHT_ORDER], *[delta[n] for n in WEIGHT_ORDER],
            *[new_m[n] for n in WEIGHT_ORDER], *[new_v[n] for n in WEIGHT_ORDER])
```

```python
import jax
import jax.numpy as jnp
import numpy as np
from jax import lax
from jax.experimental import pallas as pl
from jax.experimental.pallas import tpu as pltpu

F32 = jnp.float32
BF = jnp.bfloat16

N_DEV = 8
D_MODEL = 1024
DEPTH = 2
N_Q_HEADS = 16
N_KV_HEADS = 2
HEAD_DIM = 64
Q_PER_KV = N_Q_HEADS // N_KV_HEADS
ATTN_BLOCK = 128
ROPE_THETA = 500000.0
ROT_DIM = HEAD_DIM // 4
SGU_WIDTH = 1024
SGU_GROUPS = 8
SGU_CHUNK = 128
FFN_DIM = 2816
NORM_EPS = 1e-6
Q_END = N_Q_HEADS * HEAD_DIM
K_END = Q_END + N_KV_HEADS * HEAD_DIM
V_END = K_END + N_KV_HEADS * HEAD_DIM
Z_END = V_END + 2 * SGU_WIDTH
IN_COLS = Z_END + 2 * D_MODEL
P_Z, P_G, P_Q, P_K, P_V = 0, 2048, 4096, 5120, 5248

ADAM_LR = 0.001
ADAM_B1 = 0.9
ADAM_B2 = 0.999
ADAM_EPS = 1e-08
ADAM_WD = 0.01
ADAM_STEP = 10

VMEM_LIMIT_BYTES = 56 * 1024 * 1024

BIG = ("w_in", "proj_a", "proj_b", "w_out", "ffn_w_gate", "ffn_w_up", "ffn_w_down")
COL_SHARDED = ("w_in", "ffn_w_gate", "ffn_w_up")
BIG_SHAPE = {"w_in": (D_MODEL, IN_COLS), "proj_a": (SGU_WIDTH, D_MODEL), "proj_b": (Q_END, D_MODEL),
             "w_out": (D_MODEL, D_MODEL), "ffn_w_gate": (D_MODEL, FFN_DIM), "ffn_w_up": (D_MODEL, FFN_DIM),
             "ffn_w_down": (FFN_DIM, D_MODEL)}
BIG_ROWS = {n: BIG_SHAPE[n][0] * BIG_SHAPE[n][1] // N_DEV // 1024 for n in BIG}


def _pcall(body, **kw):
    return pl.pallas_call(body, **kw)


def _params(**kw):
    return pltpu.CompilerParams(vmem_limit_bytes=VMEM_LIMIT_BYTES, **kw)


def _tile(n, cap, unit=128):
    if n <= cap:
        return n
    best = 0
    t = unit
    while t <= cap:
        if n % t == 0:
            best = t
        t += unit
    assert best, (n, cap, unit)
    return best


def _mm(a, b, *, nt, out_dtype, name, res=None, gvec=None, after=None, tm=None, tn_cap=1024):
    a_list = list(a) if isinstance(a, (list, tuple)) else [a]
    b_list = list(b) if isinstance(b, (list, tuple)) else [b]
    a, b = a_list[0], b_list[0]
    M, K = a.shape
    N = b.shape[0] if nt else b.shape[1]
    k_total = sum(x.shape[1] for x in a_list)
    tm = _tile(M, tm or (1024 if k_total <= 1024 else 512), 8)
    tn = _tile(N, tn_cap)
    dn = (((1,), (1,)), ((), ())) if nt else (((1,), (0,)), ((), ()))

    def b_spec_of(x):
        k = x.shape[1] if nt else x.shape[0]
        return pl.BlockSpec((tn, k), lambda i, j: (j, 0)) if nt else pl.BlockSpec((k, tn), lambda i, j: (0, j))
    b_spec = b_spec_of(b)
    o_spec = pl.BlockSpec((tm, tn), lambda i, j: (i, j))
    if res is None:
        extra = [] if after is None else [after]
        n = len(a_list)

        def body(*refs):
            o_ref = refs[-1]
            acc = None
            for a_ref, b_ref in zip(refs[:n], refs[n:2 * n]):
                d = lax.dot_general(a_ref[...].astype(BF), b_ref[...].astype(BF), dn, preferred_element_type=F32)
                acc = d if acc is None else acc + d
            o_ref[...] = acc.astype(out_dtype)
        return _pcall(body, name=name, grid=(M // tm, N // tn),
                      in_specs=[pl.BlockSpec((tm, x.shape[1]), lambda i, j: (i, 0)) for x in a_list]
                      + [b_spec_of(x) for x in b_list] + [ANY] * len(extra), out_specs=o_spec,
                      out_shape=jax.ShapeDtypeStruct((M, N), out_dtype), compiler_params=_params())(
                          *a_list, *b_list, *extra)

    def body_res(a_ref, b_ref, r_ref, g_ref, o_ref, acc_ref):
        acc = lax.dot_general(a_ref[...].astype(BF), b_ref[...].astype(BF), dn, preferred_element_type=F32)
        acc_ref[...] = acc.astype(BF)
        o_ref[...] = r_ref[...] + g_ref[...] * acc
    return _pcall(body_res, name=name, grid=(M // tm, N // tn),
                  in_specs=[pl.BlockSpec((tm, K), lambda i, j: (i, 0)), b_spec, o_spec,
                            pl.BlockSpec((1, tn), lambda i, j: (0, j))],
                  out_specs=[o_spec, o_spec],
                  out_shape=[jax.ShapeDtypeStruct((M, N), F32), jax.ShapeDtypeStruct((M, N), BF)],
                  compiler_params=_params())(a, b, res, gvec)


def _mm_tn(a, b, *, name, out_dtype=BF, tk=2048, tm_cap=1408, tn_cap=1024):
    S, M = a.shape
    N = b.shape[1]
    tm = _tile(M, tm_cap)
    tn = _tile(N, tn_cap)
    if 2 * 2 * S * (tm + tn) <= VMEM_LIMIT_BYTES * 3 // 5:
        tk = S
    tk = _tile(S, tk, 8)
    nk = S // tk

    def body(a_ref, b_ref, o_ref, acc_ref):
        k = pl.program_id(2)

        @pl.when(k == 0)
        def _():
            acc_ref[...] = jnp.zeros_like(acc_ref)
        acc_ref[...] += lax.dot_general(a_ref[...].astype(BF), b_ref[...].astype(BF), (((0,), (0,)), ((), ())),
                                        preferred_element_type=F32)

        @pl.when(k == nk - 1)
        def _():
            o_ref[...] = acc_ref[...].astype(out_dtype)
    return _pcall(body, name=name, grid=(M // tm, N // tn, nk),
                  in_specs=[pl.BlockSpec((tk, tm), lambda i, j, k: (k, i)),
                            pl.BlockSpec((tk, tn), lambda i, j, k: (k, j))],
                  out_specs=pl.BlockSpec((tm, tn), lambda i, j, k: (i, j)),
                  out_shape=jax.ShapeDtypeStruct((M, N), out_dtype), scratch_shapes=[pltpu.VMEM((tm, tn), F32)],
                  compiler_params=_params())(a, b)


def _rms(x, w):
    return x * lax.rsqrt(jnp.mean(x * x, axis=-1, keepdims=True) + NORM_EPS) * w


def _normmod_fn(x, nw, sc, sh):
    return _rms(x, nw) * (1.0 + sc) + sh


def _gelu(x):
    return 0.5 * x * (1.0 + lax.erf(x * (2.0 ** -0.5)))


def _ln_gelu_fn(zv, w, b):
    v = _gelu(zv)
    mu = jnp.mean(v, axis=-1, keepdims=True)
    var = jnp.mean(jnp.square(v - mu), axis=-1, keepdims=True)
    return (v - mu) * lax.rsqrt(var + NORM_EPS) * w + b


def _sigmoid(x):
    return 1.0 / (1.0 + jnp.exp(-x))


def _row_spec(tm, n):
    return pl.BlockSpec((tm, n), lambda i: (i, 0))


def _vec_spec(n):
    return pl.BlockSpec((1, n), lambda i: (0, 0))


def _acc(ref, val):
    @pl.when(pl.program_id(0) == 0)
    def _():
        ref[...] = jnp.zeros_like(ref)
    ref[...] += val


def _norm_mm(x, nw, sc, sh, ws, *, name, after=None, tm=1024, tn_cap=768):
    S, K = x.shape
    N = ws[0].shape[0]
    tm = _tile(S, tm, 8)
    tn = _tile(N, tn_cap)
    nw_, ne = len(ws), 0 if after is None else 1

    def body(x_ref, nw_ref, sc_ref, sh_ref, *rest):
        w_refs = rest[:nw_]
        h_ref = rest[nw_ + ne]
        o_refs = rest[nw_ + ne + 1:nw_ + ne + 1 + nw_]
        h_s = rest[-1]

        @pl.when(pl.program_id(1) == 0)
        def _():
            hv = _normmod_fn(x_ref[...], nw_ref[...], sc_ref[...], sh_ref[...]).astype(BF)
            h_s[...] = hv
            h_ref[...] = hv
        for w_ref, o_ref in zip(w_refs, o_refs):
            o_ref[...] = lax.dot_general(h_s[...], w_ref[...], (((1,), (1,)), ((), ())),
                                         preferred_element_type=F32).astype(BF)
    row = pl.BlockSpec((tm, K), lambda i, j: (i, 0))
    vec = pl.BlockSpec((1, K), lambda i, j: (0, 0))
    out = pl.BlockSpec((tm, tn), lambda i, j: (i, j))
    res = _pcall(body, name=name, grid=(S // tm, N // tn),
                 in_specs=[row, vec, vec, vec] + [pl.BlockSpec((tn, K), lambda i, j: (j, 0))] * nw_ + [ANY] * ne,
                 out_specs=[row] + [out] * nw_,
                 out_shape=[jax.ShapeDtypeStruct((S, K), BF)] + [jax.ShapeDtypeStruct((S, N), BF)] * nw_,
                 scratch_shapes=[pltpu.VMEM((tm, K), BF)], compiler_params=_params())(
                     x, nw, sc, sh, *ws, *([] if after is None else [after]))
    return res[0], list(res[1:])


def _gate_bwd(dxv, o_ref, g_ref, do_ref, dg_ref):
    do_ref[...] = (dxv * g_ref[...]).astype(BF)
    _acc(dg_ref, jnp.sum(dxv * o_ref[...].astype(F32), axis=0, keepdims=True))


def _normmod_bwd(dh, x, nw, sc, sh, dres, gate, *, name, tm=512):
    S, Dm = x.shape
    tm = _tile(S, tm, 8)
    ng = 0 if gate is None else 2

    def body(dh_ref, x_ref, nw_ref, sc_ref, sh_ref, dres_ref, *rest):
        dx_ref, dnw_ref, dsc_ref, dsh_ref = rest[ng:ng + 4]
        xv, dy = x_ref[...], dh_ref[...]
        r = lax.rsqrt(jnp.mean(xv * xv, axis=-1, keepdims=True) + NORM_EPS)
        xn = xv * r
        t = dy * xn
        a = nw_ref[...] * (1.0 + sc_ref[...])
        dxv = dres_ref[...] + r * (dy * a - xn * jnp.mean(t * a, axis=-1, keepdims=True))
        dx_ref[...] = dxv
        ts = jnp.sum(t, axis=0, keepdims=True)
        _acc(dnw_ref, ts * (1.0 + sc_ref[...]))
        _acc(dsc_ref, ts * nw_ref[...])
        _acc(dsh_ref, jnp.sum(dy, axis=0, keepdims=True))
        if gate is not None:
            _gate_bwd(dxv, rest[0], rest[1], rest[ng + 4], rest[ng + 5])
    vec = jax.ShapeDtypeStruct((1, Dm), F32)
    gate_in = [] if gate is None else [_row_spec(tm, Dm), _vec_spec(Dm)]
    gate_out = [] if gate is None else [_row_spec(tm, Dm), _vec_spec(Dm)]
    gate_shape = [] if gate is None else [jax.ShapeDtypeStruct((S, Dm), BF), vec]
    return _pcall(body, name=name, grid=(S // tm,),
                  in_specs=[_row_spec(tm, Dm), _row_spec(tm, Dm), _vec_spec(Dm), _vec_spec(Dm), _vec_spec(Dm),
                            _row_spec(tm, Dm)] + gate_in,
                  out_specs=[_row_spec(tm, Dm), _vec_spec(Dm), _vec_spec(Dm), _vec_spec(Dm)] + gate_out,
                  out_shape=[jax.ShapeDtypeStruct((S, Dm), F32), vec, vec, vec] + gate_shape,
                  compiler_params=_params())(dh, x, nw, sc, sh, dres, *([] if gate is None else gate))


def _head(x, fw, target, gate, *, tm=512):
    S, Dm = x.shape
    tm = _tile(S, tm, 8)

    def body(x_ref, fw_ref, t_ref, o_ref, g_ref, dx_ref, dfw_ref, loss_ref, do_ref, dg_ref):
        xv, w = x_ref[...], fw_ref[...]
        r = lax.rsqrt(jnp.mean(xv * xv, axis=-1, keepdims=True) + NORM_EPS)
        xn = xv * r
        err = xn * w - t_ref[...]
        dy = err * (1.0 / Dm)
        t = dy * xn
        dx = r * (dy * w - xn * jnp.mean(t * w, axis=-1, keepdims=True))
        dx_ref[...] = dx
        _acc(dfw_ref, jnp.sum(t, axis=0, keepdims=True))
        part = 0.5 * jnp.sum(jnp.mean(err * err, axis=-1, keepdims=True), axis=0, keepdims=True)
        _acc(loss_ref, jnp.broadcast_to(part, (8, 128)))
        _gate_bwd(dx, o_ref, g_ref, do_ref, dg_ref)
    vec = jax.ShapeDtypeStruct((1, Dm), F32)
    return _pcall(body, name="head", grid=(S // tm,),
                  in_specs=[_row_spec(tm, Dm), _vec_spec(Dm), _row_spec(tm, Dm), _row_spec(tm, Dm), _vec_spec(Dm)],
                  out_specs=[_row_spec(tm, Dm), _vec_spec(Dm), pl.BlockSpec((8, 128), lambda i: (0, 0)),
                             _row_spec(tm, Dm), _vec_spec(Dm)],
                  out_shape=[jax.ShapeDtypeStruct((S, Dm), F32), vec, jax.ShapeDtypeStruct((8, 128), F32),
                             jax.ShapeDtypeStruct((S, Dm), BF), vec],
                  compiler_params=_params())(x, fw, target, *gate)


def _tril_mask():
    r = lax.broadcasted_iota(jnp.int32, (SGU_CHUNK, SGU_CHUNK), 0)
    c = lax.broadcasted_iota(jnp.int32, (SGU_CHUNK, SGU_CHUNK), 1)
    return c <= r


def _sgu_fwd(proj, lnw, lnb, w, b_t, *, name, after=None, tm=512):
    S = proj.shape[0]
    tm = _tile(S, tm, SGU_CHUNK)
    extra = [] if after is None else [after]

    def body(zu_ref, zv_ref, lnw_ref, lnb_ref, w_ref, bt_ref, *rest):
        o_ref = rest[-1]
        u = _gelu(zu_ref[...].astype(F32))
        vn = _ln_gelu_fn(zv_ref[...].astype(F32), lnw_ref[...], lnb_ref[...]).astype(BF)
        mask = _tril_mask()
        for g in range(SGU_GROUPS):
            wm = jnp.where(mask, w_ref[g], 0.0).astype(BF)
            cols = slice(g * 128, (g + 1) * 128)
            for ci in range(tm // SGU_CHUNK):
                rows = slice(ci * SGU_CHUNK, (ci + 1) * SGU_CHUNK)
                f = jnp.dot(wm, vn[rows, cols], preferred_element_type=F32) + bt_ref[:, g:g + 1]
                o_ref[rows, cols] = (u[rows, cols] * f).astype(BF)
    return _pcall(body, name=name, grid=(S // tm,),
                  in_specs=[pl.BlockSpec((tm, SGU_WIDTH), lambda i: (i, 0)), pl.BlockSpec((tm, SGU_WIDTH), lambda i: (i, 1)),
                            _vec_spec(SGU_WIDTH), _vec_spec(SGU_WIDTH),
                            pl.BlockSpec((SGU_GROUPS, 128, 128), lambda i: (0, 0, 0)),
                            pl.BlockSpec((128, SGU_GROUPS), lambda i: (0, 0))] + [ANY] * len(extra),
                  out_specs=_row_spec(tm, SGU_WIDTH), out_shape=jax.ShapeDtypeStruct((S, SGU_WIDTH), BF),
                  compiler_params=_params())(proj, proj, lnw, lnb, w, b_t, *extra)


def _sgu_bwd(dy, proj, lnw, lnb, w, b_t, dproj, *, name, tm=512):
    S = proj.shape[0]
    tm = _tile(S, tm, SGU_CHUNK)

    def body(dy_ref, zu_ref, zv_ref, lnw_ref, lnb_ref, w_ref, bt_ref, _, dz_ref, dlnw_ref, dlnb_ref, dw_ref, dbt_ref,
             f_s, dvn_s):
        first = pl.program_id(0) == 0

        @pl.when(first)
        def _():
            dw_ref[...] = jnp.zeros_like(dw_ref)
            dbt_ref[...] = jnp.zeros_like(dbt_ref)
        u, vjp_u = jax.vjp(_gelu, zu_ref[...].astype(F32))
        vn, vjp_v = jax.vjp(_ln_gelu_fn, zv_ref[...].astype(F32), lnw_ref[...], lnb_ref[...])
        vn = vn.astype(BF)
        dy_v = dy_ref[...]
        df = (dy_v * u).astype(BF)
        mask = _tril_mask()
        for g in range(SGU_GROUPS):
            wm = jnp.where(mask, w_ref[g], 0.0).astype(BF)
            cols = slice(g * 128, (g + 1) * 128)
            dwg = jnp.zeros((128, 128), F32)
            dbg = jnp.zeros((128, 1), F32)
            for ci in range(tm // SGU_CHUNK):
                rows = slice(ci * SGU_CHUNK, (ci + 1) * SGU_CHUNK)
                vn_c = vn[rows, cols]
                df_c = df[rows, cols]
                f_s[rows, cols] = jnp.dot(wm, vn_c, preferred_element_type=F32) + bt_ref[:, g:g + 1]
                dvn_s[rows, cols] = lax.dot_general(wm, df_c, (((0,), (0,)), ((), ())), preferred_element_type=F32)
                dwg = dwg + lax.dot_general(df_c, vn_c, (((1,), (1,)), ((), ())), preferred_element_type=F32)
                dbg = dbg + jnp.sum((dy_v[rows, cols] * u[rows, cols]), axis=1, keepdims=True)
            dw_ref[g] += jnp.where(mask, dwg, 0.0)
            dbt_ref[:, g:g + 1] += dbg
        (dzu,) = vjp_u(dy_v * f_s[...])
        dzv, dlnw, dlnb = vjp_v(dvn_s[...])
        dz_ref[:, :SGU_WIDTH] = dzu.astype(BF)
        dz_ref[:, SGU_WIDTH:] = dzv.astype(BF)
        _acc(dlnw_ref, dlnw)
        _acc(dlnb_ref, dlnb)
    vec = jax.ShapeDtypeStruct((1, SGU_WIDTH), F32)
    return _pcall(body, name=name, grid=(S // tm,),
                  in_specs=[_row_spec(tm, SGU_WIDTH),
                            pl.BlockSpec((tm, SGU_WIDTH), lambda i: (i, 0)), pl.BlockSpec((tm, SGU_WIDTH), lambda i: (i, 1)),
                            _vec_spec(SGU_WIDTH), _vec_spec(SGU_WIDTH),
                            pl.BlockSpec((SGU_GROUPS, 128, 128), lambda i: (0, 0, 0)),
                            pl.BlockSpec((128, SGU_GROUPS), lambda i: (0, 0)), ANY],
                  out_specs=[pl.BlockSpec((tm, 2 * SGU_WIDTH), lambda i: (i, P_Z // (2 * SGU_WIDTH))),
                             _vec_spec(SGU_WIDTH), _vec_spec(SGU_WIDTH),
                             pl.BlockSpec((SGU_GROUPS, 128, 128), lambda i: (0, 0, 0)),
                             pl.BlockSpec((128, SGU_GROUPS), lambda i: (0, 0))],
                  out_shape=[jax.ShapeDtypeStruct(dproj.shape, BF), vec, vec,
                             jax.ShapeDtypeStruct((SGU_GROUPS, 128, 128), F32),
                             jax.ShapeDtypeStruct((128, SGU_GROUPS), F32)],
                  scratch_shapes=[pltpu.VMEM((tm, SGU_WIDTH), F32), pltpu.VMEM((tm, SGU_WIDTH), F32)],
                  input_output_aliases={7: 0},
                  compiler_params=_params())(dy, proj, proj, lnw, lnb, w, b_t, dproj)


def _merge_fwd(y_sgu, y_attn, pa, pb, proj, *, name, after=None, tm=1024, tn=512):
    S, Dm = y_sgu.shape
    tm = _tile(S, tm, 8)
    nj = Dm // tn
    extra = [] if after is None else [after]

    def body(ys_ref, ya_ref, pa_ref, pb_ref, ga_ref, gb_ref, *rest):
        a_ref, b_ref, m_ref = rest[-3:]
        a = jnp.dot(ys_ref[...], pa_ref[...], preferred_element_type=F32)
        b = jnp.dot(ya_ref[...], pb_ref[...], preferred_element_type=F32)
        a_ref[...] = a.astype(BF)
        b_ref[...] = b.astype(BF)
        m_ref[...] = (_sigmoid(ga_ref[...].astype(F32)) * a + _sigmoid(gb_ref[...].astype(F32)) * b).astype(BF)
    row = pl.BlockSpec((tm, Dm), lambda i, j: (i, 0))
    col = pl.BlockSpec((Dm, tn), lambda i, j: (0, j))
    out = pl.BlockSpec((tm, tn), lambda i, j: (i, j))
    sh = jax.ShapeDtypeStruct((S, Dm), BF)
    return _pcall(body, name=name, grid=(S // tm, nj),
                  in_specs=[row, row, col, col, pl.BlockSpec((tm, tn), lambda i, j: (i, P_G // tn + j)),
                            pl.BlockSpec((tm, tn), lambda i, j: (i, (P_G + Dm) // tn + j))] + [ANY] * len(extra),
                  out_specs=[out, out, out], out_shape=[sh, sh, sh],
                  compiler_params=_params())(y_sgu, y_attn, pa, pb, proj, proj, *extra)


def _merge_bwd(do, w_out, a, b, proj, *, name, after=None, tm=512):
    S, Dm = a.shape
    tm = _tile(S, tm, 8)
    ga_blk, gb_blk = P_G // Dm, P_G // Dm + 1
    extra = [] if after is None else [after]

    def body(do_ref, w_ref, a_ref, b_ref, ga_ref, gb_ref, *rest):
        da_ref, db_ref, dg_ref = rest[-3:]
        dmv = lax.dot_general(do_ref[...], w_ref[...], (((1,), (1,)), ((), ())), preferred_element_type=F32)
        sa = _sigmoid(ga_ref[...].astype(F32))
        sb = _sigmoid(gb_ref[...].astype(F32))
        da_ref[...] = (dmv * sa).astype(BF)
        db_ref[...] = (dmv * sb).astype(BF)
        dg_ref[:, :Dm] = (dmv * a_ref[...].astype(F32) * sa * (1.0 - sa)).astype(BF)
        dg_ref[:, Dm:] = (dmv * b_ref[...].astype(F32) * sb * (1.0 - sb)).astype(BF)
    return _pcall(body, name=name, grid=(S // tm,),
                  in_specs=[_row_spec(tm, Dm), pl.BlockSpec((Dm, Dm), lambda i: (0, 0)), _row_spec(tm, Dm), _row_spec(tm, Dm),
                            pl.BlockSpec((tm, Dm), lambda i: (i, ga_blk)), pl.BlockSpec((tm, Dm), lambda i: (i, gb_blk))]
                  + [ANY] * len(extra),
                  out_specs=[_row_spec(tm, Dm), _row_spec(tm, Dm), pl.BlockSpec((tm, 2 * Dm), lambda i: (i, P_G // (2 * Dm)))],
                  out_shape=[jax.ShapeDtypeStruct((S, Dm), BF), jax.ShapeDtypeStruct((S, Dm), BF),
                             jax.ShapeDtypeStruct((S, IN_COLS), BF)],
                  compiler_params=_params())(do, w_out, a, b, proj, proj, *extra)


def _shift_rows(a, halo, k, up):
    n = a.shape[0]
    r8 = lax.broadcasted_iota(jnp.int32, (8, a.shape[1]), 0)
    if not up:
        rolled = pltpu.roll(a, k, 0)
        patch = jnp.where(r8 < k, pltpu.roll(halo, k, 0), rolled[:8])
        return jnp.concatenate([patch, rolled[8:]], axis=0)
    rolled = pltpu.roll(a, n - k, 0)
    patch = jnp.where(r8 >= 8 - k, pltpu.roll(halo, 8 - k, 0), rolled[n - 8:])
    return jnp.concatenate([rolled[:n - 8], patch], axis=0)


def _conv_taps(a, halo):
    return _shift_rows(a, halo, 2, False), _shift_rows(a, halo, 1, False), a


HALO = 16


def _prev_halo_spec(tm, Fd):
    return pl.BlockSpec((HALO, Fd), lambda i: (jnp.maximum(i * (tm // HALO) - 1, 0), 0))


def _conv_fwd(a_ref, halo_ref, cw_ref, cb_ref):
    halo = jnp.where(pl.program_id(0) > 0, halo_ref[...].astype(F32)[HALO - 8:], 0.0)
    t0, t1, t2 = _conv_taps(a_ref[...].astype(F32), halo)
    return t0, t1, t2, cb_ref[...] + cw_ref[0:1, :] * t0 + cw_ref[1:2, :] * t1 + cw_ref[2:3, :] * t2


def _ffn_act_fwd(a, up, cw, cb, *, name, tm=256):
    S, Fd = a.shape
    tm = _tile(S, tm, HALO)

    def body(a_ref, up_ref, halo_ref, cw_ref, cb_ref, o_ref, ac_ref):
        _, _, _, ac = _conv_fwd(a_ref, halo_ref, cw_ref, cb_ref)
        ac_ref[...] = ac.astype(BF)
        o_ref[...] = (ac * _sigmoid(ac) * up_ref[...].astype(F32)).astype(BF)
    sh = jax.ShapeDtypeStruct((S, Fd), BF)
    return _pcall(body, name=name, grid=(S // tm,),
                  in_specs=[_row_spec(tm, Fd), _row_spec(tm, Fd), _prev_halo_spec(tm, Fd),
                            pl.BlockSpec((3, Fd), lambda i: (0, 0)), _vec_spec(Fd)],
                  out_specs=[_row_spec(tm, Fd), _row_spec(tm, Fd)], out_shape=[sh, sh],
                  compiler_params=_params())(a, up, a, cw, cb)


def _ffn_act_bwd_a(dhf, ac, up, *, name, tm=512):
    S, Fd = ac.shape
    tm = _tile(S, tm, HALO)

    def body(dhf_ref, ac_ref, up_ref, dac_ref, dup_ref, dcb_ref):
        acv = ac_ref[...].astype(F32)
        s = _sigmoid(acv)
        dhf_v = dhf_ref[...].astype(F32)
        dup_ref[...] = (dhf_v * acv * s).astype(BF)
        dac = dhf_v * up_ref[...].astype(F32) * (s * (1.0 + acv * (1.0 - s)))
        dac_ref[...] = dac.astype(BF)
        _acc(dcb_ref, jnp.sum(dac, axis=0, keepdims=True))
    sh = jax.ShapeDtypeStruct((S, Fd), BF)
    return _pcall(body, name=name, grid=(S // tm,), in_specs=[_row_spec(tm, Fd)] * 3,
                  out_specs=[_row_spec(tm, Fd), _row_spec(tm, Fd), _vec_spec(Fd)],
                  out_shape=[sh, sh, jax.ShapeDtypeStruct((1, Fd), F32)], compiler_params=_params())(dhf, ac, up)


def _ffn_act_bwd_b(dac, a, cw, *, name, tm=256):
    S, Fd = dac.shape
    tm = _tile(S, tm, HALO)
    last = S // tm - 1

    def body(d_ref, halo_ref, a_ref, cw_ref, o_ref, dcw_ref):
        halo = jnp.where(pl.program_id(0) < last, halo_ref[...].astype(F32)[:8], 0.0)
        d = d_ref[...].astype(F32)
        d1, d2 = _shift_rows(d, halo, 1, True), _shift_rows(d, halo, 2, True)
        o_ref[...] = (cw_ref[2:3, :] * d + cw_ref[1:2, :] * d1 + cw_ref[0:1, :] * d2).astype(BF)
        av = a_ref[...].astype(F32)
        _acc(dcw_ref, jnp.concatenate([jnp.sum(av * d2, axis=0, keepdims=True),
                                       jnp.sum(av * d1, axis=0, keepdims=True),
                                       jnp.sum(av * d, axis=0, keepdims=True)], axis=0))
    return _pcall(body, name=name, grid=(S // tm,),
                  in_specs=[_row_spec(tm, Fd),
                            pl.BlockSpec((HALO, Fd), lambda i: (jnp.minimum((i + 1) * (tm // HALO), S // HALO - 1), 0)),
                            _row_spec(tm, Fd), pl.BlockSpec((3, Fd), lambda i: (0, 0))],
                  out_specs=[_row_spec(tm, Fd), pl.BlockSpec((3, Fd), lambda i: (0, 0))],
                  out_shape=[jax.ShapeDtypeStruct((S, Fd), BF), jax.ShapeDtypeStruct((3, Fd), F32)],
                  compiler_params=_params())(dac, dac, a, cw)


def _rope_tables(pos_col, inv_row, m1_row, m2_row):
    S = pos_col.shape[0]
    tm = _tile(S, 512, 8)

    def body(p_ref, inv_ref, m1_ref, m2_ref, c_ref, s1_ref, s2_ref):
        ang = p_ref[...] * inv_ref[...]
        sn = jnp.sin(ang)
        c_ref[...] = jnp.cos(ang)
        s1_ref[...] = -sn * m1_ref[...]
        s2_ref[...] = sn * m2_ref[...]
    sh = jax.ShapeDtypeStruct((S, 128), F32)
    return _pcall(body, name="rope_tables", grid=(S // tm,),
                  in_specs=[pl.BlockSpec((tm, 1), lambda i: (i, 0)), _vec_spec(128), _vec_spec(128), _vec_spec(128)],
                  out_specs=[_row_spec(tm, 128)] * 3, out_shape=[sh, sh, sh], compiler_params=_params())(
                      pos_col, inv_row, m1_row, m2_row)


def _rope_apply(x, c, s1, s2):
    outs = []
    for j in range(x.shape[1] // 128):
        xj = x[:, j * 128:(j + 1) * 128]
        outs.append(xj * c + pltpu.roll(xj, 120, 1) * s1 + pltpu.roll(xj, 8, 1) * s2)
    return outs[0] if len(outs) == 1 else jnp.concatenate(outs, axis=1)


def _rope_apply_t(d, c, s1, s2):
    outs = []
    for j in range(d.shape[1] // 128):
        dj = d[:, j * 128:(j + 1) * 128]
        outs.append(dj * c + pltpu.roll(dj * s1, 8, 1) + pltpu.roll(dj * s2, 120, 1))
    return outs[0] if len(outs) == 1 else jnp.concatenate(outs, axis=1)


def _rope_fwd(proj, c, s1, s2, *, name, tm=512):
    S = proj.shape[0]
    tm = _tile(S, tm, 8)

    def body(q_ref, k_ref, v_ref, c_ref, s1_ref, s2_ref, qo_ref, ko_ref, vo_ref):
        cv, s1v, s2v = c_ref[...], s1_ref[...], s2_ref[...]
        qo_ref[...] = (_rope_apply(q_ref[...].astype(F32), cv, s1v, s2v) * (HEAD_DIM ** -0.5)).astype(BF)
        ko_ref[...] = _rope_apply(k_ref[...].astype(F32), cv, s1v, s2v).astype(BF)
        vo_ref[...] = v_ref[...].astype(BF)
    return _pcall(body, name=name, grid=(S // tm,),
                  in_specs=[pl.BlockSpec((tm, Q_END), lambda i: (i, P_Q // Q_END)),
                            pl.BlockSpec((tm, 128), lambda i: (i, P_K // 128)),
                            pl.BlockSpec((tm, 128), lambda i: (i, P_V // 128)),
                            _row_spec(tm, 128), _row_spec(tm, 128), _row_spec(tm, 128)],
                  out_specs=[_row_spec(tm, Q_END), _row_spec(tm, 128), _row_spec(tm, 128)],
                  out_shape=[jax.ShapeDtypeStruct((S, Q_END), BF), jax.ShapeDtypeStruct((S, 128), BF),
                             jax.ShapeDtypeStruct((S, 128), BF)],
                  compiler_params=_params())(proj, proj, proj, c, s1, s2)


def _rope_bwd(dq, dk, dv, c, s1, s2, dproj, *, name, tm=512):
    S = dq.shape[0]
    tm = _tile(S, tm, 8)
    tabs = [_row_spec(tm, 128)] * 3
    shape = jax.ShapeDtypeStruct(dproj.shape, BF)

    def body_q(dq_ref, c_ref, s1_ref, s2_ref, _, o_ref):
        o_ref[...] = _rope_apply_t(dq_ref[...].astype(F32), c_ref[...], s1_ref[...], s2_ref[...]).astype(BF)
    dproj = _pcall(body_q, name=name + "_q", grid=(S // tm,), in_specs=[_row_spec(tm, Q_END)] + tabs + [ANY],
                   out_specs=pl.BlockSpec((tm, Q_END), lambda i: (i, P_Q // Q_END)), out_shape=shape,
                   input_output_aliases={4: 0}, compiler_params=_params())(dq, c, s1, s2, dproj)

    def body_kv(dk_ref, dv_ref, c_ref, s1_ref, s2_ref, _, o_ref):
        o_ref[:, :128] = _rope_apply_t(dk_ref[...], c_ref[...], s1_ref[...], s2_ref[...]).astype(BF)
        o_ref[:, 128:] = dv_ref[...].astype(BF)
    return _pcall(body_kv, name=name + "_kv", grid=(S // tm,),
                  in_specs=[_row_spec(tm, 128), _row_spec(tm, 128)] + tabs + [ANY],
                  out_specs=pl.BlockSpec((tm, 256), lambda i: (i, P_K // 256)), out_shape=shape,
                  input_output_aliases={5: 0}, compiler_params=_params())(dk, dv, c, s1, s2, dproj)


def _lane_lo(shape):
    return lax.broadcasted_iota(jnp.int32, shape, 1) < HEAD_DIM


def _stack_heads(x, g):
    lo = _lane_lo((ATTN_BLOCK, 128))
    zero = jnp.zeros((ATTN_BLOCK, 128), x.dtype)
    parts = []
    for p in range(Q_PER_KV // 2):
        xp = x[:, (g * 4 + p) * 128:(g * 4 + p + 1) * 128]
        parts += [jnp.where(lo, xp, zero), jnp.where(lo, zero, xp)]
    return jnp.concatenate(parts, axis=0)


def _unstack_heads(o2):
    lo = _lane_lo((ATTN_BLOCK, 128))
    return [jnp.where(lo, o2[2 * p * ATTN_BLOCK:(2 * p + 1) * ATTN_BLOCK], o2[(2 * p + 1) * ATTN_BLOCK:(2 * p + 2) * ATTN_BLOCK])
            for p in range(Q_PER_KV // 2)]


def _dup_half(prev, cur, g):
    x = jnp.concatenate([prev, cur], axis=0).astype(F32)
    lo = _lane_lo(x.shape)
    r = pltpu.roll(x, HEAD_DIM, 1)
    return (jnp.where(lo, x, r) if g == 0 else jnp.where(lo, r, x)).astype(BF)


def _fold_halves(x):
    return x + pltpu.roll(x, HEAD_DIM, 1)


def _attn_bias():
    i = lax.broadcasted_iota(jnp.int32, (Q_PER_KV * ATTN_BLOCK, 2 * ATTN_BLOCK), 0) & (ATTN_BLOCK - 1)
    j = lax.broadcasted_iota(jnp.int32, (Q_PER_KV * ATTN_BLOCK, 2 * ATTN_BLOCK), 1)
    band = (j > i) & (j <= i + ATTN_BLOCK)
    return jnp.stack([jnp.where(band & (j >= ATTN_BLOCK), 0.0, -jnp.inf), jnp.where(band, 0.0, -jnp.inf)]).astype(F32)


def _both(x):
    return jnp.concatenate([x, x], axis=1)


def _row_sums(x_bf):
    return jnp.dot(x_bf, jnp.ones((x_bf.shape[1], 128), BF), preferred_element_type=F32)


def _attn_probs(qs, kb, sink, bias):
    s = lax.dot_general(qs, kb, (((1,), (1,)), ((), ())), preferred_element_type=F32) + bias
    m = jnp.maximum(jnp.broadcast_to(jnp.max(s, axis=-1, keepdims=True), sink.shape), sink)
    return jnp.exp(s - _both(m)), jnp.exp(sink - m)


def _attn_specs(S):
    nb = S // ATTN_BLOCK
    qs = pl.BlockSpec((ATTN_BLOCK, Q_END), lambda n: (n, 0))
    cur = pl.BlockSpec((ATTN_BLOCK, 128), lambda n: (n, 0))
    prev = pl.BlockSpec((ATTN_BLOCK, 128), lambda n: (jnp.maximum(n - 1, 0), 0))
    sink = pl.BlockSpec((N_KV_HEADS, Q_PER_KV * ATTN_BLOCK, 128), lambda n: (0, 0, 0))
    bias = pl.BlockSpec((None, Q_PER_KV * ATTN_BLOCK, 2 * ATTN_BLOCK), lambda n: (jnp.minimum(n, 1), 0, 0))
    return nb, qs, cur, prev, sink, bias


def _attn_fwd(q, k, v, sink_rows, bias, *, name):
    S = q.shape[0]
    nb, qs, cur, prev, sink, bs = _attn_specs(S)

    def body(q_ref, kp_ref, kc_ref, vp_ref, vc_ref, sk_ref, b_ref, o_ref):
        for g in range(N_KV_HEADS):
            kb = _dup_half(kp_ref[...], kc_ref[...], g)
            vb = _dup_half(vp_ref[...], vc_ref[...], g)
            p, es = _attn_probs(_stack_heads(q_ref[...], g), kb, sk_ref[g], b_ref[...])
            ones = jnp.ones((2 * ATTN_BLOCK, 128), BF)
            o3 = jnp.dot(p.astype(BF), jnp.concatenate([vb, ones], axis=1), preferred_element_type=F32)
            o2 = o3[:, :128] / (o3[:, 128:] + es)
            for t, tile in enumerate(_unstack_heads(o2)):
                o_ref[:, (g * 4 + t) * 128:(g * 4 + t + 1) * 128] = tile.astype(BF)
    return _pcall(body, name=name, grid=(nb,), in_specs=[qs, prev, cur, prev, cur, sink, bs], out_specs=qs,
                  out_shape=jax.ShapeDtypeStruct(q.shape, BF), compiler_params=_params())(q, k, k, v, v, sink_rows, bias)


def _attn_bwd(do, q, k, v, sink_rows, bias, *, name):
    S = q.shape[0]
    nb, qs, cur, prev, sink, bs = _attn_specs(S)
    full = pl.BlockSpec((S, 128), lambda n: (0, 0))
    dsk_spec = pl.BlockSpec((N_KV_HEADS, Q_PER_KV, 128), lambda n: (0, 0, 0))

    def body(do_ref, q_ref, kp_ref, kc_ref, vp_ref, vc_ref, sk_ref, b_ref, dq_ref, dk_ref, dv_ref, dsk_ref):
        n = pl.program_id(0)

        @pl.when(n == 0)
        def _():
            dk_ref[...] = jnp.zeros_like(dk_ref)
            dv_ref[...] = jnp.zeros_like(dv_ref)
            dsk_ref[...] = jnp.zeros_like(dsk_ref)
        sub = lax.broadcasted_iota(jnp.int32, (Q_PER_KV, 128), 0)
        dkf, dvf = [], []
        for g in range(N_KV_HEADS):
            qst = _stack_heads(q_ref[...], g)
            dos = _stack_heads(do_ref[...], g)
            kb = _dup_half(kp_ref[...], kc_ref[...], g)
            vb = _dup_half(vp_ref[...], vc_ref[...], g)
            pu, es = _attn_probs(qst, kb, sk_ref[g], b_ref[...])
            inv = 1.0 / (_row_sums(pu.astype(BF)) + es)
            p = pu * _both(inv)
            dp = lax.dot_general(dos, vb, (((1,), (1,)), ((), ())), preferred_element_type=F32)
            dd = _row_sums((p * dp).astype(BF))
            ds = (p * (dp - _both(dd))).astype(BF)
            dq2 = jnp.dot(ds, kb, preferred_element_type=F32) * (HEAD_DIM ** -0.5)
            for t, tile in enumerate(_unstack_heads(dq2)):
                dq_ref[:, (g * 4 + t) * 128:(g * 4 + t + 1) * 128] = tile.astype(BF)
            dkf.append(_fold_halves(lax.dot_general(ds, qst, (((0,), (0,)), ((), ())), preferred_element_type=F32)))
            dvf.append(_fold_halves(lax.dot_general(p.astype(BF), dos, (((0,), (0,)), ((), ())),
                                                    preferred_element_type=F32)))
            dsr = -(es * inv * dd)
            upd = jnp.zeros((Q_PER_KV, 128), F32)
            for h in range(Q_PER_KV):
                upd = jnp.where(sub == h, jnp.sum(dsr[h * ATTN_BLOCK:(h + 1) * ATTN_BLOCK], axis=0, keepdims=True), upd)
            dsk_ref[g] += upd
        lo = _lane_lo((2 * ATTN_BLOCK, 128))
        dkb = jnp.where(lo, dkf[0], dkf[1])
        dvb = jnp.where(lo, dvf[0], dvf[1])
        r0 = pl.multiple_of(n * ATTN_BLOCK, ATTN_BLOCK)
        dk_ref[pl.ds(r0, ATTN_BLOCK), :] += dkb[ATTN_BLOCK:]
        dv_ref[pl.ds(r0, ATTN_BLOCK), :] += dvb[ATTN_BLOCK:]

        @pl.when(n > 0)
        def _():
            rp = pl.multiple_of((n - 1) * ATTN_BLOCK, ATTN_BLOCK)
            dk_ref[pl.ds(rp, ATTN_BLOCK), :] += dkb[:ATTN_BLOCK]
            dv_ref[pl.ds(rp, ATTN_BLOCK), :] += dvb[:ATTN_BLOCK]
    return _pcall(body, name=name, grid=(nb,), in_specs=[qs, qs, prev, cur, prev, cur, sink, bs],
                  out_specs=[qs, full, full, dsk_spec],
                  out_shape=[jax.ShapeDtypeStruct(q.shape, BF), jax.ShapeDtypeStruct((S, 128), F32),
                             jax.ShapeDtypeStruct((S, 128), F32), jax.ShapeDtypeStruct((N_KV_HEADS, Q_PER_KV, 128), F32)],
                  compiler_params=_params())(do, q, k, k, v, v, sink_rows, bias)


def _ada_fwd(c_all, ada_w):
    ncol = ada_w.shape[2]

    def body(c_ref, w_ref, o_ref):
        cv = c_ref[...]
        ca = (cv * _sigmoid(cv)).astype(BF)
        for l in range(DEPTH):
            o_ref[:, l * ncol:(l + 1) * ncol] = jnp.dot(ca, w_ref[l].astype(BF), preferred_element_type=F32)
    return _pcall(body, name="ada_fwd", out_shape=jax.ShapeDtypeStruct((N_DEV, DEPTH * ncol), F32),
                  compiler_params=_params())(c_all, ada_w)


def _ada_bwd(c_all, dm):
    ncol = dm.shape[2]

    def body(c_ref, dm_ref, o_ref):
        cv = c_ref[...]
        ca = (cv * _sigmoid(cv)).astype(BF)
        for l in range(DEPTH):
            o_ref[l] = lax.dot_general(ca, dm_ref[l].astype(BF), (((0,), (0,)), ((), ())), preferred_element_type=F32)
    return _pcall(body, name="ada_bwd", out_shape=jax.ShapeDtypeStruct((DEPTH, D_MODEL, ncol), F32),
                  compiler_params=_params())(c_all, dm)


def _adamw(w, g, m, v, *, name):
    R, C = w.shape
    tr = R
    for t in range(8, 513, 8):
        if R % t == 0:
            tr = t
    c1 = 1.0 - ADAM_B1 ** ADAM_STEP
    c2 = 1.0 - ADAM_B2 ** ADAM_STEP

    def body(w_ref, g_ref, m_ref, v_ref, d_ref, mo_ref, vo_ref):
        gv = g_ref[...]
        mn = ADAM_B1 * m_ref[...] + (1.0 - ADAM_B1) * gv
        vn = ADAM_B2 * v_ref[...] + (1.0 - ADAM_B2) * (gv * gv)
        mo_ref[...] = mn
        vo_ref[...] = vn
        d_ref[...] = -ADAM_LR * ((mn * (1.0 / c1)) / (jnp.sqrt(vn * (1.0 / c2)) + ADAM_EPS) + ADAM_WD * w_ref[...])
    spec = pl.BlockSpec((tr, C), lambda i: (i, 0))
    sh = jax.ShapeDtypeStruct((R, C), F32)
    return _pcall(body, name=name, grid=(R // tr,), in_specs=[spec] * 4, out_specs=[spec] * 3, out_shape=[sh, sh, sh],
                  compiler_params=_params())(w, g, m, v)


def _adamw_layers(w, g_layers, m, v, *, name):
    L, R, C = w.shape
    assert L == 2 and len(g_layers) == 2
    tr = R
    for t in range(8, 513, 8):
        if R % t == 0:
            tr = t
    c1 = 1.0 - ADAM_B1 ** ADAM_STEP
    c2 = 1.0 - ADAM_B2 ** ADAM_STEP

    def body(w_ref, g0_ref, g1_ref, m_ref, v_ref, go_ref, d_ref, mo_ref, vo_ref):
        gv = jnp.where(pl.program_id(0) == 0, g0_ref[...], g1_ref[...])
        go_ref[...] = gv
        mn = ADAM_B1 * m_ref[...] + (1.0 - ADAM_B1) * gv
        vn = ADAM_B2 * v_ref[...] + (1.0 - ADAM_B2) * (gv * gv)
        mo_ref[...] = mn
        vo_ref[...] = vn
        d_ref[...] = -ADAM_LR * ((mn * (1.0 / c1)) / (jnp.sqrt(vn * (1.0 / c2)) + ADAM_EPS) + ADAM_WD * w_ref[...])
    spec = pl.BlockSpec((None, tr, C), lambda l, i: (l, i, 0))
    sh = jax.ShapeDtypeStruct((L, R, C), F32)
    g_specs = [pl.BlockSpec((tr, C), lambda l, i, k=k: (jnp.where(l == k, i, 0), 0)) for k in range(L)]
    return _pcall(body, name=name, grid=(L, R // tr), in_specs=[spec] + g_specs + [spec, spec], out_specs=[spec] * 4,
                  out_shape=[sh] * 4, compiler_params=_params())(w, *g_layers, m, v)


def _sum8(parts, *, name):
    _, R, C = parts.shape
    tr = _tile(R, 512, 16)

    def body(p_ref, o_ref):
        acc = p_ref[0].astype(F32)
        for k in range(1, N_DEV):
            acc = acc + p_ref[k].astype(F32)
        o_ref[...] = acc
    return _pcall(body, name=name, grid=(R // tr,), in_specs=[pl.BlockSpec((N_DEV, tr, C), lambda i: (0, i, 0))],
                  out_specs=pl.BlockSpec((tr, C), lambda i: (i, 0)), out_shape=jax.ShapeDtypeStruct((R, C), F32),
                  compiler_params=_params())(parts)


MESH_ID = pl.DeviceIdType.MESH
ANY = pl.BlockSpec(memory_space=pl.ANY)


def _all_gather(x, *, name, after=None):
    R, C = x.shape
    extra = [] if after is None else [after]

    def body(x_ref, *rest):
        out_ref, send_sems, recv_sems, local_sem = rest[-4:]
        mx, my, mc = lax.axis_index("x"), lax.axis_index("y"), lax.axis_index("c")
        me, sibling = (mx, my, mc), (mx, my, 1 - mc)
        chips = [(1 - mx, my), (mx, 1 - my), (1 - mx, 1 - my)]

        def blk(px, py, pc):
            return out_ref.at[4 * px + 2 * py + pc]

        def copy(k, block, to, src=None):
            return pltpu.make_async_remote_copy(
                src_ref=blk(*block) if src is None else src, dst_ref=blk(*block),
                send_sem=send_sems.at[k], recv_sem=recv_sems.at[k], device_id=to, device_id_type=MESH_ID)

        mine = pltpu.make_async_copy(x_ref, blk(*me), local_sem)
        mine.start()
        first = [copy(0, me, sibling, src=x_ref)]
        first += [copy(1 + j, me, (*chip, mc), src=x_ref) for j, chip in enumerate(chips)]
        for cp in first:
            cp.start()
        passed = [copy(4 + j, (*chip, mc), sibling) for j, chip in enumerate(chips)]
        for j, chip in enumerate(chips):
            copy(1 + j, (*chip, mc), me).wait_recv()
            passed[j].start()
        copy(0, sibling, me).wait_recv()
        for j, chip in enumerate(chips):
            copy(4 + j, (*chip, 1 - mc), me).wait_recv()
        for cp in first + passed:
            cp.wait_send()
        mine.wait()
    return _pcall(body, name=name, in_specs=[ANY] * (1 + len(extra)), out_specs=ANY,
                  out_shape=jax.ShapeDtypeStruct((N_DEV, R, C), x.dtype),
                  scratch_shapes=[pltpu.SemaphoreType.DMA((7,)), pltpu.SemaphoreType.DMA((7,)), pltpu.SemaphoreType.DMA],
                  compiler_params=pltpu.CompilerParams(has_side_effects=True))(x, *extra)


HBM_SPEC = pl.BlockSpec(memory_space=pltpu.HBM)
SEM_SPEC = pl.BlockSpec(memory_space=pltpu.SEMAPHORE)
DATAFLOW = pltpu.SideEffectType.DATAFLOW_SIDE_EFFECTING


def _coords():
    return lax.axis_index("x"), lax.axis_index("y"), lax.axis_index("c")


def _other_chips(mx, my):
    return [(1 - mx, my), (mx, 1 - my), (1 - mx, 1 - my)]


def _plan_gather_ici(refs, send, recv):
    k = len(refs) // 2
    mx, my, mc = _coords()
    return [pltpu.make_async_remote_copy(src_ref=refs[w], dst_ref=refs[k + w].at[2 * mx + my, mc], send_sem=send[3 * w + j],
                                         recv_sem=recv[3 * w + j], device_id=(px, py, mc), device_id_type=MESH_ID)
            for w in range(k) for j, (px, py) in enumerate(_other_chips(mx, my))]


N_CHIPS = 4


def _plan_gather_d2d(refs, send, recv):
    mx, my, mc = _coords()
    return [pltpu.make_async_remote_copy(src_ref=land.at[q, mc], dst_ref=land.at[q, mc], send_sem=send[N_CHIPS * w + q],
                                         recv_sem=recv[N_CHIPS * w + q], device_id=(mx, my, 1 - mc), device_id_type=MESH_ID)
            for w, land in enumerate(refs) for q in range(N_CHIPS)]


def _plan_reduce_d2d(refs, send, recv):
    g, land = refs
    mx, my, mc = _coords()
    return [pltpu.make_async_remote_copy(src_ref=g.at[q, 1 - mc], dst_ref=land.at[q], send_sem=send[q], recv_sem=recv[q],
                                         device_id=(mx, my, 1 - mc), device_id_type=MESH_ID) for q in range(N_CHIPS)]


def _plan_reduce_ici(refs, send, recv):
    h, land = refs
    mx, my, mc = _coords()
    return [pltpu.make_async_remote_copy(src_ref=h.at[2 * px + py], dst_ref=land.at[j], send_sem=send[j], recv_sem=recv[j],
                                         device_id=(px, py, mc), device_id_type=MESH_ID)
            for j, (px, py) in enumerate(_other_chips(mx, my))]


def _rdma_start(bufs, n, plan, *, name, after=None):
    nb = len(bufs)
    extra = [] if after is None else [after]
    ne = len(extra)

    def body(*refs):
        ins, send, recv = refs[:nb], refs[nb + ne:nb + ne + n], refs[nb + ne + n:nb + ne + 2 * n]
        token = refs[-1]
        for cp in plan(ins, send, recv):
            cp.start()
        token[...] = jnp.zeros_like(token)
    out = _pcall(body, name=name,
                 out_shape=tuple([pltpu.SemaphoreType.DMA(())] * (2 * n) + [pltpu.HBM(b.shape, b.dtype) for b in bufs]
                                 + [jax.ShapeDtypeStruct((8, 128), F32)]),
                 in_specs=tuple([HBM_SPEC] * nb + [ANY] * ne),
                 out_specs=tuple([SEM_SPEC] * (2 * n) + [HBM_SPEC] * nb + [pl.BlockSpec(memory_space=pltpu.VMEM)]),
                 input_output_aliases={i: 2 * n + i for i in range(nb)},
                 compiler_params=pltpu.CompilerParams(has_side_effects=DATAFLOW))(
                     *[pltpu.with_memory_space_constraint(b, pltpu.HBM) for b in bufs], *extra)
    return list(out[:2 * n]), list(out[2 * n:2 * n + nb]), out[-1]


def _rdma_wait(sems, bufs, n, plan, after, *, name):
    nb = len(bufs)

    def body(*refs):
        ins, send, recv = refs[:nb], refs[nb:nb + n], refs[nb + n:nb + 2 * n]
        for cp in plan(ins, send, recv):
            cp.wait_send()
            cp.wait_recv()
    out = _pcall(body, name=name, out_shape=tuple(pltpu.HBM(b.shape, b.dtype) for b in bufs),
                 in_specs=tuple([HBM_SPEC] * nb + [SEM_SPEC] * (2 * n) + [ANY]), out_specs=tuple([HBM_SPEC] * nb),
                 input_output_aliases={i: i for i in range(nb)},
                 compiler_params=pltpu.CompilerParams(has_side_effects=DATAFLOW))(*bufs, *sems, after)
    return list(out)


def _sum_pair(g, land, cidx, *, name):
    nchip, _, R, C = g.shape
    tr = _tile(R, 1056, 16)

    def body(c_ref, g_ref, l_ref, o_ref):
        o_ref[...] = g_ref[...] + l_ref[...]
    grid_spec = pltpu.PrefetchScalarGridSpec(
        num_scalar_prefetch=1, grid=(nchip, R // tr),
        in_specs=[pl.BlockSpec((None, None, tr, C), lambda p, i, c_ref: (p, c_ref[0], i, 0)),
                  pl.BlockSpec((None, tr, C), lambda p, i, c_ref: (p, i, 0))],
        out_specs=pl.BlockSpec((None, tr, C), lambda p, i, c_ref: (p, i, 0)))
    return _pcall(body, name=name, grid_spec=grid_spec, out_shape=jax.ShapeDtypeStruct((nchip, R, C), BF),
                  compiler_params=_params())(cidx, g, land)


def _sum_chips(h, land, chipidx, *, name):
    _, R, C = h.shape
    tr = _tile(R, 1056, 16)

    def body(c_ref, h_ref, l_ref, o_ref):
        acc = h_ref[...].astype(F32)
        for j in range(3):
            acc = acc + l_ref[j].astype(F32)
        o_ref[...] = acc
    grid_spec = pltpu.PrefetchScalarGridSpec(
        num_scalar_prefetch=1, grid=(R // tr,),
        in_specs=[pl.BlockSpec((None, tr, C), lambda i, c_ref: (c_ref[0], i, 0)),
                  pl.BlockSpec((3, tr, C), lambda i, c_ref: (0, i, 0))],
        out_specs=pl.BlockSpec((tr, C), lambda i, c_ref: (i, 0)))
    return _pcall(body, name=name, grid_spec=grid_spec, out_shape=jax.ShapeDtypeStruct((R, C), F32),
                  compiler_params=_params())(chipidx, h, land)


PART_IN = ("w_in",)
PART_MIX = ("proj_a", "proj_b", "w_out")
PART_FFN = ("ffn_w_gate", "ffn_w_up", "ffn_w_down")


def _part_rows(names):
    return sum(BIG_ROWS[n] for n in names)


def _part_offsets(names):
    off, r = {}, 0
    for n in names:
        off[n] = r
        r += BIG_ROWS[n]
    return off


def _shard_rows(shards, l, names):
    return [(shards[n][l].T if n in COL_SHARDED else shards[n][l]).astype(BF) for n in names]


def _pack_shards(shards, l, names):
    return jnp.concatenate(_shard_rows(shards, l, names), axis=0)


def _unpack_weights(full8, names):
    off = _part_offsets(names)

    def whole(n):
        if isinstance(full8, (list, tuple)):
            return full8[names.index(n)].reshape(N_DEV * BIG_ROWS[n], 1024)
        return full8[:, off[n]:off[n] + BIG_ROWS[n], :].reshape(N_DEV * BIG_ROWS[n], 1024)
    out = {}
    if "w_in" in names:
        wt_in = whole("w_in")
        out["wt_in"] = jnp.concatenate([wt_in[V_END:], wt_in[:V_END]], axis=0)
    for n in ("proj_a", "proj_b", "w_out"):
        if n in names:
            out[n] = whole(n)
    if "ffn_w_gate" in names:
        out["wt_gate"], out["wt_up"], out["w_down"] = whole("ffn_w_gate"), whole("ffn_w_up"), whole("ffn_w_down")
    return out


def _from_land(land):
    return land.reshape(N_DEV, land.shape[2], 1024)


def _pack_grads(wg, names):
    full = {"proj_a": wg.get("proj_a"), "proj_b": wg.get("proj_b"), "w_out": wg.get("w_out"), "ffn_w_down": wg.get("w_down"),
            "ffn_w_gate": wg.get("wt_gate"), "ffn_w_up": wg.get("wt_up")}
    if "w_in" in names:
        full["w_in"] = jnp.concatenate([wg["wt_in"][P_Q:], wg["wt_in"][:P_Q]], axis=0)
    blocks = jnp.concatenate([full[n].reshape(N_DEV, BIG_ROWS[n], 1024) for n in names], axis=1)
    return blocks.reshape(N_CHIPS, 2, _part_rows(names), 1024)


def _unpack_shard_grads(gs, names):
    off = _part_offsets(names)
    return {n: gs[off[n]:off[n] + BIG_ROWS[n]] for n in names}


def _rope_setup(positions):
    S = positions.shape[0]
    inv = ROPE_THETA ** (-jnp.arange(0, ROT_DIM, 2, dtype=F32) / ROT_DIM)
    lane = np.arange(128) % HEAD_DIM
    half = ROT_DIM // 2
    inv_row = jnp.where(lane < ROT_DIM, jnp.tile(inv, 128 // half), 0.0)[None, :].astype(F32)
    m1_row = jnp.asarray((lane < half).astype(np.float32))[None, :]
    m2_row = jnp.asarray(((lane >= half) & (lane < ROT_DIM)).astype(np.float32))[None, :]
    return (*_rope_tables(positions.astype(F32).reshape(S, 1), inv_row, m1_row, m2_row), _attn_bias())


def _hook(hooks, point, after):
    f = None if hooks is None else hooks.get(point)
    return None if f is None else f(after)


def _layer_fwd(l, x, mod_l, W, small, rope, hooks=None):
    rc, rs1, rs2, bias = rope
    sh1, sc1, g1, sh2, sc2, g2 = [mod_l[i * D_MODEL:(i + 1) * D_MODEL][None, :] for i in range(6)]
    nw1, nw2 = small["norm1_w"][l][None, :], small["norm2_w"][l][None, :]
    tok = _hook(hooks, "mm_in", x)
    h, (proj,) = _norm_mm(x, nw1, sc1, sh1, [W["wt_in"]], name=f"mm_in{l}", after=tok, tm=2048, tn_cap=768)
    q_r, k_r, v_b = _rope_fwd(proj, rc, rs1, rs2, name=f"rope_fwd{l}")
    sink_rows = jnp.repeat(small["attn_sinks"][l].reshape(N_KV_HEADS, Q_PER_KV), ATTN_BLOCK, axis=1)
    sink_rows = jnp.broadcast_to(sink_rows[..., None], sink_rows.shape + (128,))
    y_attn = _attn_fwd(q_r, k_r, v_b, sink_rows, bias, name=f"attn_fwd{l}")
    lnw, lnb = small["sgu_ln_w"][l][None, :], small["sgu_ln_b"][l][None, :]
    sgu_bt = small["sgu_b"][l].T
    y_sgu = _sgu_fwd(proj, lnw, lnb, small["sgu_w"][l], sgu_bt, name=f"sgu_fwd{l}", after=_hook(hooks, "sgu", y_attn))
    tok = _hook(hooks, "mm_pa", y_sgu)
    a_br, b_br, merged = _merge_fwd(y_sgu, y_attn, W["proj_a"], W["proj_b"], proj, name=f"merge_fwd{l}", after=tok)
    x1, o1 = _mm(merged, W["w_out"], nt=False, out_dtype=F32, name=f"mm_out{l}", res=x, gvec=g1, tm=512)
    tok = _hook(hooks, "mm_gu", x1)
    h2, (a_g, a_u) = _norm_mm(x1, nw2, sc2, sh2, [W["wt_gate"], W["wt_up"]], name=f"mm_gu{l}", after=tok, tn_cap=1408)
    cw, cb = small["ffn_conv_w"][l], small["ffn_conv_b"][l][None, :]
    hf, a_c = _ffn_act_fwd(a_g, a_u, cw, cb, name=f"ffn_act_fwd{l}")
    x2, o2 = _mm(hf, W["w_down"], nt=False, out_dtype=F32, name=f"mm_down{l}", res=x1, gvec=g2)
    saved = dict(x=x, h=h, proj=proj, q_r=q_r, k_r=k_r, v_b=v_b, sink_rows=sink_rows, y_attn=y_attn, y_sgu=y_sgu,
                 a_br=a_br, b_br=b_br, merged=merged, x1=x1, o1=o1, h2=h2, a_g=a_g, a_u=a_u, a_c=a_c, hf=hf, o2=o2)
    return x2, saved


def _layer_bwd(l, dx, do2, dg2, mod_l, W, small, rope, sv, below=None, hooks=None, wg=None):
    rc, rs1, rs2, bias = rope
    sh1, sc1, g1, sh2, sc2, g2 = [mod_l[i * D_MODEL:(i + 1) * D_MODEL][None, :] for i in range(6)]
    nw1, nw2 = small["norm1_w"][l][None, :], small["norm2_w"][l][None, :]
    cw = small["ffn_conv_w"][l]
    lnw, lnb = small["sgu_ln_w"][l][None, :], small["sgu_ln_b"][l][None, :]
    sgu_bt = small["sgu_b"][l].T
    wg = {} if wg is None else wg
    dhf = _mm(do2, W["w_down"], nt=True, out_dtype=BF, name=f"mm_down_dx{l}", after=_hook(hooks, "mm_down_dx", do2),
              tn_cap=1408)
    wg["w_down"] = _mm_tn(sv["hf"], do2, name=f"mm_down_dw{l}")
    dac, dup, dcb = _ffn_act_bwd_a(dhf, sv["a_c"], sv["a_u"], name=f"ffn_act_bwd_a{l}")
    da, dcw = _ffn_act_bwd_b(dac, sv["a_g"], cw, name=f"ffn_act_bwd_b{l}")
    dh2 = _mm([da, dup], [W["wt_gate"], W["wt_up"]], nt=False, out_dtype=F32, name=f"mm_gu_dx{l}",
              after=_hook(hooks, "mm_gu_dx", da))
    wg["wt_gate"] = _mm_tn(da, sv["h2"], name=f"mm_gate_dw{l}")
    wg["wt_up"] = _mm_tn(dup, sv["h2"], name=f"mm_up_dw{l}")
    dx1, dnw2, dsc2, dsh2, do1, dg1 = _normmod_bwd(dh2, sv["x1"], nw2, sc2, sh2, dx, (sv["o1"], g1), name=f"normmod2_bwd{l}")
    d_a, d_b, dproj = _merge_bwd(do1, W["w_out"], sv["a_br"], sv["b_br"], sv["proj"], name=f"merge_bwd{l}",
                                 after=_hook(hooks, "merge_bwd", do1))
    wg["w_out"] = _mm_tn(sv["merged"], do1, name=f"mm_out_dw{l}")
    dysgu = _mm(d_a, W["proj_a"], nt=True, out_dtype=F32, name=f"mm_pa_dx{l}", after=_hook(hooks, "mm_pa_dx", d_a))
    dyattn = _mm(d_b, W["proj_b"], nt=True, out_dtype=BF, name=f"mm_pb_dx{l}")
    wg["proj_a"] = _mm_tn(sv["y_sgu"], d_a, name=f"mm_pa_dw{l}")
    wg["proj_b"] = _mm_tn(sv["y_attn"], d_b, name=f"mm_pb_dw{l}")
    dproj, dlnw, dlnb, dsguw, dsgubt = _sgu_bwd(dysgu, sv["proj"], lnw, lnb, small["sgu_w"][l], sgu_bt, dproj,
                                                name=f"sgu_bwd{l}")
    dq_r, dk_r, dv_b, dsk = _attn_bwd(dyattn, sv["q_r"], sv["k_r"], sv["v_b"], sv["sink_rows"], bias, name=f"attn_bwd{l}")
    dproj = _rope_bwd(dq_r, dk_r, dv_b, rc, rs1, rs2, dproj, name=f"rope_bwd{l}")
    wg["wt_in"] = _mm_tn(dproj, sv["h"], name=f"mm_in_dw{l}")
    dh = _mm(dproj, W["wt_in"], nt=False, out_dtype=F32, name=f"mm_in_dx{l}", after=_hook(hooks, "mm_in_dx", wg["wt_in"]))
    dx0, dnw1, dsc1, dsh1, *gate_below = _normmod_bwd(dh, sv["x"], nw1, sc1, sh1, dx1, below, name=f"normmod1_bwd{l}")
    dmod = jnp.concatenate([dsh1, dsc1, dg1, dsh2, dsc2, dg2], axis=1)[0]
    sg = {"norm1_w": dnw1[0], "norm2_w": dnw2[0], "attn_sinks": dsk[:, :, 0].reshape(N_Q_HEADS),
          "sgu_ln_w": dlnw[0], "sgu_ln_b": dlnb[0], "sgu_w": dsguw, "sgu_b": dsgubt.T,
          "ffn_conv_w": dcw, "ffn_conv_b": dcb[0]}
    return (dx0, *gate_below), wg, sg, dmod


SMALL = ("ada_b", "norm1_w", "attn_sinks", "sgu_ln_w", "sgu_ln_b", "sgu_w", "sgu_b", "norm2_w", "ffn_conv_b", "final_norm_w")
WEIGHT_ORDER = ("ada_w", "ada_b", "norm1_w", "w_in", "attn_sinks", "sgu_ln_w", "sgu_ln_b", "sgu_w", "sgu_b", "proj_a", "proj_b",
                "w_out", "norm2_w", "ffn_w_gate", "ffn_w_up", "ffn_conv_w", "ffn_conv_b", "ffn_w_down", "final_norm_w")


def _flat_pack(arrs, rows):
    flat = jnp.concatenate([a.reshape(-1) for a in arrs])
    return jnp.pad(flat, (0, rows * 1024 - flat.shape[0])).reshape(rows, 1024)


def _flat_unpack(buf, shapes):
    flat = buf.reshape(-1)
    out, o = [], 0
    for s in shapes:
        n = int(np.prod(s))
        out.append(flat[o:o + n].reshape(s))
        o += n
    return out


def _adam2d(w, g, m, v, *, name):
    shp = w.shape
    r2 = (int(np.prod(shp[:-1])), shp[-1]) if len(shp) > 1 else (1, shp[0])
    d, mn, vn = _adamw(w.reshape(r2), g.reshape(r2), m.reshape(r2), v.reshape(r2), name=name)
    return d.reshape(shp), mn.reshape(shp), vn.reshape(shp)


def kernel(x, c, positions, ada_w, ada_b, norm1_w, w_in, attn_sinks, sgu_ln_w, sgu_ln_b, sgu_w, sgu_b, proj_a, proj_b, w_out, norm2_w, ffn_w_gate, ffn_w_up, ffn_conv_w, ffn_conv_b, ffn_w_down, final_norm_w, loss_target, m_ada_w, m_ada_b, m_norm1_w, m_w_in, m_attn_sinks, m_sgu_ln_w, m_sgu_ln_b, m_sgu_w, m_sgu_b, m_proj_a, m_proj_b, m_w_out, m_norm2_w, m_ffn_w_gate, m_ffn_w_up, m_ffn_conv_w, m_ffn_conv_b, m_ffn_w_down, m_final_norm_w, v_ada_w, v_ada_b, v_norm1_w, v_w_in, v_attn_sinks, v_sgu_ln_w, v_sgu_ln_b, v_sgu_w, v_sgu_b, v_proj_a, v_proj_b, v_w_out, v_norm2_w, v_ffn_w_gate, v_ffn_w_up, v_ffn_conv_w, v_ffn_conv_b, v_ffn_w_down, v_final_norm_w):
    wts = dict(ada_w=ada_w, ada_b=ada_b, norm1_w=norm1_w, w_in=w_in, attn_sinks=attn_sinks, sgu_ln_w=sgu_ln_w,
               sgu_ln_b=sgu_ln_b, sgu_w=sgu_w, sgu_b=sgu_b, proj_a=proj_a, proj_b=proj_b, w_out=w_out, norm2_w=norm2_w,
               ffn_w_gate=ffn_w_gate, ffn_w_up=ffn_w_up, ffn_conv_w=ffn_conv_w, ffn_conv_b=ffn_conv_b,
               ffn_w_down=ffn_w_down, final_norm_w=final_norm_w)
    mom = dict(ada_w=m_ada_w, ada_b=m_ada_b, norm1_w=m_norm1_w, w_in=m_w_in, attn_sinks=m_attn_sinks, sgu_ln_w=m_sgu_ln_w,
               sgu_ln_b=m_sgu_ln_b, sgu_w=m_sgu_w, sgu_b=m_sgu_b, proj_a=m_proj_a, proj_b=m_proj_b, w_out=m_w_out,
               norm2_w=m_norm2_w, ffn_w_gate=m_ffn_w_gate, ffn_w_up=m_ffn_w_up, ffn_conv_w=m_ffn_conv_w,
               ffn_conv_b=m_ffn_conv_b, ffn_w_down=m_ffn_w_down, final_norm_w=m_final_norm_w)
    var = dict(ada_w=v_ada_w, ada_b=v_ada_b, norm1_w=v_norm1_w, w_in=v_w_in, attn_sinks=v_attn_sinks, sgu_ln_w=v_sgu_ln_w,
               sgu_ln_b=v_sgu_ln_b, sgu_w=v_sgu_w, sgu_b=v_sgu_b, proj_a=v_proj_a, proj_b=v_proj_b, w_out=v_w_out,
               norm2_w=v_norm2_w, ffn_w_gate=v_ffn_w_gate, ffn_w_up=v_ffn_w_up, ffn_conv_w=v_ffn_conv_w,
               ffn_conv_b=v_ffn_conv_b, ffn_w_down=v_ffn_w_down, final_norm_w=v_final_norm_w)
    me = 4 * lax.axis_index("x") + 2 * lax.axis_index("y") + lax.axis_index("c")
    ada_cols = ada_w.shape[2]

    conv_cols = ffn_conv_w.shape[2]
    first = _all_gather(jnp.concatenate([c, _flat_pack([ffn_conv_w], 7)], axis=0), name="ag_c")
    c_all = first[:, 0, :]
    conv_full = jnp.stack([first[j, 1:].reshape(-1)[:DEPTH * 3 * conv_cols].reshape(DEPTH, 3, conv_cols)
                           for j in range(N_DEV)], axis=2).reshape(DEPTH, 3, FFN_DIM)
    prod = _ada_fwd(c_all, ada_w)
    prod_all = _all_gather(prod, name="ag_mod")
    mine = lax.dynamic_index_in_dim(prod_all, me, axis=1, keepdims=False)
    mod = jnp.stack([mine[:, l * ada_cols:(l + 1) * ada_cols].reshape(-1) for l in range(DEPTH)]) + ada_b
    small = {n: wts[n] for n in SMALL}
    small["ffn_conv_w"] = conv_full

    mx, my, mc = _coords()
    cidx = jnp.reshape(mc, (1,)).astype(jnp.int32)
    chipidx = jnp.reshape(2 * mx + my, (1,)).astype(jnp.int32)
    rope = _rope_setup(positions[0])

    class Gather:
        def __init__(self, srcs, tag):
            self.tag, self.k, self.srcs = tag, len(srcs), list(srcs)
            self.lands = [lax.dynamic_update_slice(lax.empty((N_CHIPS, 2) + s.shape, s.dtype), s[None, None],
                                                   (2 * mx + my, mc, 0, 0)) for s in srcs]

        def ici_start(self, after):
            self.sems, bufs, tok = _rdma_start(self.srcs + self.lands, 3 * self.k, _plan_gather_ici,
                                               name=f"ag_{self.tag}_ici_start", after=after)
            self.srcs, self.lands = bufs[:self.k], bufs[self.k:]
            return tok

        def ici_wait_d2d_start(self, after):
            bufs = _rdma_wait(self.sems, self.srcs + self.lands, 3 * self.k, _plan_gather_ici, after,
                              name=f"ag_{self.tag}_ici_wait")
            self.sems, self.lands, tok = _rdma_start(bufs[self.k:], N_CHIPS * self.k, _plan_gather_d2d,
                                                     name=f"ag_{self.tag}_d2d_start")
            return tok

        def d2d_wait(self, after):
            lands = _rdma_wait(self.sems, self.lands, N_CHIPS * self.k, _plan_gather_d2d, after, name=f"ag_{self.tag}_d2d_wait")
            return [_from_land(land) for land in lands]

    def weights_job(names, l, tag):
        job = Gather(_shard_rows(wts, l, names), tag)
        job.weights = lambda after: _unpack_weights(job.d2d_wait(after), names)
        return job

    W0 = _unpack_weights(_all_gather(_pack_shards(wts, 0, PART_IN), name="ag_w0_in", after=mod), PART_IN)
    W1 = {}
    rest = PART_MIX + PART_FFN
    g_rest0 = weights_job(rest, 0, "w0_rest")
    g_in1, g_rest1 = weights_job(PART_IN, 1, "w1_in"), weights_job(rest, 1, "w1_rest")

    def rest0_then_layer1(after):
        W0.update(g_rest0.weights(after))
        return g_rest1.ici_start(g_in1.ici_start(W0["w_down"]))

    x1, sv0 = _layer_fwd(0, x[0], mod[0], W0, small, rope,
                         {"mm_in": lambda after: g_rest0.ici_start(W0["wt_in"]), "sgu": g_rest0.ici_wait_d2d_start,
                          "mm_pa": rest0_then_layer1, "mm_gu": g_in1.ici_wait_d2d_start})
    g_rest1.ici_wait_d2d_start(x1)
    x2, sv1 = _layer_fwd(1, x1, mod[1], W1, small, rope,
                         {"mm_in": lambda after: W1.update(g_in1.weights(after)),
                          "mm_pa": lambda after: W1.update(g_rest1.weights(after))})
    gate2 = [mod[l][5 * D_MODEL:][None, :] for l in range(DEPTH)]
    dx2, dfw, loss_tile, do2, dg2 = _head(x2, final_norm_w[None, :], loss_target[0], (sv1["o2"], gate2[1]))
    loss = lax.psum(loss_tile[0, 0], ("x", "y", "c"))

    class Reduce:
        def __init__(self, names, tag):
            self.names, self.tag, self.rows = names, tag, _part_rows(names)

        def d2d_start(self, wg, after=None):
            self.sems, self.bufs, tok = _rdma_start([_pack_grads(wg, self.names), lax.empty((N_CHIPS, self.rows, 1024), BF)],
                                                    N_CHIPS, _plan_reduce_d2d, name=f"rs_{self.tag}_d2d_start", after=after)
            return tok

        def d2d_wait_ici_start(self, after):
            g_t, land_a = _rdma_wait(self.sems, self.bufs, N_CHIPS, _plan_reduce_d2d, after, name=f"rs_{self.tag}_d2d_wait")
            h = _sum_pair(g_t, land_a, cidx, name=f"rs_{self.tag}_sum_pair")
            self.sems, self.bufs, tok = _rdma_start([h, lax.empty((3, self.rows, 1024), BF)], 3, _plan_reduce_ici,
                                                    name=f"rs_{self.tag}_ici_start")
            return tok

        def ici_wait(self, after):
            h_t, land_b = _rdma_wait(self.sems, self.bufs, 3, _plan_reduce_ici, after, name=f"rs_{self.tag}_ici_wait")
            return _unpack_shard_grads(_sum_chips(h_t, land_b, chipidx, name=f"rs_{self.tag}_sum_chips"), self.names)

    (dx1, do2, dg2), wg1, sg1, dmod1 = _layer_bwd(1, dx2, do2, dg2, mod[1], W1, small, rope, sv1, below=(sv0["o2"], gate2[0]))
    r_all1, r_ffn0, r_mix0 = Reduce(BIG, "g1"), Reduce(PART_FFN, "g0_ffn"), Reduce(PART_IN + PART_MIX, "g0_mix")
    tok1 = r_all1.d2d_start(wg1)
    wg0, shard1 = {}, {}

    def layer1_done_then_mix0(after):
        shard1.update(r_all1.ici_wait(after))
        return r_mix0.d2d_wait_ici_start(r_mix0.d2d_start(wg0, shard1["w_in"]))

    (grad_x,), _, sg0, dmod0 = _layer_bwd(
        0, dx1, do2, dg2, mod[0], W0, small, rope, sv0, wg=wg0,
        hooks={"mm_down_dx": lambda after: tok1, "mm_gu_dx": r_all1.d2d_wait_ici_start,
               "merge_bwd": lambda after: r_ffn0.d2d_start(wg0, after), "mm_pa_dx": r_ffn0.d2d_wait_ici_start,
               "mm_in_dx": layer1_done_then_mix0})
    sg = {n: jnp.stack([sg0[n], sg1[n]]) for n in sg0}
    sg["final_norm_w"] = dfw[0]
    dmod = jnp.stack([dmod0, dmod1])
    vec_names = [n for n in SMALL if n not in ("ada_b", "sgu_w")] + ["ffn_conv_w"]
    vec_shapes = [(DEPTH, 6 * D_MODEL)] + [sg[n].shape for n in vec_names]
    vec_rows = -(-sum(int(np.prod(s)) for s in vec_shapes) // 1024 // 16) * 16
    sgu_rows = sgu_w.size // 1024
    g_small = Gather([jnp.concatenate([_flat_pack([dmod] + [sg[n] for n in vec_names], vec_rows),
                                       sg["sgu_w"].reshape(sgu_rows, 1024)], axis=0).astype(BF)], "small")
    tok = g_small.ici_start(grad_x)

    shard0 = r_ffn0.ici_wait(tok)
    shard0.update(r_mix0.ici_wait(shard0["ffn_w_down"]))
    grads, delta, new_m, new_v = {}, {}, {}, {}
    for n in BIG:
        view = (lambda a: jnp.swapaxes(a, 1, 2)) if n in COL_SHARDED else (lambda a: a)
        out = _adamw_layers(view(wts[n]), [shard0[n], shard1[n]], view(mom[n]), view(var[n]), name=f"adamw_{n}")
        grads[n], delta[n], new_m[n], new_v[n] = [view(o) for o in out]

    (sm_all,) = g_small.d2d_wait(g_small.ici_wait_d2d_start(delta["ffn_w_gate"]))
    sm_sum = _sum8(sm_all, name="sum_small")
    vec_sum = _flat_unpack(sm_sum[:vec_rows], vec_shapes)
    grads["ada_b"] = vec_sum[0]
    for n, gsum in zip(vec_names, vec_sum[1:]):
        grads[n] = gsum
    grads["sgu_w"] = sm_sum[vec_rows:].reshape(sgu_w.shape)
    grads["ffn_conv_w"] = lax.dynamic_slice_in_dim(grads["ffn_conv_w"], me * conv_cols, conv_cols, axis=2)
    dmod_all = sm_all[:, :DEPTH * 6, :].astype(F32).reshape(N_DEV, DEPTH, 6 * D_MODEL)
    dm_mine = lax.dynamic_slice_in_dim(dmod_all, me * ada_cols, ada_cols, axis=2).transpose(1, 0, 2)
    dm_mine = jnp.pad(dm_mine, ((0, 0), (0, 8), (0, 0)))
    grads["ada_w"] = _ada_bwd(jnp.pad(c_all, ((0, 8), (0, 0))), dm_mine)

    for n in WEIGHT_ORDER:
        if n not in delta:
            delta[n], new_m[n], new_v[n] = _adam2d(wts[n], grads[n], mom[n], var[n], name=f"adamw_{n}")
    return (loss, grad_x[None], *[grads[n] for n in WEIGHT_ORDER], *[delta[n] for n in WEIGHT_ORDER],
            *[new_m[n] for n in WEIGHT_ORDER], *[new_v[n] for n in WEIGHT_ORDER])
```

```python
import jax
import jax.numpy as jnp
import numpy as np
from jax import lax
from jax.experimental import pallas as pl
from jax.experimental.pallas import tpu as pltpu

F32 = jnp.float32
BF = jnp.bfloat16

N_DEV = 8
D_MODEL = 1024
DEPTH = 2
N_Q_HEADS = 16
N_KV_HEADS = 2
HEAD_DIM = 64
Q_PER_KV = N_Q_HEADS // N_KV_HEADS
ATTN_BLOCK = 128
ROPE_THETA = 500000.0
ROT_DIM = HEAD_DIM // 4
SGU_WIDTH = 1024
SGU_GROUPS = 8
SGU_CHUNK = 128
FFN_DIM = 2816
NORM_EPS = 1e-6
Q_END = N_Q_HEADS * HEAD_DIM
K_END = Q_END + N_KV_HEADS * HEAD_DIM
V_END = K_END + N_KV_HEADS * HEAD_DIM
Z_END = V_END + 2 * SGU_WIDTH
IN_COLS = Z_END + 2 * D_MODEL
P_Z, P_G, P_Q, P_K, P_V = 0, 2048, 4096, 5120, 5248

ADAM_LR = 0.001
ADAM_B1 = 0.9
ADAM_B2 = 0.999
ADAM_EPS = 1e-08
ADAM_WD = 0.01
ADAM_STEP = 10

VMEM_LIMIT_BYTES = 56 * 1024 * 1024

BIG = ("w_in", "proj_a", "proj_b", "w_out", "ffn_w_gate", "ffn_w_up", "ffn_w_down")
COL_SHARDED = ("w_in", "ffn_w_gate", "ffn_w_up")
BIG_SHAPE = {"w_in": (D_MODEL, IN_COLS), "proj_a": (SGU_WIDTH, D_MODEL), "proj_b": (Q_END, D_MODEL),
             "w_out": (D_MODEL, D_MODEL), "ffn_w_gate": (D_MODEL, FFN_DIM), "ffn_w_up": (D_MODEL, FFN_DIM),
             "ffn_w_down": (FFN_DIM, D_MODEL)}
BIG_ROWS = {n: BIG_SHAPE[n][0] * BIG_SHAPE[n][1] // N_DEV // 1024 for n in BIG}


def _pcall(body, **kw):
    return pl.pallas_call(body, **kw)


def _params(**kw):
    return pltpu.CompilerParams(vmem_limit_bytes=VMEM_LIMIT_BYTES, **kw)


def _tile(n, cap, unit=128):
    if n <= cap:
        return n
    best = 0
    t = unit
    while t <= cap:
        if n % t == 0:
            best = t
        t += unit
    assert best, (n, cap, unit)
    return best


def _mm(a, b, *, nt, out_dtype, name, res=None, gvec=None, after=None, tm=None, tn_cap=1024):
    a_list = list(a) if isinstance(a, (list, tuple)) else [a]
    b_list = list(b) if isinstance(b, (list, tuple)) else [b]
    a, b = a_list[0], b_list[0]
    M, K = a.shape
    N = b.shape[0] if nt else b.shape[1]
    k_total = sum(x.shape[1] for x in a_list)
    tm = _tile(M, tm or (1024 if k_total <= 1024 else 512), 8)
    tn = _tile(N, tn_cap)
    dn = (((1,), (1,)), ((), ())) if nt else (((1,), (0,)), ((), ()))

    def b_spec_of(x):
        k = x.shape[1] if nt else x.shape[0]
        return pl.BlockSpec((tn, k), lambda i, j: (j, 0)) if nt else pl.BlockSpec((k, tn), lambda i, j: (0, j))
    b_spec = b_spec_of(b)
    o_spec = pl.BlockSpec((tm, tn), lambda i, j: (i, j))
    if res is None:
        extra = [] if after is None else [after]
        n = len(a_list)

        def body(*refs):
            o_ref = refs[-1]
            acc = None
            for a_ref, b_ref in zip(refs[:n], refs[n:2 * n]):
                d = lax.dot_general(a_ref[...].astype(BF), b_ref[...].astype(BF), dn, preferred_element_type=F32)
                acc = d if acc is None else acc + d
            o_ref[...] = acc.astype(out_dtype)
        return _pcall(body, name=name, grid=(M // tm, N // tn),
                      in_specs=[pl.BlockSpec((tm, x.shape[1]), lambda i, j: (i, 0)) for x in a_list]
                      + [b_spec_of(x) for x in b_list] + [ANY] * len(extra), out_specs=o_spec,
                      out_shape=jax.ShapeDtypeStruct((M, N), out_dtype), compiler_params=_params())(
                          *a_list, *b_list, *extra)

    def body_res(a_ref, b_ref, r_ref, g_ref, o_ref, acc_ref):
        acc = lax.dot_general(a_ref[...].astype(BF), b_ref[...].astype(BF), dn, preferred_element_type=F32)
        acc_ref[...] = acc.astype(BF)
        o_ref[...] = r_ref[...] + g_ref[...] * acc
    return _pcall(body_res, name=name, grid=(M // tm, N // tn),
                  in_specs=[pl.BlockSpec((tm, K), lambda i, j: (i, 0)), b_spec, o_spec,
                            pl.BlockSpec((1, tn), lambda i, j: (0, j))],
                  out_specs=[o_spec, o_spec],
                  out_shape=[jax.ShapeDtypeStruct((M, N), F32), jax.ShapeDtypeStruct((M, N), BF)],
                  compiler_params=_params())(a, b, res, gvec)


def _mm_tn(a, b, *, name, out_dtype=BF, tk=2048, tm_cap=1408, tn_cap=1024):
    S, M = a.shape
    N = b.shape[1]
    tm = _tile(M, tm_cap)
    tn = _tile(N, tn_cap)
    if 2 * 2 * S * (tm + tn) <= VMEM_LIMIT_BYTES * 3 // 5:
        tk = S
    tk = _tile(S, tk, 8)
    nk = S // tk

    def body(a_ref, b_ref, o_ref, acc_ref):
        k = pl.program_id(2)

        @pl.when(k == 0)
        def _():
            acc_ref[...] = jnp.zeros_like(acc_ref)
        acc_ref[...] += lax.dot_general(a_ref[...].astype(BF), b_ref[...].astype(BF), (((0,), (0,)), ((), ())),
                                        preferred_element_type=F32)

        @pl.when(k == nk - 1)
        def _():
            o_ref[...] = acc_ref[...].astype(out_dtype)
    return _pcall(body, name=name, grid=(M // tm, N // tn, nk),
                  in_specs=[pl.BlockSpec((tk, tm), lambda i, j, k: (k, i)),
                            pl.BlockSpec((tk, tn), lambda i, j, k: (k, j))],
                  out_specs=pl.BlockSpec((tm, tn), lambda i, j, k: (i, j)),
                  out_shape=jax.ShapeDtypeStruct((M, N), out_dtype), scratch_shapes=[pltpu.VMEM((tm, tn), F32)],
                  compiler_params=_params())(a, b)


def _rms(x, w):
    return x * lax.rsqrt(jnp.mean(x * x, axis=-1, keepdims=True) + NORM_EPS) * w


def _normmod_fn(x, nw, sc, sh):
    return _rms(x, nw) * (1.0 + sc) + sh


def _gelu(x):
    return 0.5 * x * (1.0 + lax.erf(x * (2.0 ** -0.5)))


def _ln_gelu_fn(zv, w, b):
    v = _gelu(zv)
    mu = jnp.mean(v, axis=-1, keepdims=True)
    var = jnp.mean(jnp.square(v - mu), axis=-1, keepdims=True)
    return (v - mu) * lax.rsqrt(var + NORM_EPS) * w + b


def _sigmoid(x):
    return 1.0 / (1.0 + jnp.exp(-x))


def _row_spec(tm, n):
    return pl.BlockSpec((tm, n), lambda i: (i, 0))


def _vec_spec(n):
    return pl.BlockSpec((1, n), lambda i: (0, 0))


def _acc(ref, val):
    @pl.when(pl.program_id(0) == 0)
    def _():
        ref[...] = jnp.zeros_like(ref)
    ref[...] += val


def _norm_mm(x, nw, sc, sh, ws, *, name, after=None, tm=1024, tn_cap=768):
    S, K = x.shape
    N = ws[0].shape[0]
    tm = _tile(S, tm, 8)
    tn = _tile(N, tn_cap)
    nw_, ne = len(ws), 0 if after is None else 1

    def body(x_ref, nw_ref, sc_ref, sh_ref, *rest):
        w_refs = rest[:nw_]
        h_ref = rest[nw_ + ne]
        o_refs = rest[nw_ + ne + 1:nw_ + ne + 1 + nw_]
        h_s = rest[-1]

        @pl.when(pl.program_id(1) == 0)
        def _():
            hv = _normmod_fn(x_ref[...], nw_ref[...], sc_ref[...], sh_ref[...]).astype(BF)
            h_s[...] = hv
            h_ref[...] = hv
        for w_ref, o_ref in zip(w_refs, o_refs):
            o_ref[...] = lax.dot_general(h_s[...], w_ref[...], (((1,), (1,)), ((), ())),
                                         preferred_element_type=F32).astype(BF)
    row = pl.BlockSpec((tm, K), lambda i, j: (i, 0))
    vec = pl.BlockSpec((1, K), lambda i, j: (0, 0))
    out = pl.BlockSpec((tm, tn), lambda i, j: (i, j))
    res = _pcall(body, name=name, grid=(S // tm, N // tn),
                 in_specs=[row, vec, vec, vec] + [pl.BlockSpec((tn, K), lambda i, j: (j, 0))] * nw_ + [ANY] * ne,
                 out_specs=[row] + [out] * nw_,
                 out_shape=[jax.ShapeDtypeStruct((S, K), BF)] + [jax.ShapeDtypeStruct((S, N), BF)] * nw_,
                 scratch_shapes=[pltpu.VMEM((tm, K), BF)], compiler_params=_params())(
                     x, nw, sc, sh, *ws, *([] if after is None else [after]))
    return res[0], list(res[1:])


def _gate_bwd(dxv, o_ref, g_ref, do_ref, dg_ref):
    do_ref[...] = (dxv * g_ref[...]).astype(BF)
    _acc(dg_ref, jnp.sum(dxv * o_ref[...].astype(F32), axis=0, keepdims=True))


def _normmod_bwd(dh, x, nw, sc, sh, dres, gate, *, name, tm=512):
    S, Dm = x.shape
    tm = _tile(S, tm, 8)
    ng = 0 if gate is None else 2

    def body(dh_ref, x_ref, nw_ref, sc_ref, sh_ref, dres_ref, *rest):
        dx_ref, dnw_ref, dsc_ref, dsh_ref = rest[ng:ng + 4]
        xv, dy = x_ref[...], dh_ref[...]
        r = lax.rsqrt(jnp.mean(xv * xv, axis=-1, keepdims=True) + NORM_EPS)
        xn = xv * r
        t = dy * xn
        a = nw_ref[...] * (1.0 + sc_ref[...])
        dxv = dres_ref[...] + r * (dy * a - xn * jnp.mean(t * a, axis=-1, keepdims=True))
        dx_ref[...] = dxv
        ts = jnp.sum(t, axis=0, keepdims=True)
        _acc(dnw_ref, ts * (1.0 + sc_ref[...]))
        _acc(dsc_ref, ts * nw_ref[...])
        _acc(dsh_ref, jnp.sum(dy, axis=0, keepdims=True))
        if gate is not None:
            _gate_bwd(dxv, rest[0], rest[1], rest[ng + 4], rest[ng + 5])
    vec = jax.ShapeDtypeStruct((1, Dm), F32)
    gate_in = [] if gate is None else [_row_spec(tm, Dm), _vec_spec(Dm)]
    gate_out = [] if gate is None else [_row_spec(tm, Dm), _vec_spec(Dm)]
    gate_shape = [] if gate is None else [jax.ShapeDtypeStruct((S, Dm), BF), vec]
    return _pcall(body, name=name, grid=(S // tm,),
                  in_specs=[_row_spec(tm, Dm), _row_spec(tm, Dm), _vec_spec(Dm), _vec_spec(Dm), _vec_spec(Dm),
                            _row_spec(tm, Dm)] + gate_in,
                  out_specs=[_row_spec(tm, Dm), _vec_spec(Dm), _vec_spec(Dm), _vec_spec(Dm)] + gate_out,
                  out_shape=[jax.ShapeDtypeStruct((S, Dm), F32), vec, vec, vec] + gate_shape,
                  compiler_params=_params())(dh, x, nw, sc, sh, dres, *([] if gate is None else gate))


def _head(x, fw, target, gate, *, tm=512):
    S, Dm = x.shape
    tm = _tile(S, tm, 8)

    def body(x_ref, fw_ref, t_ref, o_ref, g_ref, dx_ref, dfw_ref, loss_ref, do_ref, dg_ref):
        xv, w = x_ref[...], fw_ref[...]
        r = lax.rsqrt(jnp.mean(xv * xv, axis=-1, keepdims=True) + NORM_EPS)
        xn = xv * r
        err = xn * w - t_ref[...]
        dy = err * (1.0 / Dm)
        t = dy * xn
        dx = r * (dy * w - xn * jnp.mean(t * w, axis=-1, keepdims=True))
        dx_ref[...] = dx
        _acc(dfw_ref, jnp.sum(t, axis=0, keepdims=True))
        part = 0.5 * jnp.sum(jnp.mean(err * err, axis=-1, keepdims=True), axis=0, keepdims=True)
        _acc(loss_ref, jnp.broadcast_to(part, (8, 128)))
        _gate_bwd(dx, o_ref, g_ref, do_ref, dg_ref)
    vec = jax.ShapeDtypeStruct((1, Dm), F32)
    return _pcall(body, name="head", grid=(S // tm,),
                  in_specs=[_row_spec(tm, Dm), _vec_spec(Dm), _row_spec(tm, Dm), _row_spec(tm, Dm), _vec_spec(Dm)],
                  out_specs=[_row_spec(tm, Dm), _vec_spec(Dm), pl.BlockSpec((8, 128), lambda i: (0, 0)),
                             _row_spec(tm, Dm), _vec_spec(Dm)],
                  out_shape=[jax.ShapeDtypeStruct((S, Dm), F32), vec, jax.ShapeDtypeStruct((8, 128), F32),
                             jax.ShapeDtypeStruct((S, Dm), BF), vec],
                  compiler_params=_params())(x, fw, target, *gate)


def _tril_mask():
    r = lax.broadcasted_iota(jnp.int32, (SGU_CHUNK, SGU_CHUNK), 0)
    c = lax.broadcasted_iota(jnp.int32, (SGU_CHUNK, SGU_CHUNK), 1)
    return c <= r


def _sgu_fwd(proj, lnw, lnb, w, b_t, *, name, after=None, tm=512):
    S = proj.shape[0]
    tm = _tile(S, tm, SGU_CHUNK)
    extra = [] if after is None else [after]

    def body(zu_ref, zv_ref, lnw_ref, lnb_ref, w_ref, bt_ref, *rest):
        o_ref = rest[-1]
        u = _gelu(zu_ref[...].astype(F32))
        vn = _ln_gelu_fn(zv_ref[...].astype(F32), lnw_ref[...], lnb_ref[...]).astype(BF)
        mask = _tril_mask()
        for g in range(SGU_GROUPS):
            wm = jnp.where(mask, w_ref[g], 0.0).astype(BF)
            cols = slice(g * 128, (g + 1) * 128)
            for ci in range(tm // SGU_CHUNK):
                rows = slice(ci * SGU_CHUNK, (ci + 1) * SGU_CHUNK)
                f = jnp.dot(wm, vn[rows, cols], preferred_element_type=F32) + bt_ref[:, g:g + 1]
                o_ref[rows, cols] = (u[rows, cols] * f).astype(BF)
    return _pcall(body, name=name, grid=(S // tm,),
                  in_specs=[pl.BlockSpec((tm, SGU_WIDTH), lambda i: (i, 0)), pl.BlockSpec((tm, SGU_WIDTH), lambda i: (i, 1)),
                            _vec_spec(SGU_WIDTH), _vec_spec(SGU_WIDTH),
                            pl.BlockSpec((SGU_GROUPS, 128, 128), lambda i: (0, 0, 0)),
                            pl.BlockSpec((128, SGU_GROUPS), lambda i: (0, 0))] + [ANY] * len(extra),
                  out_specs=_row_spec(tm, SGU_WIDTH), out_shape=jax.ShapeDtypeStruct((S, SGU_WIDTH), BF),
                  compiler_params=_params())(proj, proj, lnw, lnb, w, b_t, *extra)


def _sgu_bwd(dy, proj, lnw, lnb, w, b_t, dproj, *, name, tm=512):
    S = proj.shape[0]
    tm = _tile(S, tm, SGU_CHUNK)

    def body(dy_ref, zu_ref, zv_ref, lnw_ref, lnb_ref, w_ref, bt_ref, _, dz_ref, dlnw_ref, dlnb_ref, dw_ref, dbt_ref,
             f_s, dvn_s):
        first = pl.program_id(0) == 0

        @pl.when(first)
        def _():
            dw_ref[...] = jnp.zeros_like(dw_ref)
            dbt_ref[...] = jnp.zeros_like(dbt_ref)
        u, vjp_u = jax.vjp(_gelu, zu_ref[...].astype(F32))
        vn, vjp_v = jax.vjp(_ln_gelu_fn, zv_ref[...].astype(F32), lnw_ref[...], lnb_ref[...])
        vn = vn.astype(BF)
        dy_v = dy_ref[...]
        df = (dy_v * u).astype(BF)
        mask = _tril_mask()
        for g in range(SGU_GROUPS):
            wm = jnp.where(mask, w_ref[g], 0.0).astype(BF)
            cols = slice(g * 128, (g + 1) * 128)
            dwg = jnp.zeros((128, 128), F32)
            dbg = jnp.zeros((128, 1), F32)
            for ci in range(tm // SGU_CHUNK):
                rows = slice(ci * SGU_CHUNK, (ci + 1) * SGU_CHUNK)
                vn_c = vn[rows, cols]
                df_c = df[rows, cols]
                f_s[rows, cols] = jnp.dot(wm, vn_c, preferred_element_type=F32) + bt_ref[:, g:g + 1]
                dvn_s[rows, cols] = lax.dot_general(wm, df_c, (((0,), (0,)), ((), ())), preferred_element_type=F32)
                dwg = dwg + lax.dot_general(df_c, vn_c, (((1,), (1,)), ((), ())), preferred_element_type=F32)
                dbg = dbg + jnp.sum((dy_v[rows, cols] * u[rows, cols]), axis=1, keepdims=True)
            dw_ref[g] += jnp.where(mask, dwg, 0.0)
            dbt_ref[:, g:g + 1] += dbg
        (dzu,) = vjp_u(dy_v * f_s[...])
        dzv, dlnw, dlnb = vjp_v(dvn_s[...])
        dz_ref[:, :SGU_WIDTH] = dzu.astype(BF)
        dz_ref[:, SGU_WIDTH:] = dzv.astype(BF)
        _acc(dlnw_ref, dlnw)
        _acc(dlnb_ref, dlnb)
    vec = jax.ShapeDtypeStruct((1, SGU_WIDTH), F32)
    return _pcall(body, name=name, grid=(S // tm,),
                  in_specs=[_row_spec(tm, SGU_WIDTH),
                            pl.BlockSpec((tm, SGU_WIDTH), lambda i: (i, 0)), pl.BlockSpec((tm, SGU_WIDTH), lambda i: (i, 1)),
                            _vec_spec(SGU_WIDTH), _vec_spec(SGU_WIDTH),
                            pl.BlockSpec((SGU_GROUPS, 128, 128), lambda i: (0, 0, 0)),
                            pl.BlockSpec((128, SGU_GROUPS), lambda i: (0, 0)), ANY],
                  out_specs=[pl.BlockSpec((tm, 2 * SGU_WIDTH), lambda i: (i, P_Z // (2 * SGU_WIDTH))),
                             _vec_spec(SGU_WIDTH), _vec_spec(SGU_WIDTH),
                             pl.BlockSpec((SGU_GROUPS, 128, 128), lambda i: (0, 0, 0)),
                             pl.BlockSpec((128, SGU_GROUPS), lambda i: (0, 0))],
                  out_shape=[jax.ShapeDtypeStruct(dproj.shape, BF), vec, vec,
                             jax.ShapeDtypeStruct((SGU_GROUPS, 128, 128), F32),
                             jax.ShapeDtypeStruct((128, SGU_GROUPS), F32)],
                  scratch_shapes=[pltpu.VMEM((tm, SGU_WIDTH), F32), pltpu.VMEM((tm, SGU_WIDTH), F32)],
                  input_output_aliases={7: 0},
                  compiler_params=_params())(dy, proj, proj, lnw, lnb, w, b_t, dproj)


def _merge_fwd(y_sgu, y_attn, pa, pb, proj, *, name, after=None, tm=1024, tn=512):
    S, Dm = y_sgu.shape
    tm = _tile(S, tm, 8)
    nj = Dm // tn
    extra = [] if after is None else [after]

    def body(ys_ref, ya_ref, pa_ref, pb_ref, ga_ref, gb_ref, *rest):
        a_ref, b_ref, m_ref = rest[-3:]
        a = jnp.dot(ys_ref[...], pa_ref[...], preferred_element_type=F32)
        b = jnp.dot(ya_ref[...], pb_ref[...], preferred_element_type=F32)
        a_ref[...] = a.astype(BF)
        b_ref[...] = b.astype(BF)
        m_ref[...] = (_sigmoid(ga_ref[...].astype(F32)) * a + _sigmoid(gb_ref[...].astype(F32)) * b).astype(BF)
    row = pl.BlockSpec((tm, Dm), lambda i, j: (i, 0))
    col = pl.BlockSpec((Dm, tn), lambda i, j: (0, j))
    out = pl.BlockSpec((tm, tn), lambda i, j: (i, j))
    sh = jax.ShapeDtypeStruct((S, Dm), BF)
    return _pcall(body, name=name, grid=(S // tm, nj),
                  in_specs=[row, row, col, col, pl.BlockSpec((tm, tn), lambda i, j: (i, P_G // tn + j)),
                            pl.BlockSpec((tm, tn), lambda i, j: (i, (P_G + Dm) // tn + j))] + [ANY] * len(extra),
                  out_specs=[out, out, out], out_shape=[sh, sh, sh],
                  compiler_params=_params())(y_sgu, y_attn, pa, pb, proj, proj, *extra)


def _merge_bwd(do, w_out, a, b, proj, *, name, after=None, tm=512):
    S, Dm = a.shape
    tm = _tile(S, tm, 8)
    ga_blk, gb_blk = P_G // Dm, P_G // Dm + 1
    extra = [] if after is None else [after]

    def body(do_ref, w_ref, a_ref, b_ref, ga_ref, gb_ref, *rest):
        da_ref, db_ref, dg_ref = rest[-3:]
        dmv = lax.dot_general(do_ref[...], w_ref[...], (((1,), (1,)), ((), ())), preferred_element_type=F32)
        sa = _sigmoid(ga_ref[...].astype(F32))
        sb = _sigmoid(gb_ref[...].astype(F32))
        da_ref[...] = (dmv * sa).astype(BF)
        db_ref[...] = (dmv * sb).astype(BF)
        dg_ref[:, :Dm] = (dmv * a_ref[...].astype(F32) * sa * (1.0 - sa)).astype(BF)
        dg_ref[:, Dm:] = (dmv * b_ref[...].astype(F32) * sb * (1.0 - sb)).astype(BF)
    return _pcall(body, name=name, grid=(S // tm,),
                  in_specs=[_row_spec(tm, Dm), pl.BlockSpec((Dm, Dm), lambda i: (0, 0)), _row_spec(tm, Dm), _row_spec(tm, Dm),
                            pl.BlockSpec((tm, Dm), lambda i: (i, ga_blk)), pl.BlockSpec((tm, Dm), lambda i: (i, gb_blk))]
                  + [ANY] * len(extra),
                  out_specs=[_row_spec(tm, Dm), _row_spec(tm, Dm), pl.BlockSpec((tm, 2 * Dm), lambda i: (i, P_G // (2 * Dm)))],
                  out_shape=[jax.ShapeDtypeStruct((S, Dm), BF), jax.ShapeDtypeStruct((S, Dm), BF),
                             jax.ShapeDtypeStruct((S, IN_COLS), BF)],
                  compiler_params=_params())(do, w_out, a, b, proj, proj, *extra)


def _shift_rows(a, halo, k, up):
    n = a.shape[0]
    r8 = lax.broadcasted_iota(jnp.int32, (8, a.shape[1]), 0)
    if not up:
        rolled = pltpu.roll(a, k, 0)
        patch = jnp.where(r8 < k, pltpu.roll(halo, k, 0), rolled[:8])
        return jnp.concatenate([patch, rolled[8:]], axis=0)
    rolled = pltpu.roll(a, n - k, 0)
    patch = jnp.where(r8 >= 8 - k, pltpu.roll(halo, 8 - k, 0), rolled[n - 8:])
    return jnp.concatenate([rolled[:n - 8], patch], axis=0)


def _conv_taps(a, halo):
    return _shift_rows(a, halo, 2, False), _shift_rows(a, halo, 1, False), a


HALO = 16


def _prev_halo_spec(tm, Fd):
    return pl.BlockSpec((HALO, Fd), lambda i: (jnp.maximum(i * (tm // HALO) - 1, 0), 0))


def _conv_fwd(a_ref, halo_ref, cw_ref, cb_ref):
    halo = jnp.where(pl.program_id(0) > 0, halo_ref[...].astype(F32)[HALO - 8:], 0.0)
    t0, t1, t2 = _conv_taps(a_ref[...].astype(F32), halo)
    return t0, t1, t2, cb_ref[...] + cw_ref[0:1, :] * t0 + cw_ref[1:2, :] * t1 + cw_ref[2:3, :] * t2


def _ffn_act_fwd(a, up, cw, cb, *, name, tm=256):
    S, Fd = a.shape
    tm = _tile(S, tm, HALO)

    def body(a_ref, up_ref, halo_ref, cw_ref, cb_ref, o_ref, ac_ref):
        _, _, _, ac = _conv_fwd(a_ref, halo_ref, cw_ref, cb_ref)
        ac_ref[...] = ac.astype(BF)
        o_ref[...] = (ac * _sigmoid(ac) * up_ref[...].astype(F32)).astype(BF)
    sh = jax.ShapeDtypeStruct((S, Fd), BF)
    return _pcall(body, name=name, grid=(S // tm,),
                  in_specs=[_row_spec(tm, Fd), _row_spec(tm, Fd), _prev_halo_spec(tm, Fd),
                            pl.BlockSpec((3, Fd), lambda i: (0, 0)), _vec_spec(Fd)],
                  out_specs=[_row_spec(tm, Fd), _row_spec(tm, Fd)], out_shape=[sh, sh],
                  compiler_params=_params())(a, up, a, cw, cb)


def _ffn_act_bwd_a(dhf, ac, up, *, name, tm=512):
    S, Fd = ac.shape
    tm = _tile(S, tm, HALO)

    def body(dhf_ref, ac_ref, up_ref, dac_ref, dup_ref, dcb_ref):
        acv = ac_ref[...].astype(F32)
        s = _sigmoid(acv)
        dhf_v = dhf_ref[...].astype(F32)
        dup_ref[...] = (dhf_v * acv * s).astype(BF)
        dac = dhf_v * up_ref[...].astype(F32) * (s * (1.0 + acv * (1.0 - s)))
        dac_ref[...] = dac.astype(BF)
        _acc(dcb_ref, jnp.sum(dac, axis=0, keepdims=True))
    sh = jax.ShapeDtypeStruct((S, Fd), BF)
    return _pcall(body, name=name, grid=(S // tm,), in_specs=[_row_spec(tm, Fd)] * 3,
                  out_specs=[_row_spec(tm, Fd), _row_spec(tm, Fd), _vec_spec(Fd)],
                  out_shape=[sh, sh, jax.ShapeDtypeStruct((1, Fd), F32)], compiler_params=_params())(dhf, ac, up)


def _ffn_act_bwd_b(dac, a, cw, *, name, tm=256):
    S, Fd = dac.shape
    tm = _tile(S, tm, HALO)
    last = S // tm - 1

    def body(d_ref, halo_ref, a_ref, cw_ref, o_ref, dcw_ref):
        halo = jnp.where(pl.program_id(0) < last, halo_ref[...].astype(F32)[:8], 0.0)
        d = d_ref[...].astype(F32)
        d1, d2 = _shift_rows(d, halo, 1, True), _shift_rows(d, halo, 2, True)
        o_ref[...] = (cw_ref[2:3, :] * d + cw_ref[1:2, :] * d1 + cw_ref[0:1, :] * d2).astype(BF)
        av = a_ref[...].astype(F32)
        _acc(dcw_ref, jnp.concatenate([jnp.sum(av * d2, axis=0, keepdims=True),
                                       jnp.sum(av * d1, axis=0, keepdims=True),
                                       jnp.sum(av * d, axis=0, keepdims=True)], axis=0))
    return _pcall(body, name=name, grid=(S // tm,),
                  in_specs=[_row_spec(tm, Fd),
                            pl.BlockSpec((HALO, Fd), lambda i: (jnp.minimum((i + 1) * (tm // HALO), S // HALO - 1), 0)),
                            _row_spec(tm, Fd), pl.BlockSpec((3, Fd), lambda i: (0, 0))],
                  out_specs=[_row_spec(tm, Fd), pl.BlockSpec((3, Fd), lambda i: (0, 0))],
                  out_shape=[jax.ShapeDtypeStruct((S, Fd), BF), jax.ShapeDtypeStruct((3, Fd), F32)],
                  compiler_params=_params())(dac, dac, a, cw)


def _rope_tables(pos_col, inv_row, m1_row, m2_row):
    S = pos_col.shape[0]
    tm = _tile(S, 512, 8)

    def body(p_ref, inv_ref, m1_ref, m2_ref, c_ref, s1_ref, s2_ref):
        ang = p_ref[...] * inv_ref[...]
        sn = jnp.sin(ang)
        c_ref[...] = jnp.cos(ang)
        s1_ref[...] = -sn * m1_ref[...]
        s2_ref[...] = sn * m2_ref[...]
    sh = jax.ShapeDtypeStruct((S, 128), F32)
    return _pcall(body, name="rope_tables", grid=(S // tm,),
                  in_specs=[pl.BlockSpec((tm, 1), lambda i: (i, 0)), _vec_spec(128), _vec_spec(128), _vec_spec(128)],
                  out_specs=[_row_spec(tm, 128)] * 3, out_shape=[sh, sh, sh], compiler_params=_params())(
                      pos_col, inv_row, m1_row, m2_row)


def _rope_apply(x, c, s1, s2):
    outs = []
    for j in range(x.shape[1] // 128):
        xj = x[:, j * 128:(j + 1) * 128]
        outs.append(xj * c + pltpu.roll(xj, 120, 1) * s1 + pltpu.roll(xj, 8, 1) * s2)
    return outs[0] if len(outs) == 1 else jnp.concatenate(outs, axis=1)


def _rope_apply_t(d, c, s1, s2):
    outs = []
    for j in range(d.shape[1] // 128):
        dj = d[:, j * 128:(j + 1) * 128]
        outs.append(dj * c + pltpu.roll(dj * s1, 8, 1) + pltpu.roll(dj * s2, 120, 1))
    return outs[0] if len(outs) == 1 else jnp.concatenate(outs, axis=1)


def _rope_fwd(proj, c, s1, s2, *, name, tm=512):
    S = proj.shape[0]
    tm = _tile(S, tm, 8)

    def body(q_ref, k_ref, v_ref, c_ref, s1_ref, s2_ref, qo_ref, ko_ref, vo_ref):
        cv, s1v, s2v = c_ref[...], s1_ref[...], s2_ref[...]
        qo_ref[...] = (_rope_apply(q_ref[...].astype(F32), cv, s1v, s2v) * (HEAD_DIM ** -0.5)).astype(BF)
        ko_ref[...] = _rope_apply(k_ref[...].astype(F32), cv, s1v, s2v).astype(BF)
        vo_ref[...] = v_ref[...].astype(BF)
    return _pcall(body, name=name, grid=(S // tm,),
                  in_specs=[pl.BlockSpec((tm, Q_END), lambda i: (i, P_Q // Q_END)),
                            pl.BlockSpec((tm, 128), lambda i: (i, P_K // 128)),
                            pl.BlockSpec((tm, 128), lambda i: (i, P_V // 128)),
                            _row_spec(tm, 128), _row_spec(tm, 128), _row_spec(tm, 128)],
                  out_specs=[_row_spec(tm, Q_END), _row_spec(tm, 128), _row_spec(tm, 128)],
                  out_shape=[jax.ShapeDtypeStruct((S, Q_END), BF), jax.ShapeDtypeStruct((S, 128), BF),
                             jax.ShapeDtypeStruct((S, 128), BF)],
                  compiler_params=_params())(proj, proj, proj, c, s1, s2)


def _rope_bwd(dq, dk, dv, c, s1, s2, dproj, *, name, tm=512):
    S = dq.shape[0]
    tm = _tile(S, tm, 8)
    tabs = [_row_spec(tm, 128)] * 3
    shape = jax.ShapeDtypeStruct(dproj.shape, BF)

    def body_q(dq_ref, c_ref, s1_ref, s2_ref, _, o_ref):
        o_ref[...] = _rope_apply_t(dq_ref[...].astype(F32), c_ref[...], s1_ref[...], s2_ref[...]).astype(BF)
    dproj = _pcall(body_q, name=name + "_q", grid=(S // tm,), in_specs=[_row_spec(tm, Q_END)] + tabs + [ANY],
                   out_specs=pl.BlockSpec((tm, Q_END), lambda i: (i, P_Q // Q_END)), out_shape=shape,
                   input_output_aliases={4: 0}, compiler_params=_params())(dq, c, s1, s2, dproj)

    def body_kv(dk_ref, dv_ref, c_ref, s1_ref, s2_ref, _, o_ref):
        o_ref[:, :128] = _rope_apply_t(dk_ref[...], c_ref[...], s1_ref[...], s2_ref[...]).astype(BF)
        o_ref[:, 128:] = dv_ref[...].astype(BF)
    return _pcall(body_kv, name=name + "_kv", grid=(S // tm,),
                  in_specs=[_row_spec(tm, 128), _row_spec(tm, 128)] + tabs + [ANY],
                  out_specs=pl.BlockSpec((tm, 256), lambda i: (i, P_K // 256)), out_shape=shape,
                  input_output_aliases={5: 0}, compiler_params=_params())(dk, dv, c, s1, s2, dproj)


def _lane_lo(shape):
    return lax.broadcasted_iota(jnp.int32, shape, 1) < HEAD_DIM


def _stack_heads(x, g):
    lo = _lane_lo((ATTN_BLOCK, 128))
    zero = jnp.zeros((ATTN_BLOCK, 128), x.dtype)
    parts = []
    for p in range(Q_PER_KV // 2):
        xp = x[:, (g * 4 + p) * 128:(g * 4 + p + 1) * 128]
        parts += [jnp.where(lo, xp, zero), jnp.where(lo, zero, xp)]
    return jnp.concatenate(parts, axis=0)


def _unstack_heads(o2):
    lo = _lane_lo((ATTN_BLOCK, 128))
    return [jnp.where(lo, o2[2 * p * ATTN_BLOCK:(2 * p + 1) * ATTN_BLOCK], o2[(2 * p + 1) * ATTN_BLOCK:(2 * p + 2) * ATTN_BLOCK])
            for p in range(Q_PER_KV // 2)]


def _dup_half(prev, cur, g):
    x = jnp.concatenate([prev, cur], axis=0).astype(F32)
    lo = _lane_lo(x.shape)
    r = pltpu.roll(x, HEAD_DIM, 1)
    return (jnp.where(lo, x, r) if g == 0 else jnp.where(lo, r, x)).astype(BF)


def _fold_halves(x):
    return x + pltpu.roll(x, HEAD_DIM, 1)


def _attn_bias():
    i = lax.broadcasted_iota(jnp.int32, (Q_PER_KV * ATTN_BLOCK, 2 * ATTN_BLOCK), 0) & (ATTN_BLOCK - 1)
    j = lax.broadcasted_iota(jnp.int32, (Q_PER_KV * ATTN_BLOCK, 2 * ATTN_BLOCK), 1)
    band = (j > i) & (j <= i + ATTN_BLOCK)
    return jnp.stack([jnp.where(band & (j >= ATTN_BLOCK), 0.0, -jnp.inf), jnp.where(band, 0.0, -jnp.inf)]).astype(F32)


def _both(x):
    return jnp.concatenate([x, x], axis=1)


def _row_sums(x_bf):
    return jnp.dot(x_bf, jnp.ones((x_bf.shape[1], 128), BF), preferred_element_type=F32)


def _attn_probs(qs, kb, sink, bias):
    s = lax.dot_general(qs, kb, (((1,), (1,)), ((), ())), preferred_element_type=F32) + bias
    m = jnp.maximum(jnp.broadcast_to(jnp.max(s, axis=-1, keepdims=True), sink.shape), sink)
    return jnp.exp(s - _both(m)), jnp.exp(sink - m)


def _attn_specs(S):
    nb = S // ATTN_BLOCK
    qs = pl.BlockSpec((ATTN_BLOCK, Q_END), lambda n: (n, 0))
    cur = pl.BlockSpec((ATTN_BLOCK, 128), lambda n: (n, 0))
    prev = pl.BlockSpec((ATTN_BLOCK, 128), lambda n: (jnp.maximum(n - 1, 0), 0))
    sink = pl.BlockSpec((N_KV_HEADS, Q_PER_KV * ATTN_BLOCK, 128), lambda n: (0, 0, 0))
    bias = pl.BlockSpec((None, Q_PER_KV * ATTN_BLOCK, 2 * ATTN_BLOCK), lambda n: (jnp.minimum(n, 1), 0, 0))
    return nb, qs, cur, prev, sink, bias


def _attn_fwd(q, k, v, sink_rows, bias, *, name):
    S = q.shape[0]
    nb, qs, cur, prev, sink, bs = _attn_specs(S)

    def body(q_ref, kp_ref, kc_ref, vp_ref, vc_ref, sk_ref, b_ref, o_ref):
        for g in range(N_KV_HEADS):
            kb = _dup_half(kp_ref[...], kc_ref[...], g)
            vb = _dup_half(vp_ref[...], vc_ref[...], g)
            p, es = _attn_probs(_stack_heads(q_ref[...], g), kb, sk_ref[g], b_ref[...])
            ones = jnp.ones((2 * ATTN_BLOCK, 128), BF)
            o3 = jnp.dot(p.astype(BF), jnp.concatenate([vb, ones], axis=1), preferred_element_type=F32)
            o2 = o3[:, :128] / (o3[:, 128:] + es)
            for t, tile in enumerate(_unstack_heads(o2)):
                o_ref[:, (g * 4 + t) * 128:(g * 4 + t + 1) * 128] = tile.astype(BF)
    return _pcall(body, name=name, grid=(nb,), in_specs=[qs, prev, cur, prev, cur, sink, bs], out_specs=qs,
                  out_shape=jax.ShapeDtypeStruct(q.shape, BF), compiler_params=_params())(q, k, k, v, v, sink_rows, bias)


def _attn_bwd(do, q, k, v, sink_rows, bias, *, name):
    S = q.shape[0]
    nb, qs, cur, prev, sink, bs = _attn_specs(S)
    full = pl.BlockSpec((S, 128), lambda n: (0, 0))
    dsk_spec = pl.BlockSpec((N_KV_HEADS, Q_PER_KV, 128), lambda n: (0, 0, 0))

    def body(do_ref, q_ref, kp_ref, kc_ref, vp_ref, vc_ref, sk_ref, b_ref, dq_ref, dk_ref, dv_ref, dsk_ref):
        n = pl.program_id(0)

        @pl.when(n == 0)
        def _():
            dk_ref[...] = jnp.zeros_like(dk_ref)
            dv_ref[...] = jnp.zeros_like(dv_ref)
            dsk_ref[...] = jnp.zeros_like(dsk_ref)
        sub = lax.broadcasted_iota(jnp.int32, (Q_PER_KV, 128), 0)
        dkf, dvf = [], []
        for g in range(N_KV_HEADS):
            qst = _stack_heads(q_ref[...], g)
            dos = _stack_heads(do_ref[...], g)
            kb = _dup_half(kp_ref[...], kc_ref[...], g)
            vb = _dup_half(vp_ref[...], vc_ref[...], g)
            pu, es = _attn_probs(qst, kb, sk_ref[g], b_ref[...])
            inv = 1.0 / (_row_sums(pu.astype(BF)) + es)
            p = pu * _both(inv)
            dp = lax.dot_general(dos, vb, (((1,), (1,)), ((), ())), preferred_element_type=F32)
            dd = _row_sums((p * dp).astype(BF))
            ds = (p * (dp - _both(dd))).astype(BF)
            dq2 = jnp.dot(ds, kb, preferred_element_type=F32) * (HEAD_DIM ** -0.5)
            for t, tile in enumerate(_unstack_heads(dq2)):
                dq_ref[:, (g * 4 + t) * 128:(g * 4 + t + 1) * 128] = tile.astype(BF)
            dkf.append(_fold_halves(lax.dot_general(ds, qst, (((0,), (0,)), ((), ())), preferred_element_type=F32)))
            dvf.append(_fold_halves(lax.dot_general(p.astype(BF), dos, (((0,), (0,)), ((), ())),
                                                    preferred_element_type=F32)))
            dsr = -(es * inv * dd)
            upd = jnp.zeros((Q_PER_KV, 128), F32)
            for h in range(Q_PER_KV):
                upd = jnp.where(sub == h, jnp.sum(dsr[h * ATTN_BLOCK:(h + 1) * ATTN_BLOCK], axis=0, keepdims=True), upd)
            dsk_ref[g] += upd
        lo = _lane_lo((2 * ATTN_BLOCK, 128))
        dkb = jnp.where(lo, dkf[0], dkf[1])
        dvb = jnp.where(lo, dvf[0], dvf[1])
        r0 = pl.multiple_of(n * ATTN_BLOCK, ATTN_BLOCK)
        dk_ref[pl.ds(r0, ATTN_BLOCK), :] += dkb[ATTN_BLOCK:]
        dv_ref[pl.ds(r0, ATTN_BLOCK), :] += dvb[ATTN_BLOCK:]

        @pl.when(n > 0)
        def _():
            rp = pl.multiple_of((n - 1) * ATTN_BLOCK, ATTN_BLOCK)
            dk_ref[pl.ds(rp, ATTN_BLOCK), :] += dkb[:ATTN_BLOCK]
            dv_ref[pl.ds(rp, ATTN_BLOCK), :] += dvb[:ATTN_BLOCK]
    return _pcall(body, name=name, grid=(nb,), in_specs=[qs, qs, prev, cur, prev, cur, sink, bs],
                  out_specs=[qs, full, full, dsk_spec],
                  out_shape=[jax.ShapeDtypeStruct(q.shape, BF), jax.ShapeDtypeStruct((S, 128), F32),
                             jax.ShapeDtypeStruct((S, 128), F32), jax.ShapeDtypeStruct((N_KV_HEADS, Q_PER_KV, 128), F32)],
                  compiler_params=_params())(do, q, k, k, v, v, sink_rows, bias)


def _ada_fwd(c_all, ada_w):
    ncol = ada_w.shape[2]

    def body(c_ref, w_ref, o_ref):
        cv = c_ref[...]
        ca = (cv * _sigmoid(cv)).astype(BF)
        for l in range(DEPTH):
            o_ref[:, l * ncol:(l + 1) * ncol] = jnp.dot(ca, w_ref[l].astype(BF), preferred_element_type=F32)
    return _pcall(body, name="ada_fwd", out_shape=jax.ShapeDtypeStruct((N_DEV, DEPTH * ncol), F32),
                  compiler_params=_params())(c_all, ada_w)


def _ada_bwd(c_all, dm):
    ncol = dm.shape[2]

    def body(c_ref, dm_ref, o_ref):
        cv = c_ref[...]
        ca = (cv * _sigmoid(cv)).astype(BF)
        for l in range(DEPTH):
            o_ref[l] = lax.dot_general(ca, dm_ref[l].astype(BF), (((0,), (0,)), ((), ())), preferred_element_type=F32)
    return _pcall(body, name="ada_bwd", out_shape=jax.ShapeDtypeStruct((DEPTH, D_MODEL, ncol), F32),
                  compiler_params=_params())(c_all, dm)


def _adamw(w, g, m, v, *, name):
    R, C = w.shape
    tr = R
    for t in range(8, 513, 8):
        if R % t == 0:
            tr = t
    c1 = 1.0 - ADAM_B1 ** ADAM_STEP
    c2 = 1.0 - ADAM_B2 ** ADAM_STEP

    def body(w_ref, g_ref, m_ref, v_ref, d_ref, mo_ref, vo_ref):
        gv = g_ref[...]
        mn = ADAM_B1 * m_ref[...] + (1.0 - ADAM_B1) * gv
        vn = ADAM_B2 * v_ref[...] + (1.0 - ADAM_B2) * (gv * gv)
        mo_ref[...] = mn
        vo_ref[...] = vn
        d_ref[...] = -ADAM_LR * ((mn * (1.0 / c1)) / (jnp.sqrt(vn * (1.0 / c2)) + ADAM_EPS) + ADAM_WD * w_ref[...])
    spec = pl.BlockSpec((tr, C), lambda i: (i, 0))
    sh = jax.ShapeDtypeStruct((R, C), F32)
    return _pcall(body, name=name, grid=(R // tr,), in_specs=[spec] * 4, out_specs=[spec] * 3, out_shape=[sh, sh, sh],
                  compiler_params=_params())(w, g, m, v)


def _adamw_layers(w, g_layers, m, v, *, name):
    L, R, C = w.shape
    assert L == 2 and len(g_layers) == 2
    tr = R
    for t in range(8, 513, 8):
        if R % t == 0:
            tr = t
    c1 = 1.0 - ADAM_B1 ** ADAM_STEP
    c2 = 1.0 - ADAM_B2 ** ADAM_STEP

    def body(w_ref, g0_ref, g1_ref, m_ref, v_ref, go_ref, d_ref, mo_ref, vo_ref):
        gv = jnp.where(pl.program_id(0) == 0, g0_ref[...], g1_ref[...])
        go_ref[...] = gv
        mn = ADAM_B1 * m_ref[...] + (1.0 - ADAM_B1) * gv
        vn = ADAM_B2 * v_ref[...] + (1.0 - ADAM_B2) * (gv * gv)
        mo_ref[...] = mn
        vo_ref[...] = vn
        d_ref[...] = -ADAM_LR * ((mn * (1.0 / c1)) / (jnp.sqrt(vn * (1.0 / c2)) + ADAM_EPS) + ADAM_WD * w_ref[...])
    spec = pl.BlockSpec((None, tr, C), lambda l, i: (l, i, 0))
    sh = jax.ShapeDtypeStruct((L, R, C), F32)
    g_specs = [pl.BlockSpec((tr, C), lambda l, i, k=k: (jnp.where(l == k, i, 0), 0)) for k in range(L)]
    return _pcall(body, name=name, grid=(L, R // tr), in_specs=[spec] + g_specs + [spec, spec], out_specs=[spec] * 4,
                  out_shape=[sh] * 4, compiler_params=_params())(w, *g_layers, m, v)


def _sum8(parts, *, name):
    _, R, C = parts.shape
    tr = _tile(R, 512, 16)

    def body(p_ref, o_ref):
        acc = p_ref[0].astype(F32)
        for k in range(1, N_DEV):
            acc = acc + p_ref[k].astype(F32)
        o_ref[...] = acc
    return _pcall(body, name=name, grid=(R // tr,), in_specs=[pl.BlockSpec((N_DEV, tr, C), lambda i: (0, i, 0))],
                  out_specs=pl.BlockSpec((tr, C), lambda i: (i, 0)), out_shape=jax.ShapeDtypeStruct((R, C), F32),
                  compiler_params=_params())(parts)


MESH_ID = pl.DeviceIdType.MESH
ANY = pl.BlockSpec(memory_space=pl.ANY)


def _all_gather(x, *, name, after=None):
    R, C = x.shape
    extra = [] if after is None else [after]

    def body(x_ref, *rest):
        out_ref, send_sems, recv_sems, local_sem = rest[-4:]
        mx, my, mc = lax.axis_index("x"), lax.axis_index("y"), lax.axis_index("c")
        me, sibling = (mx, my, mc), (mx, my, 1 - mc)
        chips = [(1 - mx, my), (mx, 1 - my), (1 - mx, 1 - my)]

        def blk(px, py, pc):
            return out_ref.at[4 * px + 2 * py + pc]

        def copy(k, block, to, src=None):
            return pltpu.make_async_remote_copy(
                src_ref=blk(*block) if src is None else src, dst_ref=blk(*block),
                send_sem=send_sems.at[k], recv_sem=recv_sems.at[k], device_id=to, device_id_type=MESH_ID)

        mine = pltpu.make_async_copy(x_ref, blk(*me), local_sem)
        mine.start()
        first = [copy(0, me, sibling, src=x_ref)]
        first += [copy(1 + j, me, (*chip, mc), src=x_ref) for j, chip in enumerate(chips)]
        for cp in first:
            cp.start()
        passed = [copy(4 + j, (*chip, mc), sibling) for j, chip in enumerate(chips)]
        for j, chip in enumerate(chips):
            copy(1 + j, (*chip, mc), me).wait_recv()
            passed[j].start()
        copy(0, sibling, me).wait_recv()
        for j, chip in enumerate(chips):
            copy(4 + j, (*chip, 1 - mc), me).wait_recv()
        for cp in first + passed:
            cp.wait_send()
        mine.wait()
    return _pcall(body, name=name, in_specs=[ANY] * (1 + len(extra)), out_specs=ANY,
                  out_shape=jax.ShapeDtypeStruct((N_DEV, R, C), x.dtype),
                  scratch_shapes=[pltpu.SemaphoreType.DMA((7,)), pltpu.SemaphoreType.DMA((7,)), pltpu.SemaphoreType.DMA],
                  compiler_params=pltpu.CompilerParams(has_side_effects=True))(x, *extra)


HBM_SPEC = pl.BlockSpec(memory_space=pltpu.HBM)
SEM_SPEC = pl.BlockSpec(memory_space=pltpu.SEMAPHORE)
DATAFLOW = pltpu.SideEffectType.DATAFLOW_SIDE_EFFECTING


def _coords():
    return lax.axis_index("x"), lax.axis_index("y"), lax.axis_index("c")


def _other_chips(mx, my):
    return [(1 - mx, my), (mx, 1 - my), (1 - mx, 1 - my)]


def _plan_gather_ici(refs, send, recv):
    k = len(refs) // 2
    mx, my, mc = _coords()
    return [pltpu.make_async_remote_copy(src_ref=refs[w], dst_ref=refs[k + w].at[2 * mx + my, mc], send_sem=send[3 * w + j],
                                         recv_sem=recv[3 * w + j], device_id=(px, py, mc), device_id_type=MESH_ID)
            for w in range(k) for j, (px, py) in enumerate(_other_chips(mx, my))]


N_CHIPS = 4


def _plan_gather_d2d(refs, send, recv):
    mx, my, mc = _coords()
    return [pltpu.make_async_remote_copy(src_ref=land.at[q, mc], dst_ref=land.at[q, mc], send_sem=send[N_CHIPS * w + q],
                                         recv_sem=recv[N_CHIPS * w + q], device_id=(mx, my, 1 - mc), device_id_type=MESH_ID)
            for w, land in enumerate(refs) for q in range(N_CHIPS)]


def _plan_reduce_d2d(refs, send, recv):
    g, land = refs
    mx, my, mc = _coords()
    return [pltpu.make_async_remote_copy(src_ref=g.at[q, 1 - mc], dst_ref=land.at[q], send_sem=send[q], recv_sem=recv[q],
                                         device_id=(mx, my, 1 - mc), device_id_type=MESH_ID) for q in range(N_CHIPS)]


def _plan_reduce_ici(refs, send, recv):
    h, land = refs
    mx, my, mc = _coords()
    return [pltpu.make_async_remote_copy(src_ref=h.at[2 * px + py], dst_ref=land.at[j], send_sem=send[j], recv_sem=recv[j],
                                         device_id=(px, py, mc), device_id_type=MESH_ID)
            for j, (px, py) in enumerate(_other_chips(mx, my))]


def _rdma_start(bufs, n, plan, *, name, after=None):
    nb = len(bufs)
    extra = [] if after is None else [after]
    ne = len(extra)

    def body(*refs):
        ins, send, recv = refs[:nb], refs[nb + ne:nb + ne + n], refs[nb + ne + n:nb + ne + 2 * n]
        token = refs[-1]
        for cp in plan(ins, send, recv):
            cp.start()
        token[...] = jnp.zeros_like(token)
    out = _pcall(body, name=name,
                 out_shape=tuple([pltpu.SemaphoreType.DMA(())] * (2 * n) + [pltpu.HBM(b.shape, b.dtype) for b in bufs]
                                 + [jax.ShapeDtypeStruct((8, 128), F32)]),
                 in_specs=tuple([HBM_SPEC] * nb + [ANY] * ne),
                 out_specs=tuple([SEM_SPEC] * (2 * n) + [HBM_SPEC] * nb + [pl.BlockSpec(memory_space=pltpu.VMEM)]),
                 input_output_aliases={i: 2 * n + i for i in range(nb)},
                 compiler_params=pltpu.CompilerParams(has_side_effects=DATAFLOW))(
                     *[pltpu.with_memory_space_constraint(b, pltpu.HBM) for b in bufs], *extra)
    return list(out[:2 * n]), list(out[2 * n:2 * n + nb]), out[-1]


def _rdma_wait(sems, bufs, n, plan, after, *, name):
    nb = len(bufs)

    def body(*refs):
        ins, send, recv = refs[:nb], refs[nb:nb + n], refs[nb + n:nb + 2 * n]
        for cp in plan(ins, send, recv):
            cp.wait_send()
            cp.wait_recv()
    out = _pcall(body, name=name, out_shape=tuple(pltpu.HBM(b.shape, b.dtype) for b in bufs),
                 in_specs=tuple([HBM_SPEC] * nb + [SEM_SPEC] * (2 * n) + [ANY]), out_specs=tuple([HBM_SPEC] * nb),
                 input_output_aliases={i: i for i in range(nb)},
                 compiler_params=pltpu.CompilerParams(has_side_effects=DATAFLOW))(*bufs, *sems, after)
    return list(out)


def _sum_pair(g, land, cidx, *, name):
    nchip, _, R, C = g.shape
    tr = _tile(R, 1056, 16)

    def body(c_ref, g_ref, l_ref, o_ref):
        o_ref[...] = g_ref[...] + l_ref[...]
    grid_spec = pltpu.PrefetchScalarGridSpec(
        num_scalar_prefetch=1, grid=(nchip, R // tr),
        in_specs=[pl.BlockSpec((None, None, tr, C), lambda p, i, c_ref: (p, c_ref[0], i, 0)),
                  pl.BlockSpec((None, tr, C), lambda p, i, c_ref: (p, i, 0))],
        out_specs=pl.BlockSpec((None, tr, C), lambda p, i, c_ref: (p, i, 0)))
    return _pcall(body, name=name, grid_spec=grid_spec, out_shape=jax.ShapeDtypeStruct((nchip, R, C), BF),
                  compiler_params=_params())(cidx, g, land)


def _sum_chips(h, land, chipidx, *, name):
    _, R, C = h.shape
    tr = _tile(R, 1056, 16)

    def body(c_ref, h_ref, l_ref, o_ref):
        acc = h_ref[...].astype(F32)
        for j in range(3):
            acc = acc + l_ref[j].astype(F32)
        o_ref[...] = acc
    grid_spec = pltpu.PrefetchScalarGridSpec(
        num_scalar_prefetch=1, grid=(R // tr,),
        in_specs=[pl.BlockSpec((None, tr, C), lambda i, c_ref: (c_ref[0], i, 0)),
                  pl.BlockSpec((3, tr, C), lambda i, c_ref: (0, i, 0))],
        out_specs=pl.BlockSpec((tr, C), lambda i, c_ref: (i, 0)))
    return _pcall(body, name=name, grid_spec=grid_spec, out_shape=jax.ShapeDtypeStruct((R, C), F32),
                  compiler_params=_params())(chipidx, h, land)


PART_IN = ("w_in",)
PART_MIX = ("proj_a", "proj_b", "w_out")
PART_FFN = ("ffn_w_gate", "ffn_w_up", "ffn_w_down")


def _part_rows(names):
    return sum(BIG_ROWS[n] for n in names)


def _part_offsets(names):
    off, r = {}, 0
    for n in names:
        off[n] = r
        r += BIG_ROWS[n]
    return off


def _shard_rows(shards, l, names):
    return [(shards[n][l].T if n in COL_SHARDED else shards[n][l]).astype(BF) for n in names]


def _pack_shards(shards, l, names):
    return jnp.concatenate(_shard_rows(shards, l, names), axis=0)


def _unpack_weights(full8, names):
    off = _part_offsets(names)

    def whole(n):
        if isinstance(full8, (list, tuple)):
            return full8[names.index(n)].reshape(N_DEV * BIG_ROWS[n], 1024)
        return full8[:, off[n]:off[n] + BIG_ROWS[n], :].reshape(N_DEV * BIG_ROWS[n], 1024)
    out = {}
    if "w_in" in names:
        wt_in = whole("w_in")
        out["wt_in"] = jnp.concatenate([wt_in[V_END:], wt_in[:V_END]], axis=0)
    for n in ("proj_a", "proj_b", "w_out"):
        if n in names:
            out[n] = whole(n)
    if "ffn_w_gate" in names:
        out["wt_gate"], out["wt_up"], out["w_down"] = whole("ffn_w_gate"), whole("ffn_w_up"), whole("ffn_w_down")
    return out


def _from_land(land):
    return land.reshape(N_DEV, land.shape[2], 1024)


def _pack_grads(wg, names):
    full = {"proj_a": wg.get("proj_a"), "proj_b": wg.get("proj_b"), "w_out": wg.get("w_out"), "ffn_w_down": wg.get("w_down"),
            "ffn_w_gate": wg.get("wt_gate"), "ffn_w_up": wg.get("wt_up")}
    if "w_in" in names:
        full["w_in"] = jnp.concatenate([wg["wt_in"][P_Q:], wg["wt_in"][:P_Q]], axis=0)
    blocks = jnp.concatenate([full[n].reshape(N_DEV, BIG_ROWS[n], 1024) for n in names], axis=1)
    return blocks.reshape(N_CHIPS, 2, _part_rows(names), 1024)


def _unpack_shard_grads(gs, names):
    off = _part_offsets(names)
    return {n: gs[off[n]:off[n] + BIG_ROWS[n]] for n in names}


def _rope_setup(positions):
    S = positions.shape[0]
    inv = ROPE_THETA ** (-jnp.arange(0, ROT_DIM, 2, dtype=F32) / ROT_DIM)
    lane = np.arange(128) % HEAD_DIM
    half = ROT_DIM // 2
    inv_row = jnp.where(lane < ROT_DIM, jnp.tile(inv, 128 // half), 0.0)[None, :].astype(F32)
    m1_row = jnp.asarray((lane < half).astype(np.float32))[None, :]
    m2_row = jnp.asarray(((lane >= half) & (lane < ROT_DIM)).astype(np.float32))[None, :]
    return (*_rope_tables(positions.astype(F32).reshape(S, 1), inv_row, m1_row, m2_row), _attn_bias())


def _hook(hooks, point, after):
    f = None if hooks is None else hooks.get(point)
    return None if f is None else f(after)


def _layer_fwd(l, x, mod_l, W, small, rope, hooks=None):
    rc, rs1, rs2, bias = rope
    sh1, sc1, g1, sh2, sc2, g2 = [mod_l[i * D_MODEL:(i + 1) * D_MODEL][None, :] for i in range(6)]
    nw1, nw2 = small["norm1_w"][l][None, :], small["norm2_w"][l][None, :]
    tok = _hook(hooks, "mm_in", x)
    h, (proj,) = _norm_mm(x, nw1, sc1, sh1, [W["wt_in"]], name=f"mm_in{l}", after=tok, tm=2048, tn_cap=768)
    q_r, k_r, v_b = _rope_fwd(proj, rc, rs1, rs2, name=f"rope_fwd{l}")
    sink_rows = jnp.repeat(small["attn_sinks"][l].reshape(N_KV_HEADS, Q_PER_KV), ATTN_BLOCK, axis=1)
    sink_rows = jnp.broadcast_to(sink_rows[..., None], sink_rows.shape + (128,))
    y_attn = _attn_fwd(q_r, k_r, v_b, sink_rows, bias, name=f"attn_fwd{l}")
    lnw, lnb = small["sgu_ln_w"][l][None, :], small["sgu_ln_b"][l][None, :]
    sgu_bt = small["sgu_b"][l].T
    y_sgu = _sgu_fwd(proj, lnw, lnb, small["sgu_w"][l], sgu_bt, name=f"sgu_fwd{l}", after=_hook(hooks, "sgu", y_attn))
    tok = _hook(hooks, "mm_pa", y_sgu)
    a_br, b_br, merged = _merge_fwd(y_sgu, y_attn, W["proj_a"], W["proj_b"], proj, name=f"merge_fwd{l}", after=tok)
    x1, o1 = _mm(merged, W["w_out"], nt=False, out_dtype=F32, name=f"mm_out{l}", res=x, gvec=g1, tm=512)
    tok = _hook(hooks, "mm_gu", x1)
    h2, (a_g, a_u) = _norm_mm(x1, nw2, sc2, sh2, [W["wt_gate"], W["wt_up"]], name=f"mm_gu{l}", after=tok, tn_cap=1408)
    cw, cb = small["ffn_conv_w"][l], small["ffn_conv_b"][l][None, :]
    hf, a_c = _ffn_act_fwd(a_g, a_u, cw, cb, name=f"ffn_act_fwd{l}")
    x2, o2 = _mm(hf, W["w_down"], nt=False, out_dtype=F32, name=f"mm_down{l}", res=x1, gvec=g2)
    saved = dict(x=x, h=h, proj=proj, q_r=q_r, k_r=k_r, v_b=v_b, sink_rows=sink_rows, y_attn=y_attn, y_sgu=y_sgu,
                 a_br=a_br, b_br=b_br, merged=merged, x1=x1, o1=o1, h2=h2, a_g=a_g, a_u=a_u, a_c=a_c, hf=hf, o2=o2)
    return x2, saved


def _layer_bwd(l, dx, do2, dg2, mod_l, W, small, rope, sv, below=None, hooks=None, wg=None):
    rc, rs1, rs2, bias = rope
    sh1, sc1, g1, sh2, sc2, g2 = [mod_l[i * D_MODEL:(i + 1) * D_MODEL][None, :] for i in range(6)]
    nw1, nw2 = small["norm1_w"][l][None, :], small["norm2_w"][l][None, :]
    cw = small["ffn_conv_w"][l]
    lnw, lnb = small["sgu_ln_w"][l][None, :], small["sgu_ln_b"][l][None, :]
    sgu_bt = small["sgu_b"][l].T
    wg = {} if wg is None else wg
    dhf = _mm(do2, W["w_down"], nt=True, out_dtype=BF, name=f"mm_down_dx{l}", after=_hook(hooks, "mm_down_dx", do2),
              tn_cap=1408)
    wg["w_down"] = _mm_tn(sv["hf"], do2, name=f"mm_down_dw{l}")
    dac, dup, dcb = _ffn_act_bwd_a(dhf, sv["a_c"], sv["a_u"], name=f"ffn_act_bwd_a{l}")
    da, dcw = _ffn_act_bwd_b(dac, sv["a_g"], cw, name=f"ffn_act_bwd_b{l}")
    dh2 = _mm([da, dup], [W["wt_gate"], W["wt_up"]], nt=False, out_dtype=F32, name=f"mm_gu_dx{l}",
              after=_hook(hooks, "mm_gu_dx", da))
    wg["wt_gate"] = _mm_tn(da, sv["h2"], name=f"mm_gate_dw{l}")
    wg["wt_up"] = _mm_tn(dup, sv["h2"], name=f"mm_up_dw{l}")
    dx1, dnw2, dsc2, dsh2, do1, dg1 = _normmod_bwd(dh2, sv["x1"], nw2, sc2, sh2, dx, (sv["o1"], g1), name=f"normmod2_bwd{l}")
    d_a, d_b, dproj = _merge_bwd(do1, W["w_out"], sv["a_br"], sv["b_br"], sv["proj"], name=f"merge_bwd{l}",
                                 after=_hook(hooks, "merge_bwd", do1))
    wg["w_out"] = _mm_tn(sv["merged"], do1, name=f"mm_out_dw{l}")
    dysgu = _mm(d_a, W["proj_a"], nt=True, out_dtype=F32, name=f"mm_pa_dx{l}", after=_hook(hooks, "mm_pa_dx", d_a))
    dyattn = _mm(d_b, W["proj_b"], nt=True, out_dtype=BF, name=f"mm_pb_dx{l}")
    wg["proj_a"] = _mm_tn(sv["y_sgu"], d_a, name=f"mm_pa_dw{l}")
    wg["proj_b"] = _mm_tn(sv["y_attn"], d_b, name=f"mm_pb_dw{l}")
    dproj, dlnw, dlnb, dsguw, dsgubt = _sgu_bwd(dysgu, sv["proj"], lnw, lnb, small["sgu_w"][l], sgu_bt, dproj,
                                                name=f"sgu_bwd{l}")
    dq_r, dk_r, dv_b, dsk = _attn_bwd(dyattn, sv["q_r"], sv["k_r"], sv["v_b"], sv["sink_rows"], bias, name=f"attn_bwd{l}")
    dproj = _rope_bwd(dq_r, dk_r, dv_b, rc, rs1, rs2, dproj, name=f"rope_bwd{l}")
    wg["wt_in"] = _mm_tn(dproj, sv["h"], name=f"mm_in_dw{l}")
    dh = _mm(dproj, W["wt_in"], nt=False, out_dtype=F32, name=f"mm_in_dx{l}", after=_hook(hooks, "mm_in_dx", wg["wt_in"]))
    dx0, dnw1, dsc1, dsh1, *gate_below = _normmod_bwd(dh, sv["x"], nw1, sc1, sh1, dx1, below, name=f"normmod1_bwd{l}")
    dmod = jnp.concatenate([dsh1, dsc1, dg1, dsh2, dsc2, dg2], axis=1)[0]
    sg = {"norm1_w": dnw1[0], "norm2_w": dnw2[0], "attn_sinks": dsk[:, :, 0].reshape(N_Q_HEADS),
          "sgu_ln_w": dlnw[0], "sgu_ln_b": dlnb[0], "sgu_w": dsguw, "sgu_b": dsgubt.T,
          "ffn_conv_w": dcw, "ffn_conv_b": dcb[0]}
    return (dx0, *gate_below), wg, sg, dmod


SMALL = ("ada_b", "norm1_w", "attn_sinks", "sgu_ln_w", "sgu_ln_b", "sgu_w", "sgu_b", "norm2_w", "ffn_conv_b", "final_norm_w")
WEIGHT_ORDER = ("ada_w", "ada_b", "norm1_w", "w_in", "attn_sinks", "sgu_ln_w", "sgu_ln_b", "sgu_w", "sgu_b", "proj_a", "proj_b",
                "w_out", "norm2_w", "ffn_w_gate", "ffn_w_up", "ffn_conv_w", "ffn_conv_b", "ffn_w_down", "final_norm_w")


def _flat_pack(arrs, rows):
    flat = jnp.concatenate([a.reshape(-1) for a in arrs])
    return jnp.pad(flat, (0, rows * 1024 - flat.shape[0])).reshape(rows, 1024)


def _flat_unpack(buf, shapes):
    flat = buf.reshape(-1)
    out, o = [], 0
    for s in shapes:
        n = int(np.prod(s))
        out.append(flat[o:o + n].reshape(s))
        o += n
    return out


def _adam2d(w, g, m, v, *, name):
    shp = w.shape
    r2 = (int(np.prod(shp[:-1])), shp[-1]) if len(shp) > 1 else (1, shp[0])
    d, mn, vn = _adamw(w.reshape(r2), g.reshape(r2), m.reshape(r2), v.reshape(r2), name=name)
    return d.reshape(shp), mn.reshape(shp), vn.reshape(shp)


def kernel(x, c, positions, ada_w, ada_b, norm1_w, w_in, attn_sinks, sgu_ln_w, sgu_ln_b, sgu_w, sgu_b, proj_a, proj_b, w_out, norm2_w, ffn_w_gate, ffn_w_up, ffn_conv_w, ffn_conv_b, ffn_w_down, final_norm_w, loss_target, m_ada_w, m_ada_b, m_norm1_w, m_w_in, m_attn_sinks, m_sgu_ln_w, m_sgu_ln_b, m_sgu_w, m_sgu_b, m_proj_a, m_proj_b, m_w_out, m_norm2_w, m_ffn_w_gate, m_ffn_w_up, m_ffn_conv_w, m_ffn_conv_b, m_ffn_w_down, m_final_norm_w, v_ada_w, v_ada_b, v_norm1_w, v_w_in, v_attn_sinks, v_sgu_ln_w, v_sgu_ln_b, v_sgu_w, v_sgu_b, v_proj_a, v_proj_b, v_w_out, v_norm2_w, v_ffn_w_gate, v_ffn_w_up, v_ffn_conv_w, v_ffn_conv_b, v_ffn_w_down, v_final_norm_w):
    wts = dict(ada_w=ada_w, ada_b=ada_b, norm1_w=norm1_w, w_in=w_in, attn_sinks=attn_sinks, sgu_ln_w=sgu_ln_w,
               sgu_ln_b=sgu_ln_b, sgu_w=sgu_w, sgu_b=sgu_b, proj_a=proj_a, proj_b=proj_b, w_out=w_out, norm2_w=norm2_w,
               ffn_w_gate=ffn_w_gate, ffn_w_up=ffn_w_up, ffn_conv_w=ffn_conv_w, ffn_conv_b=ffn_conv_b,
               ffn_w_down=ffn_w_down, final_norm_w=final_norm_w)
    mom = dict(ada_w=m_ada_w, ada_b=m_ada_b, norm1_w=m_norm1_w, w_in=m_w_in, attn_sinks=m_attn_sinks, sgu_ln_w=m_sgu_ln_w,
               sgu_ln_b=m_sgu_ln_b, sgu_w=m_sgu_w, sgu_b=m_sgu_b, proj_a=m_proj_a, proj_b=m_proj_b, w_out=m_w_out,
               norm2_w=m_norm2_w, ffn_w_gate=m_ffn_w_gate, ffn_w_up=m_ffn_w_up, ffn_conv_w=m_ffn_conv_w,
               ffn_conv_b=m_ffn_conv_b, ffn_w_down=m_ffn_w_down, final_norm_w=m_final_norm_w)
    var = dict(ada_w=v_ada_w, ada_b=v_ada_b, norm1_w=v_norm1_w, w_in=v_w_in, attn_sinks=v_attn_sinks, sgu_ln_w=v_sgu_ln_w,
               sgu_ln_b=v_sgu_ln_b, sgu_w=v_sgu_w, sgu_b=v_sgu_b, proj_a=v_proj_a, proj_b=v_proj_b, w_out=v_w_out,
               norm2_w=v_norm2_w, ffn_w_gate=v_ffn_w_gate, ffn_w_up=v_ffn_w_up, ffn_conv_w=v_ffn_conv_w,
               ffn_conv_b=v_ffn_conv_b, ffn_w_down=v_ffn_w_down, final_norm_w=v_final_norm_w)
    me = 4 * lax.axis_index("x") + 2 * lax.axis_index("y") + lax.axis_index("c")
    ada_cols = ada_w.shape[2]

    conv_cols = ffn_conv_w.shape[2]
    first = _all_gather(jnp.concatenate([c, _flat_pack([ffn_conv_w], 7)], axis=0), name="ag_c")
    c_all = first[:, 0, :]
    conv_full = jnp.stack([first[j, 1:].reshape(-1)[:DEPTH * 3 * conv_cols].reshape(DEPTH, 3, conv_cols)
                           for j in range(N_DEV)], axis=2).reshape(DEPTH, 3, FFN_DIM)
    prod = _ada_fwd(c_all, ada_w)
    prod_all = _all_gather(prod, name="ag_mod")
    mine = lax.dynamic_index_in_dim(prod_all, me, axis=1, keepdims=False)
    mod = jnp.stack([mine[:, l * ada_cols:(l + 1) * ada_cols].reshape(-1) for l in range(DEPTH)]) + ada_b
    small = {n: wts[n] for n in SMALL}
    small["ffn_conv_w"] = conv_full

    mx, my, mc = _coords()
    cidx = jnp.reshape(mc, (1,)).astype(jnp.int32)
    chipidx = jnp.reshape(2 * mx + my, (1,)).astype(jnp.int32)
    rope = _rope_setup(positions[0])

    class Gather:
        def __init__(self, srcs, tag):
            self.tag, self.k, self.srcs = tag, len(srcs), list(srcs)
            self.lands = [lax.dynamic_update_slice(lax.empty((N_CHIPS, 2) + s.shape, s.dtype), s[None, None],
                                                   (2 * mx + my, mc, 0, 0)) for s in srcs]

        def ici_start(self, after):
            self.sems, bufs, tok = _rdma_start(self.srcs + self.lands, 3 * self.k, _plan_gather_ici,
                                               name=f"ag_{self.tag}_ici_start", after=after)
            self.srcs, self.lands = bufs[:self.k], bufs[self.k:]
            return tok

        def ici_wait_d2d_start(self, after):
            bufs = _rdma_wait(self.sems, self.srcs + self.lands, 3 * self.k, _plan_gather_ici, after,
                              name=f"ag_{self.tag}_ici_wait")
            self.sems, self.lands, tok = _rdma_start(bufs[self.k:], N_CHIPS * self.k, _plan_gather_d2d,
                                                     name=f"ag_{self.tag}_d2d_start")
            return tok

        def d2d_wait(self, after):
            lands = _rdma_wait(self.sems, self.lands, N_CHIPS * self.k, _plan_gather_d2d, after, name=f"ag_{self.tag}_d2d_wait")
            return [_from_land(land) for land in lands]

    def weights_job(names, l, tag):
        job = Gather(_shard_rows(wts, l, names), tag)
        job.weights = lambda after: _unpack_weights(job.d2d_wait(after), names)
        return job

    W1 = {}
    rest = PART_MIX + PART_FFN
    g_in0 = weights_job(PART_IN, 0, "w0_in")
    g_in0.ici_start(mod)
    g_rest0 = weights_job(rest, 0, "w0_rest")
    g_in1, g_rest1 = weights_job(PART_IN, 1, "w1_in"), weights_job(rest, 1, "w1_rest")
    g_in0.ici_wait_d2d_start(g_rest1.lands[-1])
    W0 = g_in0.weights(rope[0])

    def rest0_then_layer1(after):
        W0.update(g_rest0.weights(after))
        return g_rest1.ici_start(g_in1.ici_start(W0["w_down"]))

    x1, sv0 = _layer_fwd(0, x[0], mod[0], W0, small, rope,
                         {"mm_in": lambda after: g_rest0.ici_start(W0["wt_in"]), "sgu": g_rest0.ici_wait_d2d_start,
                          "mm_pa": rest0_then_layer1, "mm_gu": g_in1.ici_wait_d2d_start})
    g_rest1.ici_wait_d2d_start(x1)
    x2, sv1 = _layer_fwd(1, x1, mod[1], W1, small, rope,
                         {"mm_in": lambda after: W1.update(g_in1.weights(after)),
                          "mm_pa": lambda after: W1.update(g_rest1.weights(after))})
    gate2 = [mod[l][5 * D_MODEL:][None, :] for l in range(DEPTH)]
    dx2, dfw, loss_tile, do2, dg2 = _head(x2, final_norm_w[None, :], loss_target[0], (sv1["o2"], gate2[1]))
    loss = lax.psum(loss_tile[0, 0], ("x", "y", "c"))

    class Reduce:
        def __init__(self, names, tag):
            self.names, self.tag, self.rows = names, tag, _part_rows(names)

        def d2d_start(self, wg, after=None):
            self.sems, self.bufs, tok = _rdma_start([_pack_grads(wg, self.names), lax.empty((N_CHIPS, self.rows, 1024), BF)],
                                                    N_CHIPS, _plan_reduce_d2d, name=f"rs_{self.tag}_d2d_start", after=after)
            return tok

        def d2d_wait_ici_start(self, after):
            g_t, land_a = _rdma_wait(self.sems, self.bufs, N_CHIPS, _plan_reduce_d2d, after, name=f"rs_{self.tag}_d2d_wait")
            h = _sum_pair(g_t, land_a, cidx, name=f"rs_{self.tag}_sum_pair")
            self.sems, self.bufs, tok = _rdma_start([h, lax.empty((3, self.rows, 1024), BF)], 3, _plan_reduce_ici,
                                                    name=f"rs_{self.tag}_ici_start")
            return tok

        def ici_wait(self, after):
            h_t, land_b = _rdma_wait(self.sems, self.bufs, 3, _plan_reduce_ici, after, name=f"rs_{self.tag}_ici_wait")
            return _unpack_shard_grads(_sum_chips(h_t, land_b, chipidx, name=f"rs_{self.tag}_sum_chips"), self.names)

    (dx1, do2, dg2), wg1, sg1, dmod1 = _layer_bwd(1, dx2, do2, dg2, mod[1], W1, small, rope, sv1, below=(sv0["o2"], gate2[0]))
    r_all1, r_ffn0, r_mix0 = Reduce(BIG, "g1"), Reduce(PART_FFN, "g0_ffn"), Reduce(PART_IN + PART_MIX, "g0_mix")
    tok1 = r_all1.d2d_start(wg1)
    wg0, shard1 = {}, {}

    def layer1_done_then_mix0(after):
        shard1.update(r_all1.ici_wait(after))
        return r_mix0.d2d_wait_ici_start(r_mix0.d2d_start(wg0, shard1["w_in"]))

    (grad_x,), _, sg0, dmod0 = _layer_bwd(
        0, dx1, do2, dg2, mod[0], W0, small, rope, sv0, wg=wg0,
        hooks={"mm_down_dx": lambda after: tok1, "mm_gu_dx": r_all1.d2d_wait_ici_start,
               "merge_bwd": lambda after: r_ffn0.d2d_start(wg0, after), "mm_pa_dx": r_ffn0.d2d_wait_ici_start,
               "mm_in_dx": layer1_done_then_mix0})
    sg = {n: jnp.stack([sg0[n], sg1[n]]) for n in sg0}
    sg["final_norm_w"] = dfw[0]
    dmod = jnp.stack([dmod0, dmod1])
    vec_names = [n for n in SMALL if n not in ("ada_b", "sgu_w")] + ["ffn_conv_w"]
    vec_shapes = [(DEPTH, 6 * D_MODEL)] + [sg[n].shape for n in vec_names]
    vec_rows = -(-sum(int(np.prod(s)) for s in vec_shapes) // 1024 // 16) * 16
    sgu_rows = sgu_w.size // 1024
    g_small = Gather([jnp.concatenate([_flat_pack([dmod] + [sg[n] for n in vec_names], vec_rows),
                                       sg["sgu_w"].reshape(sgu_rows, 1024)], axis=0).astype(BF)], "small")
    tok = g_small.ici_start(grad_x)

    shard0 = r_ffn0.ici_wait(tok)
    shard0.update(r_mix0.ici_wait(shard0["ffn_w_down"]))
    grads, delta, new_m, new_v = {}, {}, {}, {}
    for n in BIG:
        view = (lambda a: jnp.swapaxes(a, 1, 2)) if n in COL_SHARDED else (lambda a: a)
        out = _adamw_layers(view(wts[n]), [shard0[n], shard1[n]], view(mom[n]), view(var[n]), name=f"adamw_{n}")
        grads[n], delta[n], new_m[n], new_v[n] = [view(o) for o in out]

    (sm_all,) = g_small.d2d_wait(g_small.ici_wait_d2d_start(delta["ffn_w_gate"]))
    sm_sum = _sum8(sm_all, name="sum_small")
    vec_sum = _flat_unpack(sm_sum[:vec_rows], vec_shapes)
    grads["ada_b"] = vec_sum[0]
    for n, gsum in zip(vec_names, vec_sum[1:]):
        grads[n] = gsum
    grads["sgu_w"] = sm_sum[vec_rows:].reshape(sgu_w.shape)
    grads["ffn_conv_w"] = lax.dynamic_slice_in_dim(grads["ffn_conv_w"], me * conv_cols, conv_cols, axis=2)
    dmod_all = sm_all[:, :DEPTH * 6, :].astype(F32).reshape(N_DEV, DEPTH, 6 * D_MODEL)
    dm_mine = lax.dynamic_slice_in_dim(dmod_all, me * ada_cols, ada_cols, axis=2).transpose(1, 0, 2)
    dm_mine = jnp.pad(dm_mine, ((0, 0), (0, 8), (0, 0)))
    grads["ada_w"] = _ada_bwd(jnp.pad(c_all, ((0, 8), (0, 0))), dm_mine)

    for n in WEIGHT_ORDER:
        if n not in delta:
            delta[n], new_m[n], new_v[n] = _adam2d(wts[n], grads[n], mom[n], var[n], name=f"adamw_{n}")
    return (loss, grad_x[None], *[grads[n] for n in WEIGHT_ORDER], *[delta[n] for n in WEIGHT_ORDER],
            *[new_m[n] for n in WEIGHT_ORDER], *[new_v[n] for n in WEIGHT_ORDER])
```
